```python
import jax, jax.numpy as jnp
from jax import lax
import numpy as np

D_MODEL = 1024
BATCH = 8
SEQ = 8192
DEPTH = 1

POOL_WINDOWS = (2, 4, 8, 16)
N_POOL_GROUPS = len(POOL_WINDOWS)
D_POOL = D_MODEL
POOL_GW = D_POOL // N_POOL_GROUPS
POOL_OUT_GW = D_MODEL // N_POOL_GROUPS
D_CONV = D_MODEL
CONV_K = 3
D_IN = D_POOL + 3 * D_CONV + 2 * D_MODEL
D_FF = 2816
FFN_K = 3
N_MOD = 6
EPS = 1e-6

kernel_name = "hybrid_pool_shortconv_convffn_block"


def rmsnorm(x, g):
    xf = x.astype(jnp.float32)
    y = xf * lax.rsqrt(jnp.mean(xf * xf, axis=-1, keepdims=True) + EPS)
    return (y * g.astype(jnp.float32)).astype(x.dtype)


def causal_dwconv(x, w, b):
    k = w.shape[0]
    s = x.shape[1]
    xp = jnp.pad(x, ((0, 0), (k - 1, 0), (0, 0)))
    y = b
    for i in range(k):
        y = y + w[i] * xp[:, i:i + s]
    return y


def causal_multiscale_pool(u):
    bsz, s, _ = u.shape
    ug = u.reshape(bsz, s, N_POOL_GROUPS, POOL_GW)
    cs = jnp.cumsum(ug.astype(jnp.float32), axis=1)
    cs0 = jnp.pad(cs, ((0, 0), (1, 0), (0, 0), (0, 0)))
    t1 = jnp.arange(1, s + 1, dtype=jnp.float32)
    outs = []
    for g, w in enumerate(POOL_WINDOWS):
        upper = cs0[:, 1:, g]
        lower = jnp.pad(cs0[:, :s + 1 - w, g], ((0, 0), (w - 1, 0), (0, 0)))
        cnt = jnp.minimum(t1, float(w))[None, :, None]
        outs.append((upper - lower) / cnt)
    pooled = jnp.stack(outs, axis=2).astype(u.dtype)
    return pooled - ug


def _fwd_setup_inputs(seed: int = 0) -> dict:
    key = jax.random.key(seed)
    ks = jax.random.split(key, 20)
    L, D = DEPTH, D_MODEL
    nrm = lambda k, shp, fan: jax.random.normal(k, shp, jnp.float32) * (fan ** -0.5)
    gain = lambda k, n: 1.0 + 0.05 * jax.random.normal(k, (L, n), jnp.float32)
    return {
        "x": jax.random.normal(ks[0], (BATCH, SEQ, D), jnp.float32),
        "c": jax.random.normal(ks[1], (BATCH, D), jnp.float32),
        "g_pre_mix": gain(ks[2], D),
        "g_post_mix": gain(ks[3], D),
        "g_pre_ffn": gain(ks[4], D),
        "g_post_ffn": gain(ks[5], D),
        "w_ada": 0.5 * nrm(ks[6], (L, D, N_MOD * D), D),
        "b_ada": 0.01 * jax.random.normal(ks[7], (L, N_MOD * D), jnp.float32),
        "w_in": nrm(ks[8], (L, D, D_IN), D),
        "w_pool": nrm(ks[9], (L, N_POOL_GROUPS, POOL_GW, POOL_OUT_GW), POOL_GW),
        "pool_scale": gain(ks[10], D),
        "conv_w": nrm(ks[11], (L, CONV_K, D_CONV), CONV_K),
        "conv_b": 0.01 * jax.random.normal(ks[12], (L, D_CONV), jnp.float32),
        "w_bout": nrm(ks[13], (L, D_CONV, D), D_CONV),
        "w_o": nrm(ks[14], (L, D, D), D),
        "w_up": nrm(ks[15], (L, D, 2 * D_FF), D),
        "ffn_conv_w": nrm(ks[16], (L, FFN_K, 2 * D_FF), FFN_K),
        "ffn_conv_b": 0.01 * jax.random.normal(ks[17], (L, 2 * D_FF), jnp.float32),
        "w_down": nrm(ks[18], (L, D_FF, D), D_FF),
    }


def _fwd_reference(x, c, g_pre_mix, g_post_mix, g_pre_ffn, g_post_ffn, w_ada, b_ada, w_in, w_pool,
              pool_scale, conv_w, conv_b, w_bout, w_o, w_up, ffn_conv_w, ffn_conv_b, w_down):
    bsz, s, d = x.shape
    for l in range(DEPTH):
        mod = c @ w_ada[l] + b_ada[l]
        sh1, sc1, gt1, sh2, sc2, gt2 = [m[:, None, :] for m in jnp.split(mod, N_MOD, axis=-1)]

        h = rmsnorm(x, g_pre_mix[l]) * (1.0 + sc1) + sh1
        proj = h @ w_in[l]
        u_pool, u_x, u_b, u_c, z_a, z_b = jnp.split(
            proj, np.cumsum([D_POOL, D_CONV, D_CONV, D_CONV, D_MODEL])[:].tolist(), axis=-1)

        pg = causal_multiscale_pool(u_pool)
        y_a = jnp.einsum('bsgc,gcd->bsgd', pg, w_pool[l]).reshape(bsz, s, d) * pool_scale[l]

        y_b = (u_b * causal_dwconv(u_c * u_x, conv_w[l], conv_b[l])) @ w_bout[l]

        merged = jax.nn.sigmoid(z_a) * y_a + jax.nn.sigmoid(z_b) * y_b
        x = x + gt1 * rmsnorm(merged @ w_o[l], g_post_mix[l])

        h = rmsnorm(x, g_pre_ffn[l]) * (1.0 + sc2) + sh2
        up = causal_dwconv(h @ w_up[l], ffn_conv_w[l], ffn_conv_b[l])
        gate, val = jnp.split(up, 2, axis=-1)
        ff = (jax.nn.gelu(gate, approximate=True) * val) @ w_down[l]
        x = x + gt2 * rmsnorm(ff, g_post_ffn[l])
    return x


import jax as _jax
import jax.numpy as _jnp

TWIN_FORMAT = 'train_step'
FWD_PARAMS = ['x', 'c', 'g_pre_mix', 'g_post_mix', 'g_pre_ffn', 'g_post_ffn', 'w_ada', 'b_ada', 'w_in', 'w_pool', 'pool_scale', 'conv_w', 'conv_b', 'w_bout', 'w_o', 'w_up', 'ffn_conv_w', 'ffn_conv_b', 'w_down']
TWIN_WEIGHTS = ['g_pre_mix', 'g_post_mix', 'g_pre_ffn', 'g_post_ffn', 'w_ada', 'b_ada', 'w_in', 'w_pool', 'pool_scale', 'conv_w', 'conv_b', 'w_bout', 'w_o', 'w_up', 'ffn_conv_w', 'ffn_conv_b', 'w_down']
TWIN_DIFF_INPUT = 'x'
TWIN_INPUTS = ['x', 'c', 'g_pre_mix', 'g_post_mix', 'g_pre_ffn', 'g_post_ffn', 'w_ada', 'b_ada', 'w_in', 'w_pool', 'pool_scale', 'conv_w', 'conv_b', 'w_bout', 'w_o', 'w_up', 'ffn_conv_w', 'ffn_conv_b', 'w_down', 'loss_target', 'm_g_pre_mix', 'm_g_post_mix', 'm_g_pre_ffn', 'm_g_post_ffn', 'm_w_ada', 'm_b_ada', 'm_w_in', 'm_w_pool', 'm_pool_scale', 'm_conv_w', 'm_conv_b', 'm_w_bout', 'm_w_o', 'm_w_up', 'm_ffn_conv_w', 'm_ffn_conv_b', 'm_w_down', 'v_g_pre_mix', 'v_g_post_mix', 'v_g_pre_ffn', 'v_g_post_ffn', 'v_w_ada', 'v_b_ada', 'v_w_in', 'v_w_pool', 'v_pool_scale', 'v_conv_w', 'v_conv_b', 'v_w_bout', 'v_w_o', 'v_w_up', 'v_ffn_conv_w', 'v_ffn_conv_b', 'v_w_down']
TWIN_OUTPUTS = ['loss', 'grad_x', 'grad_g_pre_mix', 'grad_g_post_mix', 'grad_g_pre_ffn', 'grad_g_post_ffn', 'grad_w_ada', 'grad_b_ada', 'grad_w_in', 'grad_w_pool', 'grad_pool_scale', 'grad_conv_w', 'grad_conv_b', 'grad_w_bout', 'grad_w_o', 'grad_w_up', 'grad_ffn_conv_w', 'grad_ffn_conv_b', 'grad_w_down', 'delta_g_pre_mix', 'delta_g_post_mix', 'delta_g_pre_ffn', 'delta_g_post_ffn', 'delta_w_ada', 'delta_b_ada', 'delta_w_in', 'delta_w_pool', 'delta_pool_scale', 'delta_conv_w', 'delta_conv_b', 'delta_w_bout', 'delta_w_o', 'delta_w_up', 'delta_ffn_conv_w', 'delta_ffn_conv_b', 'delta_w_down', 'new_m_g_pre_mix', 'new_m_g_post_mix', 'new_m_g_pre_ffn', 'new_m_g_post_ffn', 'new_m_w_ada', 'new_m_b_ada', 'new_m_w_in', 'new_m_w_pool', 'new_m_pool_scale', 'new_m_conv_w', 'new_m_conv_b', 'new_m_w_bout', 'new_m_w_o', 'new_m_w_up', 'new_m_ffn_conv_w', 'new_m_ffn_conv_b', 'new_m_w_down', 'new_v_g_pre_mix', 'new_v_g_post_mix', 'new_v_g_pre_ffn', 'new_v_g_post_ffn', 'new_v_w_ada', 'new_v_b_ada', 'new_v_w_in', 'new_v_w_pool', 'new_v_pool_scale', 'new_v_conv_w', 'new_v_conv_b', 'new_v_w_bout', 'new_v_w_o', 'new_v_w_up', 'new_v_ffn_conv_w', 'new_v_ffn_conv_b', 'new_v_w_down']
TWIN_LEAF_KINDS = {'loss': 'loss', 'grad_x': 'grad_x', 'grad_g_pre_mix': 'grad_w', 'grad_g_post_mix': 'grad_w', 'grad_g_pre_ffn': 'grad_w', 'grad_g_post_ffn': 'grad_w', 'grad_w_ada': 'grad_w', 'grad_b_ada': 'grad_w', 'grad_w_in': 'grad_w', 'grad_w_pool': 'grad_w', 'grad_pool_scale': 'grad_w', 'grad_conv_w': 'grad_w', 'grad_conv_b': 'grad_w', 'grad_w_bout': 'grad_w', 'grad_w_o': 'grad_w', 'grad_w_up': 'grad_w', 'grad_ffn_conv_w': 'grad_w', 'grad_ffn_conv_b': 'grad_w', 'grad_w_down': 'grad_w', 'delta_g_pre_mix': 'delta_w', 'delta_g_post_mix': 'delta_w', 'delta_g_pre_ffn': 'delta_w', 'delta_g_post_ffn': 'delta_w', 'delta_w_ada': 'delta_w', 'delta_b_ada': 'delta_w', 'delta_w_in': 'delta_w', 'delta_w_pool': 'delta_w', 'delta_pool_scale': 'delta_w', 'delta_conv_w': 'delta_w', 'delta_conv_b': 'delta_w', 'delta_w_bout': 'delta_w', 'delta_w_o': 'delta_w', 'delta_w_up': 'delta_w', 'delta_ffn_conv_w': 'delta_w', 'delta_ffn_conv_b': 'delta_w', 'delta_w_down': 'delta_w', 'new_m_g_pre_mix': 'new_m', 'new_m_g_post_mix': 'new_m', 'new_m_g_pre_ffn': 'new_m', 'new_m_g_post_ffn': 'new_m', 'new_m_w_ada': 'new_m', 'new_m_b_ada': 'new_m', 'new_m_w_in': 'new_m', 'new_m_w_pool': 'new_m', 'new_m_pool_scale': 'new_m', 'new_m_conv_w': 'new_m', 'new_m_conv_b': 'new_m', 'new_m_w_bout': 'new_m', 'new_m_w_o': 'new_m', 'new_m_w_up': 'new_m', 'new_m_ffn_conv_w': 'new_m', 'new_m_ffn_conv_b': 'new_m', 'new_m_w_down': 'new_m', 'new_v_g_pre_mix': 'new_v', 'new_v_g_post_mix': 'new_v', 'new_v_g_pre_ffn': 'new_v', 'new_v_g_post_ffn': 'new_v', 'new_v_w_ada': 'new_v', 'new_v_b_ada': 'new_v', 'new_v_w_in': 'new_v', 'new_v_w_pool': 'new_v', 'new_v_pool_scale': 'new_v', 'new_v_conv_w': 'new_v', 'new_v_conv_b': 'new_v', 'new_v_w_bout': 'new_v', 'new_v_w_o': 'new_v', 'new_v_w_up': 'new_v', 'new_v_ffn_conv_w': 'new_v', 'new_v_ffn_conv_b': 'new_v', 'new_v_w_down': 'new_v'}


def _forward(args):
    return _fwd_reference(*[args[k] for k in FWD_PARAMS])


def _output_shape():
    def fwd():
        inp = _fwd_setup_inputs(0)
        return _fwd_reference(*[inp[k] for k in FWD_PARAMS])
    out = _jax.eval_shape(fwd)
    return out.shape, out.dtype

N_MICROBATCH = 1
ADAM_LR = 0.001
ADAM_B1 = 0.9
ADAM_B2 = 0.999
ADAM_EPS = 1e-08
ADAM_WD = 0.01
ADAM_STEP = 10
PER_EXAMPLE_BATCH_AXIS = {'x': 0, 'c': 0, 'loss_target': 0}
SHARED_INPUTS = []
_WEIGHT_DTYPES = {'g_pre_mix': _jnp.float32, 'g_post_mix': _jnp.float32, 'g_pre_ffn': _jnp.float32, 'g_post_ffn': _jnp.float32, 'w_ada': _jnp.float32, 'b_ada': _jnp.float32, 'w_in': _jnp.float32, 'w_pool': _jnp.float32, 'pool_scale': _jnp.float32, 'conv_w': _jnp.float32, 'conv_b': _jnp.float32, 'w_bout': _jnp.float32, 'w_o': _jnp.float32, 'w_up': _jnp.float32, 'ffn_conv_w': _jnp.float32, 'ffn_conv_b': _jnp.float32, 'w_down': _jnp.float32}
MOMENT_SCALE = {'g_pre_mix': 3.807884e-01, 'g_post_mix': 1.851877e+01, 'g_pre_ffn': 2.917872e-01, 'g_post_ffn': 1.828450e+01, 'w_ada': 6.548000e+00, 'b_ada': 6.308380e+00, 'w_in': 1.794142e-01, 'w_pool': 1.479271e-01, 'pool_scale': 1.486700e-01, 'conv_w': 2.419184e-01, 'conv_b': 2.920871e-01, 'w_bout': 2.653173e-01, 'w_o': 3.451690e-01, 'w_up': 2.023549e-01, 'ffn_conv_w': 2.419352e-01, 'ffn_conv_b': 3.618130e-01, 'w_down': 4.505576e-01}


def _to_microbatches(a, axis):
    t = _jnp.moveaxis(a, axis, 0)
    t = t.reshape((N_MICROBATCH, t.shape[0] // N_MICROBATCH) + t.shape[1:])
    return _jnp.moveaxis(t, 1, axis + 1)


def setup_inputs(seed: int = 0) -> dict:
    inp = _fwd_setup_inputs(seed)
    key = _jax.random.fold_in(_jax.random.key(seed), 7919)
    shape, _ = _output_shape()
    out = dict(inp)
    out["loss_target"] = _jax.random.normal(_jax.random.fold_in(key, 0), shape, _jnp.float32)
    for i, name in enumerate(TWIN_WEIGHTS):
        w = inp[name].astype(_jnp.float32)
        if MOMENT_SCALE is None:
            s = _jnp.sqrt(_jnp.mean(_jnp.square(w)) + 1e-30)
        else:
            s = MOMENT_SCALE[name]
        km, kv = _jax.random.split(_jax.random.fold_in(key, i + 1))
        out[name] = w
        out["m_" + name] = s * _jax.random.normal(km, w.shape, _jnp.float32)
        out["v_" + name] = (s * s) * _jax.random.uniform(kv, w.shape, _jnp.float32, 0.5, 1.5)
    if N_MICROBATCH > 1:
        for name, axis in PER_EXAMPLE_BATCH_AXIS.items():
            out[name] = _to_microbatches(out[name], axis)
    return {'x': out['x'], 'c': out['c'], 'g_pre_mix': out['g_pre_mix'], 'g_post_mix': out['g_post_mix'], 'g_pre_ffn': out['g_pre_ffn'], 'g_post_ffn': out['g_post_ffn'], 'w_ada': out['w_ada'], 'b_ada': out['b_ada'], 'w_in': out['w_in'], 'w_pool': out['w_pool'], 'pool_scale': out['pool_scale'], 'conv_w': out['conv_w'], 'conv_b': out['conv_b'], 'w_bout': out['w_bout'], 'w_o': out['w_o'], 'w_up': out['w_up'], 'ffn_conv_w': out['ffn_conv_w'], 'ffn_conv_b': out['ffn_conv_b'], 'w_down': out['w_down'], 'loss_target': out['loss_target'], 'm_g_pre_mix': out['m_g_pre_mix'], 'm_g_post_mix': out['m_g_post_mix'], 'm_g_pre_ffn': out['m_g_pre_ffn'], 'm_g_post_ffn': out['m_g_post_ffn'], 'm_w_ada': out['m_w_ada'], 'm_b_ada': out['m_b_ada'], 'm_w_in': out['m_w_in'], 'm_w_pool': out['m_w_pool'], 'm_pool_scale': out['m_pool_scale'], 'm_conv_w': out['m_conv_w'], 'm_conv_b': out['m_conv_b'], 'm_w_bout': out['m_w_bout'], 'm_w_o': out['m_w_o'], 'm_w_up': out['m_w_up'], 'm_ffn_conv_w': out['m_ffn_conv_w'], 'm_ffn_conv_b': out['m_ffn_conv_b'], 'm_w_down': out['m_w_down'], 'v_g_pre_mix': out['v_g_pre_mix'], 'v_g_post_mix': out['v_g_post_mix'], 'v_g_pre_ffn': out['v_g_pre_ffn'], 'v_g_post_ffn': out['v_g_post_ffn'], 'v_w_ada': out['v_w_ada'], 'v_b_ada': out['v_b_ada'], 'v_w_in': out['v_w_in'], 'v_w_pool': out['v_w_pool'], 'v_pool_scale': out['v_pool_scale'], 'v_conv_w': out['v_conv_w'], 'v_conv_b': out['v_conv_b'], 'v_w_bout': out['v_w_bout'], 'v_w_o': out['v_w_o'], 'v_w_up': out['v_w_up'], 'v_ffn_conv_w': out['v_ffn_conv_w'], 'v_ffn_conv_b': out['v_ffn_conv_b'], 'v_w_down': out['v_w_down']}


def _loss(weights, diff, rest, loss_target):
    with _jax.named_scope("forward"):
        args = {**rest, TWIN_DIFF_INPUT: diff, **{k: w.astype(_WEIGHT_DTYPES[k]) for k, w in weights.items()}}
        y = _forward(args)
    with _jax.named_scope("loss_head"):
        err = _jnp.square(y.astype(_jnp.float32) - loss_target)
        return 0.5 * _jnp.sum(_jnp.mean(err, axis=-1)) if err.ndim else 0.5 * err


def _adamw(w, g, m, v):
    m = ADAM_B1 * m + (1.0 - ADAM_B1) * g
    v = ADAM_B2 * v + (1.0 - ADAM_B2) * _jnp.square(g)
    m_hat = m / (1.0 - ADAM_B1 ** ADAM_STEP)
    v_hat = v / (1.0 - ADAM_B2 ** ADAM_STEP)
    delta = -ADAM_LR * (m_hat / (_jnp.sqrt(v_hat) + ADAM_EPS) + ADAM_WD * w)
    return delta, m, v


def reference(x, c, g_pre_mix, g_post_mix, g_pre_ffn, g_post_ffn, w_ada, b_ada, w_in, w_pool, pool_scale, conv_w, conv_b, w_bout, w_o, w_up, ffn_conv_w, ffn_conv_b, w_down, loss_target, m_g_pre_mix, m_g_post_mix, m_g_pre_ffn, m_g_post_ffn, m_w_ada, m_b_ada, m_w_in, m_w_pool, m_pool_scale, m_conv_w, m_conv_b, m_w_bout, m_w_o, m_w_up, m_ffn_conv_w, m_ffn_conv_b, m_w_down, v_g_pre_mix, v_g_post_mix, v_g_pre_ffn, v_g_post_ffn, v_w_ada, v_b_ada, v_w_in, v_w_pool, v_pool_scale, v_conv_w, v_conv_b, v_w_bout, v_w_o, v_w_up, v_ffn_conv_w, v_ffn_conv_b, v_w_down):
    given = dict(x=x, c=c, g_pre_mix=g_pre_mix, g_post_mix=g_post_mix, g_pre_ffn=g_pre_ffn, g_post_ffn=g_post_ffn, w_ada=w_ada, b_ada=b_ada, w_in=w_in, w_pool=w_pool, pool_scale=pool_scale, conv_w=conv_w, conv_b=conv_b, w_bout=w_bout, w_o=w_o, w_up=w_up, ffn_conv_w=ffn_conv_w, ffn_conv_b=ffn_conv_b, w_down=w_down, loss_target=loss_target, m_g_pre_mix=m_g_pre_mix, m_g_post_mix=m_g_post_mix, m_g_pre_ffn=m_g_pre_ffn, m_g_post_ffn=m_g_post_ffn, m_w_ada=m_w_ada, m_b_ada=m_b_ada, m_w_in=m_w_in, m_w_pool=m_w_pool, m_pool_scale=m_pool_scale, m_conv_w=m_conv_w, m_conv_b=m_conv_b, m_w_bout=m_w_bout, m_w_o=m_w_o, m_w_up=m_w_up, m_ffn_conv_w=m_ffn_conv_w, m_ffn_conv_b=m_ffn_conv_b, m_w_down=m_w_down, v_g_pre_mix=v_g_pre_mix, v_g_post_mix=v_g_post_mix, v_g_pre_ffn=v_g_pre_ffn, v_g_post_ffn=v_g_post_ffn, v_w_ada=v_w_ada, v_b_ada=v_b_ada, v_w_in=v_w_in, v_w_pool=v_w_pool, v_pool_scale=v_pool_scale, v_conv_w=v_conv_w, v_conv_b=v_conv_b, v_w_bout=v_w_bout, v_w_o=v_w_o, v_w_up=v_w_up, v_ffn_conv_w=v_ffn_conv_w, v_ffn_conv_b=v_ffn_conv_b, v_w_down=v_w_down)
    weights = {n: given[n] for n in TWIN_WEIGHTS}
    shared = {n: given[n] for n in SHARED_INPUTS}
    per_example = {n: given[n] for n in ['x', 'c']}
    grad_fn = _jax.value_and_grad(_loss, argnums=(0, 1))

    def one_microbatch(ex, loss_target):
        ex = dict(ex)
        diff = ex.pop(TWIN_DIFF_INPUT)
        return grad_fn(weights, diff, {**shared, **ex}, loss_target)

    if N_MICROBATCH == 1:
        loss, (grad_w, grad_x) = one_microbatch(per_example, given["loss_target"])
    else:
        def body(carry, xs):
            loss_sum, grad_sum = carry
            l_k, (gw_k, gx_k) = one_microbatch(xs[0], xs[1])
            with _jax.named_scope("update"):
                return (loss_sum + l_k, _jax.tree.map(_jnp.add, grad_sum, gw_k)), gx_k

        init = (_jnp.zeros((), _jnp.float32), _jax.tree.map(_jnp.zeros_like, weights))
        (loss, grad_w), grad_x = _jax.lax.scan(body, init, (per_example, given["loss_target"]))
    with _jax.named_scope("update"):
        delta_w, new_m, new_v = {}, {}, {}
        for n in TWIN_WEIGHTS:
            delta_w[n], new_m[n], new_v[n] = _adamw(weights[n], grad_w[n], given["m_" + n], given["v_" + n])
    return (loss, grad_x, *[grad_w[n] for n in TWIN_WEIGHTS], *[delta_w[n] for n in TWIN_WEIGHTS],
            *[new_m[n] for n in TWIN_WEIGHTS], *[new_v[n] for n in TWIN_WEIGHTS])
```

```python
import jax
import jax.numpy as jnp
from jax import lax
from jax.experimental import pallas as pl
from jax.experimental.pallas import tpu as pltpu

F32 = jnp.float32
BF16 = jnp.bfloat16

D = 1024
DIN = 6 * D
F = 2816
F2 = 2 * F
NG = 4
GW = D // NG
POOL_HALO = 16
CONV_HALO = 8
BF16_ROWS = 16
EPS = 1e-6
NCHIP = 4
NDEV = 8

ADAM_LR = 0.001
ADAM_B1 = 0.9
ADAM_B2 = 0.999
ADAM_EPS = 1e-08
ADAM_WD = 0.01
ADAM_STEP = 10

VMEM_LIMIT = 60 * 1024 * 1024

(V_SH1, V_SC1, V_GT1, V_SH2, V_SC2, V_GT2, V_GPRE1, V_GPOST1, V_GPRE2, V_GPOST2,
 V_PSCALE, V_CB, V_CW0, V_CW1, V_CW2) = range(15)
VD_ROWS = 16
FV_W0, FV_W1, FV_W2, FV_B = range(4)
FV_ROWS = 8

MESH = pl.DeviceIdType.MESH


def _params(sem=None, vmem=VMEM_LIMIT):
    return pltpu.CompilerParams(dimension_semantics=sem, vmem_limit_bytes=vmem)


def _row(ref, r):
    return ref[r:r + 1, :]


def _load_once(pairs, sem):
    @pl.when(pl.program_id(0) == 0)
    def _():
        copies = [pltpu.make_async_copy(src, dst, sem.at[n]) for n, (src, dst) in enumerate(pairs)]
        for cp in copies:
            cp.start()
        for cp in copies:
            cp.wait()


def _dot(a, b):
    return jnp.dot(a, b, preferred_element_type=F32)


def _dot_nt(a, b):
    return lax.dot_general(a, b, (((1,), (1,)), ((), ())), preferred_element_type=F32)


def _shift_down(ext, s):
    return pltpu.roll(ext, s, 0)


def _shift_up(ext, s):
    return pltpu.roll(ext, ext.shape[0] - s, 0)


def _causal_conv(x, halo, w0, w1, w2, b):
    ext = jnp.concatenate([halo, x], axis=0)
    x1 = _shift_down(ext, 1)[CONV_HALO:]
    x2 = _shift_down(ext, 2)[CONV_HALO:]
    return b + w2 * x + w1 * x1 + w0 * x2, x1, x2


def _causal_conv_bwd(dy, tail, w0, w1, w2):
    n = dy.shape[0]
    ext = jnp.concatenate([dy, tail], axis=0)
    return w2 * dy + w1 * _shift_up(ext, 1)[:n] + w0 * _shift_up(ext, 2)[:n]


def _pool_counts(t0, n, g):
    t1 = (t0 + 1 + lax.broadcasted_iota(jnp.int32, (n, 1), 0)).astype(F32)
    return jnp.minimum(t1, float(2 << g))


def _rms(x):
    return lax.rsqrt(jnp.mean(x * x, axis=-1, keepdims=True) + EPS)


def _rms_bwd(dn, n, r):
    return r * (dn - n * jnp.mean(dn * n, axis=-1, keepdims=True))


def _colsum(x):
    return jnp.sum(x, axis=0, keepdims=True)


def _gelu_and_grad(x):
    k = 0.7978845608028654
    inner = k * (x + 0.044715 * (x * x * x))
    th = jnp.tanh(inner)
    gelu = 0.5 * x * (1.0 + th)
    dgelu = 0.5 * (1.0 + th) + 0.5 * x * (1.0 - th * th) * (k * (1.0 + 3.0 * 0.044715 * (x * x)))
    return gelu, dgelu


def _fwd_proj(x, vec_d, w_in, ts):
    s = x.shape[0]
    cw = DIN // NCHIP

    def body(x_ref, v_ref, w_hbm, proj_ref, h1_ref, w_vmem, sem):
        _load_once([(w_hbm, w_vmem)], sem)
        xv = x_ref[...]
        n1 = xv * _rms(xv)
        h = n1 * (_row(v_ref, V_GPRE1) * (1.0 + _row(v_ref, V_SC1))) + _row(v_ref, V_SH1)
        hb = h.astype(BF16)
        h1_ref[...] = hb
        for k in range(NCHIP):
            cols = slice(k * cw, (k + 1) * cw)
            proj_ref[:, cols] = _dot(hb, w_vmem[:, cols]).astype(BF16)

    return pl.pallas_call(
        body, name="fwd_proj", grid=(s // ts,),
        in_specs=[pl.BlockSpec((ts, D), lambda i: (i, 0)),
                  pl.BlockSpec((VD_ROWS, D), lambda i: (0, 0)),
                  pl.BlockSpec(memory_space=pl.ANY)],
        out_specs=[pl.BlockSpec((ts, DIN), lambda i: (i, 0)),
                   pl.BlockSpec((ts, D), lambda i: (i, 0))],
        out_shape=[jax.ShapeDtypeStruct((s, DIN), BF16), jax.ShapeDtypeStruct((s, D), BF16)],
        scratch_shapes=[pltpu.VMEM((D, DIN), BF16), pltpu.SemaphoreType.DMA((1,))],
        compiler_params=_params(("arbitrary",)),
    )(x, vec_d, w_in)


def _fwd_mix(proj, x, vec_d, w_pool, w_bout, w_o, ts):
    s = x.shape[0]

    def body(p_ref, x_ref, v_ref, wp_hbm, wb_hbm, wo_hbm,
             x1_ref, o_ref, pg_ref, q_ref, mg_ref, ya_ref, yb_ref,
             wp, wb, wo, carry_p, carry_v, sem):
        i = pl.program_id(0)
        _load_once([(wp_hbm, wp), (wb_hbm, wb), (wo_hbm, wo)], sem)

        @pl.when(i == 0)
        def _():
            carry_p[...] = jnp.zeros_like(carry_p)
            carry_v[...] = jnp.zeros_like(carry_v)

        t0 = i * ts
        u_pool = p_ref[:, 0:D].astype(F32)
        ext = jnp.concatenate([carry_p[...], u_pool], axis=0)
        carry_p[...] = u_pool[ts - POOL_HALO:, :]
        for g in range(NG):
            cols = slice(g * GW, (g + 1) * GW)
            e = ext[:, cols]
            for l in range(g + 1):
                e = e + _shift_down(e, 1 << l)
            pg = e[POOL_HALO:] / _pool_counts(t0, ts, g) - u_pool[:, cols]
            pgb = pg.astype(BF16)
            pg_ref[:, cols] = pgb
            ya_ref[:, cols] = _dot(pgb, wp[g]).astype(BF16)

        u_x = p_ref[:, D:2 * D].astype(F32)
        u_c = p_ref[:, 3 * D:4 * D].astype(F32)
        v = u_c * u_x
        cv, _, _ = _causal_conv(v, carry_v[...], _row(v_ref, V_CW0), _row(v_ref, V_CW1),
                                _row(v_ref, V_CW2), _row(v_ref, V_CB))
        carry_v[...] = v[ts - CONV_HALO:, :]
        q = (p_ref[:, 2 * D:3 * D].astype(F32) * cv).astype(BF16)
        q_ref[...] = q
        y_b = _dot(q, wb[...])
        yb_ref[...] = y_b.astype(BF16)

        y_a = ya_ref[...].astype(F32) * _row(v_ref, V_PSCALE)
        merged = (jax.nn.sigmoid(p_ref[:, 4 * D:5 * D].astype(F32)) * y_a
                  + jax.nn.sigmoid(p_ref[:, 5 * D:6 * D].astype(F32)) * y_b).astype(BF16)
        mg_ref[...] = merged
        o = _dot(merged, wo[...])
        o_ref[...] = o
        x1_ref[...] = x_ref[...] + _row(v_ref, V_GT1) * ((o * _rms(o)) * _row(v_ref, V_GPOST1))

    tile = lambda w: pl.BlockSpec((ts, w), lambda i: (i, 0))
    hbm = pl.BlockSpec(memory_space=pl.ANY)
    return pl.pallas_call(
        body, name="fwd_mix", grid=(s // ts,),
        in_specs=[tile(DIN), tile(D), pl.BlockSpec((VD_ROWS, D), lambda i: (0, 0)), hbm, hbm, hbm],
        out_specs=[tile(D)] * 7,
        out_shape=[jax.ShapeDtypeStruct((s, D), F32), jax.ShapeDtypeStruct((s, D), F32)]
        + [jax.ShapeDtypeStruct((s, D), BF16)] * 5,
        scratch_shapes=[pltpu.VMEM((NG, GW, GW), BF16), pltpu.VMEM((D, D), BF16), pltpu.VMEM((D, D), BF16),
                        pltpu.VMEM((POOL_HALO, D), F32), pltpu.VMEM((CONV_HALO, D), F32),
                        pltpu.SemaphoreType.DMA((3,))],
        compiler_params=_params(("arbitrary",)),
    )(proj, x, vec_d, w_pool, w_bout, w_o)


def _fwd_ffn(x1, tgt, vec_d, vec_f, w_up, w_down, ts):
    s = x1.shape[0]
    hw = F // 2

    def body(x1_ref, t_ref, v_ref, f_ref, wu_hbm, wd_hbm,
             up_ref, a_ref, h2_ref, dx2_ref, dff_ref, vo_ref, loss_ref,
             wu, wd, carry, sem):
        i = pl.program_id(0)
        _load_once([(wu_hbm, wu), (wd_hbm, wd)], sem)

        @pl.when(i == 0)
        def _():
            carry[...] = jnp.zeros_like(carry)
            vo_ref[...] = jnp.zeros_like(vo_ref)
            loss_ref[...] = jnp.zeros_like(loss_ref)

        x1v = x1_ref[...]
        n3 = x1v * _rms(x1v)
        h2 = (n3 * (_row(v_ref, V_GPRE2) * (1.0 + _row(v_ref, V_SC2))) + _row(v_ref, V_SH2)).astype(BF16)
        h2_ref[...] = h2

        ff = jnp.zeros((ts, D), F32)
        for p in range(2):
            up = []
            for cols in (slice(p * hw, (p + 1) * hw), slice(F + p * hw, F + (p + 1) * hw)):
                u0 = _dot(h2, wu[:, cols])
                up_ref[:, cols] = u0.astype(BF16)
                y, _, _ = _causal_conv(u0, carry[:, cols], f_ref[FV_W0:FV_W0 + 1, cols], f_ref[FV_W1:FV_W1 + 1, cols],
                                       f_ref[FV_W2:FV_W2 + 1, cols], f_ref[FV_B:FV_B + 1, cols])
                carry[:, cols] = u0[ts - CONV_HALO:, :]
                up.append(y)
            gelu, _ = _gelu_and_grad(up[0])
            a = (gelu * up[1]).astype(BF16)
            a_ref[:, p * hw:(p + 1) * hw] = a
            ff = ff + _dot(a, wd[p * hw:(p + 1) * hw, :])

        r4 = _rms(ff)
        n4 = ff * r4
        gt2 = _row(v_ref, V_GT2)
        gpost = _row(v_ref, V_GPOST2)
        y4 = n4 * gpost
        diff = (x1v + gt2 * y4) - t_ref[...]
        loss_ref[...] += jnp.full(loss_ref.shape, 0.5 / D * jnp.sum(diff * diff), F32)
        dx2 = diff * (1.0 / D)
        dx2_ref[...] = dx2
        dy4 = dx2 * gt2
        vo_ref[0:1, :] += _colsum(dx2 * y4)
        vo_ref[1:2, :] += _colsum(dy4 * n4)
        dff_ref[...] = _rms_bwd(dy4 * gpost, n4, r4).astype(BF16)

    tile = lambda w: pl.BlockSpec((ts, w), lambda i: (i, 0))
    full = lambda r, w: pl.BlockSpec((r, w), lambda i: (0, 0))
    hbm = pl.BlockSpec(memory_space=pl.ANY)
    return pl.pallas_call(
        body, name="fwd_ffn", grid=(s // ts,),
        in_specs=[tile(D), tile(D), full(VD_ROWS, D), full(FV_ROWS, F2), hbm, hbm],
        out_specs=[tile(F2), tile(F), tile(D), tile(D), tile(D), full(8, D), full(8, 128)],
        out_shape=[jax.ShapeDtypeStruct((s, F2), BF16), jax.ShapeDtypeStruct((s, F), BF16),
                   jax.ShapeDtypeStruct((s, D), BF16), jax.ShapeDtypeStruct((s, D), F32),
                   jax.ShapeDtypeStruct((s, D), BF16), jax.ShapeDtypeStruct((8, D), F32),
                   jax.ShapeDtypeStruct((8, 128), F32)],
        scratch_shapes=[pltpu.VMEM((D, F2), BF16), pltpu.VMEM((F, D), BF16), pltpu.VMEM((CONV_HALO, F2), F32),
                        pltpu.SemaphoreType.DMA((2,))],
        compiler_params=_params(("arbitrary",)),
    )(x1, tgt, vec_d, vec_f, w_up, w_down)


def _bwd_ffn(dff, dx2, x1, up0, vec_d, vec_f, w_up, w_down, ts):
    s = x1.shape[0]
    nt = s // ts
    hw = F // 2
    hb = ts // BF16_ROWS

    def body(dff_ref, dx2_ref, x1_ref, up_ref, halo_ref, v_ref, f_ref, wu_hbm, wd_hbm,
             dx1_ref, dup_ref, vo_ref, fo_ref, wu, wd, carry, sem):
        i = pl.program_id(0)
        _load_once([(wu_hbm, wu), (wd_hbm, wd)], sem)

        @pl.when(i == 0)
        def _():
            carry[...] = jnp.zeros_like(carry)
            vo_ref[...] = jnp.zeros_like(vo_ref)
            fo_ref[...] = jnp.zeros_like(fo_ref)

        first = (i == nt - 1)
        dffb = dff_ref[...]
        dh2 = jnp.zeros((ts, D), F32)
        for p in range(2):
            slabs = (slice(p * hw, (p + 1) * hw), slice(F + p * hw, F + (p + 1) * hw))
            ups, taps = [], []
            for cols in slabs:
                u0 = up_ref[:, cols].astype(F32)
                halo = jnp.where(first, 0.0, halo_ref[BF16_ROWS - CONV_HALO:, cols].astype(F32))
                y, u1, u2 = _causal_conv(u0, halo, f_ref[FV_W0:FV_W0 + 1, cols], f_ref[FV_W1:FV_W1 + 1, cols],
                                         f_ref[FV_W2:FV_W2 + 1, cols], f_ref[FV_B:FV_B + 1, cols])
                ups.append(y)
                taps.append((u0, u1, u2))
            gelu, dgelu = _gelu_and_grad(ups[0])
            da = _dot_nt(dffb, wd[p * hw:(p + 1) * hw, :])
            dups = (da * ups[1] * dgelu, da * gelu)
            for cols, dup, (u0, u1, u2) in zip(slabs, dups, taps):
                fo_ref[FV_B:FV_B + 1, cols] += _colsum(dup)
                fo_ref[FV_W2:FV_W2 + 1, cols] += _colsum(dup * u0)
                fo_ref[FV_W1:FV_W1 + 1, cols] += _colsum(dup * u1)
                fo_ref[FV_W0:FV_W0 + 1, cols] += _colsum(dup * u2)
                du0 = _causal_conv_bwd(dup, carry[:, cols], f_ref[FV_W0:FV_W0 + 1, cols],
                                       f_ref[FV_W1:FV_W1 + 1, cols], f_ref[FV_W2:FV_W2 + 1, cols]).astype(BF16)
                carry[:, cols] = dup[0:CONV_HALO, :]
                dup_ref[:, cols] = du0
                dh2 = dh2 + _dot_nt(du0, wu[:, cols])

        x1v = x1_ref[...]
        r3 = _rms(x1v)
        n3 = x1v * r3
        gpre = _row(v_ref, V_GPRE2)
        sc = 1.0 + _row(v_ref, V_SC2)
        vo_ref[0:1, :] += _colsum(dh2)
        vo_ref[1:2, :] += _colsum(dh2 * n3 * gpre)
        vo_ref[2:3, :] += _colsum(dh2 * n3 * sc)
        dx1_ref[...] = dx2_ref[...] + _rms_bwd(dh2 * (gpre * sc), n3, r3)

    rev = lambda w: pl.BlockSpec((ts, w), lambda i: (nt - 1 - i, 0))
    full = lambda r, w: pl.BlockSpec((r, w), lambda i: (0, 0))
    hbm = pl.BlockSpec(memory_space=pl.ANY)
    halo = pl.BlockSpec((BF16_ROWS, F2), lambda i: (jnp.maximum((nt - 1 - i) * hb - 1, 0), 0))
    return pl.pallas_call(
        body, name="bwd_ffn", grid=(nt,),
        in_specs=[rev(D), rev(D), rev(D), rev(F2), halo, full(VD_ROWS, D), full(FV_ROWS, F2), hbm, hbm],
        out_specs=[rev(D), rev(F2), full(8, D), full(FV_ROWS, F2)],
        out_shape=[jax.ShapeDtypeStruct((s, D), F32), jax.ShapeDtypeStruct((s, F2), BF16),
                   jax.ShapeDtypeStruct((8, D), F32), jax.ShapeDtypeStruct((FV_ROWS, F2), F32)],
        scratch_shapes=[pltpu.VMEM((D, F2), BF16), pltpu.VMEM((F, D), BF16), pltpu.VMEM((CONV_HALO, F2), F32),
                        pltpu.SemaphoreType.DMA((2,))],
        compiler_params=_params(("arbitrary",)),
    )(dff, dx2, x1, up0, up0, vec_d, vec_f, w_up, w_down)


def _bwd_mix(dx1, o, proj, ya0, yb, vec_d, w_pool, w_bout, w_o, ts):
    s = dx1.shape[0]
    nt = s // ts
    hb = ts // BF16_ROWS

    def body(dx1_ref, o_ref, p_ref, halo_ref, ya_ref, yb_ref, v_ref, wp_hbm, wb_hbm, wo_hbm,
             dp_ref, do_ref, dyb_ref, dya_ref, vo_ref, wp, wb, wo, carry_d, carry_e, sem):
        i = pl.program_id(0)
        _load_once([(wp_hbm, wp), (wb_hbm, wb), (wo_hbm, wo)], sem)

        @pl.when(i == 0)
        def _():
            carry_d[...] = jnp.zeros_like(carry_d)
            carry_e[...] = jnp.zeros_like(carry_e)
            vo_ref[...] = jnp.zeros_like(vo_ref)

        first = (i == nt - 1)
        t0 = (nt - 1 - i) * ts
        dx1v = dx1_ref[...]
        ov = o_ref[...]
        r2 = _rms(ov)
        n2 = ov * r2
        gpost = _row(v_ref, V_GPOST1)
        vo_ref[0:1, :] += _colsum(dx1v * (n2 * gpost))
        dy2 = dx1v * _row(v_ref, V_GT1)
        vo_ref[1:2, :] += _colsum(dy2 * n2)
        dob = _rms_bwd(dy2 * gpost, n2, r2).astype(BF16)
        do_ref[...] = dob
        dmerged = _dot_nt(dob, wo[...])

        ya0 = ya_ref[...].astype(F32)
        pscale = _row(v_ref, V_PSCALE)
        sa = jax.nn.sigmoid(p_ref[:, 4 * D:5 * D].astype(F32))
        dp_ref[:, 4 * D:5 * D] = (dmerged * (ya0 * pscale) * sa * (1.0 - sa)).astype(BF16)
        dy_a = dmerged * sa
        vo_ref[2:3, :] += _colsum(dy_a * ya0)
        dya0 = (dy_a * pscale).astype(BF16)
        dya_ref[...] = dya0

        sb = jax.nn.sigmoid(p_ref[:, 5 * D:6 * D].astype(F32))
        dp_ref[:, 5 * D:6 * D] = (dmerged * yb_ref[...].astype(F32) * sb * (1.0 - sb)).astype(BF16)
        dy_b = (dmerged * sb).astype(BF16)
        dyb_ref[...] = dy_b
        dq = _dot_nt(dy_b, wb[...])

        u_x = p_ref[:, D:2 * D].astype(F32)
        u_b = p_ref[:, 2 * D:3 * D].astype(F32)
        u_c = p_ref[:, 3 * D:4 * D].astype(F32)
        v = u_c * u_x
        hrows = slice(BF16_ROWS - CONV_HALO, BF16_ROWS)
        hv = jnp.where(first, 0.0, halo_ref[hrows, 3 * D:4 * D].astype(F32) * halo_ref[hrows, D:2 * D].astype(F32))
        w0, w1, w2 = _row(v_ref, V_CW0), _row(v_ref, V_CW1), _row(v_ref, V_CW2)
        cv, v1, v2 = _causal_conv(v, hv, w0, w1, w2, _row(v_ref, V_CB))
        dp_ref[:, 2 * D:3 * D] = (dq * cv).astype(BF16)
        dcv = dq * u_b
        vo_ref[3:4, :] += _colsum(dcv)
        vo_ref[4:5, :] += _colsum(dcv * v2)
        vo_ref[5:6, :] += _colsum(dcv * v1)
        vo_ref[6:7, :] += _colsum(dcv * v)
        dv = _causal_conv_bwd(dcv, carry_d[...], w0, w1, w2)
        carry_d[...] = dcv[0:CONV_HALO, :]
        dp_ref[:, D:2 * D] = (dv * u_c).astype(BF16)
        dp_ref[:, 3 * D:4 * D] = (dv * u_x).astype(BF16)

        for g in range(NG):
            cols = slice(g * GW, (g + 1) * GW)
            dpg = _dot_nt(dya0[:, cols], wp[g])
            e = dpg / _pool_counts(t0, ts, g)
            ext = jnp.concatenate([e, carry_e[:, cols]], axis=0)
            carry_e[:, cols] = e[0:POOL_HALO, :]
            for l in range(g + 1):
                ext = ext + _shift_up(ext, 1 << l)
            dp_ref[:, cols] = (ext[:ts] - dpg).astype(BF16)

    rev = lambda w: pl.BlockSpec((ts, w), lambda i: (nt - 1 - i, 0))
    hbm = pl.BlockSpec(memory_space=pl.ANY)
    halo = pl.BlockSpec((BF16_ROWS, DIN), lambda i: (jnp.maximum((nt - 1 - i) * hb - 1, 0), 0))
    return pl.pallas_call(
        body, name="bwd_mix", grid=(nt,),
        in_specs=[rev(D), rev(D), rev(DIN), halo, rev(D), rev(D), pl.BlockSpec((VD_ROWS, D), lambda i: (0, 0)),
                  hbm, hbm, hbm],
        out_specs=[rev(DIN), rev(D), rev(D), rev(D), pl.BlockSpec((8, D), lambda i: (0, 0))],
        out_shape=[jax.ShapeDtypeStruct((s, DIN), BF16)] + [jax.ShapeDtypeStruct((s, D), BF16)] * 3
        + [jax.ShapeDtypeStruct((8, D), F32)],
        scratch_shapes=[pltpu.VMEM((NG, GW, GW), BF16), pltpu.VMEM((D, D), BF16), pltpu.VMEM((D, D), BF16),
                        pltpu.VMEM((CONV_HALO, D), F32), pltpu.VMEM((POOL_HALO, D), F32),
                        pltpu.SemaphoreType.DMA((3,))],
        compiler_params=_params(("arbitrary",)),
    )(dx1, o, proj, proj, ya0, yb, vec_d, w_pool, w_bout, w_o)


def _bwd_in(dproj, dx1, x, vec_d, w_in, ts):
    s = x.shape[0]

    def body(dp_ref, dx1_ref, x_ref, v_ref, w_hbm, dx_ref, vo_ref, w_vmem, sem):
        _load_once([(w_hbm, w_vmem)], sem)

        @pl.when(pl.program_id(0) == 0)
        def _():
            vo_ref[...] = jnp.zeros_like(vo_ref)

        dh1 = _dot_nt(dp_ref[...], w_vmem[...])
        xv = x_ref[...]
        r1 = _rms(xv)
        n1 = xv * r1
        gpre = _row(v_ref, V_GPRE1)
        sc = 1.0 + _row(v_ref, V_SC1)
        vo_ref[0:1, :] += _colsum(dh1)
        vo_ref[1:2, :] += _colsum(dh1 * n1 * gpre)
        vo_ref[2:3, :] += _colsum(dh1 * n1 * sc)
        dx_ref[...] = dx1_ref[...] + _rms_bwd(dh1 * (gpre * sc), n1, r1)

    tile = lambda w: pl.BlockSpec((ts, w), lambda i: (i, 0))
    return pl.pallas_call(
        body, name="bwd_in", grid=(s // ts,),
        in_specs=[tile(DIN), tile(D), tile(D), pl.BlockSpec((VD_ROWS, D), lambda i: (0, 0)),
                  pl.BlockSpec(memory_space=pl.ANY)],
        out_specs=[tile(D), pl.BlockSpec((8, D), lambda i: (0, 0))],
        out_shape=[jax.ShapeDtypeStruct((s, D), F32), jax.ShapeDtypeStruct((8, D), F32)],
        scratch_shapes=[pltpu.VMEM((D, DIN), BF16), pltpu.SemaphoreType.DMA((1,))],
        compiler_params=_params(("arbitrary",)),
    )(dproj, dx1, x, vec_d, w_in)


def _wgrad(a, b, tm, tn, ts, name):
    s, m = a.shape
    n = b.shape[1]

    def body(a_ref, b_ref, o_ref):
        @pl.when(pl.program_id(2) == 0)
        def _():
            o_ref[...] = jnp.zeros_like(o_ref)
        o_ref[...] += lax.dot_general(a_ref[...], b_ref[...], (((0,), (0,)), ((), ())), preferred_element_type=F32)

    return pl.pallas_call(
        body, name=name, grid=(m // tm, n // tn, s // ts),
        in_specs=[pl.BlockSpec((ts, tm), lambda i, j, k: (k, i)), pl.BlockSpec((ts, tn), lambda i, j, k: (k, j))],
        out_specs=pl.BlockSpec((tm, tn), lambda i, j, k: (i, j)),
        out_shape=jax.ShapeDtypeStruct((m, n), F32),
        compiler_params=_params(("parallel", "parallel", "arbitrary")),
    )(a, b)


def _wgrad_pool(pg, dya0, ts):
    s = pg.shape[0]

    def body(a_ref, b_ref, o_ref):
        @pl.when(pl.program_id(1) == 0)
        def _():
            o_ref[...] = jnp.zeros_like(o_ref)
        o_ref[0] += lax.dot_general(a_ref[...], b_ref[...], (((0,), (0,)), ((), ())), preferred_element_type=F32)

    return pl.pallas_call(
        body, name="wgrad_pool", grid=(NG, s // ts),
        in_specs=[pl.BlockSpec((ts, GW), lambda g, k: (k, g)), pl.BlockSpec((ts, GW), lambda g, k: (k, g))],
        out_specs=pl.BlockSpec((1, GW, GW), lambda g, k: (g, 0, 0)),
        out_shape=jax.ShapeDtypeStruct((NG, GW, GW), F32),
        compiler_params=_params(("parallel", "arbitrary")),
    )(pg, dya0)


TS_PROJ = 512
TS_MIX = 256
TS_FFN = 256
TS_WGRAD = 1024


def _local_step(x, tgt, vec_d, vec_f, w_in, w_pool, w_bout, w_o, w_up, w_down):
    s = x.shape[0]
    tw = min(TS_WGRAD, s)
    proj, h1 = _fwd_proj(x, vec_d, w_in, min(TS_PROJ, s))
    x1, o, pg, q, merged, ya0, yb = _fwd_mix(proj, x, vec_d, w_pool, w_bout, w_o, min(TS_MIX, s))
    up0, a, h2, dx2, dff, vo_f, loss = _fwd_ffn(x1, tgt, vec_d, vec_f, w_up, w_down, min(TS_FFN, s))
    dx1, dup0, vo_b, fo = _bwd_ffn(dff, dx2, x1, up0, vec_d, vec_f, w_up, w_down, min(TS_FFN, s))
    dproj, do, dyb, dya0, vo_m = _bwd_mix(dx1, o, proj, ya0, yb, vec_d, w_pool, w_bout, w_o, min(TS_MIX, s))
    dx, vo_i = _bwd_in(dproj, dx1, x, vec_d, w_in, min(TS_PROJ, s))
    g_down = _wgrad(a, dff, F // 2, D, tw, "wgrad_down")
    g_up = _wgrad(h2, dup0, D, F2 // NCHIP, tw, "wgrad_up")
    g_o = _wgrad(merged, do, D, D, tw, "wgrad_o")
    g_bout = _wgrad(q, dyb, D, D, tw, "wgrad_bout")
    g_pool = _wgrad_pool(pg, dya0, tw)
    g_in = _wgrad(h1, dproj, D, DIN // NCHIP, tw, "wgrad_in")
    vecs = dict(
        dsh1=vo_i[0], dsc1=vo_i[1], dg_pre_mix=vo_i[2],
        dgt1=vo_m[0], dg_post_mix=vo_m[1], dpool_scale=vo_m[2], dconv_b=vo_m[3],
        dconv_w=vo_m[4:7],
        dsh2=vo_b[0], dsc2=vo_b[1], dg_pre_ffn=vo_b[2],
        dgt2=vo_f[0], dg_post_ffn=vo_f[1],
        dffn_conv_w=fo[FV_W0:FV_W2 + 1], dffn_conv_b=fo[FV_B],
    )
    grads = dict(w_in=g_in, w_pool=g_pool, w_bout=g_bout, w_o=g_o, w_up=g_up, w_down=g_down)
    return loss, dx, vecs, grads


def _aligned(offset, n):
    return offset if isinstance(offset, int) else pl.multiple_of(offset, n)


class _Sharded:
    def __init__(self, name, full_shape, shard_axis, half_axis):
        self.name = name
        self.full_shape = full_shape
        self.shard_axis = shard_axis
        self.half_axis = half_axis
        self.shard_shape = tuple(n // NCHIP if a == shard_axis else n for a, n in enumerate(full_shape))
        self.piece_shape = tuple(n // 2 if a == half_axis else n for a, n in enumerate(self.shard_shape))

    def piece(self, full_ref, k, h):
        idx = []
        for a, n in enumerate(self.piece_shape):
            if a == self.shard_axis and a == self.half_axis:
                idx.append(pl.ds(_aligned((2 * k + h) * n, n), n))
            elif a == self.shard_axis:
                idx.append(pl.ds(_aligned(k * n, n), n))
            elif a == self.half_axis:
                idx.append(pl.ds(_aligned(h * n, n), n))
            else:
                idx.append(slice(None))
        return full_ref.at[tuple(idx)]

    def shard(self, full_ref, k):
        n = self.shard_shape[self.shard_axis]
        idx = [pl.ds(_aligned(k * n, n), n) if a == self.shard_axis else slice(None)
               for a in range(len(self.full_shape))]
        return full_ref.at[tuple(idx)]

    def half(self, shard_ref, h):
        n = self.piece_shape[self.half_axis]
        idx = [pl.ds(_aligned(h * n, n), n) if a == self.half_axis else slice(None)
               for a in range(len(self.full_shape))]
        return shard_ref.at[tuple(idx)]

    def piece_block(self):
        def index_map(k, c_ref):
            c = c_ref[0]
            out = []
            for a in range(len(self.full_shape)):
                if a == self.shard_axis and a == self.half_axis:
                    out.append(2 * k + c)
                elif a == self.shard_axis:
                    out.append(k)
                elif a == self.half_axis:
                    out.append(c)
                else:
                    out.append(0)
            return tuple(out)
        return pl.BlockSpec(self.piece_shape, index_map)


SHARDED = (
    _Sharded("w_in", (D, DIN), 1, 0),
    _Sharded("w_pool", (NG, GW, GW), 1, 0),
    _Sharded("w_bout", (D, D), 0, 0),
    _Sharded("w_o", (D, D), 0, 0),
    _Sharded("w_up", (D, F2), 1, 0),
    _Sharded("w_down", (F, D), 0, 0),
)
NW = len(SHARDED)


def _mesh_place():
    x, y, c = lax.axis_index("x"), lax.axis_index("y"), lax.axis_index("c")
    chips = [(1 - x, y), (x, 1 - y), (1 - x, 1 - y)]
    return x, y, c, 2 * x + y, chips, [2 * px + py for px, py in chips]


def _remote(src, dst, send_sem, recv_sem, device):
    return pltpu.make_async_remote_copy(src_ref=src, dst_ref=dst, send_sem=send_sem, recv_sem=recv_sem,
                                        device_id=device, device_id_type=MESH)


def _all_gather_small(block, name):
    m_per, n = block.shape

    def body(x_ref, out_ref, send_sems, recv_sems, local_sem):
        x, y, c, _, chips, _ = _mesh_place()
        me, sibling = (x, y, c), (x, y, 1 - c)

        def rows(px, py, pc):
            return out_ref.at[pl.ds((4 * px + 2 * py + pc) * m_per, m_per), :]

        def copy(k, blk, to, src=None):
            return _remote(rows(*blk) if src is None else src, rows(*blk), send_sems.at[k], recv_sems.at[k], to)

        mine = pltpu.make_async_copy(x_ref, rows(*me), local_sem)
        mine.start()
        first = [copy(0, me, sibling, src=x_ref)]
        first += [copy(1 + j, me, (*chip, c), src=x_ref) for j, chip in enumerate(chips)]
        for cp in first:
            cp.start()
        passed = [copy(4 + j, (*chip, c), sibling) for j, chip in enumerate(chips)]
        for j, chip in enumerate(chips):
            copy(1 + j, (*chip, c), me).wait_recv()
            passed[j].start()
        copy(0, sibling, me).wait_recv()
        for j, chip in enumerate(chips):
            copy(4 + j, (*chip, 1 - c), me).wait_recv()
        for cp in first + passed:
            cp.wait_send()
        mine.wait()

    return pl.pallas_call(
        body, name=name,
        out_shape=jax.ShapeDtypeStruct((NDEV * m_per, n), block.dtype),
        in_specs=[pl.BlockSpec(memory_space=pltpu.VMEM)],
        out_specs=pl.BlockSpec(memory_space=pltpu.VMEM),
        scratch_shapes=[pltpu.SemaphoreType.DMA((7,)), pltpu.SemaphoreType.DMA((7,)), pltpu.SemaphoreType.DMA],
        compiler_params=pltpu.CompilerParams(vmem_limit_bytes=VMEM_LIMIT),
    )(block)


def _gather_weights(shards):
    def body(*refs):
        ins, outs = refs[:NW], refs[NW:2 * NW]
        send_sems, recv_sems, local_sems = refs[2 * NW:]
        x, y, c, k_me, chips, kidx = _mesh_place()
        sibling = (x, y, 1 - c)

        local = [pltpu.make_async_copy(ins[w], sp.shard(outs[w], k_me), local_sems.at[w])
                 for w, sp in enumerate(SHARDED)]
        for cp in local:
            cp.start()
        sent = []
        for j, chip in enumerate(chips):
            for w, sp in enumerate(SHARDED):
                cp = _remote(sp.half(ins[w], c), sp.piece(outs[w], k_me, c),
                             send_sems.at[6 * w + j], recv_sems.at[6 * w + j], (*chip, c))
                cp.start()
                sent.append(cp)
        for j, chip in enumerate(chips):
            for w, sp in enumerate(SHARDED):
                landed = sp.piece(outs[w], kidx[j], c)
                _remote(landed, landed, send_sems.at[6 * w + j], recv_sems.at[6 * w + j], (*chip, c)).wait_recv()
                cp = _remote(landed, landed, send_sems.at[6 * w + 3 + j], recv_sems.at[6 * w + 3 + j], sibling)
                cp.start()
                sent.append(cp)
        for j in range(3):
            for w, sp in enumerate(SHARDED):
                landed = sp.piece(outs[w], kidx[j], 1 - c)
                _remote(landed, landed, send_sems.at[6 * w + 3 + j], recv_sems.at[6 * w + 3 + j], sibling).wait_recv()
        for cp in sent:
            cp.wait_send()
        for cp in local:
            cp.wait()

    hbm = pl.BlockSpec(memory_space=pl.ANY)
    return pl.pallas_call(
        body, name="gather_weights",
        out_shape=[jax.ShapeDtypeStruct(sp.full_shape, BF16) for sp in SHARDED],
        in_specs=[hbm] * NW, out_specs=[hbm] * NW,
        scratch_shapes=[pltpu.SemaphoreType.DMA((6 * NW,)), pltpu.SemaphoreType.DMA((6 * NW,)),
                        pltpu.SemaphoreType.DMA((NW,))],
    )(*shards)


def _pair_exchange(grads):
    def body(*refs):
        ins, outs = refs[:NW], refs[NW:2 * NW]
        send_sems, recv_sems = refs[2 * NW:]
        x, y, c, _, _, _ = _mesh_place()
        sibling = (x, y, 1 - c)
        sent = []
        for w, sp in enumerate(SHARDED):
            for k in range(NCHIP):
                cp = _remote(sp.piece(ins[w], k, 1 - c), outs[w].at[k],
                             send_sems.at[NCHIP * w + k], recv_sems.at[NCHIP * w + k], sibling)
                cp.start()
                sent.append(cp)
        for cp in sent:
            cp.wait_recv()
        for cp in sent:
            cp.wait_send()

    hbm = pl.BlockSpec(memory_space=pl.ANY)
    return pl.pallas_call(
        body, name="rs_pair_exchange",
        out_shape=[jax.ShapeDtypeStruct((NCHIP,) + sp.piece_shape, F32) for sp in SHARDED],
        in_specs=[hbm] * NW, out_specs=[hbm] * NW,
        scratch_shapes=[pltpu.SemaphoreType.DMA((NCHIP * NW,)), pltpu.SemaphoreType.DMA((NCHIP * NW,))],
    )(*grads)


def _pair_sum(sp, grad, recv, core):
    nd = len(sp.piece_shape)

    def body(c_ref, g_ref, r_ref, o_ref):
        o_ref[...] = (g_ref[...] + r_ref[...]).astype(BF16)

    slot = pl.BlockSpec((None,) + sp.piece_shape, lambda k, c_ref: (k,) + (0,) * nd)
    return pl.pallas_call(
        body, name="rs_pair_sum_" + sp.name,
        grid_spec=pltpu.PrefetchScalarGridSpec(
            num_scalar_prefetch=1, grid=(NCHIP,),
            in_specs=[sp.piece_block(), slot], out_specs=slot),
        out_shape=jax.ShapeDtypeStruct((NCHIP,) + sp.piece_shape, BF16),
        compiler_params=_params(("parallel",)),
    )(core, grad, recv)


def _chip_exchange(parts):
    def body(*refs):
        ins, outs = refs[:NW], refs[NW:2 * NW]
        send_sems, recv_sems, local_sems = refs[2 * NW:]
        x, y, c, k_me, chips, kidx = _mesh_place()
        local = [pltpu.make_async_copy(ins[w].at[k_me], outs[w].at[k_me], local_sems.at[w]) for w in range(NW)]
        for cp in local:
            cp.start()
        sent = []
        for j, chip in enumerate(chips):
            for w in range(NW):
                cp = _remote(ins[w].at[kidx[j]], outs[w].at[k_me], send_sems.at[3 * w + j], recv_sems.at[3 * w + j],
                             (*chip, c))
                cp.start()
                sent.append(cp)
        for j, chip in enumerate(chips):
            for w in range(NW):
                landed = outs[w].at[kidx[j]]
                _remote(landed, landed, send_sems.at[3 * w + j], recv_sems.at[3 * w + j], (*chip, c)).wait_recv()
        for cp in sent:
            cp.wait_send()
        for cp in local:
            cp.wait()

    hbm = pl.BlockSpec(memory_space=pl.ANY)
    return pl.pallas_call(
        body, name="rs_chip_exchange",
        out_shape=[jax.ShapeDtypeStruct((NCHIP,) + sp.piece_shape, BF16) for sp in SHARDED],
        in_specs=[hbm] * NW, out_specs=[hbm] * NW,
        scratch_shapes=[pltpu.SemaphoreType.DMA((3 * NW,)), pltpu.SemaphoreType.DMA((3 * NW,)),
                        pltpu.SemaphoreType.DMA((NW,))],
    )(*parts)


def _chip_sum(sp, recv):
    def body(r_ref, o_ref):
        acc = r_ref[0].astype(F32)
        for k in range(1, NCHIP):
            acc = acc + r_ref[k].astype(F32)
        o_ref[...] = acc

    return pl.pallas_call(
        body, name="rs_chip_sum_" + sp.name,
        out_shape=jax.ShapeDtypeStruct(sp.piece_shape, F32),
        compiler_params=pltpu.CompilerParams(vmem_limit_bytes=VMEM_LIMIT),
    )(recv)


def _pair_share(halves):
    def body(*refs):
        ins, outs = refs[:NW], refs[NW:2 * NW]
        send_sems, recv_sems, local_sems = refs[2 * NW:]
        x, y, c, _, _, _ = _mesh_place()
        sibling = (x, y, 1 - c)
        local, sent = [], []
        for w, sp in enumerate(SHARDED):
            cp = pltpu.make_async_copy(ins[w], sp.half(outs[w], c), local_sems.at[w])
            cp.start()
            local.append(cp)
            cp = _remote(ins[w], sp.half(outs[w], c), send_sems.at[w], recv_sems.at[w], sibling)
            cp.start()
            sent.append(cp)
        for w, sp in enumerate(SHARDED):
            landed = sp.half(outs[w], 1 - c)
            _remote(landed, landed, send_sems.at[w], recv_sems.at[w], sibling).wait_recv()
        for cp in sent:
            cp.wait_send()
        for cp in local:
            cp.wait()

    hbm = pl.BlockSpec(memory_space=pl.ANY)
    return pl.pallas_call(
        body, name="rs_pair_share",
        out_shape=[jax.ShapeDtypeStruct(sp.shard_shape, F32) for sp in SHARDED],
        in_specs=[hbm] * NW, out_specs=[hbm] * NW,
        scratch_shapes=[pltpu.SemaphoreType.DMA((NW,)), pltpu.SemaphoreType.DMA((NW,)),
                        pltpu.SemaphoreType.DMA((NW,))],
    )(*halves)


def _reduce_scatter(grads, core):
    recv_a = _pair_exchange([grads[sp.name] for sp in SHARDED])
    parts = [_pair_sum(sp, grads[sp.name], recv_a[w], core) for w, sp in enumerate(SHARDED)]
    recv_b = _chip_exchange(parts)
    halves = [_chip_sum(sp, recv_b[w]) for w, sp in enumerate(SHARDED)]
    return _pair_share(halves)


def _cast_bf16(w, name):
    def body(w_ref, o_ref):
        o_ref[...] = w_ref[...].astype(BF16)

    return pl.pallas_call(body, name="cast_" + name, out_shape=jax.ShapeDtypeStruct(w.shape, BF16),
                          compiler_params=pltpu.CompilerParams(vmem_limit_bytes=VMEM_LIMIT))(w)


def _matmul_f32(a, b, name):
    def body(a_ref, b_ref, o_ref):
        o_ref[...] = jnp.dot(a_ref[...], b_ref[...], preferred_element_type=F32, precision=lax.Precision.HIGHEST)

    return pl.pallas_call(body, name=name, out_shape=jax.ShapeDtypeStruct((a.shape[0], b.shape[1]), F32),
                          compiler_params=pltpu.CompilerParams(vmem_limit_bytes=VMEM_LIMIT))(a, b)


def _sum_devices(stacked):
    def body(x_ref, o_ref):
        acc = x_ref[0]
        for d in range(1, NDEV):
            acc = acc + x_ref[d]
        o_ref[...] = acc

    return pl.pallas_call(body, name="sum_devices", out_shape=jax.ShapeDtypeStruct(stacked.shape[1:], F32),
                          compiler_params=pltpu.CompilerParams(vmem_limit_bytes=VMEM_LIMIT))(stacked)


def _adamw(w, g, m, v, name):
    r, cdim = w.shape
    tr = r if r <= 256 else (256 if r % 256 == 0 else r // 2)

    def body(w_ref, g_ref, m_ref, v_ref, d_ref, nm_ref, nv_ref):
        gv = g_ref[...]
        nm = ADAM_B1 * m_ref[...] + (1.0 - ADAM_B1) * gv
        nv = ADAM_B2 * v_ref[...] + (1.0 - ADAM_B2) * (gv * gv)
        m_hat = nm / (1.0 - ADAM_B1 ** ADAM_STEP)
        v_hat = nv / (1.0 - ADAM_B2 ** ADAM_STEP)
        d_ref[...] = -ADAM_LR * (m_hat / (jnp.sqrt(v_hat) + ADAM_EPS) + ADAM_WD * w_ref[...])
        nm_ref[...] = nm
        nv_ref[...] = nv

    blk = pl.BlockSpec((tr, cdim), lambda i: (i, 0))
    return pl.pallas_call(
        body, name="adamw_" + name, grid=(r // tr,), in_specs=[blk] * 4, out_specs=[blk] * 3,
        out_shape=[jax.ShapeDtypeStruct(w.shape, F32)] * 3,
        compiler_params=_params(("parallel",)),
    )(w, g, m, v)


WEIGHT_NAMES = ("g_pre_mix", "g_post_mix", "g_pre_ffn", "g_post_ffn", "w_ada", "b_ada", "w_in", "w_pool",
                "pool_scale", "conv_w", "conv_b", "w_bout", "w_o", "w_up", "ffn_conv_w", "ffn_conv_b", "w_down")
MATRIX_NAMES = ("w_ada",) + tuple(sp.name for sp in SHARDED)
VECTOR_NAMES = tuple(n for n in WEIGHT_NAMES if n not in MATRIX_NAMES)

CW = D // NCHIP
FCW = F2 // NCHIP
ADA_W = DIN // NCHIP
COND_BLOCK = (8, 768)
GRAD_BLOCK = (8, 4864)


def _flat_pad(parts, shape):
    flat = jnp.concatenate([p.reshape(-1) for p in parts])
    return jnp.pad(flat, (0, shape[0] * shape[1] - flat.shape[0])).reshape(shape)


def _take(flat, offset, shape):
    size = 1
    for n in shape:
        size *= n
    return flat[offset:offset + size].reshape(shape), offset + size


def kernel(x, c, g_pre_mix, g_post_mix, g_pre_ffn, g_post_ffn, w_ada, b_ada, w_in, w_pool, pool_scale, conv_w, conv_b, w_bout, w_o, w_up, ffn_conv_w, ffn_conv_b, w_down, loss_target, m_g_pre_mix, m_g_post_mix, m_g_pre_ffn, m_g_post_ffn, m_w_ada, m_b_ada, m_w_in, m_w_pool, m_pool_scale, m_conv_w, m_conv_b, m_w_bout, m_w_o, m_w_up, m_ffn_conv_w, m_ffn_conv_b, m_w_down, v_g_pre_mix, v_g_post_mix, v_g_pre_ffn, v_g_post_ffn, v_w_ada, v_b_ada, v_w_in, v_w_pool, v_pool_scale, v_conv_w, v_conv_b, v_w_bout, v_w_o, v_w_up, v_ffn_conv_w, v_ffn_conv_b, v_w_down):
    weights = dict(g_pre_mix=g_pre_mix, g_post_mix=g_post_mix, g_pre_ffn=g_pre_ffn, g_post_ffn=g_post_ffn,
                   w_ada=w_ada, b_ada=b_ada, w_in=w_in, w_pool=w_pool, pool_scale=pool_scale, conv_w=conv_w,
                   conv_b=conv_b, w_bout=w_bout, w_o=w_o, w_up=w_up, ffn_conv_w=ffn_conv_w, ffn_conv_b=ffn_conv_b,
                   w_down=w_down)
    mom1 = dict(g_pre_mix=m_g_pre_mix, g_post_mix=m_g_post_mix, g_pre_ffn=m_g_pre_ffn, g_post_ffn=m_g_post_ffn,
                w_ada=m_w_ada, b_ada=m_b_ada, w_in=m_w_in, w_pool=m_w_pool, pool_scale=m_pool_scale,
                conv_w=m_conv_w, conv_b=m_conv_b, w_bout=m_w_bout, w_o=m_w_o, w_up=m_w_up,
                ffn_conv_w=m_ffn_conv_w, ffn_conv_b=m_ffn_conv_b, w_down=m_w_down)
    mom2 = dict(g_pre_mix=v_g_pre_mix, g_post_mix=v_g_post_mix, g_pre_ffn=v_g_pre_ffn, g_post_ffn=v_g_post_ffn,
                w_ada=v_w_ada, b_ada=v_b_ada, w_in=v_w_in, w_pool=v_w_pool, pool_scale=v_pool_scale,
                conv_w=v_conv_w, conv_b=v_conv_b, w_bout=v_w_bout, w_o=v_w_o, w_up=v_w_up,
                ffn_conv_w=v_ffn_conv_w, ffn_conv_b=v_ffn_conv_b, w_down=v_w_down)

    chip = 2 * lax.axis_index("x") + lax.axis_index("y")
    core = lax.axis_index("c")
    dev = 2 * chip + core
    core_op = jnp.reshape(core, (1,)).astype(jnp.int32)

    cond = _all_gather_small(_flat_pad([c, conv_w, ffn_conv_w], COND_BLOCK), "gather_cond")
    cond = cond.reshape(NDEV, -1)
    c_all = cond[:, :D]
    by_chip = cond[0::2]
    conv_w_full = by_chip[:, D:D + 3 * CW].reshape(NCHIP, 3, CW).transpose(1, 0, 2).reshape(3, D)
    ffn_w_full = by_chip[:, D + 3 * CW:D + 3 * CW + 3 * FCW].reshape(NCHIP, 3, FCW).transpose(1, 0, 2).reshape(3, F2)

    mod_cols = _all_gather_small(_matmul_f32(c_all, w_ada[0], "ada_mod"), "gather_mod")
    mod_cols = mod_cols.reshape(NDEV, NDEV, ADA_W)[0::2]
    mod = lax.dynamic_index_in_dim(mod_cols, dev, axis=1, keepdims=False).reshape(6, D) + b_ada.reshape(6, D)
    vec_d = jnp.concatenate([mod, g_pre_mix, g_post_mix, g_pre_ffn, g_post_ffn, pool_scale, conv_b, conv_w_full,
                             jnp.zeros((VD_ROWS - 15, D), F32)], axis=0)
    vec_f = jnp.concatenate([ffn_w_full, ffn_conv_b, jnp.zeros((FV_ROWS - 4, F2), F32)], axis=0)

    full = _gather_weights([_cast_bf16(weights[sp.name][0], sp.name) for sp in SHARDED])
    loss_blk, dx, vecs, grads = _local_step(x[0], loss_target[0], vec_d, vec_f, *full)

    dmod = [vecs[n] for n in ("dsh1", "dsc1", "dgt1", "dsh2", "dsc2", "dgt2")]
    small = [vecs["dg_pre_mix"], vecs["dg_post_mix"], vecs["dg_pre_ffn"], vecs["dg_post_ffn"]] + dmod + [
        vecs["dpool_scale"], vecs["dconv_w"], vecs["dconv_b"], vecs["dffn_conv_w"], vecs["dffn_conv_b"],
        loss_blk[0]]
    gathered = _all_gather_small(_flat_pad(small, GRAD_BLOCK), "gather_vector_grads")
    total = _sum_devices(gathered.reshape((NDEV,) + GRAD_BLOCK)).reshape(-1)
    vgrad = {}
    off = 0
    for n in ("g_pre_mix", "g_post_mix", "g_pre_ffn", "g_post_ffn"):
        vgrad[n], off = _take(total, off, (1, D))
    dmod_off = off
    vgrad["b_ada"], off = _take(total, off, (1, DIN))
    vgrad["pool_scale"], off = _take(total, off, (1, D))
    g_conv_w, off = _take(total, off, (3, D))
    vgrad["conv_w"] = lax.dynamic_slice_in_dim(g_conv_w, chip * CW, CW, axis=1)[None]
    vgrad["conv_b"], off = _take(total, off, (1, D))
    g_ffn_w, off = _take(total, off, (3, F2))
    vgrad["ffn_conv_w"] = lax.dynamic_slice_in_dim(g_ffn_w, chip * FCW, FCW, axis=1)[None]
    vgrad["ffn_conv_b"], off = _take(total, off, (1, F2))
    loss = total[off]

    dmod_all = gathered.reshape(NDEV, -1)[:, dmod_off:dmod_off + DIN]
    dmod_cols = lax.dynamic_slice_in_dim(dmod_all, chip * ADA_W, ADA_W, axis=1)
    g_ada = _matmul_f32(jnp.pad(c_all.T, ((0, 0), (0, 128 - NDEV))), jnp.pad(dmod_cols, ((0, 128 - NDEV), (0, 0))),
                        "ada_wgrad")

    reduced = _reduce_scatter(grads, core_op)
    mgrad = {"w_ada": g_ada}
    for sp, g in zip(SHARDED, reduced):
        mgrad[sp.name] = g

    grad, delta, new_m, new_v = {}, {}, {}, {}
    for n in MATRIX_NAMES:
        shape = weights[n].shape
        two_d = (-1, shape[-1])
        d, nm, nv = _adamw(weights[n].reshape(two_d), mgrad[n].reshape(two_d), mom1[n].reshape(two_d),
                           mom2[n].reshape(two_d), n)
        grad[n], delta[n], new_m[n], new_v[n] = (a.reshape(shape) for a in (mgrad[n], d, nm, nv))
    flat = lambda tree: jnp.concatenate([tree[n].reshape(1, -1) for n in VECTOR_NAMES], axis=1)
    d, nm, nv = _adamw(flat(weights), flat(vgrad), flat(mom1), flat(mom2), "vectors")
    off = 0
    for n in VECTOR_NAMES:
        shape = weights[n].shape
        grad[n] = vgrad[n].reshape(shape)
        delta[n], _ = _take(d[0], off, shape)
        new_m[n], _ = _take(nm[0], off, shape)
        new_v[n], off = _take(nv[0], off, shape)

    return (loss, dx[None], *[grad[n] for n in WEIGHT_NAMES], *[delta[n] for n in WEIGHT_NAMES],
            *[new_m[n] for n in WEIGHT_NAMES], *[new_v[n] for n in WEIGHT_NAMES])
```

```python
import jax
import jax.numpy as jnp
from jax import lax
from jax.experimental import pallas as pl
from jax.experimental.pallas import tpu as pltpu

F32 = jnp.float32
BF16 = jnp.bfloat16

D = 1024
DIN = 6 * D
F = 2816
F2 = 2 * F
NG = 4
GW = D // NG
POOL_HALO = 16
CONV_HALO = 8
BF16_ROWS = 16
EPS = 1e-6
NCHIP = 4
NDEV = 8

ADAM_LR = 0.001
ADAM_B1 = 0.9
ADAM_B2 = 0.999
ADAM_EPS = 1e-08
ADAM_WD = 0.01
ADAM_STEP = 10

VMEM_LIMIT = 60 * 1024 * 1024

(V_SH1, V_SC1, V_GT1, V_SH2, V_SC2, V_GT2, V_GPRE1, V_GPOST1, V_GPRE2, V_GPOST2,
 V_PSCALE, V_CB, V_CW0, V_CW1, V_CW2) = range(15)
VD_ROWS = 16
FV_W0, FV_W1, FV_W2, FV_B = range(4)
FV_ROWS = 8

MESH = pl.DeviceIdType.MESH


def _params(sem=None, vmem=VMEM_LIMIT):
    return pltpu.CompilerParams(dimension_semantics=sem, vmem_limit_bytes=vmem)


def _row(ref, r):
    return ref[r:r + 1, :]


def _load_once(pairs, sem):
    @pl.when(pl.program_id(0) == 0)
    def _():
        copies = [pltpu.make_async_copy(src, dst, sem.at[n]) for n, (src, dst) in enumerate(pairs)]
        for cp in copies:
            cp.start()
        for cp in copies:
            cp.wait()


def _dot(a, b):
    return jnp.dot(a, b, preferred_element_type=F32)


def _dot_nt(a, b):
    return lax.dot_general(a, b, (((1,), (1,)), ((), ())), preferred_element_type=F32)


def _shift_down(ext, s):
    return pltpu.roll(ext, s, 0)


def _shift_up(ext, s):
    return pltpu.roll(ext, ext.shape[0] - s, 0)


def _causal_conv(x, halo, w0, w1, w2, b):
    ext = jnp.concatenate([halo, x], axis=0)
    x1 = _shift_down(ext, 1)[CONV_HALO:]
    x2 = _shift_down(ext, 2)[CONV_HALO:]
    return b + w2 * x + w1 * x1 + w0 * x2, x1, x2


def _causal_conv_bwd(dy, tail, w0, w1, w2):
    n = dy.shape[0]
    ext = jnp.concatenate([dy, tail], axis=0)
    return w2 * dy + w1 * _shift_up(ext, 1)[:n] + w0 * _shift_up(ext, 2)[:n]


def _pool_counts(t0, n, g):
    t1 = (t0 + 1 + lax.broadcasted_iota(jnp.int32, (n, 1), 0)).astype(F32)
    return jnp.minimum(t1, float(2 << g))


def _rms(x):
    return lax.rsqrt(jnp.mean(x * x, axis=-1, keepdims=True) + EPS)


def _rms_bwd(dn, n, r):
    return r * (dn - n * jnp.mean(dn * n, axis=-1, keepdims=True))


def _colsum(x):
    return jnp.sum(x, axis=0, keepdims=True)


def _gelu_and_grad(x):
    k = 0.7978845608028654
    inner = k * (x + 0.044715 * (x * x * x))
    th = jnp.tanh(inner)
    gelu = 0.5 * x * (1.0 + th)
    dgelu = 0.5 * (1.0 + th) + 0.5 * x * (1.0 - th * th) * (k * (1.0 + 3.0 * 0.044715 * (x * x)))
    return gelu, dgelu


def _fwd_proj(x, vec_d, w_in, ts):
    s = x.shape[0]
    cw = DIN // NCHIP

    def body(x_ref, v_ref, w_hbm, proj_ref, h1_ref, w_vmem, sem):
        _load_once([(w_hbm, w_vmem)], sem)
        xv = x_ref[...]
        n1 = xv * _rms(xv)
        h = n1 * (_row(v_ref, V_GPRE1) * (1.0 + _row(v_ref, V_SC1))) + _row(v_ref, V_SH1)
        hb = h.astype(BF16)
        h1_ref[...] = hb
        for k in range(NCHIP):
            cols = slice(k * cw, (k + 1) * cw)
            proj_ref[:, cols] = _dot(hb, w_vmem[:, cols]).astype(BF16)

    return pl.pallas_call(
        body, name="fwd_proj", grid=(s // ts,),
        in_specs=[pl.BlockSpec((ts, D), lambda i: (i, 0)),
                  pl.BlockSpec((VD_ROWS, D), lambda i: (0, 0)),
                  pl.BlockSpec(memory_space=pl.ANY)],
        out_specs=[pl.BlockSpec((ts, DIN), lambda i: (i, 0)),
                   pl.BlockSpec((ts, D), lambda i: (i, 0))],
        out_shape=[jax.ShapeDtypeStruct((s, DIN), BF16), jax.ShapeDtypeStruct((s, D), BF16)],
        scratch_shapes=[pltpu.VMEM((D, DIN), BF16), pltpu.SemaphoreType.DMA((1,))],
        compiler_params=_params(("arbitrary",)),
    )(x, vec_d, w_in)


def _fwd_mix(proj, x, vec_d, w_pool, w_bout, w_o, ts):
    s = x.shape[0]

    def body(p_ref, x_ref, v_ref, wp_hbm, wb_hbm, wo_hbm,
             x1_ref, o_ref, pg_ref, q_ref, mg_ref, ya_ref, yb_ref,
             wp, wb, wo, carry_p, carry_v, sem):
        i = pl.program_id(0)
        _load_once([(wp_hbm, wp), (wb_hbm, wb), (wo_hbm, wo)], sem)

        @pl.when(i == 0)
        def _():
            carry_p[...] = jnp.zeros_like(carry_p)
            carry_v[...] = jnp.zeros_like(carry_v)

        t0 = i * ts
        u_pool = p_ref[:, 0:D].astype(F32)
        ext = jnp.concatenate([carry_p[...], u_pool], axis=0)
        carry_p[...] = u_pool[ts - POOL_HALO:, :]
        for g in range(NG):
            cols = slice(g * GW, (g + 1) * GW)
            e = ext[:, cols]
            for l in range(g + 1):
                e = e + _shift_down(e, 1 << l)
            pg = e[POOL_HALO:] / _pool_counts(t0, ts, g) - u_pool[:, cols]
            pgb = pg.astype(BF16)
            pg_ref[:, cols] = pgb
            ya_ref[:, cols] = _dot(pgb, wp[g]).astype(BF16)

        u_x = p_ref[:, D:2 * D].astype(F32)
        u_c = p_ref[:, 3 * D:4 * D].astype(F32)
        v = u_c * u_x
        cv, _, _ = _causal_conv(v, carry_v[...], _row(v_ref, V_CW0), _row(v_ref, V_CW1),
                                _row(v_ref, V_CW2), _row(v_ref, V_CB))
        carry_v[...] = v[ts - CONV_HALO:, :]
        q = (p_ref[:, 2 * D:3 * D].astype(F32) * cv).astype(BF16)
        q_ref[...] = q
        y_b = _dot(q, wb[...])
        yb_ref[...] = y_b.astype(BF16)

        y_a = ya_ref[...].astype(F32) * _row(v_ref, V_PSCALE)
        merged = (jax.nn.sigmoid(p_ref[:, 4 * D:5 * D].astype(F32)) * y_a
                  + jax.nn.sigmoid(p_ref[:, 5 * D:6 * D].astype(F32)) * y_b).astype(BF16)
        mg_ref[...] = merged
        o = _dot(merged, wo[...])
        o_ref[...] = o
        x1_ref[...] = x_ref[...] + _row(v_ref, V_GT1) * ((o * _rms(o)) * _row(v_ref, V_GPOST1))

    tile = lambda w: pl.BlockSpec((ts, w), lambda i: (i, 0))
    hbm = pl.BlockSpec(memory_space=pl.ANY)
    return pl.pallas_call(
        body, name="fwd_mix", grid=(s // ts,),
        in_specs=[tile(DIN), tile(D), pl.BlockSpec((VD_ROWS, D), lambda i: (0, 0)), hbm, hbm, hbm],
        out_specs=[tile(D)] * 7,
        out_shape=[jax.ShapeDtypeStruct((s, D), F32), jax.ShapeDtypeStruct((s, D), F32)]
        + [jax.ShapeDtypeStruct((s, D), BF16)] * 5,
        scratch_shapes=[pltpu.VMEM((NG, GW, GW), BF16), pltpu.VMEM((D, D), BF16), pltpu.VMEM((D, D), BF16),
                        pltpu.VMEM((POOL_HALO, D), F32), pltpu.VMEM((CONV_HALO, D), F32),
                        pltpu.SemaphoreType.DMA((3,))],
        compiler_params=_params(("arbitrary",)),
    )(proj, x, vec_d, w_pool, w_bout, w_o)


def _fwd_ffn(x1, tgt, vec_d, vec_f, w_up, w_down, ts):
    s = x1.shape[0]
    hw = F // 2

    def body(x1_ref, t_ref, v_ref, f_ref, wu_hbm, wd_hbm,
             up_ref, a_ref, h2_ref, dx2_ref, dff_ref, vo_ref, loss_ref,
             wu, wd, carry, sem):
        i = pl.program_id(0)
        _load_once([(wu_hbm, wu), (wd_hbm, wd)], sem)

        @pl.when(i == 0)
        def _():
            carry[...] = jnp.zeros_like(carry)
            vo_ref[...] = jnp.zeros_like(vo_ref)
            loss_ref[...] = jnp.zeros_like(loss_ref)

        x1v = x1_ref[...]
        n3 = x1v * _rms(x1v)
        h2 = (n3 * (_row(v_ref, V_GPRE2) * (1.0 + _row(v_ref, V_SC2))) + _row(v_ref, V_SH2)).astype(BF16)
        h2_ref[...] = h2

        ff = jnp.zeros((ts, D), F32)
        for p in range(2):
            up = []
            for cols in (slice(p * hw, (p + 1) * hw), slice(F + p * hw, F + (p + 1) * hw)):
                u0 = _dot(h2, wu[:, cols])
                up_ref[:, cols] = u0.astype(BF16)
                y, _, _ = _causal_conv(u0, carry[:, cols], f_ref[FV_W0:FV_W0 + 1, cols], f_ref[FV_W1:FV_W1 + 1, cols],
                                       f_ref[FV_W2:FV_W2 + 1, cols], f_ref[FV_B:FV_B + 1, cols])
                carry[:, cols] = u0[ts - CONV_HALO:, :]
                up.append(y)
            gelu, _ = _gelu_and_grad(up[0])
            a = (gelu * up[1]).astype(BF16)
            a_ref[:, p * hw:(p + 1) * hw] = a
            ff = ff + _dot(a, wd[p * hw:(p + 1) * hw, :])

        r4 = _rms(ff)
        n4 = ff * r4
        gt2 = _row(v_ref, V_GT2)
        gpost = _row(v_ref, V_GPOST2)
        y4 = n4 * gpost
        diff = (x1v + gt2 * y4) - t_ref[...]
        loss_ref[...] += jnp.full(loss_ref.shape, 0.5 / D * jnp.sum(diff * diff), F32)
        dx2 = diff * (1.0 / D)
        dx2_ref[...] = dx2
        dy4 = dx2 * gt2
        vo_ref[0:1, :] += _colsum(dx2 * y4)
        vo_ref[1:2, :] += _colsum(dy4 * n4)
        dff_ref[...] = _rms_bwd(dy4 * gpost, n4, r4).astype(BF16)

    tile = lambda w: pl.BlockSpec((ts, w), lambda i: (i, 0))
    full = lambda r, w: pl.BlockSpec((r, w), lambda i: (0, 0))
    hbm = pl.BlockSpec(memory_space=pl.ANY)
    return pl.pallas_call(
        body, name="fwd_ffn", grid=(s // ts,),
        in_specs=[tile(D), tile(D), full(VD_ROWS, D), full(FV_ROWS, F2), hbm, hbm],
        out_specs=[tile(F2), tile(F), tile(D), tile(D), tile(D), full(8, D), full(8, 128)],
        out_shape=[jax.ShapeDtypeStruct((s, F2), BF16), jax.ShapeDtypeStruct((s, F), BF16),
                   jax.ShapeDtypeStruct((s, D), BF16), jax.ShapeDtypeStruct((s, D), F32),
                   jax.ShapeDtypeStruct((s, D), BF16), jax.ShapeDtypeStruct((8, D), F32),
                   jax.ShapeDtypeStruct((8, 128), F32)],
        scratch_shapes=[pltpu.VMEM((D, F2), BF16), pltpu.VMEM((F, D), BF16), pltpu.VMEM((CONV_HALO, F2), F32),
                        pltpu.SemaphoreType.DMA((2,))],
        compiler_params=_params(("arbitrary",)),
    )(x1, tgt, vec_d, vec_f, w_up, w_down)


def _bwd_ffn(dff, dx2, x1, up0, vec_d, vec_f, w_up, w_down, ts):
    s = x1.shape[0]
    nt = s // ts
    hw = F // 2
    hb = ts // BF16_ROWS

    def body(dff_ref, dx2_ref, x1_ref, up_ref, halo_ref, v_ref, f_ref, wu_hbm, wd_hbm,
             dx1_ref, dup_ref, vo_ref, fo_ref, wu, wd, carry, sem):
        i = pl.program_id(0)
        _load_once([(wu_hbm, wu), (wd_hbm, wd)], sem)

        @pl.when(i == 0)
        def _():
            carry[...] = jnp.zeros_like(carry)
            vo_ref[...] = jnp.zeros_like(vo_ref)
            fo_ref[...] = jnp.zeros_like(fo_ref)

        first = (i == nt - 1)
        dffb = dff_ref[...]
        dh2 = jnp.zeros((ts, D), F32)
        for p in range(2):
            slabs = (slice(p * hw, (p + 1) * hw), slice(F + p * hw, F + (p + 1) * hw))
            ups, taps = [], []
            for cols in slabs:
                u0 = up_ref[:, cols].astype(F32)
                halo = jnp.where(first, 0.0, halo_ref[BF16_ROWS - CONV_HALO:, cols].astype(F32))
                y, u1, u2 = _causal_conv(u0, halo, f_ref[FV_W0:FV_W0 + 1, cols], f_ref[FV_W1:FV_W1 + 1, cols],
                                         f_ref[FV_W2:FV_W2 + 1, cols], f_ref[FV_B:FV_B + 1, cols])
                ups.append(y)
                taps.append((u0, u1, u2))
            gelu, dgelu = _gelu_and_grad(ups[0])
            da = _dot_nt(dffb, wd[p * hw:(p + 1) * hw, :])
            dups = (da * ups[1] * dgelu, da * gelu)
            for cols, dup, (u0, u1, u2) in zip(slabs, dups, taps):
                fo_ref[FV_B:FV_B + 1, cols] += _colsum(dup)
                fo_ref[FV_W2:FV_W2 + 1, cols] += _colsum(dup * u0)
                fo_ref[FV_W1:FV_W1 + 1, cols] += _colsum(dup * u1)
                fo_ref[FV_W0:FV_W0 + 1, cols] += _colsum(dup * u2)
                du0 = _causal_conv_bwd(dup, carry[:, cols], f_ref[FV_W0:FV_W0 + 1, cols],
                                       f_ref[FV_W1:FV_W1 + 1, cols], f_ref[FV_W2:FV_W2 + 1, cols]).astype(BF16)
                carry[:, cols] = dup[0:CONV_HALO, :]
                dup_ref[:, cols] = du0
                dh2 = dh2 + _dot_nt(du0, wu[:, cols])

        x1v = x1_ref[...]
        r3 = _rms(x1v)
        n3 = x1v * r3
        gpre = _row(v_ref, V_GPRE2)
        sc = 1.0 + _row(v_ref, V_SC2)
        vo_ref[0:1, :] += _colsum(dh2)
        vo_ref[1:2, :] += _colsum(dh2 * n3 * gpre)
        vo_ref[2:3, :] += _colsum(dh2 * n3 * sc)
        dx1_ref[...] = dx2_ref[...] + _rms_bwd(dh2 * (gpre * sc), n3, r3)

    rev = lambda w: pl.BlockSpec((ts, w), lambda i: (nt - 1 - i, 0))
    full = lambda r, w: pl.BlockSpec((r, w), lambda i: (0, 0))
    hbm = pl.BlockSpec(memory_space=pl.ANY)
    halo = pl.BlockSpec((BF16_ROWS, F2), lambda i: (jnp.maximum((nt - 1 - i) * hb - 1, 0), 0))
    return pl.pallas_call(
        body, name="bwd_ffn", grid=(nt,),
        in_specs=[rev(D), rev(D), rev(D), rev(F2), halo, full(VD_ROWS, D), full(FV_ROWS, F2), hbm, hbm],
        out_specs=[rev(D), rev(F2), full(8, D), full(FV_ROWS, F2)],
        out_shape=[jax.ShapeDtypeStruct((s, D), F32), jax.ShapeDtypeStruct((s, F2), BF16),
                   jax.ShapeDtypeStruct((8, D), F32), jax.ShapeDtypeStruct((FV_ROWS, F2), F32)],
        scratch_shapes=[pltpu.VMEM((D, F2), BF16), pltpu.VMEM((F, D), BF16), pltpu.VMEM((CONV_HALO, F2), F32),
                        pltpu.SemaphoreType.DMA((2,))],
        compiler_params=_params(("arbitrary",)),
    )(dff, dx2, x1, up0, up0, vec_d, vec_f, w_up, w_down)


def _bwd_mix(dx1, o, proj, ya0, yb, vec_d, w_pool, w_bout, w_o, ts):
    s = dx1.shape[0]
    nt = s // ts
    hb = ts // BF16_ROWS

    def body(dx1_ref, o_ref, p_ref, halo_ref, ya_ref, yb_ref, v_ref, wp_hbm, wb_hbm, wo_hbm,
             dp_ref, do_ref, dyb_ref, dya_ref, vo_ref, wp, wb, wo, carry_d, carry_e, sem):
        i = pl.program_id(0)
        _load_once([(wp_hbm, wp), (wb_hbm, wb), (wo_hbm, wo)], sem)

        @pl.when(i == 0)
        def _():
            carry_d[...] = jnp.zeros_like(carry_d)
            carry_e[...] = jnp.zeros_like(carry_e)
            vo_ref[...] = jnp.zeros_like(vo_ref)

        first = (i == nt - 1)
        t0 = (nt - 1 - i) * ts
        dx1v = dx1_ref[...]
        ov = o_ref[...]
        r2 = _rms(ov)
        n2 = ov * r2
        gpost = _row(v_ref, V_GPOST1)
        vo_ref[0:1, :] += _colsum(dx1v * (n2 * gpost))
        dy2 = dx1v * _row(v_ref, V_GT1)
        vo_ref[1:2, :] += _colsum(dy2 * n2)
        dob = _rms_bwd(dy2 * gpost, n2, r2).astype(BF16)
        do_ref[...] = dob
        dmerged = _dot_nt(dob, wo[...])

        ya0 = ya_ref[...].astype(F32)
        pscale = _row(v_ref, V_PSCALE)
        sa = jax.nn.sigmoid(p_ref[:, 4 * D:5 * D].astype(F32))
        dp_ref[:, 4 * D:5 * D] = (dmerged * (ya0 * pscale) * sa * (1.0 - sa)).astype(BF16)
        dy_a = dmerged * sa
        vo_ref[2:3, :] += _colsum(dy_a * ya0)
        dya0 = (dy_a * pscale).astype(BF16)
        dya_ref[...] = dya0

        sb = jax.nn.sigmoid(p_ref[:, 5 * D:6 * D].astype(F32))
        dp_ref[:, 5 * D:6 * D] = (dmerged * yb_ref[...].astype(F32) * sb * (1.0 - sb)).astype(BF16)
        dy_b = (dmerged * sb).astype(BF16)
        dyb_ref[...] = dy_b
        dq = _dot_nt(dy_b, wb[...])

        u_x = p_ref[:, D:2 * D].astype(F32)
        u_b = p_ref[:, 2 * D:3 * D].astype(F32)
        u_c = p_ref[:, 3 * D:4 * D].astype(F32)
        v = u_c * u_x
        hrows = slice(BF16_ROWS - CONV_HALO, BF16_ROWS)
        hv = jnp.where(first, 0.0, halo_ref[hrows, 3 * D:4 * D].astype(F32) * halo_ref[hrows, D:2 * D].astype(F32))
        w0, w1, w2 = _row(v_ref, V_CW0), _row(v_ref, V_CW1), _row(v_ref, V_CW2)
        cv, v1, v2 = _causal_conv(v, hv, w0, w1, w2, _row(v_ref, V_CB))
        dp_ref[:, 2 * D:3 * D] = (dq * cv).astype(BF16)
        dcv = dq * u_b
        vo_ref[3:4, :] += _colsum(dcv)
        vo_ref[4:5, :] += _colsum(dcv * v2)
        vo_ref[5:6, :] += _colsum(dcv * v1)
        vo_ref[6:7, :] += _colsum(dcv * v)
        dv = _causal_conv_bwd(dcv, carry_d[...], w0, w1, w2)
        carry_d[...] = dcv[0:CONV_HALO, :]
        dp_ref[:, D:2 * D] = (dv * u_c).astype(BF16)
        dp_ref[:, 3 * D:4 * D] = (dv * u_x).astype(BF16)

        for g in range(NG):
            cols = slice(g * GW, (g + 1) * GW)
            dpg = _dot_nt(dya0[:, cols], wp[g])
            e = dpg / _pool_counts(t0, ts, g)
            ext = jnp.concatenate([e, carry_e[:, cols]], axis=0)
            carry_e[:, cols] = e[0:POOL_HALO, :]
            for l in range(g + 1):
                ext = ext + _shift_up(ext, 1 << l)
            dp_ref[:, cols] = (ext[:ts] - dpg).astype(BF16)

    rev = lambda w: pl.BlockSpec((ts, w), lambda i: (nt - 1 - i, 0))
    hbm = pl.BlockSpec(memory_space=pl.ANY)
    halo = pl.BlockSpec((BF16_ROWS, DIN), lambda i: (jnp.maximum((nt - 1 - i) * hb - 1, 0), 0))
    return pl.pallas_call(
        body, name="bwd_mix", grid=(nt,),
        in_specs=[rev(D), rev(D), rev(DIN), halo, rev(D), rev(D), pl.BlockSpec((VD_ROWS, D), lambda i: (0, 0)),
                  hbm, hbm, hbm],
        out_specs=[rev(DIN), rev(D), rev(D), rev(D), pl.BlockSpec((8, D), lambda i: (0, 0))],
        out_shape=[jax.ShapeDtypeStruct((s, DIN), BF16)] + [jax.ShapeDtypeStruct((s, D), BF16)] * 3
        + [jax.ShapeDtypeStruct((8, D), F32)],
        scratch_shapes=[pltpu.VMEM((NG, GW, GW), BF16), pltpu.VMEM((D, D), BF16), pltpu.VMEM((D, D), BF16),
                        pltpu.VMEM((CONV_HALO, D), F32), pltpu.VMEM((POOL_HALO, D), F32),
                        pltpu.SemaphoreType.DMA((3,))],
        compiler_params=_params(("arbitrary",)),
    )(dx1, o, proj, proj, ya0, yb, vec_d, w_pool, w_bout, w_o)


def _bwd_in(dproj, dx1, x, vec_d, w_in, ts):
    s = x.shape[0]

    def body(dp_ref, dx1_ref, x_ref, v_ref, w_hbm, dx_ref, vo_ref, w_vmem, sem):
        _load_once([(w_hbm, w_vmem)], sem)

        @pl.when(pl.program_id(0) == 0)
        def _():
            vo_ref[...] = jnp.zeros_like(vo_ref)

        dh1 = _dot_nt(dp_ref[...], w_vmem[...])
        xv = x_ref[...]
        r1 = _rms(xv)
        n1 = xv * r1
        gpre = _row(v_ref, V_GPRE1)
        sc = 1.0 + _row(v_ref, V_SC1)
        vo_ref[0:1, :] += _colsum(dh1)
        vo_ref[1:2, :] += _colsum(dh1 * n1 * gpre)
        vo_ref[2:3, :] += _colsum(dh1 * n1 * sc)
        dx_ref[...] = dx1_ref[...] + _rms_bwd(dh1 * (gpre * sc), n1, r1)

    tile = lambda w: pl.BlockSpec((ts, w), lambda i: (i, 0))
    return pl.pallas_call(
        body, name="bwd_in", grid=(s // ts,),
        in_specs=[tile(DIN), tile(D), tile(D), pl.BlockSpec((VD_ROWS, D), lambda i: (0, 0)),
                  pl.BlockSpec(memory_space=pl.ANY)],
        out_specs=[tile(D), pl.BlockSpec((8, D), lambda i: (0, 0))],
        out_shape=[jax.ShapeDtypeStruct((s, D), F32), jax.ShapeDtypeStruct((8, D), F32)],
        scratch_shapes=[pltpu.VMEM((D, DIN), BF16), pltpu.SemaphoreType.DMA((1,))],
        compiler_params=_params(("arbitrary",)),
    )(dproj, dx1, x, vec_d, w_in)


def _wgrad(a, b, tm, tn, ts, name):
    s, m = a.shape
    n = b.shape[1]

    def body(a_ref, b_ref, o_ref):
        @pl.when(pl.program_id(2) == 0)
        def _():
            o_ref[...] = jnp.zeros_like(o_ref)
        o_ref[...] += lax.dot_general(a_ref[...], b_ref[...], (((0,), (0,)), ((), ())), preferred_element_type=F32)

    return pl.pallas_call(
        body, name=name, grid=(m // tm, n // tn, s // ts),
        in_specs=[pl.BlockSpec((ts, tm), lambda i, j, k: (k, i)), pl.BlockSpec((ts, tn), lambda i, j, k: (k, j))],
        out_specs=pl.BlockSpec((tm, tn), lambda i, j, k: (i, j)),
        out_shape=jax.ShapeDtypeStruct((m, n), F32),
        compiler_params=_params(("parallel", "parallel", "arbitrary")),
    )(a, b)


def _wgrad_pool(pg, dya0, ts):
    s = pg.shape[0]

    def body(a_ref, b_ref, o_ref):
        @pl.when(pl.program_id(1) == 0)
        def _():
            o_ref[...] = jnp.zeros_like(o_ref)
        o_ref[0] += lax.dot_general(a_ref[...], b_ref[...], (((0,), (0,)), ((), ())), preferred_element_type=F32)

    return pl.pallas_call(
        body, name="wgrad_pool", grid=(NG, s // ts),
        in_specs=[pl.BlockSpec((ts, GW), lambda g, k: (k, g)), pl.BlockSpec((ts, GW), lambda g, k: (k, g))],
        out_specs=pl.BlockSpec((1, GW, GW), lambda g, k: (g, 0, 0)),
        out_shape=jax.ShapeDtypeStruct((NG, GW, GW), F32),
        compiler_params=_params(("parallel", "arbitrary")),
    )(pg, dya0)


TS_PROJ = 512
TS_MIX = 256
TS_FFN = 256
TS_WGRAD = 1024


def _local_step(x, tgt, vec_d, vec_f, w_in, w_pool, w_bout, w_o, w_up, w_down):
    s = x.shape[0]
    tw = min(TS_WGRAD, s)
    proj, h1 = _fwd_proj(x, vec_d, w_in, min(TS_PROJ, s))
    x1, o, pg, q, merged, ya0, yb = _fwd_mix(proj, x, vec_d, w_pool, w_bout, w_o, min(TS_MIX, s))
    up0, a, h2, dx2, dff, vo_f, loss = _fwd_ffn(x1, tgt, vec_d, vec_f, w_up, w_down, min(TS_FFN, s))
    dx1, dup0, vo_b, fo = _bwd_ffn(dff, dx2, x1, up0, vec_d, vec_f, w_up, w_down, min(TS_FFN, s))
    dproj, do, dyb, dya0, vo_m = _bwd_mix(dx1, o, proj, ya0, yb, vec_d, w_pool, w_bout, w_o, min(TS_MIX, s))
    dx, vo_i = _bwd_in(dproj, dx1, x, vec_d, w_in, min(TS_PROJ, s))
    g_down = _wgrad(a, dff, F // 2, D, tw, "wgrad_down")
    g_up = _wgrad(h2, dup0, D, F2 // NCHIP, tw, "wgrad_up")
    g_o = _wgrad(merged, do, D, D, tw, "wgrad_o")
    g_bout = _wgrad(q, dyb, D, D, tw, "wgrad_bout")
    g_pool = _wgrad_pool(pg, dya0, tw)
    g_in = _wgrad(h1, dproj, D, DIN // NCHIP, tw, "wgrad_in")
    vecs = dict(
        dsh1=vo_i[0], dsc1=vo_i[1], dg_pre_mix=vo_i[2],
        dgt1=vo_m[0], dg_post_mix=vo_m[1], dpool_scale=vo_m[2], dconv_b=vo_m[3],
        dconv_w=vo_m[4:7],
        dsh2=vo_b[0], dsc2=vo_b[1], dg_pre_ffn=vo_b[2],
        dgt2=vo_f[0], dg_post_ffn=vo_f[1],
        dffn_conv_w=fo[FV_W0:FV_W2 + 1], dffn_conv_b=fo[FV_B],
    )
    grads = dict(w_in=g_in, w_pool=g_pool, w_bout=g_bout, w_o=g_o, w_up=g_up, w_down=g_down)
    return loss, dx, vecs, grads


def _aligned(offset, n):
    return offset if isinstance(offset, int) else pl.multiple_of(offset, n)


class _Sharded:
    def __init__(self, name, full_shape, shard_axis, half_axis):
        self.name = name
        self.full_shape = full_shape
        self.shard_axis = shard_axis
        self.half_axis = half_axis
        self.shard_shape = tuple(n // NCHIP if a == shard_axis else n for a, n in enumerate(full_shape))
        self.piece_shape = tuple(n // 2 if a == half_axis else n for a, n in enumerate(self.shard_shape))

    def piece(self, full_ref, k, h):
        idx = []
        for a, n in enumerate(self.piece_shape):
            if a == self.shard_axis and a == self.half_axis:
                idx.append(pl.ds(_aligned((2 * k + h) * n, n), n))
            elif a == self.shard_axis:
                idx.append(pl.ds(_aligned(k * n, n), n))
            elif a == self.half_axis:
                idx.append(pl.ds(_aligned(h * n, n), n))
            else:
                idx.append(slice(None))
        return full_ref.at[tuple(idx)]

    def shard(self, full_ref, k):
        n = self.shard_shape[self.shard_axis]
        idx = [pl.ds(_aligned(k * n, n), n) if a == self.shard_axis else slice(None)
               for a in range(len(self.full_shape))]
        return full_ref.at[tuple(idx)]

    def half(self, shard_ref, h):
        n = self.piece_shape[self.half_axis]
        idx = [pl.ds(_aligned(h * n, n), n) if a == self.half_axis else slice(None)
               for a in range(len(self.full_shape))]
        return shard_ref.at[tuple(idx)]

    def piece_block(self):
        def index_map(k, c_ref):
            c = c_ref[0]
            out = []
            for a in range(len(self.full_shape)):
                if a == self.shard_axis and a == self.half_axis:
                    out.append(2 * k + c)
                elif a == self.shard_axis:
                    out.append(k)
                elif a == self.half_axis:
                    out.append(c)
                else:
                    out.append(0)
            return tuple(out)
        return pl.BlockSpec(self.piece_shape, index_map)


SHARDED = (
    _Sharded("w_in", (D, DIN), 1, 0),
    _Sharded("w_pool", (NG, GW, GW), 1, 0),
    _Sharded("w_bout", (D, D), 0, 0),
    _Sharded("w_o", (D, D), 0, 0),
    _Sharded("w_up", (D, F2), 1, 0),
    _Sharded("w_down", (F, D), 0, 0),
)
NW = len(SHARDED)


def _mesh_place():
    x, y, c = lax.axis_index("x"), lax.axis_index("y"), lax.axis_index("c")
    chips = [(1 - x, y), (x, 1 - y), (1 - x, 1 - y)]
    return x, y, c, 2 * x + y, chips, [2 * px + py for px, py in chips]


def _remote(src, dst, send_sem, recv_sem, device):
    return pltpu.make_async_remote_copy(src_ref=src, dst_ref=dst, send_sem=send_sem, recv_sem=recv_sem,
                                        device_id=device, device_id_type=MESH)


def _all_gather_small(block, name):
    m_per, n = block.shape

    def body(x_ref, out_ref, send_sems, recv_sems, local_sem):
        x, y, c, _, chips, _ = _mesh_place()
        me, sibling = (x, y, c), (x, y, 1 - c)

        def rows(px, py, pc):
            return out_ref.at[pl.ds((4 * px + 2 * py + pc) * m_per, m_per), :]

        def copy(k, blk, to, src=None):
            return _remote(rows(*blk) if src is None else src, rows(*blk), send_sems.at[k], recv_sems.at[k], to)

        mine = pltpu.make_async_copy(x_ref, rows(*me), local_sem)
        mine.start()
        first = [copy(0, me, sibling, src=x_ref)]
        first += [copy(1 + j, me, (*chip, c), src=x_ref) for j, chip in enumerate(chips)]
        for cp in first:
            cp.start()
        passed = [copy(4 + j, (*chip, c), sibling) for j, chip in enumerate(chips)]
        for j, chip in enumerate(chips):
            copy(1 + j, (*chip, c), me).wait_recv()
            passed[j].start()
        copy(0, sibling, me).wait_recv()
        for j, chip in enumerate(chips):
            copy(4 + j, (*chip, 1 - c), me).wait_recv()
        for cp in first + passed:
            cp.wait_send()
        mine.wait()

    return pl.pallas_call(
        body, name=name,
        out_shape=jax.ShapeDtypeStruct((NDEV * m_per, n), block.dtype),
        in_specs=[pl.BlockSpec(memory_space=pltpu.VMEM)],
        out_specs=pl.BlockSpec(memory_space=pltpu.VMEM),
        scratch_shapes=[pltpu.SemaphoreType.DMA((7,)), pltpu.SemaphoreType.DMA((7,)), pltpu.SemaphoreType.DMA],
        compiler_params=pltpu.CompilerParams(vmem_limit_bytes=VMEM_LIMIT),
    )(block)


def _gather_weights(placed):
    def body(*refs):
        outs = refs[NW:2 * NW]
        send_sems, recv_sems = refs[2 * NW:]
        x, y, c, k_me, chips, kidx = _mesh_place()
        sibling = (x, y, 1 - c)

        sent = []
        for j, chip in enumerate(chips):
            for w, sp in enumerate(SHARDED):
                mine = sp.piece(outs[w], k_me, c)
                cp = _remote(mine, mine, send_sems.at[6 * w + j], recv_sems.at[6 * w + j], (*chip, c))
                cp.start()
                sent.append(cp)
        for j, chip in enumerate(chips):
            for w, sp in enumerate(SHARDED):
                landed = sp.piece(outs[w], kidx[j], c)
                _remote(landed, landed, send_sems.at[6 * w + j], recv_sems.at[6 * w + j], (*chip, c)).wait_recv()
                cp = _remote(landed, landed, send_sems.at[6 * w + 3 + j], recv_sems.at[6 * w + 3 + j], sibling)
                cp.start()
                sent.append(cp)
        for j in range(3):
            for w, sp in enumerate(SHARDED):
                landed = sp.piece(outs[w], kidx[j], 1 - c)
                _remote(landed, landed, send_sems.at[6 * w + 3 + j], recv_sems.at[6 * w + 3 + j], sibling).wait_recv()
        for cp in sent:
            cp.wait_send()

    hbm = pl.BlockSpec(memory_space=pl.ANY)
    return pl.pallas_call(
        body, name="gather_weights",
        out_shape=[jax.ShapeDtypeStruct(sp.full_shape, BF16) for sp in SHARDED],
        in_specs=[hbm] * NW, out_specs=[hbm] * NW,
        input_output_aliases={w: w for w in range(NW)},
        scratch_shapes=[pltpu.SemaphoreType.DMA((6 * NW,)), pltpu.SemaphoreType.DMA((6 * NW,))],
    )(*placed)


def _pair_exchange(grads):
    def body(*refs):
        ins, outs = refs[:NW], refs[NW:2 * NW]
        send_sems, recv_sems = refs[2 * NW:]
        x, y, c, _, _, _ = _mesh_place()
        sibling = (x, y, 1 - c)
        sent = []
        for w, sp in enumerate(SHARDED):
            for k in range(NCHIP):
                cp = _remote(sp.piece(ins[w], k, 1 - c), outs[w].at[k],
                             send_sems.at[NCHIP * w + k], recv_sems.at[NCHIP * w + k], sibling)
                cp.start()
                sent.append(cp)
        for cp in sent:
            cp.wait_recv()
        for cp in sent:
            cp.wait_send()

    hbm = pl.BlockSpec(memory_space=pl.ANY)
    return pl.pallas_call(
        body, name="rs_pair_exchange",
        out_shape=[jax.ShapeDtypeStruct((NCHIP,) + sp.piece_shape, F32) for sp in SHARDED],
        in_specs=[hbm] * NW, out_specs=[hbm] * NW,
        scratch_shapes=[pltpu.SemaphoreType.DMA((NCHIP * NW,)), pltpu.SemaphoreType.DMA((NCHIP * NW,))],
    )(*grads)


def _pair_sum(sp, grad, recv, core):
    nd = len(sp.piece_shape)

    def body(c_ref, g_ref, r_ref, o_ref):
        o_ref[...] = (g_ref[...] + r_ref[...]).astype(BF16)

    slot = pl.BlockSpec((None,) + sp.piece_shape, lambda k, c_ref: (k,) + (0,) * nd)
    return pl.pallas_call(
        body, name="rs_pair_sum_" + sp.name,
        grid_spec=pltpu.PrefetchScalarGridSpec(
            num_scalar_prefetch=1, grid=(NCHIP,),
            in_specs=[sp.piece_block(), slot], out_specs=slot),
        out_shape=jax.ShapeDtypeStruct((NCHIP,) + sp.piece_shape, BF16),
        compiler_params=_params(("parallel",)),
    )(core, grad, recv)


def _chip_exchange(parts):
    def body(*refs):
        ins, outs = refs[:NW], refs[NW:2 * NW]
        send_sems, recv_sems = refs[2 * NW:]
        x, y, c, k_me, chips, kidx = _mesh_place()
        sent = []
        for j, chip in enumerate(chips):
            for w in range(NW):
                cp = _remote(ins[w].at[kidx[j]], outs[w].at[k_me], send_sems.at[3 * w + j], recv_sems.at[3 * w + j],
                             (*chip, c))
                cp.start()
                sent.append(cp)
        for j, chip in enumerate(chips):
            for w in range(NW):
                landed = outs[w].at[kidx[j]]
                _remote(landed, landed, send_sems.at[3 * w + j], recv_sems.at[3 * w + j], (*chip, c)).wait_recv()
        for cp in sent:
            cp.wait_send()

    hbm = pl.BlockSpec(memory_space=pl.ANY)
    return pl.pallas_call(
        body, name="rs_chip_exchange",
        out_shape=[jax.ShapeDtypeStruct((NCHIP,) + sp.piece_shape, BF16) for sp in SHARDED],
        in_specs=[hbm] * NW, out_specs=[hbm] * NW,
        scratch_shapes=[pltpu.SemaphoreType.DMA((3 * NW,)), pltpu.SemaphoreType.DMA((3 * NW,))],
    )(*parts)


def _chip_sum(sp, parts, recv, place):
    nd = len(sp.piece_shape)

    def body(p_ref, a_ref, b_ref, o_ref):
        k = pl.program_id(0)
        term = jnp.where(k == p_ref[0], a_ref[...], b_ref[...]).astype(F32)

        @pl.when(k == 0)
        def _():
            o_ref[...] = term

        @pl.when(k > 0)
        def _():
            o_ref[...] += term

    def others(k, p_ref):
        return (jnp.where(k == p_ref[0], (k + 1) % NCHIP, k),) + (0,) * nd

    return pl.pallas_call(
        body, name="rs_chip_sum_" + sp.name,
        grid_spec=pltpu.PrefetchScalarGridSpec(
            num_scalar_prefetch=1, grid=(NCHIP,),
            in_specs=[pl.BlockSpec((None,) + sp.piece_shape, lambda k, p_ref: (p_ref[0],) + (0,) * nd),
                      pl.BlockSpec((None,) + sp.piece_shape, others)],
            out_specs=pl.BlockSpec(sp.piece_shape,
                                   lambda k, p_ref: tuple(p_ref[1] if a == sp.half_axis else 0 for a in range(nd)))),
        out_shape=jax.ShapeDtypeStruct(sp.shard_shape, F32),
        compiler_params=_params(("arbitrary",)),
    )(place, parts, recv)


def _pair_share(halves):
    def body(*refs):
        outs = refs[NW:2 * NW]
        send_sems, recv_sems = refs[2 * NW:]
        x, y, c, _, _, _ = _mesh_place()
        sibling = (x, y, 1 - c)
        sent = []
        for w, sp in enumerate(SHARDED):
            mine = sp.half(outs[w], c)
            cp = _remote(mine, mine, send_sems.at[w], recv_sems.at[w], sibling)
            cp.start()
            sent.append(cp)
        for w, sp in enumerate(SHARDED):
            landed = sp.half(outs[w], 1 - c)
            _remote(landed, landed, send_sems.at[w], recv_sems.at[w], sibling).wait_recv()
        for cp in sent:
            cp.wait_send()

    hbm = pl.BlockSpec(memory_space=pl.ANY)
    return pl.pallas_call(
        body, name="rs_pair_share",
        out_shape=[jax.ShapeDtypeStruct(sp.shard_shape, F32) for sp in SHARDED],
        in_specs=[hbm] * NW, out_specs=[hbm] * NW,
        input_output_aliases={w: w for w in range(NW)},
        scratch_shapes=[pltpu.SemaphoreType.DMA((NW,)), pltpu.SemaphoreType.DMA((NW,))],
    )(*halves)


def _reduce_scatter(grads, core, place):
    recv_a = _pair_exchange([grads[sp.name] for sp in SHARDED])
    parts = [_pair_sum(sp, grads[sp.name], recv_a[w], core) for w, sp in enumerate(SHARDED)]
    recv_b = _chip_exchange(parts)
    halves = [_chip_sum(sp, parts[w], recv_b[w], place) for w, sp in enumerate(SHARDED)]
    return _pair_share(halves)


def _place_bf16(sp, w, place):
    nd = len(sp.full_shape)

    def body(p_ref, w_ref, o_ref):
        o_ref[...] = w_ref[...].astype(BF16)

    return pl.pallas_call(
        body, name="place_" + sp.name,
        grid_spec=pltpu.PrefetchScalarGridSpec(
            num_scalar_prefetch=1, grid=(1,),
            in_specs=[pl.BlockSpec(sp.shard_shape, lambda i, p_ref: (0,) * nd)],
            out_specs=pl.BlockSpec(sp.shard_shape,
                                   lambda i, p_ref: tuple(p_ref[0] if a == sp.shard_axis else 0 for a in range(nd)))),
        out_shape=jax.ShapeDtypeStruct(sp.full_shape, BF16),
        compiler_params=_params(("arbitrary",)),
    )(place, w)


def _matmul_f32(a, b, name):
    def body(a_ref, b_ref, o_ref):
        o_ref[...] = jnp.dot(a_ref[...], b_ref[...], preferred_element_type=F32, precision=lax.Precision.HIGHEST)

    return pl.pallas_call(body, name=name, out_shape=jax.ShapeDtypeStruct((a.shape[0], b.shape[1]), F32),
                          compiler_params=pltpu.CompilerParams(vmem_limit_bytes=VMEM_LIMIT))(a, b)


def _sum_devices(stacked):
    def body(x_ref, o_ref):
        acc = x_ref[0]
        for d in range(1, NDEV):
            acc = acc + x_ref[d]
        o_ref[...] = acc

    return pl.pallas_call(body, name="sum_devices", out_shape=jax.ShapeDtypeStruct(stacked.shape[1:], F32),
                          compiler_params=pltpu.CompilerParams(vmem_limit_bytes=VMEM_LIMIT))(stacked)


def _adamw(w, g, m, v, name):
    r, cdim = w.shape
    tr = r if r <= 256 else (256 if r % 256 == 0 else r // 2)

    def body(w_ref, g_ref, m_ref, v_ref, d_ref, nm_ref, nv_ref):
        gv = g_ref[...]
        nm = ADAM_B1 * m_ref[...] + (1.0 - ADAM_B1) * gv
        nv = ADAM_B2 * v_ref[...] + (1.0 - ADAM_B2) * (gv * gv)
        m_hat = nm / (1.0 - ADAM_B1 ** ADAM_STEP)
        v_hat = nv / (1.0 - ADAM_B2 ** ADAM_STEP)
        d_ref[...] = -ADAM_LR * (m_hat / (jnp.sqrt(v_hat) + ADAM_EPS) + ADAM_WD * w_ref[...])
        nm_ref[...] = nm
        nv_ref[...] = nv

    blk = pl.BlockSpec((tr, cdim), lambda i: (i, 0))
    return pl.pallas_call(
        body, name="adamw_" + name, grid=(r // tr,), in_specs=[blk] * 4, out_specs=[blk] * 3,
        out_shape=[jax.ShapeDtypeStruct(w.shape, F32)] * 3,
        compiler_params=_params(("parallel",)),
    )(w, g, m, v)


WEIGHT_NAMES = ("g_pre_mix", "g_post_mix", "g_pre_ffn", "g_post_ffn", "w_ada", "b_ada", "w_in", "w_pool",
                "pool_scale", "conv_w", "conv_b", "w_bout", "w_o", "w_up", "ffn_conv_w", "ffn_conv_b", "w_down")
MATRIX_NAMES = ("w_ada",) + tuple(sp.name for sp in SHARDED)
VECTOR_NAMES = tuple(n for n in WEIGHT_NAMES if n not in MATRIX_NAMES)

CW = D // NCHIP
FCW = F2 // NCHIP
ADA_W = DIN // NCHIP
COND_BLOCK = (8, 768)
GRAD_BLOCK = (8, 4864)


def _flat_pad(parts, shape):
    flat = jnp.concatenate([p.reshape(-1) for p in parts])
    return jnp.pad(flat, (0, shape[0] * shape[1] - flat.shape[0])).reshape(shape)


def _take(flat, offset, shape):
    size = 1
    for n in shape:
        size *= n
    return flat[offset:offset + size].reshape(shape), offset + size


def kernel(x, c, g_pre_mix, g_post_mix, g_pre_ffn, g_post_ffn, w_ada, b_ada, w_in, w_pool, pool_scale, conv_w, conv_b, w_bout, w_o, w_up, ffn_conv_w, ffn_conv_b, w_down, loss_target, m_g_pre_mix, m_g_post_mix, m_g_pre_ffn, m_g_post_ffn, m_w_ada, m_b_ada, m_w_in, m_w_pool, m_pool_scale, m_conv_w, m_conv_b, m_w_bout, m_w_o, m_w_up, m_ffn_conv_w, m_ffn_conv_b, m_w_down, v_g_pre_mix, v_g_post_mix, v_g_pre_ffn, v_g_post_ffn, v_w_ada, v_b_ada, v_w_in, v_w_pool, v_pool_scale, v_conv_w, v_conv_b, v_w_bout, v_w_o, v_w_up, v_ffn_conv_w, v_ffn_conv_b, v_w_down):
    weights = dict(g_pre_mix=g_pre_mix, g_post_mix=g_post_mix, g_pre_ffn=g_pre_ffn, g_post_ffn=g_post_ffn,
                   w_ada=w_ada, b_ada=b_ada, w_in=w_in, w_pool=w_pool, pool_scale=pool_scale, conv_w=conv_w,
                   conv_b=conv_b, w_bout=w_bout, w_o=w_o, w_up=w_up, ffn_conv_w=ffn_conv_w, ffn_conv_b=ffn_conv_b,
                   w_down=w_down)
    mom1 = dict(g_pre_mix=m_g_pre_mix, g_post_mix=m_g_post_mix, g_pre_ffn=m_g_pre_ffn, g_post_ffn=m_g_post_ffn,
                w_ada=m_w_ada, b_ada=m_b_ada, w_in=m_w_in, w_pool=m_w_pool, pool_scale=m_pool_scale,
                conv_w=m_conv_w, conv_b=m_conv_b, w_bout=m_w_bout, w_o=m_w_o, w_up=m_w_up,
                ffn_conv_w=m_ffn_conv_w, ffn_conv_b=m_ffn_conv_b, w_down=m_w_down)
    mom2 = dict(g_pre_mix=v_g_pre_mix, g_post_mix=v_g_post_mix, g_pre_ffn=v_g_pre_ffn, g_post_ffn=v_g_post_ffn,
                w_ada=v_w_ada, b_ada=v_b_ada, w_in=v_w_in, w_pool=v_w_pool, pool_scale=v_pool_scale,
                conv_w=v_conv_w, conv_b=v_conv_b, w_bout=v_w_bout, w_o=v_w_o, w_up=v_w_up,
                ffn_conv_w=v_ffn_conv_w, ffn_conv_b=v_ffn_conv_b, w_down=v_w_down)

    chip = 2 * lax.axis_index("x") + lax.axis_index("y")
    core = lax.axis_index("c")
    dev = 2 * chip + core
    core_op = jnp.reshape(core, (1,)).astype(jnp.int32)
    place = jnp.stack([chip, core]).astype(jnp.int32)

    cond = _all_gather_small(_flat_pad([c, conv_w, ffn_conv_w], COND_BLOCK), "gather_cond")
    cond = cond.reshape(NDEV, -1)
    c_all = cond[:, :D]
    by_chip = cond[0::2]
    conv_w_full = by_chip[:, D:D + 3 * CW].reshape(NCHIP, 3, CW).transpose(1, 0, 2).reshape(3, D)
    ffn_w_full = by_chip[:, D + 3 * CW:D + 3 * CW + 3 * FCW].reshape(NCHIP, 3, FCW).transpose(1, 0, 2).reshape(3, F2)

    mod_cols = _all_gather_small(_matmul_f32(c_all, w_ada[0], "ada_mod"), "gather_mod")
    mod_cols = mod_cols.reshape(NDEV, NDEV, ADA_W)[0::2]
    mod = lax.dynamic_index_in_dim(mod_cols, dev, axis=1, keepdims=False).reshape(6, D) + b_ada.reshape(6, D)
    vec_d = jnp.concatenate([mod, g_pre_mix, g_post_mix, g_pre_ffn, g_post_ffn, pool_scale, conv_b, conv_w_full,
                             jnp.zeros((VD_ROWS - 15, D), F32)], axis=0)
    vec_f = jnp.concatenate([ffn_w_full, ffn_conv_b, jnp.zeros((FV_ROWS - 4, F2), F32)], axis=0)

    full = _gather_weights([_place_bf16(sp, weights[sp.name][0], place) for sp in SHARDED])
    loss_blk, dx, vecs, grads = _local_step(x[0], loss_target[0], vec_d, vec_f, *full)

    dmod = [vecs[n] for n in ("dsh1", "dsc1", "dgt1", "dsh2", "dsc2", "dgt2")]
    small = [vecs["dg_pre_mix"], vecs["dg_post_mix"], vecs["dg_pre_ffn"], vecs["dg_post_ffn"]] + dmod + [
        vecs["dpool_scale"], vecs["dconv_w"], vecs["dconv_b"], vecs["dffn_conv_w"], vecs["dffn_conv_b"],
        loss_blk[0]]
    gathered = _all_gather_small(_flat_pad(small, GRAD_BLOCK), "gather_vector_grads")
    total = _sum_devices(gathered.reshape((NDEV,) + GRAD_BLOCK)).reshape(-1)
    vgrad = {}
    off = 0
    for n in ("g_pre_mix", "g_post_mix", "g_pre_ffn", "g_post_ffn"):
        vgrad[n], off = _take(total, off, (1, D))
    dmod_off = off
    vgrad["b_ada"], off = _take(total, off, (1, DIN))
    vgrad["pool_scale"], off = _take(total, off, (1, D))
    g_conv_w, off = _take(total, off, (3, D))
    vgrad["conv_w"] = lax.dynamic_slice_in_dim(g_conv_w, chip * CW, CW, axis=1)[None]
    vgrad["conv_b"], off = _take(total, off, (1, D))
    g_ffn_w, off = _take(total, off, (3, F2))
    vgrad["ffn_conv_w"] = lax.dynamic_slice_in_dim(g_ffn_w, chip * FCW, FCW, axis=1)[None]
    vgrad["ffn_conv_b"], off = _take(total, off, (1, F2))
    loss = total[off]

    dmod_all = gathered.reshape(NDEV, -1)[:, dmod_off:dmod_off + DIN]
    dmod_cols = lax.dynamic_slice_in_dim(dmod_all, chip * ADA_W, ADA_W, axis=1)
    g_ada = _matmul_f32(jnp.pad(c_all.T, ((0, 0), (0, 128 - NDEV))), jnp.pad(dmod_cols, ((0, 128 - NDEV), (0, 0))),
                        "ada_wgrad")

    reduced = _reduce_scatter(grads, core_op, place)
    mgrad = {"w_ada": g_ada}
    for sp, g in zip(SHARDED, reduced):
        mgrad[sp.name] = g

    grad, delta, new_m, new_v = {}, {}, {}, {}
    for n in MATRIX_NAMES:
        shape = weights[n].shape
        two_d = (-1, shape[-1])
        d, nm, nv = _adamw(weights[n].reshape(two_d), mgrad[n].reshape(two_d), mom1[n].reshape(two_d),
                           mom2[n].reshape(two_d), n)
        grad[n], delta[n], new_m[n], new_v[n] = (a.reshape(shape) for a in (mgrad[n], d, nm, nv))
    flat = lambda tree: jnp.concatenate([tree[n].reshape(1, -1) for n in VECTOR_NAMES], axis=1)
    d, nm, nv = _adamw(flat(weights), flat(vgrad), flat(mom1), flat(mom2), "vectors")
    off = 0
    for n in VECTOR_NAMES:
        shape = weights[n].shape
        grad[n] = vgrad[n].reshape(shape)
        delta[n], _ = _take(d[0], off, shape)
        new_m[n], _ = _take(nm[0], off, shape)
        new_v[n], off = _take(nv[0], off, shape)

    return (loss, dx[None], *[grad[n] for n in WEIGHT_NAMES], *[delta[n] for n in WEIGHT_NAMES],
            *[new_m[n] for n in WEIGHT_NAMES], *[new_v[n] for n in WEIGHT_NAMES])
```

```python
import jax
import jax.numpy as jnp
from jax import lax
from jax.experimental import pallas as pl
from jax.experimental.pallas import tpu as pltpu

F32 = jnp.float32
BF16 = jnp.bfloat16

D = 1024
DIN = 6 * D
F = 2816
F2 = 2 * F
NG = 4
GW = D // NG
POOL_HALO = 16
CONV_HALO = 8
BF16_ROWS = 16
EPS = 1e-6
NCHIP = 4
NDEV = 8

ADAM_LR = 0.001
ADAM_B1 = 0.9
ADAM_B2 = 0.999
ADAM_EPS = 1e-08
ADAM_WD = 0.01
ADAM_STEP = 10

VMEM_LIMIT = 60 * 1024 * 1024

(V_SH1, V_SC1, V_GT1, V_SH2, V_SC2, V_GT2, V_GPRE1, V_GPOST1, V_GPRE2, V_GPOST2,
 V_PSCALE, V_CB, V_CW0, V_CW1, V_CW2) = range(15)
VD_ROWS = 16
FV_W0, FV_W1, FV_W2, FV_B = range(4)
FV_ROWS = 8

MESH = pl.DeviceIdType.MESH


def _params(sem=None, vmem=VMEM_LIMIT):
    return pltpu.CompilerParams(dimension_semantics=sem, vmem_limit_bytes=vmem)


def _row(ref, r):
    return ref[r:r + 1, :]


def _load_once(pairs, sem):
    @pl.when(pl.program_id(0) == 0)
    def _():
        copies = [pltpu.make_async_copy(src, dst, sem.at[n]) for n, (src, dst) in enumerate(pairs)]
        for cp in copies:
            cp.start()
        for cp in copies:
            cp.wait()


def _dot(a, b):
    return jnp.dot(a, b, preferred_element_type=F32)


def _dot_nt(a, b):
    return lax.dot_general(a, b, (((1,), (1,)), ((), ())), preferred_element_type=F32)


def _shift_down(ext, s):
    return pltpu.roll(ext, s, 0)


def _shift_up(ext, s):
    return pltpu.roll(ext, ext.shape[0] - s, 0)


def _causal_conv(x, halo, w0, w1, w2, b):
    ext = jnp.concatenate([halo, x], axis=0)
    x1 = _shift_down(ext, 1)[CONV_HALO:]
    x2 = _shift_down(ext, 2)[CONV_HALO:]
    return b + w2 * x + w1 * x1 + w0 * x2, x1, x2


def _causal_conv_bwd(dy, tail, w0, w1, w2):
    n = dy.shape[0]
    ext = jnp.concatenate([dy, tail], axis=0)
    return w2 * dy + w1 * _shift_up(ext, 1)[:n] + w0 * _shift_up(ext, 2)[:n]


def _pool_counts(t0, n, g):
    t1 = (t0 + 1 + lax.broadcasted_iota(jnp.int32, (n, 1), 0)).astype(F32)
    return jnp.minimum(t1, float(2 << g))


def _rms(x):
    return lax.rsqrt(jnp.mean(x * x, axis=-1, keepdims=True) + EPS)


def _rms_bwd(dn, n, r):
    return r * (dn - n * jnp.mean(dn * n, axis=-1, keepdims=True))


def _colsum(x):
    return jnp.sum(x, axis=0, keepdims=True)


def _gelu_and_grad(x):
    k = 0.7978845608028654
    inner = k * (x + 0.044715 * (x * x * x))
    th = jnp.tanh(inner)
    gelu = 0.5 * x * (1.0 + th)
    dgelu = 0.5 * (1.0 + th) + 0.5 * x * (1.0 - th * th) * (k * (1.0 + 3.0 * 0.044715 * (x * x)))
    return gelu, dgelu


def _fwd_proj(x, vec_d, w_in, placed_rest, ts):
    s = x.shape[0]
    cw = DIN // NCHIP
    gather = _WeightGather(SHARDED[1:])
    n = gather.n

    def body(*refs):
        x_ref, v_ref, w_hbm = refs[:3]
        proj_ref, h1_ref = refs[3 + n:5 + n]
        rest = refs[5 + n:5 + 2 * n]
        w_vmem, sem, send_sems, recv_sems = refs[5 + 2 * n:]
        i = pl.program_id(0)
        pl.when(i == 0)(lambda: gather.start(rest, send_sems, recv_sems))
        compute(x_ref, v_ref, w_hbm, proj_ref, h1_ref, w_vmem, sem)
        pl.when(i == s // ts - 1)(lambda: gather.finish(rest, send_sems, recv_sems))

    def compute(x_ref, v_ref, w_hbm, proj_ref, h1_ref, w_vmem, sem):
        _load_once([(w_hbm, w_vmem)], sem)
        xv = x_ref[...]
        n1 = xv * _rms(xv)
        h = n1 * (_row(v_ref, V_GPRE1) * (1.0 + _row(v_ref, V_SC1))) + _row(v_ref, V_SH1)
        hb = h.astype(BF16)
        h1_ref[...] = hb
        for k in range(NCHIP):
            cols = slice(k * cw, (k + 1) * cw)
            proj_ref[:, cols] = _dot(hb, w_vmem[:, cols]).astype(BF16)

    return pl.pallas_call(
        body, name="fwd_proj", grid=(s // ts,),
        in_specs=[pl.BlockSpec((ts, D), lambda i: (i, 0)),
                  pl.BlockSpec((VD_ROWS, D), lambda i: (0, 0)),
                  pl.BlockSpec(memory_space=pl.ANY)] + gather.specs_any,
        out_specs=[pl.BlockSpec((ts, DIN), lambda i: (i, 0)),
                   pl.BlockSpec((ts, D), lambda i: (i, 0))] + gather.specs_any,
        out_shape=[jax.ShapeDtypeStruct((s, DIN), BF16), jax.ShapeDtypeStruct((s, D), BF16)] + gather.out_shape,
        input_output_aliases={3 + w: 2 + w for w in range(n)},
        scratch_shapes=[pltpu.VMEM((D, DIN), BF16), pltpu.SemaphoreType.DMA((1,))] + gather.scratch,
        compiler_params=_params(("arbitrary",)),
    )(x, vec_d, w_in, *placed_rest)


def _fwd_mix(proj, x, vec_d, w_pool, w_bout, w_o, ts):
    s = x.shape[0]

    def body(p_ref, x_ref, v_ref, wp_hbm, wb_hbm, wo_hbm,
             x1_ref, o_ref, pg_ref, q_ref, mg_ref, ya_ref, yb_ref,
             wp, wb, wo, carry_p, carry_v, sem):
        i = pl.program_id(0)
        _load_once([(wp_hbm, wp), (wb_hbm, wb), (wo_hbm, wo)], sem)

        @pl.when(i == 0)
        def _():
            carry_p[...] = jnp.zeros_like(carry_p)
            carry_v[...] = jnp.zeros_like(carry_v)

        t0 = i * ts
        u_pool = p_ref[:, 0:D].astype(F32)
        ext = jnp.concatenate([carry_p[...], u_pool], axis=0)
        carry_p[...] = u_pool[ts - POOL_HALO:, :]
        for g in range(NG):
            cols = slice(g * GW, (g + 1) * GW)
            e = ext[:, cols]
            for l in range(g + 1):
                e = e + _shift_down(e, 1 << l)
            pg = e[POOL_HALO:] / _pool_counts(t0, ts, g) - u_pool[:, cols]
            pgb = pg.astype(BF16)
            pg_ref[:, cols] = pgb
            ya_ref[:, cols] = _dot(pgb, wp[g]).astype(BF16)

        u_x = p_ref[:, D:2 * D].astype(F32)
        u_c = p_ref[:, 3 * D:4 * D].astype(F32)
        v = u_c * u_x
        cv, _, _ = _causal_conv(v, carry_v[...], _row(v_ref, V_CW0), _row(v_ref, V_CW1),
                                _row(v_ref, V_CW2), _row(v_ref, V_CB))
        carry_v[...] = v[ts - CONV_HALO:, :]
        q = (p_ref[:, 2 * D:3 * D].astype(F32) * cv).astype(BF16)
        q_ref[...] = q
        y_b = _dot(q, wb[...])
        yb_ref[...] = y_b.astype(BF16)

        y_a = ya_ref[...].astype(F32) * _row(v_ref, V_PSCALE)
        merged = (jax.nn.sigmoid(p_ref[:, 4 * D:5 * D].astype(F32)) * y_a
                  + jax.nn.sigmoid(p_ref[:, 5 * D:6 * D].astype(F32)) * y_b).astype(BF16)
        mg_ref[...] = merged
        o = _dot(merged, wo[...])
        o_ref[...] = o
        x1_ref[...] = x_ref[...] + _row(v_ref, V_GT1) * ((o * _rms(o)) * _row(v_ref, V_GPOST1))

    tile = lambda w: pl.BlockSpec((ts, w), lambda i: (i, 0))
    hbm = pl.BlockSpec(memory_space=pl.ANY)
    return pl.pallas_call(
        body, name="fwd_mix", grid=(s // ts,),
        in_specs=[tile(DIN), tile(D), pl.BlockSpec((VD_ROWS, D), lambda i: (0, 0)), hbm, hbm, hbm],
        out_specs=[tile(D)] * 7,
        out_shape=[jax.ShapeDtypeStruct((s, D), F32), jax.ShapeDtypeStruct((s, D), F32)]
        + [jax.ShapeDtypeStruct((s, D), BF16)] * 5,
        scratch_shapes=[pltpu.VMEM((NG, GW, GW), BF16), pltpu.VMEM((D, D), BF16), pltpu.VMEM((D, D), BF16),
                        pltpu.VMEM((POOL_HALO, D), F32), pltpu.VMEM((CONV_HALO, D), F32),
                        pltpu.SemaphoreType.DMA((3,))],
        compiler_params=_params(("arbitrary",)),
    )(proj, x, vec_d, w_pool, w_bout, w_o)


def _fwd_ffn(x1, tgt, vec_d, vec_f, w_up, w_down, ts):
    s = x1.shape[0]
    hw = F // 2

    def body(x1_ref, t_ref, v_ref, f_ref, wu_hbm, wd_hbm,
             up_ref, a_ref, h2_ref, dx2_ref, dff_ref, vo_ref, loss_ref,
             wu, wd, carry, sem):
        i = pl.program_id(0)
        _load_once([(wu_hbm, wu), (wd_hbm, wd)], sem)

        @pl.when(i == 0)
        def _():
            carry[...] = jnp.zeros_like(carry)
            vo_ref[...] = jnp.zeros_like(vo_ref)
            loss_ref[...] = jnp.zeros_like(loss_ref)

        x1v = x1_ref[...]
        n3 = x1v * _rms(x1v)
        h2 = (n3 * (_row(v_ref, V_GPRE2) * (1.0 + _row(v_ref, V_SC2))) + _row(v_ref, V_SH2)).astype(BF16)
        h2_ref[...] = h2

        ff = jnp.zeros((ts, D), F32)
        for p in range(2):
            up = []
            for cols in (slice(p * hw, (p + 1) * hw), slice(F + p * hw, F + (p + 1) * hw)):
                u0 = _dot(h2, wu[:, cols])
                up_ref[:, cols] = u0.astype(BF16)
                y, _, _ = _causal_conv(u0, carry[:, cols], f_ref[FV_W0:FV_W0 + 1, cols], f_ref[FV_W1:FV_W1 + 1, cols],
                                       f_ref[FV_W2:FV_W2 + 1, cols], f_ref[FV_B:FV_B + 1, cols])
                carry[:, cols] = u0[ts - CONV_HALO:, :]
                up.append(y)
            gelu, _ = _gelu_and_grad(up[0])
            a = (gelu * up[1]).astype(BF16)
            a_ref[:, p * hw:(p + 1) * hw] = a
            ff = ff + _dot(a, wd[p * hw:(p + 1) * hw, :])

        r4 = _rms(ff)
        n4 = ff * r4
        gt2 = _row(v_ref, V_GT2)
        gpost = _row(v_ref, V_GPOST2)
        y4 = n4 * gpost
        diff = (x1v + gt2 * y4) - t_ref[...]
        loss_ref[...] += jnp.full(loss_ref.shape, 0.5 / D * jnp.sum(diff * diff), F32)
        dx2 = diff * (1.0 / D)
        dx2_ref[...] = dx2
        dy4 = dx2 * gt2
        vo_ref[0:1, :] += _colsum(dx2 * y4)
        vo_ref[1:2, :] += _colsum(dy4 * n4)
        dff_ref[...] = _rms_bwd(dy4 * gpost, n4, r4).astype(BF16)

    tile = lambda w: pl.BlockSpec((ts, w), lambda i: (i, 0))
    full = lambda r, w: pl.BlockSpec((r, w), lambda i: (0, 0))
    hbm = pl.BlockSpec(memory_space=pl.ANY)
    return pl.pallas_call(
        body, name="fwd_ffn", grid=(s // ts,),
        in_specs=[tile(D), tile(D), full(VD_ROWS, D), full(FV_ROWS, F2), hbm, hbm],
        out_specs=[tile(F2), tile(F), tile(D), tile(D), tile(D), full(8, D), full(8, 128)],
        out_shape=[jax.ShapeDtypeStruct((s, F2), BF16), jax.ShapeDtypeStruct((s, F), BF16),
                   jax.ShapeDtypeStruct((s, D), BF16), jax.ShapeDtypeStruct((s, D), F32),
                   jax.ShapeDtypeStruct((s, D), BF16), jax.ShapeDtypeStruct((8, D), F32),
                   jax.ShapeDtypeStruct((8, 128), F32)],
        scratch_shapes=[pltpu.VMEM((D, F2), BF16), pltpu.VMEM((F, D), BF16), pltpu.VMEM((CONV_HALO, F2), F32),
                        pltpu.SemaphoreType.DMA((2,))],
        compiler_params=_params(("arbitrary",)),
    )(x1, tgt, vec_d, vec_f, w_up, w_down)


def _bwd_ffn(dff, dx2, x1, up0, vec_d, vec_f, w_up, w_down, exchange, ex_grads, ts):
    s = x1.shape[0]
    nt = s // ts
    hw = F // 2
    hb = ts // BF16_ROWS
    n = exchange.n

    def body(*refs):
        ins, grads = refs[:9], refs[9:9 + n]
        outs, recvs = refs[9 + n:13 + n], refs[13 + n:13 + 2 * n]
        scratch, sems = refs[13 + 2 * n:-2], refs[-2:]
        i = pl.program_id(0)
        pl.when(i == 0)(lambda: exchange.start(grads, recvs, *sems))
        compute(*ins, *outs, *scratch)
        pl.when(i == nt - 1)(lambda: exchange.finish(grads, recvs, *sems))

    def compute(dff_ref, dx2_ref, x1_ref, up_ref, halo_ref, v_ref, f_ref, wu_hbm, wd_hbm,
                dx1_ref, dup_ref, vo_ref, fo_ref, wu, wd, carry, sem):
        i = pl.program_id(0)
        _load_once([(wu_hbm, wu), (wd_hbm, wd)], sem)

        @pl.when(i == 0)
        def _():
            carry[...] = jnp.zeros_like(carry)
            vo_ref[...] = jnp.zeros_like(vo_ref)
            fo_ref[...] = jnp.zeros_like(fo_ref)

        first = (i == nt - 1)
        dffb = dff_ref[...]
        dh2 = jnp.zeros((ts, D), F32)
        for p in range(2):
            slabs = (slice(p * hw, (p + 1) * hw), slice(F + p * hw, F + (p + 1) * hw))
            ups, taps = [], []
            for cols in slabs:
                u0 = up_ref[:, cols].astype(F32)
                halo = jnp.where(first, 0.0, halo_ref[BF16_ROWS - CONV_HALO:, cols].astype(F32))
                y, u1, u2 = _causal_conv(u0, halo, f_ref[FV_W0:FV_W0 + 1, cols], f_ref[FV_W1:FV_W1 + 1, cols],
                                         f_ref[FV_W2:FV_W2 + 1, cols], f_ref[FV_B:FV_B + 1, cols])
                ups.append(y)
                taps.append((u0, u1, u2))
            gelu, dgelu = _gelu_and_grad(ups[0])
            da = _dot_nt(dffb, wd[p * hw:(p + 1) * hw, :])
            dups = (da * ups[1] * dgelu, da * gelu)
            for cols, dup, (u0, u1, u2) in zip(slabs, dups, taps):
                fo_ref[FV_B:FV_B + 1, cols] += _colsum(dup)
                fo_ref[FV_W2:FV_W2 + 1, cols] += _colsum(dup * u0)
                fo_ref[FV_W1:FV_W1 + 1, cols] += _colsum(dup * u1)
                fo_ref[FV_W0:FV_W0 + 1, cols] += _colsum(dup * u2)
                du0 = _causal_conv_bwd(dup, carry[:, cols], f_ref[FV_W0:FV_W0 + 1, cols],
                                       f_ref[FV_W1:FV_W1 + 1, cols], f_ref[FV_W2:FV_W2 + 1, cols]).astype(BF16)
                carry[:, cols] = dup[0:CONV_HALO, :]
                dup_ref[:, cols] = du0
                dh2 = dh2 + _dot_nt(du0, wu[:, cols])

        x1v = x1_ref[...]
        r3 = _rms(x1v)
        n3 = x1v * r3
        gpre = _row(v_ref, V_GPRE2)
        sc = 1.0 + _row(v_ref, V_SC2)
        vo_ref[0:1, :] += _colsum(dh2)
        vo_ref[1:2, :] += _colsum(dh2 * n3 * gpre)
        vo_ref[2:3, :] += _colsum(dh2 * n3 * sc)
        dx1_ref[...] = dx2_ref[...] + _rms_bwd(dh2 * (gpre * sc), n3, r3)

    rev = lambda w: pl.BlockSpec((ts, w), lambda i: (nt - 1 - i, 0))
    full = lambda r, w: pl.BlockSpec((r, w), lambda i: (0, 0))
    hbm = pl.BlockSpec(memory_space=pl.ANY)
    halo = pl.BlockSpec((BF16_ROWS, F2), lambda i: (jnp.maximum((nt - 1 - i) * hb - 1, 0), 0))
    return pl.pallas_call(
        body, name="bwd_ffn", grid=(nt,),
        in_specs=[rev(D), rev(D), rev(D), rev(F2), halo, full(VD_ROWS, D), full(FV_ROWS, F2), hbm, hbm]
        + exchange.specs_any,
        out_specs=[rev(D), rev(F2), full(8, D), full(FV_ROWS, F2)] + exchange.specs_any,
        out_shape=[jax.ShapeDtypeStruct((s, D), F32), jax.ShapeDtypeStruct((s, F2), BF16),
                   jax.ShapeDtypeStruct((8, D), F32), jax.ShapeDtypeStruct((FV_ROWS, F2), F32)] + exchange.out_shape,
        scratch_shapes=[pltpu.VMEM((D, F2), BF16), pltpu.VMEM((F, D), BF16), pltpu.VMEM((CONV_HALO, F2), F32),
                        pltpu.SemaphoreType.DMA((2,))] + exchange.scratch,
        compiler_params=_params(("arbitrary",)),
    )(dff, dx2, x1, up0, up0, vec_d, vec_f, w_up, w_down, *ex_grads)


def _bwd_mix(dx1, o, proj, ya0, yb, vec_d, w_pool, w_bout, w_o, exchange, ex_grads, ts):
    s = dx1.shape[0]
    nt = s // ts
    hb = ts // BF16_ROWS
    n = exchange.n

    def body(*refs):
        ins, grads = refs[:10], refs[10:10 + n]
        outs, recvs = refs[10 + n:15 + n], refs[15 + n:15 + 2 * n]
        scratch, sems = refs[15 + 2 * n:-2], refs[-2:]
        i = pl.program_id(0)
        pl.when(i == 0)(lambda: exchange.start(grads, recvs, *sems))
        compute(*ins, *outs, *scratch)
        pl.when(i == nt - 1)(lambda: exchange.finish(grads, recvs, *sems))

    def compute(dx1_ref, o_ref, p_ref, halo_ref, ya_ref, yb_ref, v_ref, wp_hbm, wb_hbm, wo_hbm,
                dp_ref, do_ref, dyb_ref, dya_ref, vo_ref, wp, wb, wo, carry_d, carry_e, sem):
        i = pl.program_id(0)
        _load_once([(wp_hbm, wp), (wb_hbm, wb), (wo_hbm, wo)], sem)

        @pl.when(i == 0)
        def _():
            carry_d[...] = jnp.zeros_like(carry_d)
            carry_e[...] = jnp.zeros_like(carry_e)
            vo_ref[...] = jnp.zeros_like(vo_ref)

        first = (i == nt - 1)
        t0 = (nt - 1 - i) * ts
        dx1v = dx1_ref[...]
        ov = o_ref[...]
        r2 = _rms(ov)
        n2 = ov * r2
        gpost = _row(v_ref, V_GPOST1)
        vo_ref[0:1, :] += _colsum(dx1v * (n2 * gpost))
        dy2 = dx1v * _row(v_ref, V_GT1)
        vo_ref[1:2, :] += _colsum(dy2 * n2)
        dob = _rms_bwd(dy2 * gpost, n2, r2).astype(BF16)
        do_ref[...] = dob
        dmerged = _dot_nt(dob, wo[...])

        ya0 = ya_ref[...].astype(F32)
        pscale = _row(v_ref, V_PSCALE)
        sa = jax.nn.sigmoid(p_ref[:, 4 * D:5 * D].astype(F32))
        dp_ref[:, 4 * D:5 * D] = (dmerged * (ya0 * pscale) * sa * (1.0 - sa)).astype(BF16)
        dy_a = dmerged * sa
        vo_ref[2:3, :] += _colsum(dy_a * ya0)
        dya0 = (dy_a * pscale).astype(BF16)
        dya_ref[...] = dya0

        sb = jax.nn.sigmoid(p_ref[:, 5 * D:6 * D].astype(F32))
        dp_ref[:, 5 * D:6 * D] = (dmerged * yb_ref[...].astype(F32) * sb * (1.0 - sb)).astype(BF16)
        dy_b = (dmerged * sb).astype(BF16)
        dyb_ref[...] = dy_b
        dq = _dot_nt(dy_b, wb[...])

        u_x = p_ref[:, D:2 * D].astype(F32)
        u_b = p_ref[:, 2 * D:3 * D].astype(F32)
        u_c = p_ref[:, 3 * D:4 * D].astype(F32)
        v = u_c * u_x
        hrows = slice(BF16_ROWS - CONV_HALO, BF16_ROWS)
        hv = jnp.where(first, 0.0, halo_ref[hrows, 3 * D:4 * D].astype(F32) * halo_ref[hrows, D:2 * D].astype(F32))
        w0, w1, w2 = _row(v_ref, V_CW0), _row(v_ref, V_CW1), _row(v_ref, V_CW2)
        cv, v1, v2 = _causal_conv(v, hv, w0, w1, w2, _row(v_ref, V_CB))
        dp_ref[:, 2 * D:3 * D] = (dq * cv).astype(BF16)
        dcv = dq * u_b
        vo_ref[3:4, :] += _colsum(dcv)
        vo_ref[4:5, :] += _colsum(dcv * v2)
        vo_ref[5:6, :] += _colsum(dcv * v1)
        vo_ref[6:7, :] += _colsum(dcv * v)
        dv = _causal_conv_bwd(dcv, carry_d[...], w0, w1, w2)
        carry_d[...] = dcv[0:CONV_HALO, :]
        dp_ref[:, D:2 * D] = (dv * u_c).astype(BF16)
        dp_ref[:, 3 * D:4 * D] = (dv * u_x).astype(BF16)

        for g in range(NG):
            cols = slice(g * GW, (g + 1) * GW)
            dpg = _dot_nt(dya0[:, cols], wp[g])
            e = dpg / _pool_counts(t0, ts, g)
            ext = jnp.concatenate([e, carry_e[:, cols]], axis=0)
            carry_e[:, cols] = e[0:POOL_HALO, :]
            for l in range(g + 1):
                ext = ext + _shift_up(ext, 1 << l)
            dp_ref[:, cols] = (ext[:ts] - dpg).astype(BF16)

    rev = lambda w: pl.BlockSpec((ts, w), lambda i: (nt - 1 - i, 0))
    hbm = pl.BlockSpec(memory_space=pl.ANY)
    halo = pl.BlockSpec((BF16_ROWS, DIN), lambda i: (jnp.maximum((nt - 1 - i) * hb - 1, 0), 0))
    return pl.pallas_call(
        body, name="bwd_mix", grid=(nt,),
        in_specs=[rev(D), rev(D), rev(DIN), halo, rev(D), rev(D), pl.BlockSpec((VD_ROWS, D), lambda i: (0, 0)),
                  hbm, hbm, hbm] + exchange.specs_any,
        out_specs=[rev(DIN), rev(D), rev(D), rev(D), pl.BlockSpec((8, D), lambda i: (0, 0))] + exchange.specs_any,
        out_shape=[jax.ShapeDtypeStruct((s, DIN), BF16)] + [jax.ShapeDtypeStruct((s, D), BF16)] * 3
        + [jax.ShapeDtypeStruct((8, D), F32)] + exchange.out_shape,
        scratch_shapes=[pltpu.VMEM((NG, GW, GW), BF16), pltpu.VMEM((D, D), BF16), pltpu.VMEM((D, D), BF16),
                        pltpu.VMEM((CONV_HALO, D), F32), pltpu.VMEM((POOL_HALO, D), F32),
                        pltpu.SemaphoreType.DMA((3,))] + exchange.scratch,
        compiler_params=_params(("arbitrary",)),
    )(dx1, o, proj, proj, ya0, yb, vec_d, w_pool, w_bout, w_o, *ex_grads)


def _bwd_in(dproj, dx1, x, vec_d, w_in, ts):
    s = x.shape[0]

    def body(dp_ref, dx1_ref, x_ref, v_ref, w_hbm, dx_ref, vo_ref, w_vmem, sem):
        _load_once([(w_hbm, w_vmem)], sem)

        @pl.when(pl.program_id(0) == 0)
        def _():
            vo_ref[...] = jnp.zeros_like(vo_ref)

        dh1 = _dot_nt(dp_ref[...], w_vmem[...])
        xv = x_ref[...]
        r1 = _rms(xv)
        n1 = xv * r1
        gpre = _row(v_ref, V_GPRE1)
        sc = 1.0 + _row(v_ref, V_SC1)
        vo_ref[0:1, :] += _colsum(dh1)
        vo_ref[1:2, :] += _colsum(dh1 * n1 * gpre)
        vo_ref[2:3, :] += _colsum(dh1 * n1 * sc)
        dx_ref[...] = dx1_ref[...] + _rms_bwd(dh1 * (gpre * sc), n1, r1)

    tile = lambda w: pl.BlockSpec((ts, w), lambda i: (i, 0))
    return pl.pallas_call(
        body, name="bwd_in", grid=(s // ts,),
        in_specs=[tile(DIN), tile(D), tile(D), pl.BlockSpec((VD_ROWS, D), lambda i: (0, 0)),
                  pl.BlockSpec(memory_space=pl.ANY)],
        out_specs=[tile(D), pl.BlockSpec((8, D), lambda i: (0, 0))],
        out_shape=[jax.ShapeDtypeStruct((s, D), F32), jax.ShapeDtypeStruct((8, D), F32)],
        scratch_shapes=[pltpu.VMEM((D, DIN), BF16), pltpu.SemaphoreType.DMA((1,))],
        compiler_params=_params(("arbitrary",)),
    )(dproj, dx1, x, vec_d, w_in)


def _dot_tn(a, b):
    return lax.dot_general(a, b, (((0,), (0,)), ((), ())), preferred_element_type=F32)


def _wgrad(a, b, tm, tn, ts, name, dtype, exchange=None, ex_grads=()):
    s, m = a.shape
    nn = b.shape[1]
    grid = (m // tm, nn // tn, s // ts)
    n = exchange.n if exchange else 0

    def body(*refs):
        a_ref, b_ref = refs[:2]
        grads = refs[2:2 + n]
        o_ref = refs[2 + n]
        recvs = refs[3 + n:3 + 2 * n]
        acc = refs[3 + 2 * n]
        sems = refs[4 + 2 * n:]
        i, j, k = pl.program_id(0), pl.program_id(1), pl.program_id(2)
        if exchange:
            pl.when((i == 0) & (j == 0) & (k == 0))(lambda: exchange.start(grads, recvs, *sems))
        part = _dot_tn(a_ref[...], b_ref[...])

        @pl.when(k == 0)
        def _():
            acc[...] = part

        @pl.when(k > 0)
        def _():
            acc[...] += part

        @pl.when(k == grid[2] - 1)
        def _():
            o_ref[...] = acc[...].astype(dtype)

        if exchange:
            pl.when((i == grid[0] - 1) & (j == grid[1] - 1) & (k == grid[2] - 1))(
                lambda: exchange.finish(grads, recvs, *sems))

    hosted = exchange.specs_any if exchange else []
    return pl.pallas_call(
        body, name=name, grid=grid,
        in_specs=[pl.BlockSpec((ts, tm), lambda i, j, k: (k, i)), pl.BlockSpec((ts, tn), lambda i, j, k: (k, j))]
        + hosted,
        out_specs=[pl.BlockSpec((tm, tn), lambda i, j, k: (i, j))] + hosted,
        out_shape=[jax.ShapeDtypeStruct((m, nn), dtype)] + (exchange.out_shape if exchange else []),
        scratch_shapes=[pltpu.VMEM((tm, tn), F32)] + (exchange.scratch if exchange else []),
        compiler_params=_params(("arbitrary", "arbitrary", "arbitrary")),
    )(a, b, *ex_grads)


def _wgrad_pool(pg, dya0, ts):
    s = pg.shape[0]
    nk = s // ts

    def body(a_ref, b_ref, o_ref, acc):
        k = pl.program_id(1)
        part = _dot_tn(a_ref[...], b_ref[...])

        @pl.when(k == 0)
        def _():
            acc[...] = part

        @pl.when(k > 0)
        def _():
            acc[...] += part

        @pl.when(k == nk - 1)
        def _():
            o_ref[0] = acc[...].astype(BF16)

    return pl.pallas_call(
        body, name="wgrad_pool", grid=(NG, nk),
        in_specs=[pl.BlockSpec((ts, GW), lambda g, k: (k, g)), pl.BlockSpec((ts, GW), lambda g, k: (k, g))],
        out_specs=pl.BlockSpec((1, GW, GW), lambda g, k: (g, 0, 0)),
        out_shape=jax.ShapeDtypeStruct((NG, GW, GW), BF16),
        scratch_shapes=[pltpu.VMEM((GW, GW), F32)],
        compiler_params=_params(("arbitrary", "arbitrary")),
    )(pg, dya0)


TS_PROJ = 512
TS_MIX = 256
TS_FFN = 256
TS_WGRAD = 1024


def _local_step(x, tgt, vec_d, vec_f, w_in, placed_rest):
    s = x.shape[0]
    tw = min(TS_WGRAD, s)
    sp_in, sp_pool, sp_bout, sp_o, sp_up, sp_down = SHARDED
    proj, h1, w_pool, w_bout, w_o, w_up, w_down = _fwd_proj(x, vec_d, w_in, placed_rest, min(TS_PROJ, s))
    x1, o, pg, q, merged, ya0, yb = _fwd_mix(proj, x, vec_d, w_pool, w_bout, w_o, min(TS_MIX, s))
    up0, a, h2, dx2, dff, vo_f, loss = _fwd_ffn(x1, tgt, vec_d, vec_f, w_up, w_down, min(TS_FFN, s))
    g_down, = _wgrad(a, dff, F // 2, D, tw, "wgrad_down", BF16)
    dx1, dup0, vo_b, fo, r_down = _bwd_ffn(dff, dx2, x1, up0, vec_d, vec_f, w_up, w_down,
                                           _GradExchange([sp_down]), [g_down], min(TS_FFN, s))
    g_up, = _wgrad(h2, dup0, D, F2 // NCHIP, tw, "wgrad_up", BF16)
    dproj, do, dyb, dya0, vo_m, r_up = _bwd_mix(dx1, o, proj, ya0, yb, vec_d, w_pool, w_bout, w_o,
                                                _GradExchange([sp_up]), [g_up], min(TS_MIX, s))
    dx, vo_i = _bwd_in(dproj, dx1, x, vec_d, w_in, min(TS_PROJ, s))
    g_o, = _wgrad(merged, do, D, D, tw, "wgrad_o", BF16)
    g_bout, = _wgrad(q, dyb, D, D, tw, "wgrad_bout", BF16)
    g_pool = _wgrad_pool(pg, dya0, tw)
    g_in, r_pool, r_bout, r_o = _wgrad(h1, dproj, D, DIN // NCHIP, tw, "wgrad_in", F32,
                                       _GradExchange([sp_pool, sp_bout, sp_o]), [g_pool, g_bout, g_o])
    vecs = dict(
        dsh1=vo_i[0], dsc1=vo_i[1], dg_pre_mix=vo_i[2],
        dgt1=vo_m[0], dg_post_mix=vo_m[1], dpool_scale=vo_m[2], dconv_b=vo_m[3],
        dconv_w=vo_m[4:7],
        dsh2=vo_b[0], dsc2=vo_b[1], dg_pre_ffn=vo_b[2],
        dgt2=vo_f[0], dg_post_ffn=vo_f[1],
        dffn_conv_w=fo[FV_W0:FV_W2 + 1], dffn_conv_b=fo[FV_B],
    )
    local = dict(w_pool=g_pool, w_bout=g_bout, w_o=g_o, w_up=g_up, w_down=g_down)
    received = dict(w_pool=r_pool, w_bout=r_bout, w_o=r_o, w_up=r_up, w_down=r_down)
    return loss, dx, vecs, g_in, local, received


def _aligned(offset, n):
    return offset if isinstance(offset, int) else pl.multiple_of(offset, n)


class _Sharded:
    def __init__(self, name, full_shape, shard_axis, half_axis):
        self.name = name
        self.full_shape = full_shape
        self.shard_axis = shard_axis
        self.half_axis = half_axis
        self.shard_shape = tuple(n // NCHIP if a == shard_axis else n for a, n in enumerate(full_shape))
        self.piece_shape = tuple(n // 2 if a == half_axis else n for a, n in enumerate(self.shard_shape))

    def piece(self, full_ref, k, h):
        idx = []
        for a, n in enumerate(self.piece_shape):
            if a == self.shard_axis and a == self.half_axis:
                idx.append(pl.ds(_aligned((2 * k + h) * n, n), n))
            elif a == self.shard_axis:
                idx.append(pl.ds(_aligned(k * n, n), n))
            elif a == self.half_axis:
                idx.append(pl.ds(_aligned(h * n, n), n))
            else:
                idx.append(slice(None))
        return full_ref.at[tuple(idx)]

    def shard(self, full_ref, k):
        n = self.shard_shape[self.shard_axis]
        idx = [pl.ds(_aligned(k * n, n), n) if a == self.shard_axis else slice(None)
               for a in range(len(self.full_shape))]
        return full_ref.at[tuple(idx)]

    def half(self, shard_ref, h):
        n = self.piece_shape[self.half_axis]
        idx = [pl.ds(_aligned(h * n, n), n) if a == self.half_axis else slice(None)
               for a in range(len(self.full_shape))]
        return shard_ref.at[tuple(idx)]

    def piece_block(self):
        def index_map(k, c_ref):
            c = c_ref[0]
            out = []
            for a in range(len(self.full_shape)):
                if a == self.shard_axis and a == self.half_axis:
                    out.append(2 * k + c)
                elif a == self.shard_axis:
                    out.append(k)
                elif a == self.half_axis:
                    out.append(c)
                else:
                    out.append(0)
            return tuple(out)
        return pl.BlockSpec(self.piece_shape, index_map)


SHARDED = (
    _Sharded("w_in", (D, DIN), 1, 0),
    _Sharded("w_pool", (NG, GW, GW), 1, 0),
    _Sharded("w_bout", (D, D), 0, 0),
    _Sharded("w_o", (D, D), 0, 0),
    _Sharded("w_up", (D, F2), 1, 0),
    _Sharded("w_down", (F, D), 0, 0),
)
NW = len(SHARDED)


def _mesh_place():
    x, y, c = lax.axis_index("x"), lax.axis_index("y"), lax.axis_index("c")
    chips = [(1 - x, y), (x, 1 - y), (1 - x, 1 - y)]
    return x, y, c, 2 * x + y, chips, [2 * px + py for px, py in chips]


def _remote(src, dst, send_sem, recv_sem, device):
    return pltpu.make_async_remote_copy(src_ref=src, dst_ref=dst, send_sem=send_sem, recv_sem=recv_sem,
                                        device_id=device, device_id_type=MESH)


def _all_gather_small(block, name):
    m_per, n = block.shape

    def body(x_ref, out_ref, send_sems, recv_sems, local_sem):
        x, y, c, _, chips, _ = _mesh_place()
        me, sibling = (x, y, c), (x, y, 1 - c)

        def rows(px, py, pc):
            return out_ref.at[pl.ds((4 * px + 2 * py + pc) * m_per, m_per), :]

        def copy(k, blk, to, src=None):
            return _remote(rows(*blk) if src is None else src, rows(*blk), send_sems.at[k], recv_sems.at[k], to)

        mine = pltpu.make_async_copy(x_ref, rows(*me), local_sem)
        mine.start()
        first = [copy(0, me, sibling, src=x_ref)]
        first += [copy(1 + j, me, (*chip, c), src=x_ref) for j, chip in enumerate(chips)]
        for cp in first:
            cp.start()
        passed = [copy(4 + j, (*chip, c), sibling) for j, chip in enumerate(chips)]
        for j, chip in enumerate(chips):
            copy(1 + j, (*chip, c), me).wait_recv()
            passed[j].start()
        copy(0, sibling, me).wait_recv()
        for j, chip in enumerate(chips):
            copy(4 + j, (*chip, 1 - c), me).wait_recv()
        for cp in first + passed:
            cp.wait_send()
        mine.wait()

    return pl.pallas_call(
        body, name=name,
        out_shape=jax.ShapeDtypeStruct((NDEV * m_per, n), block.dtype),
        in_specs=[pl.BlockSpec(memory_space=pltpu.VMEM)],
        out_specs=pl.BlockSpec(memory_space=pltpu.VMEM),
        scratch_shapes=[pltpu.SemaphoreType.DMA((7,)), pltpu.SemaphoreType.DMA((7,)), pltpu.SemaphoreType.DMA],
        compiler_params=pltpu.CompilerParams(vmem_limit_bytes=VMEM_LIMIT),
    )(block)


class _WeightGather:
    def __init__(self, specs):
        self.specs = specs
        self.n = len(specs)
        self.specs_any = [pl.BlockSpec(memory_space=pl.ANY)] * self.n
        self.out_shape = [jax.ShapeDtypeStruct(sp.full_shape, BF16) for sp in specs]
        self.scratch = [pltpu.SemaphoreType.DMA((6 * self.n,)), pltpu.SemaphoreType.DMA((6 * self.n,))]

    def _sends(self, outs, send_sems, recv_sems):
        x, y, c, k_me, chips, _ = _mesh_place()
        sends = []
        for j, chip in enumerate(chips):
            for w, sp in enumerate(self.specs):
                mine = sp.piece(outs[w], k_me, c)
                sends.append(_remote(mine, mine, send_sems.at[6 * w + j], recv_sems.at[6 * w + j], (*chip, c)))
        return sends

    def start(self, outs, send_sems, recv_sems):
        for cp in self._sends(outs, send_sems, recv_sems):
            cp.start()

    def finish(self, outs, send_sems, recv_sems):
        x, y, c, _, chips, kidx = _mesh_place()
        sibling = (x, y, 1 - c)
        passed = []
        for j, chip in enumerate(chips):
            for w, sp in enumerate(self.specs):
                landed = sp.piece(outs[w], kidx[j], c)
                _remote(landed, landed, send_sems.at[6 * w + j], recv_sems.at[6 * w + j], (*chip, c)).wait_recv()
                cp = _remote(landed, landed, send_sems.at[6 * w + 3 + j], recv_sems.at[6 * w + 3 + j], sibling)
                cp.start()
                passed.append(cp)
        for j in range(3):
            for w, sp in enumerate(self.specs):
                landed = sp.piece(outs[w], kidx[j], 1 - c)
                _remote(landed, landed, send_sems.at[6 * w + 3 + j], recv_sems.at[6 * w + 3 + j], sibling).wait_recv()
        for cp in self._sends(outs, send_sems, recv_sems) + passed:
            cp.wait_send()


def _gather_weights(placed, specs, name):
    gather = _WeightGather(specs)
    n = gather.n

    def body(*refs):
        outs, sems = refs[n:2 * n], refs[2 * n:]
        gather.start(outs, *sems)
        gather.finish(outs, *sems)

    return pl.pallas_call(
        body, name=name, out_shape=gather.out_shape, in_specs=gather.specs_any, out_specs=gather.specs_any,
        input_output_aliases={w: w for w in range(n)}, scratch_shapes=gather.scratch,
    )(*placed)


class _GradExchange:
    def __init__(self, specs):
        self.specs = specs
        self.n = len(specs)
        self.specs_any = [pl.BlockSpec(memory_space=pl.ANY)] * self.n
        self.out_shape = [jax.ShapeDtypeStruct((NDEV,) + sp.piece_shape, BF16) for sp in specs]
        self.scratch = [pltpu.SemaphoreType.DMA((7 * self.n,)), pltpu.SemaphoreType.DMA((NDEV * self.n,))]

    def _sends(self, grads, recvs, send_sems, recv_sems):
        x, y, c, k_me, chips, kidx = _mesh_place()
        dev = 2 * k_me + c
        sends = []
        for w, sp in enumerate(self.specs):
            slot, arrival = recvs[w].at[dev], recv_sems.at[NDEV * w + dev]
            sends.append(_remote(sp.piece(grads[w], k_me, 1 - c), slot, send_sems.at[7 * w], arrival, (x, y, 1 - c)))
            for j, chip in enumerate(chips):
                for h in range(2):
                    sends.append(_remote(sp.piece(grads[w], kidx[j], h), slot, send_sems.at[7 * w + 1 + 2 * j + h],
                                         arrival, (*chip, h)))
        return sends

    def start(self, grads, recvs, send_sems, recv_sems):
        for cp in self._sends(grads, recvs, send_sems, recv_sems):
            cp.start()

    def finish(self, grads, recvs, send_sems, recv_sems):
        x, y, c, k_me, _, _ = _mesh_place()
        dev = 2 * k_me + c
        for w in range(self.n):
            for d in range(NDEV):
                landed = recvs[w].at[d]
                arrival = _remote(landed, landed, send_sems.at[7 * w], recv_sems.at[NDEV * w + d], (x, y, c))
                pl.when(d != dev)(arrival.wait_recv)
        for cp in self._sends(grads, recvs, send_sems, recv_sems):
            cp.wait_send()


def _device_sum(sp, local, recv, place):
    nd = len(sp.piece_shape)

    def body(p_ref, a_ref, b_ref, o_ref):
        d = pl.program_id(0)
        term = jnp.where(d == p_ref[2], a_ref[...], b_ref[...]).astype(F32)

        @pl.when(d == 0)
        def _():
            o_ref[...] = term

        @pl.when(d > 0)
        def _():
            o_ref[...] += term

    def mine(d, p_ref):
        return tuple(2 * p_ref[0] + p_ref[1] if a == sp.shard_axis == sp.half_axis else
                     p_ref[0] if a == sp.shard_axis else p_ref[1] if a == sp.half_axis else 0 for a in range(nd))

    def others(d, p_ref):
        return (jnp.where(d == p_ref[2], (d + 1) % NDEV, d),) + (0,) * nd

    return pl.pallas_call(
        body, name="rs_device_sum_" + sp.name,
        grid_spec=pltpu.PrefetchScalarGridSpec(
            num_scalar_prefetch=1, grid=(NDEV,),
            in_specs=[pl.BlockSpec(sp.piece_shape, mine), pl.BlockSpec((None,) + sp.piece_shape, others)],
            out_specs=pl.BlockSpec(sp.piece_shape,
                                   lambda d, p_ref: tuple(p_ref[1] if a == sp.half_axis else 0 for a in range(nd)))),
        out_shape=jax.ShapeDtypeStruct(sp.shard_shape, F32),
        compiler_params=_params(("arbitrary",)),
    )(place, local, recv)


def _pair_exchange(grads, specs):
    n = len(specs)

    def body(*refs):
        ins, outs = refs[:n], refs[n:2 * n]
        send_sems, recv_sems = refs[2 * n:]
        x, y, c, _, _, _ = _mesh_place()
        sibling = (x, y, 1 - c)
        sent = []
        for w, sp in enumerate(specs):
            for k in range(NCHIP):
                cp = _remote(sp.piece(ins[w], k, 1 - c), outs[w].at[k],
                             send_sems.at[NCHIP * w + k], recv_sems.at[NCHIP * w + k], sibling)
                cp.start()
                sent.append(cp)
        for cp in sent:
            cp.wait_recv()
        for cp in sent:
            cp.wait_send()

    hbm = pl.BlockSpec(memory_space=pl.ANY)
    return pl.pallas_call(
        body, name="rs_pair_exchange",
        out_shape=[jax.ShapeDtypeStruct((NCHIP,) + sp.piece_shape, F32) for sp in specs],
        in_specs=[hbm] * n, out_specs=[hbm] * n,
        scratch_shapes=[pltpu.SemaphoreType.DMA((NCHIP * n,)), pltpu.SemaphoreType.DMA((NCHIP * n,))],
    )(*grads)


def _pair_sum(sp, grad, recv, core):
    nd = len(sp.piece_shape)

    def body(c_ref, g_ref, r_ref, o_ref):
        o_ref[...] = (g_ref[...] + r_ref[...]).astype(BF16)

    slot = pl.BlockSpec((None,) + sp.piece_shape, lambda k, c_ref: (k,) + (0,) * nd)
    return pl.pallas_call(
        body, name="rs_pair_sum_" + sp.name,
        grid_spec=pltpu.PrefetchScalarGridSpec(
            num_scalar_prefetch=1, grid=(NCHIP,),
            in_specs=[sp.piece_block(), slot], out_specs=slot),
        out_shape=jax.ShapeDtypeStruct((NCHIP,) + sp.piece_shape, BF16),
        compiler_params=_params(("parallel",)),
    )(core, grad, recv)


def _chip_exchange(parts, specs):
    n = len(specs)

    def body(*refs):
        ins, outs = refs[:n], refs[n:2 * n]
        send_sems, recv_sems = refs[2 * n:]
        x, y, c, k_me, chips, kidx = _mesh_place()
        sent = []
        for j, chip in enumerate(chips):
            for w in range(n):
                cp = _remote(ins[w].at[kidx[j]], outs[w].at[k_me], send_sems.at[3 * w + j], recv_sems.at[3 * w + j],
                             (*chip, c))
                cp.start()
                sent.append(cp)
        for j, chip in enumerate(chips):
            for w in range(n):
                landed = outs[w].at[kidx[j]]
                _remote(landed, landed, send_sems.at[3 * w + j], recv_sems.at[3 * w + j], (*chip, c)).wait_recv()
        for cp in sent:
            cp.wait_send()

    hbm = pl.BlockSpec(memory_space=pl.ANY)
    return pl.pallas_call(
        body, name="rs_chip_exchange",
        out_shape=[jax.ShapeDtypeStruct((NCHIP,) + sp.piece_shape, BF16) for sp in specs],
        in_specs=[hbm] * n, out_specs=[hbm] * n,
        scratch_shapes=[pltpu.SemaphoreType.DMA((3 * n,)), pltpu.SemaphoreType.DMA((3 * n,))],
    )(*parts)


def _chip_sum(sp, parts, recv, place):
    nd = len(sp.piece_shape)

    def body(p_ref, a_ref, b_ref, o_ref):
        k = pl.program_id(0)
        term = jnp.where(k == p_ref[0], a_ref[...], b_ref[...]).astype(F32)

        @pl.when(k == 0)
        def _():
            o_ref[...] = term

        @pl.when(k > 0)
        def _():
            o_ref[...] += term

    def others(k, p_ref):
        return (jnp.where(k == p_ref[0], (k + 1) % NCHIP, k),) + (0,) * nd

    return pl.pallas_call(
        body, name="rs_chip_sum_" + sp.name,
        grid_spec=pltpu.PrefetchScalarGridSpec(
            num_scalar_prefetch=1, grid=(NCHIP,),
            in_specs=[pl.BlockSpec((None,) + sp.piece_shape, lambda k, p_ref: (p_ref[0],) + (0,) * nd),
                      pl.BlockSpec((None,) + sp.piece_shape, others)],
            out_specs=pl.BlockSpec(sp.piece_shape,
                                   lambda k, p_ref: tuple(p_ref[1] if a == sp.half_axis else 0 for a in range(nd)))),
        out_shape=jax.ShapeDtypeStruct(sp.shard_shape, F32),
        compiler_params=_params(("arbitrary",)),
    )(place, parts, recv)


def _pair_share(halves):
    def body(*refs):
        outs = refs[NW:2 * NW]
        send_sems, recv_sems = refs[2 * NW:]
        x, y, c, _, _, _ = _mesh_place()
        sibling = (x, y, 1 - c)
        sent = []
        for w, sp in enumerate(SHARDED):
            mine = sp.half(outs[w], c)
            cp = _remote(mine, mine, send_sems.at[w], recv_sems.at[w], sibling)
            cp.start()
            sent.append(cp)
        for w, sp in enumerate(SHARDED):
            landed = sp.half(outs[w], 1 - c)
            _remote(landed, landed, send_sems.at[w], recv_sems.at[w], sibling).wait_recv()
        for cp in sent:
            cp.wait_send()

    hbm = pl.BlockSpec(memory_space=pl.ANY)
    return pl.pallas_call(
        body, name="rs_pair_share",
        out_shape=[jax.ShapeDtypeStruct(sp.shard_shape, F32) for sp in SHARDED],
        in_specs=[hbm] * NW, out_specs=[hbm] * NW,
        input_output_aliases={w: w for w in range(NW)},
        scratch_shapes=[pltpu.SemaphoreType.DMA((NW,)), pltpu.SemaphoreType.DMA((NW,))],
    )(*halves)


def _reduce_scatter(g_in, local, received, core, place):
    sp_in = SHARDED[0]
    recv_a = _pair_exchange([g_in], [sp_in])
    part = _pair_sum(sp_in, g_in, recv_a[0], core)
    recv_b = _chip_exchange([part], [sp_in])
    halves = [_chip_sum(sp_in, part, recv_b[0], place)]
    halves += [_device_sum(sp, local[sp.name], received[sp.name], place) for sp in SHARDED[1:]]
    return _pair_share(halves)


def _place_bf16(sp, w, place):
    nd = len(sp.full_shape)

    def body(p_ref, w_ref, o_ref):
        o_ref[...] = w_ref[...].astype(BF16)

    return pl.pallas_call(
        body, name="place_" + sp.name,
        grid_spec=pltpu.PrefetchScalarGridSpec(
            num_scalar_prefetch=1, grid=(1,),
            in_specs=[pl.BlockSpec(sp.shard_shape, lambda i, p_ref: (0,) * nd)],
            out_specs=pl.BlockSpec(sp.shard_shape,
                                   lambda i, p_ref: tuple(p_ref[0] if a == sp.shard_axis else 0 for a in range(nd)))),
        out_shape=jax.ShapeDtypeStruct(sp.full_shape, BF16),
        compiler_params=_params(("arbitrary",)),
    )(place, w)


def _matmul_f32(a, b, name):
    def body(a_ref, b_ref, o_ref):
        o_ref[...] = jnp.dot(a_ref[...], b_ref[...], preferred_element_type=F32, precision=lax.Precision.HIGHEST)

    return pl.pallas_call(body, name=name, out_shape=jax.ShapeDtypeStruct((a.shape[0], b.shape[1]), F32),
                          compiler_params=pltpu.CompilerParams(vmem_limit_bytes=VMEM_LIMIT))(a, b)


def _sum_devices(stacked):
    def body(x_ref, o_ref):
        acc = x_ref[0]
        for d in range(1, NDEV):
            acc = acc + x_ref[d]
        o_ref[...] = acc

    return pl.pallas_call(body, name="sum_devices", out_shape=jax.ShapeDtypeStruct(stacked.shape[1:], F32),
                          compiler_params=pltpu.CompilerParams(vmem_limit_bytes=VMEM_LIMIT))(stacked)


def _adamw(w, g, m, v, name):
    r, cdim = w.shape
    tr = r if r <= 256 else (256 if r % 256 == 0 else r // 2)

    def body(w_ref, g_ref, m_ref, v_ref, d_ref, nm_ref, nv_ref):
        gv = g_ref[...]
        nm = ADAM_B1 * m_ref[...] + (1.0 - ADAM_B1) * gv
        nv = ADAM_B2 * v_ref[...] + (1.0 - ADAM_B2) * (gv * gv)
        m_hat = nm / (1.0 - ADAM_B1 ** ADAM_STEP)
        v_hat = nv / (1.0 - ADAM_B2 ** ADAM_STEP)
        d_ref[...] = -ADAM_LR * (m_hat / (jnp.sqrt(v_hat) + ADAM_EPS) + ADAM_WD * w_ref[...])
        nm_ref[...] = nm
        nv_ref[...] = nv

    blk = pl.BlockSpec((tr, cdim), lambda i: (i, 0))
    return pl.pallas_call(
        body, name="adamw_" + name, grid=(r // tr,), in_specs=[blk] * 4, out_specs=[blk] * 3,
        out_shape=[jax.ShapeDtypeStruct(w.shape, F32)] * 3,
        compiler_params=_params(("parallel",)),
    )(w, g, m, v)


WEIGHT_NAMES = ("g_pre_mix", "g_post_mix", "g_pre_ffn", "g_post_ffn", "w_ada", "b_ada", "w_in", "w_pool",
                "pool_scale", "conv_w", "conv_b", "w_bout", "w_o", "w_up", "ffn_conv_w", "ffn_conv_b", "w_down")
MATRIX_NAMES = ("w_ada",) + tuple(sp.name for sp in SHARDED)
VECTOR_NAMES = tuple(n for n in WEIGHT_NAMES if n not in MATRIX_NAMES)

CW = D // NCHIP
FCW = F2 // NCHIP
ADA_W = DIN // NCHIP
COND_BLOCK = (8, 768)
GRAD_BLOCK = (8, 4864)


def _flat_pad(parts, shape):
    flat = jnp.concatenate([p.reshape(-1) for p in parts])
    return jnp.pad(flat, (0, shape[0] * shape[1] - flat.shape[0])).reshape(shape)


def _take(flat, offset, shape):
    size = 1
    for n in shape:
        size *= n
    return flat[offset:offset + size].reshape(shape), offset + size


def kernel(x, c, g_pre_mix, g_post_mix, g_pre_ffn, g_post_ffn, w_ada, b_ada, w_in, w_pool, pool_scale, conv_w, conv_b, w_bout, w_o, w_up, ffn_conv_w, ffn_conv_b, w_down, loss_target, m_g_pre_mix, m_g_post_mix, m_g_pre_ffn, m_g_post_ffn, m_w_ada, m_b_ada, m_w_in, m_w_pool, m_pool_scale, m_conv_w, m_conv_b, m_w_bout, m_w_o, m_w_up, m_ffn_conv_w, m_ffn_conv_b, m_w_down, v_g_pre_mix, v_g_post_mix, v_g_pre_ffn, v_g_post_ffn, v_w_ada, v_b_ada, v_w_in, v_w_pool, v_pool_scale, v_conv_w, v_conv_b, v_w_bout, v_w_o, v_w_up, v_ffn_conv_w, v_ffn_conv_b, v_w_down):
    weights = dict(g_pre_mix=g_pre_mix, g_post_mix=g_post_mix, g_pre_ffn=g_pre_ffn, g_post_ffn=g_post_ffn,
                   w_ada=w_ada, b_ada=b_ada, w_in=w_in, w_pool=w_pool, pool_scale=pool_scale, conv_w=conv_w,
                   conv_b=conv_b, w_bout=w_bout, w_o=w_o, w_up=w_up, ffn_conv_w=ffn_conv_w, ffn_conv_b=ffn_conv_b,
                   w_down=w_down)
    mom1 = dict(g_pre_mix=m_g_pre_mix, g_post_mix=m_g_post_mix, g_pre_ffn=m_g_pre_ffn, g_post_ffn=m_g_post_ffn,
                w_ada=m_w_ada, b_ada=m_b_ada, w_in=m_w_in, w_pool=m_w_pool, pool_scale=m_pool_scale,
                conv_w=m_conv_w, conv_b=m_conv_b, w_bout=m_w_bout, w_o=m_w_o, w_up=m_w_up,
                ffn_conv_w=m_ffn_conv_w, ffn_conv_b=m_ffn_conv_b, w_down=m_w_down)
    mom2 = dict(g_pre_mix=v_g_pre_mix, g_post_mix=v_g_post_mix, g_pre_ffn=v_g_pre_ffn, g_post_ffn=v_g_post_ffn,
                w_ada=v_w_ada, b_ada=v_b_ada, w_in=v_w_in, w_pool=v_w_pool, pool_scale=v_pool_scale,
                conv_w=v_conv_w, conv_b=v_conv_b, w_bout=v_w_bout, w_o=v_w_o, w_up=v_w_up,
                ffn_conv_w=v_ffn_conv_w, ffn_conv_b=v_ffn_conv_b, w_down=v_w_down)

    chip = 2 * lax.axis_index("x") + lax.axis_index("y")
    core = lax.axis_index("c")
    dev = 2 * chip + core
    core_op = jnp.reshape(core, (1,)).astype(jnp.int32)
    place = jnp.stack([chip, core, dev]).astype(jnp.int32)

    cond = _all_gather_small(_flat_pad([c, conv_w, ffn_conv_w], COND_BLOCK), "gather_cond")
    cond = cond.reshape(NDEV, -1)
    c_all = cond[:, :D]
    by_chip = cond[0::2]
    conv_w_full = by_chip[:, D:D + 3 * CW].reshape(NCHIP, 3, CW).transpose(1, 0, 2).reshape(3, D)
    ffn_w_full = by_chip[:, D + 3 * CW:D + 3 * CW + 3 * FCW].reshape(NCHIP, 3, FCW).transpose(1, 0, 2).reshape(3, F2)

    mod_cols = _all_gather_small(_matmul_f32(c_all, w_ada[0], "ada_mod"), "gather_mod")
    mod_cols = mod_cols.reshape(NDEV, NDEV, ADA_W)[0::2]
    mod = lax.dynamic_index_in_dim(mod_cols, dev, axis=1, keepdims=False).reshape(6, D) + b_ada.reshape(6, D)
    vec_d = jnp.concatenate([mod, g_pre_mix, g_post_mix, g_pre_ffn, g_post_ffn, pool_scale, conv_b, conv_w_full,
                             jnp.zeros((VD_ROWS - 15, D), F32)], axis=0)
    vec_f = jnp.concatenate([ffn_w_full, ffn_conv_b, jnp.zeros((FV_ROWS - 4, F2), F32)], axis=0)

    placed = [_place_bf16(sp, weights[sp.name][0], place) for sp in SHARDED]
    w_in_full, = _gather_weights(placed[:1], SHARDED[:1], "gather_w_in")
    loss_blk, dx, vecs, g_in, local, received = _local_step(x[0], loss_target[0], vec_d, vec_f, w_in_full, placed[1:])

    dmod = [vecs[n] for n in ("dsh1", "dsc1", "dgt1", "dsh2", "dsc2", "dgt2")]
    small = [vecs["dg_pre_mix"], vecs["dg_post_mix"], vecs["dg_pre_ffn"], vecs["dg_post_ffn"]] + dmod + [
        vecs["dpool_scale"], vecs["dconv_w"], vecs["dconv_b"], vecs["dffn_conv_w"], vecs["dffn_conv_b"],
        loss_blk[0]]
    gathered = _all_gather_small(_flat_pad(small, GRAD_BLOCK), "gather_vector_grads")
    total = _sum_devices(gathered.reshape((NDEV,) + GRAD_BLOCK)).reshape(-1)
    vgrad = {}
    off = 0
    for n in ("g_pre_mix", "g_post_mix", "g_pre_ffn", "g_post_ffn"):
        vgrad[n], off = _take(total, off, (1, D))
    dmod_off = off
    vgrad["b_ada"], off = _take(total, off, (1, DIN))
    vgrad["pool_scale"], off = _take(total, off, (1, D))
    g_conv_w, off = _take(total, off, (3, D))
    vgrad["conv_w"] = lax.dynamic_slice_in_dim(g_conv_w, chip * CW, CW, axis=1)[None]
    vgrad["conv_b"], off = _take(total, off, (1, D))
    g_ffn_w, off = _take(total, off, (3, F2))
    vgrad["ffn_conv_w"] = lax.dynamic_slice_in_dim(g_ffn_w, chip * FCW, FCW, axis=1)[None]
    vgrad["ffn_conv_b"], off = _take(total, off, (1, F2))
    loss = total[off]

    dmod_all = gathered.reshape(NDEV, -1)[:, dmod_off:dmod_off + DIN]
    dmod_cols = lax.dynamic_slice_in_dim(dmod_all, chip * ADA_W, ADA_W, axis=1)
    g_ada = _matmul_f32(jnp.pad(c_all.T, ((0, 0), (0, 128 - NDEV))), jnp.pad(dmod_cols, ((0, 128 - NDEV), (0, 0))),
                        "ada_wgrad")

    reduced = _reduce_scatter(g_in, local, received, core_op, place)
    mgrad = {"w_ada": g_ada}
    for sp, g in zip(SHARDED, reduced):
        mgrad[sp.name] = g

    grad, delta, new_m, new_v = {}, {}, {}, {}
    for n in MATRIX_NAMES:
        shape = weights[n].shape
        two_d = (-1, shape[-1])
        d, nm, nv = _adamw(weights[n].reshape(two_d), mgrad[n].reshape(two_d), mom1[n].reshape(two_d),
                           mom2[n].reshape(two_d), n)
        grad[n], delta[n], new_m[n], new_v[n] = (a.reshape(shape) for a in (mgrad[n], d, nm, nv))
    flat = lambda tree: jnp.concatenate([tree[n].reshape(1, -1) for n in VECTOR_NAMES], axis=1)
    d, nm, nv = _adamw(flat(weights), flat(vgrad), flat(mom1), flat(mom2), "vectors")
    off = 0
    for n in VECTOR_NAMES:
        shape = weights[n].shape
        grad[n] = vgrad[n].reshape(shape)
        delta[n], _ = _take(d[0], off, shape)
        new_m[n], _ = _take(nm[0], off, shape)
        new_v[n], off = _take(nv[0], off, shape)

    return (loss, dx[None], *[grad[n] for n in WEIGHT_NAMES], *[delta[n] for n in WEIGHT_NAMES],
            *[new_m[n] for n in WEIGHT_NAMES], *[new_v[n] for n in WEIGHT_NAMES])
```

```python
import jax
import jax.numpy as jnp
from jax import lax
from jax.experimental import pallas as pl
from jax.experimental.pallas import tpu as pltpu

F32 = jnp.float32
BF16 = jnp.bfloat16

D = 1024
DIN = 6 * D
F = 2816
F2 = 2 * F
NG = 4
GW = D // NG
POOL_CARRY = 16
CONV_CARRY = 3
EPS = 1e-6
NCHIP = 4
NDEV = 8

ADAM_LR = 0.001
ADAM_B1 = 0.9
ADAM_B2 = 0.999
ADAM_EPS = 1e-08
ADAM_WD = 0.01
ADAM_STEP = 10

VMEM_LIMIT = 60 * 1024 * 1024

(V_SH1, V_SC1, V_GT1, V_SH2, V_SC2, V_GT2, V_GPRE1, V_GPOST1, V_GPRE2, V_GPOST2,
 V_PSCALE, V_CB, V_CW0, V_CW1, V_CW2) = range(15)
VD_ROWS = 16
FV_W0, FV_W1, FV_W2, FV_B = range(4)
FV_ROWS = 8

MESH = pl.DeviceIdType.MESH


def _params(sem=None, vmem=VMEM_LIMIT):
    return pltpu.CompilerParams(dimension_semantics=sem, vmem_limit_bytes=vmem)


def _row(ref, r):
    return ref[r:r + 1, :]


def _load_once(pairs, sem):
    @pl.when(pl.program_id(0) == 0)
    def _():
        copies = [pltpu.make_async_copy(src, dst, sem.at[n]) for n, (src, dst) in enumerate(pairs)]
        for cp in copies:
            cp.start()
        for cp in copies:
            cp.wait()


def _dot(a, b):
    return jnp.dot(a, b, preferred_element_type=F32)


def _dot_nt(a, b):
    return lax.dot_general(a, b, (((1,), (1,)), ((), ())), preferred_element_type=F32)


BLK = 256
SEG = BLK // 8


LANES = 128


def _load_rows(ref, scr, ts):
    nc = ref.shape[-1] // LANES
    for c in range(nc):
        scr[c] = ref[:, c * LANES:(c + 1) * LANES]
    rows = [jnp.concatenate([scr[c, pl.ds(b * BLK + j, 8, stride=SEG), :] for c in range(nc)], axis=1)
            for b in range(ts // BLK) for j in range(SEG)]
    return jnp.concatenate(rows, axis=0)


def _store_rows(ref, val, scr, ts):
    nc = ref.shape[-1] // LANES
    for c in range(nc):
        scr[c] = val[:, c * LANES:(c + 1) * LANES]
    for b in range(ts // BLK):
        for r in range(8):
            for q in range(SEG // 8):
                t = b * BLK + r * SEG + 8 * q
                ref[t:t + 8, :] = jnp.concatenate(
                    [scr[c, pl.ds(b * BLK + 64 * q + r, 8, stride=8), :] for c in range(nc)], axis=1)


def _times(t0):
    p = lax.broadcasted_iota(jnp.int32, (BLK, 1), 0)
    return t0 + (p & 7) * SEG + (p >> 3)


def _before(x, carry, s):
    x3 = x.reshape(SEG, 8, x.shape[-1])
    tail = pltpu.roll(x3[SEG - s:], 1, 1)
    row = lax.broadcasted_iota(jnp.int32, tail.shape, 1)
    out = jnp.concatenate([jnp.where(row == 0, carry, tail), x3[:SEG - s]], axis=0)
    return out.reshape(x.shape), tail


def _after(x, carry, s):
    x3 = x.reshape(SEG, 8, x.shape[-1])
    head = pltpu.roll(x3[:s], 7, 1)
    row = lax.broadcasted_iota(jnp.int32, head.shape, 1)
    out = jnp.concatenate([x3[s:], jnp.where(row == 7, carry, head)], axis=0)
    return out.reshape(x.shape), head


def _causal_conv(x, carry, cols, w0, w1, w2, b):
    x1, carry[0:1, :, cols] = _before(x, carry[0:1, :, cols], 1)
    x2, carry[1:3, :, cols] = _before(x, carry[1:3, :, cols], 2)
    return b + w2 * x + w1 * x1 + w0 * x2


def _causal_conv_bwd(dy, carry, cols, w0, w1, w2):
    d1, carry[0:1, :, cols] = _after(dy, carry[0:1, :, cols], 1)
    d2, carry[1:3, :, cols] = _after(dy, carry[1:3, :, cols], 2)
    return w2 * dy + w1 * d1 + w0 * d2, d1, d2


def _pool_counts(t0, g):
    return jnp.minimum((_times(t0) + 1).astype(F32), float(2 << g))


def _rms(x):
    return lax.rsqrt(jnp.mean(x * x, axis=-1, keepdims=True) + EPS)


def _rms_bwd(dn, n, r):
    return r * (dn - n * jnp.mean(dn * n, axis=-1, keepdims=True))


def _colsum(x):
    return jnp.sum(x, axis=0, keepdims=True)


def _gelu_and_grad(x):
    k = 0.7978845608028654
    inner = k * (x + 0.044715 * (x * x * x))
    th = jnp.tanh(inner)
    gelu = 0.5 * x * (1.0 + th)
    dgelu = 0.5 * (1.0 + th) + 0.5 * x * (1.0 - th * th) * (k * (1.0 + 3.0 * 0.044715 * (x * x)))
    return gelu, dgelu


def _fwd_proj(x, vec_d, w_in, placed_rest, ts):
    s = x.shape[0]
    cw = DIN // NCHIP
    gather = _WeightGather(SHARDED[1:])
    n = gather.n

    def body(*refs):
        x_ref, v_ref, w_hbm = refs[:3]
        proj_ref, h1_ref, xs_ref = refs[3 + n:6 + n]
        rest = refs[6 + n:6 + 2 * n]
        w_vmem, rowbuf, sem, send_sems, recv_sems = refs[6 + 2 * n:]
        i = pl.program_id(0)
        pl.when(i == 0)(lambda: gather.start(rest, send_sems, recv_sems))
        compute(x_ref, v_ref, w_hbm, proj_ref, h1_ref, xs_ref, w_vmem, rowbuf, sem)
        pl.when(i == s // ts - 1)(lambda: gather.finish(rest, send_sems, recv_sems))

    def compute(x_ref, v_ref, w_hbm, proj_ref, h1_ref, xs_ref, w_vmem, rowbuf, sem):
        _load_once([(w_hbm, w_vmem)], sem)
        xv = _load_rows(x_ref, rowbuf, ts)
        xs_ref[...] = xv
        n1 = xv * _rms(xv)
        h = n1 * (_row(v_ref, V_GPRE1) * (1.0 + _row(v_ref, V_SC1))) + _row(v_ref, V_SH1)
        hb = h.astype(BF16)
        h1_ref[...] = hb
        for k in range(NCHIP):
            cols = slice(k * cw, (k + 1) * cw)
            proj_ref[:, cols] = _dot(hb, w_vmem[:, cols]).astype(BF16)

    return pl.pallas_call(
        body, name="fwd_proj", grid=(s // ts,),
        in_specs=[pl.BlockSpec((ts, D), lambda i: (i, 0)),
                  pl.BlockSpec((VD_ROWS, D), lambda i: (0, 0)),
                  pl.BlockSpec(memory_space=pl.ANY)] + gather.specs_any,
        out_specs=[pl.BlockSpec((ts, DIN), lambda i: (i, 0)), pl.BlockSpec((ts, D), lambda i: (i, 0)),
                   pl.BlockSpec((ts, D), lambda i: (i, 0))] + gather.specs_any,
        out_shape=[jax.ShapeDtypeStruct((s, DIN), BF16), jax.ShapeDtypeStruct((s, D), BF16),
                   jax.ShapeDtypeStruct((s, D), F32)] + gather.out_shape,
        input_output_aliases={3 + w: 3 + w for w in range(n)},
        scratch_shapes=[pltpu.VMEM((D, DIN), BF16), pltpu.VMEM((D // LANES, ts, LANES), F32),
                        pltpu.SemaphoreType.DMA((1,))] + gather.scratch,
        compiler_params=_params(("arbitrary",)),
    )(x, vec_d, w_in, *placed_rest)


def _fwd_mix(proj, x, vec_d, w_pool, w_bout, w_o, ts):
    s = x.shape[0]

    def body(p_ref, x_ref, v_ref, wp_hbm, wb_hbm, wo_hbm,
             x1_ref, o_ref, pg_ref, q_ref, mg_ref, ya_ref, yb_ref, cv_ref,
             wp, wb, wo, carry_p, carry_v, sem):
        i = pl.program_id(0)
        _load_once([(wp_hbm, wp), (wb_hbm, wb), (wo_hbm, wo)], sem)

        @pl.when(i == 0)
        def _():
            carry_p[...] = jnp.zeros_like(carry_p)
            carry_v[...] = jnp.zeros_like(carry_v)

        t0 = i * ts
        for g in range(NG):
            cols = slice(g * GW, (g + 1) * GW)
            u = p_ref[:, cols].astype(F32)
            e = u
            for l in range(g + 1):
                slot = slice((1 << l) - 1, (2 << l) - 1)
                shifted, carry_p[slot, :, cols] = _before(e, carry_p[slot, :, cols], 1 << l)
                e = e + shifted
            pgb = (e / _pool_counts(t0, g) - u).astype(BF16)
            pg_ref[:, cols] = pgb
            ya_ref[:, cols] = _dot(pgb, wp[g]).astype(BF16)

        u_x = p_ref[:, D:2 * D].astype(F32)
        u_c = p_ref[:, 3 * D:4 * D].astype(F32)
        v = u_c * u_x
        cv = _causal_conv(v, carry_v, slice(None), _row(v_ref, V_CW0), _row(v_ref, V_CW1),
                          _row(v_ref, V_CW2), _row(v_ref, V_CB))
        cv_ref[...] = cv.astype(BF16)
        q = (p_ref[:, 2 * D:3 * D].astype(F32) * cv).astype(BF16)
        q_ref[...] = q
        y_b = _dot(q, wb[...])
        yb_ref[...] = y_b.astype(BF16)

        y_a = ya_ref[...].astype(F32) * _row(v_ref, V_PSCALE)
        merged = (jax.nn.sigmoid(p_ref[:, 4 * D:5 * D].astype(F32)) * y_a
                  + jax.nn.sigmoid(p_ref[:, 5 * D:6 * D].astype(F32)) * y_b).astype(BF16)
        mg_ref[...] = merged
        o = _dot(merged, wo[...])
        o_ref[...] = o
        x1_ref[...] = x_ref[...] + _row(v_ref, V_GT1) * ((o * _rms(o)) * _row(v_ref, V_GPOST1))

    tile = lambda w: pl.BlockSpec((ts, w), lambda i: (i, 0))
    hbm = pl.BlockSpec(memory_space=pl.ANY)
    return pl.pallas_call(
        body, name="fwd_mix", grid=(s // ts,),
        in_specs=[tile(DIN), tile(D), pl.BlockSpec((VD_ROWS, D), lambda i: (0, 0)), hbm, hbm, hbm],
        out_specs=[tile(D)] * 8,
        out_shape=[jax.ShapeDtypeStruct((s, D), F32), jax.ShapeDtypeStruct((s, D), F32)]
        + [jax.ShapeDtypeStruct((s, D), BF16)] * 6,
        scratch_shapes=[pltpu.VMEM((NG, GW, GW), BF16), pltpu.VMEM((D, D), BF16), pltpu.VMEM((D, D), BF16),
                        pltpu.VMEM((POOL_CARRY, 8, D), F32), pltpu.VMEM((CONV_CARRY, 8, D), F32),
                        pltpu.SemaphoreType.DMA((3,))],
        compiler_params=_params(("arbitrary",)),
    )(proj, x, vec_d, w_pool, w_bout, w_o)


def _fwd_ffn(x1, tgt, vec_d, vec_f, w_up, w_down, ts):
    s = x1.shape[0]
    hw = F // 2

    def body(x1_ref, t_ref, v_ref, f_ref, wu_hbm, wd_hbm,
             up_ref, upc_ref, a_ref, h2_ref, dx2_ref, dff_ref, vo_ref, loss_ref,
             wu, wd, carry, rowbuf, sem):
        i = pl.program_id(0)
        _load_once([(wu_hbm, wu), (wd_hbm, wd)], sem)

        @pl.when(i == 0)
        def _():
            carry[...] = jnp.zeros_like(carry)
            vo_ref[...] = jnp.zeros_like(vo_ref)
            loss_ref[...] = jnp.zeros_like(loss_ref)

        x1v = x1_ref[...]
        n3 = x1v * _rms(x1v)
        h2 = (n3 * (_row(v_ref, V_GPRE2) * (1.0 + _row(v_ref, V_SC2))) + _row(v_ref, V_SH2)).astype(BF16)
        h2_ref[...] = h2

        ff = jnp.zeros((ts, D), F32)
        for p in range(2):
            up = []
            for cols in (slice(p * hw, (p + 1) * hw), slice(F + p * hw, F + (p + 1) * hw)):
                u0 = _dot(h2, wu[:, cols])
                up_ref[:, cols] = u0.astype(BF16)
                y = _causal_conv(u0, carry, cols, f_ref[FV_W0:FV_W0 + 1, cols], f_ref[FV_W1:FV_W1 + 1, cols],
                                 f_ref[FV_W2:FV_W2 + 1, cols], f_ref[FV_B:FV_B + 1, cols])
                upc_ref[:, cols] = y.astype(BF16)
                up.append(y)
            gelu, _ = _gelu_and_grad(up[0])
            a = (gelu * up[1]).astype(BF16)
            a_ref[:, p * hw:(p + 1) * hw] = a
            ff = ff + _dot(a, wd[p * hw:(p + 1) * hw, :])

        r4 = _rms(ff)
        n4 = ff * r4
        gt2 = _row(v_ref, V_GT2)
        gpost = _row(v_ref, V_GPOST2)
        y4 = n4 * gpost
        diff = (x1v + gt2 * y4) - _load_rows(t_ref, rowbuf, ts)
        loss_ref[...] += jnp.full(loss_ref.shape, 0.5 / D * jnp.sum(diff * diff), F32)
        dx2 = diff * (1.0 / D)
        dx2_ref[...] = dx2
        dy4 = dx2 * gt2
        vo_ref[0:1, :] += _colsum(dx2 * y4)
        vo_ref[1:2, :] += _colsum(dy4 * n4)
        dff_ref[...] = _rms_bwd(dy4 * gpost, n4, r4).astype(BF16)

    tile = lambda w: pl.BlockSpec((ts, w), lambda i: (i, 0))
    full = lambda r, w: pl.BlockSpec((r, w), lambda i: (0, 0))
    hbm = pl.BlockSpec(memory_space=pl.ANY)
    return pl.pallas_call(
        body, name="fwd_ffn", grid=(s // ts,),
        in_specs=[tile(D), tile(D), full(VD_ROWS, D), full(FV_ROWS, F2), hbm, hbm],
        out_specs=[tile(F2), tile(F2), tile(F), tile(D), tile(D), tile(D), full(8, D), full(8, 128)],
        out_shape=[jax.ShapeDtypeStruct((s, F2), BF16), jax.ShapeDtypeStruct((s, F2), BF16),
                   jax.ShapeDtypeStruct((s, F), BF16),
                   jax.ShapeDtypeStruct((s, D), BF16), jax.ShapeDtypeStruct((s, D), F32),
                   jax.ShapeDtypeStruct((s, D), BF16), jax.ShapeDtypeStruct((8, D), F32),
                   jax.ShapeDtypeStruct((8, 128), F32)],
        scratch_shapes=[pltpu.VMEM((D, F2), BF16), pltpu.VMEM((F, D), BF16), pltpu.VMEM((CONV_CARRY, 8, F2), F32),
                        pltpu.VMEM((D // LANES, ts, LANES), F32), pltpu.SemaphoreType.DMA((2,))],
        compiler_params=_params(("arbitrary",)),
    )(x1, tgt, vec_d, vec_f, w_up, w_down)


def _bwd_ffn(dff, dx2, x1, up0, upc, vec_d, vec_f, w_up, w_down, exchange, ex_grads, ts):
    s = x1.shape[0]
    nt = s // ts
    hw = F // 2
    n = exchange.n

    def body(*refs):
        ins, grads = refs[:9], refs[9:9 + n]
        outs, recvs = refs[9 + n:13 + n], refs[13 + n:13 + 2 * n]
        scratch, sems = refs[13 + 2 * n:-2], refs[-2:]
        i = pl.program_id(0)
        pl.when(i == 0)(lambda: exchange.start(grads, recvs, *sems))
        compute(*ins, *outs, *scratch)
        pl.when(i == nt - 1)(lambda: exchange.finish(grads, recvs, *sems))

    def compute(dff_ref, dx2_ref, x1_ref, up_ref, upc_ref, v_ref, f_ref, wu_hbm, wd_hbm,
                dx1_ref, dup_ref, vo_ref, fo_ref, wu, wd, carry, sem):
        i = pl.program_id(0)
        _load_once([(wu_hbm, wu), (wd_hbm, wd)], sem)

        @pl.when(i == 0)
        def _():
            carry[...] = jnp.zeros_like(carry)
            vo_ref[...] = jnp.zeros_like(vo_ref)
            fo_ref[...] = jnp.zeros_like(fo_ref)

        dffb = dff_ref[...]
        dh2 = jnp.zeros((ts, D), F32)
        for p in range(2):
            slabs = (slice(p * hw, (p + 1) * hw), slice(F + p * hw, F + (p + 1) * hw))
            gelu, dgelu = _gelu_and_grad(upc_ref[:, slabs[0]].astype(F32))
            da = _dot_nt(dffb, wd[p * hw:(p + 1) * hw, :])
            dups = (da * upc_ref[:, slabs[1]].astype(F32) * dgelu, da * gelu)
            for cols, dup in zip(slabs, dups):
                du0, d1, d2 = _causal_conv_bwd(dup, carry, cols, f_ref[FV_W0:FV_W0 + 1, cols],
                                               f_ref[FV_W1:FV_W1 + 1, cols], f_ref[FV_W2:FV_W2 + 1, cols])
                u0 = up_ref[:, cols].astype(F32)
                fo_ref[FV_B:FV_B + 1, cols] += _colsum(dup)
                fo_ref[FV_W2:FV_W2 + 1, cols] += _colsum(dup * u0)
                fo_ref[FV_W1:FV_W1 + 1, cols] += _colsum(d1 * u0)
                fo_ref[FV_W0:FV_W0 + 1, cols] += _colsum(d2 * u0)
                du0 = du0.astype(BF16)
                dup_ref[:, cols] = du0
                dh2 = dh2 + _dot_nt(du0, wu[:, cols])

        x1v = x1_ref[...]
        r3 = _rms(x1v)
        n3 = x1v * r3
        gpre = _row(v_ref, V_GPRE2)
        sc = 1.0 + _row(v_ref, V_SC2)
        vo_ref[0:1, :] += _colsum(dh2)
        vo_ref[1:2, :] += _colsum(dh2 * n3 * gpre)
        vo_ref[2:3, :] += _colsum(dh2 * n3 * sc)
        dx1_ref[...] = dx2_ref[...] + _rms_bwd(dh2 * (gpre * sc), n3, r3)

    rev = lambda w: pl.BlockSpec((ts, w), lambda i: (nt - 1 - i, 0))
    full = lambda r, w: pl.BlockSpec((r, w), lambda i: (0, 0))
    hbm = pl.BlockSpec(memory_space=pl.ANY)
    return pl.pallas_call(
        body, name="bwd_ffn", grid=(nt,),
        in_specs=[rev(D), rev(D), rev(D), rev(F2), rev(F2), full(VD_ROWS, D), full(FV_ROWS, F2), hbm, hbm]
        + exchange.specs_any,
        out_specs=[rev(D), rev(F2), full(8, D), full(FV_ROWS, F2)] + exchange.specs_any,
        out_shape=[jax.ShapeDtypeStruct((s, D), F32), jax.ShapeDtypeStruct((s, F2), BF16),
                   jax.ShapeDtypeStruct((8, D), F32), jax.ShapeDtypeStruct((FV_ROWS, F2), F32)] + exchange.out_shape,
        scratch_shapes=[pltpu.VMEM((D, F2), BF16), pltpu.VMEM((F, D), BF16), pltpu.VMEM((CONV_CARRY, 8, F2), F32),
                        pltpu.SemaphoreType.DMA((2,))] + exchange.scratch,
        compiler_params=_params(("arbitrary",)),
    )(dff, dx2, x1, up0, upc, vec_d, vec_f, w_up, w_down, *ex_grads)


def _bwd_mix(dx1, o, proj, cv, ya0, yb, vec_d, w_pool, w_bout, w_o, exchange, ex_grads, ts):
    s = dx1.shape[0]
    nt = s // ts
    n = exchange.n

    def body(*refs):
        ins, grads = refs[:10], refs[10:10 + n]
        outs, recvs = refs[10 + n:15 + n], refs[15 + n:15 + 2 * n]
        scratch, sems = refs[15 + 2 * n:-2], refs[-2:]
        i = pl.program_id(0)
        pl.when(i == 0)(lambda: exchange.start(grads, recvs, *sems))
        compute(*ins, *outs, *scratch)
        pl.when(i == nt - 1)(lambda: exchange.finish(grads, recvs, *sems))

    def compute(dx1_ref, o_ref, p_ref, cv_ref, ya_ref, yb_ref, v_ref, wp_hbm, wb_hbm, wo_hbm,
                dp_ref, do_ref, dyb_ref, dya_ref, vo_ref, wp, wb, wo, carry_d, carry_e, sem):
        i = pl.program_id(0)
        _load_once([(wp_hbm, wp), (wb_hbm, wb), (wo_hbm, wo)], sem)

        @pl.when(i == 0)
        def _():
            carry_d[...] = jnp.zeros_like(carry_d)
            carry_e[...] = jnp.zeros_like(carry_e)
            vo_ref[...] = jnp.zeros_like(vo_ref)

        t0 = (nt - 1 - i) * ts
        dx1v = dx1_ref[...]
        ov = o_ref[...]
        r2 = _rms(ov)
        n2 = ov * r2
        gpost = _row(v_ref, V_GPOST1)
        vo_ref[0:1, :] += _colsum(dx1v * (n2 * gpost))
        dy2 = dx1v * _row(v_ref, V_GT1)
        vo_ref[1:2, :] += _colsum(dy2 * n2)
        dob = _rms_bwd(dy2 * gpost, n2, r2).astype(BF16)
        do_ref[...] = dob
        dmerged = _dot_nt(dob, wo[...])

        ya0 = ya_ref[...].astype(F32)
        pscale = _row(v_ref, V_PSCALE)
        sa = jax.nn.sigmoid(p_ref[:, 4 * D:5 * D].astype(F32))
        dp_ref[:, 4 * D:5 * D] = (dmerged * (ya0 * pscale) * sa * (1.0 - sa)).astype(BF16)
        dy_a = dmerged * sa
        vo_ref[2:3, :] += _colsum(dy_a * ya0)
        dya0 = (dy_a * pscale).astype(BF16)
        dya_ref[...] = dya0

        sb = jax.nn.sigmoid(p_ref[:, 5 * D:6 * D].astype(F32))
        dp_ref[:, 5 * D:6 * D] = (dmerged * yb_ref[...].astype(F32) * sb * (1.0 - sb)).astype(BF16)
        dy_b = (dmerged * sb).astype(BF16)
        dyb_ref[...] = dy_b
        dq = _dot_nt(dy_b, wb[...])

        u_x = p_ref[:, D:2 * D].astype(F32)
        u_b = p_ref[:, 2 * D:3 * D].astype(F32)
        u_c = p_ref[:, 3 * D:4 * D].astype(F32)
        w0, w1, w2 = _row(v_ref, V_CW0), _row(v_ref, V_CW1), _row(v_ref, V_CW2)
        dp_ref[:, 2 * D:3 * D] = (dq * cv_ref[...].astype(F32)).astype(BF16)
        dcv = dq * u_b
        dv, d1, d2 = _causal_conv_bwd(dcv, carry_d, slice(None), w0, w1, w2)
        v = u_c * u_x
        vo_ref[3:4, :] += _colsum(dcv)
        vo_ref[4:5, :] += _colsum(d2 * v)
        vo_ref[5:6, :] += _colsum(d1 * v)
        vo_ref[6:7, :] += _colsum(dcv * v)
        dp_ref[:, D:2 * D] = (dv * u_c).astype(BF16)
        dp_ref[:, 3 * D:4 * D] = (dv * u_x).astype(BF16)

        for g in range(NG):
            cols = slice(g * GW, (g + 1) * GW)
            dpg = _dot_nt(dya0[:, cols], wp[g])
            e = dpg / _pool_counts(t0, g)
            for l in range(g + 1):
                slot = slice((1 << l) - 1, (2 << l) - 1)
                shifted, carry_e[slot, :, cols] = _after(e, carry_e[slot, :, cols], 1 << l)
                e = e + shifted
            dp_ref[:, cols] = (e - dpg).astype(BF16)

    rev = lambda w: pl.BlockSpec((ts, w), lambda i: (nt - 1 - i, 0))
    hbm = pl.BlockSpec(memory_space=pl.ANY)
    return pl.pallas_call(
        body, name="bwd_mix", grid=(nt,),
        in_specs=[rev(D), rev(D), rev(DIN), rev(D), rev(D), rev(D), pl.BlockSpec((VD_ROWS, D), lambda i: (0, 0)),
                  hbm, hbm, hbm] + exchange.specs_any,
        out_specs=[rev(DIN), rev(D), rev(D), rev(D), pl.BlockSpec((8, D), lambda i: (0, 0))] + exchange.specs_any,
        out_shape=[jax.ShapeDtypeStruct((s, DIN), BF16)] + [jax.ShapeDtypeStruct((s, D), BF16)] * 3
        + [jax.ShapeDtypeStruct((8, D), F32)] + exchange.out_shape,
        scratch_shapes=[pltpu.VMEM((NG, GW, GW), BF16), pltpu.VMEM((D, D), BF16), pltpu.VMEM((D, D), BF16),
                        pltpu.VMEM((CONV_CARRY, 8, D), F32), pltpu.VMEM((POOL_CARRY, 8, D), F32),
                        pltpu.SemaphoreType.DMA((3,))] + exchange.scratch,
        compiler_params=_params(("arbitrary",)),
    )(dx1, o, proj, cv, ya0, yb, vec_d, w_pool, w_bout, w_o, *ex_grads)


def _bwd_in(dproj, dx1, x, vec_d, w_in, ts):
    s = x.shape[0]

    def body(dp_ref, dx1_ref, x_ref, v_ref, w_hbm, dx_ref, vo_ref, w_vmem, rowbuf, sem):
        _load_once([(w_hbm, w_vmem)], sem)

        @pl.when(pl.program_id(0) == 0)
        def _():
            vo_ref[...] = jnp.zeros_like(vo_ref)

        dh1 = _dot_nt(dp_ref[...], w_vmem[...])
        xv = x_ref[...]
        r1 = _rms(xv)
        n1 = xv * r1
        gpre = _row(v_ref, V_GPRE1)
        sc = 1.0 + _row(v_ref, V_SC1)
        vo_ref[0:1, :] += _colsum(dh1)
        vo_ref[1:2, :] += _colsum(dh1 * n1 * gpre)
        vo_ref[2:3, :] += _colsum(dh1 * n1 * sc)
        _store_rows(dx_ref, dx1_ref[...] + _rms_bwd(dh1 * (gpre * sc), n1, r1), rowbuf, ts)

    tile = lambda w: pl.BlockSpec((ts, w), lambda i: (i, 0))
    return pl.pallas_call(
        body, name="bwd_in", grid=(s // ts,),
        in_specs=[tile(DIN), tile(D), tile(D), pl.BlockSpec((VD_ROWS, D), lambda i: (0, 0)),
                  pl.BlockSpec(memory_space=pl.ANY)],
        out_specs=[tile(D), pl.BlockSpec((8, D), lambda i: (0, 0))],
        out_shape=[jax.ShapeDtypeStruct((s, D), F32), jax.ShapeDtypeStruct((8, D), F32)],
        scratch_shapes=[pltpu.VMEM((D, DIN), BF16), pltpu.VMEM((D // LANES, ts, LANES), F32),
                        pltpu.SemaphoreType.DMA((1,))],
        compiler_params=_params(("arbitrary",)),
    )(dproj, dx1, x, vec_d, w_in)


def _dot_tn(a, b):
    return lax.dot_general(a, b, (((0,), (0,)), ((), ())), preferred_element_type=F32)


def _wgrad(a, b, tm, tn, ts, name, dtype, exchange=None, ex_grads=()):
    s, m = a.shape
    nn = b.shape[1]
    grid = (m // tm, nn // tn, s // ts)
    n = exchange.n if exchange else 0

    def body(*refs):
        a_ref, b_ref = refs[:2]
        grads = refs[2:2 + n]
        o_ref = refs[2 + n]
        recvs = refs[3 + n:3 + 2 * n]
        acc = refs[3 + 2 * n]
        sems = refs[4 + 2 * n:]
        i, j, k = pl.program_id(0), pl.program_id(1), pl.program_id(2)
        if exchange:
            pl.when((i == 0) & (j == 0) & (k == 0))(lambda: exchange.start(grads, recvs, *sems))
        part = _dot_tn(a_ref[...], b_ref[...])

        @pl.when(k == 0)
        def _():
            acc[...] = part

        @pl.when(k > 0)
        def _():
            acc[...] += part

        @pl.when(k == grid[2] - 1)
        def _():
            o_ref[...] = acc[...].astype(dtype)

        if exchange:
            pl.when((i == grid[0] - 1) & (j == grid[1] - 1) & (k == grid[2] - 1))(
                lambda: exchange.finish(grads, recvs, *sems))

    hosted = exchange.specs_any if exchange else []
    return pl.pallas_call(
        body, name=name, grid=grid,
        in_specs=[pl.BlockSpec((ts, tm), lambda i, j, k: (k, i)), pl.BlockSpec((ts, tn), lambda i, j, k: (k, j))]
        + hosted,
        out_specs=[pl.BlockSpec((tm, tn), lambda i, j, k: (i, j))] + hosted,
        out_shape=[jax.ShapeDtypeStruct((m, nn), dtype)] + (exchange.out_shape if exchange else []),
        scratch_shapes=[pltpu.VMEM((tm, tn), F32)] + (exchange.scratch if exchange else []),
        compiler_params=_params(("arbitrary", "arbitrary", "arbitrary")),
    )(a, b, *ex_grads)


def _wgrad_pool(pg, dya0, ts):
    s = pg.shape[0]
    nk = s // ts

    def body(a_ref, b_ref, o_ref, acc):
        k = pl.program_id(1)
        part = _dot_tn(a_ref[...], b_ref[...])

        @pl.when(k == 0)
        def _():
            acc[...] = part

        @pl.when(k > 0)
        def _():
            acc[...] += part

        @pl.when(k == nk - 1)
        def _():
            o_ref[0] = acc[...].astype(BF16)

    return pl.pallas_call(
        body, name="wgrad_pool", grid=(NG, nk),
        in_specs=[pl.BlockSpec((ts, GW), lambda g, k: (k, g)), pl.BlockSpec((ts, GW), lambda g, k: (k, g))],
        out_specs=pl.BlockSpec((1, GW, GW), lambda g, k: (g, 0, 0)),
        out_shape=jax.ShapeDtypeStruct((NG, GW, GW), BF16),
        scratch_shapes=[pltpu.VMEM((GW, GW), F32)],
        compiler_params=_params(("arbitrary", "arbitrary")),
    )(pg, dya0)


TS_PROJ = 512
TS_MIX = 256
TS_FFN = 256
TS_WGRAD = 1024


def _local_step(x, tgt, vec_d, vec_f, w_in, placed_rest):
    s = x.shape[0]
    tw = min(TS_WGRAD, s)
    sp_in, sp_pool, sp_bout, sp_o, sp_up, sp_down = SHARDED
    proj, h1, xs, w_pool, w_bout, w_o, w_up, w_down = _fwd_proj(x, vec_d, w_in, placed_rest, min(TS_PROJ, s))
    x1, o, pg, q, merged, ya0, yb, cv = _fwd_mix(proj, xs, vec_d, w_pool, w_bout, w_o, min(TS_MIX, s))
    up0, upc, a, h2, dx2, dff, vo_f, loss = _fwd_ffn(x1, tgt, vec_d, vec_f, w_up, w_down, min(TS_FFN, s))
    g_down, = _wgrad(a, dff, F // 2, D, tw, "wgrad_down", BF16)
    dx1, dup0, vo_b, fo, r_down = _bwd_ffn(dff, dx2, x1, up0, upc, vec_d, vec_f, w_up, w_down,
                                           _GradExchange([sp_down]), [g_down], min(TS_FFN, s))
    g_up, = _wgrad(h2, dup0, D, F2 // NCHIP, tw, "wgrad_up", BF16)
    dproj, do, dyb, dya0, vo_m, r_up = _bwd_mix(dx1, o, proj, cv, ya0, yb, vec_d, w_pool, w_bout, w_o,
                                                _GradExchange([sp_up]), [g_up], min(TS_MIX, s))
    dx, vo_i = _bwd_in(dproj, dx1, xs, vec_d, w_in, min(TS_PROJ, s))
    g_o, = _wgrad(merged, do, D, D, tw, "wgrad_o", BF16)
    g_bout, = _wgrad(q, dyb, D, D, tw, "wgrad_bout", BF16)
    g_pool = _wgrad_pool(pg, dya0, tw)
    g_in, r_pool, r_bout, r_o = _wgrad(h1, dproj, D, DIN // NCHIP, tw, "wgrad_in", F32,
                                       _GradExchange([sp_pool, sp_bout, sp_o]), [g_pool, g_bout, g_o])
    vecs = dict(
        dsh1=vo_i[0], dsc1=vo_i[1], dg_pre_mix=vo_i[2],
        dgt1=vo_m[0], dg_post_mix=vo_m[1], dpool_scale=vo_m[2], dconv_b=vo_m[3],
        dconv_w=vo_m[4:7],
        dsh2=vo_b[0], dsc2=vo_b[1], dg_pre_ffn=vo_b[2],
        dgt2=vo_f[0], dg_post_ffn=vo_f[1],
        dffn_conv_w=fo[FV_W0:FV_W2 + 1], dffn_conv_b=fo[FV_B],
    )
    local = dict(w_pool=g_pool, w_bout=g_bout, w_o=g_o, w_up=g_up, w_down=g_down)
    received = dict(w_pool=r_pool, w_bout=r_bout, w_o=r_o, w_up=r_up, w_down=r_down)
    return loss, dx, vecs, g_in, local, received


def _aligned(offset, n):
    return offset if isinstance(offset, int) else pl.multiple_of(offset, n)


class _Sharded:
    def __init__(self, name, full_shape, shard_axis, half_axis):
        self.name = name
        self.full_shape = full_shape
        self.shard_axis = shard_axis
        self.half_axis = half_axis
        self.shard_shape = tuple(n // NCHIP if a == shard_axis else n for a, n in enumerate(full_shape))
        self.piece_shape = tuple(n // 2 if a == half_axis else n for a, n in enumerate(self.shard_shape))

    def piece(self, full_ref, k, h):
        idx = []
        for a, n in enumerate(self.piece_shape):
            if a == self.shard_axis and a == self.half_axis:
                idx.append(pl.ds(_aligned((2 * k + h) * n, n), n))
            elif a == self.shard_axis:
                idx.append(pl.ds(_aligned(k * n, n), n))
            elif a == self.half_axis:
                idx.append(pl.ds(_aligned(h * n, n), n))
            else:
                idx.append(slice(None))
        return full_ref.at[tuple(idx)]

    def shard(self, full_ref, k):
        n = self.shard_shape[self.shard_axis]
        idx = [pl.ds(_aligned(k * n, n), n) if a == self.shard_axis else slice(None)
               for a in range(len(self.full_shape))]
        return full_ref.at[tuple(idx)]

    def half(self, shard_ref, h):
        n = self.piece_shape[self.half_axis]
        idx = [pl.ds(_aligned(h * n, n), n) if a == self.half_axis else slice(None)
               for a in range(len(self.full_shape))]
        return shard_ref.at[tuple(idx)]

    def piece_block(self):
        def index_map(k, c_ref):
            c = c_ref[0]
            out = []
            for a in range(len(self.full_shape)):
                if a == self.shard_axis and a == self.half_axis:
                    out.append(2 * k + c)
                elif a == self.shard_axis:
                    out.append(k)
                elif a == self.half_axis:
                    out.append(c)
                else:
                    out.append(0)
            return tuple(out)
        return pl.BlockSpec(self.piece_shape, index_map)


SHARDED = (
    _Sharded("w_in", (D, DIN), 1, 0),
    _Sharded("w_pool", (NG, GW, GW), 1, 0),
    _Sharded("w_bout", (D, D), 0, 0),
    _Sharded("w_o", (D, D), 0, 0),
    _Sharded("w_up", (D, F2), 1, 0),
    _Sharded("w_down", (F, D), 0, 0),
)
NW = len(SHARDED)


def _mesh_place():
    x, y, c = lax.axis_index("x"), lax.axis_index("y"), lax.axis_index("c")
    chips = [(1 - x, y), (x, 1 - y), (1 - x, 1 - y)]
    return x, y, c, 2 * x + y, chips, [2 * px + py for px, py in chips]


def _remote(src, dst, send_sem, recv_sem, device):
    return pltpu.make_async_remote_copy(src_ref=src, dst_ref=dst, send_sem=send_sem, recv_sem=recv_sem,
                                        device_id=device, device_id_type=MESH)


def _all_gather_small(block, name):
    m_per, n = block.shape

    def body(x_ref, out_ref, send_sems, recv_sems, local_sem):
        x, y, c, _, chips, _ = _mesh_place()
        me, sibling = (x, y, c), (x, y, 1 - c)

        def rows(px, py, pc):
            return out_ref.at[pl.ds((4 * px + 2 * py + pc) * m_per, m_per), :]

        def copy(k, blk, to, src=None):
            return _remote(rows(*blk) if src is None else src, rows(*blk), send_sems.at[k], recv_sems.at[k], to)

        mine = pltpu.make_async_copy(x_ref, rows(*me), local_sem)
        mine.start()
        first = [copy(0, me, sibling, src=x_ref)]
        first += [copy(1 + j, me, (*chip, c), src=x_ref) for j, chip in enumerate(chips)]
        for cp in first:
            cp.start()
        passed = [copy(4 + j, (*chip, c), sibling) for j, chip in enumerate(chips)]
        for j, chip in enumerate(chips):
            copy(1 + j, (*chip, c), me).wait_recv()
            passed[j].start()
        copy(0, sibling, me).wait_recv()
        for j, chip in enumerate(chips):
            copy(4 + j, (*chip, 1 - c), me).wait_recv()
        for cp in first + passed:
            cp.wait_send()
        mine.wait()

    return pl.pallas_call(
        body, name=name,
        out_shape=jax.ShapeDtypeStruct((NDEV * m_per, n), block.dtype),
        in_specs=[pl.BlockSpec(memory_space=pltpu.VMEM)],
        out_specs=pl.BlockSpec(memory_space=pltpu.VMEM),
        scratch_shapes=[pltpu.SemaphoreType.DMA((7,)), pltpu.SemaphoreType.DMA((7,)), pltpu.SemaphoreType.DMA],
        compiler_params=pltpu.CompilerParams(vmem_limit_bytes=VMEM_LIMIT),
    )(block)


class _WeightGather:
    def __init__(self, specs):
        self.specs = specs
        self.n = len(specs)
        self.specs_any = [pl.BlockSpec(memory_space=pl.ANY)] * self.n
        self.out_shape = [jax.ShapeDtypeStruct(sp.full_shape, BF16) for sp in specs]
        self.scratch = [pltpu.SemaphoreType.DMA((6 * self.n,)), pltpu.SemaphoreType.DMA((6 * self.n,))]

    def _sends(self, outs, send_sems, recv_sems):
        x, y, c, k_me, chips, _ = _mesh_place()
        sends = []
        for j, chip in enumerate(chips):
            for w, sp in enumerate(self.specs):
                mine = sp.piece(outs[w], k_me, c)
                sends.append(_remote(mine, mine, send_sems.at[6 * w + j], recv_sems.at[6 * w + j], (*chip, c)))
        return sends

    def start(self, outs, send_sems, recv_sems):
        for cp in self._sends(outs, send_sems, recv_sems):
            cp.start()

    def finish(self, outs, send_sems, recv_sems):
        x, y, c, _, chips, kidx = _mesh_place()
        sibling = (x, y, 1 - c)
        passed = []
        for j, chip in enumerate(chips):
            for w, sp in enumerate(self.specs):
                landed = sp.piece(outs[w], kidx[j], c)
                _remote(landed, landed, send_sems.at[6 * w + j], recv_sems.at[6 * w + j], (*chip, c)).wait_recv()
                cp = _remote(landed, landed, send_sems.at[6 * w + 3 + j], recv_sems.at[6 * w + 3 + j], sibling)
                cp.start()
                passed.append(cp)
        for j in range(3):
            for w, sp in enumerate(self.specs):
                landed = sp.piece(outs[w], kidx[j], 1 - c)
                _remote(landed, landed, send_sems.at[6 * w + 3 + j], recv_sems.at[6 * w + 3 + j], sibling).wait_recv()
        for cp in self._sends(outs, send_sems, recv_sems) + passed:
            cp.wait_send()


def _gather_weights(placed, specs, name):
    gather = _WeightGather(specs)
    n = gather.n

    def body(*refs):
        outs, sems = refs[n:2 * n], refs[2 * n:]
        gather.start(outs, *sems)
        gather.finish(outs, *sems)

    return pl.pallas_call(
        body, name=name, out_shape=gather.out_shape, in_specs=gather.specs_any, out_specs=gather.specs_any,
        input_output_aliases={w: w for w in range(n)}, scratch_shapes=gather.scratch,
    )(*placed)


class _GradExchange:
    def __init__(self, specs):
        self.specs = specs
        self.n = len(specs)
        self.specs_any = [pl.BlockSpec(memory_space=pl.ANY)] * self.n
        self.out_shape = [jax.ShapeDtypeStruct((NDEV,) + sp.piece_shape, BF16) for sp in specs]
        self.scratch = [pltpu.SemaphoreType.DMA((7 * self.n,)), pltpu.SemaphoreType.DMA((NDEV * self.n,))]

    def _sends(self, grads, recvs, send_sems, recv_sems):
        x, y, c, k_me, chips, kidx = _mesh_place()
        dev = 2 * k_me + c
        sends = []
        for w, sp in enumerate(self.specs):
            slot, arrival = recvs[w].at[dev], recv_sems.at[NDEV * w + dev]
            sends.append(_remote(sp.piece(grads[w], k_me, 1 - c), slot, send_sems.at[7 * w], arrival, (x, y, 1 - c)))
            for j, chip in enumerate(chips):
                for h in range(2):
                    sends.append(_remote(sp.piece(grads[w], kidx[j], h), slot, send_sems.at[7 * w + 1 + 2 * j + h],
                                         arrival, (*chip, h)))
        return sends

    def start(self, grads, recvs, send_sems, recv_sems):
        for cp in self._sends(grads, recvs, send_sems, recv_sems):
            cp.start()

    def finish(self, grads, recvs, send_sems, recv_sems):
        x, y, c, k_me, _, _ = _mesh_place()
        dev = 2 * k_me + c
        for w in range(self.n):
            for d in range(NDEV):
                landed = recvs[w].at[d]
                arrival = _remote(landed, landed, send_sems.at[7 * w], recv_sems.at[NDEV * w + d], (x, y, c))
                pl.when(d != dev)(arrival.wait_recv)
        for cp in self._sends(grads, recvs, send_sems, recv_sems):
            cp.wait_send()


def _device_sum(sp, local, recv, place):
    nd = len(sp.piece_shape)

    def body(p_ref, a_ref, b_ref, o_ref):
        d = pl.program_id(0)
        term = jnp.where(d == p_ref[2], a_ref[...], b_ref[...]).astype(F32)

        @pl.when(d == 0)
        def _():
            o_ref[...] = term

        @pl.when(d > 0)
        def _():
            o_ref[...] += term

    def mine(d, p_ref):
        return tuple(2 * p_ref[0] + p_ref[1] if a == sp.shard_axis == sp.half_axis else
                     p_ref[0] if a == sp.shard_axis else p_ref[1] if a == sp.half_axis else 0 for a in range(nd))

    def others(d, p_ref):
        return (jnp.where(d == p_ref[2], (d + 1) % NDEV, d),) + (0,) * nd

    return pl.pallas_call(
        body, name="rs_device_sum_" + sp.name,
        grid_spec=pltpu.PrefetchScalarGridSpec(
            num_scalar_prefetch=1, grid=(NDEV,),
            in_specs=[pl.BlockSpec(sp.piece_shape, mine), pl.BlockSpec((None,) + sp.piece_shape, others)],
            out_specs=pl.BlockSpec(sp.piece_shape,
                                   lambda d, p_ref: tuple(p_ref[1] if a == sp.half_axis else 0 for a in range(nd)))),
        out_shape=jax.ShapeDtypeStruct(sp.shard_shape, F32),
        compiler_params=_params(("arbitrary",)),
    )(place, local, recv)


def _pair_exchange(grads, specs):
    n = len(specs)

    def body(*refs):
        ins, outs = refs[:n], refs[n:2 * n]
        send_sems, recv_sems = refs[2 * n:]
        x, y, c, _, _, _ = _mesh_place()
        sibling = (x, y, 1 - c)
        sent = []
        for w, sp in enumerate(specs):
            for k in range(NCHIP):
                cp = _remote(sp.piece(ins[w], k, 1 - c), outs[w].at[k],
                             send_sems.at[NCHIP * w + k], recv_sems.at[NCHIP * w + k], sibling)
                cp.start()
                sent.append(cp)
        for cp in sent:
            cp.wait_recv()
        for cp in sent:
            cp.wait_send()

    hbm = pl.BlockSpec(memory_space=pl.ANY)
    return pl.pallas_call(
        body, name="rs_pair_exchange",
        out_shape=[jax.ShapeDtypeStruct((NCHIP,) + sp.piece_shape, F32) for sp in specs],
        in_specs=[hbm] * n, out_specs=[hbm] * n,
        scratch_shapes=[pltpu.SemaphoreType.DMA((NCHIP * n,)), pltpu.SemaphoreType.DMA((NCHIP * n,))],
    )(*grads)


def _pair_sum(sp, grad, recv, core):
    nd = len(sp.piece_shape)

    def body(c_ref, g_ref, r_ref, o_ref):
        o_ref[...] = (g_ref[...] + r_ref[...]).astype(BF16)

    slot = pl.BlockSpec((None,) + sp.piece_shape, lambda k, c_ref: (k,) + (0,) * nd)
    return pl.pallas_call(
        body, name="rs_pair_sum_" + sp.name,
        grid_spec=pltpu.PrefetchScalarGridSpec(
            num_scalar_prefetch=1, grid=(NCHIP,),
            in_specs=[sp.piece_block(), slot], out_specs=slot),
        out_shape=jax.ShapeDtypeStruct((NCHIP,) + sp.piece_shape, BF16),
        compiler_params=_params(("parallel",)),
    )(core, grad, recv)


def _chip_exchange(parts, specs):
    n = len(specs)

    def body(*refs):
        ins, outs = refs[:n], refs[n:2 * n]
        send_sems, recv_sems = refs[2 * n:]
        x, y, c, k_me, chips, kidx = _mesh_place()
        sent = []
        for j, chip in enumerate(chips):
            for w in range(n):
                cp = _remote(ins[w].at[kidx[j]], outs[w].at[k_me], send_sems.at[3 * w + j], recv_sems.at[3 * w + j],
                             (*chip, c))
                cp.start()
                sent.append(cp)
        for j, chip in enumerate(chips):
            for w in range(n):
                landed = outs[w].at[kidx[j]]
                _remote(landed, landed, send_sems.at[3 * w + j], recv_sems.at[3 * w + j], (*chip, c)).wait_recv()
        for cp in sent:
            cp.wait_send()

    hbm = pl.BlockSpec(memory_space=pl.ANY)
    return pl.pallas_call(
        body, name="rs_chip_exchange",
        out_shape=[jax.ShapeDtypeStruct((NCHIP,) + sp.piece_shape, BF16) for sp in specs],
        in_specs=[hbm] * n, out_specs=[hbm] * n,
        scratch_shapes=[pltpu.SemaphoreType.DMA((3 * n,)), pltpu.SemaphoreType.DMA((3 * n,))],
    )(*parts)


def _chip_sum(sp, parts, recv, place):
    nd = len(sp.piece_shape)

    def body(p_ref, a_ref, b_ref, o_ref):
        k = pl.program_id(0)
        term = jnp.where(k == p_ref[0], a_ref[...], b_ref[...]).astype(F32)

        @pl.when(k == 0)
        def _():
            o_ref[...] = term

        @pl.when(k > 0)
        def _():
            o_ref[...] += term

    def others(k, p_ref):
        return (jnp.where(k == p_ref[0], (k + 1) % NCHIP, k),) + (0,) * nd

    return pl.pallas_call(
        body, name="rs_chip_sum_" + sp.name,
        grid_spec=pltpu.PrefetchScalarGridSpec(
            num_scalar_prefetch=1, grid=(NCHIP,),
            in_specs=[pl.BlockSpec((None,) + sp.piece_shape, lambda k, p_ref: (p_ref[0],) + (0,) * nd),
                      pl.BlockSpec((None,) + sp.piece_shape, others)],
            out_specs=pl.BlockSpec(sp.piece_shape,
                                   lambda k, p_ref: tuple(p_ref[1] if a == sp.half_axis else 0 for a in range(nd)))),
        out_shape=jax.ShapeDtypeStruct(sp.shard_shape, F32),
        compiler_params=_params(("arbitrary",)),
    )(place, parts, recv)


def _pair_share(halves):
    def body(*refs):
        outs = refs[NW:2 * NW]
        send_sems, recv_sems = refs[2 * NW:]
        x, y, c, _, _, _ = _mesh_place()
        sibling = (x, y, 1 - c)
        sent = []
        for w, sp in enumerate(SHARDED):
            mine = sp.half(outs[w], c)
            cp = _remote(mine, mine, send_sems.at[w], recv_sems.at[w], sibling)
            cp.start()
            sent.append(cp)
        for w, sp in enumerate(SHARDED):
            landed = sp.half(outs[w], 1 - c)
            _remote(landed, landed, send_sems.at[w], recv_sems.at[w], sibling).wait_recv()
        for cp in sent:
            cp.wait_send()

    hbm = pl.BlockSpec(memory_space=pl.ANY)
    return pl.pallas_call(
        body, name="rs_pair_share",
        out_shape=[jax.ShapeDtypeStruct(sp.shard_shape, F32) for sp in SHARDED],
        in_specs=[hbm] * NW, out_specs=[hbm] * NW,
        input_output_aliases={w: w for w in range(NW)},
        scratch_shapes=[pltpu.SemaphoreType.DMA((NW,)), pltpu.SemaphoreType.DMA((NW,))],
    )(*halves)


def _reduce_scatter(g_in, local, received, core, place):
    sp_in = SHARDED[0]
    recv_a = _pair_exchange([g_in], [sp_in])
    part = _pair_sum(sp_in, g_in, recv_a[0], core)
    recv_b = _chip_exchange([part], [sp_in])
    halves = [_chip_sum(sp_in, part, recv_b[0], place)]
    halves += [_device_sum(sp, local[sp.name], received[sp.name], place) for sp in SHARDED[1:]]
    return _pair_share(halves)


def _place_bf16(sp, w, place):
    nd = len(sp.full_shape)

    def body(p_ref, w_ref, o_ref):
        o_ref[...] = w_ref[...].astype(BF16)

    return pl.pallas_call(
        body, name="place_" + sp.name,
        grid_spec=pltpu.PrefetchScalarGridSpec(
            num_scalar_prefetch=1, grid=(1,),
            in_specs=[pl.BlockSpec(sp.shard_shape, lambda i, p_ref: (0,) * nd)],
            out_specs=pl.BlockSpec(sp.shard_shape,
                                   lambda i, p_ref: tuple(p_ref[0] if a == sp.shard_axis else 0 for a in range(nd)))),
        out_shape=jax.ShapeDtypeStruct(sp.full_shape, BF16),
        compiler_params=_params(("arbitrary",)),
    )(place, w)


def _matmul_f32(a, b, name):
    def body(a_ref, b_ref, o_ref):
        o_ref[...] = jnp.dot(a_ref[...], b_ref[...], preferred_element_type=F32, precision=lax.Precision.HIGHEST)

    return pl.pallas_call(body, name=name, out_shape=jax.ShapeDtypeStruct((a.shape[0], b.shape[1]), F32),
                          compiler_params=pltpu.CompilerParams(vmem_limit_bytes=VMEM_LIMIT))(a, b)


def _sum_devices(stacked):
    def body(x_ref, o_ref):
        acc = x_ref[0]
        for d in range(1, NDEV):
            acc = acc + x_ref[d]
        o_ref[...] = acc

    return pl.pallas_call(body, name="sum_devices", out_shape=jax.ShapeDtypeStruct(stacked.shape[1:], F32),
                          compiler_params=pltpu.CompilerParams(vmem_limit_bytes=VMEM_LIMIT))(stacked)


def _adamw(w, g, m, v, name):
    r, cdim = w.shape
    tr = r if r <= 256 else (256 if r % 256 == 0 else r // 2)

    def body(w_ref, g_ref, m_ref, v_ref, d_ref, nm_ref, nv_ref):
        gv = g_ref[...]
        nm = ADAM_B1 * m_ref[...] + (1.0 - ADAM_B1) * gv
        nv = ADAM_B2 * v_ref[...] + (1.0 - ADAM_B2) * (gv * gv)
        m_hat = nm / (1.0 - ADAM_B1 ** ADAM_STEP)
        v_hat = nv / (1.0 - ADAM_B2 ** ADAM_STEP)
        d_ref[...] = -ADAM_LR * (m_hat / (jnp.sqrt(v_hat) + ADAM_EPS) + ADAM_WD * w_ref[...])
        nm_ref[...] = nm
        nv_ref[...] = nv

    blk = pl.BlockSpec((tr, cdim), lambda i: (i, 0))
    return pl.pallas_call(
        body, name="adamw_" + name, grid=(r // tr,), in_specs=[blk] * 4, out_specs=[blk] * 3,
        out_shape=[jax.ShapeDtypeStruct(w.shape, F32)] * 3,
        compiler_params=_params(("parallel",)),
    )(w, g, m, v)


WEIGHT_NAMES = ("g_pre_mix", "g_post_mix", "g_pre_ffn", "g_post_ffn", "w_ada", "b_ada", "w_in", "w_pool",
                "pool_scale", "conv_w", "conv_b", "w_bout", "w_o", "w_up", "ffn_conv_w", "ffn_conv_b", "w_down")
MATRIX_NAMES = ("w_ada",) + tuple(sp.name for sp in SHARDED)
VECTOR_NAMES = tuple(n for n in WEIGHT_NAMES if n not in MATRIX_NAMES)

CW = D // NCHIP
FCW = F2 // NCHIP
ADA_W = DIN // NCHIP
COND_BLOCK = (8, 768)
GRAD_BLOCK = (8, 4864)


def _flat_pad(parts, shape):
    flat = jnp.concatenate([p.reshape(-1) for p in parts])
    return jnp.pad(flat, (0, shape[0] * shape[1] - flat.shape[0])).reshape(shape)


def _take(flat, offset, shape):
    size = 1
    for n in shape:
        size *= n
    return flat[offset:offset + size].reshape(shape), offset + size


def kernel(x, c, g_pre_mix, g_post_mix, g_pre_ffn, g_post_ffn, w_ada, b_ada, w_in, w_pool, pool_scale, conv_w, conv_b, w_bout, w_o, w_up, ffn_conv_w, ffn_conv_b, w_down, loss_target, m_g_pre_mix, m_g_post_mix, m_g_pre_ffn, m_g_post_ffn, m_w_ada, m_b_ada, m_w_in, m_w_pool, m_pool_scale, m_conv_w, m_conv_b, m_w_bout, m_w_o, m_w_up, m_ffn_conv_w, m_ffn_conv_b, m_w_down, v_g_pre_mix, v_g_post_mix, v_g_pre_ffn, v_g_post_ffn, v_w_ada, v_b_ada, v_w_in, v_w_pool, v_pool_scale, v_conv_w, v_conv_b, v_w_bout, v_w_o, v_w_up, v_ffn_conv_w, v_ffn_conv_b, v_w_down):
    weights = dict(g_pre_mix=g_pre_mix, g_post_mix=g_post_mix, g_pre_ffn=g_pre_ffn, g_post_ffn=g_post_ffn,
                   w_ada=w_ada, b_ada=b_ada, w_in=w_in, w_pool=w_pool, pool_scale=pool_scale, conv_w=conv_w,
                   conv_b=conv_b, w_bout=w_bout, w_o=w_o, w_up=w_up, ffn_conv_w=ffn_conv_w, ffn_conv_b=ffn_conv_b,
                   w_down=w_down)
    mom1 = dict(g_pre_mix=m_g_pre_mix, g_post_mix=m_g_post_mix, g_pre_ffn=m_g_pre_ffn, g_post_ffn=m_g_post_ffn,
                w_ada=m_w_ada, b_ada=m_b_ada, w_in=m_w_in, w_pool=m_w_pool, pool_scale=m_pool_scale,
                conv_w=m_conv_w, conv_b=m_conv_b, w_bout=m_w_bout, w_o=m_w_o, w_up=m_w_up,
                ffn_conv_w=m_ffn_conv_w, ffn_conv_b=m_ffn_conv_b, w_down=m_w_down)
    mom2 = dict(g_pre_mix=v_g_pre_mix, g_post_mix=v_g_post_mix, g_pre_ffn=v_g_pre_ffn, g_post_ffn=v_g_post_ffn,
                w_ada=v_w_ada, b_ada=v_b_ada, w_in=v_w_in, w_pool=v_w_pool, pool_scale=v_pool_scale,
                conv_w=v_conv_w, conv_b=v_conv_b, w_bout=v_w_bout, w_o=v_w_o, w_up=v_w_up,
                ffn_conv_w=v_ffn_conv_w, ffn_conv_b=v_ffn_conv_b, w_down=v_w_down)

    chip = 2 * lax.axis_index("x") + lax.axis_index("y")
    core = lax.axis_index("c")
    dev = 2 * chip + core
    core_op = jnp.reshape(core, (1,)).astype(jnp.int32)
    place = jnp.stack([chip, core, dev]).astype(jnp.int32)

    cond = _all_gather_small(_flat_pad([c, conv_w, ffn_conv_w], COND_BLOCK), "gather_cond")
    cond = cond.reshape(NDEV, -1)
    c_all = cond[:, :D]
    by_chip = cond[0::2]
    conv_w_full = by_chip[:, D:D + 3 * CW].reshape(NCHIP, 3, CW).transpose(1, 0, 2).reshape(3, D)
    ffn_w_full = by_chip[:, D + 3 * CW:D + 3 * CW + 3 * FCW].reshape(NCHIP, 3, FCW).transpose(1, 0, 2).reshape(3, F2)

    mod_cols = _all_gather_small(_matmul_f32(c_all, w_ada[0], "ada_mod"), "gather_mod")
    mod_cols = mod_cols.reshape(NDEV, NDEV, ADA_W)[0::2]
    mod = lax.dynamic_index_in_dim(mod_cols, dev, axis=1, keepdims=False).reshape(6, D) + b_ada.reshape(6, D)
    vec_d = jnp.concatenate([mod, g_pre_mix, g_post_mix, g_pre_ffn, g_post_ffn, pool_scale, conv_b, conv_w_full,
                             jnp.zeros((VD_ROWS - 15, D), F32)], axis=0)
    vec_f = jnp.concatenate([ffn_w_full, ffn_conv_b, jnp.zeros((FV_ROWS - 4, F2), F32)], axis=0)

    placed = [_place_bf16(sp, weights[sp.name][0], place) for sp in SHARDED]
    w_in_full, = _gather_weights(placed[:1], SHARDED[:1], "gather_w_in")
    loss_blk, dx, vecs, g_in, local, received = _local_step(x[0], loss_target[0], vec_d, vec_f, w_in_full, placed[1:])

    dmod = [vecs[n] for n in ("dsh1", "dsc1", "dgt1", "dsh2", "dsc2", "dgt2")]
    small = [vecs["dg_pre_mix"], vecs["dg_post_mix"], vecs["dg_pre_ffn"], vecs["dg_post_ffn"]] + dmod + [
        vecs["dpool_scale"], vecs["dconv_w"], vecs["dconv_b"], vecs["dffn_conv_w"], vecs["dffn_conv_b"],
        loss_blk[0]]
    gathered = _all_gather_small(_flat_pad(small, GRAD_BLOCK), "gather_vector_grads")
    total = _sum_devices(gathered.reshape((NDEV,) + GRAD_BLOCK)).reshape(-1)
    vgrad = {}
    off = 0
    for n in ("g_pre_mix", "g_post_mix", "g_pre_ffn", "g_post_ffn"):
        vgrad[n], off = _take(total, off, (1, D))
    dmod_off = off
    vgrad["b_ada"], off = _take(total, off, (1, DIN))
    vgrad["pool_scale"], off = _take(total, off, (1, D))
    g_conv_w, off = _take(total, off, (3, D))
    vgrad["conv_w"] = lax.dynamic_slice_in_dim(g_conv_w, chip * CW, CW, axis=1)[None]
    vgrad["conv_b"], off = _take(total, off, (1, D))
    g_ffn_w, off = _take(total, off, (3, F2))
    vgrad["ffn_conv_w"] = lax.dynamic_slice_in_dim(g_ffn_w, chip * FCW, FCW, axis=1)[None]
    vgrad["ffn_conv_b"], off = _take(total, off, (1, F2))
    loss = total[off]

    dmod_all = gathered.reshape(NDEV, -1)[:, dmod_off:dmod_off + DIN]
    dmod_cols = lax.dynamic_slice_in_dim(dmod_all, chip * ADA_W, ADA_W, axis=1)
    g_ada = _matmul_f32(jnp.pad(c_all.T, ((0, 0), (0, 128 - NDEV))), jnp.pad(dmod_cols, ((0, 128 - NDEV), (0, 0))),
                        "ada_wgrad")

    reduced = _reduce_scatter(g_in, local, received, core_op, place)
    mgrad = {"w_ada": g_ada}
    for sp, g in zip(SHARDED, reduced):
        mgrad[sp.name] = g

    grad, delta, new_m, new_v = {}, {}, {}, {}
    for n in MATRIX_NAMES:
        shape = weights[n].shape
        two_d = (-1, shape[-1])
        d, nm, nv = _adamw(weights[n].reshape(two_d), mgrad[n].reshape(two_d), mom1[n].reshape(two_d),
                           mom2[n].reshape(two_d), n)
        grad[n], delta[n], new_m[n], new_v[n] = (a.reshape(shape) for a in (mgrad[n], d, nm, nv))
    flat = lambda tree: jnp.concatenate([tree[n].reshape(1, -1) for n in VECTOR_NAMES], axis=1)
    d, nm, nv = _adamw(flat(weights), flat(vgrad), flat(mom1), flat(mom2), "vectors")
    off = 0
    for n in VECTOR_NAMES:
        shape = weights[n].shape
        grad[n] = vgrad[n].reshape(shape)
        delta[n], _ = _take(d[0], off, shape)
        new_m[n], _ = _take(nm[0], off, shape)
        new_v[n], off = _take(nv[0], off, shape)

    return (loss, dx[None], *[grad[n] for n in WEIGHT_NAMES], *[delta[n] for n in WEIGHT_NAMES],
            *[new_m[n] for n in WEIGHT_NAMES], *[new_v[n] for n in WEIGHT_NAMES])
```

```python
import jax
import jax.numpy as jnp
from jax import lax
from jax.experimental import pallas as pl
from jax.experimental.pallas import tpu as pltpu

F32 = jnp.float32
BF16 = jnp.bfloat16

D = 1024
DIN = 6 * D
F = 2816
F2 = 2 * F
NG = 4
GW = D // NG
POOL_CARRY = 16
CONV_CARRY = 3
EPS = 1e-6
NCHIP = 4
NDEV = 8

ADAM_LR = 0.001
ADAM_B1 = 0.9
ADAM_B2 = 0.999
ADAM_EPS = 1e-08
ADAM_WD = 0.01
ADAM_STEP = 10

VMEM_LIMIT = 60 * 1024 * 1024

(V_SH1, V_SC1, V_GT1, V_SH2, V_SC2, V_GT2, V_GPRE1, V_GPOST1, V_GPRE2, V_GPOST2,
 V_PSCALE, V_CB, V_CW0, V_CW1, V_CW2) = range(15)
VD_ROWS = 16
FV_W0, FV_W1, FV_W2, FV_B = range(4)
FV_ROWS = 8

MESH = pl.DeviceIdType.MESH


def _params(sem=None, vmem=VMEM_LIMIT):
    return pltpu.CompilerParams(dimension_semantics=sem, vmem_limit_bytes=vmem)


def _row(ref, r):
    return ref[r:r + 1, :]


def _load_once(pairs, sem):
    @pl.when(pl.program_id(0) == 0)
    def _():
        copies = [pltpu.make_async_copy(src, dst, sem.at[n]) for n, (src, dst) in enumerate(pairs)]
        for cp in copies:
            cp.start()
        for cp in copies:
            cp.wait()


def _dot(a, b):
    return jnp.dot(a, b, preferred_element_type=F32)


def _dot_nt(a, b):
    return lax.dot_general(a, b, (((1,), (1,)), ((), ())), preferred_element_type=F32)


BLK = 256
SEG = BLK // 8


LANES = 128


def _load_rows(ref, scr, ts):
    nc = ref.shape[-1] // LANES
    for c in range(nc):
        scr[c] = ref[:, c * LANES:(c + 1) * LANES]
    rows = [jnp.concatenate([scr[c, pl.ds(b * BLK + j, 8, stride=SEG), :] for c in range(nc)], axis=1)
            for b in range(ts // BLK) for j in range(SEG)]
    return jnp.concatenate(rows, axis=0)


def _store_rows(ref, val, scr, ts):
    nc = ref.shape[-1] // LANES
    for c in range(nc):
        scr[c] = val[:, c * LANES:(c + 1) * LANES]
    for b in range(ts // BLK):
        for r in range(8):
            for q in range(SEG // 8):
                t = b * BLK + r * SEG + 8 * q
                ref[t:t + 8, :] = jnp.concatenate(
                    [scr[c, pl.ds(b * BLK + 64 * q + r, 8, stride=8), :] for c in range(nc)], axis=1)


def _times(t0):
    p = lax.broadcasted_iota(jnp.int32, (BLK, 1), 0)
    return t0 + (p & 7) * SEG + (p >> 3)


def _before(x, carry, s):
    x3 = x.reshape(SEG, 8, x.shape[-1])
    tail = pltpu.roll(x3[SEG - s:], 1, 1)
    row = lax.broadcasted_iota(jnp.int32, tail.shape, 1)
    out = jnp.concatenate([jnp.where(row == 0, carry, tail), x3[:SEG - s]], axis=0)
    return out.reshape(x.shape), tail


def _after(x, carry, s):
    x3 = x.reshape(SEG, 8, x.shape[-1])
    head = pltpu.roll(x3[:s], 7, 1)
    row = lax.broadcasted_iota(jnp.int32, head.shape, 1)
    out = jnp.concatenate([x3[s:], jnp.where(row == 7, carry, head)], axis=0)
    return out.reshape(x.shape), head


def _causal_conv(x, carry, cols, w0, w1, w2, b):
    x1, carry[0:1, :, cols] = _before(x, carry[0:1, :, cols], 1)
    x2, carry[1:3, :, cols] = _before(x, carry[1:3, :, cols], 2)
    return b + w2 * x + w1 * x1 + w0 * x2


def _causal_conv_bwd(dy, carry, cols, w0, w1, w2):
    d1, carry[0:1, :, cols] = _after(dy, carry[0:1, :, cols], 1)
    d2, carry[1:3, :, cols] = _after(dy, carry[1:3, :, cols], 2)
    return w2 * dy + w1 * d1 + w0 * d2, d1, d2


def _pool_counts(t0, g):
    return jnp.minimum((_times(t0) + 1).astype(F32), float(2 << g))


def _rms(x):
    return lax.rsqrt(jnp.mean(x * x, axis=-1, keepdims=True) + EPS)


def _rms_bwd(dn, n, r):
    return r * (dn - n * jnp.mean(dn * n, axis=-1, keepdims=True))


def _colsum(x):
    return jnp.sum(x, axis=0, keepdims=True)


def _gelu_and_grad(x):
    k = 0.7978845608028654
    inner = k * (x + 0.044715 * (x * x * x))
    th = jnp.tanh(inner)
    gelu = 0.5 * x * (1.0 + th)
    dgelu = 0.5 * (1.0 + th) + 0.5 * x * (1.0 - th * th) * (k * (1.0 + 3.0 * 0.044715 * (x * x)))
    return gelu, dgelu


def _fwd_proj(x, vec_d, w_in, placed_rest, ts):
    s = x.shape[0]
    cw = DIN // NCHIP
    gather = _WeightGather(SHARDED[1:])
    n = gather.n

    def body(*refs):
        x_ref, v_ref, w_hbm = refs[:3]
        proj_ref, h1_ref, xs_ref = refs[3 + n:6 + n]
        rest = refs[6 + n:6 + 2 * n]
        w_vmem, rowbuf, sem, send_sems, recv_sems = refs[6 + 2 * n:]
        i = pl.program_id(0)
        pl.when(i == 0)(lambda: gather.start(rest, send_sems, recv_sems))
        compute(x_ref, v_ref, w_hbm, proj_ref, h1_ref, xs_ref, w_vmem, rowbuf, sem)
        pl.when(i == s // ts - 1)(lambda: gather.finish(rest, send_sems, recv_sems))

    def compute(x_ref, v_ref, w_hbm, proj_ref, h1_ref, xs_ref, w_vmem, rowbuf, sem):
        _load_once([(w_hbm, w_vmem)], sem)
        xv = _load_rows(x_ref, rowbuf, ts)
        xs_ref[...] = xv
        n1 = xv * _rms(xv)
        h = n1 * (_row(v_ref, V_GPRE1) * (1.0 + _row(v_ref, V_SC1))) + _row(v_ref, V_SH1)
        hb = h.astype(BF16)
        h1_ref[...] = hb
        for k in range(NCHIP):
            cols = slice(k * cw, (k + 1) * cw)
            proj_ref[:, cols] = _dot(hb, w_vmem[:, cols]).astype(BF16)

    return pl.pallas_call(
        body, name="fwd_proj", grid=(s // ts,),
        in_specs=[pl.BlockSpec((ts, D), lambda i: (i, 0)),
                  pl.BlockSpec((VD_ROWS, D), lambda i: (0, 0)),
                  pl.BlockSpec(memory_space=pl.ANY)] + gather.specs_any,
        out_specs=[pl.BlockSpec((ts, DIN), lambda i: (i, 0)), pl.BlockSpec((ts, D), lambda i: (i, 0)),
                   pl.BlockSpec((ts, D), lambda i: (i, 0))] + gather.specs_any,
        out_shape=[jax.ShapeDtypeStruct((s, DIN), BF16), jax.ShapeDtypeStruct((s, D), BF16),
                   jax.ShapeDtypeStruct((s, D), F32)] + gather.out_shape,
        input_output_aliases={3 + w: 3 + w for w in range(n)},
        scratch_shapes=[pltpu.VMEM((D, DIN), BF16), pltpu.VMEM((D // LANES, ts, LANES), F32),
                        pltpu.SemaphoreType.DMA((1,))] + gather.scratch,
        compiler_params=_params(("arbitrary",)),
    )(x, vec_d, w_in, *placed_rest)


def _fwd_mix(proj, x, vec_d, w_pool, w_bout, w_o, ts):
    s = x.shape[0]

    def body(p_ref, x_ref, v_ref, wp_hbm, wb_hbm, wo_hbm,
             x1_ref, o_ref, pg_ref, q_ref, mg_ref, ya_ref, yb_ref, cv_ref,
             wp, wb, wo, carry_p, carry_v, sem):
        i = pl.program_id(0)
        _load_once([(wp_hbm, wp), (wb_hbm, wb), (wo_hbm, wo)], sem)

        @pl.when(i == 0)
        def _():
            carry_p[...] = jnp.zeros_like(carry_p)
            carry_v[...] = jnp.zeros_like(carry_v)

        t0 = i * ts
        for g in range(NG):
            cols = slice(g * GW, (g + 1) * GW)
            u = p_ref[:, cols].astype(F32)
            e = u
            for l in range(g + 1):
                slot = slice((1 << l) - 1, (2 << l) - 1)
                shifted, carry_p[slot, :, cols] = _before(e, carry_p[slot, :, cols], 1 << l)
                e = e + shifted
            pgb = (e / _pool_counts(t0, g) - u).astype(BF16)
            pg_ref[:, cols] = pgb
            ya_ref[:, cols] = _dot(pgb, wp[g]).astype(BF16)

        u_x = p_ref[:, D:2 * D].astype(F32)
        u_c = p_ref[:, 3 * D:4 * D].astype(F32)
        v = u_c * u_x
        cv = _causal_conv(v, carry_v, slice(None), _row(v_ref, V_CW0), _row(v_ref, V_CW1),
                          _row(v_ref, V_CW2), _row(v_ref, V_CB))
        cv_ref[...] = cv.astype(BF16)
        q = (p_ref[:, 2 * D:3 * D].astype(F32) * cv).astype(BF16)
        q_ref[...] = q
        y_b = _dot(q, wb[...])
        yb_ref[...] = y_b.astype(BF16)

        y_a = ya_ref[...].astype(F32) * _row(v_ref, V_PSCALE)
        merged = (jax.nn.sigmoid(p_ref[:, 4 * D:5 * D].astype(F32)) * y_a
                  + jax.nn.sigmoid(p_ref[:, 5 * D:6 * D].astype(F32)) * y_b).astype(BF16)
        mg_ref[...] = merged
        o = _dot(merged, wo[...])
        o_ref[...] = o
        x1_ref[...] = x_ref[...] + _row(v_ref, V_GT1) * ((o * _rms(o)) * _row(v_ref, V_GPOST1))

    tile = lambda w: pl.BlockSpec((ts, w), lambda i: (i, 0))
    hbm = pl.BlockSpec(memory_space=pl.ANY)
    return pl.pallas_call(
        body, name="fwd_mix", grid=(s // ts,),
        in_specs=[tile(DIN), tile(D), pl.BlockSpec((VD_ROWS, D), lambda i: (0, 0)), hbm, hbm, hbm],
        out_specs=[tile(D)] * 8,
        out_shape=[jax.ShapeDtypeStruct((s, D), F32), jax.ShapeDtypeStruct((s, D), F32)]
        + [jax.ShapeDtypeStruct((s, D), BF16)] * 6,
        scratch_shapes=[pltpu.VMEM((NG, GW, GW), BF16), pltpu.VMEM((D, D), BF16), pltpu.VMEM((D, D), BF16),
                        pltpu.VMEM((POOL_CARRY, 8, D), F32), pltpu.VMEM((CONV_CARRY, 8, D), F32),
                        pltpu.SemaphoreType.DMA((3,))],
        compiler_params=_params(("arbitrary",)),
    )(proj, x, vec_d, w_pool, w_bout, w_o)


def _fwd_ffn(x1, tgt, vec_d, vec_f, w_up, w_down, ts):
    s = x1.shape[0]
    hw = F // 2

    def body(x1_ref, t_ref, v_ref, f_ref, wu_hbm, wd_hbm,
             up_ref, upc_ref, a_ref, h2_ref, dx2_ref, dff_ref, vo_ref, loss_ref,
             wu, wd, carry, rowbuf, sem):
        i = pl.program_id(0)
        _load_once([(wu_hbm, wu), (wd_hbm, wd)], sem)

        @pl.when(i == 0)
        def _():
            carry[...] = jnp.zeros_like(carry)
            vo_ref[...] = jnp.zeros_like(vo_ref)
            loss_ref[...] = jnp.zeros_like(loss_ref)

        x1v = x1_ref[...]
        n3 = x1v * _rms(x1v)
        h2 = (n3 * (_row(v_ref, V_GPRE2) * (1.0 + _row(v_ref, V_SC2))) + _row(v_ref, V_SH2)).astype(BF16)
        h2_ref[...] = h2

        ff = jnp.zeros((ts, D), F32)
        for p in range(2):
            up = []
            for cols in (slice(p * hw, (p + 1) * hw), slice(F + p * hw, F + (p + 1) * hw)):
                u0 = _dot(h2, wu[:, cols])
                up_ref[:, cols] = u0.astype(BF16)
                y = _causal_conv(u0, carry, cols, f_ref[FV_W0:FV_W0 + 1, cols], f_ref[FV_W1:FV_W1 + 1, cols],
                                 f_ref[FV_W2:FV_W2 + 1, cols], f_ref[FV_B:FV_B + 1, cols])
                upc_ref[:, cols] = y.astype(BF16)
                up.append(y)
            gelu, _ = _gelu_and_grad(up[0])
            a = (gelu * up[1]).astype(BF16)
            a_ref[:, p * hw:(p + 1) * hw] = a
            ff = ff + _dot(a, wd[p * hw:(p + 1) * hw, :])

        r4 = _rms(ff)
        n4 = ff * r4
        gt2 = _row(v_ref, V_GT2)
        gpost = _row(v_ref, V_GPOST2)
        y4 = n4 * gpost
        diff = (x1v + gt2 * y4) - _load_rows(t_ref, rowbuf, ts)
        loss_ref[...] += jnp.full(loss_ref.shape, 0.5 / D * jnp.sum(diff * diff), F32)
        dx2 = diff * (1.0 / D)
        dx2_ref[...] = dx2
        dy4 = dx2 * gt2
        vo_ref[0:1, :] += _colsum(dx2 * y4)
        vo_ref[1:2, :] += _colsum(dy4 * n4)
        dff_ref[...] = _rms_bwd(dy4 * gpost, n4, r4).astype(BF16)

    tile = lambda w: pl.BlockSpec((ts, w), lambda i: (i, 0))
    full = lambda r, w: pl.BlockSpec((r, w), lambda i: (0, 0))
    hbm = pl.BlockSpec(memory_space=pl.ANY)
    return pl.pallas_call(
        body, name="fwd_ffn", grid=(s // ts,),
        in_specs=[tile(D), tile(D), full(VD_ROWS, D), full(FV_ROWS, F2), hbm, hbm],
        out_specs=[tile(F2), tile(F2), tile(F), tile(D), tile(D), tile(D), full(8, D), full(8, 128)],
        out_shape=[jax.ShapeDtypeStruct((s, F2), BF16), jax.ShapeDtypeStruct((s, F2), BF16),
                   jax.ShapeDtypeStruct((s, F), BF16),
                   jax.ShapeDtypeStruct((s, D), BF16), jax.ShapeDtypeStruct((s, D), F32),
                   jax.ShapeDtypeStruct((s, D), BF16), jax.ShapeDtypeStruct((8, D), F32),
                   jax.ShapeDtypeStruct((8, 128), F32)],
        scratch_shapes=[pltpu.VMEM((D, F2), BF16), pltpu.VMEM((F, D), BF16), pltpu.VMEM((CONV_CARRY, 8, F2), F32),
                        pltpu.VMEM((D // LANES, ts, LANES), F32), pltpu.SemaphoreType.DMA((2,))],
        compiler_params=_params(("arbitrary",)),
    )(x1, tgt, vec_d, vec_f, w_up, w_down)


def _bwd_ffn(dff, dx2, x1, up0, upc, vec_d, vec_f, w_up, w_down, exchange, ex_grads, ts):
    s = x1.shape[0]
    nt = s // ts
    hw = F // 2
    n = exchange.n

    def body(*refs):
        ins, grads = refs[:9], refs[9:9 + n]
        outs, recvs = refs[9 + n:13 + n], refs[13 + n:13 + 2 * n]
        scratch, sems = refs[13 + 2 * n:-2], refs[-2:]
        i = pl.program_id(0)
        pl.when(i == 0)(lambda: exchange.start(grads, recvs, *sems))
        compute(*ins, *outs, *scratch)
        pl.when(i == nt - 1)(lambda: exchange.finish(grads, recvs, *sems))

    def compute(dff_ref, dx2_ref, x1_ref, up_ref, upc_ref, v_ref, f_ref, wu_hbm, wd_hbm,
                dx1_ref, dup_ref, vo_ref, fo_ref, wu, wd, carry, sem):
        i = pl.program_id(0)
        _load_once([(wu_hbm, wu), (wd_hbm, wd)], sem)

        @pl.when(i == 0)
        def _():
            carry[...] = jnp.zeros_like(carry)
            vo_ref[...] = jnp.zeros_like(vo_ref)
            fo_ref[...] = jnp.zeros_like(fo_ref)

        dffb = dff_ref[...]
        dh2 = jnp.zeros((ts, D), F32)
        for p in range(2):
            slabs = (slice(p * hw, (p + 1) * hw), slice(F + p * hw, F + (p + 1) * hw))
            gelu, dgelu = _gelu_and_grad(upc_ref[:, slabs[0]].astype(F32))
            da = _dot_nt(dffb, wd[p * hw:(p + 1) * hw, :])
            dups = (da * upc_ref[:, slabs[1]].astype(F32) * dgelu, da * gelu)
            for cols, dup in zip(slabs, dups):
                du0, d1, d2 = _causal_conv_bwd(dup, carry, cols, f_ref[FV_W0:FV_W0 + 1, cols],
                                               f_ref[FV_W1:FV_W1 + 1, cols], f_ref[FV_W2:FV_W2 + 1, cols])
                u0 = up_ref[:, cols].astype(F32)
                fo_ref[FV_B:FV_B + 1, cols] += _colsum(dup)
                fo_ref[FV_W2:FV_W2 + 1, cols] += _colsum(dup * u0)
                fo_ref[FV_W1:FV_W1 + 1, cols] += _colsum(d1 * u0)
                fo_ref[FV_W0:FV_W0 + 1, cols] += _colsum(d2 * u0)
                du0 = du0.astype(BF16)
                dup_ref[:, cols] = du0
                dh2 = dh2 + _dot_nt(du0, wu[:, cols])

        x1v = x1_ref[...]
        r3 = _rms(x1v)
        n3 = x1v * r3
        gpre = _row(v_ref, V_GPRE2)
        sc = 1.0 + _row(v_ref, V_SC2)
        vo_ref[0:1, :] += _colsum(dh2)
        vo_ref[1:2, :] += _colsum(dh2 * n3 * gpre)
        vo_ref[2:3, :] += _colsum(dh2 * n3 * sc)
        dx1_ref[...] = dx2_ref[...] + _rms_bwd(dh2 * (gpre * sc), n3, r3)

    rev = lambda w: pl.BlockSpec((ts, w), lambda i: (nt - 1 - i, 0))
    full = lambda r, w: pl.BlockSpec((r, w), lambda i: (0, 0))
    hbm = pl.BlockSpec(memory_space=pl.ANY)
    return pl.pallas_call(
        body, name="bwd_ffn", grid=(nt,),
        in_specs=[rev(D), rev(D), rev(D), rev(F2), rev(F2), full(VD_ROWS, D), full(FV_ROWS, F2), hbm, hbm]
        + exchange.specs_any,
        out_specs=[rev(D), rev(F2), full(8, D), full(FV_ROWS, F2)] + exchange.specs_any,
        out_shape=[jax.ShapeDtypeStruct((s, D), F32), jax.ShapeDtypeStruct((s, F2), BF16),
                   jax.ShapeDtypeStruct((8, D), F32), jax.ShapeDtypeStruct((FV_ROWS, F2), F32)] + exchange.out_shape,
        scratch_shapes=[pltpu.VMEM((D, F2), BF16), pltpu.VMEM((F, D), BF16), pltpu.VMEM((CONV_CARRY, 8, F2), F32),
                        pltpu.SemaphoreType.DMA((2,))] + exchange.scratch,
        compiler_params=_params(("arbitrary",)),
    )(dff, dx2, x1, up0, upc, vec_d, vec_f, w_up, w_down, *ex_grads)


def _bwd_mix(dx1, o, proj, cv, ya0, yb, vec_d, w_pool, w_bout, w_o, exchange, ex_grads, ts):
    s = dx1.shape[0]
    nt = s // ts
    n = exchange.n

    def body(*refs):
        ins, grads = refs[:10], refs[10:10 + n]
        outs, recvs = refs[10 + n:15 + n], refs[15 + n:15 + 2 * n]
        scratch, sems = refs[15 + 2 * n:-2], refs[-2:]
        i = pl.program_id(0)
        pl.when(i == 0)(lambda: exchange.start(grads, recvs, *sems))
        compute(*ins, *outs, *scratch)
        pl.when(i == nt - 1)(lambda: exchange.finish(grads, recvs, *sems))

    def compute(dx1_ref, o_ref, p_ref, cv_ref, ya_ref, yb_ref, v_ref, wp_hbm, wb_hbm, wo_hbm,
                dp_ref, do_ref, dyb_ref, dya_ref, vo_ref, wp, wb, wo, carry_d, carry_e, sem):
        i = pl.program_id(0)
        _load_once([(wp_hbm, wp), (wb_hbm, wb), (wo_hbm, wo)], sem)

        @pl.when(i == 0)
        def _():
            carry_d[...] = jnp.zeros_like(carry_d)
            carry_e[...] = jnp.zeros_like(carry_e)
            vo_ref[...] = jnp.zeros_like(vo_ref)

        t0 = (nt - 1 - i) * ts
        dx1v = dx1_ref[...]
        ov = o_ref[...]
        r2 = _rms(ov)
        n2 = ov * r2
        gpost = _row(v_ref, V_GPOST1)
        vo_ref[0:1, :] += _colsum(dx1v * (n2 * gpost))
        dy2 = dx1v * _row(v_ref, V_GT1)
        vo_ref[1:2, :] += _colsum(dy2 * n2)
        dob = _rms_bwd(dy2 * gpost, n2, r2).astype(BF16)
        do_ref[...] = dob
        dmerged = _dot_nt(dob, wo[...])

        ya0 = ya_ref[...].astype(F32)
        pscale = _row(v_ref, V_PSCALE)
        sa = jax.nn.sigmoid(p_ref[:, 4 * D:5 * D].astype(F32))
        dp_ref[:, 4 * D:5 * D] = (dmerged * (ya0 * pscale) * sa * (1.0 - sa)).astype(BF16)
        dy_a = dmerged * sa
        vo_ref[2:3, :] += _colsum(dy_a * ya0)
        dya0 = (dy_a * pscale).astype(BF16)
        dya_ref[...] = dya0

        sb = jax.nn.sigmoid(p_ref[:, 5 * D:6 * D].astype(F32))
        dp_ref[:, 5 * D:6 * D] = (dmerged * yb_ref[...].astype(F32) * sb * (1.0 - sb)).astype(BF16)
        dy_b = (dmerged * sb).astype(BF16)
        dyb_ref[...] = dy_b
        dq = _dot_nt(dy_b, wb[...])

        u_x = p_ref[:, D:2 * D].astype(F32)
        u_b = p_ref[:, 2 * D:3 * D].astype(F32)
        u_c = p_ref[:, 3 * D:4 * D].astype(F32)
        w0, w1, w2 = _row(v_ref, V_CW0), _row(v_ref, V_CW1), _row(v_ref, V_CW2)
        dp_ref[:, 2 * D:3 * D] = (dq * cv_ref[...].astype(F32)).astype(BF16)
        dcv = dq * u_b
        dv, d1, d2 = _causal_conv_bwd(dcv, carry_d, slice(None), w0, w1, w2)
        v = u_c * u_x
        vo_ref[3:4, :] += _colsum(dcv)
        vo_ref[4:5, :] += _colsum(d2 * v)
        vo_ref[5:6, :] += _colsum(d1 * v)
        vo_ref[6:7, :] += _colsum(dcv * v)
        dp_ref[:, D:2 * D] = (dv * u_c).astype(BF16)
        dp_ref[:, 3 * D:4 * D] = (dv * u_x).astype(BF16)

        for g in range(NG):
            cols = slice(g * GW, (g + 1) * GW)
            dpg = _dot_nt(dya0[:, cols], wp[g])
            e = dpg / _pool_counts(t0, g)
            for l in range(g + 1):
                slot = slice((1 << l) - 1, (2 << l) - 1)
                shifted, carry_e[slot, :, cols] = _after(e, carry_e[slot, :, cols], 1 << l)
                e = e + shifted
            dp_ref[:, cols] = (e - dpg).astype(BF16)

    rev = lambda w: pl.BlockSpec((ts, w), lambda i: (nt - 1 - i, 0))
    hbm = pl.BlockSpec(memory_space=pl.ANY)
    return pl.pallas_call(
        body, name="bwd_mix", grid=(nt,),
        in_specs=[rev(D), rev(D), rev(DIN), rev(D), rev(D), rev(D), pl.BlockSpec((VD_ROWS, D), lambda i: (0, 0)),
                  hbm, hbm, hbm] + exchange.specs_any,
        out_specs=[rev(DIN), rev(D), rev(D), rev(D), pl.BlockSpec((8, D), lambda i: (0, 0))] + exchange.specs_any,
        out_shape=[jax.ShapeDtypeStruct((s, DIN), BF16)] + [jax.ShapeDtypeStruct((s, D), BF16)] * 3
        + [jax.ShapeDtypeStruct((8, D), F32)] + exchange.out_shape,
        scratch_shapes=[pltpu.VMEM((NG, GW, GW), BF16), pltpu.VMEM((D, D), BF16), pltpu.VMEM((D, D), BF16),
                        pltpu.VMEM((CONV_CARRY, 8, D), F32), pltpu.VMEM((POOL_CARRY, 8, D), F32),
                        pltpu.SemaphoreType.DMA((3,))] + exchange.scratch,
        compiler_params=_params(("arbitrary",)),
    )(dx1, o, proj, cv, ya0, yb, vec_d, w_pool, w_bout, w_o, *ex_grads)


def _bwd_in(dproj, dx1, x, vec_d, w_in, exchange, ex_grads, ts):
    s = x.shape[0]
    nt = s // ts
    n = exchange.n

    def body(*refs):
        ins, grads = refs[:5], refs[5:5 + n]
        outs, recvs = refs[5 + n:7 + n], refs[7 + n:7 + 2 * n]
        scratch, sems = refs[7 + 2 * n:-2], refs[-2:]
        i = pl.program_id(0)
        pl.when(i == 0)(lambda: exchange.start(grads, recvs, *sems))
        compute(*ins, *outs, *scratch)
        pl.when(i == nt - 1)(lambda: exchange.finish(grads, recvs, *sems))

    def compute(dp_ref, dx1_ref, x_ref, v_ref, w_hbm, dx_ref, vo_ref, w_vmem, rowbuf, sem):
        _load_once([(w_hbm, w_vmem)], sem)

        @pl.when(pl.program_id(0) == 0)
        def _():
            vo_ref[...] = jnp.zeros_like(vo_ref)

        dh1 = _dot_nt(dp_ref[...], w_vmem[...])
        xv = x_ref[...]
        r1 = _rms(xv)
        n1 = xv * r1
        gpre = _row(v_ref, V_GPRE1)
        sc = 1.0 + _row(v_ref, V_SC1)
        vo_ref[0:1, :] += _colsum(dh1)
        vo_ref[1:2, :] += _colsum(dh1 * n1 * gpre)
        vo_ref[2:3, :] += _colsum(dh1 * n1 * sc)
        _store_rows(dx_ref, dx1_ref[...] + _rms_bwd(dh1 * (gpre * sc), n1, r1), rowbuf, ts)

    tile = lambda w: pl.BlockSpec((ts, w), lambda i: (i, 0))
    return pl.pallas_call(
        body, name="bwd_in", grid=(s // ts,),
        in_specs=[tile(DIN), tile(D), tile(D), pl.BlockSpec((VD_ROWS, D), lambda i: (0, 0)),
                  pl.BlockSpec(memory_space=pl.ANY)] + exchange.specs_any,
        out_specs=[tile(D), pl.BlockSpec((8, D), lambda i: (0, 0))] + exchange.specs_any,
        out_shape=[jax.ShapeDtypeStruct((s, D), F32), jax.ShapeDtypeStruct((8, D), F32)] + exchange.out_shape,
        scratch_shapes=[pltpu.VMEM((D, DIN), BF16), pltpu.VMEM((D // LANES, ts, LANES), F32),
                        pltpu.SemaphoreType.DMA((1,))] + exchange.scratch,
        compiler_params=_params(("arbitrary",)),
    )(dproj, dx1, x, vec_d, w_in, *ex_grads)


def _dot_tn(a, b):
    return lax.dot_general(a, b, (((0,), (0,)), ((), ())), preferred_element_type=F32)


def _wgrad(a, b, tm, tn, ts, name, dtype, exchange=None, ex_grads=()):
    s, m = a.shape
    nn = b.shape[1]
    grid = (m // tm, nn // tn, s // ts)
    n = exchange.n if exchange else 0

    def body(*refs):
        a_ref, b_ref = refs[:2]
        grads = refs[2:2 + n]
        o_ref = refs[2 + n]
        recvs = refs[3 + n:3 + 2 * n]
        acc = refs[3 + 2 * n]
        sems = refs[4 + 2 * n:]
        i, j, k = pl.program_id(0), pl.program_id(1), pl.program_id(2)
        if exchange:
            pl.when((i == 0) & (j == 0) & (k == 0))(lambda: exchange.start(grads, recvs, *sems))
        part = _dot_tn(a_ref[...], b_ref[...])

        @pl.when(k == 0)
        def _():
            acc[...] = part

        @pl.when(k > 0)
        def _():
            acc[...] += part

        @pl.when(k == grid[2] - 1)
        def _():
            o_ref[...] = acc[...].astype(dtype)

        if exchange:
            pl.when((i == grid[0] - 1) & (j == grid[1] - 1) & (k == grid[2] - 1))(
                lambda: exchange.finish(grads, recvs, *sems))

    hosted = exchange.specs_any if exchange else []
    return pl.pallas_call(
        body, name=name, grid=grid,
        in_specs=[pl.BlockSpec((ts, tm), lambda i, j, k: (k, i)), pl.BlockSpec((ts, tn), lambda i, j, k: (k, j))]
        + hosted,
        out_specs=[pl.BlockSpec((tm, tn), lambda i, j, k: (i, j))] + hosted,
        out_shape=[jax.ShapeDtypeStruct((m, nn), dtype)] + (exchange.out_shape if exchange else []),
        scratch_shapes=[pltpu.VMEM((tm, tn), F32)] + (exchange.scratch if exchange else []),
        compiler_params=_params(("arbitrary", "arbitrary", "arbitrary")),
    )(a, b, *ex_grads)


def _wgrad_pool(pg, dya0, ts):
    s = pg.shape[0]
    nk = s // ts

    def body(a_ref, b_ref, o_ref, acc):
        k = pl.program_id(1)
        part = _dot_tn(a_ref[...], b_ref[...])

        @pl.when(k == 0)
        def _():
            acc[...] = part

        @pl.when(k > 0)
        def _():
            acc[...] += part

        @pl.when(k == nk - 1)
        def _():
            o_ref[0] = acc[...].astype(BF16)

    return pl.pallas_call(
        body, name="wgrad_pool", grid=(NG, nk),
        in_specs=[pl.BlockSpec((ts, GW), lambda g, k: (k, g)), pl.BlockSpec((ts, GW), lambda g, k: (k, g))],
        out_specs=pl.BlockSpec((1, GW, GW), lambda g, k: (g, 0, 0)),
        out_shape=jax.ShapeDtypeStruct((NG, GW, GW), BF16),
        scratch_shapes=[pltpu.VMEM((GW, GW), F32)],
        compiler_params=_params(("arbitrary", "arbitrary")),
    )(pg, dya0)


TS_PROJ = 512
TS_MIX = 256
TS_FFN = 256
TS_WGRAD = 1024


def _local_step(x, tgt, vec_d, vec_f, w_in, placed_rest):
    s = x.shape[0]
    tw = min(TS_WGRAD, s)
    sp_in, sp_pool, sp_bout, sp_o, sp_up, sp_down = SHARDED
    proj, h1, xs, w_pool, w_bout, w_o, w_up, w_down = _fwd_proj(x, vec_d, w_in, placed_rest, min(TS_PROJ, s))
    x1, o, pg, q, merged, ya0, yb, cv = _fwd_mix(proj, xs, vec_d, w_pool, w_bout, w_o, min(TS_MIX, s))
    up0, upc, a, h2, dx2, dff, vo_f, loss = _fwd_ffn(x1, tgt, vec_d, vec_f, w_up, w_down, min(TS_FFN, s))
    g_down, = _wgrad(a, dff, F // 2, D, tw, "wgrad_down", BF16)
    dx1, dup0, vo_b, fo, r_down = _bwd_ffn(dff, dx2, x1, up0, upc, vec_d, vec_f, w_up, w_down,
                                           _GradExchange([sp_down]), [g_down], min(TS_FFN, s))
    g_up, = _wgrad(h2, dup0, D, F2 // NCHIP, tw, "wgrad_up", BF16)
    dproj, do, dyb, dya0, vo_m, r_up = _bwd_mix(dx1, o, proj, cv, ya0, yb, vec_d, w_pool, w_bout, w_o,
                                                _GradExchange([sp_up]), [g_up], min(TS_MIX, s))
    g_o, = _wgrad(merged, do, D, D, tw, "wgrad_o", BF16)
    g_bout, = _wgrad(q, dyb, D, D, tw, "wgrad_bout", BF16)
    g_pool = _wgrad_pool(pg, dya0, tw)
    g_in, r_pool, r_bout, r_o = _wgrad(h1, dproj, D, DIN // NCHIP, tw, "wgrad_in", BF16,
                                       _GradExchange([sp_pool, sp_bout, sp_o]), [g_pool, g_bout, g_o])
    dx, vo_i, r_in = _bwd_in(dproj, dx1, xs, vec_d, w_in, _GradExchange([sp_in]), [g_in], min(TS_PROJ, s))
    vecs = dict(
        dsh1=vo_i[0], dsc1=vo_i[1], dg_pre_mix=vo_i[2],
        dgt1=vo_m[0], dg_post_mix=vo_m[1], dpool_scale=vo_m[2], dconv_b=vo_m[3],
        dconv_w=vo_m[4:7],
        dsh2=vo_b[0], dsc2=vo_b[1], dg_pre_ffn=vo_b[2],
        dgt2=vo_f[0], dg_post_ffn=vo_f[1],
        dffn_conv_w=fo[FV_W0:FV_W2 + 1], dffn_conv_b=fo[FV_B],
    )
    local = dict(w_in=g_in, w_pool=g_pool, w_bout=g_bout, w_o=g_o, w_up=g_up, w_down=g_down)
    received = dict(w_in=r_in, w_pool=r_pool, w_bout=r_bout, w_o=r_o, w_up=r_up, w_down=r_down)
    return loss, dx, vecs, local, received


def _aligned(offset, n):
    return offset if isinstance(offset, int) else pl.multiple_of(offset, n)


class _Sharded:
    def __init__(self, name, full_shape, shard_axis, half_axis):
        self.name = name
        self.full_shape = full_shape
        self.shard_axis = shard_axis
        self.half_axis = half_axis
        self.shard_shape = tuple(n // NCHIP if a == shard_axis else n for a, n in enumerate(full_shape))
        self.piece_shape = tuple(n // 2 if a == half_axis else n for a, n in enumerate(self.shard_shape))

    def piece(self, full_ref, k, h):
        idx = []
        for a, n in enumerate(self.piece_shape):
            if a == self.shard_axis and a == self.half_axis:
                idx.append(pl.ds(_aligned((2 * k + h) * n, n), n))
            elif a == self.shard_axis:
                idx.append(pl.ds(_aligned(k * n, n), n))
            elif a == self.half_axis:
                idx.append(pl.ds(_aligned(h * n, n), n))
            else:
                idx.append(slice(None))
        return full_ref.at[tuple(idx)]

    def shard(self, full_ref, k):
        n = self.shard_shape[self.shard_axis]
        idx = [pl.ds(_aligned(k * n, n), n) if a == self.shard_axis else slice(None)
               for a in range(len(self.full_shape))]
        return full_ref.at[tuple(idx)]

    def half(self, shard_ref, h):
        n = self.piece_shape[self.half_axis]
        idx = [pl.ds(_aligned(h * n, n), n) if a == self.half_axis else slice(None)
               for a in range(len(self.full_shape))]
        return shard_ref.at[tuple(idx)]

    def piece_block(self):
        def index_map(k, c_ref):
            c = c_ref[0]
            out = []
            for a in range(len(self.full_shape)):
                if a == self.shard_axis and a == self.half_axis:
                    out.append(2 * k + c)
                elif a == self.shard_axis:
                    out.append(k)
                elif a == self.half_axis:
                    out.append(c)
                else:
                    out.append(0)
            return tuple(out)
        return pl.BlockSpec(self.piece_shape, index_map)


SHARDED = (
    _Sharded("w_in", (D, DIN), 1, 0),
    _Sharded("w_pool", (NG, GW, GW), 1, 0),
    _Sharded("w_bout", (D, D), 0, 0),
    _Sharded("w_o", (D, D), 0, 0),
    _Sharded("w_up", (D, F2), 1, 0),
    _Sharded("w_down", (F, D), 0, 0),
)
NW = len(SHARDED)


def _mesh_place():
    x, y, c = lax.axis_index("x"), lax.axis_index("y"), lax.axis_index("c")
    chips = [(1 - x, y), (x, 1 - y), (1 - x, 1 - y)]
    return x, y, c, 2 * x + y, chips, [2 * px + py for px, py in chips]


def _remote(src, dst, send_sem, recv_sem, device):
    return pltpu.make_async_remote_copy(src_ref=src, dst_ref=dst, send_sem=send_sem, recv_sem=recv_sem,
                                        device_id=device, device_id_type=MESH)


def _all_gather_small(block, name):
    m_per, n = block.shape

    def body(x_ref, out_ref, send_sems, recv_sems, local_sem):
        x, y, c, _, chips, _ = _mesh_place()
        me, sibling = (x, y, c), (x, y, 1 - c)

        def rows(px, py, pc):
            return out_ref.at[pl.ds((4 * px + 2 * py + pc) * m_per, m_per), :]

        def copy(k, blk, to, src=None):
            return _remote(rows(*blk) if src is None else src, rows(*blk), send_sems.at[k], recv_sems.at[k], to)

        mine = pltpu.make_async_copy(x_ref, rows(*me), local_sem)
        mine.start()
        first = [copy(0, me, sibling, src=x_ref)]
        first += [copy(1 + j, me, (*chip, c), src=x_ref) for j, chip in enumerate(chips)]
        for cp in first:
            cp.start()
        passed = [copy(4 + j, (*chip, c), sibling) for j, chip in enumerate(chips)]
        for j, chip in enumerate(chips):
            copy(1 + j, (*chip, c), me).wait_recv()
            passed[j].start()
        copy(0, sibling, me).wait_recv()
        for j, chip in enumerate(chips):
            copy(4 + j, (*chip, 1 - c), me).wait_recv()
        for cp in first + passed:
            cp.wait_send()
        mine.wait()

    return pl.pallas_call(
        body, name=name,
        out_shape=jax.ShapeDtypeStruct((NDEV * m_per, n), block.dtype),
        in_specs=[pl.BlockSpec(memory_space=pltpu.VMEM)],
        out_specs=pl.BlockSpec(memory_space=pltpu.VMEM),
        scratch_shapes=[pltpu.SemaphoreType.DMA((7,)), pltpu.SemaphoreType.DMA((7,)), pltpu.SemaphoreType.DMA],
        compiler_params=pltpu.CompilerParams(vmem_limit_bytes=VMEM_LIMIT),
    )(block)


class _WeightGather:
    def __init__(self, specs):
        self.specs = specs
        self.n = len(specs)
        self.specs_any = [pl.BlockSpec(memory_space=pl.ANY)] * self.n
        self.out_shape = [jax.ShapeDtypeStruct(sp.full_shape, BF16) for sp in specs]
        self.scratch = [pltpu.SemaphoreType.DMA((6 * self.n,)), pltpu.SemaphoreType.DMA((6 * self.n,))]

    def _sends(self, outs, send_sems, recv_sems):
        x, y, c, k_me, chips, _ = _mesh_place()
        sends = []
        for j, chip in enumerate(chips):
            for w, sp in enumerate(self.specs):
                mine = sp.piece(outs[w], k_me, c)
                sends.append(_remote(mine, mine, send_sems.at[6 * w + j], recv_sems.at[6 * w + j], (*chip, c)))
        return sends

    def start(self, outs, send_sems, recv_sems):
        for cp in self._sends(outs, send_sems, recv_sems):
            cp.start()

    def finish(self, outs, send_sems, recv_sems):
        x, y, c, _, chips, kidx = _mesh_place()
        sibling = (x, y, 1 - c)
        passed = []
        for j, chip in enumerate(chips):
            for w, sp in enumerate(self.specs):
                landed = sp.piece(outs[w], kidx[j], c)
                _remote(landed, landed, send_sems.at[6 * w + j], recv_sems.at[6 * w + j], (*chip, c)).wait_recv()
                cp = _remote(landed, landed, send_sems.at[6 * w + 3 + j], recv_sems.at[6 * w + 3 + j], sibling)
                cp.start()
                passed.append(cp)
        for j in range(3):
            for w, sp in enumerate(self.specs):
                landed = sp.piece(outs[w], kidx[j], 1 - c)
                _remote(landed, landed, send_sems.at[6 * w + 3 + j], recv_sems.at[6 * w + 3 + j], sibling).wait_recv()
        for cp in self._sends(outs, send_sems, recv_sems) + passed:
            cp.wait_send()


def _gather_weights(placed, specs, name):
    gather = _WeightGather(specs)
    n = gather.n

    def body(*refs):
        outs, sems = refs[n:2 * n], refs[2 * n:]
        gather.start(outs, *sems)
        gather.finish(outs, *sems)

    return pl.pallas_call(
        body, name=name, out_shape=gather.out_shape, in_specs=gather.specs_any, out_specs=gather.specs_any,
        input_output_aliases={w: w for w in range(n)}, scratch_shapes=gather.scratch,
    )(*placed)


class _GradExchange:
    def __init__(self, specs):
        self.specs = specs
        self.n = len(specs)
        self.specs_any = [pl.BlockSpec(memory_space=pl.ANY)] * self.n
        self.out_shape = [jax.ShapeDtypeStruct((NDEV,) + sp.piece_shape, BF16) for sp in specs]
        self.scratch = [pltpu.SemaphoreType.DMA((7 * self.n,)), pltpu.SemaphoreType.DMA((NDEV * self.n,))]

    def _sends(self, grads, recvs, send_sems, recv_sems):
        x, y, c, k_me, chips, kidx = _mesh_place()
        dev = 2 * k_me + c
        sends = []
        for w, sp in enumerate(self.specs):
            slot, arrival = recvs[w].at[dev], recv_sems.at[NDEV * w + dev]
            sends.append(_remote(sp.piece(grads[w], k_me, 1 - c), slot, send_sems.at[7 * w], arrival, (x, y, 1 - c)))
            for j, chip in enumerate(chips):
                for h in range(2):
                    sends.append(_remote(sp.piece(grads[w], kidx[j], h), slot, send_sems.at[7 * w + 1 + 2 * j + h],
                                         arrival, (*chip, h)))
        return sends

    def start(self, grads, recvs, send_sems, recv_sems):
        for cp in self._sends(grads, recvs, send_sems, recv_sems):
            cp.start()

    def finish(self, grads, recvs, send_sems, recv_sems):
        x, y, c, k_me, _, _ = _mesh_place()
        dev = 2 * k_me + c
        for w in range(self.n):
            for d in range(NDEV):
                landed = recvs[w].at[d]
                arrival = _remote(landed, landed, send_sems.at[7 * w], recv_sems.at[NDEV * w + d], (x, y, c))
                pl.when(d != dev)(arrival.wait_recv)
        for cp in self._sends(grads, recvs, send_sems, recv_sems):
            cp.wait_send()


def _device_sum(sp, local, recv, place):
    nd = len(sp.piece_shape)

    def body(p_ref, a_ref, b_ref, o_ref):
        d = pl.program_id(0)
        term = jnp.where(d == p_ref[2], a_ref[...], b_ref[...]).astype(F32)

        @pl.when(d == 0)
        def _():
            o_ref[...] = term

        @pl.when(d > 0)
        def _():
            o_ref[...] += term

    def mine(d, p_ref):
        return tuple(2 * p_ref[0] + p_ref[1] if a == sp.shard_axis == sp.half_axis else
                     p_ref[0] if a == sp.shard_axis else p_ref[1] if a == sp.half_axis else 0 for a in range(nd))

    def others(d, p_ref):
        return (jnp.where(d == p_ref[2], (d + 1) % NDEV, d),) + (0,) * nd

    return pl.pallas_call(
        body, name="rs_device_sum_" + sp.name,
        grid_spec=pltpu.PrefetchScalarGridSpec(
            num_scalar_prefetch=1, grid=(NDEV,),
            in_specs=[pl.BlockSpec(sp.piece_shape, mine), pl.BlockSpec((None,) + sp.piece_shape, others)],
            out_specs=pl.BlockSpec(sp.piece_shape,
                                   lambda d, p_ref: tuple(p_ref[1] if a == sp.half_axis else 0 for a in range(nd)))),
        out_shape=jax.ShapeDtypeStruct(sp.shard_shape, F32),
        compiler_params=_params(("arbitrary",)),
    )(place, local, recv)


def _pair_exchange(grads, specs):
    n = len(specs)

    def body(*refs):
        ins, outs = refs[:n], refs[n:2 * n]
        send_sems, recv_sems = refs[2 * n:]
        x, y, c, _, _, _ = _mesh_place()
        sibling = (x, y, 1 - c)
        sent = []
        for w, sp in enumerate(specs):
            for k in range(NCHIP):
                cp = _remote(sp.piece(ins[w], k, 1 - c), outs[w].at[k],
                             send_sems.at[NCHIP * w + k], recv_sems.at[NCHIP * w + k], sibling)
                cp.start()
                sent.append(cp)
        for cp in sent:
            cp.wait_recv()
        for cp in sent:
            cp.wait_send()

    hbm = pl.BlockSpec(memory_space=pl.ANY)
    return pl.pallas_call(
        body, name="rs_pair_exchange",
        out_shape=[jax.ShapeDtypeStruct((NCHIP,) + sp.piece_shape, F32) for sp in specs],
        in_specs=[hbm] * n, out_specs=[hbm] * n,
        scratch_shapes=[pltpu.SemaphoreType.DMA((NCHIP * n,)), pltpu.SemaphoreType.DMA((NCHIP * n,))],
    )(*grads)


def _pair_sum(sp, grad, recv, core):
    nd = len(sp.piece_shape)

    def body(c_ref, g_ref, r_ref, o_ref):
        o_ref[...] = (g_ref[...] + r_ref[...]).astype(BF16)

    slot = pl.BlockSpec((None,) + sp.piece_shape, lambda k, c_ref: (k,) + (0,) * nd)
    return pl.pallas_call(
        body, name="rs_pair_sum_" + sp.name,
        grid_spec=pltpu.PrefetchScalarGridSpec(
            num_scalar_prefetch=1, grid=(NCHIP,),
            in_specs=[sp.piece_block(), slot], out_specs=slot),
        out_shape=jax.ShapeDtypeStruct((NCHIP,) + sp.piece_shape, BF16),
        compiler_params=_params(("parallel",)),
    )(core, grad, recv)


def _chip_exchange(parts, specs):
    n = len(specs)

    def body(*refs):
        ins, outs = refs[:n], refs[n:2 * n]
        send_sems, recv_sems = refs[2 * n:]
        x, y, c, k_me, chips, kidx = _mesh_place()
        sent = []
        for j, chip in enumerate(chips):
            for w in range(n):
                cp = _remote(ins[w].at[kidx[j]], outs[w].at[k_me], send_sems.at[3 * w + j], recv_sems.at[3 * w + j],
                             (*chip, c))
                cp.start()
                sent.append(cp)
        for j, chip in enumerate(chips):
            for w in range(n):
                landed = outs[w].at[kidx[j]]
                _remote(landed, landed, send_sems.at[3 * w + j], recv_sems.at[3 * w + j], (*chip, c)).wait_recv()
        for cp in sent:
            cp.wait_send()

    hbm = pl.BlockSpec(memory_space=pl.ANY)
    return pl.pallas_call(
        body, name="rs_chip_exchange",
        out_shape=[jax.ShapeDtypeStruct((NCHIP,) + sp.piece_shape, BF16) for sp in specs],
        in_specs=[hbm] * n, out_specs=[hbm] * n,
        scratch_shapes=[pltpu.SemaphoreType.DMA((3 * n,)), pltpu.SemaphoreType.DMA((3 * n,))],
    )(*parts)


def _chip_sum(sp, parts, recv, place):
    nd = len(sp.piece_shape)

    def body(p_ref, a_ref, b_ref, o_ref):
        k = pl.program_id(0)
        term = jnp.where(k == p_ref[0], a_ref[...], b_ref[...]).astype(F32)

        @pl.when(k == 0)
        def _():
            o_ref[...] = term

        @pl.when(k > 0)
        def _():
            o_ref[...] += term

    def others(k, p_ref):
        return (jnp.where(k == p_ref[0], (k + 1) % NCHIP, k),) + (0,) * nd

    return pl.pallas_call(
        body, name="rs_chip_sum_" + sp.name,
        grid_spec=pltpu.PrefetchScalarGridSpec(
            num_scalar_prefetch=1, grid=(NCHIP,),
            in_specs=[pl.BlockSpec((None,) + sp.piece_shape, lambda k, p_ref: (p_ref[0],) + (0,) * nd),
                      pl.BlockSpec((None,) + sp.piece_shape, others)],
            out_specs=pl.BlockSpec(sp.piece_shape,
                                   lambda k, p_ref: tuple(p_ref[1] if a == sp.half_axis else 0 for a in range(nd)))),
        out_shape=jax.ShapeDtypeStruct(sp.shard_shape, F32),
        compiler_params=_params(("arbitrary",)),
    )(place, parts, recv)


def _pair_share(halves):
    def body(*refs):
        outs = refs[NW:2 * NW]
        send_sems, recv_sems = refs[2 * NW:]
        x, y, c, _, _, _ = _mesh_place()
        sibling = (x, y, 1 - c)
        sent = []
        for w, sp in enumerate(SHARDED):
            mine = sp.half(outs[w], c)
            cp = _remote(mine, mine, send_sems.at[w], recv_sems.at[w], sibling)
            cp.start()
            sent.append(cp)
        for w, sp in enumerate(SHARDED):
            landed = sp.half(outs[w], 1 - c)
            _remote(landed, landed, send_sems.at[w], recv_sems.at[w], sibling).wait_recv()
        for cp in sent:
            cp.wait_send()

    hbm = pl.BlockSpec(memory_space=pl.ANY)
    return pl.pallas_call(
        body, name="rs_pair_share",
        out_shape=[jax.ShapeDtypeStruct(sp.shard_shape, F32) for sp in SHARDED],
        in_specs=[hbm] * NW, out_specs=[hbm] * NW,
        input_output_aliases={w: w for w in range(NW)},
        scratch_shapes=[pltpu.SemaphoreType.DMA((NW,)), pltpu.SemaphoreType.DMA((NW,))],
    )(*halves)


def _reduce_scatter(local, received, place):
    return _pair_share([_device_sum(sp, local[sp.name], received[sp.name], place) for sp in SHARDED])


def _place_bf16(sp, w, place):
    nd = len(sp.full_shape)

    def body(p_ref, w_ref, o_ref):
        o_ref[...] = w_ref[...].astype(BF16)

    return pl.pallas_call(
        body, name="place_" + sp.name,
        grid_spec=pltpu.PrefetchScalarGridSpec(
            num_scalar_prefetch=1, grid=(1,),
            in_specs=[pl.BlockSpec(sp.shard_shape, lambda i, p_ref: (0,) * nd)],
            out_specs=pl.BlockSpec(sp.shard_shape,
                                   lambda i, p_ref: tuple(p_ref[0] if a == sp.shard_axis else 0 for a in range(nd)))),
        out_shape=jax.ShapeDtypeStruct(sp.full_shape, BF16),
        compiler_params=_params(("arbitrary",)),
    )(place, w)


def _matmul_f32(a, b, name):
    def body(a_ref, b_ref, o_ref):
        o_ref[...] = jnp.dot(a_ref[...], b_ref[...], preferred_element_type=F32, precision=lax.Precision.HIGHEST)

    return pl.pallas_call(body, name=name, out_shape=jax.ShapeDtypeStruct((a.shape[0], b.shape[1]), F32),
                          compiler_params=pltpu.CompilerParams(vmem_limit_bytes=VMEM_LIMIT))(a, b)


def _sum_devices(stacked):
    def body(x_ref, o_ref):
        acc = x_ref[0]
        for d in range(1, NDEV):
            acc = acc + x_ref[d]
        o_ref[...] = acc

    return pl.pallas_call(body, name="sum_devices", out_shape=jax.ShapeDtypeStruct(stacked.shape[1:], F32),
                          compiler_params=pltpu.CompilerParams(vmem_limit_bytes=VMEM_LIMIT))(stacked)


def _adamw(w, g, m, v, name):
    r, cdim = w.shape
    tr = r if r <= 256 else (256 if r % 256 == 0 else r // 2)

    def body(w_ref, g_ref, m_ref, v_ref, d_ref, nm_ref, nv_ref):
        gv = g_ref[...]
        nm = ADAM_B1 * m_ref[...] + (1.0 - ADAM_B1) * gv
        nv = ADAM_B2 * v_ref[...] + (1.0 - ADAM_B2) * (gv * gv)
        m_hat = nm / (1.0 - ADAM_B1 ** ADAM_STEP)
        v_hat = nv / (1.0 - ADAM_B2 ** ADAM_STEP)
        d_ref[...] = -ADAM_LR * (m_hat / (jnp.sqrt(v_hat) + ADAM_EPS) + ADAM_WD * w_ref[...])
        nm_ref[...] = nm
        nv_ref[...] = nv

    blk = pl.BlockSpec((tr, cdim), lambda i: (i, 0))
    return pl.pallas_call(
        body, name="adamw_" + name, grid=(r // tr,), in_specs=[blk] * 4, out_specs=[blk] * 3,
        out_shape=[jax.ShapeDtypeStruct(w.shape, F32)] * 3,
        compiler_params=_params(("parallel",)),
    )(w, g, m, v)


WEIGHT_NAMES = ("g_pre_mix", "g_post_mix", "g_pre_ffn", "g_post_ffn", "w_ada", "b_ada", "w_in", "w_pool",
                "pool_scale", "conv_w", "conv_b", "w_bout", "w_o", "w_up", "ffn_conv_w", "ffn_conv_b", "w_down")
MATRIX_NAMES = ("w_ada",) + tuple(sp.name for sp in SHARDED)
VECTOR_NAMES = tuple(n for n in WEIGHT_NAMES if n not in MATRIX_NAMES)

CW = D // NCHIP
FCW = F2 // NCHIP
ADA_W = DIN // NCHIP
COND_BLOCK = (8, 768)
GRAD_BLOCK = (8, 4864)


def _flat_pad(parts, shape):
    flat = jnp.concatenate([p.reshape(-1) for p in parts])
    return jnp.pad(flat, (0, shape[0] * shape[1] - flat.shape[0])).reshape(shape)


def _take(flat, offset, shape):
    size = 1
    for n in shape:
        size *= n
    return flat[offset:offset + size].reshape(shape), offset + size


def kernel(x, c, g_pre_mix, g_post_mix, g_pre_ffn, g_post_ffn, w_ada, b_ada, w_in, w_pool, pool_scale, conv_w, conv_b, w_bout, w_o, w_up, ffn_conv_w, ffn_conv_b, w_down, loss_target, m_g_pre_mix, m_g_post_mix, m_g_pre_ffn, m_g_post_ffn, m_w_ada, m_b_ada, m_w_in, m_w_pool, m_pool_scale, m_conv_w, m_conv_b, m_w_bout, m_w_o, m_w_up, m_ffn_conv_w, m_ffn_conv_b, m_w_down, v_g_pre_mix, v_g_post_mix, v_g_pre_ffn, v_g_post_ffn, v_w_ada, v_b_ada, v_w_in, v_w_pool, v_pool_scale, v_conv_w, v_conv_b, v_w_bout, v_w_o, v_w_up, v_ffn_conv_w, v_ffn_conv_b, v_w_down):
    weights = dict(g_pre_mix=g_pre_mix, g_post_mix=g_post_mix, g_pre_ffn=g_pre_ffn, g_post_ffn=g_post_ffn,
                   w_ada=w_ada, b_ada=b_ada, w_in=w_in, w_pool=w_pool, pool_scale=pool_scale, conv_w=conv_w,
                   conv_b=conv_b, w_bout=w_bout, w_o=w_o, w_up=w_up, ffn_conv_w=ffn_conv_w, ffn_conv_b=ffn_conv_b,
                   w_down=w_down)
    mom1 = dict(g_pre_mix=m_g_pre_mix, g_post_mix=m_g_post_mix, g_pre_ffn=m_g_pre_ffn, g_post_ffn=m_g_post_ffn,
                w_ada=m_w_ada, b_ada=m_b_ada, w_in=m_w_in, w_pool=m_w_pool, pool_scale=m_pool_scale,
                conv_w=m_conv_w, conv_b=m_conv_b, w_bout=m_w_bout, w_o=m_w_o, w_up=m_w_up,
                ffn_conv_w=m_ffn_conv_w, ffn_conv_b=m_ffn_conv_b, w_down=m_w_down)
    mom2 = dict(g_pre_mix=v_g_pre_mix, g_post_mix=v_g_post_mix, g_pre_ffn=v_g_pre_ffn, g_post_ffn=v_g_post_ffn,
                w_ada=v_w_ada, b_ada=v_b_ada, w_in=v_w_in, w_pool=v_w_pool, pool_scale=v_pool_scale,
                conv_w=v_conv_w, conv_b=v_conv_b, w_bout=v_w_bout, w_o=v_w_o, w_up=v_w_up,
                ffn_conv_w=v_ffn_conv_w, ffn_conv_b=v_ffn_conv_b, w_down=v_w_down)

    chip = 2 * lax.axis_index("x") + lax.axis_index("y")
    core = lax.axis_index("c")
    dev = 2 * chip + core
    place = jnp.stack([chip, core, dev]).astype(jnp.int32)

    cond = _all_gather_small(_flat_pad([c, conv_w, ffn_conv_w], COND_BLOCK), "gather_cond")
    cond = cond.reshape(NDEV, -1)
    c_all = cond[:, :D]
    by_chip = cond[0::2]
    conv_w_full = by_chip[:, D:D + 3 * CW].reshape(NCHIP, 3, CW).transpose(1, 0, 2).reshape(3, D)
    ffn_w_full = by_chip[:, D + 3 * CW:D + 3 * CW + 3 * FCW].reshape(NCHIP, 3, FCW).transpose(1, 0, 2).reshape(3, F2)

    mod_cols = _all_gather_small(_matmul_f32(c_all, w_ada[0], "ada_mod"), "gather_mod")
    mod_cols = mod_cols.reshape(NDEV, NDEV, ADA_W)[0::2]
    mod = lax.dynamic_index_in_dim(mod_cols, dev, axis=1, keepdims=False).reshape(6, D) + b_ada.reshape(6, D)
    vec_d = jnp.concatenate([mod, g_pre_mix, g_post_mix, g_pre_ffn, g_post_ffn, pool_scale, conv_b, conv_w_full,
                             jnp.zeros((VD_ROWS - 15, D), F32)], axis=0)
    vec_f = jnp.concatenate([ffn_w_full, ffn_conv_b, jnp.zeros((FV_ROWS - 4, F2), F32)], axis=0)

    placed = [_place_bf16(sp, weights[sp.name][0], place) for sp in SHARDED]
    w_in_full, = _gather_weights(placed[:1], SHARDED[:1], "gather_w_in")
    loss_blk, dx, vecs, local, received = _local_step(x[0], loss_target[0], vec_d, vec_f, w_in_full, placed[1:])

    dmod = [vecs[n] for n in ("dsh1", "dsc1", "dgt1", "dsh2", "dsc2", "dgt2")]
    small = [vecs["dg_pre_mix"], vecs["dg_post_mix"], vecs["dg_pre_ffn"], vecs["dg_post_ffn"]] + dmod + [
        vecs["dpool_scale"], vecs["dconv_w"], vecs["dconv_b"], vecs["dffn_conv_w"], vecs["dffn_conv_b"],
        loss_blk[0]]
    gathered = _all_gather_small(_flat_pad(small, GRAD_BLOCK), "gather_vector_grads")
    total = _sum_devices(gathered.reshape((NDEV,) + GRAD_BLOCK)).reshape(-1)
    vgrad = {}
    off = 0
    for n in ("g_pre_mix", "g_post_mix", "g_pre_ffn", "g_post_ffn"):
        vgrad[n], off = _take(total, off, (1, D))
    dmod_off = off
    vgrad["b_ada"], off = _take(total, off, (1, DIN))
    vgrad["pool_scale"], off = _take(total, off, (1, D))
    g_conv_w, off = _take(total, off, (3, D))
    vgrad["conv_w"] = lax.dynamic_slice_in_dim(g_conv_w, chip * CW, CW, axis=1)[None]
    vgrad["conv_b"], off = _take(total, off, (1, D))
    g_ffn_w, off = _take(total, off, (3, F2))
    vgrad["ffn_conv_w"] = lax.dynamic_slice_in_dim(g_ffn_w, chip * FCW, FCW, axis=1)[None]
    vgrad["ffn_conv_b"], off = _take(total, off, (1, F2))
    loss = total[off]

    dmod_all = gathered.reshape(NDEV, -1)[:, dmod_off:dmod_off + DIN]
    dmod_cols = lax.dynamic_slice_in_dim(dmod_all, chip * ADA_W, ADA_W, axis=1)
    g_ada = _matmul_f32(jnp.pad(c_all.T, ((0, 0), (0, 128 - NDEV))), jnp.pad(dmod_cols, ((0, 128 - NDEV), (0, 0))),
                        "ada_wgrad")

    reduced = _reduce_scatter(local, received, place)
    mgrad = {"w_ada": g_ada}
    for sp, g in zip(SHARDED, reduced):
        mgrad[sp.name] = g

    grad, delta, new_m, new_v = {}, {}, {}, {}
    for n in MATRIX_NAMES:
        shape = weights[n].shape
        two_d = (-1, shape[-1])
        d, nm, nv = _adamw(weights[n].reshape(two_d), mgrad[n].reshape(two_d), mom1[n].reshape(two_d),
                           mom2[n].reshape(two_d), n)
        grad[n], delta[n], new_m[n], new_v[n] = (a.reshape(shape) for a in (mgrad[n], d, nm, nv))
    flat = lambda tree: jnp.concatenate([tree[n].reshape(1, -1) for n in VECTOR_NAMES], axis=1)
    d, nm, nv = _adamw(flat(weights), flat(vgrad), flat(mom1), flat(mom2), "vectors")
    off = 0
    for n in VECTOR_NAMES:
        shape = weights[n].shape
        grad[n] = vgrad[n].reshape(shape)
        delta[n], _ = _take(d[0], off, shape)
        new_m[n], _ = _take(nm[0], off, shape)
        new_v[n], off = _take(nv[0], off, shape)

    return (loss, dx[None], *[grad[n] for n in WEIGHT_NAMES], *[delta[n] for n in WEIGHT_NAMES],
            *[new_m[n] for n in WEIGHT_NAMES], *[new_v[n] for n in WEIGHT_NAMES])
```

```python
import jax
import jax.numpy as jnp
from jax import lax
from jax.experimental import pallas as pl
from jax.experimental.pallas import tpu as pltpu

F32 = jnp.float32
BF16 = jnp.bfloat16

D = 1024
DIN = 6 * D
F = 2816
F2 = 2 * F
NG = 4
GW = D // NG
POOL_CARRY = 16
CONV_CARRY = 3
EPS = 1e-6
NCHIP = 4
NDEV = 8

ADAM_LR = 0.001
ADAM_B1 = 0.9
ADAM_B2 = 0.999
ADAM_EPS = 1e-08
ADAM_WD = 0.01
ADAM_STEP = 10

VMEM_LIMIT = 60 * 1024 * 1024

(V_SH1, V_SC1, V_GT1, V_SH2, V_SC2, V_GT2, V_GPRE1, V_GPOST1, V_GPRE2, V_GPOST2,
 V_PSCALE, V_CB, V_CW0, V_CW1, V_CW2) = range(15)
VD_ROWS = 16
FV_W0, FV_W1, FV_W2, FV_B = range(4)
FV_ROWS = 8

MESH = pl.DeviceIdType.MESH


def _params(sem=None, vmem=VMEM_LIMIT):
    return pltpu.CompilerParams(dimension_semantics=sem, vmem_limit_bytes=vmem)


def _row(ref, r):
    return ref[r:r + 1, :]


def _load_once(pairs, sem):
    @pl.when(pl.program_id(0) == 0)
    def _():
        copies = [pltpu.make_async_copy(src, dst, sem.at[n]) for n, (src, dst) in enumerate(pairs)]
        for cp in copies:
            cp.start()
        for cp in copies:
            cp.wait()


def _dot(a, b):
    return jnp.dot(a, b, preferred_element_type=F32)


def _dot_nt(a, b):
    return lax.dot_general(a, b, (((1,), (1,)), ((), ())), preferred_element_type=F32)


BLK = 256
SEG = BLK // 8


LANES = 128


def _load_rows(ref, scr, ts):
    nc = ref.shape[-1] // LANES
    for c in range(nc):
        scr[c] = ref[:, c * LANES:(c + 1) * LANES]
    rows = [jnp.concatenate([scr[c, pl.ds(b * BLK + j, 8, stride=SEG), :] for c in range(nc)], axis=1)
            for b in range(ts // BLK) for j in range(SEG)]
    return jnp.concatenate(rows, axis=0)


def _store_rows(ref, val, scr, ts):
    nc = ref.shape[-1] // LANES
    for c in range(nc):
        scr[c] = val[:, c * LANES:(c + 1) * LANES]
    for b in range(ts // BLK):
        for r in range(8):
            for q in range(SEG // 8):
                t = b * BLK + r * SEG + 8 * q
                ref[t:t + 8, :] = jnp.concatenate(
                    [scr[c, pl.ds(b * BLK + 64 * q + r, 8, stride=8), :] for c in range(nc)], axis=1)


def _times(t0):
    p = lax.broadcasted_iota(jnp.int32, (BLK, 1), 0)
    return t0 + (p & 7) * SEG + (p >> 3)


def _before(x, carry, s):
    x3 = x.reshape(SEG, 8, x.shape[-1])
    tail = pltpu.roll(x3[SEG - s:], 1, 1)
    row = lax.broadcasted_iota(jnp.int32, tail.shape, 1)
    out = jnp.concatenate([jnp.where(row == 0, carry, tail), x3[:SEG - s]], axis=0)
    return out.reshape(x.shape), tail


def _after(x, carry, s):
    x3 = x.reshape(SEG, 8, x.shape[-1])
    head = pltpu.roll(x3[:s], 7, 1)
    row = lax.broadcasted_iota(jnp.int32, head.shape, 1)
    out = jnp.concatenate([x3[s:], jnp.where(row == 7, carry, head)], axis=0)
    return out.reshape(x.shape), head


def _causal_conv(x, carry, cols, w0, w1, w2, b):
    x1, carry[0:1, :, cols] = _before(x, carry[0:1, :, cols], 1)
    x2, carry[1:3, :, cols] = _before(x, carry[1:3, :, cols], 2)
    return b + w2 * x + w1 * x1 + w0 * x2


def _causal_conv_bwd(dy, carry, cols, w0, w1, w2):
    d1, carry[0:1, :, cols] = _after(dy, carry[0:1, :, cols], 1)
    d2, carry[1:3, :, cols] = _after(dy, carry[1:3, :, cols], 2)
    return w2 * dy + w1 * d1 + w0 * d2, d1, d2


def _pool_counts(t0, g):
    return jnp.minimum((_times(t0) + 1).astype(F32), float(2 << g))


def _rms(x):
    return lax.rsqrt(jnp.mean(x * x, axis=-1, keepdims=True) + EPS)


def _rms_bwd(dn, n, r):
    return r * (dn - n * jnp.mean(dn * n, axis=-1, keepdims=True))


def _colsum(x):
    return jnp.sum(x, axis=0, keepdims=True)


def _gelu_and_grad(x):
    k = 0.7978845608028654
    inner = k * (x + 0.044715 * (x * x * x))
    th = jnp.tanh(inner)
    gelu = 0.5 * x * (1.0 + th)
    dgelu = 0.5 * (1.0 + th) + 0.5 * x * (1.0 - th * th) * (k * (1.0 + 3.0 * 0.044715 * (x * x)))
    return gelu, dgelu


def _fwd_proj(x, vec_d, placed_in, placed_rest, place, ts):
    s = x.shape[0]
    nt = s // ts
    cw = DIN // NCHIP
    sp_in = SHARDED[0]
    gather = _WeightGather(SHARDED[1:4])
    n = gather.n

    def body(*refs):
        p_ref, x_ref, v_ref = refs[:3]
        proj_ref, h1_ref, xs_ref, w_full = refs[4 + n:8 + n]
        rest = refs[8 + n:8 + 2 * n]
        w_vmem, h1_all, rowbuf, sem, in_send, in_recv, send_sems, recv_sems = refs[8 + 2 * n:]
        j, i = pl.program_id(0), pl.program_id(1)
        x_, y_, c, k_me, _, _ = _mesh_place()
        sibling = (x_, y_, 1 - c)

        def peer(t):
            return (x_ ^ (t >> 1), y_ ^ (t & 1))

        def w_in_sends():
            mine = sp_in.piece(w_full, k_me, c)
            return [_remote(mine, mine, in_send.at[t - 1], in_recv.at[t - 1], (*peer(t), c)) for t in (1, 2, 3)]

        def load_block(k):
            cp = pltpu.make_async_copy(sp_in.shard(w_full, k), w_vmem.at[k], sem.at[0])
            cp.start()
            cp.wait()

        @pl.when((j == 0) & (i == 0))
        def _():
            for cp in w_in_sends():
                cp.start()
            gather.start(rest, send_sems, recv_sems)
            load_block(k_me)

        for t in (1, 2, 3):
            @pl.when((j == t) & (i == 0))
            def _(t=t):
                k = k_me ^ t
                landed = sp_in.piece(w_full, k, c)
                _remote(landed, landed, in_send.at[t - 1], in_recv.at[t - 1], (*peer(t), c)).wait_recv()
                _remote(landed, landed, in_send.at[2 + t], in_recv.at[2 + t], sibling).start()
                other = sp_in.piece(w_full, k, 1 - c)
                _remote(other, other, in_send.at[2 + t], in_recv.at[2 + t], sibling).wait_recv()
                load_block(k)

        @pl.when(j == 0)
        def _():
            xv = _load_rows(x_ref, rowbuf, ts)
            xs_ref[...] = xv
            n1 = xv * _rms(xv)
            h = n1 * (_row(v_ref, V_GPRE1) * (1.0 + _row(v_ref, V_SC1))) + _row(v_ref, V_SH1)
            hb = h.astype(BF16)
            h1_ref[...] = hb
            h1_all[i] = hb

        proj_ref[...] = _dot(h1_all[i], w_vmem[k_me ^ j]).astype(BF16)

        @pl.when((j == NCHIP - 1) & (i == nt - 1))
        def _():
            for cp in w_in_sends():
                cp.wait_send()
            for t in (1, 2, 3):
                landed = sp_in.piece(w_full, k_me ^ t, c)
                _remote(landed, landed, in_send.at[2 + t], in_recv.at[2 + t], sibling).wait_send()
            gather.finish(rest, send_sems, recv_sems)

    once = lambda w: pl.BlockSpec((ts, w), lambda j, i, p: (jnp.where(j == 0, i, nt - 1), 0))
    return pl.pallas_call(
        body, name="fwd_proj",
        grid_spec=pltpu.PrefetchScalarGridSpec(
            num_scalar_prefetch=1, grid=(NCHIP, nt),
            in_specs=[once(D), pl.BlockSpec((VD_ROWS, D), lambda j, i, p: (0, 0)),
                      pl.BlockSpec(memory_space=pl.ANY)] + gather.specs_any,
            out_specs=[pl.BlockSpec((ts, cw), lambda j, i, p: (i, p[0] ^ j)), once(D), once(D),
                       pl.BlockSpec(memory_space=pl.ANY)] + gather.specs_any,
            scratch_shapes=[pltpu.VMEM((NCHIP, D, cw), BF16), pltpu.VMEM((nt, ts, D), BF16),
                            pltpu.VMEM((D // LANES, ts, LANES), F32), pltpu.SemaphoreType.DMA((1,)),
                            pltpu.SemaphoreType.DMA((6,)), pltpu.SemaphoreType.DMA((6,))] + gather.scratch),
        out_shape=[jax.ShapeDtypeStruct((s, DIN), BF16), jax.ShapeDtypeStruct((s, D), BF16),
                   jax.ShapeDtypeStruct((s, D), F32), jax.ShapeDtypeStruct(sp_in.full_shape, BF16)] + gather.out_shape,
        input_output_aliases={3 + w: 3 + w for w in range(n + 1)},
        compiler_params=_params(("arbitrary", "arbitrary")),
    )(place, x, vec_d, placed_in, *placed_rest)


def _fwd_mix(proj, x, vec_d, w_pool, w_bout, w_o, placed_ffn, ts):
    s = x.shape[0]
    gather = _WeightGather(SHARDED[4:])
    n = gather.n

    def body(*refs):
        ins, outs, rest = refs[:6], refs[6 + n:14 + n], refs[14 + n:14 + 2 * n]
        scratch, sems = refs[14 + 2 * n:-2], refs[-2:]
        i = pl.program_id(0)
        pl.when(i == 0)(lambda: gather.start(rest, *sems))
        compute(*ins, *outs, *scratch)
        pl.when(i == s // ts - 1)(lambda: gather.finish(rest, *sems))

    def compute(p_ref, x_ref, v_ref, wp_hbm, wb_hbm, wo_hbm,
                x1_ref, o_ref, pg_ref, q_ref, mg_ref, ya_ref, yb_ref, cv_ref,
                wp, wb, wo, carry_p, carry_v, sem):
        i = pl.program_id(0)
        _load_once([(wp_hbm, wp), (wb_hbm, wb), (wo_hbm, wo)], sem)

        @pl.when(i == 0)
        def _():
            carry_p[...] = jnp.zeros_like(carry_p)
            carry_v[...] = jnp.zeros_like(carry_v)

        t0 = i * ts
        for g in range(NG):
            cols = slice(g * GW, (g + 1) * GW)
            u = p_ref[:, cols].astype(F32)
            e = u
            for l in range(g + 1):
                slot = slice((1 << l) - 1, (2 << l) - 1)
                shifted, carry_p[slot, :, cols] = _before(e, carry_p[slot, :, cols], 1 << l)
                e = e + shifted
            pgb = (e / _pool_counts(t0, g) - u).astype(BF16)
            pg_ref[:, cols] = pgb
            ya_ref[:, cols] = _dot(pgb, wp[g]).astype(BF16)

        u_x = p_ref[:, D:2 * D].astype(F32)
        u_c = p_ref[:, 3 * D:4 * D].astype(F32)
        v = u_c * u_x
        cv = _causal_conv(v, carry_v, slice(None), _row(v_ref, V_CW0), _row(v_ref, V_CW1),
                          _row(v_ref, V_CW2), _row(v_ref, V_CB))
        cv_ref[...] = cv.astype(BF16)
        q = (p_ref[:, 2 * D:3 * D].astype(F32) * cv).astype(BF16)
        q_ref[...] = q
        y_b = _dot(q, wb[...])
        yb_ref[...] = y_b.astype(BF16)

        y_a = ya_ref[...].astype(F32) * _row(v_ref, V_PSCALE)
        merged = (jax.nn.sigmoid(p_ref[:, 4 * D:5 * D].astype(F32)) * y_a
                  + jax.nn.sigmoid(p_ref[:, 5 * D:6 * D].astype(F32)) * y_b).astype(BF16)
        mg_ref[...] = merged
        o = _dot(merged, wo[...])
        o_ref[...] = o
        x1_ref[...] = x_ref[...] + _row(v_ref, V_GT1) * ((o * _rms(o)) * _row(v_ref, V_GPOST1))

    tile = lambda w: pl.BlockSpec((ts, w), lambda i: (i, 0))
    hbm = pl.BlockSpec(memory_space=pl.ANY)
    return pl.pallas_call(
        body, name="fwd_mix", grid=(s // ts,),
        in_specs=[tile(DIN), tile(D), pl.BlockSpec((VD_ROWS, D), lambda i: (0, 0)), hbm, hbm, hbm] + gather.specs_any,
        out_specs=[tile(D)] * 8 + gather.specs_any,
        out_shape=[jax.ShapeDtypeStruct((s, D), F32), jax.ShapeDtypeStruct((s, D), F32)]
        + [jax.ShapeDtypeStruct((s, D), BF16)] * 6 + gather.out_shape,
        input_output_aliases={6 + w: 8 + w for w in range(n)},
        scratch_shapes=[pltpu.VMEM((NG, GW, GW), BF16), pltpu.VMEM((D, D), BF16), pltpu.VMEM((D, D), BF16),
                        pltpu.VMEM((POOL_CARRY, 8, D), F32), pltpu.VMEM((CONV_CARRY, 8, D), F32),
                        pltpu.SemaphoreType.DMA((3,))] + gather.scratch,
        compiler_params=_params(("arbitrary",)),
    )(proj, x, vec_d, w_pool, w_bout, w_o, *placed_ffn)


def _fwd_ffn(x1, tgt, vec_d, vec_f, w_up, w_down, ts):
    s = x1.shape[0]
    hw = F // FFN_PASSES

    def body(x1_ref, t_ref, v_ref, f_ref, wu_hbm, wd_hbm,
             up_ref, upc_ref, a_ref, h2_ref, dx2_ref, dff_ref, vo_ref, loss_ref,
             wu, wd, carry, rowbuf, sem):
        i = pl.program_id(0)
        _load_once([(wu_hbm, wu), (wd_hbm, wd)], sem)

        @pl.when(i == 0)
        def _():
            carry[...] = jnp.zeros_like(carry)
            vo_ref[...] = jnp.zeros_like(vo_ref)
            loss_ref[...] = jnp.zeros_like(loss_ref)

        x1v = x1_ref[...]
        n3 = x1v * _rms(x1v)
        h2 = (n3 * (_row(v_ref, V_GPRE2) * (1.0 + _row(v_ref, V_SC2))) + _row(v_ref, V_SH2)).astype(BF16)
        h2_ref[...] = h2

        ff = jnp.zeros((ts, D), F32)
        for p in range(FFN_PASSES):
            up = []
            for cols in (slice(p * hw, (p + 1) * hw), slice(F + p * hw, F + (p + 1) * hw)):
                u0 = _dot(h2, wu[:, cols])
                up_ref[:, cols] = u0.astype(BF16)
                y = _causal_conv(u0, carry, cols, f_ref[FV_W0:FV_W0 + 1, cols], f_ref[FV_W1:FV_W1 + 1, cols],
                                 f_ref[FV_W2:FV_W2 + 1, cols], f_ref[FV_B:FV_B + 1, cols])
                upc_ref[:, cols] = y.astype(BF16)
                up.append(y)
            gelu, _ = _gelu_and_grad(up[0])
            a = (gelu * up[1]).astype(BF16)
            a_ref[:, p * hw:(p + 1) * hw] = a
            ff = ff + _dot(a, wd[p * hw:(p + 1) * hw, :])

        r4 = _rms(ff)
        n4 = ff * r4
        gt2 = _row(v_ref, V_GT2)
        gpost = _row(v_ref, V_GPOST2)
        y4 = n4 * gpost
        diff = (x1v + gt2 * y4) - _load_rows(t_ref, rowbuf, ts)
        loss_ref[...] += jnp.full(loss_ref.shape, 0.5 / D * jnp.sum(diff * diff), F32)
        dx2 = diff * (1.0 / D)
        dx2_ref[...] = dx2
        dy4 = dx2 * gt2
        vo_ref[0:1, :] += _colsum(dx2 * y4)
        vo_ref[1:2, :] += _colsum(dy4 * n4)
        dff_ref[...] = _rms_bwd(dy4 * gpost, n4, r4).astype(BF16)

    tile = lambda w: pl.BlockSpec((ts, w), lambda i: (i, 0))
    full = lambda r, w: pl.BlockSpec((r, w), lambda i: (0, 0))
    hbm = pl.BlockSpec(memory_space=pl.ANY)
    return pl.pallas_call(
        body, name="fwd_ffn", grid=(s // ts,),
        in_specs=[tile(D), tile(D), full(VD_ROWS, D), full(FV_ROWS, F2), hbm, hbm],
        out_specs=[tile(F2), tile(F2), tile(F), tile(D), tile(D), tile(D), full(8, D), full(8, 128)],
        out_shape=[jax.ShapeDtypeStruct((s, F2), BF16), jax.ShapeDtypeStruct((s, F2), BF16),
                   jax.ShapeDtypeStruct((s, F), BF16),
                   jax.ShapeDtypeStruct((s, D), BF16), jax.ShapeDtypeStruct((s, D), F32),
                   jax.ShapeDtypeStruct((s, D), BF16), jax.ShapeDtypeStruct((8, D), F32),
                   jax.ShapeDtypeStruct((8, 128), F32)],
        scratch_shapes=[pltpu.VMEM((D, F2), BF16), pltpu.VMEM((F, D), BF16), pltpu.VMEM((CONV_CARRY, 8, F2), F32),
                        pltpu.VMEM((D // LANES, ts, LANES), F32), pltpu.SemaphoreType.DMA((2,))],
        compiler_params=_params(("arbitrary",)),
    )(x1, tgt, vec_d, vec_f, w_up, w_down)


def _bwd_ffn(dff, dx2, x1, up0, upc, vec_d, vec_f, w_up, w_down, exchange, ex_grads, ts):
    s = x1.shape[0]
    nt = s // ts
    hw = F // FFN_PASSES
    n = exchange.n

    def body(*refs):
        ins, grads = refs[:9], refs[9:9 + n]
        outs, recvs = refs[9 + n:13 + n], refs[13 + n:13 + 2 * n]
        scratch, sems = refs[13 + 2 * n:-2], refs[-2:]
        i = pl.program_id(0)
        pl.when(i == 0)(lambda: exchange.start(grads, recvs, *sems))
        compute(*ins, *outs, *scratch)
        pl.when(i == nt - 1)(lambda: exchange.finish(grads, recvs, *sems))

    def compute(dff_ref, dx2_ref, x1_ref, up_ref, upc_ref, v_ref, f_ref, wu_hbm, wd_hbm,
                dx1_ref, dup_ref, vo_ref, fo_ref, wu, wd, carry, sem):
        i = pl.program_id(0)
        _load_once([(wu_hbm, wu), (wd_hbm, wd)], sem)

        @pl.when(i == 0)
        def _():
            carry[...] = jnp.zeros_like(carry)
            vo_ref[...] = jnp.zeros_like(vo_ref)
            fo_ref[...] = jnp.zeros_like(fo_ref)

        dffb = dff_ref[...]

        def down_t(p):
            return _dot_nt(dffb, wd[p * hw:(p + 1) * hw, :])

        dh2 = jnp.zeros((ts, D), F32)
        ahead = down_t(0)
        for p in range(FFN_PASSES):
            da, ahead = ahead, (down_t(p + 1) if p + 1 < FFN_PASSES else None)
            slabs = (slice(p * hw, (p + 1) * hw), slice(F + p * hw, F + (p + 1) * hw))
            gelu, dgelu = _gelu_and_grad(upc_ref[:, slabs[0]].astype(F32))
            dups = (da * upc_ref[:, slabs[1]].astype(F32) * dgelu, da * gelu)
            for cols, dup in zip(slabs, dups):
                du0, d1, d2 = _causal_conv_bwd(dup, carry, cols, f_ref[FV_W0:FV_W0 + 1, cols],
                                               f_ref[FV_W1:FV_W1 + 1, cols], f_ref[FV_W2:FV_W2 + 1, cols])
                u0 = up_ref[:, cols].astype(F32)
                fo_ref[FV_B:FV_B + 1, cols] += _colsum(dup)
                fo_ref[FV_W2:FV_W2 + 1, cols] += _colsum(dup * u0)
                fo_ref[FV_W1:FV_W1 + 1, cols] += _colsum(d1 * u0)
                fo_ref[FV_W0:FV_W0 + 1, cols] += _colsum(d2 * u0)
                du0 = du0.astype(BF16)
                dup_ref[:, cols] = du0
                dh2 = dh2 + _dot_nt(du0, wu[:, cols])

        x1v = x1_ref[...]
        r3 = _rms(x1v)
        n3 = x1v * r3
        gpre = _row(v_ref, V_GPRE2)
        sc = 1.0 + _row(v_ref, V_SC2)
        vo_ref[0:1, :] += _colsum(dh2)
        vo_ref[1:2, :] += _colsum(dh2 * n3 * gpre)
        vo_ref[2:3, :] += _colsum(dh2 * n3 * sc)
        dx1_ref[...] = dx2_ref[...] + _rms_bwd(dh2 * (gpre * sc), n3, r3)

    rev = lambda w: pl.BlockSpec((ts, w), lambda i: (nt - 1 - i, 0))
    full = lambda r, w: pl.BlockSpec((r, w), lambda i: (0, 0))
    hbm = pl.BlockSpec(memory_space=pl.ANY)
    return pl.pallas_call(
        body, name="bwd_ffn", grid=(nt,),
        in_specs=[rev(D), rev(D), rev(D), rev(F2), rev(F2), full(VD_ROWS, D), full(FV_ROWS, F2), hbm, hbm]
        + exchange.specs_any,
        out_specs=[rev(D), rev(F2), full(8, D), full(FV_ROWS, F2)] + exchange.specs_any,
        out_shape=[jax.ShapeDtypeStruct((s, D), F32), jax.ShapeDtypeStruct((s, F2), BF16),
                   jax.ShapeDtypeStruct((8, D), F32), jax.ShapeDtypeStruct((FV_ROWS, F2), F32)] + exchange.out_shape,
        scratch_shapes=[pltpu.VMEM((D, F2), BF16), pltpu.VMEM((F, D), BF16), pltpu.VMEM((CONV_CARRY, 8, F2), F32),
                        pltpu.SemaphoreType.DMA((2,))] + exchange.scratch,
        compiler_params=_params(("arbitrary",)),
    )(dff, dx2, x1, up0, upc, vec_d, vec_f, w_up, w_down, *ex_grads)


def _bwd_mix(dx1, o, proj, cv, ya0, yb, vec_d, w_pool, w_bout, w_o, exchange, ex_grads, ts):
    s = dx1.shape[0]
    nt = s // ts
    n = exchange.n

    def body(*refs):
        ins, grads = refs[:10], refs[10:10 + n]
        outs, recvs = refs[10 + n:15 + n], refs[15 + n:15 + 2 * n]
        scratch, sems = refs[15 + 2 * n:-2], refs[-2:]
        i = pl.program_id(0)
        pl.when(i == 0)(lambda: exchange.start(grads, recvs, *sems))
        compute(*ins, *outs, *scratch)
        pl.when(i == nt - 1)(lambda: exchange.finish(grads, recvs, *sems))

    def compute(dx1_ref, o_ref, p_ref, cv_ref, ya_ref, yb_ref, v_ref, wp_hbm, wb_hbm, wo_hbm,
                dp_ref, do_ref, dyb_ref, dya_ref, vo_ref, wp, wb, wo, carry_d, carry_e, sem):
        i = pl.program_id(0)
        _load_once([(wp_hbm, wp), (wb_hbm, wb), (wo_hbm, wo)], sem)

        @pl.when(i == 0)
        def _():
            carry_d[...] = jnp.zeros_like(carry_d)
            carry_e[...] = jnp.zeros_like(carry_e)
            vo_ref[...] = jnp.zeros_like(vo_ref)

        t0 = (nt - 1 - i) * ts
        dx1v = dx1_ref[...]
        ov = o_ref[...]
        r2 = _rms(ov)
        n2 = ov * r2
        gpost = _row(v_ref, V_GPOST1)
        vo_ref[0:1, :] += _colsum(dx1v * (n2 * gpost))
        dy2 = dx1v * _row(v_ref, V_GT1)
        vo_ref[1:2, :] += _colsum(dy2 * n2)
        dob = _rms_bwd(dy2 * gpost, n2, r2).astype(BF16)
        do_ref[...] = dob
        dmerged = _dot_nt(dob, wo[...])

        ya0 = ya_ref[...].astype(F32)
        pscale = _row(v_ref, V_PSCALE)
        sa = jax.nn.sigmoid(p_ref[:, 4 * D:5 * D].astype(F32))
        dp_ref[:, 4 * D:5 * D] = (dmerged * (ya0 * pscale) * sa * (1.0 - sa)).astype(BF16)
        dy_a = dmerged * sa
        vo_ref[2:3, :] += _colsum(dy_a * ya0)
        dya0 = (dy_a * pscale).astype(BF16)
        dya_ref[...] = dya0

        sb = jax.nn.sigmoid(p_ref[:, 5 * D:6 * D].astype(F32))
        dp_ref[:, 5 * D:6 * D] = (dmerged * yb_ref[...].astype(F32) * sb * (1.0 - sb)).astype(BF16)
        dy_b = (dmerged * sb).astype(BF16)
        dyb_ref[...] = dy_b
        dq = _dot_nt(dy_b, wb[...])

        u_x = p_ref[:, D:2 * D].astype(F32)
        u_b = p_ref[:, 2 * D:3 * D].astype(F32)
        u_c = p_ref[:, 3 * D:4 * D].astype(F32)
        w0, w1, w2 = _row(v_ref, V_CW0), _row(v_ref, V_CW1), _row(v_ref, V_CW2)
        dp_ref[:, 2 * D:3 * D] = (dq * cv_ref[...].astype(F32)).astype(BF16)
        dcv = dq * u_b
        dv, d1, d2 = _causal_conv_bwd(dcv, carry_d, slice(None), w0, w1, w2)
        v = u_c * u_x
        vo_ref[3:4, :] += _colsum(dcv)
        vo_ref[4:5, :] += _colsum(d2 * v)
        vo_ref[5:6, :] += _colsum(d1 * v)
        vo_ref[6:7, :] += _colsum(dcv * v)
        dp_ref[:, D:2 * D] = (dv * u_c).astype(BF16)
        dp_ref[:, 3 * D:4 * D] = (dv * u_x).astype(BF16)

        for g in range(NG):
            cols = slice(g * GW, (g + 1) * GW)
            dpg = _dot_nt(dya0[:, cols], wp[g])
            e = dpg / _pool_counts(t0, g)
            for l in range(g + 1):
                slot = slice((1 << l) - 1, (2 << l) - 1)
                shifted, carry_e[slot, :, cols] = _after(e, carry_e[slot, :, cols], 1 << l)
                e = e + shifted
            dp_ref[:, cols] = (e - dpg).astype(BF16)

    rev = lambda w: pl.BlockSpec((ts, w), lambda i: (nt - 1 - i, 0))
    hbm = pl.BlockSpec(memory_space=pl.ANY)
    return pl.pallas_call(
        body, name="bwd_mix", grid=(nt,),
        in_specs=[rev(D), rev(D), rev(DIN), rev(D), rev(D), rev(D), pl.BlockSpec((VD_ROWS, D), lambda i: (0, 0)),
                  hbm, hbm, hbm] + exchange.specs_any,
        out_specs=[rev(DIN), rev(D), rev(D), rev(D), pl.BlockSpec((8, D), lambda i: (0, 0))] + exchange.specs_any,
        out_shape=[jax.ShapeDtypeStruct((s, DIN), BF16)] + [jax.ShapeDtypeStruct((s, D), BF16)] * 3
        + [jax.ShapeDtypeStruct((8, D), F32)] + exchange.out_shape,
        scratch_shapes=[pltpu.VMEM((NG, GW, GW), BF16), pltpu.VMEM((D, D), BF16), pltpu.VMEM((D, D), BF16),
                        pltpu.VMEM((CONV_CARRY, 8, D), F32), pltpu.VMEM((POOL_CARRY, 8, D), F32),
                        pltpu.SemaphoreType.DMA((3,))] + exchange.scratch,
        compiler_params=_params(("arbitrary",)),
    )(dx1, o, proj, cv, ya0, yb, vec_d, w_pool, w_bout, w_o, *ex_grads)


def _bwd_in(dproj, dx1, x, vec_d, w_in, exchange, ex_grads, ts):
    s = x.shape[0]
    nt = s // ts
    n = exchange.n

    def body(*refs):
        ins, grads = refs[:5], refs[5:5 + n]
        outs, recvs = refs[5 + n:7 + n], refs[7 + n:7 + 2 * n]
        scratch, sems = refs[7 + 2 * n:-2], refs[-2:]
        i = pl.program_id(0)
        pl.when(i == 0)(lambda: exchange.start(grads, recvs, *sems))
        compute(*ins, *outs, *scratch)
        pl.when(i == nt - 1)(lambda: exchange.finish(grads, recvs, *sems))

    def compute(dp_ref, dx1_ref, x_ref, v_ref, w_hbm, dx_ref, vo_ref, w_vmem, rowbuf, sem):
        _load_once([(w_hbm, w_vmem)], sem)

        @pl.when(pl.program_id(0) == 0)
        def _():
            vo_ref[...] = jnp.zeros_like(vo_ref)

        dh1 = _dot_nt(dp_ref[...], w_vmem[...])
        xv = x_ref[...]
        r1 = _rms(xv)
        n1 = xv * r1
        gpre = _row(v_ref, V_GPRE1)
        sc = 1.0 + _row(v_ref, V_SC1)
        vo_ref[0:1, :] += _colsum(dh1)
        vo_ref[1:2, :] += _colsum(dh1 * n1 * gpre)
        vo_ref[2:3, :] += _colsum(dh1 * n1 * sc)
        _store_rows(dx_ref, dx1_ref[...] + _rms_bwd(dh1 * (gpre * sc), n1, r1), rowbuf, ts)

    tile = lambda w: pl.BlockSpec((ts, w), lambda i: (i, 0))
    return pl.pallas_call(
        body, name="bwd_in", grid=(s // ts,),
        in_specs=[tile(DIN), tile(D), tile(D), pl.BlockSpec((VD_ROWS, D), lambda i: (0, 0)),
                  pl.BlockSpec(memory_space=pl.ANY)] + exchange.specs_any,
        out_specs=[tile(D), pl.BlockSpec((8, D), lambda i: (0, 0))] + exchange.specs_any,
        out_shape=[jax.ShapeDtypeStruct((s, D), F32), jax.ShapeDtypeStruct((8, D), F32)] + exchange.out_shape,
        scratch_shapes=[pltpu.VMEM((D, DIN), BF16), pltpu.VMEM((D // LANES, ts, LANES), F32),
                        pltpu.SemaphoreType.DMA((1,))] + exchange.scratch,
        compiler_params=_params(("arbitrary",)),
    )(dproj, dx1, x, vec_d, w_in, *ex_grads)


def _dot_tn(a, b):
    return lax.dot_general(a, b, (((0,), (0,)), ((), ())), preferred_element_type=F32)


def _wgrad(a, b, tm, tn, ts, name, dtype, exchange=None, ex_grads=()):
    s, m = a.shape
    nn = b.shape[1]
    grid = (m // tm, nn // tn, s // ts)
    n = exchange.n if exchange else 0

    def body(*refs):
        a_ref, b_ref = refs[:2]
        grads = refs[2:2 + n]
        o_ref = refs[2 + n]
        recvs = refs[3 + n:3 + 2 * n]
        acc = refs[3 + 2 * n]
        sems = refs[4 + 2 * n:]
        i, j, k = pl.program_id(0), pl.program_id(1), pl.program_id(2)
        if exchange:
            pl.when((i == 0) & (j == 0) & (k == 0))(lambda: exchange.start(grads, recvs, *sems))
        part = _dot_tn(a_ref[...], b_ref[...])

        @pl.when(k == 0)
        def _():
            acc[...] = part

        @pl.when(k > 0)
        def _():
            acc[...] += part

        @pl.when(k == grid[2] - 1)
        def _():
            o_ref[...] = acc[...].astype(dtype)

        if exchange:
            pl.when((i == grid[0] - 1) & (j == grid[1] - 1) & (k == grid[2] - 1))(
                lambda: exchange.finish(grads, recvs, *sems))

    hosted = exchange.specs_any if exchange else []
    return pl.pallas_call(
        body, name=name, grid=grid,
        in_specs=[pl.BlockSpec((ts, tm), lambda i, j, k: (k, i)), pl.BlockSpec((ts, tn), lambda i, j, k: (k, j))]
        + hosted,
        out_specs=[pl.BlockSpec((tm, tn), lambda i, j, k: (i, j))] + hosted,
        out_shape=[jax.ShapeDtypeStruct((m, nn), dtype)] + (exchange.out_shape if exchange else []),
        scratch_shapes=[pltpu.VMEM((tm, tn), F32)] + (exchange.scratch if exchange else []),
        compiler_params=_params(("arbitrary", "arbitrary", "arbitrary")),
    )(a, b, *ex_grads)


def _wgrad_pool(pg, dya0, ts):
    s = pg.shape[0]
    nk = s // ts

    def body(a_ref, b_ref, o_ref, acc):
        k = pl.program_id(1)
        part = _dot_tn(a_ref[...], b_ref[...])

        @pl.when(k == 0)
        def _():
            acc[...] = part

        @pl.when(k > 0)
        def _():
            acc[...] += part

        @pl.when(k == nk - 1)
        def _():
            o_ref[0] = acc[...].astype(BF16)

    return pl.pallas_call(
        body, name="wgrad_pool", grid=(NG, nk),
        in_specs=[pl.BlockSpec((ts, GW), lambda g, k: (k, g)), pl.BlockSpec((ts, GW), lambda g, k: (k, g))],
        out_specs=pl.BlockSpec((1, GW, GW), lambda g, k: (g, 0, 0)),
        out_shape=jax.ShapeDtypeStruct((NG, GW, GW), BF16),
        scratch_shapes=[pltpu.VMEM((GW, GW), F32)],
        compiler_params=_params(("arbitrary", "arbitrary")),
    )(pg, dya0)


FFN_PASSES = 2
TS_PROJ = 512
TS_MIX = 256
TS_FFN = 256
TS_WGRAD = 1024


def _local_step(x, tgt, vec_d, vec_f, placed, place):
    s = x.shape[0]
    tw = min(TS_WGRAD, s)
    sp_in, sp_pool, sp_bout, sp_o, sp_up, sp_down = SHARDED
    proj, h1, xs, w_in, w_pool, w_bout, w_o = _fwd_proj(x, vec_d, placed[0], placed[1:4], place, min(TS_PROJ, s))
    x1, o, pg, q, merged, ya0, yb, cv, w_up, w_down = _fwd_mix(proj, xs, vec_d, w_pool, w_bout, w_o, placed[4:],
                                                               min(TS_MIX, s))
    up0, upc, a, h2, dx2, dff, vo_f, loss = _fwd_ffn(x1, tgt, vec_d, vec_f, w_up, w_down, min(TS_FFN, s))
    g_down, = _wgrad(a, dff, F // 2, D, tw, "wgrad_down", BF16)
    dx1, dup0, vo_b, fo, r_down = _bwd_ffn(dff, dx2, x1, up0, upc, vec_d, vec_f, w_up, w_down,
                                           _GradExchange([sp_down]), [g_down], min(TS_FFN, s))
    g_up, = _wgrad(h2, dup0, D, F2 // NCHIP, tw, "wgrad_up", BF16)
    dproj, do, dyb, dya0, vo_m, r_up = _bwd_mix(dx1, o, proj, cv, ya0, yb, vec_d, w_pool, w_bout, w_o,
                                                _GradExchange([sp_up]), [g_up], min(TS_MIX, s))
    g_o, = _wgrad(merged, do, D, D, tw, "wgrad_o", BF16)
    g_bout, = _wgrad(q, dyb, D, D, tw, "wgrad_bout", BF16)
    g_pool = _wgrad_pool(pg, dya0, tw)
    g_in, r_pool, r_bout, r_o = _wgrad(h1, dproj, D, DIN // NCHIP, tw, "wgrad_in", BF16,
                                       _GradExchange([sp_pool, sp_bout, sp_o]), [g_pool, g_bout, g_o])
    dx, vo_i, r_in = _bwd_in(dproj, dx1, xs, vec_d, w_in, _GradExchange([sp_in]), [g_in], min(TS_PROJ, s))
    vecs = dict(
        dsh1=vo_i[0], dsc1=vo_i[1], dg_pre_mix=vo_i[2],
        dgt1=vo_m[0], dg_post_mix=vo_m[1], dpool_scale=vo_m[2], dconv_b=vo_m[3],
        dconv_w=vo_m[4:7],
        dsh2=vo_b[0], dsc2=vo_b[1], dg_pre_ffn=vo_b[2],
        dgt2=vo_f[0], dg_post_ffn=vo_f[1],
        dffn_conv_w=fo[FV_W0:FV_W2 + 1], dffn_conv_b=fo[FV_B],
    )
    local = dict(w_in=g_in, w_pool=g_pool, w_bout=g_bout, w_o=g_o, w_up=g_up, w_down=g_down)
    received = dict(w_in=r_in, w_pool=r_pool, w_bout=r_bout, w_o=r_o, w_up=r_up, w_down=r_down)
    return loss, dx, vecs, local, received


def _aligned(offset, n):
    return offset if isinstance(offset, int) else pl.multiple_of(offset, n)


class _Sharded:
    def __init__(self, name, full_shape, shard_axis, half_axis):
        self.name = name
        self.full_shape = full_shape
        self.shard_axis = shard_axis
        self.half_axis = half_axis
        self.shard_shape = tuple(n // NCHIP if a == shard_axis else n for a, n in enumerate(full_shape))
        self.piece_shape = tuple(n // 2 if a == half_axis else n for a, n in enumerate(self.shard_shape))

    def piece(self, full_ref, k, h):
        idx = []
        for a, n in enumerate(self.piece_shape):
            if a == self.shard_axis and a == self.half_axis:
                idx.append(pl.ds(_aligned((2 * k + h) * n, n), n))
            elif a == self.shard_axis:
                idx.append(pl.ds(_aligned(k * n, n), n))
            elif a == self.half_axis:
                idx.append(pl.ds(_aligned(h * n, n), n))
            else:
                idx.append(slice(None))
        return full_ref.at[tuple(idx)]

    def shard(self, full_ref, k):
        n = self.shard_shape[self.shard_axis]
        idx = [pl.ds(_aligned(k * n, n), n) if a == self.shard_axis else slice(None)
               for a in range(len(self.full_shape))]
        return full_ref.at[tuple(idx)]

    def half(self, shard_ref, h):
        n = self.piece_shape[self.half_axis]
        idx = [pl.ds(_aligned(h * n, n), n) if a == self.half_axis else slice(None)
               for a in range(len(self.full_shape))]
        return shard_ref.at[tuple(idx)]

    def piece_block(self):
        def index_map(k, c_ref):
            c = c_ref[0]
            out = []
            for a in range(len(self.full_shape)):
                if a == self.shard_axis and a == self.half_axis:
                    out.append(2 * k + c)
                elif a == self.shard_axis:
                    out.append(k)
                elif a == self.half_axis:
                    out.append(c)
                else:
                    out.append(0)
            return tuple(out)
        return pl.BlockSpec(self.piece_shape, index_map)


SHARDED = (
    _Sharded("w_in", (D, DIN), 1, 0),
    _Sharded("w_pool", (NG, GW, GW), 1, 0),
    _Sharded("w_bout", (D, D), 0, 0),
    _Sharded("w_o", (D, D), 0, 0),
    _Sharded("w_up", (D, F2), 1, 0),
    _Sharded("w_down", (F, D), 0, 0),
)
NW = len(SHARDED)


def _mesh_place():
    x, y, c = lax.axis_index("x"), lax.axis_index("y"), lax.axis_index("c")
    chips = [(1 - x, y), (x, 1 - y), (1 - x, 1 - y)]
    return x, y, c, 2 * x + y, chips, [2 * px + py for px, py in chips]


def _remote(src, dst, send_sem, recv_sem, device):
    return pltpu.make_async_remote_copy(src_ref=src, dst_ref=dst, send_sem=send_sem, recv_sem=recv_sem,
                                        device_id=device, device_id_type=MESH)


def _all_gather_small(block, name):
    m_per, n = block.shape

    def body(x_ref, out_ref, send_sems, recv_sems, local_sem):
        x, y, c, _, chips, _ = _mesh_place()
        me, sibling = (x, y, c), (x, y, 1 - c)

        def rows(px, py, pc):
            return out_ref.at[pl.ds((4 * px + 2 * py + pc) * m_per, m_per), :]

        def copy(k, blk, to, src=None):
            return _remote(rows(*blk) if src is None else src, rows(*blk), send_sems.at[k], recv_sems.at[k], to)

        mine = pltpu.make_async_copy(x_ref, rows(*me), local_sem)
        mine.start()
        first = [copy(0, me, sibling, src=x_ref)]
        first += [copy(1 + j, me, (*chip, c), src=x_ref) for j, chip in enumerate(chips)]
        for cp in first:
            cp.start()
        passed = [copy(4 + j, (*chip, c), sibling) for j, chip in enumerate(chips)]
        for j, chip in enumerate(chips):
            copy(1 + j, (*chip, c), me).wait_recv()
            passed[j].start()
        copy(0, sibling, me).wait_recv()
        for j, chip in enumerate(chips):
            copy(4 + j, (*chip, 1 - c), me).wait_recv()
        for cp in first + passed:
            cp.wait_send()
        mine.wait()

    return pl.pallas_call(
        body, name=name,
        out_shape=jax.ShapeDtypeStruct((NDEV * m_per, n), block.dtype),
        in_specs=[pl.BlockSpec(memory_space=pltpu.VMEM)],
        out_specs=pl.BlockSpec(memory_space=pltpu.VMEM),
        scratch_shapes=[pltpu.SemaphoreType.DMA((7,)), pltpu.SemaphoreType.DMA((7,)), pltpu.SemaphoreType.DMA],
        compiler_params=pltpu.CompilerParams(vmem_limit_bytes=VMEM_LIMIT),
    )(block)


class _WeightGather:
    def __init__(self, specs):
        self.specs = specs
        self.n = len(specs)
        self.specs_any = [pl.BlockSpec(memory_space=pl.ANY)] * self.n
        self.out_shape = [jax.ShapeDtypeStruct(sp.full_shape, BF16) for sp in specs]
        self.scratch = [pltpu.SemaphoreType.DMA((6 * self.n,)), pltpu.SemaphoreType.DMA((6 * self.n,))]

    def _sends(self, outs, send_sems, recv_sems):
        x, y, c, k_me, chips, _ = _mesh_place()
        sends = []
        for j, chip in enumerate(chips):
            for w, sp in enumerate(self.specs):
                mine = sp.piece(outs[w], k_me, c)
                sends.append(_remote(mine, mine, send_sems.at[6 * w + j], recv_sems.at[6 * w + j], (*chip, c)))
        return sends

    def start(self, outs, send_sems, recv_sems):
        for cp in self._sends(outs, send_sems, recv_sems):
            cp.start()

    def finish(self, outs, send_sems, recv_sems):
        x, y, c, _, chips, kidx = _mesh_place()
        sibling = (x, y, 1 - c)
        passed = []
        for j, chip in enumerate(chips):
            for w, sp in enumerate(self.specs):
                landed = sp.piece(outs[w], kidx[j], c)
                _remote(landed, landed, send_sems.at[6 * w + j], recv_sems.at[6 * w + j], (*chip, c)).wait_recv()
                cp = _remote(landed, landed, send_sems.at[6 * w + 3 + j], recv_sems.at[6 * w + 3 + j], sibling)
                cp.start()
                passed.append(cp)
        for j in range(3):
            for w, sp in enumerate(self.specs):
                landed = sp.piece(outs[w], kidx[j], 1 - c)
                _remote(landed, landed, send_sems.at[6 * w + 3 + j], recv_sems.at[6 * w + 3 + j], sibling).wait_recv()
        for cp in self._sends(outs, send_sems, recv_sems) + passed:
            cp.wait_send()


def _gather_weights(placed, specs, name):
    gather = _WeightGather(specs)
    n = gather.n

    def body(*refs):
        outs, sems = refs[n:2 * n], refs[2 * n:]
        gather.start(outs, *sems)
        gather.finish(outs, *sems)

    return pl.pallas_call(
        body, name=name, out_shape=gather.out_shape, in_specs=gather.specs_any, out_specs=gather.specs_any,
        input_output_aliases={w: w for w in range(n)}, scratch_shapes=gather.scratch,
    )(*placed)


class _GradExchange:
    def __init__(self, specs):
        self.specs = specs
        self.n = len(specs)
        self.specs_any = [pl.BlockSpec(memory_space=pl.ANY)] * self.n
        self.out_shape = [jax.ShapeDtypeStruct((NDEV,) + sp.piece_shape, BF16) for sp in specs]
        self.scratch = [pltpu.SemaphoreType.DMA((7 * self.n,)), pltpu.SemaphoreType.DMA((NDEV * self.n,))]

    def _sends(self, grads, recvs, send_sems, recv_sems):
        x, y, c, k_me, chips, kidx = _mesh_place()
        dev = 2 * k_me + c
        sends = []
        for w, sp in enumerate(self.specs):
            slot, arrival = recvs[w].at[dev], recv_sems.at[NDEV * w + dev]
            sends.append(_remote(sp.piece(grads[w], k_me, 1 - c), slot, send_sems.at[7 * w], arrival, (x, y, 1 - c)))
            for j, chip in enumerate(chips):
                for h in range(2):
                    sends.append(_remote(sp.piece(grads[w], kidx[j], h), slot, send_sems.at[7 * w + 1 + 2 * j + h],
                                         arrival, (*chip, h)))
        return sends

    def start(self, grads, recvs, send_sems, recv_sems):
        for cp in self._sends(grads, recvs, send_sems, recv_sems):
            cp.start()

    def finish(self, grads, recvs, send_sems, recv_sems):
        x, y, c, k_me, _, _ = _mesh_place()
        dev = 2 * k_me + c
        for w in range(self.n):
            for d in range(NDEV):
                landed = recvs[w].at[d]
                arrival = _remote(landed, landed, send_sems.at[7 * w], recv_sems.at[NDEV * w + d], (x, y, c))
                pl.when(d != dev)(arrival.wait_recv)
        for cp in self._sends(grads, recvs, send_sems, recv_sems):
            cp.wait_send()


def _device_sum(sp, local, recv, place):
    nd = len(sp.piece_shape)

    def body(p_ref, a_ref, b_ref, o_ref):
        d = pl.program_id(0)
        term = jnp.where(d == p_ref[2], a_ref[...], b_ref[...]).astype(F32)

        @pl.when(d == 0)
        def _():
            o_ref[...] = term

        @pl.when(d > 0)
        def _():
            o_ref[...] += term

    def mine(d, p_ref):
        return tuple(2 * p_ref[0] + p_ref[1] if a == sp.shard_axis == sp.half_axis else
                     p_ref[0] if a == sp.shard_axis else p_ref[1] if a == sp.half_axis else 0 for a in range(nd))

    def others(d, p_ref):
        return (jnp.where(d == p_ref[2], (d + 1) % NDEV, d),) + (0,) * nd

    return pl.pallas_call(
        body, name="rs_device_sum_" + sp.name,
        grid_spec=pltpu.PrefetchScalarGridSpec(
            num_scalar_prefetch=1, grid=(NDEV,),
            in_specs=[pl.BlockSpec(sp.piece_shape, mine), pl.BlockSpec((None,) + sp.piece_shape, others)],
            out_specs=pl.BlockSpec(sp.piece_shape,
                                   lambda d, p_ref: tuple(p_ref[1] if a == sp.half_axis else 0 for a in range(nd)))),
        out_shape=jax.ShapeDtypeStruct(sp.shard_shape, F32),
        compiler_params=_params(("arbitrary",)),
    )(place, local, recv)


def _pair_exchange(grads, specs):
    n = len(specs)

    def body(*refs):
        ins, outs = refs[:n], refs[n:2 * n]
        send_sems, recv_sems = refs[2 * n:]
        x, y, c, _, _, _ = _mesh_place()
        sibling = (x, y, 1 - c)
        sent = []
        for w, sp in enumerate(specs):
            for k in range(NCHIP):
                cp = _remote(sp.piece(ins[w], k, 1 - c), outs[w].at[k],
                             send_sems.at[NCHIP * w + k], recv_sems.at[NCHIP * w + k], sibling)
                cp.start()
                sent.append(cp)
        for cp in sent:
            cp.wait_recv()
        for cp in sent:
            cp.wait_send()

    hbm = pl.BlockSpec(memory_space=pl.ANY)
    return pl.pallas_call(
        body, name="rs_pair_exchange",
        out_shape=[jax.ShapeDtypeStruct((NCHIP,) + sp.piece_shape, F32) for sp in specs],
        in_specs=[hbm] * n, out_specs=[hbm] * n,
        scratch_shapes=[pltpu.SemaphoreType.DMA((NCHIP * n,)), pltpu.SemaphoreType.DMA((NCHIP * n,))],
    )(*grads)


def _pair_sum(sp, grad, recv, core):
    nd = len(sp.piece_shape)

    def body(c_ref, g_ref, r_ref, o_ref):
        o_ref[...] = (g_ref[...] + r_ref[...]).astype(BF16)

    slot = pl.BlockSpec((None,) + sp.piece_shape, lambda k, c_ref: (k,) + (0,) * nd)
    return pl.pallas_call(
        body, name="rs_pair_sum_" + sp.name,
        grid_spec=pltpu.PrefetchScalarGridSpec(
            num_scalar_prefetch=1, grid=(NCHIP,),
            in_specs=[sp.piece_block(), slot], out_specs=slot),
        out_shape=jax.ShapeDtypeStruct((NCHIP,) + sp.piece_shape, BF16),
        compiler_params=_params(("parallel",)),
    )(core, grad, recv)


def _chip_exchange(parts, specs):
    n = len(specs)

    def body(*refs):
        ins, outs = refs[:n], refs[n:2 * n]
        send_sems, recv_sems = refs[2 * n:]
        x, y, c, k_me, chips, kidx = _mesh_place()
        sent = []
        for j, chip in enumerate(chips):
            for w in range(n):
                cp = _remote(ins[w].at[kidx[j]], outs[w].at[k_me], send_sems.at[3 * w + j], recv_sems.at[3 * w + j],
                             (*chip, c))
                cp.start()
                sent.append(cp)
        for j, chip in enumerate(chips):
            for w in range(n):
                landed = outs[w].at[kidx[j]]
                _remote(landed, landed, send_sems.at[3 * w + j], recv_sems.at[3 * w + j], (*chip, c)).wait_recv()
        for cp in sent:
            cp.wait_send()

    hbm = pl.BlockSpec(memory_space=pl.ANY)
    return pl.pallas_call(
        body, name="rs_chip_exchange",
        out_shape=[jax.ShapeDtypeStruct((NCHIP,) + sp.piece_shape, BF16) for sp in specs],
        in_specs=[hbm] * n, out_specs=[hbm] * n,
        scratch_shapes=[pltpu.SemaphoreType.DMA((3 * n,)), pltpu.SemaphoreType.DMA((3 * n,))],
    )(*parts)


def _chip_sum(sp, parts, recv, place):
    nd = len(sp.piece_shape)

    def body(p_ref, a_ref, b_ref, o_ref):
        k = pl.program_id(0)
        term = jnp.where(k == p_ref[0], a_ref[...], b_ref[...]).astype(F32)

        @pl.when(k == 0)
        def _():
            o_ref[...] = term

        @pl.when(k > 0)
        def _():
            o_ref[...] += term

    def others(k, p_ref):
        return (jnp.where(k == p_ref[0], (k + 1) % NCHIP, k),) + (0,) * nd

    return pl.pallas_call(
        body, name="rs_chip_sum_" + sp.name,
        grid_spec=pltpu.PrefetchScalarGridSpec(
            num_scalar_prefetch=1, grid=(NCHIP,),
            in_specs=[pl.BlockSpec((None,) + sp.piece_shape, lambda k, p_ref: (p_ref[0],) + (0,) * nd),
                      pl.BlockSpec((None,) + sp.piece_shape, others)],
            out_specs=pl.BlockSpec(sp.piece_shape,
                                   lambda k, p_ref: tuple(p_ref[1] if a == sp.half_axis else 0 for a in range(nd)))),
        out_shape=jax.ShapeDtypeStruct(sp.shard_shape, F32),
        compiler_params=_params(("arbitrary",)),
    )(place, parts, recv)


def _pair_share(halves):
    def body(*refs):
        outs = refs[NW:2 * NW]
        send_sems, recv_sems = refs[2 * NW:]
        x, y, c, _, _, _ = _mesh_place()
        sibling = (x, y, 1 - c)
        sent = []
        for w, sp in enumerate(SHARDED):
            mine = sp.half(outs[w], c)
            cp = _remote(mine, mine, send_sems.at[w], recv_sems.at[w], sibling)
            cp.start()
            sent.append(cp)
        for w, sp in enumerate(SHARDED):
            landed = sp.half(outs[w], 1 - c)
            _remote(landed, landed, send_sems.at[w], recv_sems.at[w], sibling).wait_recv()
        for cp in sent:
            cp.wait_send()

    hbm = pl.BlockSpec(memory_space=pl.ANY)
    return pl.pallas_call(
        body, name="rs_pair_share",
        out_shape=[jax.ShapeDtypeStruct(sp.shard_shape, F32) for sp in SHARDED],
        in_specs=[hbm] * NW, out_specs=[hbm] * NW,
        input_output_aliases={w: w for w in range(NW)},
        scratch_shapes=[pltpu.SemaphoreType.DMA((NW,)), pltpu.SemaphoreType.DMA((NW,))],
    )(*halves)


def _reduce_scatter(local, received, place):
    return _pair_share([_device_sum(sp, local[sp.name], received[sp.name], place) for sp in SHARDED])


def _place_bf16(sp, w, place):
    nd = len(sp.full_shape)

    def body(p_ref, w_ref, o_ref):
        o_ref[...] = w_ref[...].astype(BF16)

    return pl.pallas_call(
        body, name="place_" + sp.name,
        grid_spec=pltpu.PrefetchScalarGridSpec(
            num_scalar_prefetch=1, grid=(1,),
            in_specs=[pl.BlockSpec(sp.shard_shape, lambda i, p_ref: (0,) * nd)],
            out_specs=pl.BlockSpec(sp.shard_shape,
                                   lambda i, p_ref: tuple(p_ref[0] if a == sp.shard_axis else 0 for a in range(nd)))),
        out_shape=jax.ShapeDtypeStruct(sp.full_shape, BF16),
        compiler_params=_params(("arbitrary",)),
    )(place, w)


def _matmul_f32(a, b, name):
    def body(a_ref, b_ref, o_ref):
        o_ref[...] = jnp.dot(a_ref[...], b_ref[...], preferred_element_type=F32, precision=lax.Precision.HIGHEST)

    return pl.pallas_call(body, name=name, out_shape=jax.ShapeDtypeStruct((a.shape[0], b.shape[1]), F32),
                          compiler_params=pltpu.CompilerParams(vmem_limit_bytes=VMEM_LIMIT))(a, b)


def _sum_devices(stacked):
    def body(x_ref, o_ref):
        acc = x_ref[0]
        for d in range(1, NDEV):
            acc = acc + x_ref[d]
        o_ref[...] = acc

    return pl.pallas_call(body, name="sum_devices", out_shape=jax.ShapeDtypeStruct(stacked.shape[1:], F32),
                          compiler_params=pltpu.CompilerParams(vmem_limit_bytes=VMEM_LIMIT))(stacked)


def _adamw(w, g, m, v, name):
    r, cdim = w.shape
    tr = r if r <= 256 else (256 if r % 256 == 0 else r // 2)

    def body(w_ref, g_ref, m_ref, v_ref, d_ref, nm_ref, nv_ref):
        gv = g_ref[...]
        nm = ADAM_B1 * m_ref[...] + (1.0 - ADAM_B1) * gv
        nv = ADAM_B2 * v_ref[...] + (1.0 - ADAM_B2) * (gv * gv)
        m_hat = nm / (1.0 - ADAM_B1 ** ADAM_STEP)
        v_hat = nv / (1.0 - ADAM_B2 ** ADAM_STEP)
        d_ref[...] = -ADAM_LR * (m_hat / (jnp.sqrt(v_hat) + ADAM_EPS) + ADAM_WD * w_ref[...])
        nm_ref[...] = nm
        nv_ref[...] = nv

    blk = pl.BlockSpec((tr, cdim), lambda i: (i, 0))
    return pl.pallas_call(
        body, name="adamw_" + name, grid=(r // tr,), in_specs=[blk] * 4, out_specs=[blk] * 3,
        out_shape=[jax.ShapeDtypeStruct(w.shape, F32)] * 3,
        compiler_params=_params(("parallel",)),
    )(w, g, m, v)


WEIGHT_NAMES = ("g_pre_mix", "g_post_mix", "g_pre_ffn", "g_post_ffn", "w_ada", "b_ada", "w_in", "w_pool",
                "pool_scale", "conv_w", "conv_b", "w_bout", "w_o", "w_up", "ffn_conv_w", "ffn_conv_b", "w_down")
MATRIX_NAMES = ("w_ada",) + tuple(sp.name for sp in SHARDED)
VECTOR_NAMES = tuple(n for n in WEIGHT_NAMES if n not in MATRIX_NAMES)

CW = D // NCHIP
FCW = F2 // NCHIP
ADA_W = DIN // NCHIP
COND_BLOCK = (8, 768)
GRAD_BLOCK = (8, 4864)


def _flat_pad(parts, shape):
    flat = jnp.concatenate([p.reshape(-1) for p in parts])
    return jnp.pad(flat, (0, shape[0] * shape[1] - flat.shape[0])).reshape(shape)


def _take(flat, offset, shape):
    size = 1
    for n in shape:
        size *= n
    return flat[offset:offset + size].reshape(shape), offset + size


def kernel(x, c, g_pre_mix, g_post_mix, g_pre_ffn, g_post_ffn, w_ada, b_ada, w_in, w_pool, pool_scale, conv_w, conv_b, w_bout, w_o, w_up, ffn_conv_w, ffn_conv_b, w_down, loss_target, m_g_pre_mix, m_g_post_mix, m_g_pre_ffn, m_g_post_ffn, m_w_ada, m_b_ada, m_w_in, m_w_pool, m_pool_scale, m_conv_w, m_conv_b, m_w_bout, m_w_o, m_w_up, m_ffn_conv_w, m_ffn_conv_b, m_w_down, v_g_pre_mix, v_g_post_mix, v_g_pre_ffn, v_g_post_ffn, v_w_ada, v_b_ada, v_w_in, v_w_pool, v_pool_scale, v_conv_w, v_conv_b, v_w_bout, v_w_o, v_w_up, v_ffn_conv_w, v_ffn_conv_b, v_w_down):
    weights = dict(g_pre_mix=g_pre_mix, g_post_mix=g_post_mix, g_pre_ffn=g_pre_ffn, g_post_ffn=g_post_ffn,
                   w_ada=w_ada, b_ada=b_ada, w_in=w_in, w_pool=w_pool, pool_scale=pool_scale, conv_w=conv_w,
                   conv_b=conv_b, w_bout=w_bout, w_o=w_o, w_up=w_up, ffn_conv_w=ffn_conv_w, ffn_conv_b=ffn_conv_b,
                   w_down=w_down)
    mom1 = dict(g_pre_mix=m_g_pre_mix, g_post_mix=m_g_post_mix, g_pre_ffn=m_g_pre_ffn, g_post_ffn=m_g_post_ffn,
                w_ada=m_w_ada, b_ada=m_b_ada, w_in=m_w_in, w_pool=m_w_pool, pool_scale=m_pool_scale,
                conv_w=m_conv_w, conv_b=m_conv_b, w_bout=m_w_bout, w_o=m_w_o, w_up=m_w_up,
                ffn_conv_w=m_ffn_conv_w, ffn_conv_b=m_ffn_conv_b, w_down=m_w_down)
    mom2 = dict(g_pre_mix=v_g_pre_mix, g_post_mix=v_g_post_mix, g_pre_ffn=v_g_pre_ffn, g_post_ffn=v_g_post_ffn,
                w_ada=v_w_ada, b_ada=v_b_ada, w_in=v_w_in, w_pool=v_w_pool, pool_scale=v_pool_scale,
                conv_w=v_conv_w, conv_b=v_conv_b, w_bout=v_w_bout, w_o=v_w_o, w_up=v_w_up,
                ffn_conv_w=v_ffn_conv_w, ffn_conv_b=v_ffn_conv_b, w_down=v_w_down)

    chip = 2 * lax.axis_index("x") + lax.axis_index("y")
    core = lax.axis_index("c")
    dev = 2 * chip + core
    place = jnp.stack([chip, core, dev]).astype(jnp.int32)

    cond = _all_gather_small(_flat_pad([c, conv_w, ffn_conv_w], COND_BLOCK), "gather_cond")
    cond = cond.reshape(NDEV, -1)
    c_all = cond[:, :D]
    by_chip = cond[0::2]
    conv_w_full = by_chip[:, D:D + 3 * CW].reshape(NCHIP, 3, CW).transpose(1, 0, 2).reshape(3, D)
    ffn_w_full = by_chip[:, D + 3 * CW:D + 3 * CW + 3 * FCW].reshape(NCHIP, 3, FCW).transpose(1, 0, 2).reshape(3, F2)

    mod_cols = _all_gather_small(_matmul_f32(c_all, w_ada[0], "ada_mod"), "gather_mod")
    mod_cols = mod_cols.reshape(NDEV, NDEV, ADA_W)[0::2]
    mod = lax.dynamic_index_in_dim(mod_cols, dev, axis=1, keepdims=False).reshape(6, D) + b_ada.reshape(6, D)
    vec_d = jnp.concatenate([mod, g_pre_mix, g_post_mix, g_pre_ffn, g_post_ffn, pool_scale, conv_b, conv_w_full,
                             jnp.zeros((VD_ROWS - 15, D), F32)], axis=0)
    vec_f = jnp.concatenate([ffn_w_full, ffn_conv_b, jnp.zeros((FV_ROWS - 4, F2), F32)], axis=0)

    placed = [_place_bf16(sp, weights[sp.name][0], place) for sp in SHARDED]
    loss_blk, dx, vecs, local, received = _local_step(x[0], loss_target[0], vec_d, vec_f, placed, place)

    dmod = [vecs[n] for n in ("dsh1", "dsc1", "dgt1", "dsh2", "dsc2", "dgt2")]
    small = [vecs["dg_pre_mix"], vecs["dg_post_mix"], vecs["dg_pre_ffn"], vecs["dg_post_ffn"]] + dmod + [
        vecs["dpool_scale"], vecs["dconv_w"], vecs["dconv_b"], vecs["dffn_conv_w"], vecs["dffn_conv_b"],
        loss_blk[0]]
    gathered = _all_gather_small(_flat_pad(small, GRAD_BLOCK), "gather_vector_grads")
    total = _sum_devices(gathered.reshape((NDEV,) + GRAD_BLOCK)).reshape(-1)
    vgrad = {}
    off = 0
    for n in ("g_pre_mix", "g_post_mix", "g_pre_ffn", "g_post_ffn"):
        vgrad[n], off = _take(total, off, (1, D))
    dmod_off = off
    vgrad["b_ada"], off = _take(total, off, (1, DIN))
    vgrad["pool_scale"], off = _take(total, off, (1, D))
    g_conv_w, off = _take(total, off, (3, D))
    vgrad["conv_w"] = lax.dynamic_slice_in_dim(g_conv_w, chip * CW, CW, axis=1)[None]
    vgrad["conv_b"], off = _take(total, off, (1, D))
    g_ffn_w, off = _take(total, off, (3, F2))
    vgrad["ffn_conv_w"] = lax.dynamic_slice_in_dim(g_ffn_w, chip * FCW, FCW, axis=1)[None]
    vgrad["ffn_conv_b"], off = _take(total, off, (1, F2))
    loss = total[off]

    dmod_all = gathered.reshape(NDEV, -1)[:, dmod_off:dmod_off + DIN]
    dmod_cols = lax.dynamic_slice_in_dim(dmod_all, chip * ADA_W, ADA_W, axis=1)
    g_ada = _matmul_f32(jnp.pad(c_all.T, ((0, 0), (0, 128 - NDEV))), jnp.pad(dmod_cols, ((0, 128 - NDEV), (0, 0))),
                        "ada_wgrad")

    reduced = _reduce_scatter(local, received, place)
    mgrad = {"w_ada": g_ada}
    for sp, g in zip(SHARDED, reduced):
        mgrad[sp.name] = g

    grad, delta, new_m, new_v = {}, {}, {}, {}
    for n in MATRIX_NAMES:
        shape = weights[n].shape
        two_d = (-1, shape[-1])
        d, nm, nv = _adamw(weights[n].reshape(two_d), mgrad[n].reshape(two_d), mom1[n].reshape(two_d),
                           mom2[n].reshape(two_d), n)
        grad[n], delta[n], new_m[n], new_v[n] = (a.reshape(shape) for a in (mgrad[n], d, nm, nv))
    flat = lambda tree: jnp.concatenate([tree[n].reshape(1, -1) for n in VECTOR_NAMES], axis=1)
    d, nm, nv = _adamw(flat(weights), flat(vgrad), flat(mom1), flat(mom2), "vectors")
    off = 0
    for n in VECTOR_NAMES:
        shape = weights[n].shape
        grad[n] = vgrad[n].reshape(shape)
        delta[n], _ = _take(d[0], off, shape)
        new_m[n], _ = _take(nm[0], off, shape)
        new_v[n], off = _take(nv[0], off, shape)

    return (loss, dx[None], *[grad[n] for n in WEIGHT_NAMES], *[delta[n] for n in WEIGHT_NAMES],
            *[new_m[n] for n in WEIGHT_NAMES], *[new_v[n] for n in WEIGHT_NAMES])
```

```python
import jax
import jax.numpy as jnp
from jax import lax
from jax.experimental import pallas as pl
from jax.experimental.pallas import tpu as pltpu

F32 = jnp.float32
BF16 = jnp.bfloat16

D = 1024
DIN = 6 * D
F = 2816
F2 = 2 * F
NG = 4
GW = D // NG
POOL_CARRY = 16
CONV_CARRY = 3
EPS = 1e-6
NCHIP = 4
NDEV = 8

ADAM_LR = 0.001
ADAM_B1 = 0.9
ADAM_B2 = 0.999
ADAM_EPS = 1e-08
ADAM_WD = 0.01
ADAM_STEP = 10

VMEM_LIMIT = 60 * 1024 * 1024

(V_SH1, V_SC1, V_GT1, V_SH2, V_SC2, V_GT2, V_GPRE1, V_GPOST1, V_GPRE2, V_GPOST2,
 V_PSCALE, V_CB, V_CW0, V_CW1, V_CW2) = range(15)
VD_ROWS = 16
FV_W0, FV_W1, FV_W2, FV_B = range(4)
FV_ROWS = 8

MESH = pl.DeviceIdType.MESH


def _params(sem=None, vmem=VMEM_LIMIT):
    return pltpu.CompilerParams(dimension_semantics=sem, vmem_limit_bytes=vmem)


def _row(ref, r):
    return ref[r:r + 1, :]


def _load_once(pairs, sem):
    @pl.when(pl.program_id(0) == 0)
    def _():
        copies = [pltpu.make_async_copy(src, dst, sem.at[n]) for n, (src, dst) in enumerate(pairs)]
        for cp in copies:
            cp.start()
        for cp in copies:
            cp.wait()


def _dot(a, b):
    return jnp.dot(a, b, preferred_element_type=F32)


def _dot_nt(a, b):
    return lax.dot_general(a, b, (((1,), (1,)), ((), ())), preferred_element_type=F32)


BLK = 256
SEG = BLK // 8


def _load_rows(ref, ts):
    blocks = [jnp.swapaxes(ref[b * BLK:(b + 1) * BLK, :].reshape(8, SEG, ref.shape[-1]), 0, 1).reshape(BLK, -1)
              for b in range(ts // BLK)]
    return jnp.concatenate(blocks, axis=0)


def _store_rows(ref, val, ts):
    for b in range(ts // BLK):
        blk = val[b * BLK:(b + 1) * BLK, :].reshape(SEG, 8, val.shape[-1])
        ref[b * BLK:(b + 1) * BLK, :] = jnp.swapaxes(blk, 0, 1).reshape(BLK, -1)


def _times(t0):
    p = lax.broadcasted_iota(jnp.int32, (BLK, 1), 0)
    return t0 + (p & 7) * SEG + (p >> 3)


def _before(x, carry, s):
    x3 = x.reshape(SEG, 8, x.shape[-1])
    tail = pltpu.roll(x3[SEG - s:], 1, 1)
    row = lax.broadcasted_iota(jnp.int32, tail.shape, 1)
    out = jnp.concatenate([jnp.where(row == 0, carry, tail), x3[:SEG - s]], axis=0)
    return out.reshape(x.shape), tail


def _after(x, carry, s):
    x3 = x.reshape(SEG, 8, x.shape[-1])
    head = pltpu.roll(x3[:s], 7, 1)
    row = lax.broadcasted_iota(jnp.int32, head.shape, 1)
    out = jnp.concatenate([x3[s:], jnp.where(row == 7, carry, head)], axis=0)
    return out.reshape(x.shape), head


def _causal_conv(x, carry, cols, w0, w1, w2, b):
    x1, carry[0:1, :, cols] = _before(x, carry[0:1, :, cols], 1)
    x2, carry[1:3, :, cols] = _before(x, carry[1:3, :, cols], 2)
    return b + w2 * x + w1 * x1 + w0 * x2


def _causal_conv_bwd(dy, carry, cols, w0, w1, w2):
    d1, carry[0:1, :, cols] = _after(dy, carry[0:1, :, cols], 1)
    d2, carry[1:3, :, cols] = _after(dy, carry[1:3, :, cols], 2)
    return w2 * dy + w1 * d1 + w0 * d2, d1, d2


def _pool_counts(t0, g):
    return jnp.minimum((_times(t0) + 1).astype(F32), float(2 << g))


def _rms(x):
    return lax.rsqrt(jnp.mean(x * x, axis=-1, keepdims=True) + EPS)


def _rms_bwd(dn, n, r):
    return r * (dn - n * jnp.mean(dn * n, axis=-1, keepdims=True))


def _colsum(x):
    return jnp.sum(x, axis=0, keepdims=True)


def _gelu_and_grad(x):
    k = 0.7978845608028654
    inner = k * (x + 0.044715 * (x * x * x))
    th = jnp.tanh(inner)
    gelu = 0.5 * x * (1.0 + th)
    dgelu = 0.5 * (1.0 + th) + 0.5 * x * (1.0 - th * th) * (k * (1.0 + 3.0 * 0.044715 * (x * x)))
    return gelu, dgelu


def _fwd_proj(x, vec_d, placed_in, placed_rest, place, ts):
    s = x.shape[0]
    nt = s // ts
    cw = DIN // NCHIP
    sp_in = SHARDED[0]
    gather = _WeightGather(SHARDED[1:4])
    n = gather.n

    def body(*refs):
        p_ref, x_ref, v_ref = refs[:3]
        proj_ref, h1_ref, xs_ref, w_full = refs[4 + n:8 + n]
        rest = refs[8 + n:8 + 2 * n]
        w_vmem, h1_all, sem, in_send, in_recv, send_sems, recv_sems = refs[8 + 2 * n:]
        j, i = pl.program_id(0), pl.program_id(1)
        x_, y_, c, k_me, _, _ = _mesh_place()
        sibling = (x_, y_, 1 - c)

        def peer(t):
            return (x_ ^ (t >> 1), y_ ^ (t & 1))

        def w_in_sends():
            mine = sp_in.piece(w_full, k_me, c)
            return [_remote(mine, mine, in_send.at[t - 1], in_recv.at[t - 1], (*peer(t), c)) for t in (1, 2, 3)]

        def load_block(k):
            cp = pltpu.make_async_copy(sp_in.shard(w_full, k), w_vmem.at[k], sem.at[0])
            cp.start()
            cp.wait()

        @pl.when((j == 0) & (i == 0))
        def _():
            for cp in w_in_sends():
                cp.start()
            gather.start(rest, send_sems, recv_sems)
            load_block(k_me)

        for t in (1, 2, 3):
            @pl.when((j == t) & (i == 0))
            def _(t=t):
                k = k_me ^ t
                landed = sp_in.piece(w_full, k, c)
                _remote(landed, landed, in_send.at[t - 1], in_recv.at[t - 1], (*peer(t), c)).wait_recv()
                _remote(landed, landed, in_send.at[2 + t], in_recv.at[2 + t], sibling).start()
                other = sp_in.piece(w_full, k, 1 - c)
                _remote(other, other, in_send.at[2 + t], in_recv.at[2 + t], sibling).wait_recv()
                load_block(k)

        @pl.when(j == 0)
        def _():
            xv = _load_rows(x_ref, ts)
            xs_ref[...] = xv
            n1 = xv * _rms(xv)
            h = n1 * (_row(v_ref, V_GPRE1) * (1.0 + _row(v_ref, V_SC1))) + _row(v_ref, V_SH1)
            hb = h.astype(BF16)
            h1_ref[...] = hb
            h1_all[i] = hb

        proj_ref[...] = _dot(h1_all[i], w_vmem[k_me ^ j]).astype(BF16)

        @pl.when((j == NCHIP - 1) & (i == nt - 1))
        def _():
            for cp in w_in_sends():
                cp.wait_send()
            for t in (1, 2, 3):
                landed = sp_in.piece(w_full, k_me ^ t, c)
                _remote(landed, landed, in_send.at[2 + t], in_recv.at[2 + t], sibling).wait_send()
            gather.finish(rest, send_sems, recv_sems)

    once = lambda w: pl.BlockSpec((ts, w), lambda j, i, p: (jnp.where(j == 0, i, nt - 1), 0))
    return pl.pallas_call(
        body, name="fwd_proj",
        grid_spec=pltpu.PrefetchScalarGridSpec(
            num_scalar_prefetch=1, grid=(NCHIP, nt),
            in_specs=[once(D), pl.BlockSpec((VD_ROWS, D), lambda j, i, p: (0, 0)),
                      pl.BlockSpec(memory_space=pl.ANY)] + gather.specs_any,
            out_specs=[pl.BlockSpec((ts, cw), lambda j, i, p: (i, p[0] ^ j)), once(D), once(D),
                       pl.BlockSpec(memory_space=pl.ANY)] + gather.specs_any,
            scratch_shapes=[pltpu.VMEM((NCHIP, D, cw), BF16), pltpu.VMEM((nt, ts, D), BF16),
                            pltpu.SemaphoreType.DMA((1,)),
                            pltpu.SemaphoreType.DMA((6,)), pltpu.SemaphoreType.DMA((6,))] + gather.scratch),
        out_shape=[jax.ShapeDtypeStruct((s, DIN), BF16), jax.ShapeDtypeStruct((s, D), BF16),
                   jax.ShapeDtypeStruct((s, D), F32), jax.ShapeDtypeStruct(sp_in.full_shape, BF16)] + gather.out_shape,
        input_output_aliases={3 + w: 3 + w for w in range(n + 1)},
        compiler_params=_params(("arbitrary", "arbitrary")),
    )(place, x, vec_d, placed_in, *placed_rest)


def _fwd_mix(proj, x, vec_d, w_pool, w_bout, w_o, placed_ffn, ts):
    s = x.shape[0]
    gather = _WeightGather(SHARDED[4:])
    n = gather.n

    def body(*refs):
        ins, outs, rest = refs[:6], refs[6 + n:14 + n], refs[14 + n:14 + 2 * n]
        scratch, sems = refs[14 + 2 * n:-2], refs[-2:]
        i = pl.program_id(0)
        nt = s // ts
        pl.when(i == 0)(lambda: gather.start(rest, *sems))
        pl.when(i == nt - 1 - nt // 8)(lambda: gather.forward(rest, *sems))
        compute(*ins, *outs, *scratch)
        pl.when(i == nt - 1)(lambda: gather.drain(rest, *sems))

    def compute(p_ref, x_ref, v_ref, wp_hbm, wb_hbm, wo_hbm,
                x1_ref, o_ref, pg_ref, q_ref, mg_ref, ya_ref, yb_ref, cv_ref,
                wp, wb, wo, carry_p, carry_v, sem):
        i = pl.program_id(0)
        _load_once([(wp_hbm, wp), (wb_hbm, wb), (wo_hbm, wo)], sem)

        @pl.when(i == 0)
        def _():
            carry_p[...] = jnp.zeros_like(carry_p)
            carry_v[...] = jnp.zeros_like(carry_v)

        t0 = i * ts
        for g in range(NG):
            cols = slice(g * GW, (g + 1) * GW)
            u = p_ref[:, cols].astype(F32)
            e = u
            for l in range(g + 1):
                slot = slice((1 << l) - 1, (2 << l) - 1)
                shifted, carry_p[slot, :, cols] = _before(e, carry_p[slot, :, cols], 1 << l)
                e = e + shifted
            pgb = (e / _pool_counts(t0, g) - u).astype(BF16)
            pg_ref[:, cols] = pgb
            ya_ref[:, cols] = _dot(pgb, wp[g]).astype(BF16)

        u_x = p_ref[:, D:2 * D].astype(F32)
        u_c = p_ref[:, 3 * D:4 * D].astype(F32)
        v = u_c * u_x
        cv = _causal_conv(v, carry_v, slice(None), _row(v_ref, V_CW0), _row(v_ref, V_CW1),
                          _row(v_ref, V_CW2), _row(v_ref, V_CB))
        cv_ref[...] = cv.astype(BF16)
        q = (p_ref[:, 2 * D:3 * D].astype(F32) * cv).astype(BF16)
        q_ref[...] = q
        y_b = _dot(q, wb[...])
        yb_ref[...] = y_b.astype(BF16)

        y_a = ya_ref[...].astype(F32) * _row(v_ref, V_PSCALE)
        merged = (jax.nn.sigmoid(p_ref[:, 4 * D:5 * D].astype(F32)) * y_a
                  + jax.nn.sigmoid(p_ref[:, 5 * D:6 * D].astype(F32)) * y_b).astype(BF16)
        mg_ref[...] = merged
        o = _dot(merged, wo[...])
        o_ref[...] = o
        x1_ref[...] = x_ref[...] + _row(v_ref, V_GT1) * ((o * _rms(o)) * _row(v_ref, V_GPOST1))

    tile = lambda w: pl.BlockSpec((ts, w), lambda i: (i, 0))
    hbm = pl.BlockSpec(memory_space=pl.ANY)
    return pl.pallas_call(
        body, name="fwd_mix", grid=(s // ts,),
        in_specs=[tile(DIN), tile(D), pl.BlockSpec((VD_ROWS, D), lambda i: (0, 0)), hbm, hbm, hbm] + gather.specs_any,
        out_specs=[tile(D)] * 8 + gather.specs_any,
        out_shape=[jax.ShapeDtypeStruct((s, D), F32), jax.ShapeDtypeStruct((s, D), F32)]
        + [jax.ShapeDtypeStruct((s, D), BF16)] * 6 + gather.out_shape,
        input_output_aliases={6 + w: 8 + w for w in range(n)},
        scratch_shapes=[pltpu.VMEM((NG, GW, GW), BF16), pltpu.VMEM((D, D), BF16), pltpu.VMEM((D, D), BF16),
                        pltpu.VMEM((POOL_CARRY, 8, D), F32), pltpu.VMEM((CONV_CARRY, 8, D), F32),
                        pltpu.SemaphoreType.DMA((3,))] + gather.scratch,
        compiler_params=_params(("arbitrary",)),
    )(proj, x, vec_d, w_pool, w_bout, w_o, *placed_ffn)


def _fwd_ffn(x1, tgt, vec_d, vec_f, w_up, w_down, ts):
    s = x1.shape[0]
    hw = F // FFN_PASSES

    def body(x1_ref, t_ref, v_ref, f_ref, wu_hbm, wd_hbm,
             up_ref, upc_ref, a_ref, h2_ref, dx2_ref, dff_ref, vo_ref, loss_ref,
             wu, wd, carry, sem):
        i = pl.program_id(0)
        _load_once([(wu_hbm, wu), (wd_hbm, wd)], sem)

        @pl.when(i == 0)
        def _():
            carry[...] = jnp.zeros_like(carry)
            vo_ref[...] = jnp.zeros_like(vo_ref)
            loss_ref[...] = jnp.zeros_like(loss_ref)

        x1v = x1_ref[...]
        n3 = x1v * _rms(x1v)
        h2 = (n3 * (_row(v_ref, V_GPRE2) * (1.0 + _row(v_ref, V_SC2))) + _row(v_ref, V_SH2)).astype(BF16)
        h2_ref[...] = h2

        ff = jnp.zeros((ts, D), F32)
        for p in range(FFN_PASSES):
            up = []
            for cols in (slice(p * hw, (p + 1) * hw), slice(F + p * hw, F + (p + 1) * hw)):
                u0 = _dot(h2, wu[:, cols])
                up_ref[:, cols] = u0.astype(BF16)
                y = _causal_conv(u0, carry, cols, f_ref[FV_W0:FV_W0 + 1, cols], f_ref[FV_W1:FV_W1 + 1, cols],
                                 f_ref[FV_W2:FV_W2 + 1, cols], f_ref[FV_B:FV_B + 1, cols])
                upc_ref[:, cols] = y.astype(BF16)
                up.append(y)
            gelu, _ = _gelu_and_grad(up[0])
            a = (gelu * up[1]).astype(BF16)
            a_ref[:, p * hw:(p + 1) * hw] = a
            ff = ff + _dot(a, wd[p * hw:(p + 1) * hw, :])

        r4 = _rms(ff)
        n4 = ff * r4
        gt2 = _row(v_ref, V_GT2)
        gpost = _row(v_ref, V_GPOST2)
        y4 = n4 * gpost
        diff = (x1v + gt2 * y4) - _load_rows(t_ref, ts)
        loss_ref[...] += jnp.full(loss_ref.shape, 0.5 / D * jnp.sum(diff * diff), F32)
        dx2 = diff * (1.0 / D)
        dx2_ref[...] = dx2
        dy4 = dx2 * gt2
        vo_ref[0:1, :] += _colsum(dx2 * y4)
        vo_ref[1:2, :] += _colsum(dy4 * n4)
        dff_ref[...] = _rms_bwd(dy4 * gpost, n4, r4).astype(BF16)

    tile = lambda w: pl.BlockSpec((ts, w), lambda i: (i, 0))
    full = lambda r, w: pl.BlockSpec((r, w), lambda i: (0, 0))
    hbm = pl.BlockSpec(memory_space=pl.ANY)
    return pl.pallas_call(
        body, name="fwd_ffn", grid=(s // ts,),
        in_specs=[tile(D), tile(D), full(VD_ROWS, D), full(FV_ROWS, F2), hbm, hbm],
        out_specs=[tile(F2), tile(F2), tile(F), tile(D), tile(D), tile(D), full(8, D), full(8, 128)],
        out_shape=[jax.ShapeDtypeStruct((s, F2), BF16), jax.ShapeDtypeStruct((s, F2), BF16),
                   jax.ShapeDtypeStruct((s, F), BF16),
                   jax.ShapeDtypeStruct((s, D), BF16), jax.ShapeDtypeStruct((s, D), F32),
                   jax.ShapeDtypeStruct((s, D), BF16), jax.ShapeDtypeStruct((8, D), F32),
                   jax.ShapeDtypeStruct((8, 128), F32)],
        scratch_shapes=[pltpu.VMEM((D, F2), BF16), pltpu.VMEM((F, D), BF16), pltpu.VMEM((CONV_CARRY, 8, F2), F32),
                        pltpu.SemaphoreType.DMA((2,))],
        compiler_params=_params(("arbitrary",)),
    )(x1, tgt, vec_d, vec_f, w_up, w_down)


def _bwd_ffn(dff, dx2, x1, up0, upc, vec_d, vec_f, w_up, w_down, exchange, ex_grads, ts):
    s = x1.shape[0]
    nt = s // ts
    hw = F // FFN_PASSES
    n = exchange.n

    def body(*refs):
        ins, grads = refs[:9], refs[9:9 + n]
        outs, recvs = refs[9 + n:13 + n], refs[13 + n:13 + 2 * n]
        scratch, sems = refs[13 + 2 * n:-2], refs[-2:]
        i = pl.program_id(0)
        pl.when(i == 0)(lambda: exchange.start(grads, recvs, *sems))
        compute(*ins, *outs, *scratch)
        pl.when(i == nt - 1)(lambda: exchange.finish(grads, recvs, *sems))

    def compute(dff_ref, dx2_ref, x1_ref, up_ref, upc_ref, v_ref, f_ref, wu_hbm, wd_hbm,
                dx1_ref, dup_ref, vo_ref, fo_ref, wu, wd, carry, sem):
        i = pl.program_id(0)
        _load_once([(wu_hbm, wu), (wd_hbm, wd)], sem)

        @pl.when(i == 0)
        def _():
            carry[...] = jnp.zeros_like(carry)
            vo_ref[...] = jnp.zeros_like(vo_ref)
            fo_ref[...] = jnp.zeros_like(fo_ref)

        dffb = dff_ref[...]

        dh2 = jnp.zeros((ts, D), F32)
        for p in range(FFN_PASSES):
            slabs = (slice(p * hw, (p + 1) * hw), slice(F + p * hw, F + (p + 1) * hw))
            gelu, dgelu = _gelu_and_grad(upc_ref[:, slabs[0]].astype(F32))
            da = _dot_nt(dffb, wd[p * hw:(p + 1) * hw, :])
            dups = (da * upc_ref[:, slabs[1]].astype(F32) * dgelu, da * gelu)
            for cols, dup in zip(slabs, dups):
                du0, d1, d2 = _causal_conv_bwd(dup, carry, cols, f_ref[FV_W0:FV_W0 + 1, cols],
                                               f_ref[FV_W1:FV_W1 + 1, cols], f_ref[FV_W2:FV_W2 + 1, cols])
                u0 = up_ref[:, cols].astype(F32)
                fo_ref[FV_B:FV_B + 1, cols] += _colsum(dup)
                fo_ref[FV_W2:FV_W2 + 1, cols] += _colsum(dup * u0)
                fo_ref[FV_W1:FV_W1 + 1, cols] += _colsum(d1 * u0)
                fo_ref[FV_W0:FV_W0 + 1, cols] += _colsum(d2 * u0)
                du0 = du0.astype(BF16)
                dup_ref[:, cols] = du0
                dh2 = dh2 + _dot_nt(du0, wu[:, cols])

        x1v = x1_ref[...]
        r3 = _rms(x1v)
        n3 = x1v * r3
        gpre = _row(v_ref, V_GPRE2)
        sc = 1.0 + _row(v_ref, V_SC2)
        vo_ref[0:1, :] += _colsum(dh2)
        vo_ref[1:2, :] += _colsum(dh2 * n3 * gpre)
        vo_ref[2:3, :] += _colsum(dh2 * n3 * sc)
        dx1_ref[...] = dx2_ref[...] + _rms_bwd(dh2 * (gpre * sc), n3, r3)

    rev = lambda w: pl.BlockSpec((ts, w), lambda i: (nt - 1 - i, 0))
    full = lambda r, w: pl.BlockSpec((r, w), lambda i: (0, 0))
    hbm = pl.BlockSpec(memory_space=pl.ANY)
    return pl.pallas_call(
        body, name="bwd_ffn", grid=(nt,),
        in_specs=[rev(D), rev(D), rev(D), rev(F2), rev(F2), full(VD_ROWS, D), full(FV_ROWS, F2), hbm, hbm]
        + exchange.specs_any,
        out_specs=[rev(D), rev(F2), full(8, D), full(FV_ROWS, F2)] + exchange.specs_any,
        out_shape=[jax.ShapeDtypeStruct((s, D), F32), jax.ShapeDtypeStruct((s, F2), BF16),
                   jax.ShapeDtypeStruct((8, D), F32), jax.ShapeDtypeStruct((FV_ROWS, F2), F32)] + exchange.out_shape,
        scratch_shapes=[pltpu.VMEM((D, F2), BF16), pltpu.VMEM((F, D), BF16), pltpu.VMEM((CONV_CARRY, 8, F2), F32),
                        pltpu.SemaphoreType.DMA((2,))] + exchange.scratch,
        compiler_params=_params(("arbitrary",)),
    )(dff, dx2, x1, up0, upc, vec_d, vec_f, w_up, w_down, *ex_grads)


def _bwd_mix(dx1, o, proj, cv, ya0, yb, vec_d, w_pool, w_bout, w_o, exchange, ex_grads, ts):
    s = dx1.shape[0]
    nt = s // ts
    n = exchange.n

    def body(*refs):
        ins, grads = refs[:10], refs[10:10 + n]
        outs, recvs = refs[10 + n:15 + n], refs[15 + n:15 + 2 * n]
        scratch, sems = refs[15 + 2 * n:-2], refs[-2:]
        i = pl.program_id(0)
        pl.when(i == 0)(lambda: exchange.start(grads, recvs, *sems))
        compute(*ins, *outs, *scratch)
        pl.when(i == nt - 1)(lambda: exchange.finish(grads, recvs, *sems))

    def compute(dx1_ref, o_ref, p_ref, cv_ref, ya_ref, yb_ref, v_ref, wp_hbm, wb_hbm, wo_hbm,
                dp_ref, do_ref, dyb_ref, dya_ref, vo_ref, wp, wb, wo, carry_d, carry_e, sem):
        i = pl.program_id(0)
        _load_once([(wp_hbm, wp), (wb_hbm, wb), (wo_hbm, wo)], sem)

        @pl.when(i == 0)
        def _():
            carry_d[...] = jnp.zeros_like(carry_d)
            carry_e[...] = jnp.zeros_like(carry_e)
            vo_ref[...] = jnp.zeros_like(vo_ref)

        t0 = (nt - 1 - i) * ts
        dx1v = dx1_ref[...]
        ov = o_ref[...]
        r2 = _rms(ov)
        n2 = ov * r2
        gpost = _row(v_ref, V_GPOST1)
        vo_ref[0:1, :] += _colsum(dx1v * (n2 * gpost))
        dy2 = dx1v * _row(v_ref, V_GT1)
        vo_ref[1:2, :] += _colsum(dy2 * n2)
        dob = _rms_bwd(dy2 * gpost, n2, r2).astype(BF16)
        do_ref[...] = dob
        dmerged = _dot_nt(dob, wo[...])

        ya0 = ya_ref[...].astype(F32)
        pscale = _row(v_ref, V_PSCALE)
        sa = jax.nn.sigmoid(p_ref[:, 4 * D:5 * D].astype(F32))
        dp_ref[:, 4 * D:5 * D] = (dmerged * (ya0 * pscale) * sa * (1.0 - sa)).astype(BF16)
        dy_a = dmerged * sa
        vo_ref[2:3, :] += _colsum(dy_a * ya0)
        dya0 = (dy_a * pscale).astype(BF16)
        dya_ref[...] = dya0

        sb = jax.nn.sigmoid(p_ref[:, 5 * D:6 * D].astype(F32))
        dp_ref[:, 5 * D:6 * D] = (dmerged * yb_ref[...].astype(F32) * sb * (1.0 - sb)).astype(BF16)
        dy_b = (dmerged * sb).astype(BF16)
        dyb_ref[...] = dy_b
        dq = _dot_nt(dy_b, wb[...])

        u_x = p_ref[:, D:2 * D].astype(F32)
        u_b = p_ref[:, 2 * D:3 * D].astype(F32)
        u_c = p_ref[:, 3 * D:4 * D].astype(F32)
        w0, w1, w2 = _row(v_ref, V_CW0), _row(v_ref, V_CW1), _row(v_ref, V_CW2)
        dp_ref[:, 2 * D:3 * D] = (dq * cv_ref[...].astype(F32)).astype(BF16)
        dcv = dq * u_b
        dv, d1, d2 = _causal_conv_bwd(dcv, carry_d, slice(None), w0, w1, w2)
        v = u_c * u_x
        vo_ref[3:4, :] += _colsum(dcv)
        vo_ref[4:5, :] += _colsum(d2 * v)
        vo_ref[5:6, :] += _colsum(d1 * v)
        vo_ref[6:7, :] += _colsum(dcv * v)
        dp_ref[:, D:2 * D] = (dv * u_c).astype(BF16)
        dp_ref[:, 3 * D:4 * D] = (dv * u_x).astype(BF16)

        for g in range(NG):
            cols = slice(g * GW, (g + 1) * GW)
            dpg = _dot_nt(dya0[:, cols], wp[g])
            e = dpg / _pool_counts(t0, g)
            for l in range(g + 1):
                slot = slice((1 << l) - 1, (2 << l) - 1)
                shifted, carry_e[slot, :, cols] = _after(e, carry_e[slot, :, cols], 1 << l)
                e = e + shifted
            dp_ref[:, cols] = (e - dpg).astype(BF16)

    rev = lambda w: pl.BlockSpec((ts, w), lambda i: (nt - 1 - i, 0))
    hbm = pl.BlockSpec(memory_space=pl.ANY)
    return pl.pallas_call(
        body, name="bwd_mix", grid=(nt,),
        in_specs=[rev(D), rev(D), rev(DIN), rev(D), rev(D), rev(D), pl.BlockSpec((VD_ROWS, D), lambda i: (0, 0)),
                  hbm, hbm, hbm] + exchange.specs_any,
        out_specs=[rev(DIN), rev(D), rev(D), rev(D), pl.BlockSpec((8, D), lambda i: (0, 0))] + exchange.specs_any,
        out_shape=[jax.ShapeDtypeStruct((s, DIN), BF16)] + [jax.ShapeDtypeStruct((s, D), BF16)] * 3
        + [jax.ShapeDtypeStruct((8, D), F32)] + exchange.out_shape,
        scratch_shapes=[pltpu.VMEM((NG, GW, GW), BF16), pltpu.VMEM((D, D), BF16), pltpu.VMEM((D, D), BF16),
                        pltpu.VMEM((CONV_CARRY, 8, D), F32), pltpu.VMEM((POOL_CARRY, 8, D), F32),
                        pltpu.SemaphoreType.DMA((3,))] + exchange.scratch,
        compiler_params=_params(("arbitrary",)),
    )(dx1, o, proj, cv, ya0, yb, vec_d, w_pool, w_bout, w_o, *ex_grads)


def _bwd_in(dproj, dx1, x, vec_d, w_in, exchange, ex_grads, ts):
    s = x.shape[0]
    nt = s // ts
    n = exchange.n

    def body(*refs):
        ins, grads = refs[:5], refs[5:5 + n]
        outs, recvs = refs[5 + n:7 + n], refs[7 + n:7 + 2 * n]
        scratch, sems = refs[7 + 2 * n:-2], refs[-2:]
        i = pl.program_id(0)
        pl.when(i == 0)(lambda: exchange.start(grads, recvs, *sems))
        compute(*ins, *outs, *scratch)
        pl.when(i == nt - 1)(lambda: exchange.finish(grads, recvs, *sems))

    def compute(dp_ref, dx1_ref, x_ref, v_ref, w_hbm, dx_ref, vo_ref, w_vmem, sem):
        _load_once([(w_hbm, w_vmem)], sem)

        @pl.when(pl.program_id(0) == 0)
        def _():
            vo_ref[...] = jnp.zeros_like(vo_ref)

        dh1 = _dot_nt(dp_ref[...], w_vmem[...])
        xv = x_ref[...]
        r1 = _rms(xv)
        n1 = xv * r1
        gpre = _row(v_ref, V_GPRE1)
        sc = 1.0 + _row(v_ref, V_SC1)
        vo_ref[0:1, :] += _colsum(dh1)
        vo_ref[1:2, :] += _colsum(dh1 * n1 * gpre)
        vo_ref[2:3, :] += _colsum(dh1 * n1 * sc)
        _store_rows(dx_ref, dx1_ref[...] + _rms_bwd(dh1 * (gpre * sc), n1, r1), ts)

    tile = lambda w: pl.BlockSpec((ts, w), lambda i: (i, 0))
    return pl.pallas_call(
        body, name="bwd_in", grid=(s // ts,),
        in_specs=[tile(DIN), tile(D), tile(D), pl.BlockSpec((VD_ROWS, D), lambda i: (0, 0)),
                  pl.BlockSpec(memory_space=pl.ANY)] + exchange.specs_any,
        out_specs=[tile(D), pl.BlockSpec((8, D), lambda i: (0, 0))] + exchange.specs_any,
        out_shape=[jax.ShapeDtypeStruct((s, D), F32), jax.ShapeDtypeStruct((8, D), F32)] + exchange.out_shape,
        scratch_shapes=[pltpu.VMEM((D, DIN), BF16), pltpu.SemaphoreType.DMA((1,))] + exchange.scratch,
        compiler_params=_params(("arbitrary",)),
    )(dproj, dx1, x, vec_d, w_in, *ex_grads)


def _dot_tn(a, b):
    return lax.dot_general(a, b, (((0,), (0,)), ((), ())), preferred_element_type=F32)


def _wgrad(a, b, tm, tn, ts, name, dtype, exchange=None, ex_grads=()):
    s, m = a.shape
    nn = b.shape[1]
    grid = (m // tm, nn // tn, s // ts)
    n = exchange.n if exchange else 0

    def body(*refs):
        a_ref, b_ref = refs[:2]
        grads = refs[2:2 + n]
        o_ref = refs[2 + n]
        recvs = refs[3 + n:3 + 2 * n]
        acc = refs[3 + 2 * n]
        sems = refs[4 + 2 * n:]
        i, j, k = pl.program_id(0), pl.program_id(1), pl.program_id(2)
        if exchange:
            pl.when((i == 0) & (j == 0) & (k == 0))(lambda: exchange.start(grads, recvs, *sems))
        part = _dot_tn(a_ref[...], b_ref[...])

        @pl.when(k == 0)
        def _():
            acc[...] = part

        @pl.when(k > 0)
        def _():
            acc[...] += part

        @pl.when(k == grid[2] - 1)
        def _():
            o_ref[...] = acc[...].astype(dtype)

        if exchange:
            pl.when((i == grid[0] - 1) & (j == grid[1] - 1) & (k == grid[2] - 1))(
                lambda: exchange.finish(grads, recvs, *sems))

    hosted = exchange.specs_any if exchange else []
    return pl.pallas_call(
        body, name=name, grid=grid,
        in_specs=[pl.BlockSpec((ts, tm), lambda i, j, k: (k, i)), pl.BlockSpec((ts, tn), lambda i, j, k: (k, j))]
        + hosted,
        out_specs=[pl.BlockSpec((tm, tn), lambda i, j, k: (i, j))] + hosted,
        out_shape=[jax.ShapeDtypeStruct((m, nn), dtype)] + (exchange.out_shape if exchange else []),
        scratch_shapes=[pltpu.VMEM((tm, tn), F32)] + (exchange.scratch if exchange else []),
        compiler_params=_params(("arbitrary", "arbitrary", "arbitrary")),
    )(a, b, *ex_grads)


def _wgrad_pool(pg, dya0, ts):
    s = pg.shape[0]
    nk = s // ts

    def body(a_ref, b_ref, o_ref, acc):
        k = pl.program_id(1)
        part = _dot_tn(a_ref[...], b_ref[...])

        @pl.when(k == 0)
        def _():
            acc[...] = part

        @pl.when(k > 0)
        def _():
            acc[...] += part

        @pl.when(k == nk - 1)
        def _():
            o_ref[0] = acc[...].astype(BF16)

    return pl.pallas_call(
        body, name="wgrad_pool", grid=(NG, nk),
        in_specs=[pl.BlockSpec((ts, GW), lambda g, k: (k, g)), pl.BlockSpec((ts, GW), lambda g, k: (k, g))],
        out_specs=pl.BlockSpec((1, GW, GW), lambda g, k: (g, 0, 0)),
        out_shape=jax.ShapeDtypeStruct((NG, GW, GW), BF16),
        scratch_shapes=[pltpu.VMEM((GW, GW), F32)],
        compiler_params=_params(("arbitrary", "arbitrary")),
    )(pg, dya0)


FFN_PASSES = 2
TS_PROJ = 512
TS_MIX = 256
TS_FFN = 256
TS_WGRAD = 1024


def _local_step(x, tgt, vec_d, vec_f, placed, place):
    s = x.shape[0]
    tw = min(TS_WGRAD, s)
    sp_in, sp_pool, sp_bout, sp_o, sp_up, sp_down = SHARDED
    proj, h1, xs, w_in, w_pool, w_bout, w_o = _fwd_proj(x, vec_d, placed[0], placed[1:4], place, min(TS_PROJ, s))
    x1, o, pg, q, merged, ya0, yb, cv, w_up, w_down = _fwd_mix(proj, xs, vec_d, w_pool, w_bout, w_o, placed[4:],
                                                               min(TS_MIX, s))
    up0, upc, a, h2, dx2, dff, vo_f, loss = _fwd_ffn(x1, tgt, vec_d, vec_f, w_up, w_down, min(TS_FFN, s))
    g_down, = _wgrad(a, dff, F // 2, D, tw, "wgrad_down", BF16)
    dx1, dup0, vo_b, fo, r_down = _bwd_ffn(dff, dx2, x1, up0, upc, vec_d, vec_f, w_up, w_down,
                                           _GradExchange([sp_down]), [g_down], min(TS_FFN, s))
    g_up, = _wgrad(h2, dup0, D, F2 // NCHIP, tw, "wgrad_up", BF16)
    dproj, do, dyb, dya0, vo_m, r_up = _bwd_mix(dx1, o, proj, cv, ya0, yb, vec_d, w_pool, w_bout, w_o,
                                                _GradExchange([sp_up]), [g_up], min(TS_MIX, s))
    g_o, = _wgrad(merged, do, D, D, tw, "wgrad_o", BF16)
    g_bout, = _wgrad(q, dyb, D, D, tw, "wgrad_bout", BF16)
    g_pool = _wgrad_pool(pg, dya0, tw)
    g_in, r_pool, r_bout, r_o = _wgrad(h1, dproj, D, DIN // NCHIP, tw, "wgrad_in", BF16,
                                       _GradExchange([sp_pool, sp_bout, sp_o]), [g_pool, g_bout, g_o])
    dx, vo_i, r_in = _bwd_in(dproj, dx1, xs, vec_d, w_in, _GradExchange([sp_in]), [g_in], min(TS_PROJ, s))
    vecs = dict(
        dsh1=vo_i[0], dsc1=vo_i[1], dg_pre_mix=vo_i[2],
        dgt1=vo_m[0], dg_post_mix=vo_m[1], dpool_scale=vo_m[2], dconv_b=vo_m[3],
        dconv_w=vo_m[4:7],
        dsh2=vo_b[0], dsc2=vo_b[1], dg_pre_ffn=vo_b[2],
        dgt2=vo_f[0], dg_post_ffn=vo_f[1],
        dffn_conv_w=fo[FV_W0:FV_W2 + 1], dffn_conv_b=fo[FV_B],
    )
    local = dict(w_in=g_in, w_pool=g_pool, w_bout=g_bout, w_o=g_o, w_up=g_up, w_down=g_down)
    received = dict(w_in=r_in, w_pool=r_pool, w_bout=r_bout, w_o=r_o, w_up=r_up, w_down=r_down)
    return loss, dx, vecs, local, received


def _aligned(offset, n):
    return offset if isinstance(offset, int) else pl.multiple_of(offset, n)


class _Sharded:
    def __init__(self, name, full_shape, shard_axis, half_axis):
        self.name = name
        self.full_shape = full_shape
        self.shard_axis = shard_axis
        self.half_axis = half_axis
        self.shard_shape = tuple(n // NCHIP if a == shard_axis else n for a, n in enumerate(full_shape))
        self.piece_shape = tuple(n // 2 if a == half_axis else n for a, n in enumerate(self.shard_shape))

    def piece(self, full_ref, k, h):
        idx = []
        for a, n in enumerate(self.piece_shape):
            if a == self.shard_axis and a == self.half_axis:
                idx.append(pl.ds(_aligned((2 * k + h) * n, n), n))
            elif a == self.shard_axis:
                idx.append(pl.ds(_aligned(k * n, n), n))
            elif a == self.half_axis:
                idx.append(pl.ds(_aligned(h * n, n), n))
            else:
                idx.append(slice(None))
        return full_ref.at[tuple(idx)]

    def shard(self, full_ref, k):
        n = self.shard_shape[self.shard_axis]
        idx = [pl.ds(_aligned(k * n, n), n) if a == self.shard_axis else slice(None)
               for a in range(len(self.full_shape))]
        return full_ref.at[tuple(idx)]

    def half(self, shard_ref, h):
        n = self.piece_shape[self.half_axis]
        idx = [pl.ds(_aligned(h * n, n), n) if a == self.half_axis else slice(None)
               for a in range(len(self.full_shape))]
        return shard_ref.at[tuple(idx)]

    def piece_block(self):
        def index_map(k, c_ref):
            c = c_ref[0]
            out = []
            for a in range(len(self.full_shape)):
                if a == self.shard_axis and a == self.half_axis:
                    out.append(2 * k + c)
                elif a == self.shard_axis:
                    out.append(k)
                elif a == self.half_axis:
                    out.append(c)
                else:
                    out.append(0)
            return tuple(out)
        return pl.BlockSpec(self.piece_shape, index_map)


SHARDED = (
    _Sharded("w_in", (D, DIN), 1, 0),
    _Sharded("w_pool", (NG, GW, GW), 1, 0),
    _Sharded("w_bout", (D, D), 0, 0),
    _Sharded("w_o", (D, D), 0, 0),
    _Sharded("w_up", (D, F2), 1, 0),
    _Sharded("w_down", (F, D), 0, 0),
)
NW = len(SHARDED)


def _mesh_place():
    x, y, c = lax.axis_index("x"), lax.axis_index("y"), lax.axis_index("c")
    chips = [(1 - x, y), (x, 1 - y), (1 - x, 1 - y)]
    return x, y, c, 2 * x + y, chips, [2 * px + py for px, py in chips]


def _remote(src, dst, send_sem, recv_sem, device):
    return pltpu.make_async_remote_copy(src_ref=src, dst_ref=dst, send_sem=send_sem, recv_sem=recv_sem,
                                        device_id=device, device_id_type=MESH)


def _all_gather_small(block, name):
    m_per, n = block.shape

    def body(x_ref, out_ref, send_sems, recv_sems, local_sem):
        x, y, c, _, chips, _ = _mesh_place()
        me, sibling = (x, y, c), (x, y, 1 - c)

        def rows(px, py, pc):
            return out_ref.at[pl.ds((4 * px + 2 * py + pc) * m_per, m_per), :]

        def copy(k, blk, to, src=None):
            return _remote(rows(*blk) if src is None else src, rows(*blk), send_sems.at[k], recv_sems.at[k], to)

        mine = pltpu.make_async_copy(x_ref, rows(*me), local_sem)
        mine.start()
        first = [copy(0, me, sibling, src=x_ref)]
        first += [copy(1 + j, me, (*chip, c), src=x_ref) for j, chip in enumerate(chips)]
        for cp in first:
            cp.start()
        passed = [copy(4 + j, (*chip, c), sibling) for j, chip in enumerate(chips)]
        for j, chip in enumerate(chips):
            copy(1 + j, (*chip, c), me).wait_recv()
            passed[j].start()
        copy(0, sibling, me).wait_recv()
        for j, chip in enumerate(chips):
            copy(4 + j, (*chip, 1 - c), me).wait_recv()
        for cp in first + passed:
            cp.wait_send()
        mine.wait()

    return pl.pallas_call(
        body, name=name,
        out_shape=jax.ShapeDtypeStruct((NDEV * m_per, n), block.dtype),
        in_specs=[pl.BlockSpec(memory_space=pltpu.VMEM)],
        out_specs=pl.BlockSpec(memory_space=pltpu.VMEM),
        scratch_shapes=[pltpu.SemaphoreType.DMA((7,)), pltpu.SemaphoreType.DMA((7,)), pltpu.SemaphoreType.DMA],
        compiler_params=pltpu.CompilerParams(vmem_limit_bytes=VMEM_LIMIT),
    )(block)


class _WeightGather:
    def __init__(self, specs):
        self.specs = specs
        self.n = len(specs)
        self.specs_any = [pl.BlockSpec(memory_space=pl.ANY)] * self.n
        self.out_shape = [jax.ShapeDtypeStruct(sp.full_shape, BF16) for sp in specs]
        self.scratch = [pltpu.SemaphoreType.DMA((6 * self.n,)), pltpu.SemaphoreType.DMA((6 * self.n,))]

    def _sends(self, outs, send_sems, recv_sems):
        x, y, c, k_me, chips, _ = _mesh_place()
        sends = []
        for j, chip in enumerate(chips):
            for w, sp in enumerate(self.specs):
                mine = sp.piece(outs[w], k_me, c)
                sends.append(_remote(mine, mine, send_sems.at[6 * w + j], recv_sems.at[6 * w + j], (*chip, c)))
        return sends

    def start(self, outs, send_sems, recv_sems):
        for cp in self._sends(outs, send_sems, recv_sems):
            cp.start()

    def _passes(self, outs, send_sems, recv_sems):
        x, y, c, _, chips, kidx = _mesh_place()
        return [_remote(sp.piece(outs[w], kidx[j], c), sp.piece(outs[w], kidx[j], c),
                        send_sems.at[6 * w + 3 + j], recv_sems.at[6 * w + 3 + j], (x, y, 1 - c))
                for j in range(3) for w, sp in enumerate(self.specs)]

    def forward(self, outs, send_sems, recv_sems):
        x, y, c, _, chips, kidx = _mesh_place()
        for j, chip in enumerate(chips):
            for w, sp in enumerate(self.specs):
                landed = sp.piece(outs[w], kidx[j], c)
                _remote(landed, landed, send_sems.at[6 * w + j], recv_sems.at[6 * w + j], (*chip, c)).wait_recv()
        for cp in self._passes(outs, send_sems, recv_sems):
            cp.start()

    def drain(self, outs, send_sems, recv_sems):
        x, y, c, _, chips, kidx = _mesh_place()
        for j in range(3):
            for w, sp in enumerate(self.specs):
                landed = sp.piece(outs[w], kidx[j], 1 - c)
                _remote(landed, landed, send_sems.at[6 * w + 3 + j], recv_sems.at[6 * w + 3 + j],
                        (x, y, 1 - c)).wait_recv()
        for cp in self._sends(outs, send_sems, recv_sems) + self._passes(outs, send_sems, recv_sems):
            cp.wait_send()

    def finish(self, outs, send_sems, recv_sems):
        self.forward(outs, send_sems, recv_sems)
        self.drain(outs, send_sems, recv_sems)


def _gather_weights(placed, specs, name):
    gather = _WeightGather(specs)
    n = gather.n

    def body(*refs):
        outs, sems = refs[n:2 * n], refs[2 * n:]
        gather.start(outs, *sems)
        gather.finish(outs, *sems)

    return pl.pallas_call(
        body, name=name, out_shape=gather.out_shape, in_specs=gather.specs_any, out_specs=gather.specs_any,
        input_output_aliases={w: w for w in range(n)}, scratch_shapes=gather.scratch,
    )(*placed)


class _GradExchange:
    def __init__(self, specs):
        self.specs = specs
        self.n = len(specs)
        self.specs_any = [pl.BlockSpec(memory_space=pl.ANY)] * self.n
        self.out_shape = [jax.ShapeDtypeStruct((NDEV,) + sp.piece_shape, BF16) for sp in specs]
        self.scratch = [pltpu.SemaphoreType.DMA((7 * self.n,)), pltpu.SemaphoreType.DMA((NDEV * self.n,))]

    def _sends(self, grads, recvs, send_sems, recv_sems):
        x, y, c, k_me, chips, kidx = _mesh_place()
        dev = 2 * k_me + c
        sends = []
        for w, sp in enumerate(self.specs):
            slot, arrival = recvs[w].at[dev], recv_sems.at[NDEV * w + dev]
            sends.append(_remote(sp.piece(grads[w], k_me, 1 - c), slot, send_sems.at[7 * w], arrival, (x, y, 1 - c)))
            for j, chip in enumerate(chips):
                for h in range(2):
                    sends.append(_remote(sp.piece(grads[w], kidx[j], h), slot, send_sems.at[7 * w + 1 + 2 * j + h],
                                         arrival, (*chip, h)))
        return sends

    def start(self, grads, recvs, send_sems, recv_sems):
        for cp in self._sends(grads, recvs, send_sems, recv_sems):
            cp.start()

    def finish(self, grads, recvs, send_sems, recv_sems):
        x, y, c, k_me, _, _ = _mesh_place()
        dev = 2 * k_me + c
        for w in range(self.n):
            for d in range(NDEV):
                landed = recvs[w].at[d]
                arrival = _remote(landed, landed, send_sems.at[7 * w], recv_sems.at[NDEV * w + d], (x, y, c))
                pl.when(d != dev)(arrival.wait_recv)
        for cp in self._sends(grads, recvs, send_sems, recv_sems):
            cp.wait_send()


def _device_sum(sp, local, recv, place):
    nd = len(sp.piece_shape)

    def body(p_ref, a_ref, b_ref, o_ref):
        d = pl.program_id(0)
        term = jnp.where(d == p_ref[2], a_ref[...], b_ref[...]).astype(F32)

        @pl.when(d == 0)
        def _():
            o_ref[...] = term

        @pl.when(d > 0)
        def _():
            o_ref[...] += term

    def mine(d, p_ref):
        return tuple(2 * p_ref[0] + p_ref[1] if a == sp.shard_axis == sp.half_axis else
                     p_ref[0] if a == sp.shard_axis else p_ref[1] if a == sp.half_axis else 0 for a in range(nd))

    def others(d, p_ref):
        return (jnp.where(d == p_ref[2], (d + 1) % NDEV, d),) + (0,) * nd

    return pl.pallas_call(
        body, name="rs_device_sum_" + sp.name,
        grid_spec=pltpu.PrefetchScalarGridSpec(
            num_scalar_prefetch=1, grid=(NDEV,),
            in_specs=[pl.BlockSpec(sp.piece_shape, mine), pl.BlockSpec((None,) + sp.piece_shape, others)],
            out_specs=pl.BlockSpec(sp.piece_shape,
                                   lambda d, p_ref: tuple(p_ref[1] if a == sp.half_axis else 0 for a in range(nd)))),
        out_shape=jax.ShapeDtypeStruct(sp.shard_shape, F32),
        compiler_params=_params(("arbitrary",)),
    )(place, local, recv)


def _pair_exchange(grads, specs):
    n = len(specs)

    def body(*refs):
        ins, outs = refs[:n], refs[n:2 * n]
        send_sems, recv_sems = refs[2 * n:]
        x, y, c, _, _, _ = _mesh_place()
        sibling = (x, y, 1 - c)
        sent = []
        for w, sp in enumerate(specs):
            for k in range(NCHIP):
                cp = _remote(sp.piece(ins[w], k, 1 - c), outs[w].at[k],
                             send_sems.at[NCHIP * w + k], recv_sems.at[NCHIP * w + k], sibling)
                cp.start()
                sent.append(cp)
        for cp in sent:
            cp.wait_recv()
        for cp in sent:
            cp.wait_send()

    hbm = pl.BlockSpec(memory_space=pl.ANY)
    return pl.pallas_call(
        body, name="rs_pair_exchange",
        out_shape=[jax.ShapeDtypeStruct((NCHIP,) + sp.piece_shape, F32) for sp in specs],
        in_specs=[hbm] * n, out_specs=[hbm] * n,
        scratch_shapes=[pltpu.SemaphoreType.DMA((NCHIP * n,)), pltpu.SemaphoreType.DMA((NCHIP * n,))],
    )(*grads)


def _pair_sum(sp, grad, recv, core):
    nd = len(sp.piece_shape)

    def body(c_ref, g_ref, r_ref, o_ref):
        o_ref[...] = (g_ref[...] + r_ref[...]).astype(BF16)

    slot = pl.BlockSpec((None,) + sp.piece_shape, lambda k, c_ref: (k,) + (0,) * nd)
    return pl.pallas_call(
        body, name="rs_pair_sum_" + sp.name,
        grid_spec=pltpu.PrefetchScalarGridSpec(
            num_scalar_prefetch=1, grid=(NCHIP,),
            in_specs=[sp.piece_block(), slot], out_specs=slot),
        out_shape=jax.ShapeDtypeStruct((NCHIP,) + sp.piece_shape, BF16),
        compiler_params=_params(("parallel",)),
    )(core, grad, recv)


def _chip_exchange(parts, specs):
    n = len(specs)

    def body(*refs):
        ins, outs = refs[:n], refs[n:2 * n]
        send_sems, recv_sems = refs[2 * n:]
        x, y, c, k_me, chips, kidx = _mesh_place()
        sent = []
        for j, chip in enumerate(chips):
            for w in range(n):
                cp = _remote(ins[w].at[kidx[j]], outs[w].at[k_me], send_sems.at[3 * w + j], recv_sems.at[3 * w + j],
                             (*chip, c))
                cp.start()
                sent.append(cp)
        for j, chip in enumerate(chips):
            for w in range(n):
                landed = outs[w].at[kidx[j]]
                _remote(landed, landed, send_sems.at[3 * w + j], recv_sems.at[3 * w + j], (*chip, c)).wait_recv()
        for cp in sent:
            cp.wait_send()

    hbm = pl.BlockSpec(memory_space=pl.ANY)
    return pl.pallas_call(
        body, name="rs_chip_exchange",
        out_shape=[jax.ShapeDtypeStruct((NCHIP,) + sp.piece_shape, BF16) for sp in specs],
        in_specs=[hbm] * n, out_specs=[hbm] * n,
        scratch_shapes=[pltpu.SemaphoreType.DMA((3 * n,)), pltpu.SemaphoreType.DMA((3 * n,))],
    )(*parts)


def _chip_sum(sp, parts, recv, place):
    nd = len(sp.piece_shape)

    def body(p_ref, a_ref, b_ref, o_ref):
        k = pl.program_id(0)
        term = jnp.where(k == p_ref[0], a_ref[...], b_ref[...]).astype(F32)

        @pl.when(k == 0)
        def _():
            o_ref[...] = term

        @pl.when(k > 0)
        def _():
            o_ref[...] += term

    def others(k, p_ref):
        return (jnp.where(k == p_ref[0], (k + 1) % NCHIP, k),) + (0,) * nd

    return pl.pallas_call(
        body, name="rs_chip_sum_" + sp.name,
        grid_spec=pltpu.PrefetchScalarGridSpec(
            num_scalar_prefetch=1, grid=(NCHIP,),
            in_specs=[pl.BlockSpec((None,) + sp.piece_shape, lambda k, p_ref: (p_ref[0],) + (0,) * nd),
                      pl.BlockSpec((None,) + sp.piece_shape, others)],
            out_specs=pl.BlockSpec(sp.piece_shape,
                                   lambda k, p_ref: tuple(p_ref[1] if a == sp.half_axis else 0 for a in range(nd)))),
        out_shape=jax.ShapeDtypeStruct(sp.shard_shape, F32),
        compiler_params=_params(("arbitrary",)),
    )(place, parts, recv)


def _pair_share(halves):
    def body(*refs):
        outs = refs[NW:2 * NW]
        send_sems, recv_sems = refs[2 * NW:]
        x, y, c, _, _, _ = _mesh_place()
        sibling = (x, y, 1 - c)
        sent = []
        for w, sp in enumerate(SHARDED):
            mine = sp.half(outs[w], c)
            cp = _remote(mine, mine, send_sems.at[w], recv_sems.at[w], sibling)
            cp.start()
            sent.append(cp)
        for w, sp in enumerate(SHARDED):
            landed = sp.half(outs[w], 1 - c)
            _remote(landed, landed, send_sems.at[w], recv_sems.at[w], sibling).wait_recv()
        for cp in sent:
            cp.wait_send()

    hbm = pl.BlockSpec(memory_space=pl.ANY)
    return pl.pallas_call(
        body, name="rs_pair_share",
        out_shape=[jax.ShapeDtypeStruct(sp.shard_shape, F32) for sp in SHARDED],
        in_specs=[hbm] * NW, out_specs=[hbm] * NW,
        input_output_aliases={w: w for w in range(NW)},
        scratch_shapes=[pltpu.SemaphoreType.DMA((NW,)), pltpu.SemaphoreType.DMA((NW,))],
    )(*halves)


def _reduce_scatter(local, received, place):
    return _pair_share([_device_sum(sp, local[sp.name], received[sp.name], place) for sp in SHARDED])


def _place_bf16(sp, w, place):
    nd = len(sp.full_shape)

    def body(p_ref, w_ref, o_ref):
        o_ref[...] = w_ref[...].astype(BF16)

    return pl.pallas_call(
        body, name="place_" + sp.name,
        grid_spec=pltpu.PrefetchScalarGridSpec(
            num_scalar_prefetch=1, grid=(1,),
            in_specs=[pl.BlockSpec(sp.shard_shape, lambda i, p_ref: (0,) * nd)],
            out_specs=pl.BlockSpec(sp.shard_shape,
                                   lambda i, p_ref: tuple(p_ref[0] if a == sp.shard_axis else 0 for a in range(nd)))),
        out_shape=jax.ShapeDtypeStruct(sp.full_shape, BF16),
        compiler_params=_params(("arbitrary",)),
    )(place, w)


def _matmul_f32(a, b, name):
    def body(a_ref, b_ref, o_ref):
        o_ref[...] = jnp.dot(a_ref[...], b_ref[...], preferred_element_type=F32, precision=lax.Precision.HIGHEST)

    return pl.pallas_call(body, name=name, out_shape=jax.ShapeDtypeStruct((a.shape[0], b.shape[1]), F32),
                          compiler_params=pltpu.CompilerParams(vmem_limit_bytes=VMEM_LIMIT))(a, b)


def _sum_devices(stacked):
    def body(x_ref, o_ref):
        acc = x_ref[0]
        for d in range(1, NDEV):
            acc = acc + x_ref[d]
        o_ref[...] = acc

    return pl.pallas_call(body, name="sum_devices", out_shape=jax.ShapeDtypeStruct(stacked.shape[1:], F32),
                          compiler_params=pltpu.CompilerParams(vmem_limit_bytes=VMEM_LIMIT))(stacked)


def _adamw(w, g, m, v, name):
    r, cdim = w.shape
    tr = r if r <= 256 else (256 if r % 256 == 0 else r // 2)

    def body(w_ref, g_ref, m_ref, v_ref, d_ref, nm_ref, nv_ref):
        gv = g_ref[...]
        nm = ADAM_B1 * m_ref[...] + (1.0 - ADAM_B1) * gv
        nv = ADAM_B2 * v_ref[...] + (1.0 - ADAM_B2) * (gv * gv)
        m_hat = nm / (1.0 - ADAM_B1 ** ADAM_STEP)
        v_hat = nv / (1.0 - ADAM_B2 ** ADAM_STEP)
        d_ref[...] = -ADAM_LR * (m_hat / (jnp.sqrt(v_hat) + ADAM_EPS) + ADAM_WD * w_ref[...])
        nm_ref[...] = nm
        nv_ref[...] = nv

    blk = pl.BlockSpec((tr, cdim), lambda i: (i, 0))
    return pl.pallas_call(
        body, name="adamw_" + name, grid=(r // tr,), in_specs=[blk] * 4, out_specs=[blk] * 3,
        out_shape=[jax.ShapeDtypeStruct(w.shape, F32)] * 3,
        compiler_params=_params(("parallel",)),
    )(w, g, m, v)


WEIGHT_NAMES = ("g_pre_mix", "g_post_mix", "g_pre_ffn", "g_post_ffn", "w_ada", "b_ada", "w_in", "w_pool",
                "pool_scale", "conv_w", "conv_b", "w_bout", "w_o", "w_up", "ffn_conv_w", "ffn_conv_b", "w_down")
MATRIX_NAMES = ("w_ada",) + tuple(sp.name for sp in SHARDED)
VECTOR_NAMES = tuple(n for n in WEIGHT_NAMES if n not in MATRIX_NAMES)

CW = D // NCHIP
FCW = F2 // NCHIP
ADA_W = DIN // NCHIP
COND_BLOCK = (8, 768)
GRAD_BLOCK = (8, 4864)


def _flat_pad(parts, shape):
    flat = jnp.concatenate([p.reshape(-1) for p in parts])
    return jnp.pad(flat, (0, shape[0] * shape[1] - flat.shape[0])).reshape(shape)


def _take(flat, offset, shape):
    size = 1
    for n in shape:
        size *= n
    return flat[offset:offset + size].reshape(shape), offset + size


def kernel(x, c, g_pre_mix, g_post_mix, g_pre_ffn, g_post_ffn, w_ada, b_ada, w_in, w_pool, pool_scale, conv_w, conv_b, w_bout, w_o, w_up, ffn_conv_w, ffn_conv_b, w_down, loss_target, m_g_pre_mix, m_g_post_mix, m_g_pre_ffn, m_g_post_ffn, m_w_ada, m_b_ada, m_w_in, m_w_pool, m_pool_scale, m_conv_w, m_conv_b, m_w_bout, m_w_o, m_w_up, m_ffn_conv_w, m_ffn_conv_b, m_w_down, v_g_pre_mix, v_g_post_mix, v_g_pre_ffn, v_g_post_ffn, v_w_ada, v_b_ada, v_w_in, v_w_pool, v_pool_scale, v_conv_w, v_conv_b, v_w_bout, v_w_o, v_w_up, v_ffn_conv_w, v_ffn_conv_b, v_w_down):
    weights = dict(g_pre_mix=g_pre_mix, g_post_mix=g_post_mix, g_pre_ffn=g_pre_ffn, g_post_ffn=g_post_ffn,
                   w_ada=w_ada, b_ada=b_ada, w_in=w_in, w_pool=w_pool, pool_scale=pool_scale, conv_w=conv_w,
                   conv_b=conv_b, w_bout=w_bout, w_o=w_o, w_up=w_up, ffn_conv_w=ffn_conv_w, ffn_conv_b=ffn_conv_b,
                   w_down=w_down)
    mom1 = dict(g_pre_mix=m_g_pre_mix, g_post_mix=m_g_post_mix, g_pre_ffn=m_g_pre_ffn, g_post_ffn=m_g_post_ffn,
                w_ada=m_w_ada, b_ada=m_b_ada, w_in=m_w_in, w_pool=m_w_pool, pool_scale=m_pool_scale,
                conv_w=m_conv_w, conv_b=m_conv_b, w_bout=m_w_bout, w_o=m_w_o, w_up=m_w_up,
                ffn_conv_w=m_ffn_conv_w, ffn_conv_b=m_ffn_conv_b, w_down=m_w_down)
    mom2 = dict(g_pre_mix=v_g_pre_mix, g_post_mix=v_g_post_mix, g_pre_ffn=v_g_pre_ffn, g_post_ffn=v_g_post_ffn,
                w_ada=v_w_ada, b_ada=v_b_ada, w_in=v_w_in, w_pool=v_w_pool, pool_scale=v_pool_scale,
                conv_w=v_conv_w, conv_b=v_conv_b, w_bout=v_w_bout, w_o=v_w_o, w_up=v_w_up,
                ffn_conv_w=v_ffn_conv_w, ffn_conv_b=v_ffn_conv_b, w_down=v_w_down)

    chip = 2 * lax.axis_index("x") + lax.axis_index("y")
    core = lax.axis_index("c")
    dev = 2 * chip + core
    place = jnp.stack([chip, core, dev]).astype(jnp.int32)

    cond = _all_gather_small(_flat_pad([c, conv_w, ffn_conv_w], COND_BLOCK), "gather_cond")
    cond = cond.reshape(NDEV, -1)
    c_all = cond[:, :D]
    by_chip = cond[0::2]
    conv_w_full = by_chip[:, D:D + 3 * CW].reshape(NCHIP, 3, CW).transpose(1, 0, 2).reshape(3, D)
    ffn_w_full = by_chip[:, D + 3 * CW:D + 3 * CW + 3 * FCW].reshape(NCHIP, 3, FCW).transpose(1, 0, 2).reshape(3, F2)

    mod_cols = _all_gather_small(_matmul_f32(c_all, w_ada[0], "ada_mod"), "gather_mod")
    mod_cols = mod_cols.reshape(NDEV, NDEV, ADA_W)[0::2]
    mod = lax.dynamic_index_in_dim(mod_cols, dev, axis=1, keepdims=False).reshape(6, D) + b_ada.reshape(6, D)
    vec_d = jnp.concatenate([mod, g_pre_mix, g_post_mix, g_pre_ffn, g_post_ffn, pool_scale, conv_b, conv_w_full,
                             jnp.zeros((VD_ROWS - 15, D), F32)], axis=0)
    vec_f = jnp.concatenate([ffn_w_full, ffn_conv_b, jnp.zeros((FV_ROWS - 4, F2), F32)], axis=0)

    placed = [_place_bf16(sp, weights[sp.name][0], place) for sp in SHARDED]
    loss_blk, dx, vecs, local, received = _local_step(x[0], loss_target[0], vec_d, vec_f, placed, place)

    dmod = [vecs[n] for n in ("dsh1", "dsc1", "dgt1", "dsh2", "dsc2", "dgt2")]
    small = [vecs["dg_pre_mix"], vecs["dg_post_mix"], vecs["dg_pre_ffn"], vecs["dg_post_ffn"]] + dmod + [
        vecs["dpool_scale"], vecs["dconv_w"], vecs["dconv_b"], vecs["dffn_conv_w"], vecs["dffn_conv_b"],
        loss_blk[0]]
    gathered = _all_gather_small(_flat_pad(small, GRAD_BLOCK), "gather_vector_grads")
    total = _sum_devices(gathered.reshape((NDEV,) + GRAD_BLOCK)).reshape(-1)
    vgrad = {}
    off = 0
    for n in ("g_pre_mix", "g_post_mix", "g_pre_ffn", "g_post_ffn"):
        vgrad[n], off = _take(total, off, (1, D))
    dmod_off = off
    vgrad["b_ada"], off = _take(total, off, (1, DIN))
    vgrad["pool_scale"], off = _take(total, off, (1, D))
    g_conv_w, off = _take(total, off, (3, D))
    vgrad["conv_w"] = lax.dynamic_slice_in_dim(g_conv_w, chip * CW, CW, axis=1)[None]
    vgrad["conv_b"], off = _take(total, off, (1, D))
    g_ffn_w, off = _take(total, off, (3, F2))
    vgrad["ffn_conv_w"] = lax.dynamic_slice_in_dim(g_ffn_w, chip * FCW, FCW, axis=1)[None]
    vgrad["ffn_conv_b"], off = _take(total, off, (1, F2))
    loss = total[off]

    dmod_all = gathered.reshape(NDEV, -1)[:, dmod_off:dmod_off + DIN]
    dmod_cols = lax.dynamic_slice_in_dim(dmod_all, chip * ADA_W, ADA_W, axis=1)
    g_ada = _matmul_f32(jnp.pad(c_all.T, ((0, 0), (0, 128 - NDEV))), jnp.pad(dmod_cols, ((0, 128 - NDEV), (0, 0))),
                        "ada_wgrad")

    reduced = _reduce_scatter(local, received, place)
    mgrad = {"w_ada": g_ada}
    for sp, g in zip(SHARDED, reduced):
        mgrad[sp.name] = g

    grad, delta, new_m, new_v = {}, {}, {}, {}
    for n in MATRIX_NAMES:
        shape = weights[n].shape
        two_d = (-1, shape[-1])
        d, nm, nv = _adamw(weights[n].reshape(two_d), mgrad[n].reshape(two_d), mom1[n].reshape(two_d),
                           mom2[n].reshape(two_d), n)
        grad[n], delta[n], new_m[n], new_v[n] = (a.reshape(shape) for a in (mgrad[n], d, nm, nv))
    flat = lambda tree: jnp.concatenate([tree[n].reshape(1, -1) for n in VECTOR_NAMES], axis=1)
    d, nm, nv = _adamw(flat(weights), flat(vgrad), flat(mom1), flat(mom2), "vectors")
    off = 0
    for n in VECTOR_NAMES:
        shape = weights[n].shape
        grad[n] = vgrad[n].reshape(shape)
        delta[n], _ = _take(d[0], off, shape)
        new_m[n], _ = _take(nm[0], off, shape)
        new_v[n], off = _take(nv[0], off, shape)

    return (loss, dx[None], *[grad[n] for n in WEIGHT_NAMES], *[delta[n] for n in WEIGHT_NAMES],
            *[new_m[n] for n in WEIGHT_NAMES], *[new_v[n] for n in WEIGHT_NAMES])
```

```python
import jax
import jax.numpy as jnp
from jax import lax
from jax.experimental import pallas as pl
from jax.experimental.pallas import tpu as pltpu

F32 = jnp.float32
BF16 = jnp.bfloat16

D = 1024
DIN = 6 * D
F = 2816
F2 = 2 * F
NG = 4
GW = D // NG
POOL_CARRY = 16
CONV_CARRY = 3
EPS = 1e-6
NCHIP = 4
NDEV = 8

ADAM_LR = 0.001
ADAM_B1 = 0.9
ADAM_B2 = 0.999
ADAM_EPS = 1e-08
ADAM_WD = 0.01
ADAM_STEP = 10

VMEM_LIMIT = 60 * 1024 * 1024

(V_SH1, V_SC1, V_GT1, V_SH2, V_SC2, V_GT2, V_GPRE1, V_GPOST1, V_GPRE2, V_GPOST2,
 V_PSCALE, V_CB, V_CW0, V_CW1, V_CW2) = range(15)
VD_ROWS = 16
FV_W0, FV_W1, FV_W2, FV_B = range(4)
FV_ROWS = 8

MESH = pl.DeviceIdType.MESH


def _params(sem=None, vmem=VMEM_LIMIT):
    return pltpu.CompilerParams(dimension_semantics=sem, vmem_limit_bytes=vmem)


def _row(ref, r):
    return ref[r:r + 1, :]


def _load_once(pairs, sem):
    @pl.when(pl.program_id(0) == 0)
    def _():
        copies = [pltpu.make_async_copy(src, dst, sem.at[n]) for n, (src, dst) in enumerate(pairs)]
        for cp in copies:
            cp.start()
        for cp in copies:
            cp.wait()


def _dot(a, b):
    return jnp.dot(a, b, preferred_element_type=F32)


def _dot_nt(a, b):
    return lax.dot_general(a, b, (((1,), (1,)), ((), ())), preferred_element_type=F32)


BLK = 256
SEG = BLK // 8


def _load_rows(ref, ts):
    blocks = [jnp.swapaxes(ref[b * BLK:(b + 1) * BLK, :].reshape(8, SEG, ref.shape[-1]), 0, 1).reshape(BLK, -1)
              for b in range(ts // BLK)]
    return jnp.concatenate(blocks, axis=0)


def _store_rows(ref, val, ts):
    for b in range(ts // BLK):
        blk = val[b * BLK:(b + 1) * BLK, :].reshape(SEG, 8, val.shape[-1])
        ref[b * BLK:(b + 1) * BLK, :] = jnp.swapaxes(blk, 0, 1).reshape(BLK, -1)


def _times(t0):
    p = lax.broadcasted_iota(jnp.int32, (BLK, 1), 0)
    return t0 + (p & 7) * SEG + (p >> 3)


def _before(x, carry, s):
    x3 = x.reshape(SEG, 8, x.shape[-1])
    tail = pltpu.roll(x3[SEG - s:], 1, 1)
    row = lax.broadcasted_iota(jnp.int32, tail.shape, 1)
    out = jnp.concatenate([jnp.where(row == 0, carry, tail), x3[:SEG - s]], axis=0)
    return out.reshape(x.shape), tail


def _after(x, carry, s):
    x3 = x.reshape(SEG, 8, x.shape[-1])
    head = pltpu.roll(x3[:s], 7, 1)
    row = lax.broadcasted_iota(jnp.int32, head.shape, 1)
    out = jnp.concatenate([x3[s:], jnp.where(row == 7, carry, head)], axis=0)
    return out.reshape(x.shape), head


def _causal_conv(x, carry, cols, w0, w1, w2, b):
    x1, carry[0:1, :, cols] = _before(x, carry[0:1, :, cols], 1)
    x2, carry[1:3, :, cols] = _before(x, carry[1:3, :, cols], 2)
    return b + w2 * x + w1 * x1 + w0 * x2


def _causal_conv_bwd(dy, carry, cols, w0, w1, w2):
    d1, carry[0:1, :, cols] = _after(dy, carry[0:1, :, cols], 1)
    d2, carry[1:3, :, cols] = _after(dy, carry[1:3, :, cols], 2)
    return w2 * dy + w1 * d1 + w0 * d2, d1, d2


def _pool_counts(t0, g):
    return jnp.minimum((_times(t0) + 1).astype(F32), float(2 << g))


def _rms(x):
    return lax.rsqrt(jnp.mean(x * x, axis=-1, keepdims=True) + EPS)


def _rms_bwd(dn, n, r):
    return r * (dn - n * jnp.mean(dn * n, axis=-1, keepdims=True))


def _colsum(x):
    return jnp.sum(x, axis=0, keepdims=True)


def _gelu_and_grad(x):
    k = 0.7978845608028654
    inner = k * (x + 0.044715 * (x * x * x))
    th = jnp.tanh(inner)
    gelu = 0.5 * x * (1.0 + th)
    dgelu = 0.5 * (1.0 + th) + 0.5 * x * (1.0 - th * th) * (k * (1.0 + 3.0 * 0.044715 * (x * x)))
    return gelu, dgelu


def _fwd_proj(x, vec_d, placed_in, placed_rest, place, ts):
    s = x.shape[0]
    nt = s // ts
    cw = DIN // NCHIP
    sp_in = SHARDED[0]
    gather = _WeightGather(SHARDED[1:5])
    n = gather.n

    def body(*refs):
        p_ref, x_ref, v_ref = refs[:3]
        proj_ref, h1_ref, xs_ref, w_full = refs[4 + n:8 + n]
        rest = refs[8 + n:8 + 2 * n]
        w_vmem, h1_all, sem, in_send, in_recv, send_sems, recv_sems = refs[8 + 2 * n:]
        j, i = pl.program_id(0), pl.program_id(1)
        x_, y_, c, k_me, _, _ = _mesh_place()
        sibling = (x_, y_, 1 - c)

        def peer(t):
            return (x_ ^ (t >> 1), y_ ^ (t & 1))

        def w_in_sends():
            mine = sp_in.piece(w_full, k_me, c)
            return [_remote(mine, mine, in_send.at[t - 1], in_recv.at[t - 1], (*peer(t), c)) for t in (1, 2, 3)]

        def load_block(k):
            cp = pltpu.make_async_copy(sp_in.shard(w_full, k), w_vmem.at[k], sem.at[0])
            cp.start()
            cp.wait()

        @pl.when((j == 0) & (i == 0))
        def _():
            for cp in w_in_sends():
                cp.start()
            gather.start(rest, send_sems, recv_sems)
            load_block(k_me)

        for t in (1, 2, 3):
            @pl.when((j == t) & (i == 0))
            def _(t=t):
                k = k_me ^ t
                landed = sp_in.piece(w_full, k, c)
                _remote(landed, landed, in_send.at[t - 1], in_recv.at[t - 1], (*peer(t), c)).wait_recv()
                _remote(landed, landed, in_send.at[2 + t], in_recv.at[2 + t], sibling).start()
                other = sp_in.piece(w_full, k, 1 - c)
                _remote(other, other, in_send.at[2 + t], in_recv.at[2 + t], sibling).wait_recv()
                load_block(k)

        @pl.when(j == 0)
        def _():
            xv = _load_rows(x_ref, ts)
            xs_ref[...] = xv
            n1 = xv * _rms(xv)
            h = n1 * (_row(v_ref, V_GPRE1) * (1.0 + _row(v_ref, V_SC1))) + _row(v_ref, V_SH1)
            hb = h.astype(BF16)
            h1_ref[...] = hb
            h1_all[i] = hb

        proj_ref[...] = _dot(h1_all[i], w_vmem[k_me ^ j]).astype(BF16)

        @pl.when((j == NCHIP - 1) & (i == nt - 1))
        def _():
            for cp in w_in_sends():
                cp.wait_send()
            for t in (1, 2, 3):
                landed = sp_in.piece(w_full, k_me ^ t, c)
                _remote(landed, landed, in_send.at[2 + t], in_recv.at[2 + t], sibling).wait_send()
            gather.finish(rest, send_sems, recv_sems)

    once = lambda w: pl.BlockSpec((ts, w), lambda j, i, p: (jnp.where(j == 0, i, nt - 1), 0))
    return pl.pallas_call(
        body, name="fwd_proj",
        grid_spec=pltpu.PrefetchScalarGridSpec(
            num_scalar_prefetch=1, grid=(NCHIP, nt),
            in_specs=[once(D), pl.BlockSpec((VD_ROWS, D), lambda j, i, p: (0, 0)),
                      pl.BlockSpec(memory_space=pl.ANY)] + gather.specs_any,
            out_specs=[pl.BlockSpec((ts, cw), lambda j, i, p: (i, p[0] ^ j)), once(D), once(D),
                       pl.BlockSpec(memory_space=pl.ANY)] + gather.specs_any,
            scratch_shapes=[pltpu.VMEM((NCHIP, D, cw), BF16), pltpu.VMEM((nt, ts, D), BF16),
                            pltpu.SemaphoreType.DMA((1,)),
                            pltpu.SemaphoreType.DMA((6,)), pltpu.SemaphoreType.DMA((6,))] + gather.scratch),
        out_shape=[jax.ShapeDtypeStruct((s, DIN), BF16), jax.ShapeDtypeStruct((s, D), BF16),
                   jax.ShapeDtypeStruct((s, D), F32), jax.ShapeDtypeStruct(sp_in.full_shape, BF16)] + gather.out_shape,
        input_output_aliases={3 + w: 3 + w for w in range(n + 1)},
        compiler_params=_params(("arbitrary", "arbitrary")),
    )(place, x, vec_d, placed_in, *placed_rest)


def _fwd_mix(proj, x, vec_d, w_pool, w_bout, w_o, placed_ffn, ts):
    s = x.shape[0]
    gather = _WeightGather(SHARDED[5:])
    n = gather.n

    def body(*refs):
        ins, outs, rest = refs[:6], refs[6 + n:14 + n], refs[14 + n:14 + 2 * n]
        scratch, sems = refs[14 + 2 * n:-2], refs[-2:]
        i = pl.program_id(0)
        nt = s // ts
        pl.when(i == 0)(lambda: gather.start(rest, *sems))
        pl.when(i == nt // 2)(lambda: gather.forward(rest, *sems))
        compute(*ins, *outs, *scratch)
        pl.when(i == nt - 1)(lambda: gather.drain(rest, *sems))

    def compute(p_ref, x_ref, v_ref, wp_hbm, wb_hbm, wo_hbm,
                x1_ref, o_ref, pg_ref, q_ref, mg_ref, ya_ref, yb_ref, cv_ref,
                wp, wb, wo, carry_p, carry_v, sem):
        i = pl.program_id(0)
        _load_once([(wp_hbm, wp), (wb_hbm, wb), (wo_hbm, wo)], sem)

        @pl.when(i == 0)
        def _():
            carry_p[...] = jnp.zeros_like(carry_p)
            carry_v[...] = jnp.zeros_like(carry_v)

        t0 = i * ts
        for g in range(NG):
            cols = slice(g * GW, (g + 1) * GW)
            u = p_ref[:, cols].astype(F32)
            e = u
            for l in range(g + 1):
                slot = slice((1 << l) - 1, (2 << l) - 1)
                shifted, carry_p[slot, :, cols] = _before(e, carry_p[slot, :, cols], 1 << l)
                e = e + shifted
            pgb = (e / _pool_counts(t0, g) - u).astype(BF16)
            pg_ref[:, cols] = pgb
            ya_ref[:, cols] = _dot(pgb, wp[g]).astype(BF16)

        u_x = p_ref[:, D:2 * D].astype(F32)
        u_c = p_ref[:, 3 * D:4 * D].astype(F32)
        v = u_c * u_x
        cv = _causal_conv(v, carry_v, slice(None), _row(v_ref, V_CW0), _row(v_ref, V_CW1),
                          _row(v_ref, V_CW2), _row(v_ref, V_CB))
        cv_ref[...] = cv.astype(BF16)
        q = (p_ref[:, 2 * D:3 * D].astype(F32) * cv).astype(BF16)
        q_ref[...] = q
        y_b = _dot(q, wb[...])
        yb_ref[...] = y_b.astype(BF16)

        y_a = ya_ref[...].astype(F32) * _row(v_ref, V_PSCALE)
        merged = (jax.nn.sigmoid(p_ref[:, 4 * D:5 * D].astype(F32)) * y_a
                  + jax.nn.sigmoid(p_ref[:, 5 * D:6 * D].astype(F32)) * y_b).astype(BF16)
        mg_ref[...] = merged
        o = _dot(merged, wo[...])
        o_ref[...] = o
        x1_ref[...] = x_ref[...] + _row(v_ref, V_GT1) * ((o * _rms(o)) * _row(v_ref, V_GPOST1))

    tile = lambda w: pl.BlockSpec((ts, w), lambda i: (i, 0))
    hbm = pl.BlockSpec(memory_space=pl.ANY)
    return pl.pallas_call(
        body, name="fwd_mix", grid=(s // ts,),
        in_specs=[tile(DIN), tile(D), pl.BlockSpec((VD_ROWS, D), lambda i: (0, 0)), hbm, hbm, hbm] + gather.specs_any,
        out_specs=[tile(D)] * 8 + gather.specs_any,
        out_shape=[jax.ShapeDtypeStruct((s, D), F32), jax.ShapeDtypeStruct((s, D), F32)]
        + [jax.ShapeDtypeStruct((s, D), BF16)] * 6 + gather.out_shape,
        input_output_aliases={6 + w: 8 + w for w in range(n)},
        scratch_shapes=[pltpu.VMEM((NG, GW, GW), BF16), pltpu.VMEM((D, D), BF16), pltpu.VMEM((D, D), BF16),
                        pltpu.VMEM((POOL_CARRY, 8, D), F32), pltpu.VMEM((CONV_CARRY, 8, D), F32),
                        pltpu.SemaphoreType.DMA((3,))] + gather.scratch,
        compiler_params=_params(("arbitrary",)),
    )(proj, x, vec_d, w_pool, w_bout, w_o, *placed_ffn)


def _fwd_ffn(x1, tgt, vec_d, vec_f, w_up, w_down, ts):
    s = x1.shape[0]

    def body(x1_ref, t_ref, v_ref, f_ref, wu_hbm, wd_hbm,
             up_ref, upc_ref, a_ref, h2_ref, dx2_ref, dff_ref, vo_ref, loss_ref,
             wu, wd, carry, sem):
        i = pl.program_id(0)
        _load_once([(wu_hbm, wu), (wd_hbm, wd)], sem)

        @pl.when(i == 0)
        def _():
            carry[...] = jnp.zeros_like(carry)
            vo_ref[...] = jnp.zeros_like(vo_ref)
            loss_ref[...] = jnp.zeros_like(loss_ref)

        x1v = x1_ref[...]
        n3 = x1v * _rms(x1v)
        h2 = (n3 * (_row(v_ref, V_GPRE2) * (1.0 + _row(v_ref, V_SC2))) + _row(v_ref, V_SH2)).astype(BF16)
        h2_ref[...] = h2

        ff = jnp.zeros((ts, D), F32)
        for lo, hi in FFN_SLABS:
            up = []
            for cols in (slice(lo, hi), slice(F + lo, F + hi)):
                u0 = _dot(h2, wu[:, cols])
                up_ref[:, cols] = u0.astype(BF16)
                y = _causal_conv(u0, carry, cols, f_ref[FV_W0:FV_W0 + 1, cols], f_ref[FV_W1:FV_W1 + 1, cols],
                                 f_ref[FV_W2:FV_W2 + 1, cols], f_ref[FV_B:FV_B + 1, cols])
                upc_ref[:, cols] = y.astype(BF16)
                up.append(y)
            gelu, _ = _gelu_and_grad(up[0])
            a = (gelu * up[1]).astype(BF16)
            a_ref[:, lo:hi] = a
            ff = ff + _dot(a, wd[lo:hi, :])

        r4 = _rms(ff)
        n4 = ff * r4
        gt2 = _row(v_ref, V_GT2)
        gpost = _row(v_ref, V_GPOST2)
        y4 = n4 * gpost
        diff = (x1v + gt2 * y4) - _load_rows(t_ref, ts)
        loss_ref[...] += jnp.full(loss_ref.shape, 0.5 / D * jnp.sum(diff * diff), F32)
        dx2 = diff * (1.0 / D)
        dx2_ref[...] = dx2
        dy4 = dx2 * gt2
        vo_ref[0:1, :] += _colsum(dx2 * y4)
        vo_ref[1:2, :] += _colsum(dy4 * n4)
        dff_ref[...] = _rms_bwd(dy4 * gpost, n4, r4).astype(BF16)

    tile = lambda w: pl.BlockSpec((ts, w), lambda i: (i, 0))
    full = lambda r, w: pl.BlockSpec((r, w), lambda i: (0, 0))
    hbm = pl.BlockSpec(memory_space=pl.ANY)
    return pl.pallas_call(
        body, name="fwd_ffn", grid=(s // ts,),
        in_specs=[tile(D), tile(D), full(VD_ROWS, D), full(FV_ROWS, F2), hbm, hbm],
        out_specs=[tile(F2), tile(F2), tile(F), tile(D), tile(D), tile(D), full(8, D), full(8, 128)],
        out_shape=[jax.ShapeDtypeStruct((s, F2), BF16), jax.ShapeDtypeStruct((s, F2), BF16),
                   jax.ShapeDtypeStruct((s, F), BF16),
                   jax.ShapeDtypeStruct((s, D), BF16), jax.ShapeDtypeStruct((s, D), F32),
                   jax.ShapeDtypeStruct((s, D), BF16), jax.ShapeDtypeStruct((8, D), F32),
                   jax.ShapeDtypeStruct((8, 128), F32)],
        scratch_shapes=[pltpu.VMEM((D, F2), BF16), pltpu.VMEM((F, D), BF16), pltpu.VMEM((CONV_CARRY, 8, F2), F32),
                        pltpu.SemaphoreType.DMA((2,))],
        compiler_params=_params(("arbitrary",)),
    )(x1, tgt, vec_d, vec_f, w_up, w_down)


def _bwd_ffn(dff, dx2, x1, up0, upc, vec_d, vec_f, w_up, w_down, exchange, ex_grads, ts):
    s = x1.shape[0]
    nt = s // ts
    n = exchange.n

    def body(*refs):
        ins, grads = refs[:9], refs[9:9 + n]
        outs, recvs = refs[9 + n:13 + n], refs[13 + n:13 + 2 * n]
        scratch, sems = refs[13 + 2 * n:-2], refs[-2:]
        i = pl.program_id(0)
        pl.when(i == 0)(lambda: exchange.start(grads, recvs, *sems))
        compute(*ins, *outs, *scratch)
        pl.when(i == nt - 1)(lambda: exchange.finish(grads, recvs, *sems))

    def compute(dff_ref, dx2_ref, x1_ref, up_ref, upc_ref, v_ref, f_ref, wu_hbm, wd_hbm,
                dx1_ref, dup_ref, vo_ref, fo_ref, wu, wd, carry, sem):
        i = pl.program_id(0)
        _load_once([(wu_hbm, wu), (wd_hbm, wd)], sem)

        @pl.when(i == 0)
        def _():
            carry[...] = jnp.zeros_like(carry)
            vo_ref[...] = jnp.zeros_like(vo_ref)
            fo_ref[...] = jnp.zeros_like(fo_ref)

        dffb = dff_ref[...]

        dh2 = jnp.zeros((ts, D), F32)
        for lo, hi in FFN_SLABS:
            slabs = (slice(lo, hi), slice(F + lo, F + hi))
            gelu, dgelu = _gelu_and_grad(upc_ref[:, slabs[0]].astype(F32))
            da = _dot_nt(dffb, wd[lo:hi, :])
            dups = (da * upc_ref[:, slabs[1]].astype(F32) * dgelu, da * gelu)
            for cols, dup in zip(slabs, dups):
                du0, d1, d2 = _causal_conv_bwd(dup, carry, cols, f_ref[FV_W0:FV_W0 + 1, cols],
                                               f_ref[FV_W1:FV_W1 + 1, cols], f_ref[FV_W2:FV_W2 + 1, cols])
                u0 = up_ref[:, cols].astype(F32)
                fo_ref[FV_B:FV_B + 1, cols] += _colsum(dup)
                fo_ref[FV_W2:FV_W2 + 1, cols] += _colsum(dup * u0)
                fo_ref[FV_W1:FV_W1 + 1, cols] += _colsum(d1 * u0)
                fo_ref[FV_W0:FV_W0 + 1, cols] += _colsum(d2 * u0)
                du0 = du0.astype(BF16)
                dup_ref[:, cols] = du0
                dh2 = dh2 + _dot_nt(du0, wu[:, cols])

        x1v = x1_ref[...]
        r3 = _rms(x1v)
        n3 = x1v * r3
        gpre = _row(v_ref, V_GPRE2)
        sc = 1.0 + _row(v_ref, V_SC2)
        vo_ref[0:1, :] += _colsum(dh2)
        vo_ref[1:2, :] += _colsum(dh2 * n3 * gpre)
        vo_ref[2:3, :] += _colsum(dh2 * n3 * sc)
        dx1_ref[...] = dx2_ref[...] + _rms_bwd(dh2 * (gpre * sc), n3, r3)

    rev = lambda w: pl.BlockSpec((ts, w), lambda i: (nt - 1 - i, 0))
    full = lambda r, w: pl.BlockSpec((r, w), lambda i: (0, 0))
    hbm = pl.BlockSpec(memory_space=pl.ANY)
    return pl.pallas_call(
        body, name="bwd_ffn", grid=(nt,),
        in_specs=[rev(D), rev(D), rev(D), rev(F2), rev(F2), full(VD_ROWS, D), full(FV_ROWS, F2), hbm, hbm]
        + exchange.specs_any,
        out_specs=[rev(D), rev(F2), full(8, D), full(FV_ROWS, F2)] + exchange.specs_any,
        out_shape=[jax.ShapeDtypeStruct((s, D), F32), jax.ShapeDtypeStruct((s, F2), BF16),
                   jax.ShapeDtypeStruct((8, D), F32), jax.ShapeDtypeStruct((FV_ROWS, F2), F32)] + exchange.out_shape,
        scratch_shapes=[pltpu.VMEM((D, F2), BF16), pltpu.VMEM((F, D), BF16), pltpu.VMEM((CONV_CARRY, 8, F2), F32),
                        pltpu.SemaphoreType.DMA((2,))] + exchange.scratch,
        compiler_params=_params(("arbitrary",)),
    )(dff, dx2, x1, up0, upc, vec_d, vec_f, w_up, w_down, *ex_grads)


def _bwd_mix(dx1, o, proj, cv, ya0, yb, vec_d, w_pool, w_bout, w_o, exchange, ex_grads, ts):
    s = dx1.shape[0]
    nt = s // ts
    n = exchange.n

    def body(*refs):
        ins, grads = refs[:10], refs[10:10 + n]
        outs, recvs = refs[10 + n:15 + n], refs[15 + n:15 + 2 * n]
        scratch, sems = refs[15 + 2 * n:-2], refs[-2:]
        i = pl.program_id(0)
        pl.when(i == 0)(lambda: exchange.start(grads, recvs, *sems))
        compute(*ins, *outs, *scratch)
        pl.when(i == nt - 1)(lambda: exchange.finish(grads, recvs, *sems))

    def compute(dx1_ref, o_ref, p_ref, cv_ref, ya_ref, yb_ref, v_ref, wp_hbm, wb_hbm, wo_hbm,
                dp_ref, do_ref, dyb_ref, dya_ref, vo_ref, wp, wb, wo, carry_d, carry_e, sem):
        i = pl.program_id(0)
        _load_once([(wp_hbm, wp), (wb_hbm, wb), (wo_hbm, wo)], sem)

        @pl.when(i == 0)
        def _():
            carry_d[...] = jnp.zeros_like(carry_d)
            carry_e[...] = jnp.zeros_like(carry_e)
            vo_ref[...] = jnp.zeros_like(vo_ref)

        t0 = (nt - 1 - i) * ts
        dx1v = dx1_ref[...]
        ov = o_ref[...]
        r2 = _rms(ov)
        n2 = ov * r2
        gpost = _row(v_ref, V_GPOST1)
        vo_ref[0:1, :] += _colsum(dx1v * (n2 * gpost))
        dy2 = dx1v * _row(v_ref, V_GT1)
        vo_ref[1:2, :] += _colsum(dy2 * n2)
        dob = _rms_bwd(dy2 * gpost, n2, r2).astype(BF16)
        do_ref[...] = dob
        dmerged = _dot_nt(dob, wo[...])

        ya0 = ya_ref[...].astype(F32)
        pscale = _row(v_ref, V_PSCALE)
        sa = jax.nn.sigmoid(p_ref[:, 4 * D:5 * D].astype(F32))
        dp_ref[:, 4 * D:5 * D] = (dmerged * (ya0 * pscale) * sa * (1.0 - sa)).astype(BF16)
        dy_a = dmerged * sa
        vo_ref[2:3, :] += _colsum(dy_a * ya0)
        dya0 = (dy_a * pscale).astype(BF16)
        dya_ref[...] = dya0

        sb = jax.nn.sigmoid(p_ref[:, 5 * D:6 * D].astype(F32))
        dp_ref[:, 5 * D:6 * D] = (dmerged * yb_ref[...].astype(F32) * sb * (1.0 - sb)).astype(BF16)
        dy_b = (dmerged * sb).astype(BF16)
        dyb_ref[...] = dy_b
        dq = _dot_nt(dy_b, wb[...])

        u_x = p_ref[:, D:2 * D].astype(F32)
        u_b = p_ref[:, 2 * D:3 * D].astype(F32)
        u_c = p_ref[:, 3 * D:4 * D].astype(F32)
        w0, w1, w2 = _row(v_ref, V_CW0), _row(v_ref, V_CW1), _row(v_ref, V_CW2)
        dp_ref[:, 2 * D:3 * D] = (dq * cv_ref[...].astype(F32)).astype(BF16)
        dcv = dq * u_b
        dv, d1, d2 = _causal_conv_bwd(dcv, carry_d, slice(None), w0, w1, w2)
        v = u_c * u_x
        vo_ref[3:4, :] += _colsum(dcv)
        vo_ref[4:5, :] += _colsum(d2 * v)
        vo_ref[5:6, :] += _colsum(d1 * v)
        vo_ref[6:7, :] += _colsum(dcv * v)
        dp_ref[:, D:2 * D] = (dv * u_c).astype(BF16)
        dp_ref[:, 3 * D:4 * D] = (dv * u_x).astype(BF16)

        for g in range(NG):
            cols = slice(g * GW, (g + 1) * GW)
            dpg = _dot_nt(dya0[:, cols], wp[g])
            e = dpg / _pool_counts(t0, g)
            for l in range(g + 1):
                slot = slice((1 << l) - 1, (2 << l) - 1)
                shifted, carry_e[slot, :, cols] = _after(e, carry_e[slot, :, cols], 1 << l)
                e = e + shifted
            dp_ref[:, cols] = (e - dpg).astype(BF16)

    rev = lambda w: pl.BlockSpec((ts, w), lambda i: (nt - 1 - i, 0))
    hbm = pl.BlockSpec(memory_space=pl.ANY)
    return pl.pallas_call(
        body, name="bwd_mix", grid=(nt,),
        in_specs=[rev(D), rev(D), rev(DIN), rev(D), rev(D), rev(D), pl.BlockSpec((VD_ROWS, D), lambda i: (0, 0)),
                  hbm, hbm, hbm] + exchange.specs_any,
        out_specs=[rev(DIN), rev(D), rev(D), rev(D), pl.BlockSpec((8, D), lambda i: (0, 0))] + exchange.specs_any,
        out_shape=[jax.ShapeDtypeStruct((s, DIN), BF16)] + [jax.ShapeDtypeStruct((s, D), BF16)] * 3
        + [jax.ShapeDtypeStruct((8, D), F32)] + exchange.out_shape,
        scratch_shapes=[pltpu.VMEM((NG, GW, GW), BF16), pltpu.VMEM((D, D), BF16), pltpu.VMEM((D, D), BF16),
                        pltpu.VMEM((CONV_CARRY, 8, D), F32), pltpu.VMEM((POOL_CARRY, 8, D), F32),
                        pltpu.SemaphoreType.DMA((3,))] + exchange.scratch,
        compiler_params=_params(("arbitrary",)),
    )(dx1, o, proj, cv, ya0, yb, vec_d, w_pool, w_bout, w_o, *ex_grads)


def _bwd_in(dproj, dx1, x, vec_d, w_in, exchange, ex_grads, ts):
    s = x.shape[0]
    nt = s // ts
    n = exchange.n

    def body(*refs):
        ins, grads = refs[:5], refs[5:5 + n]
        outs, recvs = refs[5 + n:7 + n], refs[7 + n:7 + 2 * n]
        scratch, sems = refs[7 + 2 * n:-2], refs[-2:]
        i = pl.program_id(0)
        pl.when(i == 0)(lambda: exchange.start(grads, recvs, *sems))
        compute(*ins, *outs, *scratch)
        pl.when(i == nt - 1)(lambda: exchange.finish(grads, recvs, *sems))

    def compute(dp_ref, dx1_ref, x_ref, v_ref, w_hbm, dx_ref, vo_ref, w_vmem, sem):
        _load_once([(w_hbm, w_vmem)], sem)

        @pl.when(pl.program_id(0) == 0)
        def _():
            vo_ref[...] = jnp.zeros_like(vo_ref)

        dh1 = _dot_nt(dp_ref[...], w_vmem[...])
        xv = x_ref[...]
        r1 = _rms(xv)
        n1 = xv * r1
        gpre = _row(v_ref, V_GPRE1)
        sc = 1.0 + _row(v_ref, V_SC1)
        vo_ref[0:1, :] += _colsum(dh1)
        vo_ref[1:2, :] += _colsum(dh1 * n1 * gpre)
        vo_ref[2:3, :] += _colsum(dh1 * n1 * sc)
        _store_rows(dx_ref, dx1_ref[...] + _rms_bwd(dh1 * (gpre * sc), n1, r1), ts)

    tile = lambda w: pl.BlockSpec((ts, w), lambda i: (i, 0))
    return pl.pallas_call(
        body, name="bwd_in", grid=(s // ts,),
        in_specs=[tile(DIN), tile(D), tile(D), pl.BlockSpec((VD_ROWS, D), lambda i: (0, 0)),
                  pl.BlockSpec(memory_space=pl.ANY)] + exchange.specs_any,
        out_specs=[tile(D), pl.BlockSpec((8, D), lambda i: (0, 0))] + exchange.specs_any,
        out_shape=[jax.ShapeDtypeStruct((s, D), F32), jax.ShapeDtypeStruct((8, D), F32)] + exchange.out_shape,
        scratch_shapes=[pltpu.VMEM((D, DIN), BF16), pltpu.SemaphoreType.DMA((1,))] + exchange.scratch,
        compiler_params=_params(("arbitrary",)),
    )(dproj, dx1, x, vec_d, w_in, *ex_grads)


def _dot_tn(a, b):
    return lax.dot_general(a, b, (((0,), (0,)), ((), ())), preferred_element_type=F32)


def _wgrad(a, b, tm, tn, ts, name, dtype, exchange=None, ex_grads=()):
    s, m = a.shape
    nn = b.shape[1]
    grid = (m // tm, nn // tn, s // ts)
    n = exchange.n if exchange else 0

    def body(*refs):
        a_ref, b_ref = refs[:2]
        grads = refs[2:2 + n]
        o_ref = refs[2 + n]
        recvs = refs[3 + n:3 + 2 * n]
        acc = refs[3 + 2 * n]
        sems = refs[4 + 2 * n:]
        i, j, k = pl.program_id(0), pl.program_id(1), pl.program_id(2)
        if exchange:
            pl.when((i == 0) & (j == 0) & (k == 0))(lambda: exchange.start(grads, recvs, *sems))
        part = _dot_tn(a_ref[...], b_ref[...])

        @pl.when(k == 0)
        def _():
            acc[...] = part

        @pl.when(k > 0)
        def _():
            acc[...] += part

        @pl.when(k == grid[2] - 1)
        def _():
            o_ref[...] = acc[...].astype(dtype)

        if exchange:
            pl.when((i == grid[0] - 1) & (j == grid[1] - 1) & (k == grid[2] - 1))(
                lambda: exchange.finish(grads, recvs, *sems))

    hosted = exchange.specs_any if exchange else []
    return pl.pallas_call(
        body, name=name, grid=grid,
        in_specs=[pl.BlockSpec((ts, tm), lambda i, j, k: (k, i)), pl.BlockSpec((ts, tn), lambda i, j, k: (k, j))]
        + hosted,
        out_specs=[pl.BlockSpec((tm, tn), lambda i, j, k: (i, j))] + hosted,
        out_shape=[jax.ShapeDtypeStruct((m, nn), dtype)] + (exchange.out_shape if exchange else []),
        scratch_shapes=[pltpu.VMEM((tm, tn), F32)] + (exchange.scratch if exchange else []),
        compiler_params=_params(("arbitrary", "arbitrary", "arbitrary")),
    )(a, b, *ex_grads)


def _wgrad_pool(pg, dya0, ts):
    s = pg.shape[0]
    nk = s // ts

    def body(a_ref, b_ref, o_ref, acc):
        k = pl.program_id(1)
        part = _dot_tn(a_ref[...], b_ref[...])

        @pl.when(k == 0)
        def _():
            acc[...] = part

        @pl.when(k > 0)
        def _():
            acc[...] += part

        @pl.when(k == nk - 1)
        def _():
            o_ref[0] = acc[...].astype(BF16)

    return pl.pallas_call(
        body, name="wgrad_pool", grid=(NG, nk),
        in_specs=[pl.BlockSpec((ts, GW), lambda g, k: (k, g)), pl.BlockSpec((ts, GW), lambda g, k: (k, g))],
        out_specs=pl.BlockSpec((1, GW, GW), lambda g, k: (g, 0, 0)),
        out_shape=jax.ShapeDtypeStruct((NG, GW, GW), BF16),
        scratch_shapes=[pltpu.VMEM((GW, GW), F32)],
        compiler_params=_params(("arbitrary", "arbitrary")),
    )(pg, dya0)


FFN_SLABS = ((0, 1408), (1408, 2816))
TS_PROJ = 512
TS_MIX = 256
TS_FFN = 256
TS_WGRAD = 1024


def _local_step(x, tgt, vec_d, vec_f, placed, place):
    s = x.shape[0]
    tw = min(TS_WGRAD, s)
    sp_in, sp_pool, sp_bout, sp_o, sp_up, sp_down = SHARDED
    proj, h1, xs, w_in, w_pool, w_bout, w_o, w_up = _fwd_proj(x, vec_d, placed[0], placed[1:5], place,
                                                              min(TS_PROJ, s))
    x1, o, pg, q, merged, ya0, yb, cv, w_down = _fwd_mix(proj, xs, vec_d, w_pool, w_bout, w_o, placed[5:],
                                                         min(TS_MIX, s))
    up0, upc, a, h2, dx2, dff, vo_f, loss = _fwd_ffn(x1, tgt, vec_d, vec_f, w_up, w_down, min(TS_FFN, s))
    g_down, = _wgrad(a, dff, F // 2, D, tw, "wgrad_down", BF16)
    dx1, dup0, vo_b, fo, r_down = _bwd_ffn(dff, dx2, x1, up0, upc, vec_d, vec_f, w_up, w_down,
                                           _GradExchange([sp_down]), [g_down], min(TS_FFN, s))
    g_up, = _wgrad(h2, dup0, D, F2 // NCHIP, tw, "wgrad_up", BF16)
    dproj, do, dyb, dya0, vo_m, r_up = _bwd_mix(dx1, o, proj, cv, ya0, yb, vec_d, w_pool, w_bout, w_o,
                                                _GradExchange([sp_up]), [g_up], min(TS_MIX, s))
    g_o, = _wgrad(merged, do, D, D, tw, "wgrad_o", BF16)
    g_bout, = _wgrad(q, dyb, D, D, tw, "wgrad_bout", BF16)
    g_pool = _wgrad_pool(pg, dya0, tw)
    g_in, r_pool, r_bout, r_o = _wgrad(h1, dproj, D, DIN // NCHIP, tw, "wgrad_in", BF16,
                                       _GradExchange([sp_pool, sp_bout, sp_o]), [g_pool, g_bout, g_o])
    dx, vo_i, r_in = _bwd_in(dproj, dx1, xs, vec_d, w_in, _GradExchange([sp_in]), [g_in], min(TS_PROJ, s))
    vecs = dict(
        dsh1=vo_i[0], dsc1=vo_i[1], dg_pre_mix=vo_i[2],
        dgt1=vo_m[0], dg_post_mix=vo_m[1], dpool_scale=vo_m[2], dconv_b=vo_m[3],
        dconv_w=vo_m[4:7],
        dsh2=vo_b[0], dsc2=vo_b[1], dg_pre_ffn=vo_b[2],
        dgt2=vo_f[0], dg_post_ffn=vo_f[1],
        dffn_conv_w=fo[FV_W0:FV_W2 + 1], dffn_conv_b=fo[FV_B],
    )
    local = dict(w_in=g_in, w_pool=g_pool, w_bout=g_bout, w_o=g_o, w_up=g_up, w_down=g_down)
    received = dict(w_in=r_in, w_pool=r_pool, w_bout=r_bout, w_o=r_o, w_up=r_up, w_down=r_down)
    return loss, dx, vecs, local, received


def _aligned(offset, n):
    return offset if isinstance(offset, int) else pl.multiple_of(offset, n)


class _Sharded:
    def __init__(self, name, full_shape, shard_axis, half_axis):
        self.name = name
        self.full_shape = full_shape
        self.shard_axis = shard_axis
        self.half_axis = half_axis
        self.shard_shape = tuple(n // NCHIP if a == shard_axis else n for a, n in enumerate(full_shape))
        self.piece_shape = tuple(n // 2 if a == half_axis else n for a, n in enumerate(self.shard_shape))

    def piece(self, full_ref, k, h):
        idx = []
        for a, n in enumerate(self.piece_shape):
            if a == self.shard_axis and a == self.half_axis:
                idx.append(pl.ds(_aligned((2 * k + h) * n, n), n))
            elif a == self.shard_axis:
                idx.append(pl.ds(_aligned(k * n, n), n))
            elif a == self.half_axis:
                idx.append(pl.ds(_aligned(h * n, n), n))
            else:
                idx.append(slice(None))
        return full_ref.at[tuple(idx)]

    def shard(self, full_ref, k):
        n = self.shard_shape[self.shard_axis]
        idx = [pl.ds(_aligned(k * n, n), n) if a == self.shard_axis else slice(None)
               for a in range(len(self.full_shape))]
        return full_ref.at[tuple(idx)]

    def half(self, shard_ref, h):
        n = self.piece_shape[self.half_axis]
        idx = [pl.ds(_aligned(h * n, n), n) if a == self.half_axis else slice(None)
               for a in range(len(self.full_shape))]
        return shard_ref.at[tuple(idx)]

    def piece_block(self):
        def index_map(k, c_ref):
            c = c_ref[0]
            out = []
            for a in range(len(self.full_shape)):
                if a == self.shard_axis and a == self.half_axis:
                    out.append(2 * k + c)
                elif a == self.shard_axis:
                    out.append(k)
                elif a == self.half_axis:
                    out.append(c)
                else:
                    out.append(0)
            return tuple(out)
        return pl.BlockSpec(self.piece_shape, index_map)


SHARDED = (
    _Sharded("w_in", (D, DIN), 1, 0),
    _Sharded("w_pool", (NG, GW, GW), 1, 0),
    _Sharded("w_bout", (D, D), 0, 0),
    _Sharded("w_o", (D, D), 0, 0),
    _Sharded("w_up", (D, F2), 1, 0),
    _Sharded("w_down", (F, D), 0, 0),
)
NW = len(SHARDED)


def _mesh_place():
    x, y, c = lax.axis_index("x"), lax.axis_index("y"), lax.axis_index("c")
    chips = [(1 - x, y), (x, 1 - y), (1 - x, 1 - y)]
    return x, y, c, 2 * x + y, chips, [2 * px + py for px, py in chips]


def _remote(src, dst, send_sem, recv_sem, device):
    return pltpu.make_async_remote_copy(src_ref=src, dst_ref=dst, send_sem=send_sem, recv_sem=recv_sem,
                                        device_id=device, device_id_type=MESH)


def _all_gather_small(block, name):
    m_per, n = block.shape

    def body(x_ref, out_ref, send_sems, recv_sems, local_sem):
        x, y, c, _, chips, _ = _mesh_place()
        me, sibling = (x, y, c), (x, y, 1 - c)

        def rows(px, py, pc):
            return out_ref.at[pl.ds((4 * px + 2 * py + pc) * m_per, m_per), :]

        def copy(k, blk, to, src=None):
            return _remote(rows(*blk) if src is None else src, rows(*blk), send_sems.at[k], recv_sems.at[k], to)

        mine = pltpu.make_async_copy(x_ref, rows(*me), local_sem)
        mine.start()
        first = [copy(0, me, sibling, src=x_ref)]
        first += [copy(1 + j, me, (*chip, c), src=x_ref) for j, chip in enumerate(chips)]
        for cp in first:
            cp.start()
        passed = [copy(4 + j, (*chip, c), sibling) for j, chip in enumerate(chips)]
        for j, chip in enumerate(chips):
            copy(1 + j, (*chip, c), me).wait_recv()
            passed[j].start()
        copy(0, sibling, me).wait_recv()
        for j, chip in enumerate(chips):
            copy(4 + j, (*chip, 1 - c), me).wait_recv()
        for cp in first + passed:
            cp.wait_send()
        mine.wait()

    return pl.pallas_call(
        body, name=name,
        out_shape=jax.ShapeDtypeStruct((NDEV * m_per, n), block.dtype),
        in_specs=[pl.BlockSpec(memory_space=pltpu.VMEM)],
        out_specs=pl.BlockSpec(memory_space=pltpu.VMEM),
        scratch_shapes=[pltpu.SemaphoreType.DMA((7,)), pltpu.SemaphoreType.DMA((7,)), pltpu.SemaphoreType.DMA],
        compiler_params=pltpu.CompilerParams(vmem_limit_bytes=VMEM_LIMIT),
    )(block)


class _WeightGather:
    def __init__(self, specs):
        self.specs = specs
        self.n = len(specs)
        self.specs_any = [pl.BlockSpec(memory_space=pl.ANY)] * self.n
        self.out_shape = [jax.ShapeDtypeStruct(sp.full_shape, BF16) for sp in specs]
        self.scratch = [pltpu.SemaphoreType.DMA((6 * self.n,)), pltpu.SemaphoreType.DMA((6 * self.n,))]

    def _sends(self, outs, send_sems, recv_sems):
        x, y, c, k_me, chips, _ = _mesh_place()
        sends = []
        for j, chip in enumerate(chips):
            for w, sp in enumerate(self.specs):
                mine = sp.piece(outs[w], k_me, c)
                sends.append(_remote(mine, mine, send_sems.at[6 * w + j], recv_sems.at[6 * w + j], (*chip, c)))
        return sends

    def start(self, outs, send_sems, recv_sems):
        for cp in self._sends(outs, send_sems, recv_sems):
            cp.start()

    def _passes(self, outs, send_sems, recv_sems):
        x, y, c, _, chips, kidx = _mesh_place()
        return [_remote(sp.piece(outs[w], kidx[j], c), sp.piece(outs[w], kidx[j], c),
                        send_sems.at[6 * w + 3 + j], recv_sems.at[6 * w + 3 + j], (x, y, 1 - c))
                for j in range(3) for w, sp in enumerate(self.specs)]

    def forward(self, outs, send_sems, recv_sems):
        x, y, c, _, chips, kidx = _mesh_place()
        for j, chip in enumerate(chips):
            for w, sp in enumerate(self.specs):
                landed = sp.piece(outs[w], kidx[j], c)
                _remote(landed, landed, send_sems.at[6 * w + j], recv_sems.at[6 * w + j], (*chip, c)).wait_recv()
        for cp in self._passes(outs, send_sems, recv_sems):
            cp.start()

    def drain(self, outs, send_sems, recv_sems):
        x, y, c, _, chips, kidx = _mesh_place()
        for j in range(3):
            for w, sp in enumerate(self.specs):
                landed = sp.piece(outs[w], kidx[j], 1 - c)
                _remote(landed, landed, send_sems.at[6 * w + 3 + j], recv_sems.at[6 * w + 3 + j],
                        (x, y, 1 - c)).wait_recv()
        for cp in self._sends(outs, send_sems, recv_sems) + self._passes(outs, send_sems, recv_sems):
            cp.wait_send()

    def finish(self, outs, send_sems, recv_sems):
        self.forward(outs, send_sems, recv_sems)
        self.drain(outs, send_sems, recv_sems)


def _gather_weights(placed, specs, name):
    gather = _WeightGather(specs)
    n = gather.n

    def body(*refs):
        outs, sems = refs[n:2 * n], refs[2 * n:]
        gather.start(outs, *sems)
        gather.finish(outs, *sems)

    return pl.pallas_call(
        body, name=name, out_shape=gather.out_shape, in_specs=gather.specs_any, out_specs=gather.specs_any,
        input_output_aliases={w: w for w in range(n)}, scratch_shapes=gather.scratch,
    )(*placed)


class _GradExchange:
    def __init__(self, specs):
        self.specs = specs
        self.n = len(specs)
        self.specs_any = [pl.BlockSpec(memory_space=pl.ANY)] * self.n
        self.out_shape = [jax.ShapeDtypeStruct((NDEV,) + sp.piece_shape, BF16) for sp in specs]
        self.scratch = [pltpu.SemaphoreType.DMA((7 * self.n,)), pltpu.SemaphoreType.DMA((NDEV * self.n,))]

    def _sends(self, grads, recvs, send_sems, recv_sems):
        x, y, c, k_me, chips, kidx = _mesh_place()
        dev = 2 * k_me + c
        sends = []
        for w, sp in enumerate(self.specs):
            slot, arrival = recvs[w].at[dev], recv_sems.at[NDEV * w + dev]
            sends.append(_remote(sp.piece(grads[w], k_me, 1 - c), slot, send_sems.at[7 * w], arrival, (x, y, 1 - c)))
            for j, chip in enumerate(chips):
                for h in range(2):
                    sends.append(_remote(sp.piece(grads[w], kidx[j], h), slot, send_sems.at[7 * w + 1 + 2 * j + h],
                                         arrival, (*chip, h)))
        return sends

    def start(self, grads, recvs, send_sems, recv_sems):
        for cp in self._sends(grads, recvs, send_sems, recv_sems):
            cp.start()

    def finish(self, grads, recvs, send_sems, recv_sems):
        x, y, c, k_me, _, _ = _mesh_place()
        dev = 2 * k_me + c
        for w in range(self.n):
            for d in range(NDEV):
                landed = recvs[w].at[d]
                arrival = _remote(landed, landed, send_sems.at[7 * w], recv_sems.at[NDEV * w + d], (x, y, c))
                pl.when(d != dev)(arrival.wait_recv)
        for cp in self._sends(grads, recvs, send_sems, recv_sems):
            cp.wait_send()


def _device_sum(sp, local, recv, place):
    nd = len(sp.piece_shape)

    def body(p_ref, a_ref, b_ref, o_ref):
        d = pl.program_id(0)
        term = jnp.where(d == p_ref[2], a_ref[...], b_ref[...]).astype(F32)

        @pl.when(d == 0)
        def _():
            o_ref[...] = term

        @pl.when(d > 0)
        def _():
            o_ref[...] += term

    def mine(d, p_ref):
        return tuple(2 * p_ref[0] + p_ref[1] if a == sp.shard_axis == sp.half_axis else
                     p_ref[0] if a == sp.shard_axis else p_ref[1] if a == sp.half_axis else 0 for a in range(nd))

    def others(d, p_ref):
        return (jnp.where(d == p_ref[2], (d + 1) % NDEV, d),) + (0,) * nd

    return pl.pallas_call(
        body, name="rs_device_sum_" + sp.name,
        grid_spec=pltpu.PrefetchScalarGridSpec(
            num_scalar_prefetch=1, grid=(NDEV,),
            in_specs=[pl.BlockSpec(sp.piece_shape, mine), pl.BlockSpec((None,) + sp.piece_shape, others)],
            out_specs=pl.BlockSpec(sp.piece_shape,
                                   lambda d, p_ref: tuple(p_ref[1] if a == sp.half_axis else 0 for a in range(nd)))),
        out_shape=jax.ShapeDtypeStruct(sp.shard_shape, F32),
        compiler_params=_params(("arbitrary",)),
    )(place, local, recv)


def _pair_exchange(grads, specs):
    n = len(specs)

    def body(*refs):
        ins, outs = refs[:n], refs[n:2 * n]
        send_sems, recv_sems = refs[2 * n:]
        x, y, c, _, _, _ = _mesh_place()
        sibling = (x, y, 1 - c)
        sent = []
        for w, sp in enumerate(specs):
            for k in range(NCHIP):
                cp = _remote(sp.piece(ins[w], k, 1 - c), outs[w].at[k],
                             send_sems.at[NCHIP * w + k], recv_sems.at[NCHIP * w + k], sibling)
                cp.start()
                sent.append(cp)
        for cp in sent:
            cp.wait_recv()
        for cp in sent:
            cp.wait_send()

    hbm = pl.BlockSpec(memory_space=pl.ANY)
    return pl.pallas_call(
        body, name="rs_pair_exchange",
        out_shape=[jax.ShapeDtypeStruct((NCHIP,) + sp.piece_shape, F32) for sp in specs],
        in_specs=[hbm] * n, out_specs=[hbm] * n,
        scratch_shapes=[pltpu.SemaphoreType.DMA((NCHIP * n,)), pltpu.SemaphoreType.DMA((NCHIP * n,))],
    )(*grads)


def _pair_sum(sp, grad, recv, core):
    nd = len(sp.piece_shape)

    def body(c_ref, g_ref, r_ref, o_ref):
        o_ref[...] = (g_ref[...] + r_ref[...]).astype(BF16)

    slot = pl.BlockSpec((None,) + sp.piece_shape, lambda k, c_ref: (k,) + (0,) * nd)
    return pl.pallas_call(
        body, name="rs_pair_sum_" + sp.name,
        grid_spec=pltpu.PrefetchScalarGridSpec(
            num_scalar_prefetch=1, grid=(NCHIP,),
            in_specs=[sp.piece_block(), slot], out_specs=slot),
        out_shape=jax.ShapeDtypeStruct((NCHIP,) + sp.piece_shape, BF16),
        compiler_params=_params(("parallel",)),
    )(core, grad, recv)


def _chip_exchange(parts, specs):
    n = len(specs)

    def body(*refs):
        ins, outs = refs[:n], refs[n:2 * n]
        send_sems, recv_sems = refs[2 * n:]
        x, y, c, k_me, chips, kidx = _mesh_place()
        sent = []
        for j, chip in enumerate(chips):
            for w in range(n):
                cp = _remote(ins[w].at[kidx[j]], outs[w].at[k_me], send_sems.at[3 * w + j], recv_sems.at[3 * w + j],
                             (*chip, c))
                cp.start()
                sent.append(cp)
        for j, chip in enumerate(chips):
            for w in range(n):
                landed = outs[w].at[kidx[j]]
                _remote(landed, landed, send_sems.at[3 * w + j], recv_sems.at[3 * w + j], (*chip, c)).wait_recv()
        for cp in sent:
            cp.wait_send()

    hbm = pl.BlockSpec(memory_space=pl.ANY)
    return pl.pallas_call(
        body, name="rs_chip_exchange",
        out_shape=[jax.ShapeDtypeStruct((NCHIP,) + sp.piece_shape, BF16) for sp in specs],
        in_specs=[hbm] * n, out_specs=[hbm] * n,
        scratch_shapes=[pltpu.SemaphoreType.DMA((3 * n,)), pltpu.SemaphoreType.DMA((3 * n,))],
    )(*parts)


def _chip_sum(sp, parts, recv, place):
    nd = len(sp.piece_shape)

    def body(p_ref, a_ref, b_ref, o_ref):
        k = pl.program_id(0)
        term = jnp.where(k == p_ref[0], a_ref[...], b_ref[...]).astype(F32)

        @pl.when(k == 0)
        def _():
            o_ref[...] = term

        @pl.when(k > 0)
        def _():
            o_ref[...] += term

    def others(k, p_ref):
        return (jnp.where(k == p_ref[0], (k + 1) % NCHIP, k),) + (0,) * nd

    return pl.pallas_call(
        body, name="rs_chip_sum_" + sp.name,
        grid_spec=pltpu.PrefetchScalarGridSpec(
            num_scalar_prefetch=1, grid=(NCHIP,),
            in_specs=[pl.BlockSpec((None,) + sp.piece_shape, lambda k, p_ref: (p_ref[0],) + (0,) * nd),
                      pl.BlockSpec((None,) + sp.piece_shape, others)],
            out_specs=pl.BlockSpec(sp.piece_shape,
                                   lambda k, p_ref: tuple(p_ref[1] if a == sp.half_axis else 0 for a in range(nd)))),
        out_shape=jax.ShapeDtypeStruct(sp.shard_shape, F32),
        compiler_params=_params(("arbitrary",)),
    )(place, parts, recv)


def _pair_share(halves):
    def body(*refs):
        outs = refs[NW:2 * NW]
        send_sems, recv_sems = refs[2 * NW:]
        x, y, c, _, _, _ = _mesh_place()
        sibling = (x, y, 1 - c)
        sent = []
        for w, sp in enumerate(SHARDED):
            mine = sp.half(outs[w], c)
            cp = _remote(mine, mine, send_sems.at[w], recv_sems.at[w], sibling)
            cp.start()
            sent.append(cp)
        for w, sp in enumerate(SHARDED):
            landed = sp.half(outs[w], 1 - c)
            _remote(landed, landed, send_sems.at[w], recv_sems.at[w], sibling).wait_recv()
        for cp in sent:
            cp.wait_send()

    hbm = pl.BlockSpec(memory_space=pl.ANY)
    return pl.pallas_call(
        body, name="rs_pair_share",
        out_shape=[jax.ShapeDtypeStruct(sp.shard_shape, F32) for sp in SHARDED],
        in_specs=[hbm] * NW, out_specs=[hbm] * NW,
        input_output_aliases={w: w for w in range(NW)},
        scratch_shapes=[pltpu.SemaphoreType.DMA((NW,)), pltpu.SemaphoreType.DMA((NW,))],
    )(*halves)


def _reduce_scatter(local, received, place):
    return _pair_share([_device_sum(sp, local[sp.name], received[sp.name], place) for sp in SHARDED])


def _place_bf16(sp, w, place):
    nd = len(sp.full_shape)

    def body(p_ref, w_ref, o_ref):
        o_ref[...] = w_ref[...].astype(BF16)

    return pl.pallas_call(
        body, name="place_" + sp.name,
        grid_spec=pltpu.PrefetchScalarGridSpec(
            num_scalar_prefetch=1, grid=(1,),
            in_specs=[pl.BlockSpec(sp.shard_shape, lambda i, p_ref: (0,) * nd)],
            out_specs=pl.BlockSpec(sp.shard_shape,
                                   lambda i, p_ref: tuple(p_ref[0] if a == sp.shard_axis else 0 for a in range(nd)))),
        out_shape=jax.ShapeDtypeStruct(sp.full_shape, BF16),
        compiler_params=_params(("arbitrary",)),
    )(place, w)


def _matmul_f32(a, b, name):
    def body(a_ref, b_ref, o_ref):
        o_ref[...] = jnp.dot(a_ref[...], b_ref[...], preferred_element_type=F32, precision=lax.Precision.HIGHEST)

    return pl.pallas_call(body, name=name, out_shape=jax.ShapeDtypeStruct((a.shape[0], b.shape[1]), F32),
                          compiler_params=pltpu.CompilerParams(vmem_limit_bytes=VMEM_LIMIT))(a, b)


def _sum_devices(stacked):
    def body(x_ref, o_ref):
        acc = x_ref[0]
        for d in range(1, NDEV):
            acc = acc + x_ref[d]
        o_ref[...] = acc

    return pl.pallas_call(body, name="sum_devices", out_shape=jax.ShapeDtypeStruct(stacked.shape[1:], F32),
                          compiler_params=pltpu.CompilerParams(vmem_limit_bytes=VMEM_LIMIT))(stacked)


def _adamw(w, g, m, v, name):
    r, cdim = w.shape
    tr = r if r <= 256 else (256 if r % 256 == 0 else r // 2)

    def body(w_ref, g_ref, m_ref, v_ref, d_ref, nm_ref, nv_ref):
        gv = g_ref[...]
        nm = ADAM_B1 * m_ref[...] + (1.0 - ADAM_B1) * gv
        nv = ADAM_B2 * v_ref[...] + (1.0 - ADAM_B2) * (gv * gv)
        m_hat = nm / (1.0 - ADAM_B1 ** ADAM_STEP)
        v_hat = nv / (1.0 - ADAM_B2 ** ADAM_STEP)
        d_ref[...] = -ADAM_LR * (m_hat / (jnp.sqrt(v_hat) + ADAM_EPS) + ADAM_WD * w_ref[...])
        nm_ref[...] = nm
        nv_ref[...] = nv

    blk = pl.BlockSpec((tr, cdim), lambda i: (i, 0))
    return pl.pallas_call(
        body, name="adamw_" + name, grid=(r // tr,), in_specs=[blk] * 4, out_specs=[blk] * 3,
        out_shape=[jax.ShapeDtypeStruct(w.shape, F32)] * 3,
        compiler_params=_params(("parallel",)),
    )(w, g, m, v)


WEIGHT_NAMES = ("g_pre_mix", "g_post_mix", "g_pre_ffn", "g_post_ffn", "w_ada", "b_ada", "w_in", "w_pool",
                "pool_scale", "conv_w", "conv_b", "w_bout", "w_o", "w_up", "ffn_conv_w", "ffn_conv_b", "w_down")
MATRIX_NAMES = ("w_ada",) + tuple(sp.name for sp in SHARDED)
VECTOR_NAMES = tuple(n for n in WEIGHT_NAMES if n not in MATRIX_NAMES)

CW = D // NCHIP
FCW = F2 // NCHIP
ADA_W = DIN // NCHIP
COND_BLOCK = (8, 768)
GRAD_BLOCK = (8, 4864)


def _flat_pad(parts, shape):
    flat = jnp.concatenate([p.reshape(-1) for p in parts])
    return jnp.pad(flat, (0, shape[0] * shape[1] - flat.shape[0])).reshape(shape)


def _take(flat, offset, shape):
    size = 1
    for n in shape:
        size *= n
    return flat[offset:offset + size].reshape(shape), offset + size


def kernel(x, c, g_pre_mix, g_post_mix, g_pre_ffn, g_post_ffn, w_ada, b_ada, w_in, w_pool, pool_scale, conv_w, conv_b, w_bout, w_o, w_up, ffn_conv_w, ffn_conv_b, w_down, loss_target, m_g_pre_mix, m_g_post_mix, m_g_pre_ffn, m_g_post_ffn, m_w_ada, m_b_ada, m_w_in, m_w_pool, m_pool_scale, m_conv_w, m_conv_b, m_w_bout, m_w_o, m_w_up, m_ffn_conv_w, m_ffn_conv_b, m_w_down, v_g_pre_mix, v_g_post_mix, v_g_pre_ffn, v_g_post_ffn, v_w_ada, v_b_ada, v_w_in, v_w_pool, v_pool_scale, v_conv_w, v_conv_b, v_w_bout, v_w_o, v_w_up, v_ffn_conv_w, v_ffn_conv_b, v_w_down):
    weights = dict(g_pre_mix=g_pre_mix, g_post_mix=g_post_mix, g_pre_ffn=g_pre_ffn, g_post_ffn=g_post_ffn,
                   w_ada=w_ada, b_ada=b_ada, w_in=w_in, w_pool=w_pool, pool_scale=pool_scale, conv_w=conv_w,
                   conv_b=conv_b, w_bout=w_bout, w_o=w_o, w_up=w_up, ffn_conv_w=ffn_conv_w, ffn_conv_b=ffn_conv_b,
                   w_down=w_down)
    mom1 = dict(g_pre_mix=m_g_pre_mix, g_post_mix=m_g_post_mix, g_pre_ffn=m_g_pre_ffn, g_post_ffn=m_g_post_ffn,
                w_ada=m_w_ada, b_ada=m_b_ada, w_in=m_w_in, w_pool=m_w_pool, pool_scale=m_pool_scale,
                conv_w=m_conv_w, conv_b=m_conv_b, w_bout=m_w_bout, w_o=m_w_o, w_up=m_w_up,
                ffn_conv_w=m_ffn_conv_w, ffn_conv_b=m_ffn_conv_b, w_down=m_w_down)
    mom2 = dict(g_pre_mix=v_g_pre_mix, g_post_mix=v_g_post_mix, g_pre_ffn=v_g_pre_ffn, g_post_ffn=v_g_post_ffn,
                w_ada=v_w_ada, b_ada=v_b_ada, w_in=v_w_in, w_pool=v_w_pool, pool_scale=v_pool_scale,
                conv_w=v_conv_w, conv_b=v_conv_b, w_bout=v_w_bout, w_o=v_w_o, w_up=v_w_up,
                ffn_conv_w=v_ffn_conv_w, ffn_conv_b=v_ffn_conv_b, w_down=v_w_down)

    chip = 2 * lax.axis_index("x") + lax.axis_index("y")
    core = lax.axis_index("c")
    dev = 2 * chip + core
    place = jnp.stack([chip, core, dev]).astype(jnp.int32)

    cond = _all_gather_small(_flat_pad([c, conv_w, ffn_conv_w], COND_BLOCK), "gather_cond")
    cond = cond.reshape(NDEV, -1)
    c_all = cond[:, :D]
    by_chip = cond[0::2]
    conv_w_full = by_chip[:, D:D + 3 * CW].reshape(NCHIP, 3, CW).transpose(1, 0, 2).reshape(3, D)
    ffn_w_full = by_chip[:, D + 3 * CW:D + 3 * CW + 3 * FCW].reshape(NCHIP, 3, FCW).transpose(1, 0, 2).reshape(3, F2)

    mod_cols = _all_gather_small(_matmul_f32(c_all, w_ada[0], "ada_mod"), "gather_mod")
    mod_cols = mod_cols.reshape(NDEV, NDEV, ADA_W)[0::2]
    mod = lax.dynamic_index_in_dim(mod_cols, dev, axis=1, keepdims=False).reshape(6, D) + b_ada.reshape(6, D)
    vec_d = jnp.concatenate([mod, g_pre_mix, g_post_mix, g_pre_ffn, g_post_ffn, pool_scale, conv_b, conv_w_full,
                             jnp.zeros((VD_ROWS - 15, D), F32)], axis=0)
    vec_f = jnp.concatenate([ffn_w_full, ffn_conv_b, jnp.zeros((FV_ROWS - 4, F2), F32)], axis=0)

    placed = [_place_bf16(sp, weights[sp.name][0], place) for sp in SHARDED]
    loss_blk, dx, vecs, local, received = _local_step(x[0], loss_target[0], vec_d, vec_f, placed, place)

    dmod = [vecs[n] for n in ("dsh1", "dsc1", "dgt1", "dsh2", "dsc2", "dgt2")]
    small = [vecs["dg_pre_mix"], vecs["dg_post_mix"], vecs["dg_pre_ffn"], vecs["dg_post_ffn"]] + dmod + [
        vecs["dpool_scale"], vecs["dconv_w"], vecs["dconv_b"], vecs["dffn_conv_w"], vecs["dffn_conv_b"],
        loss_blk[0]]
    gathered = _all_gather_small(_flat_pad(small, GRAD_BLOCK), "gather_vector_grads")
    total = _sum_devices(gathered.reshape((NDEV,) + GRAD_BLOCK)).reshape(-1)
    vgrad = {}
    off = 0
    for n in ("g_pre_mix", "g_post_mix", "g_pre_ffn", "g_post_ffn"):
        vgrad[n], off = _take(total, off, (1, D))
    dmod_off = off
    vgrad["b_ada"], off = _take(total, off, (1, DIN))
    vgrad["pool_scale"], off = _take(total, off, (1, D))
    g_conv_w, off = _take(total, off, (3, D))
    vgrad["conv_w"] = lax.dynamic_slice_in_dim(g_conv_w, chip * CW, CW, axis=1)[None]
    vgrad["conv_b"], off = _take(total, off, (1, D))
    g_ffn_w, off = _take(total, off, (3, F2))
    vgrad["ffn_conv_w"] = lax.dynamic_slice_in_dim(g_ffn_w, chip * FCW, FCW, axis=1)[None]
    vgrad["ffn_conv_b"], off = _take(total, off, (1, F2))
    loss = total[off]

    dmod_all = gathered.reshape(NDEV, -1)[:, dmod_off:dmod_off + DIN]
    dmod_cols = lax.dynamic_slice_in_dim(dmod_all, chip * ADA_W, ADA_W, axis=1)
    g_ada = _matmul_f32(jnp.pad(c_all.T, ((0, 0), (0, 128 - NDEV))), jnp.pad(dmod_cols, ((0, 128 - NDEV), (0, 0))),
                        "ada_wgrad")

    reduced = _reduce_scatter(local, received, place)
    mgrad = {"w_ada": g_ada}
    for sp, g in zip(SHARDED, reduced):
        mgrad[sp.name] = g

    grad, delta, new_m, new_v = {}, {}, {}, {}
    for n in MATRIX_NAMES:
        shape = weights[n].shape
        two_d = (-1, shape[-1])
        d, nm, nv = _adamw(weights[n].reshape(two_d), mgrad[n].reshape(two_d), mom1[n].reshape(two_d),
                           mom2[n].reshape(two_d), n)
        grad[n], delta[n], new_m[n], new_v[n] = (a.reshape(shape) for a in (mgrad[n], d, nm, nv))
    flat = lambda tree: jnp.concatenate([tree[n].reshape(1, -1) for n in VECTOR_NAMES], axis=1)
    d, nm, nv = _adamw(flat(weights), flat(vgrad), flat(mom1), flat(mom2), "vectors")
    off = 0
    for n in VECTOR_NAMES:
        shape = weights[n].shape
        grad[n] = vgrad[n].reshape(shape)
        delta[n], _ = _take(d[0], off, shape)
        new_m[n], _ = _take(nm[0], off, shape)
        new_v[n], off = _take(nv[0], off, shape)

    return (loss, dx[None], *[grad[n] for n in WEIGHT_NAMES], *[delta[n] for n in WEIGHT_NAMES],
            *[new_m[n] for n in WEIGHT_NAMES], *[new_v[n] for n in WEIGHT_NAMES])
```

```python
import jax
import jax.numpy as jnp
from jax import lax
from jax.experimental import pallas as pl
from jax.experimental.pallas import tpu as pltpu

F32 = jnp.float32
BF16 = jnp.bfloat16

D = 1024
DIN = 6 * D
F = 2816
F2 = 2 * F
NG = 4
GW = D // NG
POOL_CARRY = 16
CONV_CARRY = 3
EPS = 1e-6
NCHIP = 4
NDEV = 8

ADAM_LR = 0.001
ADAM_B1 = 0.9
ADAM_B2 = 0.999
ADAM_EPS = 1e-08
ADAM_WD = 0.01
ADAM_STEP = 10

VMEM_LIMIT = 60 * 1024 * 1024

(V_SH1, V_SC1, V_GT1, V_SH2, V_SC2, V_GT2, V_GPRE1, V_GPOST1, V_GPRE2, V_GPOST2,
 V_PSCALE, V_CB, V_CW0, V_CW1, V_CW2) = range(15)
VD_ROWS = 16
FV_W0, FV_W1, FV_W2, FV_B = range(4)
FV_ROWS = 8

MESH = pl.DeviceIdType.MESH


def _params(sem=None, vmem=VMEM_LIMIT):
    return pltpu.CompilerParams(dimension_semantics=sem, vmem_limit_bytes=vmem)


def _row(ref, r):
    return ref[r:r + 1, :]


def _load_once(pairs, sem):
    @pl.when(pl.program_id(0) == 0)
    def _():
        copies = [pltpu.make_async_copy(src, dst, sem.at[n]) for n, (src, dst) in enumerate(pairs)]
        for cp in copies:
            cp.start()
        for cp in copies:
            cp.wait()


def _dot(a, b):
    return jnp.dot(a, b, preferred_element_type=F32)


def _dot_nt(a, b):
    return lax.dot_general(a, b, (((1,), (1,)), ((), ())), preferred_element_type=F32)


BLK = 256
SEG = BLK // 8


def _load_rows(ref, ts):
    blocks = [jnp.swapaxes(ref[b * BLK:(b + 1) * BLK, :].reshape(8, SEG, ref.shape[-1]), 0, 1).reshape(BLK, -1)
              for b in range(ts // BLK)]
    return jnp.concatenate(blocks, axis=0)


def _store_rows(ref, val, ts):
    for b in range(ts // BLK):
        blk = val[b * BLK:(b + 1) * BLK, :].reshape(SEG, 8, val.shape[-1])
        ref[b * BLK:(b + 1) * BLK, :] = jnp.swapaxes(blk, 0, 1).reshape(BLK, -1)


def _times(t0):
    p = lax.broadcasted_iota(jnp.int32, (BLK, 1), 0)
    return t0 + (p & 7) * SEG + (p >> 3)


def _before(x, carry, s):
    x3 = x.reshape(SEG, 8, x.shape[-1])
    tail = pltpu.roll(x3[SEG - s:], 1, 1)
    row = lax.broadcasted_iota(jnp.int32, tail.shape, 1)
    out = jnp.concatenate([jnp.where(row == 0, carry, tail), x3[:SEG - s]], axis=0)
    return out.reshape(x.shape), tail


def _after(x, carry, s):
    x3 = x.reshape(SEG, 8, x.shape[-1])
    head = pltpu.roll(x3[:s], 7, 1)
    row = lax.broadcasted_iota(jnp.int32, head.shape, 1)
    out = jnp.concatenate([x3[s:], jnp.where(row == 7, carry, head)], axis=0)
    return out.reshape(x.shape), head


def _causal_conv(x, carry, cols, w0, w1, w2, b):
    x1, carry[0:1, :, cols] = _before(x, carry[0:1, :, cols], 1)
    x2, carry[1:3, :, cols] = _before(x, carry[1:3, :, cols], 2)
    return b + w2 * x + w1 * x1 + w0 * x2


def _causal_conv_bwd(dy, carry, cols, w0, w1, w2):
    d1, carry[0:1, :, cols] = _after(dy, carry[0:1, :, cols], 1)
    d2, carry[1:3, :, cols] = _after(dy, carry[1:3, :, cols], 2)
    return w2 * dy + w1 * d1 + w0 * d2, d1, d2


def _pool_counts(t0, g):
    return jnp.minimum((_times(t0) + 1).astype(F32), float(2 << g))


def _rms(x):
    return lax.rsqrt(jnp.mean(x * x, axis=-1, keepdims=True) + EPS)


def _rms_bwd(dn, n, r):
    return r * (dn - n * jnp.mean(dn * n, axis=-1, keepdims=True))


def _colsum(x):
    return jnp.sum(x, axis=0, keepdims=True)


def _gelu_and_grad(x):
    k = 0.7978845608028654
    inner = k * (x + 0.044715 * (x * x * x))
    th = jnp.tanh(inner)
    gelu = 0.5 * x * (1.0 + th)
    dgelu = 0.5 * (1.0 + th) + 0.5 * x * (1.0 - th * th) * (k * (1.0 + 3.0 * 0.044715 * (x * x)))
    return gelu, dgelu


def _fwd_proj(x, vec_d, placed_in, placed_rest, place, ts):
    s = x.shape[0]
    nt = s // ts
    cw = DIN // NCHIP
    sp_in = SHARDED[0]
    gather = _WeightGather(SHARDED[1:4])
    n = gather.n

    def body(*refs):
        p_ref, x_ref, v_ref = refs[:3]
        proj_ref, h1_ref, xs_ref, w_full = refs[4 + n:8 + n]
        rest = refs[8 + n:8 + 2 * n]
        w_vmem, h1_all, sem, in_send, in_recv, send_sems, recv_sems = refs[8 + 2 * n:]
        j, i = pl.program_id(0), pl.program_id(1)
        x_, y_, c, k_me, _, _ = _mesh_place()
        sibling = (x_, y_, 1 - c)

        def peer(t):
            return (x_ ^ (t >> 1), y_ ^ (t & 1))

        def w_in_sends():
            mine = sp_in.piece(w_full, k_me, c)
            return [_remote(mine, mine, in_send.at[t - 1], in_recv.at[t - 1], (*peer(t), c)) for t in (1, 2, 3)]

        def load_block(k):
            cp = pltpu.make_async_copy(sp_in.shard(w_full, k), w_vmem.at[k], sem.at[0])
            cp.start()
            cp.wait()

        @pl.when((j == 0) & (i == 0))
        def _():
            for cp in w_in_sends()[:2]:
                cp.start()
            load_block(k_me)

        @pl.when((j == 1) & (i == 0))
        def _():
            for cp in w_in_sends()[:2]:
                cp.wait_send()
            w_in_sends()[2].start()
            gather.start(rest, send_sems, recv_sems)

        for t in (1, 2, 3):
            @pl.when((j == t) & (i == 0))
            def _(t=t):
                k = k_me ^ t
                landed = sp_in.piece(w_full, k, c)
                _remote(landed, landed, in_send.at[t - 1], in_recv.at[t - 1], (*peer(t), c)).wait_recv()
                _remote(landed, landed, in_send.at[2 + t], in_recv.at[2 + t], sibling).start()
                other = sp_in.piece(w_full, k, 1 - c)
                _remote(other, other, in_send.at[2 + t], in_recv.at[2 + t], sibling).wait_recv()
                load_block(k)

        @pl.when(j == 0)
        def _():
            xv = _load_rows(x_ref, ts)
            xs_ref[...] = xv
            n1 = xv * _rms(xv)
            h = n1 * (_row(v_ref, V_GPRE1) * (1.0 + _row(v_ref, V_SC1))) + _row(v_ref, V_SH1)
            hb = h.astype(BF16)
            h1_ref[...] = hb
            h1_all[i] = hb

        proj_ref[...] = _dot(h1_all[i], w_vmem[k_me ^ j]).astype(BF16)

        @pl.when((j == NCHIP - 1) & (i == nt - 1))
        def _():
            w_in_sends()[2].wait_send()
            for t in (1, 2, 3):
                landed = sp_in.piece(w_full, k_me ^ t, c)
                _remote(landed, landed, in_send.at[2 + t], in_recv.at[2 + t], sibling).wait_send()
            gather.finish(rest, send_sems, recv_sems)

    once = lambda w: pl.BlockSpec((ts, w), lambda j, i, p: (jnp.where(j == 0, i, nt - 1), 0))
    return pl.pallas_call(
        body, name="fwd_proj",
        grid_spec=pltpu.PrefetchScalarGridSpec(
            num_scalar_prefetch=1, grid=(NCHIP, nt),
            in_specs=[once(D), pl.BlockSpec((VD_ROWS, D), lambda j, i, p: (0, 0)),
                      pl.BlockSpec(memory_space=pl.ANY)] + gather.specs_any,
            out_specs=[pl.BlockSpec((ts, cw), lambda j, i, p: (i, p[0] ^ j)), once(D), once(D),
                       pl.BlockSpec(memory_space=pl.ANY)] + gather.specs_any,
            scratch_shapes=[pltpu.VMEM((NCHIP, D, cw), BF16), pltpu.VMEM((nt, ts, D), BF16),
                            pltpu.SemaphoreType.DMA((1,)),
                            pltpu.SemaphoreType.DMA((6,)), pltpu.SemaphoreType.DMA((6,))] + gather.scratch),
        out_shape=[jax.ShapeDtypeStruct((s, DIN), BF16), jax.ShapeDtypeStruct((s, D), BF16),
                   jax.ShapeDtypeStruct((s, D), F32), jax.ShapeDtypeStruct(sp_in.full_shape, BF16)] + gather.out_shape,
        input_output_aliases={3 + w: 3 + w for w in range(n + 1)},
        compiler_params=_params(("arbitrary", "arbitrary")),
    )(place, x, vec_d, placed_in, *placed_rest)


def _fwd_mix(proj, x, vec_d, w_pool, w_bout, w_o, placed_ffn, ts):
    s = x.shape[0]
    gather = _WeightGather(SHARDED[4:])
    n = gather.n

    def body(*refs):
        ins, outs, rest = refs[:6], refs[6 + n:14 + n], refs[14 + n:14 + 2 * n]
        scratch, sems = refs[14 + 2 * n:-2], refs[-2:]
        i = pl.program_id(0)
        nt = s // ts
        pl.when(i == 0)(lambda: gather.start(rest, *sems))
        pl.when(i == nt - 1 - nt // 8)(lambda: gather.forward(rest, *sems))
        compute(*ins, *outs, *scratch)
        pl.when(i == nt - 1)(lambda: gather.drain(rest, *sems))

    def compute(p_ref, x_ref, v_ref, wp_hbm, wb_hbm, wo_hbm,
                x1_ref, o_ref, pg_ref, q_ref, mg_ref, ya_ref, yb_ref, cv_ref,
                wp, wb, wo, carry_p, carry_v, sem):
        i = pl.program_id(0)
        _load_once([(wp_hbm, wp), (wb_hbm, wb), (wo_hbm, wo)], sem)

        @pl.when(i == 0)
        def _():
            carry_p[...] = jnp.zeros_like(carry_p)
            carry_v[...] = jnp.zeros_like(carry_v)

        t0 = i * ts
        for g in range(NG):
            cols = slice(g * GW, (g + 1) * GW)
            u = p_ref[:, cols].astype(F32)
            e = u
            for l in range(g + 1):
                slot = slice((1 << l) - 1, (2 << l) - 1)
                shifted, carry_p[slot, :, cols] = _before(e, carry_p[slot, :, cols], 1 << l)
                e = e + shifted
            pgb = (e / _pool_counts(t0, g) - u).astype(BF16)
            pg_ref[:, cols] = pgb
            ya_ref[:, cols] = _dot(pgb, wp[g]).astype(BF16)

        u_x = p_ref[:, D:2 * D].astype(F32)
        u_c = p_ref[:, 3 * D:4 * D].astype(F32)
        v = u_c * u_x
        cv = _causal_conv(v, carry_v, slice(None), _row(v_ref, V_CW0), _row(v_ref, V_CW1),
                          _row(v_ref, V_CW2), _row(v_ref, V_CB))
        cv_ref[...] = cv.astype(BF16)
        q = (p_ref[:, 2 * D:3 * D].astype(F32) * cv).astype(BF16)
        q_ref[...] = q
        y_b = _dot(q, wb[...])
        yb_ref[...] = y_b.astype(BF16)

        y_a = ya_ref[...].astype(F32) * _row(v_ref, V_PSCALE)
        merged = (jax.nn.sigmoid(p_ref[:, 4 * D:5 * D].astype(F32)) * y_a
                  + jax.nn.sigmoid(p_ref[:, 5 * D:6 * D].astype(F32)) * y_b).astype(BF16)
        mg_ref[...] = merged
        o = _dot(merged, wo[...])
        o_ref[...] = o
        x1_ref[...] = x_ref[...] + _row(v_ref, V_GT1) * ((o * _rms(o)) * _row(v_ref, V_GPOST1))

    tile = lambda w: pl.BlockSpec((ts, w), lambda i: (i, 0))
    hbm = pl.BlockSpec(memory_space=pl.ANY)
    return pl.pallas_call(
        body, name="fwd_mix", grid=(s // ts,),
        in_specs=[tile(DIN), tile(D), pl.BlockSpec((VD_ROWS, D), lambda i: (0, 0)), hbm, hbm, hbm] + gather.specs_any,
        out_specs=[tile(D)] * 8 + gather.specs_any,
        out_shape=[jax.ShapeDtypeStruct((s, D), F32), jax.ShapeDtypeStruct((s, D), F32)]
        + [jax.ShapeDtypeStruct((s, D), BF16)] * 6 + gather.out_shape,
        input_output_aliases={6 + w: 8 + w for w in range(n)},
        scratch_shapes=[pltpu.VMEM((NG, GW, GW), BF16), pltpu.VMEM((D, D), BF16), pltpu.VMEM((D, D), BF16),
                        pltpu.VMEM((POOL_CARRY, 8, D), F32), pltpu.VMEM((CONV_CARRY, 8, D), F32),
                        pltpu.SemaphoreType.DMA((3,))] + gather.scratch,
        compiler_params=_params(("arbitrary",)),
    )(proj, x, vec_d, w_pool, w_bout, w_o, *placed_ffn)


def _fwd_ffn(x1, tgt, vec_d, vec_f, w_up, w_down, ts):
    s = x1.shape[0]

    def body(x1_ref, t_ref, v_ref, f_ref, wu_hbm, wd_hbm,
             up_ref, upc_ref, a_ref, h2_ref, dx2_ref, dff_ref, vo_ref, loss_ref,
             wu, wd, carry, sem):
        i = pl.program_id(0)
        _load_once([(wu_hbm, wu), (wd_hbm, wd)], sem)

        @pl.when(i == 0)
        def _():
            carry[...] = jnp.zeros_like(carry)
            vo_ref[...] = jnp.zeros_like(vo_ref)
            loss_ref[...] = jnp.zeros_like(loss_ref)

        x1v = x1_ref[...]
        n3 = x1v * _rms(x1v)
        h2 = (n3 * (_row(v_ref, V_GPRE2) * (1.0 + _row(v_ref, V_SC2))) + _row(v_ref, V_SH2)).astype(BF16)
        h2_ref[...] = h2

        ff = jnp.zeros((ts, D), F32)
        for lo, hi in FFN_SLABS:
            up = []
            for cols in (slice(lo, hi), slice(F + lo, F + hi)):
                u0 = _dot(h2, wu[:, cols])
                up_ref[:, cols] = u0.astype(BF16)
                y = _causal_conv(u0, carry, cols, f_ref[FV_W0:FV_W0 + 1, cols], f_ref[FV_W1:FV_W1 + 1, cols],
                                 f_ref[FV_W2:FV_W2 + 1, cols], f_ref[FV_B:FV_B + 1, cols])
                upc_ref[:, cols] = y.astype(BF16)
                up.append(y)
            gelu, _ = _gelu_and_grad(up[0])
            a = (gelu * up[1]).astype(BF16)
            a_ref[:, lo:hi] = a
            ff = ff + _dot(a, wd[lo:hi, :])

        r4 = _rms(ff)
        n4 = ff * r4
        gt2 = _row(v_ref, V_GT2)
        gpost = _row(v_ref, V_GPOST2)
        y4 = n4 * gpost
        diff = (x1v + gt2 * y4) - _load_rows(t_ref, ts)
        loss_ref[...] += jnp.full(loss_ref.shape, 0.5 / D * jnp.sum(diff * diff), F32)
        dx2 = diff * (1.0 / D)
        dx2_ref[...] = dx2
        dy4 = dx2 * gt2
        vo_ref[0:1, :] += _colsum(dx2 * y4)
        vo_ref[1:2, :] += _colsum(dy4 * n4)
        dff_ref[...] = _rms_bwd(dy4 * gpost, n4, r4).astype(BF16)

    tile = lambda w: pl.BlockSpec((ts, w), lambda i: (i, 0))
    full = lambda r, w: pl.BlockSpec((r, w), lambda i: (0, 0))
    hbm = pl.BlockSpec(memory_space=pl.ANY)
    return pl.pallas_call(
        body, name="fwd_ffn", grid=(s // ts,),
        in_specs=[tile(D), tile(D), full(VD_ROWS, D), full(FV_ROWS, F2), hbm, hbm],
        out_specs=[tile(F2), tile(F2), tile(F), tile(D), tile(D), tile(D), full(8, D), full(8, 128)],
        out_shape=[jax.ShapeDtypeStruct((s, F2), BF16), jax.ShapeDtypeStruct((s, F2), BF16),
                   jax.ShapeDtypeStruct((s, F), BF16),
                   jax.ShapeDtypeStruct((s, D), BF16), jax.ShapeDtypeStruct((s, D), F32),
                   jax.ShapeDtypeStruct((s, D), BF16), jax.ShapeDtypeStruct((8, D), F32),
                   jax.ShapeDtypeStruct((8, 128), F32)],
        scratch_shapes=[pltpu.VMEM((D, F2), BF16), pltpu.VMEM((F, D), BF16), pltpu.VMEM((CONV_CARRY, 8, F2), F32),
                        pltpu.SemaphoreType.DMA((2,))],
        compiler_params=_params(("arbitrary",)),
    )(x1, tgt, vec_d, vec_f, w_up, w_down)


def _bwd_ffn(dff, dx2, x1, up0, upc, vec_d, vec_f, w_up, w_down, exchange, ex_grads, ts):
    s = x1.shape[0]
    nt = s // ts
    n = exchange.n

    def body(*refs):
        ins, grads = refs[:9], refs[9:9 + n]
        outs, recvs = refs[9 + n:13 + n], refs[13 + n:13 + 2 * n]
        scratch, sems = refs[13 + 2 * n:-2], refs[-2:]
        i = pl.program_id(0)
        pl.when(i == 0)(lambda: exchange.start(grads, recvs, *sems))
        compute(*ins, *outs, *scratch)
        pl.when(i == nt - 1)(lambda: exchange.finish(grads, recvs, *sems))

    def compute(dff_ref, dx2_ref, x1_ref, up_ref, upc_ref, v_ref, f_ref, wu_hbm, wd_hbm,
                dx1_ref, dup_ref, vo_ref, fo_ref, wu, wd, carry, sem):
        i = pl.program_id(0)
        _load_once([(wu_hbm, wu), (wd_hbm, wd)], sem)

        @pl.when(i == 0)
        def _():
            carry[...] = jnp.zeros_like(carry)
            vo_ref[...] = jnp.zeros_like(vo_ref)
            fo_ref[...] = jnp.zeros_like(fo_ref)

        dffb = dff_ref[...]

        dh2 = jnp.zeros((ts, D), F32)
        for lo, hi in FFN_SLABS:
            slabs = (slice(lo, hi), slice(F + lo, F + hi))
            gelu, dgelu = _gelu_and_grad(upc_ref[:, slabs[0]].astype(F32))
            da = _dot_nt(dffb, wd[lo:hi, :])
            dups = (da * upc_ref[:, slabs[1]].astype(F32) * dgelu, da * gelu)
            for cols, dup in zip(slabs, dups):
                du0, d1, d2 = _causal_conv_bwd(dup, carry, cols, f_ref[FV_W0:FV_W0 + 1, cols],
                                               f_ref[FV_W1:FV_W1 + 1, cols], f_ref[FV_W2:FV_W2 + 1, cols])
                u0 = up_ref[:, cols].astype(F32)
                fo_ref[FV_B:FV_B + 1, cols] += _colsum(dup)
                fo_ref[FV_W2:FV_W2 + 1, cols] += _colsum(dup * u0)
                fo_ref[FV_W1:FV_W1 + 1, cols] += _colsum(d1 * u0)
                fo_ref[FV_W0:FV_W0 + 1, cols] += _colsum(d2 * u0)
                du0 = du0.astype(BF16)
                dup_ref[:, cols] = du0
                dh2 = dh2 + _dot_nt(du0, wu[:, cols])

        x1v = x1_ref[...]
        r3 = _rms(x1v)
        n3 = x1v * r3
        gpre = _row(v_ref, V_GPRE2)
        sc = 1.0 + _row(v_ref, V_SC2)
        vo_ref[0:1, :] += _colsum(dh2)
        vo_ref[1:2, :] += _colsum(dh2 * n3 * gpre)
        vo_ref[2:3, :] += _colsum(dh2 * n3 * sc)
        dx1_ref[...] = dx2_ref[...] + _rms_bwd(dh2 * (gpre * sc), n3, r3)

    rev = lambda w: pl.BlockSpec((ts, w), lambda i: (nt - 1 - i, 0))
    full = lambda r, w: pl.BlockSpec((r, w), lambda i: (0, 0))
    hbm = pl.BlockSpec(memory_space=pl.ANY)
    return pl.pallas_call(
        body, name="bwd_ffn", grid=(nt,),
        in_specs=[rev(D), rev(D), rev(D), rev(F2), rev(F2), full(VD_ROWS, D), full(FV_ROWS, F2), hbm, hbm]
        + exchange.specs_any,
        out_specs=[rev(D), rev(F2), full(8, D), full(FV_ROWS, F2)] + exchange.specs_any,
        out_shape=[jax.ShapeDtypeStruct((s, D), F32), jax.ShapeDtypeStruct((s, F2), BF16),
                   jax.ShapeDtypeStruct((8, D), F32), jax.ShapeDtypeStruct((FV_ROWS, F2), F32)] + exchange.out_shape,
        scratch_shapes=[pltpu.VMEM((D, F2), BF16), pltpu.VMEM((F, D), BF16), pltpu.VMEM((CONV_CARRY, 8, F2), F32),
                        pltpu.SemaphoreType.DMA((2,))] + exchange.scratch,
        compiler_params=_params(("arbitrary",)),
    )(dff, dx2, x1, up0, upc, vec_d, vec_f, w_up, w_down, *ex_grads)


def _bwd_mix(dx1, o, proj, cv, ya0, yb, vec_d, w_pool, w_bout, w_o, exchange, ex_grads, ts):
    s = dx1.shape[0]
    nt = s // ts
    n = exchange.n

    def body(*refs):
        ins, grads = refs[:10], refs[10:10 + n]
        outs, recvs = refs[10 + n:15 + n], refs[15 + n:15 + 2 * n]
        scratch, sems = refs[15 + 2 * n:-2], refs[-2:]
        i = pl.program_id(0)
        pl.when(i == 0)(lambda: exchange.start(grads, recvs, *sems))
        compute(*ins, *outs, *scratch)
        pl.when(i == nt - 1)(lambda: exchange.finish(grads, recvs, *sems))

    def compute(dx1_ref, o_ref, p_ref, cv_ref, ya_ref, yb_ref, v_ref, wp_hbm, wb_hbm, wo_hbm,
                dp_ref, do_ref, dyb_ref, dya_ref, vo_ref, wp, wb, wo, carry_d, carry_e, sem):
        i = pl.program_id(0)
        _load_once([(wp_hbm, wp), (wb_hbm, wb), (wo_hbm, wo)], sem)

        @pl.when(i == 0)
        def _():
            carry_d[...] = jnp.zeros_like(carry_d)
            carry_e[...] = jnp.zeros_like(carry_e)
            vo_ref[...] = jnp.zeros_like(vo_ref)

        t0 = (nt - 1 - i) * ts
        dx1v = dx1_ref[...]
        ov = o_ref[...]
        r2 = _rms(ov)
        n2 = ov * r2
        gpost = _row(v_ref, V_GPOST1)
        vo_ref[0:1, :] += _colsum(dx1v * (n2 * gpost))
        dy2 = dx1v * _row(v_ref, V_GT1)
        vo_ref[1:2, :] += _colsum(dy2 * n2)
        dob = _rms_bwd(dy2 * gpost, n2, r2).astype(BF16)
        do_ref[...] = dob
        dmerged = _dot_nt(dob, wo[...])

        ya0 = ya_ref[...].astype(F32)
        pscale = _row(v_ref, V_PSCALE)
        sa = jax.nn.sigmoid(p_ref[:, 4 * D:5 * D].astype(F32))
        dp_ref[:, 4 * D:5 * D] = (dmerged * (ya0 * pscale) * sa * (1.0 - sa)).astype(BF16)
        dy_a = dmerged * sa
        vo_ref[2:3, :] += _colsum(dy_a * ya0)
        dya0 = (dy_a * pscale).astype(BF16)
        dya_ref[...] = dya0

        sb = jax.nn.sigmoid(p_ref[:, 5 * D:6 * D].astype(F32))
        dp_ref[:, 5 * D:6 * D] = (dmerged * yb_ref[...].astype(F32) * sb * (1.0 - sb)).astype(BF16)
        dy_b = (dmerged * sb).astype(BF16)
        dyb_ref[...] = dy_b
        dq = _dot_nt(dy_b, wb[...])

        u_x = p_ref[:, D:2 * D].astype(F32)
        u_b = p_ref[:, 2 * D:3 * D].astype(F32)
        u_c = p_ref[:, 3 * D:4 * D].astype(F32)
        w0, w1, w2 = _row(v_ref, V_CW0), _row(v_ref, V_CW1), _row(v_ref, V_CW2)
        dp_ref[:, 2 * D:3 * D] = (dq * cv_ref[...].astype(F32)).astype(BF16)
        dcv = dq * u_b
        dv, d1, d2 = _causal_conv_bwd(dcv, carry_d, slice(None), w0, w1, w2)
        v = u_c * u_x
        vo_ref[3:4, :] += _colsum(dcv)
        vo_ref[4:5, :] += _colsum(d2 * v)
        vo_ref[5:6, :] += _colsum(d1 * v)
        vo_ref[6:7, :] += _colsum(dcv * v)
        dp_ref[:, D:2 * D] = (dv * u_c).astype(BF16)
        dp_ref[:, 3 * D:4 * D] = (dv * u_x).astype(BF16)

        for g in range(NG):
            cols = slice(g * GW, (g + 1) * GW)
            dpg = _dot_nt(dya0[:, cols], wp[g])
            e = dpg / _pool_counts(t0, g)
            for l in range(g + 1):
                slot = slice((1 << l) - 1, (2 << l) - 1)
                shifted, carry_e[slot, :, cols] = _after(e, carry_e[slot, :, cols], 1 << l)
                e = e + shifted
            dp_ref[:, cols] = (e - dpg).astype(BF16)

    rev = lambda w: pl.BlockSpec((ts, w), lambda i: (nt - 1 - i, 0))
    hbm = pl.BlockSpec(memory_space=pl.ANY)
    return pl.pallas_call(
        body, name="bwd_mix", grid=(nt,),
        in_specs=[rev(D), rev(D), rev(DIN), rev(D), rev(D), rev(D), pl.BlockSpec((VD_ROWS, D), lambda i: (0, 0)),
                  hbm, hbm, hbm] + exchange.specs_any,
        out_specs=[rev(DIN), rev(D), rev(D), rev(D), pl.BlockSpec((8, D), lambda i: (0, 0))] + exchange.specs_any,
        out_shape=[jax.ShapeDtypeStruct((s, DIN), BF16)] + [jax.ShapeDtypeStruct((s, D), BF16)] * 3
        + [jax.ShapeDtypeStruct((8, D), F32)] + exchange.out_shape,
        scratch_shapes=[pltpu.VMEM((NG, GW, GW), BF16), pltpu.VMEM((D, D), BF16), pltpu.VMEM((D, D), BF16),
                        pltpu.VMEM((CONV_CARRY, 8, D), F32), pltpu.VMEM((POOL_CARRY, 8, D), F32),
                        pltpu.SemaphoreType.DMA((3,))] + exchange.scratch,
        compiler_params=_params(("arbitrary",)),
    )(dx1, o, proj, cv, ya0, yb, vec_d, w_pool, w_bout, w_o, *ex_grads)


def _bwd_in(dproj, dx1, x, vec_d, w_in, exchange, ex_grads, ts):
    s = x.shape[0]
    nt = s // ts
    n = exchange.n

    def body(*refs):
        ins, grads = refs[:5], refs[5:5 + n]
        outs, recvs = refs[5 + n:7 + n], refs[7 + n:7 + 2 * n]
        scratch, sems = refs[7 + 2 * n:-2], refs[-2:]
        i = pl.program_id(0)
        pl.when(i == 0)(lambda: exchange.start(grads, recvs, *sems))
        compute(*ins, *outs, *scratch)
        pl.when(i == nt - 1)(lambda: exchange.finish(grads, recvs, *sems))

    def compute(dp_ref, dx1_ref, x_ref, v_ref, w_hbm, dx_ref, vo_ref, w_vmem, sem):
        _load_once([(w_hbm, w_vmem)], sem)

        @pl.when(pl.program_id(0) == 0)
        def _():
            vo_ref[...] = jnp.zeros_like(vo_ref)

        dh1 = _dot_nt(dp_ref[...], w_vmem[...])
        xv = x_ref[...]
        r1 = _rms(xv)
        n1 = xv * r1
        gpre = _row(v_ref, V_GPRE1)
        sc = 1.0 + _row(v_ref, V_SC1)
        vo_ref[0:1, :] += _colsum(dh1)
        vo_ref[1:2, :] += _colsum(dh1 * n1 * gpre)
        vo_ref[2:3, :] += _colsum(dh1 * n1 * sc)
        _store_rows(dx_ref, dx1_ref[...] + _rms_bwd(dh1 * (gpre * sc), n1, r1), ts)

    tile = lambda w: pl.BlockSpec((ts, w), lambda i: (i, 0))
    return pl.pallas_call(
        body, name="bwd_in", grid=(s // ts,),
        in_specs=[tile(DIN), tile(D), tile(D), pl.BlockSpec((VD_ROWS, D), lambda i: (0, 0)),
                  pl.BlockSpec(memory_space=pl.ANY)] + exchange.specs_any,
        out_specs=[tile(D), pl.BlockSpec((8, D), lambda i: (0, 0))] + exchange.specs_any,
        out_shape=[jax.ShapeDtypeStruct((s, D), F32), jax.ShapeDtypeStruct((8, D), F32)] + exchange.out_shape,
        scratch_shapes=[pltpu.VMEM((D, DIN), BF16), pltpu.SemaphoreType.DMA((1,))] + exchange.scratch,
        compiler_params=_params(("arbitrary",)),
    )(dproj, dx1, x, vec_d, w_in, *ex_grads)


def _dot_tn(a, b):
    return lax.dot_general(a, b, (((0,), (0,)), ((), ())), preferred_element_type=F32)


def _wgrad(a, b, tm, tn, ts, name, dtype, exchange=None, ex_grads=()):
    s, m = a.shape
    nn = b.shape[1]
    grid = (m // tm, nn // tn, s // ts)
    n = exchange.n if exchange else 0

    def body(*refs):
        a_ref, b_ref = refs[:2]
        grads = refs[2:2 + n]
        o_ref = refs[2 + n]
        recvs = refs[3 + n:3 + 2 * n]
        acc = refs[3 + 2 * n]
        sems = refs[4 + 2 * n:]
        i, j, k = pl.program_id(0), pl.program_id(1), pl.program_id(2)
        if exchange:
            pl.when((i == 0) & (j == 0) & (k == 0))(lambda: exchange.start(grads, recvs, *sems))
        part = _dot_tn(a_ref[...], b_ref[...])

        @pl.when(k == 0)
        def _():
            acc[...] = part

        @pl.when(k > 0)
        def _():
            acc[...] += part

        @pl.when(k == grid[2] - 1)
        def _():
            o_ref[...] = acc[...].astype(dtype)

        if exchange:
            pl.when((i == grid[0] - 1) & (j == grid[1] - 1) & (k == grid[2] - 1))(
                lambda: exchange.finish(grads, recvs, *sems))

    hosted = exchange.specs_any if exchange else []
    return pl.pallas_call(
        body, name=name, grid=grid,
        in_specs=[pl.BlockSpec((ts, tm), lambda i, j, k: (k, i)), pl.BlockSpec((ts, tn), lambda i, j, k: (k, j))]
        + hosted,
        out_specs=[pl.BlockSpec((tm, tn), lambda i, j, k: (i, j))] + hosted,
        out_shape=[jax.ShapeDtypeStruct((m, nn), dtype)] + (exchange.out_shape if exchange else []),
        scratch_shapes=[pltpu.VMEM((tm, tn), F32)] + (exchange.scratch if exchange else []),
        compiler_params=_params(("arbitrary", "arbitrary", "arbitrary")),
    )(a, b, *ex_grads)


def _wgrad_pool(pg, dya0, ts):
    s = pg.shape[0]
    nk = s // ts

    def body(a_ref, b_ref, o_ref, acc):
        k = pl.program_id(1)
        part = _dot_tn(a_ref[...], b_ref[...])

        @pl.when(k == 0)
        def _():
            acc[...] = part

        @pl.when(k > 0)
        def _():
            acc[...] += part

        @pl.when(k == nk - 1)
        def _():
            o_ref[0] = acc[...].astype(BF16)

    return pl.pallas_call(
        body, name="wgrad_pool", grid=(NG, nk),
        in_specs=[pl.BlockSpec((ts, GW), lambda g, k: (k, g)), pl.BlockSpec((ts, GW), lambda g, k: (k, g))],
        out_specs=pl.BlockSpec((1, GW, GW), lambda g, k: (g, 0, 0)),
        out_shape=jax.ShapeDtypeStruct((NG, GW, GW), BF16),
        scratch_shapes=[pltpu.VMEM((GW, GW), F32)],
        compiler_params=_params(("arbitrary", "arbitrary")),
    )(pg, dya0)


FFN_SLABS = ((0, 1408), (1408, 2816))
TS_PROJ = 512
TS_MIX = 256
TS_FFN = 256
TS_WGRAD = 1024


def _local_step(x, tgt, vec_d, vec_f, placed, place):
    s = x.shape[0]
    tw = min(TS_WGRAD, s)
    sp_in, sp_pool, sp_bout, sp_o, sp_up, sp_down = SHARDED
    proj, h1, xs, w_in, w_pool, w_bout, w_o = _fwd_proj(x, vec_d, placed[0], placed[1:4], place, min(TS_PROJ, s))
    x1, o, pg, q, merged, ya0, yb, cv, w_up, w_down = _fwd_mix(proj, xs, vec_d, w_pool, w_bout, w_o, placed[4:],
                                                               min(TS_MIX, s))
    up0, upc, a, h2, dx2, dff, vo_f, loss = _fwd_ffn(x1, tgt, vec_d, vec_f, w_up, w_down, min(TS_FFN, s))
    g_down, = _wgrad(a, dff, F // 2, D, tw, "wgrad_down", BF16)
    dx1, dup0, vo_b, fo, r_down = _bwd_ffn(dff, dx2, x1, up0, upc, vec_d, vec_f, w_up, w_down,
                                           _GradExchange([sp_down]), [g_down], min(TS_FFN, s))
    g_up, = _wgrad(h2, dup0, D, F2 // NCHIP, tw, "wgrad_up", BF16)
    dproj, do, dyb, dya0, vo_m, r_up = _bwd_mix(dx1, o, proj, cv, ya0, yb, vec_d, w_pool, w_bout, w_o,
                                                _GradExchange([sp_up]), [g_up], min(TS_MIX, s))
    g_o, = _wgrad(merged, do, D, D, tw, "wgrad_o", BF16)
    g_bout, = _wgrad(q, dyb, D, D, tw, "wgrad_bout", BF16)
    g_pool = _wgrad_pool(pg, dya0, tw)
    g_in, r_pool, r_bout, r_o = _wgrad(h1, dproj, D, DIN // NCHIP, tw, "wgrad_in", BF16,
                                       _GradExchange([sp_pool, sp_bout, sp_o]), [g_pool, g_bout, g_o])
    dx, vo_i, r_in = _bwd_in(dproj, dx1, xs, vec_d, w_in, _GradExchange([sp_in]), [g_in], min(TS_PROJ, s))
    vecs = dict(
        dsh1=vo_i[0], dsc1=vo_i[1], dg_pre_mix=vo_i[2],
        dgt1=vo_m[0], dg_post_mix=vo_m[1], dpool_scale=vo_m[2], dconv_b=vo_m[3],
        dconv_w=vo_m[4:7],
        dsh2=vo_b[0], dsc2=vo_b[1], dg_pre_ffn=vo_b[2],
        dgt2=vo_f[0], dg_post_ffn=vo_f[1],
        dffn_conv_w=fo[FV_W0:FV_W2 + 1], dffn_conv_b=fo[FV_B],
    )
    local = dict(w_in=g_in, w_pool=g_pool, w_bout=g_bout, w_o=g_o, w_up=g_up, w_down=g_down)
    received = dict(w_in=r_in, w_pool=r_pool, w_bout=r_bout, w_o=r_o, w_up=r_up, w_down=r_down)
    return loss, dx, vecs, local, received


def _aligned(offset, n):
    return offset if isinstance(offset, int) else pl.multiple_of(offset, n)


class _Sharded:
    def __init__(self, name, full_shape, shard_axis, half_axis):
        self.name = name
        self.full_shape = full_shape
        self.shard_axis = shard_axis
        self.half_axis = half_axis
        self.shard_shape = tuple(n // NCHIP if a == shard_axis else n for a, n in enumerate(full_shape))
        self.piece_shape = tuple(n // 2 if a == half_axis else n for a, n in enumerate(self.shard_shape))

    def piece(self, full_ref, k, h):
        idx = []
        for a, n in enumerate(self.piece_shape):
            if a == self.shard_axis and a == self.half_axis:
                idx.append(pl.ds(_aligned((2 * k + h) * n, n), n))
            elif a == self.shard_axis:
                idx.append(pl.ds(_aligned(k * n, n), n))
            elif a == self.half_axis:
                idx.append(pl.ds(_aligned(h * n, n), n))
            else:
                idx.append(slice(None))
        return full_ref.at[tuple(idx)]

    def shard(self, full_ref, k):
        n = self.shard_shape[self.shard_axis]
        idx = [pl.ds(_aligned(k * n, n), n) if a == self.shard_axis else slice(None)
               for a in range(len(self.full_shape))]
        return full_ref.at[tuple(idx)]

    def half(self, shard_ref, h):
        n = self.piece_shape[self.half_axis]
        idx = [pl.ds(_aligned(h * n, n), n) if a == self.half_axis else slice(None)
               for a in range(len(self.full_shape))]
        return shard_ref.at[tuple(idx)]

    def piece_block(self):
        def index_map(k, c_ref):
            c = c_ref[0]
            out = []
            for a in range(len(self.full_shape)):
                if a == self.shard_axis and a == self.half_axis:
                    out.append(2 * k + c)
                elif a == self.shard_axis:
                    out.append(k)
                elif a == self.half_axis:
                    out.append(c)
                else:
                    out.append(0)
            return tuple(out)
        return pl.BlockSpec(self.piece_shape, index_map)


SHARDED = (
    _Sharded("w_in", (D, DIN), 1, 0),
    _Sharded("w_pool", (NG, GW, GW), 1, 0),
    _Sharded("w_bout", (D, D), 0, 0),
    _Sharded("w_o", (D, D), 0, 0),
    _Sharded("w_up", (D, F2), 1, 0),
    _Sharded("w_down", (F, D), 0, 0),
)
NW = len(SHARDED)


def _mesh_place():
    x, y, c = lax.axis_index("x"), lax.axis_index("y"), lax.axis_index("c")
    chips = [(1 - x, y), (x, 1 - y), (1 - x, 1 - y)]
    return x, y, c, 2 * x + y, chips, [2 * px + py for px, py in chips]


def _remote(src, dst, send_sem, recv_sem, device):
    return pltpu.make_async_remote_copy(src_ref=src, dst_ref=dst, send_sem=send_sem, recv_sem=recv_sem,
                                        device_id=device, device_id_type=MESH)


def _all_gather_small(block, name):
    m_per, n = block.shape

    def body(x_ref, out_ref, send_sems, recv_sems, local_sem):
        x, y, c, _, chips, _ = _mesh_place()
        me, sibling = (x, y, c), (x, y, 1 - c)

        def rows(px, py, pc):
            return out_ref.at[pl.ds((4 * px + 2 * py + pc) * m_per, m_per), :]

        def copy(k, blk, to, src=None):
            return _remote(rows(*blk) if src is None else src, rows(*blk), send_sems.at[k], recv_sems.at[k], to)

        mine = pltpu.make_async_copy(x_ref, rows(*me), local_sem)
        mine.start()
        first = [copy(0, me, sibling, src=x_ref)]
        first += [copy(1 + j, me, (*chip, c), src=x_ref) for j, chip in enumerate(chips)]
        for cp in first:
            cp.start()
        passed = [copy(4 + j, (*chip, c), sibling) for j, chip in enumerate(chips)]
        for j, chip in enumerate(chips):
            copy(1 + j, (*chip, c), me).wait_recv()
            passed[j].start()
        copy(0, sibling, me).wait_recv()
        for j, chip in enumerate(chips):
            copy(4 + j, (*chip, 1 - c), me).wait_recv()
        for cp in first + passed:
            cp.wait_send()
        mine.wait()

    return pl.pallas_call(
        body, name=name,
        out_shape=jax.ShapeDtypeStruct((NDEV * m_per, n), block.dtype),
        in_specs=[pl.BlockSpec(memory_space=pltpu.VMEM)],
        out_specs=pl.BlockSpec(memory_space=pltpu.VMEM),
        scratch_shapes=[pltpu.SemaphoreType.DMA((7,)), pltpu.SemaphoreType.DMA((7,)), pltpu.SemaphoreType.DMA],
        compiler_params=pltpu.CompilerParams(vmem_limit_bytes=VMEM_LIMIT),
    )(block)


class _WeightGather:
    def __init__(self, specs):
        self.specs = specs
        self.n = len(specs)
        self.specs_any = [pl.BlockSpec(memory_space=pl.ANY)] * self.n
        self.out_shape = [jax.ShapeDtypeStruct(sp.full_shape, BF16) for sp in specs]
        self.scratch = [pltpu.SemaphoreType.DMA((6 * self.n,)), pltpu.SemaphoreType.DMA((6 * self.n,))]

    def _sends(self, outs, send_sems, recv_sems):
        x, y, c, k_me, chips, _ = _mesh_place()
        sends = []
        for j, chip in enumerate(chips):
            for w, sp in enumerate(self.specs):
                mine = sp.piece(outs[w], k_me, c)
                sends.append(_remote(mine, mine, send_sems.at[6 * w + j], recv_sems.at[6 * w + j], (*chip, c)))
        return sends

    def start(self, outs, send_sems, recv_sems):
        for cp in self._sends(outs, send_sems, recv_sems):
            cp.start()

    def _passes(self, outs, send_sems, recv_sems):
        x, y, c, _, chips, kidx = _mesh_place()
        return [_remote(sp.piece(outs[w], kidx[j], c), sp.piece(outs[w], kidx[j], c),
                        send_sems.at[6 * w + 3 + j], recv_sems.at[6 * w + 3 + j], (x, y, 1 - c))
                for j in range(3) for w, sp in enumerate(self.specs)]

    def forward(self, outs, send_sems, recv_sems):
        x, y, c, _, chips, kidx = _mesh_place()
        for j, chip in enumerate(chips):
            for w, sp in enumerate(self.specs):
                landed = sp.piece(outs[w], kidx[j], c)
                _remote(landed, landed, send_sems.at[6 * w + j], recv_sems.at[6 * w + j], (*chip, c)).wait_recv()
        for cp in self._passes(outs, send_sems, recv_sems):
            cp.start()

    def drain(self, outs, send_sems, recv_sems):
        x, y, c, _, chips, kidx = _mesh_place()
        for j in range(3):
            for w, sp in enumerate(self.specs):
                landed = sp.piece(outs[w], kidx[j], 1 - c)
                _remote(landed, landed, send_sems.at[6 * w + 3 + j], recv_sems.at[6 * w + 3 + j],
                        (x, y, 1 - c)).wait_recv()
        for cp in self._sends(outs, send_sems, recv_sems) + self._passes(outs, send_sems, recv_sems):
            cp.wait_send()

    def finish(self, outs, send_sems, recv_sems):
        self.forward(outs, send_sems, recv_sems)
        self.drain(outs, send_sems, recv_sems)


def _gather_weights(placed, specs, name):
    gather = _WeightGather(specs)
    n = gather.n

    def body(*refs):
        outs, sems = refs[n:2 * n], refs[2 * n:]
        gather.start(outs, *sems)
        gather.finish(outs, *sems)

    return pl.pallas_call(
        body, name=name, out_shape=gather.out_shape, in_specs=gather.specs_any, out_specs=gather.specs_any,
        input_output_aliases={w: w for w in range(n)}, scratch_shapes=gather.scratch,
    )(*placed)


class _GradExchange:
    def __init__(self, specs):
        self.specs = specs
        self.n = len(specs)
        self.specs_any = [pl.BlockSpec(memory_space=pl.ANY)] * self.n
        self.out_shape = [jax.ShapeDtypeStruct((NDEV,) + sp.piece_shape, BF16) for sp in specs]
        self.scratch = [pltpu.SemaphoreType.DMA((7 * self.n,)), pltpu.SemaphoreType.DMA((NDEV * self.n,))]

    def _sends(self, grads, recvs, send_sems, recv_sems):
        x, y, c, k_me, chips, kidx = _mesh_place()
        dev = 2 * k_me + c
        sends = []
        for w, sp in enumerate(self.specs):
            slot, arrival = recvs[w].at[dev], recv_sems.at[NDEV * w + dev]
            sends.append(_remote(sp.piece(grads[w], k_me, 1 - c), slot, send_sems.at[7 * w], arrival, (x, y, 1 - c)))
            for j, chip in enumerate(chips):
                for h in range(2):
                    sends.append(_remote(sp.piece(grads[w], kidx[j], h), slot, send_sems.at[7 * w + 1 + 2 * j + h],
                                         arrival, (*chip, h)))
        return sends

    def start(self, grads, recvs, send_sems, recv_sems):
        for cp in self._sends(grads, recvs, send_sems, recv_sems):
            cp.start()

    def finish(self, grads, recvs, send_sems, recv_sems):
        x, y, c, k_me, _, _ = _mesh_place()
        dev = 2 * k_me + c
        for w in range(self.n):
            for d in range(NDEV):
                landed = recvs[w].at[d]
                arrival = _remote(landed, landed, send_sems.at[7 * w], recv_sems.at[NDEV * w + d], (x, y, c))
                pl.when(d != dev)(arrival.wait_recv)
        for cp in self._sends(grads, recvs, send_sems, recv_sems):
            cp.wait_send()


def _device_sum(sp, local, recv, place):
    nd = len(sp.piece_shape)

    def body(p_ref, a_ref, b_ref, o_ref):
        d = pl.program_id(0)
        term = jnp.where(d == p_ref[2], a_ref[...], b_ref[...]).astype(F32)

        @pl.when(d == 0)
        def _():
            o_ref[...] = term

        @pl.when(d > 0)
        def _():
            o_ref[...] += term

    def mine(d, p_ref):
        return tuple(2 * p_ref[0] + p_ref[1] if a == sp.shard_axis == sp.half_axis else
                     p_ref[0] if a == sp.shard_axis else p_ref[1] if a == sp.half_axis else 0 for a in range(nd))

    def others(d, p_ref):
        return (jnp.where(d == p_ref[2], (d + 1) % NDEV, d),) + (0,) * nd

    return pl.pallas_call(
        body, name="rs_device_sum_" + sp.name,
        grid_spec=pltpu.PrefetchScalarGridSpec(
            num_scalar_prefetch=1, grid=(NDEV,),
            in_specs=[pl.BlockSpec(sp.piece_shape, mine), pl.BlockSpec((None,) + sp.piece_shape, others)],
            out_specs=pl.BlockSpec(sp.piece_shape,
                                   lambda d, p_ref: tuple(p_ref[1] if a == sp.half_axis else 0 for a in range(nd)))),
        out_shape=jax.ShapeDtypeStruct(sp.shard_shape, F32),
        compiler_params=_params(("arbitrary",)),
    )(place, local, recv)


def _pair_exchange(grads, specs):
    n = len(specs)

    def body(*refs):
        ins, outs = refs[:n], refs[n:2 * n]
        send_sems, recv_sems = refs[2 * n:]
        x, y, c, _, _, _ = _mesh_place()
        sibling = (x, y, 1 - c)
        sent = []
        for w, sp in enumerate(specs):
            for k in range(NCHIP):
                cp = _remote(sp.piece(ins[w], k, 1 - c), outs[w].at[k],
                             send_sems.at[NCHIP * w + k], recv_sems.at[NCHIP * w + k], sibling)
                cp.start()
                sent.append(cp)
        for cp in sent:
            cp.wait_recv()
        for cp in sent:
            cp.wait_send()

    hbm = pl.BlockSpec(memory_space=pl.ANY)
    return pl.pallas_call(
        body, name="rs_pair_exchange",
        out_shape=[jax.ShapeDtypeStruct((NCHIP,) + sp.piece_shape, F32) for sp in specs],
        in_specs=[hbm] * n, out_specs=[hbm] * n,
        scratch_shapes=[pltpu.SemaphoreType.DMA((NCHIP * n,)), pltpu.SemaphoreType.DMA((NCHIP * n,))],
    )(*grads)


def _pair_sum(sp, grad, recv, core):
    nd = len(sp.piece_shape)

    def body(c_ref, g_ref, r_ref, o_ref):
        o_ref[...] = (g_ref[...] + r_ref[...]).astype(BF16)

    slot = pl.BlockSpec((None,) + sp.piece_shape, lambda k, c_ref: (k,) + (0,) * nd)
    return pl.pallas_call(
        body, name="rs_pair_sum_" + sp.name,
        grid_spec=pltpu.PrefetchScalarGridSpec(
            num_scalar_prefetch=1, grid=(NCHIP,),
            in_specs=[sp.piece_block(), slot], out_specs=slot),
        out_shape=jax.ShapeDtypeStruct((NCHIP,) + sp.piece_shape, BF16),
        compiler_params=_params(("parallel",)),
    )(core, grad, recv)


def _chip_exchange(parts, specs):
    n = len(specs)

    def body(*refs):
        ins, outs = refs[:n], refs[n:2 * n]
        send_sems, recv_sems = refs[2 * n:]
        x, y, c, k_me, chips, kidx = _mesh_place()
        sent = []
        for j, chip in enumerate(chips):
            for w in range(n):
                cp = _remote(ins[w].at[kidx[j]], outs[w].at[k_me], send_sems.at[3 * w + j], recv_sems.at[3 * w + j],
                             (*chip, c))
                cp.start()
                sent.append(cp)
        for j, chip in enumerate(chips):
            for w in range(n):
                landed = outs[w].at[kidx[j]]
                _remote(landed, landed, send_sems.at[3 * w + j], recv_sems.at[3 * w + j], (*chip, c)).wait_recv()
        for cp in sent:
            cp.wait_send()

    hbm = pl.BlockSpec(memory_space=pl.ANY)
    return pl.pallas_call(
        body, name="rs_chip_exchange",
        out_shape=[jax.ShapeDtypeStruct((NCHIP,) + sp.piece_shape, BF16) for sp in specs],
        in_specs=[hbm] * n, out_specs=[hbm] * n,
        scratch_shapes=[pltpu.SemaphoreType.DMA((3 * n,)), pltpu.SemaphoreType.DMA((3 * n,))],
    )(*parts)


def _chip_sum(sp, parts, recv, place):
    nd = len(sp.piece_shape)

    def body(p_ref, a_ref, b_ref, o_ref):
        k = pl.program_id(0)
        term = jnp.where(k == p_ref[0], a_ref[...], b_ref[...]).astype(F32)

        @pl.when(k == 0)
        def _():
            o_ref[...] = term

        @pl.when(k > 0)
        def _():
            o_ref[...] += term

    def others(k, p_ref):
        return (jnp.where(k == p_ref[0], (k + 1) % NCHIP, k),) + (0,) * nd

    return pl.pallas_call(
        body, name="rs_chip_sum_" + sp.name,
        grid_spec=pltpu.PrefetchScalarGridSpec(
            num_scalar_prefetch=1, grid=(NCHIP,),
            in_specs=[pl.BlockSpec((None,) + sp.piece_shape, lambda k, p_ref: (p_ref[0],) + (0,) * nd),
                      pl.BlockSpec((None,) + sp.piece_shape, others)],
            out_specs=pl.BlockSpec(sp.piece_shape,
                                   lambda k, p_ref: tuple(p_ref[1] if a == sp.half_axis else 0 for a in range(nd)))),
        out_shape=jax.ShapeDtypeStruct(sp.shard_shape, F32),
        compiler_params=_params(("arbitrary",)),
    )(place, parts, recv)


def _pair_share(halves):
    def body(*refs):
        outs = refs[NW:2 * NW]
        send_sems, recv_sems = refs[2 * NW:]
        x, y, c, _, _, _ = _mesh_place()
        sibling = (x, y, 1 - c)
        sent = []
        for w, sp in enumerate(SHARDED):
            mine = sp.half(outs[w], c)
            cp = _remote(mine, mine, send_sems.at[w], recv_sems.at[w], sibling)
            cp.start()
            sent.append(cp)
        for w, sp in enumerate(SHARDED):
            landed = sp.half(outs[w], 1 - c)
            _remote(landed, landed, send_sems.at[w], recv_sems.at[w], sibling).wait_recv()
        for cp in sent:
            cp.wait_send()

    hbm = pl.BlockSpec(memory_space=pl.ANY)
    return pl.pallas_call(
        body, name="rs_pair_share",
        out_shape=[jax.ShapeDtypeStruct(sp.shard_shape, F32) for sp in SHARDED],
        in_specs=[hbm] * NW, out_specs=[hbm] * NW,
        input_output_aliases={w: w for w in range(NW)},
        scratch_shapes=[pltpu.SemaphoreType.DMA((NW,)), pltpu.SemaphoreType.DMA((NW,))],
    )(*halves)


def _reduce_scatter(local, received, place):
    return _pair_share([_device_sum(sp, local[sp.name], received[sp.name], place) for sp in SHARDED])


def _place_bf16(sp, w, place):
    nd = len(sp.full_shape)

    def body(p_ref, w_ref, o_ref):
        o_ref[...] = w_ref[...].astype(BF16)

    return pl.pallas_call(
        body, name="place_" + sp.name,
        grid_spec=pltpu.PrefetchScalarGridSpec(
            num_scalar_prefetch=1, grid=(1,),
            in_specs=[pl.BlockSpec(sp.shard_shape, lambda i, p_ref: (0,) * nd)],
            out_specs=pl.BlockSpec(sp.shard_shape,
                                   lambda i, p_ref: tuple(p_ref[0] if a == sp.shard_axis else 0 for a in range(nd)))),
        out_shape=jax.ShapeDtypeStruct(sp.full_shape, BF16),
        compiler_params=_params(("arbitrary",)),
    )(place, w)


def _matmul_f32(a, b, name):
    def body(a_ref, b_ref, o_ref):
        o_ref[...] = jnp.dot(a_ref[...], b_ref[...], preferred_element_type=F32, precision=lax.Precision.HIGHEST)

    return pl.pallas_call(body, name=name, out_shape=jax.ShapeDtypeStruct((a.shape[0], b.shape[1]), F32),
                          compiler_params=pltpu.CompilerParams(vmem_limit_bytes=VMEM_LIMIT))(a, b)


def _sum_devices(stacked):
    def body(x_ref, o_ref):
        acc = x_ref[0]
        for d in range(1, NDEV):
            acc = acc + x_ref[d]
        o_ref[...] = acc

    return pl.pallas_call(body, name="sum_devices", out_shape=jax.ShapeDtypeStruct(stacked.shape[1:], F32),
                          compiler_params=pltpu.CompilerParams(vmem_limit_bytes=VMEM_LIMIT))(stacked)


def _adamw(w, g, m, v, name):
    r, cdim = w.shape
    tr = r if r <= 256 else (256 if r % 256 == 0 else r // 2)

    def body(w_ref, g_ref, m_ref, v_ref, d_ref, nm_ref, nv_ref):
        gv = g_ref[...]
        nm = ADAM_B1 * m_ref[...] + (1.0 - ADAM_B1) * gv
        nv = ADAM_B2 * v_ref[...] + (1.0 - ADAM_B2) * (gv * gv)
        m_hat = nm / (1.0 - ADAM_B1 ** ADAM_STEP)
        v_hat = nv / (1.0 - ADAM_B2 ** ADAM_STEP)
        d_ref[...] = -ADAM_LR * (m_hat / (jnp.sqrt(v_hat) + ADAM_EPS) + ADAM_WD * w_ref[...])
        nm_ref[...] = nm
        nv_ref[...] = nv

    blk = pl.BlockSpec((tr, cdim), lambda i: (i, 0))
    return pl.pallas_call(
        body, name="adamw_" + name, grid=(r // tr,), in_specs=[blk] * 4, out_specs=[blk] * 3,
        out_shape=[jax.ShapeDtypeStruct(w.shape, F32)] * 3,
        compiler_params=_params(("parallel",)),
    )(w, g, m, v)


WEIGHT_NAMES = ("g_pre_mix", "g_post_mix", "g_pre_ffn", "g_post_ffn", "w_ada", "b_ada", "w_in", "w_pool",
                "pool_scale", "conv_w", "conv_b", "w_bout", "w_o", "w_up", "ffn_conv_w", "ffn_conv_b", "w_down")
MATRIX_NAMES = ("w_ada",) + tuple(sp.name for sp in SHARDED)
VECTOR_NAMES = tuple(n for n in WEIGHT_NAMES if n not in MATRIX_NAMES)

CW = D // NCHIP
FCW = F2 // NCHIP
ADA_W = DIN // NCHIP
COND_BLOCK = (8, 768)
GRAD_BLOCK = (8, 4864)


def _flat_pad(parts, shape):
    flat = jnp.concatenate([p.reshape(-1) for p in parts])
    return jnp.pad(flat, (0, shape[0] * shape[1] - flat.shape[0])).reshape(shape)


def _take(flat, offset, shape):
    size = 1
    for n in shape:
        size *= n
    return flat[offset:offset + size].reshape(shape), offset + size


def kernel(x, c, g_pre_mix, g_post_mix, g_pre_ffn, g_post_ffn, w_ada, b_ada, w_in, w_pool, pool_scale, conv_w, conv_b, w_bout, w_o, w_up, ffn_conv_w, ffn_conv_b, w_down, loss_target, m_g_pre_mix, m_g_post_mix, m_g_pre_ffn, m_g_post_ffn, m_w_ada, m_b_ada, m_w_in, m_w_pool, m_pool_scale, m_conv_w, m_conv_b, m_w_bout, m_w_o, m_w_up, m_ffn_conv_w, m_ffn_conv_b, m_w_down, v_g_pre_mix, v_g_post_mix, v_g_pre_ffn, v_g_post_ffn, v_w_ada, v_b_ada, v_w_in, v_w_pool, v_pool_scale, v_conv_w, v_conv_b, v_w_bout, v_w_o, v_w_up, v_ffn_conv_w, v_ffn_conv_b, v_w_down):
    weights = dict(g_pre_mix=g_pre_mix, g_post_mix=g_post_mix, g_pre_ffn=g_pre_ffn, g_post_ffn=g_post_ffn,
                   w_ada=w_ada, b_ada=b_ada, w_in=w_in, w_pool=w_pool, pool_scale=pool_scale, conv_w=conv_w,
                   conv_b=conv_b, w_bout=w_bout, w_o=w_o, w_up=w_up, ffn_conv_w=ffn_conv_w, ffn_conv_b=ffn_conv_b,
                   w_down=w_down)
    mom1 = dict(g_pre_mix=m_g_pre_mix, g_post_mix=m_g_post_mix, g_pre_ffn=m_g_pre_ffn, g_post_ffn=m_g_post_ffn,
                w_ada=m_w_ada, b_ada=m_b_ada, w_in=m_w_in, w_pool=m_w_pool, pool_scale=m_pool_scale,
                conv_w=m_conv_w, conv_b=m_conv_b, w_bout=m_w_bout, w_o=m_w_o, w_up=m_w_up,
                ffn_conv_w=m_ffn_conv_w, ffn_conv_b=m_ffn_conv_b, w_down=m_w_down)
    mom2 = dict(g_pre_mix=v_g_pre_mix, g_post_mix=v_g_post_mix, g_pre_ffn=v_g_pre_ffn, g_post_ffn=v_g_post_ffn,
                w_ada=v_w_ada, b_ada=v_b_ada, w_in=v_w_in, w_pool=v_w_pool, pool_scale=v_pool_scale,
                conv_w=v_conv_w, conv_b=v_conv_b, w_bout=v_w_bout, w_o=v_w_o, w_up=v_w_up,
                ffn_conv_w=v_ffn_conv_w, ffn_conv_b=v_ffn_conv_b, w_down=v_w_down)

    chip = 2 * lax.axis_index("x") + lax.axis_index("y")
    core = lax.axis_index("c")
    dev = 2 * chip + core
    place = jnp.stack([chip, core, dev]).astype(jnp.int32)

    cond = _all_gather_small(_flat_pad([c, conv_w, ffn_conv_w], COND_BLOCK), "gather_cond")
    cond = cond.reshape(NDEV, -1)
    c_all = cond[:, :D]
    by_chip = cond[0::2]
    conv_w_full = by_chip[:, D:D + 3 * CW].reshape(NCHIP, 3, CW).transpose(1, 0, 2).reshape(3, D)
    ffn_w_full = by_chip[:, D + 3 * CW:D + 3 * CW + 3 * FCW].reshape(NCHIP, 3, FCW).transpose(1, 0, 2).reshape(3, F2)

    mod_cols = _all_gather_small(_matmul_f32(c_all, w_ada[0], "ada_mod"), "gather_mod")
    mod_cols = mod_cols.reshape(NDEV, NDEV, ADA_W)[0::2]
    mod = lax.dynamic_index_in_dim(mod_cols, dev, axis=1, keepdims=False).reshape(6, D) + b_ada.reshape(6, D)
    vec_d = jnp.concatenate([mod, g_pre_mix, g_post_mix, g_pre_ffn, g_post_ffn, pool_scale, conv_b, conv_w_full,
                             jnp.zeros((VD_ROWS - 15, D), F32)], axis=0)
    vec_f = jnp.concatenate([ffn_w_full, ffn_conv_b, jnp.zeros((FV_ROWS - 4, F2), F32)], axis=0)

    placed = [_place_bf16(sp, weights[sp.name][0], place) for sp in SHARDED]
    loss_blk, dx, vecs, local, received = _local_step(x[0], loss_target[0], vec_d, vec_f, placed, place)

    dmod = [vecs[n] for n in ("dsh1", "dsc1", "dgt1", "dsh2", "dsc2", "dgt2")]
    small = [vecs["dg_pre_mix"], vecs["dg_post_mix"], vecs["dg_pre_ffn"], vecs["dg_post_ffn"]] + dmod + [
        vecs["dpool_scale"], vecs["dconv_w"], vecs["dconv_b"], vecs["dffn_conv_w"], vecs["dffn_conv_b"],
        loss_blk[0]]
    gathered = _all_gather_small(_flat_pad(small, GRAD_BLOCK), "gather_vector_grads")
    total = _sum_devices(gathered.reshape((NDEV,) + GRAD_BLOCK)).reshape(-1)
    vgrad = {}
    off = 0
    for n in ("g_pre_mix", "g_post_mix", "g_pre_ffn", "g_post_ffn"):
        vgrad[n], off = _take(total, off, (1, D))
    dmod_off = off
    vgrad["b_ada"], off = _take(total, off, (1, DIN))
    vgrad["pool_scale"], off = _take(total, off, (1, D))
    g_conv_w, off = _take(total, off, (3, D))
    vgrad["conv_w"] = lax.dynamic_slice_in_dim(g_conv_w, chip * CW, CW, axis=1)[None]
    vgrad["conv_b"], off = _take(total, off, (1, D))
    g_ffn_w, off = _take(total, off, (3, F2))
    vgrad["ffn_conv_w"] = lax.dynamic_slice_in_dim(g_ffn_w, chip * FCW, FCW, axis=1)[None]
    vgrad["ffn_conv_b"], off = _take(total, off, (1, F2))
    loss = total[off]

    dmod_all = gathered.reshape(NDEV, -1)[:, dmod_off:dmod_off + DIN]
    dmod_cols = lax.dynamic_slice_in_dim(dmod_all, chip * ADA_W, ADA_W, axis=1)
    g_ada = _matmul_f32(jnp.pad(c_all.T, ((0, 0), (0, 128 - NDEV))), jnp.pad(dmod_cols, ((0, 128 - NDEV), (0, 0))),
                        "ada_wgrad")

    reduced = _reduce_scatter(local, received, place)
    mgrad = {"w_ada": g_ada}
    for sp, g in zip(SHARDED, reduced):
        mgrad[sp.name] = g

    grad, delta, new_m, new_v = {}, {}, {}, {}
    for n in MATRIX_NAMES:
        shape = weights[n].shape
        two_d = (-1, shape[-1])
        d, nm, nv = _adamw(weights[n].reshape(two_d), mgrad[n].reshape(two_d), mom1[n].reshape(two_d),
                           mom2[n].reshape(two_d), n)
        grad[n], delta[n], new_m[n], new_v[n] = (a.reshape(shape) for a in (mgrad[n], d, nm, nv))
    flat = lambda tree: jnp.concatenate([tree[n].reshape(1, -1) for n in VECTOR_NAMES], axis=1)
    d, nm, nv = _adamw(flat(weights), flat(vgrad), flat(mom1), flat(mom2), "vectors")
    off = 0
    for n in VECTOR_NAMES:
        shape = weights[n].shape
        grad[n] = vgrad[n].reshape(shape)
        delta[n], _ = _take(d[0], off, shape)
        new_m[n], _ = _take(nm[0], off, shape)
        new_v[n], off = _take(nv[0], off, shape)

    return (loss, dx[None], *[grad[n] for n in WEIGHT_NAMES], *[delta[n] for n in WEIGHT_NAMES],
            *[new_m[n] for n in WEIGHT_NAMES], *[new_v[n] for n in WEIGHT_NAMES])
```

```python
import jax
import jax.numpy as jnp
from jax import lax
from jax.experimental import pallas as pl
from jax.experimental.pallas import tpu as pltpu

F32 = jnp.float32
BF16 = jnp.bfloat16

D = 1024
DIN = 6 * D
F = 2816
F2 = 2 * F
NG = 4
GW = D // NG
POOL_CARRY = 16
CONV_CARRY = 3
EPS = 1e-6
NCHIP = 4
NDEV = 8

ADAM_LR = 0.001
ADAM_B1 = 0.9
ADAM_B2 = 0.999
ADAM_EPS = 1e-08
ADAM_WD = 0.01
ADAM_STEP = 10

VMEM_LIMIT = 60 * 1024 * 1024

(V_SH1, V_SC1, V_GT1, V_SH2, V_SC2, V_GT2, V_GPRE1, V_GPOST1, V_GPRE2, V_GPOST2,
 V_PSCALE, V_CB, V_CW0, V_CW1, V_CW2) = range(15)
VD_ROWS = 16
FV_W0, FV_W1, FV_W2, FV_B = range(4)
FV_ROWS = 8

MESH = pl.DeviceIdType.MESH


def _params(sem=None, vmem=VMEM_LIMIT):
    return pltpu.CompilerParams(dimension_semantics=sem, vmem_limit_bytes=vmem)


def _row(ref, r):
    return ref[r:r + 1, :]


def _load_once(pairs, sem):
    @pl.when(pl.program_id(0) == 0)
    def _():
        copies = [pltpu.make_async_copy(src, dst, sem.at[n]) for n, (src, dst) in enumerate(pairs)]
        for cp in copies:
            cp.start()
        for cp in copies:
            cp.wait()


def _dot(a, b):
    return jnp.dot(a, b, preferred_element_type=F32)


def _dot_nt(a, b):
    return lax.dot_general(a, b, (((1,), (1,)), ((), ())), preferred_element_type=F32)


BLK = 256
SEG = BLK // 8


def _load_rows(ref, ts):
    blocks = [jnp.swapaxes(ref[b * BLK:(b + 1) * BLK, :].reshape(8, SEG, ref.shape[-1]), 0, 1).reshape(BLK, -1)
              for b in range(ts // BLK)]
    return jnp.concatenate(blocks, axis=0)


def _store_rows(ref, val, ts):
    for b in range(ts // BLK):
        blk = val[b * BLK:(b + 1) * BLK, :].reshape(SEG, 8, val.shape[-1])
        ref[b * BLK:(b + 1) * BLK, :] = jnp.swapaxes(blk, 0, 1).reshape(BLK, -1)


def _times(t0):
    p = lax.broadcasted_iota(jnp.int32, (BLK, 1), 0)
    return t0 + (p & 7) * SEG + (p >> 3)


def _before(x, carry, s):
    x3 = x.reshape(SEG, 8, x.shape[-1])
    tail = pltpu.roll(x3[SEG - s:], 1, 1)
    row = lax.broadcasted_iota(jnp.int32, tail.shape, 1)
    out = jnp.concatenate([jnp.where(row == 0, carry, tail), x3[:SEG - s]], axis=0)
    return out.reshape(x.shape), tail


def _after(x, carry, s):
    x3 = x.reshape(SEG, 8, x.shape[-1])
    head = pltpu.roll(x3[:s], 7, 1)
    row = lax.broadcasted_iota(jnp.int32, head.shape, 1)
    out = jnp.concatenate([x3[s:], jnp.where(row == 7, carry, head)], axis=0)
    return out.reshape(x.shape), head


def _causal_conv(x, carry, cols, w0, w1, w2, b):
    x1, carry[0:1, :, cols] = _before(x, carry[0:1, :, cols], 1)
    x2, carry[1:3, :, cols] = _before(x, carry[1:3, :, cols], 2)
    return b + w2 * x + w1 * x1 + w0 * x2


def _causal_conv_bwd(dy, carry, cols, w0, w1, w2):
    d1, carry[0:1, :, cols] = _after(dy, carry[0:1, :, cols], 1)
    d2, carry[1:3, :, cols] = _after(dy, carry[1:3, :, cols], 2)
    return w2 * dy + w1 * d1 + w0 * d2, d1, d2


def _pool_counts(t0, g):
    return jnp.minimum((_times(t0) + 1).astype(F32), float(2 << g))


def _rms(x):
    return lax.rsqrt(jnp.mean(x * x, axis=-1, keepdims=True) + EPS)


def _rms_bwd(dn, n, r):
    return r * (dn - n * jnp.mean(dn * n, axis=-1, keepdims=True))


def _colsum(x):
    return jnp.sum(x, axis=0, keepdims=True)


def _gelu_and_grad(x):
    k = 0.7978845608028654
    inner = k * (x + 0.044715 * (x * x * x))
    th = jnp.tanh(inner)
    gelu = 0.5 * x * (1.0 + th)
    dgelu = 0.5 * (1.0 + th) + 0.5 * x * (1.0 - th * th) * (k * (1.0 + 3.0 * 0.044715 * (x * x)))
    return gelu, dgelu


def _fwd_proj(x, vec_d, placed_in, placed_rest, place, ts):
    s = x.shape[0]
    nt = s // ts
    cw = DIN // NCHIP
    sp_in = SHARDED[0]
    gather = _WeightGather(SHARDED[1:4])
    n = gather.n

    def body(*refs):
        p_ref, x_ref, v_ref = refs[:3]
        proj_ref, h1_ref, xs_ref, w_full = refs[4 + n:8 + n]
        rest = refs[8 + n:8 + 2 * n]
        w_vmem, h1_all, sem, in_send, in_recv, send_sems, recv_sems = refs[8 + 2 * n:]
        j, i = pl.program_id(0), pl.program_id(1)
        x_, y_, c, k_me, _, _ = _mesh_place()
        sibling = (x_, y_, 1 - c)

        def peer(t):
            return (x_ ^ (t >> 1), y_ ^ (t & 1))

        def w_in_sends():
            mine = sp_in.piece(w_full, k_me, c)
            return [_remote(mine, mine, in_send.at[t - 1], in_recv.at[t - 1], (*peer(t), c)) for t in (1, 2, 3)]

        def load_block(k):
            cp = pltpu.make_async_copy(sp_in.shard(w_full, k), w_vmem.at[k], sem.at[0])
            cp.start()
            cp.wait()

        @pl.when((j == 0) & (i == 0))
        def _():
            for cp in w_in_sends()[:2]:
                cp.start()
            load_block(k_me)

        @pl.when((j == 1) & (i == 0))
        def _():
            for cp in w_in_sends()[:2]:
                cp.wait_send()
            w_in_sends()[2].start()
            gather.start(rest, send_sems, recv_sems)

        for t in (1, 2, 3):
            @pl.when((j == t) & (i == 0))
            def _(t=t):
                k = k_me ^ t
                landed = sp_in.piece(w_full, k, c)
                _remote(landed, landed, in_send.at[t - 1], in_recv.at[t - 1], (*peer(t), c)).wait_recv()
                _remote(landed, landed, in_send.at[2 + t], in_recv.at[2 + t], sibling).start()
                other = sp_in.piece(w_full, k, 1 - c)
                _remote(other, other, in_send.at[2 + t], in_recv.at[2 + t], sibling).wait_recv()
                load_block(k)

        @pl.when(j == 0)
        def _():
            xv = _load_rows(x_ref, ts)
            xs_ref[...] = xv
            n1 = xv * _rms(xv)
            h = n1 * (_row(v_ref, V_GPRE1) * (1.0 + _row(v_ref, V_SC1))) + _row(v_ref, V_SH1)
            hb = h.astype(BF16)
            h1_ref[...] = hb
            h1_all[i] = hb

        proj_ref[...] = _dot(h1_all[i], w_vmem[k_me ^ j]).astype(BF16)

        @pl.when((j == NCHIP - 1) & (i == nt - 1))
        def _():
            w_in_sends()[2].wait_send()
            for t in (1, 2, 3):
                landed = sp_in.piece(w_full, k_me ^ t, c)
                _remote(landed, landed, in_send.at[2 + t], in_recv.at[2 + t], sibling).wait_send()
            gather.finish(rest, send_sems, recv_sems)

    once = lambda w: pl.BlockSpec((ts, w), lambda j, i, p: (jnp.where(j == 0, i, nt - 1), 0))
    return pl.pallas_call(
        body, name="fwd_proj",
        grid_spec=pltpu.PrefetchScalarGridSpec(
            num_scalar_prefetch=1, grid=(NCHIP, nt),
            in_specs=[once(D), pl.BlockSpec((VD_ROWS, D), lambda j, i, p: (0, 0)),
                      pl.BlockSpec(memory_space=pl.ANY)] + gather.specs_any,
            out_specs=[pl.BlockSpec((ts, cw), lambda j, i, p: (i, p[0] ^ j)), once(D), once(D),
                       pl.BlockSpec(memory_space=pl.ANY)] + gather.specs_any,
            scratch_shapes=[pltpu.VMEM((NCHIP, D, cw), BF16), pltpu.VMEM((nt, ts, D), BF16),
                            pltpu.SemaphoreType.DMA((1,)),
                            pltpu.SemaphoreType.DMA((6,)), pltpu.SemaphoreType.DMA((6,))] + gather.scratch),
        out_shape=[jax.ShapeDtypeStruct((s, DIN), BF16), jax.ShapeDtypeStruct((s, D), BF16),
                   jax.ShapeDtypeStruct((s, D), F32), jax.ShapeDtypeStruct(sp_in.full_shape, BF16)] + gather.out_shape,
        input_output_aliases={3 + w: 3 + w for w in range(n + 1)},
        compiler_params=_params(("arbitrary", "arbitrary")),
    )(place, x, vec_d, placed_in, *placed_rest)


def _fwd_mix(proj, x, vec_d, w_pool, w_bout, w_o, placed_ffn, ts):
    s = x.shape[0]
    gather = _WeightGather(SHARDED[4:])
    n = gather.n

    def body(*refs):
        ins, outs, rest = refs[:6], refs[6 + n:14 + n], refs[14 + n:14 + 2 * n]
        scratch, sems = refs[14 + 2 * n:-2], refs[-2:]
        i = pl.program_id(0)
        nt = s // ts
        pl.when(i == 0)(lambda: gather.start(rest, *sems))
        pl.when(i == nt - 1 - nt // 8)(lambda: gather.forward(rest, *sems))
        compute(*ins, *outs, *scratch)
        pl.when(i == nt - 1)(lambda: gather.drain(rest, *sems))

    def compute(p_ref, x_ref, v_ref, wp_hbm, wb_hbm, wo_hbm,
                x1_ref, o_ref, pg_ref, q_ref, mg_ref, ya_ref, yb_ref, cv_ref,
                wp, wb, wo, carry_p, carry_v, sem):
        i = pl.program_id(0)
        _load_once([(wp_hbm, wp), (wb_hbm, wb), (wo_hbm, wo)], sem)

        @pl.when(i == 0)
        def _():
            carry_p[...] = jnp.zeros_like(carry_p)
            carry_v[...] = jnp.zeros_like(carry_v)

        t0 = i * ts
        for g in range(NG):
            cols = slice(g * GW, (g + 1) * GW)
            u = p_ref[:, cols].astype(F32)
            e = u
            for l in range(g + 1):
                slot = slice((1 << l) - 1, (2 << l) - 1)
                shifted, carry_p[slot, :, cols] = _before(e, carry_p[slot, :, cols], 1 << l)
                e = e + shifted
            pgb = (e / _pool_counts(t0, g) - u).astype(BF16)
            pg_ref[:, cols] = pgb
            ya_ref[:, cols] = _dot(pgb, wp[g]).astype(BF16)

        u_x = p_ref[:, D:2 * D].astype(F32)
        u_c = p_ref[:, 3 * D:4 * D].astype(F32)
        v = u_c * u_x
        cv = _causal_conv(v, carry_v, slice(None), _row(v_ref, V_CW0), _row(v_ref, V_CW1),
                          _row(v_ref, V_CW2), _row(v_ref, V_CB))
        cv_ref[...] = cv.astype(BF16)
        q = (p_ref[:, 2 * D:3 * D].astype(F32) * cv).astype(BF16)
        q_ref[...] = q
        y_b = _dot(q, wb[...])
        yb_ref[...] = y_b.astype(BF16)

        y_a = ya_ref[...].astype(F32) * _row(v_ref, V_PSCALE)
        merged = (jax.nn.sigmoid(p_ref[:, 4 * D:5 * D].astype(F32)) * y_a
                  + jax.nn.sigmoid(p_ref[:, 5 * D:6 * D].astype(F32)) * y_b).astype(BF16)
        mg_ref[...] = merged
        o = _dot(merged, wo[...])
        o_ref[...] = o.astype(BF16)
        x1_ref[...] = x_ref[...] + _row(v_ref, V_GT1) * ((o * _rms(o)) * _row(v_ref, V_GPOST1))

    tile = lambda w: pl.BlockSpec((ts, w), lambda i: (i, 0))
    hbm = pl.BlockSpec(memory_space=pl.ANY)
    return pl.pallas_call(
        body, name="fwd_mix", grid=(s // ts,),
        in_specs=[tile(DIN), tile(D), pl.BlockSpec((VD_ROWS, D), lambda i: (0, 0)), hbm, hbm, hbm] + gather.specs_any,
        out_specs=[tile(D)] * 8 + gather.specs_any,
        out_shape=[jax.ShapeDtypeStruct((s, D), F32)] + [jax.ShapeDtypeStruct((s, D), BF16)] * 7 + gather.out_shape,
        input_output_aliases={6 + w: 8 + w for w in range(n)},
        scratch_shapes=[pltpu.VMEM((NG, GW, GW), BF16), pltpu.VMEM((D, D), BF16), pltpu.VMEM((D, D), BF16),
                        pltpu.VMEM((POOL_CARRY, 8, D), F32), pltpu.VMEM((CONV_CARRY, 8, D), F32),
                        pltpu.SemaphoreType.DMA((3,))] + gather.scratch,
        compiler_params=_params(("arbitrary",)),
    )(proj, x, vec_d, w_pool, w_bout, w_o, *placed_ffn)


def _fwd_ffn(x1, tgt, vec_d, vec_f, w_up, w_down, ts):
    s = x1.shape[0]

    def body(x1_ref, t_ref, v_ref, f_ref, wu_hbm, wd_hbm,
             up_ref, upc_ref, a_ref, h2_ref, dx2_ref, dff_ref, vo_ref, loss_ref,
             wu, wd, carry, sem):
        i = pl.program_id(0)
        _load_once([(wu_hbm, wu), (wd_hbm, wd)], sem)

        @pl.when(i == 0)
        def _():
            carry[...] = jnp.zeros_like(carry)
            vo_ref[...] = jnp.zeros_like(vo_ref)
            loss_ref[...] = jnp.zeros_like(loss_ref)

        x1v = x1_ref[...]
        n3 = x1v * _rms(x1v)
        h2 = (n3 * (_row(v_ref, V_GPRE2) * (1.0 + _row(v_ref, V_SC2))) + _row(v_ref, V_SH2)).astype(BF16)
        h2_ref[...] = h2

        ff = jnp.zeros((ts, D), F32)
        for lo, hi in FFN_SLABS_FWD:
            up = []
            for cols in (slice(lo, hi), slice(F + lo, F + hi)):
                u0 = _dot(h2, wu[:, cols])
                up_ref[:, cols] = u0.astype(BF16)
                y = _causal_conv(u0, carry, cols, f_ref[FV_W0:FV_W0 + 1, cols], f_ref[FV_W1:FV_W1 + 1, cols],
                                 f_ref[FV_W2:FV_W2 + 1, cols], f_ref[FV_B:FV_B + 1, cols])
                upc_ref[:, cols] = y.astype(BF16)
                up.append(y)
            gelu, _ = _gelu_and_grad(up[0])
            a = (gelu * up[1]).astype(BF16)
            a_ref[:, lo:hi] = a
            ff = ff + _dot(a, wd[lo:hi, :])

        r4 = _rms(ff)
        n4 = ff * r4
        gt2 = _row(v_ref, V_GT2)
        gpost = _row(v_ref, V_GPOST2)
        y4 = n4 * gpost
        diff = (x1v + gt2 * y4) - _load_rows(t_ref, ts)
        loss_ref[...] += jnp.full(loss_ref.shape, 0.5 / D * jnp.sum(diff * diff), F32)
        dx2 = diff * (1.0 / D)
        dx2_ref[...] = dx2
        dy4 = dx2 * gt2
        vo_ref[0:1, :] += _colsum(dx2 * y4)
        vo_ref[1:2, :] += _colsum(dy4 * n4)
        dff_ref[...] = _rms_bwd(dy4 * gpost, n4, r4).astype(BF16)

    tile = lambda w: pl.BlockSpec((ts, w), lambda i: (i, 0))
    full = lambda r, w: pl.BlockSpec((r, w), lambda i: (0, 0))
    hbm = pl.BlockSpec(memory_space=pl.ANY)
    return pl.pallas_call(
        body, name="fwd_ffn", grid=(s // ts,),
        in_specs=[tile(D), tile(D), full(VD_ROWS, D), full(FV_ROWS, F2), hbm, hbm],
        out_specs=[tile(F2), tile(F2), tile(F), tile(D), tile(D), tile(D), full(8, D), full(8, 128)],
        out_shape=[jax.ShapeDtypeStruct((s, F2), BF16), jax.ShapeDtypeStruct((s, F2), BF16),
                   jax.ShapeDtypeStruct((s, F), BF16),
                   jax.ShapeDtypeStruct((s, D), BF16), jax.ShapeDtypeStruct((s, D), F32),
                   jax.ShapeDtypeStruct((s, D), BF16), jax.ShapeDtypeStruct((8, D), F32),
                   jax.ShapeDtypeStruct((8, 128), F32)],
        scratch_shapes=[pltpu.VMEM((D, F2), BF16), pltpu.VMEM((F, D), BF16), pltpu.VMEM((CONV_CARRY, 8, F2), F32),
                        pltpu.SemaphoreType.DMA((2,))],
        compiler_params=_params(("arbitrary",)),
    )(x1, tgt, vec_d, vec_f, w_up, w_down)


def _bwd_ffn(dff, dx2, x1, up0, upc, vec_d, vec_f, w_up, w_down, exchange, ex_grads, ts):
    s = x1.shape[0]
    nt = s // ts
    n = exchange.n

    def body(*refs):
        ins, grads = refs[:9], refs[9:9 + n]
        outs, recvs = refs[9 + n:13 + n], refs[13 + n:13 + 2 * n]
        scratch, sems = refs[13 + 2 * n:-2], refs[-2:]
        i = pl.program_id(0)
        pl.when(i == 0)(lambda: exchange.start(grads, recvs, *sems))
        compute(*ins, *outs, *scratch)
        pl.when(i == nt - 1)(lambda: exchange.finish(grads, recvs, *sems))

    def compute(dff_ref, dx2_ref, x1_ref, up_ref, upc_ref, v_ref, f_ref, wu_hbm, wd_hbm,
                dx1_ref, dup_ref, vo_ref, fo_ref, wu, wd, carry, sem):
        i = pl.program_id(0)
        _load_once([(wu_hbm, wu), (wd_hbm, wd)], sem)

        @pl.when(i == 0)
        def _():
            carry[...] = jnp.zeros_like(carry)
            vo_ref[...] = jnp.zeros_like(vo_ref)
            fo_ref[...] = jnp.zeros_like(fo_ref)

        dffb = dff_ref[...]

        dh2 = jnp.zeros((ts, D), F32)
        for lo, hi in FFN_SLABS_BWD:
            slabs = (slice(lo, hi), slice(F + lo, F + hi))
            gelu, dgelu = _gelu_and_grad(upc_ref[:, slabs[0]].astype(F32))
            da = _dot_nt(dffb, wd[lo:hi, :])
            dups = (da * upc_ref[:, slabs[1]].astype(F32) * dgelu, da * gelu)
            for cols, dup in zip(slabs, dups):
                du0, d1, d2 = _causal_conv_bwd(dup, carry, cols, f_ref[FV_W0:FV_W0 + 1, cols],
                                               f_ref[FV_W1:FV_W1 + 1, cols], f_ref[FV_W2:FV_W2 + 1, cols])
                u0 = up_ref[:, cols].astype(F32)
                fo_ref[FV_B:FV_B + 1, cols] += _colsum(dup)
                fo_ref[FV_W2:FV_W2 + 1, cols] += _colsum(dup * u0)
                fo_ref[FV_W1:FV_W1 + 1, cols] += _colsum(d1 * u0)
                fo_ref[FV_W0:FV_W0 + 1, cols] += _colsum(d2 * u0)
                du0 = du0.astype(BF16)
                dup_ref[:, cols] = du0
                dh2 = dh2 + _dot_nt(du0, wu[:, cols])

        x1v = x1_ref[...]
        r3 = _rms(x1v)
        n3 = x1v * r3
        gpre = _row(v_ref, V_GPRE2)
        sc = 1.0 + _row(v_ref, V_SC2)
        vo_ref[0:1, :] += _colsum(dh2)
        vo_ref[1:2, :] += _colsum(dh2 * n3 * gpre)
        vo_ref[2:3, :] += _colsum(dh2 * n3 * sc)
        dx1_ref[...] = dx2_ref[...] + _rms_bwd(dh2 * (gpre * sc), n3, r3)

    rev = lambda w: pl.BlockSpec((ts, w), lambda i: (nt - 1 - i, 0))
    full = lambda r, w: pl.BlockSpec((r, w), lambda i: (0, 0))
    hbm = pl.BlockSpec(memory_space=pl.ANY)
    return pl.pallas_call(
        body, name="bwd_ffn", grid=(nt,),
        in_specs=[rev(D), rev(D), rev(D), rev(F2), rev(F2), full(VD_ROWS, D), full(FV_ROWS, F2), hbm, hbm]
        + exchange.specs_any,
        out_specs=[rev(D), rev(F2), full(8, D), full(FV_ROWS, F2)] + exchange.specs_any,
        out_shape=[jax.ShapeDtypeStruct((s, D), F32), jax.ShapeDtypeStruct((s, F2), BF16),
                   jax.ShapeDtypeStruct((8, D), F32), jax.ShapeDtypeStruct((FV_ROWS, F2), F32)] + exchange.out_shape,
        scratch_shapes=[pltpu.VMEM((D, F2), BF16), pltpu.VMEM((F, D), BF16), pltpu.VMEM((CONV_CARRY, 8, F2), F32),
                        pltpu.SemaphoreType.DMA((2,))] + exchange.scratch,
        compiler_params=_params(("arbitrary",)),
    )(dff, dx2, x1, up0, upc, vec_d, vec_f, w_up, w_down, *ex_grads)


def _bwd_mix(dx1, o, proj, cv, ya0, yb, vec_d, w_pool, w_bout, w_o, exchange, ex_grads, ts):
    s = dx1.shape[0]
    nt = s // ts
    n = exchange.n

    def body(*refs):
        ins, grads = refs[:10], refs[10:10 + n]
        outs, recvs = refs[10 + n:15 + n], refs[15 + n:15 + 2 * n]
        scratch, sems = refs[15 + 2 * n:-2], refs[-2:]
        i = pl.program_id(0)
        pl.when(i == 0)(lambda: exchange.start(grads, recvs, *sems))
        compute(*ins, *outs, *scratch)
        pl.when(i == nt - 1)(lambda: exchange.finish(grads, recvs, *sems))

    def compute(dx1_ref, o_ref, p_ref, cv_ref, ya_ref, yb_ref, v_ref, wp_hbm, wb_hbm, wo_hbm,
                dp_ref, do_ref, dyb_ref, dya_ref, vo_ref, wp, wb, wo, carry_d, carry_e, sem):
        i = pl.program_id(0)
        _load_once([(wp_hbm, wp), (wb_hbm, wb), (wo_hbm, wo)], sem)

        @pl.when(i == 0)
        def _():
            carry_d[...] = jnp.zeros_like(carry_d)
            carry_e[...] = jnp.zeros_like(carry_e)
            vo_ref[...] = jnp.zeros_like(vo_ref)

        t0 = (nt - 1 - i) * ts
        dx1v = dx1_ref[...]
        ov = o_ref[...].astype(F32)
        r2 = _rms(ov)
        n2 = ov * r2
        gpost = _row(v_ref, V_GPOST1)
        vo_ref[0:1, :] += _colsum(dx1v * (n2 * gpost))
        dy2 = dx1v * _row(v_ref, V_GT1)
        vo_ref[1:2, :] += _colsum(dy2 * n2)
        dob = _rms_bwd(dy2 * gpost, n2, r2).astype(BF16)
        do_ref[...] = dob
        dmerged = _dot_nt(dob, wo[...])

        ya0 = ya_ref[...].astype(F32)
        pscale = _row(v_ref, V_PSCALE)
        sa = jax.nn.sigmoid(p_ref[:, 4 * D:5 * D].astype(F32))
        dp_ref[:, 4 * D:5 * D] = (dmerged * (ya0 * pscale) * sa * (1.0 - sa)).astype(BF16)
        dy_a = dmerged * sa
        vo_ref[2:3, :] += _colsum(dy_a * ya0)
        dya0 = (dy_a * pscale).astype(BF16)
        dya_ref[...] = dya0

        sb = jax.nn.sigmoid(p_ref[:, 5 * D:6 * D].astype(F32))
        dp_ref[:, 5 * D:6 * D] = (dmerged * yb_ref[...].astype(F32) * sb * (1.0 - sb)).astype(BF16)
        dy_b = (dmerged * sb).astype(BF16)
        dyb_ref[...] = dy_b
        dq = _dot_nt(dy_b, wb[...])

        u_x = p_ref[:, D:2 * D].astype(F32)
        u_b = p_ref[:, 2 * D:3 * D].astype(F32)
        u_c = p_ref[:, 3 * D:4 * D].astype(F32)
        w0, w1, w2 = _row(v_ref, V_CW0), _row(v_ref, V_CW1), _row(v_ref, V_CW2)
        dp_ref[:, 2 * D:3 * D] = (dq * cv_ref[...].astype(F32)).astype(BF16)
        dcv = dq * u_b
        dv, d1, d2 = _causal_conv_bwd(dcv, carry_d, slice(None), w0, w1, w2)
        v = u_c * u_x
        vo_ref[3:4, :] += _colsum(dcv)
        vo_ref[4:5, :] += _colsum(d2 * v)
        vo_ref[5:6, :] += _colsum(d1 * v)
        vo_ref[6:7, :] += _colsum(dcv * v)
        dp_ref[:, D:2 * D] = (dv * u_c).astype(BF16)
        dp_ref[:, 3 * D:4 * D] = (dv * u_x).astype(BF16)

        for g in range(NG):
            cols = slice(g * GW, (g + 1) * GW)
            dpg = _dot_nt(dya0[:, cols], wp[g])
            e = dpg / _pool_counts(t0, g)
            for l in range(g + 1):
                slot = slice((1 << l) - 1, (2 << l) - 1)
                shifted, carry_e[slot, :, cols] = _after(e, carry_e[slot, :, cols], 1 << l)
                e = e + shifted
            dp_ref[:, cols] = (e - dpg).astype(BF16)

    rev = lambda w: pl.BlockSpec((ts, w), lambda i: (nt - 1 - i, 0))
    hbm = pl.BlockSpec(memory_space=pl.ANY)
    return pl.pallas_call(
        body, name="bwd_mix", grid=(nt,),
        in_specs=[rev(D), rev(D), rev(DIN), rev(D), rev(D), rev(D), pl.BlockSpec((VD_ROWS, D), lambda i: (0, 0)),
                  hbm, hbm, hbm] + exchange.specs_any,
        out_specs=[rev(DIN), rev(D), rev(D), rev(D), pl.BlockSpec((8, D), lambda i: (0, 0))] + exchange.specs_any,
        out_shape=[jax.ShapeDtypeStruct((s, DIN), BF16)] + [jax.ShapeDtypeStruct((s, D), BF16)] * 3
        + [jax.ShapeDtypeStruct((8, D), F32)] + exchange.out_shape,
        scratch_shapes=[pltpu.VMEM((NG, GW, GW), BF16), pltpu.VMEM((D, D), BF16), pltpu.VMEM((D, D), BF16),
                        pltpu.VMEM((CONV_CARRY, 8, D), F32), pltpu.VMEM((POOL_CARRY, 8, D), F32),
                        pltpu.SemaphoreType.DMA((3,))] + exchange.scratch,
        compiler_params=_params(("arbitrary",)),
    )(dx1, o, proj, cv, ya0, yb, vec_d, w_pool, w_bout, w_o, *ex_grads)


def _bwd_in(dproj, dx1, x, vec_d, w_in, exchange, ex_grads, ts):
    s = x.shape[0]
    nt = s // ts
    n = exchange.n

    def body(*refs):
        ins, grads = refs[:5], refs[5:5 + n]
        outs, recvs = refs[5 + n:7 + n], refs[7 + n:7 + 2 * n]
        scratch, sems = refs[7 + 2 * n:-2], refs[-2:]
        i = pl.program_id(0)
        pl.when(i == 0)(lambda: exchange.start(grads, recvs, *sems))
        compute(*ins, *outs, *scratch)
        pl.when(i == nt - 1)(lambda: exchange.finish(grads, recvs, *sems))

    def compute(dp_ref, dx1_ref, x_ref, v_ref, w_hbm, dx_ref, vo_ref, w_vmem, sem):
        _load_once([(w_hbm, w_vmem)], sem)

        @pl.when(pl.program_id(0) == 0)
        def _():
            vo_ref[...] = jnp.zeros_like(vo_ref)

        dh1 = _dot_nt(dp_ref[...], w_vmem[...])
        xv = x_ref[...]
        r1 = _rms(xv)
        n1 = xv * r1
        gpre = _row(v_ref, V_GPRE1)
        sc = 1.0 + _row(v_ref, V_SC1)
        vo_ref[0:1, :] += _colsum(dh1)
        vo_ref[1:2, :] += _colsum(dh1 * n1 * gpre)
        vo_ref[2:3, :] += _colsum(dh1 * n1 * sc)
        _store_rows(dx_ref, dx1_ref[...] + _rms_bwd(dh1 * (gpre * sc), n1, r1), ts)

    tile = lambda w: pl.BlockSpec((ts, w), lambda i: (i, 0))
    return pl.pallas_call(
        body, name="bwd_in", grid=(s // ts,),
        in_specs=[tile(DIN), tile(D), tile(D), pl.BlockSpec((VD_ROWS, D), lambda i: (0, 0)),
                  pl.BlockSpec(memory_space=pl.ANY)] + exchange.specs_any,
        out_specs=[tile(D), pl.BlockSpec((8, D), lambda i: (0, 0))] + exchange.specs_any,
        out_shape=[jax.ShapeDtypeStruct((s, D), F32), jax.ShapeDtypeStruct((8, D), F32)] + exchange.out_shape,
        scratch_shapes=[pltpu.VMEM((D, DIN), BF16), pltpu.SemaphoreType.DMA((1,))] + exchange.scratch,
        compiler_params=_params(("arbitrary",)),
    )(dproj, dx1, x, vec_d, w_in, *ex_grads)


def _dot_tn(a, b):
    return lax.dot_general(a, b, (((0,), (0,)), ((), ())), preferred_element_type=F32)


def _wgrad(a, b, tm, tn, ts, name, dtype, exchange=None, ex_grads=()):
    s, m = a.shape
    nn = b.shape[1]
    grid = (m // tm, nn // tn, s // ts)
    n = exchange.n if exchange else 0

    def body(*refs):
        a_ref, b_ref = refs[:2]
        grads = refs[2:2 + n]
        o_ref = refs[2 + n]
        recvs = refs[3 + n:3 + 2 * n]
        acc = refs[3 + 2 * n]
        sems = refs[4 + 2 * n:]
        i, j, k = pl.program_id(0), pl.program_id(1), pl.program_id(2)
        if exchange:
            pl.when((i == 0) & (j == 0) & (k == 0))(lambda: exchange.start(grads, recvs, *sems))
        part = _dot_tn(a_ref[...], b_ref[...])

        @pl.when(k == 0)
        def _():
            acc[...] = part

        @pl.when(k > 0)
        def _():
            acc[...] += part

        @pl.when(k == grid[2] - 1)
        def _():
            o_ref[...] = acc[...].astype(dtype)

        if exchange:
            pl.when((i == grid[0] - 1) & (j == grid[1] - 1) & (k == grid[2] - 1))(
                lambda: exchange.finish(grads, recvs, *sems))

    hosted = exchange.specs_any if exchange else []
    return pl.pallas_call(
        body, name=name, grid=grid,
        in_specs=[pl.BlockSpec((ts, tm), lambda i, j, k: (k, i)), pl.BlockSpec((ts, tn), lambda i, j, k: (k, j))]
        + hosted,
        out_specs=[pl.BlockSpec((tm, tn), lambda i, j, k: (i, j))] + hosted,
        out_shape=[jax.ShapeDtypeStruct((m, nn), dtype)] + (exchange.out_shape if exchange else []),
        scratch_shapes=[pltpu.VMEM((tm, tn), F32)] + (exchange.scratch if exchange else []),
        compiler_params=_params(("arbitrary", "arbitrary", "arbitrary")),
    )(a, b, *ex_grads)


def _wgrad_pool(pg, dya0, ts):
    s = pg.shape[0]
    nk = s // ts

    def body(a_ref, b_ref, o_ref, acc):
        k = pl.program_id(1)
        part = _dot_tn(a_ref[...], b_ref[...])

        @pl.when(k == 0)
        def _():
            acc[...] = part

        @pl.when(k > 0)
        def _():
            acc[...] += part

        @pl.when(k == nk - 1)
        def _():
            o_ref[0] = acc[...].astype(BF16)

    return pl.pallas_call(
        body, name="wgrad_pool", grid=(NG, nk),
        in_specs=[pl.BlockSpec((ts, GW), lambda g, k: (k, g)), pl.BlockSpec((ts, GW), lambda g, k: (k, g))],
        out_specs=pl.BlockSpec((1, GW, GW), lambda g, k: (g, 0, 0)),
        out_shape=jax.ShapeDtypeStruct((NG, GW, GW), BF16),
        scratch_shapes=[pltpu.VMEM((GW, GW), F32)],
        compiler_params=_params(("arbitrary", "arbitrary")),
    )(pg, dya0)


FFN_SLABS_FWD = ((0, 2816),)
FFN_SLABS_BWD = ((0, 1536), (1536, 2816))
TS_PROJ = 512
TS_MIX = 256
TS_FFN = 256
TS_WGRAD = 2048


def _local_step(x, tgt, vec_d, vec_f, placed, place):
    s = x.shape[0]
    tw = min(TS_WGRAD, s)
    sp_in, sp_pool, sp_bout, sp_o, sp_up, sp_down = SHARDED
    proj, h1, xs, w_in, w_pool, w_bout, w_o = _fwd_proj(x, vec_d, placed[0], placed[1:4], place, min(TS_PROJ, s))
    x1, o, pg, q, merged, ya0, yb, cv, w_up, w_down = _fwd_mix(proj, xs, vec_d, w_pool, w_bout, w_o, placed[4:],
                                                               min(TS_MIX, s))
    up0, upc, a, h2, dx2, dff, vo_f, loss = _fwd_ffn(x1, tgt, vec_d, vec_f, w_up, w_down, min(TS_FFN, s))
    g_down, = _wgrad(a, dff, F // 2, D, tw, "wgrad_down", BF16)
    dx1, dup0, vo_b, fo, r_down = _bwd_ffn(dff, dx2, x1, up0, upc, vec_d, vec_f, w_up, w_down,
                                           _GradExchange([sp_down]), [g_down], min(TS_FFN, s))
    g_up, = _wgrad(h2, dup0, D, F2 // NCHIP, tw, "wgrad_up", BF16)
    dproj, do, dyb, dya0, vo_m, r_up = _bwd_mix(dx1, o, proj, cv, ya0, yb, vec_d, w_pool, w_bout, w_o,
                                                _GradExchange([sp_up]), [g_up], min(TS_MIX, s))
    g_o, = _wgrad(merged, do, D, D, tw, "wgrad_o", BF16)
    g_bout, = _wgrad(q, dyb, D, D, tw, "wgrad_bout", BF16)
    g_pool = _wgrad_pool(pg, dya0, tw)
    g_in, r_pool, r_bout, r_o = _wgrad(h1, dproj, D, DIN // NCHIP, tw, "wgrad_in", BF16,
                                       _GradExchange([sp_pool, sp_bout, sp_o]), [g_pool, g_bout, g_o])
    dx, vo_i, r_in = _bwd_in(dproj, dx1, xs, vec_d, w_in, _GradExchange([sp_in]), [g_in], min(TS_PROJ, s))
    vecs = dict(
        dsh1=vo_i[0], dsc1=vo_i[1], dg_pre_mix=vo_i[2],
        dgt1=vo_m[0], dg_post_mix=vo_m[1], dpool_scale=vo_m[2], dconv_b=vo_m[3],
        dconv_w=vo_m[4:7],
        dsh2=vo_b[0], dsc2=vo_b[1], dg_pre_ffn=vo_b[2],
        dgt2=vo_f[0], dg_post_ffn=vo_f[1],
        dffn_conv_w=fo[FV_W0:FV_W2 + 1], dffn_conv_b=fo[FV_B],
    )
    local = dict(w_in=g_in, w_pool=g_pool, w_bout=g_bout, w_o=g_o, w_up=g_up, w_down=g_down)
    received = dict(w_in=r_in, w_pool=r_pool, w_bout=r_bout, w_o=r_o, w_up=r_up, w_down=r_down)
    return loss, dx, vecs, local, received


def _aligned(offset, n):
    return offset if isinstance(offset, int) else pl.multiple_of(offset, n)


class _Sharded:
    def __init__(self, name, full_shape, shard_axis, half_axis):
        self.name = name
        self.full_shape = full_shape
        self.shard_axis = shard_axis
        self.half_axis = half_axis
        self.shard_shape = tuple(n // NCHIP if a == shard_axis else n for a, n in enumerate(full_shape))
        self.piece_shape = tuple(n // 2 if a == half_axis else n for a, n in enumerate(self.shard_shape))

    def piece(self, full_ref, k, h):
        idx = []
        for a, n in enumerate(self.piece_shape):
            if a == self.shard_axis and a == self.half_axis:
                idx.append(pl.ds(_aligned((2 * k + h) * n, n), n))
            elif a == self.shard_axis:
                idx.append(pl.ds(_aligned(k * n, n), n))
            elif a == self.half_axis:
                idx.append(pl.ds(_aligned(h * n, n), n))
            else:
                idx.append(slice(None))
        return full_ref.at[tuple(idx)]

    def shard(self, full_ref, k):
        n = self.shard_shape[self.shard_axis]
        idx = [pl.ds(_aligned(k * n, n), n) if a == self.shard_axis else slice(None)
               for a in range(len(self.full_shape))]
        return full_ref.at[tuple(idx)]

    def half(self, shard_ref, h):
        n = self.piece_shape[self.half_axis]
        idx = [pl.ds(_aligned(h * n, n), n) if a == self.half_axis else slice(None)
               for a in range(len(self.full_shape))]
        return shard_ref.at[tuple(idx)]

    def piece_block(self):
        def index_map(k, c_ref):
            c = c_ref[0]
            out = []
            for a in range(len(self.full_shape)):
                if a == self.shard_axis and a == self.half_axis:
                    out.append(2 * k + c)
                elif a == self.shard_axis:
                    out.append(k)
                elif a == self.half_axis:
                    out.append(c)
                else:
                    out.append(0)
            return tuple(out)
        return pl.BlockSpec(self.piece_shape, index_map)


SHARDED = (
    _Sharded("w_in", (D, DIN), 1, 0),
    _Sharded("w_pool", (NG, GW, GW), 1, 0),
    _Sharded("w_bout", (D, D), 0, 0),
    _Sharded("w_o", (D, D), 0, 0),
    _Sharded("w_up", (D, F2), 1, 0),
    _Sharded("w_down", (F, D), 0, 0),
)
NW = len(SHARDED)


def _mesh_place():
    x, y, c = lax.axis_index("x"), lax.axis_index("y"), lax.axis_index("c")
    chips = [(1 - x, y), (x, 1 - y), (1 - x, 1 - y)]
    return x, y, c, 2 * x + y, chips, [2 * px + py for px, py in chips]


def _remote(src, dst, send_sem, recv_sem, device):
    return pltpu.make_async_remote_copy(src_ref=src, dst_ref=dst, send_sem=send_sem, recv_sem=recv_sem,
                                        device_id=device, device_id_type=MESH)


def _all_gather_small(block, name):
    m_per, n = block.shape

    def body(x_ref, out_ref, send_sems, recv_sems, local_sem):
        x, y, c, _, chips, _ = _mesh_place()
        me, sibling = (x, y, c), (x, y, 1 - c)

        def rows(px, py, pc):
            return out_ref.at[pl.ds((4 * px + 2 * py + pc) * m_per, m_per), :]

        def copy(k, blk, to, src=None):
            return _remote(rows(*blk) if src is None else src, rows(*blk), send_sems.at[k], recv_sems.at[k], to)

        mine = pltpu.make_async_copy(x_ref, rows(*me), local_sem)
        mine.start()
        first = [copy(0, me, sibling, src=x_ref)]
        first += [copy(1 + j, me, (*chip, c), src=x_ref) for j, chip in enumerate(chips)]
        for cp in first:
            cp.start()
        passed = [copy(4 + j, (*chip, c), sibling) for j, chip in enumerate(chips)]
        for j, chip in enumerate(chips):
            copy(1 + j, (*chip, c), me).wait_recv()
            passed[j].start()
        copy(0, sibling, me).wait_recv()
        for j, chip in enumerate(chips):
            copy(4 + j, (*chip, 1 - c), me).wait_recv()
        for cp in first + passed:
            cp.wait_send()
        mine.wait()

    return pl.pallas_call(
        body, name=name,
        out_shape=jax.ShapeDtypeStruct((NDEV * m_per, n), block.dtype),
        in_specs=[pl.BlockSpec(memory_space=pltpu.VMEM)],
        out_specs=pl.BlockSpec(memory_space=pltpu.VMEM),
        scratch_shapes=[pltpu.SemaphoreType.DMA((7,)), pltpu.SemaphoreType.DMA((7,)), pltpu.SemaphoreType.DMA],
        compiler_params=pltpu.CompilerParams(vmem_limit_bytes=VMEM_LIMIT),
    )(block)


class _WeightGather:
    def __init__(self, specs):
        self.specs = specs
        self.n = len(specs)
        self.specs_any = [pl.BlockSpec(memory_space=pl.ANY)] * self.n
        self.out_shape = [jax.ShapeDtypeStruct(sp.full_shape, BF16) for sp in specs]
        self.scratch = [pltpu.SemaphoreType.DMA((6 * self.n,)), pltpu.SemaphoreType.DMA((6 * self.n,))]

    def _sends(self, outs, send_sems, recv_sems):
        x, y, c, k_me, chips, _ = _mesh_place()
        sends = []
        for j, chip in enumerate(chips):
            for w, sp in enumerate(self.specs):
                mine = sp.piece(outs[w], k_me, c)
                sends.append(_remote(mine, mine, send_sems.at[6 * w + j], recv_sems.at[6 * w + j], (*chip, c)))
        return sends

    def start(self, outs, send_sems, recv_sems):
        for cp in self._sends(outs, send_sems, recv_sems):
            cp.start()

    def _passes(self, outs, send_sems, recv_sems):
        x, y, c, _, chips, kidx = _mesh_place()
        return [_remote(sp.piece(outs[w], kidx[j], c), sp.piece(outs[w], kidx[j], c),
                        send_sems.at[6 * w + 3 + j], recv_sems.at[6 * w + 3 + j], (x, y, 1 - c))
                for j in range(3) for w, sp in enumerate(self.specs)]

    def forward(self, outs, send_sems, recv_sems):
        x, y, c, _, chips, kidx = _mesh_place()
        for j, chip in enumerate(chips):
            for w, sp in enumerate(self.specs):
                landed = sp.piece(outs[w], kidx[j], c)
                _remote(landed, landed, send_sems.at[6 * w + j], recv_sems.at[6 * w + j], (*chip, c)).wait_recv()
        for cp in self._passes(outs, send_sems, recv_sems):
            cp.start()

    def drain(self, outs, send_sems, recv_sems):
        x, y, c, _, chips, kidx = _mesh_place()
        for j in range(3):
            for w, sp in enumerate(self.specs):
                landed = sp.piece(outs[w], kidx[j], 1 - c)
                _remote(landed, landed, send_sems.at[6 * w + 3 + j], recv_sems.at[6 * w + 3 + j],
                        (x, y, 1 - c)).wait_recv()
        for cp in self._sends(outs, send_sems, recv_sems) + self._passes(outs, send_sems, recv_sems):
            cp.wait_send()

    def finish(self, outs, send_sems, recv_sems):
        self.forward(outs, send_sems, recv_sems)
        self.drain(outs, send_sems, recv_sems)


def _gather_weights(placed, specs, name):
    gather = _WeightGather(specs)
    n = gather.n

    def body(*refs):
        outs, sems = refs[n:2 * n], refs[2 * n:]
        gather.start(outs, *sems)
        gather.finish(outs, *sems)

    return pl.pallas_call(
        body, name=name, out_shape=gather.out_shape, in_specs=gather.specs_any, out_specs=gather.specs_any,
        input_output_aliases={w: w for w in range(n)}, scratch_shapes=gather.scratch,
    )(*placed)


class _GradExchange:
    def __init__(self, specs):
        self.specs = specs
        self.n = len(specs)
        self.specs_any = [pl.BlockSpec(memory_space=pl.ANY)] * self.n
        self.out_shape = [jax.ShapeDtypeStruct((NDEV,) + sp.piece_shape, BF16) for sp in specs]
        self.scratch = [pltpu.SemaphoreType.DMA((7 * self.n,)), pltpu.SemaphoreType.DMA((NDEV * self.n,))]

    def _sends(self, grads, recvs, send_sems, recv_sems):
        x, y, c, k_me, chips, kidx = _mesh_place()
        dev = 2 * k_me + c
        sends = []
        for w, sp in enumerate(self.specs):
            slot, arrival = recvs[w].at[dev], recv_sems.at[NDEV * w + dev]
            sends.append(_remote(sp.piece(grads[w], k_me, 1 - c), slot, send_sems.at[7 * w], arrival, (x, y, 1 - c)))
            for j, chip in enumerate(chips):
                for h in range(2):
                    sends.append(_remote(sp.piece(grads[w], kidx[j], h), slot, send_sems.at[7 * w + 1 + 2 * j + h],
                                         arrival, (*chip, h)))
        return sends

    def start(self, grads, recvs, send_sems, recv_sems):
        for cp in self._sends(grads, recvs, send_sems, recv_sems):
            cp.start()

    def finish(self, grads, recvs, send_sems, recv_sems):
        x, y, c, k_me, _, _ = _mesh_place()
        dev = 2 * k_me + c
        for w in range(self.n):
            for d in range(NDEV):
                landed = recvs[w].at[d]
                arrival = _remote(landed, landed, send_sems.at[7 * w], recv_sems.at[NDEV * w + d], (x, y, c))
                pl.when(d != dev)(arrival.wait_recv)
        for cp in self._sends(grads, recvs, send_sems, recv_sems):
            cp.wait_send()


def _device_sum(sp, local, recv, place):
    nd = len(sp.piece_shape)

    def body(p_ref, a_ref, b_ref, o_ref):
        d = pl.program_id(0)
        term = jnp.where(d == p_ref[2], a_ref[...], b_ref[...]).astype(F32)

        @pl.when(d == 0)
        def _():
            o_ref[...] = term

        @pl.when(d > 0)
        def _():
            o_ref[...] += term

    def mine(d, p_ref):
        return tuple(2 * p_ref[0] + p_ref[1] if a == sp.shard_axis == sp.half_axis else
                     p_ref[0] if a == sp.shard_axis else p_ref[1] if a == sp.half_axis else 0 for a in range(nd))

    def others(d, p_ref):
        return (jnp.where(d == p_ref[2], (d + 1) % NDEV, d),) + (0,) * nd

    return pl.pallas_call(
        body, name="rs_device_sum_" + sp.name,
        grid_spec=pltpu.PrefetchScalarGridSpec(
            num_scalar_prefetch=1, grid=(NDEV,),
            in_specs=[pl.BlockSpec(sp.piece_shape, mine), pl.BlockSpec((None,) + sp.piece_shape, others)],
            out_specs=pl.BlockSpec(sp.piece_shape,
                                   lambda d, p_ref: tuple(p_ref[1] if a == sp.half_axis else 0 for a in range(nd)))),
        out_shape=jax.ShapeDtypeStruct(sp.shard_shape, F32),
        compiler_params=_params(("arbitrary",)),
    )(place, local, recv)


def _pair_exchange(grads, specs):
    n = len(specs)

    def body(*refs):
        ins, outs = refs[:n], refs[n:2 * n]
        send_sems, recv_sems = refs[2 * n:]
        x, y, c, _, _, _ = _mesh_place()
        sibling = (x, y, 1 - c)
        sent = []
        for w, sp in enumerate(specs):
            for k in range(NCHIP):
                cp = _remote(sp.piece(ins[w], k, 1 - c), outs[w].at[k],
                             send_sems.at[NCHIP * w + k], recv_sems.at[NCHIP * w + k], sibling)
                cp.start()
                sent.append(cp)
        for cp in sent:
            cp.wait_recv()
        for cp in sent:
            cp.wait_send()

    hbm = pl.BlockSpec(memory_space=pl.ANY)
    return pl.pallas_call(
        body, name="rs_pair_exchange",
        out_shape=[jax.ShapeDtypeStruct((NCHIP,) + sp.piece_shape, F32) for sp in specs],
        in_specs=[hbm] * n, out_specs=[hbm] * n,
        scratch_shapes=[pltpu.SemaphoreType.DMA((NCHIP * n,)), pltpu.SemaphoreType.DMA((NCHIP * n,))],
    )(*grads)


def _pair_sum(sp, grad, recv, core):
    nd = len(sp.piece_shape)

    def body(c_ref, g_ref, r_ref, o_ref):
        o_ref[...] = (g_ref[...] + r_ref[...]).astype(BF16)

    slot = pl.BlockSpec((None,) + sp.piece_shape, lambda k, c_ref: (k,) + (0,) * nd)
    return pl.pallas_call(
        body, name="rs_pair_sum_" + sp.name,
        grid_spec=pltpu.PrefetchScalarGridSpec(
            num_scalar_prefetch=1, grid=(NCHIP,),
            in_specs=[sp.piece_block(), slot], out_specs=slot),
        out_shape=jax.ShapeDtypeStruct((NCHIP,) + sp.piece_shape, BF16),
        compiler_params=_params(("parallel",)),
    )(core, grad, recv)


def _chip_exchange(parts, specs):
    n = len(specs)

    def body(*refs):
        ins, outs = refs[:n], refs[n:2 * n]
        send_sems, recv_sems = refs[2 * n:]
        x, y, c, k_me, chips, kidx = _mesh_place()
        sent = []
        for j, chip in enumerate(chips):
            for w in range(n):
                cp = _remote(ins[w].at[kidx[j]], outs[w].at[k_me], send_sems.at[3 * w + j], recv_sems.at[3 * w + j],
                             (*chip, c))
                cp.start()
                sent.append(cp)
        for j, chip in enumerate(chips):
            for w in range(n):
                landed = outs[w].at[kidx[j]]
                _remote(landed, landed, send_sems.at[3 * w + j], recv_sems.at[3 * w + j], (*chip, c)).wait_recv()
        for cp in sent:
            cp.wait_send()

    hbm = pl.BlockSpec(memory_space=pl.ANY)
    return pl.pallas_call(
        body, name="rs_chip_exchange",
        out_shape=[jax.ShapeDtypeStruct((NCHIP,) + sp.piece_shape, BF16) for sp in specs],
        in_specs=[hbm] * n, out_specs=[hbm] * n,
        scratch_shapes=[pltpu.SemaphoreType.DMA((3 * n,)), pltpu.SemaphoreType.DMA((3 * n,))],
    )(*parts)


def _chip_sum(sp, parts, recv, place):
    nd = len(sp.piece_shape)

    def body(p_ref, a_ref, b_ref, o_ref):
        k = pl.program_id(0)
        term = jnp.where(k == p_ref[0], a_ref[...], b_ref[...]).astype(F32)

        @pl.when(k == 0)
        def _():
            o_ref[...] = term

        @pl.when(k > 0)
        def _():
            o_ref[...] += term

    def others(k, p_ref):
        return (jnp.where(k == p_ref[0], (k + 1) % NCHIP, k),) + (0,) * nd

    return pl.pallas_call(
        body, name="rs_chip_sum_" + sp.name,
        grid_spec=pltpu.PrefetchScalarGridSpec(
            num_scalar_prefetch=1, grid=(NCHIP,),
            in_specs=[pl.BlockSpec((None,) + sp.piece_shape, lambda k, p_ref: (p_ref[0],) + (0,) * nd),
                      pl.BlockSpec((None,) + sp.piece_shape, others)],
            out_specs=pl.BlockSpec(sp.piece_shape,
                                   lambda k, p_ref: tuple(p_ref[1] if a == sp.half_axis else 0 for a in range(nd)))),
        out_shape=jax.ShapeDtypeStruct(sp.shard_shape, F32),
        compiler_params=_params(("arbitrary",)),
    )(place, parts, recv)


def _pair_share(halves):
    def body(*refs):
        outs = refs[NW:2 * NW]
        send_sems, recv_sems = refs[2 * NW:]
        x, y, c, _, _, _ = _mesh_place()
        sibling = (x, y, 1 - c)
        sent = []
        for w, sp in enumerate(SHARDED):
            mine = sp.half(outs[w], c)
            cp = _remote(mine, mine, send_sems.at[w], recv_sems.at[w], sibling)
            cp.start()
            sent.append(cp)
        for w, sp in enumerate(SHARDED):
            landed = sp.half(outs[w], 1 - c)
            _remote(landed, landed, send_sems.at[w], recv_sems.at[w], sibling).wait_recv()
        for cp in sent:
            cp.wait_send()

    hbm = pl.BlockSpec(memory_space=pl.ANY)
    return pl.pallas_call(
        body, name="rs_pair_share",
        out_shape=[jax.ShapeDtypeStruct(sp.shard_shape, F32) for sp in SHARDED],
        in_specs=[hbm] * NW, out_specs=[hbm] * NW,
        input_output_aliases={w: w for w in range(NW)},
        scratch_shapes=[pltpu.SemaphoreType.DMA((NW,)), pltpu.SemaphoreType.DMA((NW,))],
    )(*halves)


def _reduce_scatter(local, received, place):
    return _pair_share([_device_sum(sp, local[sp.name], received[sp.name], place) for sp in SHARDED])


def _place_bf16(sp, w, place):
    nd = len(sp.full_shape)

    def body(p_ref, w_ref, o_ref):
        o_ref[...] = w_ref[...].astype(BF16)

    return pl.pallas_call(
        body, name="place_" + sp.name,
        grid_spec=pltpu.PrefetchScalarGridSpec(
            num_scalar_prefetch=1, grid=(1,),
            in_specs=[pl.BlockSpec(sp.shard_shape, lambda i, p_ref: (0,) * nd)],
            out_specs=pl.BlockSpec(sp.shard_shape,
                                   lambda i, p_ref: tuple(p_ref[0] if a == sp.shard_axis else 0 for a in range(nd)))),
        out_shape=jax.ShapeDtypeStruct(sp.full_shape, BF16),
        compiler_params=_params(("arbitrary",)),
    )(place, w)


def _matmul_f32(a, b, name):
    def body(a_ref, b_ref, o_ref):
        o_ref[...] = jnp.dot(a_ref[...], b_ref[...], preferred_element_type=F32, precision=lax.Precision.HIGHEST)

    return pl.pallas_call(body, name=name, out_shape=jax.ShapeDtypeStruct((a.shape[0], b.shape[1]), F32),
                          compiler_params=pltpu.CompilerParams(vmem_limit_bytes=VMEM_LIMIT))(a, b)


def _sum_devices(stacked):
    def body(x_ref, o_ref):
        acc = x_ref[0]
        for d in range(1, NDEV):
            acc = acc + x_ref[d]
        o_ref[...] = acc

    return pl.pallas_call(body, name="sum_devices", out_shape=jax.ShapeDtypeStruct(stacked.shape[1:], F32),
                          compiler_params=pltpu.CompilerParams(vmem_limit_bytes=VMEM_LIMIT))(stacked)


def _adamw(w, g, m, v, name):
    r, cdim = w.shape
    tr = r if r <= 256 else (256 if r % 256 == 0 else r // 2)

    def body(w_ref, g_ref, m_ref, v_ref, d_ref, nm_ref, nv_ref):
        gv = g_ref[...]
        nm = ADAM_B1 * m_ref[...] + (1.0 - ADAM_B1) * gv
        nv = ADAM_B2 * v_ref[...] + (1.0 - ADAM_B2) * (gv * gv)
        m_hat = nm / (1.0 - ADAM_B1 ** ADAM_STEP)
        v_hat = nv / (1.0 - ADAM_B2 ** ADAM_STEP)
        d_ref[...] = -ADAM_LR * (m_hat / (jnp.sqrt(v_hat) + ADAM_EPS) + ADAM_WD * w_ref[...])
        nm_ref[...] = nm
        nv_ref[...] = nv

    blk = pl.BlockSpec((tr, cdim), lambda i: (i, 0))
    return pl.pallas_call(
        body, name="adamw_" + name, grid=(r // tr,), in_specs=[blk] * 4, out_specs=[blk] * 3,
        out_shape=[jax.ShapeDtypeStruct(w.shape, F32)] * 3,
        compiler_params=_params(("parallel",)),
    )(w, g, m, v)


WEIGHT_NAMES = ("g_pre_mix", "g_post_mix", "g_pre_ffn", "g_post_ffn", "w_ada", "b_ada", "w_in", "w_pool",
                "pool_scale", "conv_w", "conv_b", "w_bout", "w_o", "w_up", "ffn_conv_w", "ffn_conv_b", "w_down")
MATRIX_NAMES = ("w_ada",) + tuple(sp.name for sp in SHARDED)
VECTOR_NAMES = tuple(n for n in WEIGHT_NAMES if n not in MATRIX_NAMES)

CW = D // NCHIP
FCW = F2 // NCHIP
ADA_W = DIN // NCHIP
COND_BLOCK = (8, 768)
GRAD_BLOCK = (8, 4864)


def _flat_pad(parts, shape):
    flat = jnp.concatenate([p.reshape(-1) for p in parts])
    return jnp.pad(flat, (0, shape[0] * shape[1] - flat.shape[0])).reshape(shape)


def _take(flat, offset, shape):
    size = 1
    for n in shape:
        size *= n
    return flat[offset:offset + size].reshape(shape), offset + size


def kernel(x, c, g_pre_mix, g_post_mix, g_pre_ffn, g_post_ffn, w_ada, b_ada, w_in, w_pool, pool_scale, conv_w, conv_b, w_bout, w_o, w_up, ffn_conv_w, ffn_conv_b, w_down, loss_target, m_g_pre_mix, m_g_post_mix, m_g_pre_ffn, m_g_post_ffn, m_w_ada, m_b_ada, m_w_in, m_w_pool, m_pool_scale, m_conv_w, m_conv_b, m_w_bout, m_w_o, m_w_up, m_ffn_conv_w, m_ffn_conv_b, m_w_down, v_g_pre_mix, v_g_post_mix, v_g_pre_ffn, v_g_post_ffn, v_w_ada, v_b_ada, v_w_in, v_w_pool, v_pool_scale, v_conv_w, v_conv_b, v_w_bout, v_w_o, v_w_up, v_ffn_conv_w, v_ffn_conv_b, v_w_down):
    weights = dict(g_pre_mix=g_pre_mix, g_post_mix=g_post_mix, g_pre_ffn=g_pre_ffn, g_post_ffn=g_post_ffn,
                   w_ada=w_ada, b_ada=b_ada, w_in=w_in, w_pool=w_pool, pool_scale=pool_scale, conv_w=conv_w,
                   conv_b=conv_b, w_bout=w_bout, w_o=w_o, w_up=w_up, ffn_conv_w=ffn_conv_w, ffn_conv_b=ffn_conv_b,
                   w_down=w_down)
    mom1 = dict(g_pre_mix=m_g_pre_mix, g_post_mix=m_g_post_mix, g_pre_ffn=m_g_pre_ffn, g_post_ffn=m_g_post_ffn,
                w_ada=m_w_ada, b_ada=m_b_ada, w_in=m_w_in, w_pool=m_w_pool, pool_scale=m_pool_scale,
                conv_w=m_conv_w, conv_b=m_conv_b, w_bout=m_w_bout, w_o=m_w_o, w_up=m_w_up,
                ffn_conv_w=m_ffn_conv_w, ffn_conv_b=m_ffn_conv_b, w_down=m_w_down)
    mom2 = dict(g_pre_mix=v_g_pre_mix, g_post_mix=v_g_post_mix, g_pre_ffn=v_g_pre_ffn, g_post_ffn=v_g_post_ffn,
                w_ada=v_w_ada, b_ada=v_b_ada, w_in=v_w_in, w_pool=v_w_pool, pool_scale=v_pool_scale,
                conv_w=v_conv_w, conv_b=v_conv_b, w_bout=v_w_bout, w_o=v_w_o, w_up=v_w_up,
                ffn_conv_w=v_ffn_conv_w, ffn_conv_b=v_ffn_conv_b, w_down=v_w_down)

    chip = 2 * lax.axis_index("x") + lax.axis_index("y")
    core = lax.axis_index("c")
    dev = 2 * chip + core
    place = jnp.stack([chip, core, dev]).astype(jnp.int32)

    cond = _all_gather_small(_flat_pad([c, conv_w, ffn_conv_w], COND_BLOCK), "gather_cond")
    cond = cond.reshape(NDEV, -1)
    c_all = cond[:, :D]
    by_chip = cond[0::2]
    conv_w_full = by_chip[:, D:D + 3 * CW].reshape(NCHIP, 3, CW).transpose(1, 0, 2).reshape(3, D)
    ffn_w_full = by_chip[:, D + 3 * CW:D + 3 * CW + 3 * FCW].reshape(NCHIP, 3, FCW).transpose(1, 0, 2).reshape(3, F2)

    mod_cols = _all_gather_small(_matmul_f32(c_all, w_ada[0], "ada_mod"), "gather_mod")
    mod_cols = mod_cols.reshape(NDEV, NDEV, ADA_W)[0::2]
    mod = lax.dynamic_index_in_dim(mod_cols, dev, axis=1, keepdims=False).reshape(6, D) + b_ada.reshape(6, D)
    vec_d = jnp.concatenate([mod, g_pre_mix, g_post_mix, g_pre_ffn, g_post_ffn, pool_scale, conv_b, conv_w_full,
                             jnp.zeros((VD_ROWS - 15, D), F32)], axis=0)
    vec_f = jnp.concatenate([ffn_w_full, ffn_conv_b, jnp.zeros((FV_ROWS - 4, F2), F32)], axis=0)

    placed = [_place_bf16(sp, weights[sp.name][0], place) for sp in SHARDED]
    loss_blk, dx, vecs, local, received = _local_step(x[0], loss_target[0], vec_d, vec_f, placed, place)

    dmod = [vecs[n] for n in ("dsh1", "dsc1", "dgt1", "dsh2", "dsc2", "dgt2")]
    small = [vecs["dg_pre_mix"], vecs["dg_post_mix"], vecs["dg_pre_ffn"], vecs["dg_post_ffn"]] + dmod + [
        vecs["dpool_scale"], vecs["dconv_w"], vecs["dconv_b"], vecs["dffn_conv_w"], vecs["dffn_conv_b"],
        loss_blk[0]]
    gathered = _all_gather_small(_flat_pad(small, GRAD_BLOCK), "gather_vector_grads")
    total = _sum_devices(gathered.reshape((NDEV,) + GRAD_BLOCK)).reshape(-1)
    vgrad = {}
    off = 0
    for n in ("g_pre_mix", "g_post_mix", "g_pre_ffn", "g_post_ffn"):
        vgrad[n], off = _take(total, off, (1, D))
    dmod_off = off
    vgrad["b_ada"], off = _take(total, off, (1, DIN))
    vgrad["pool_scale"], off = _take(total, off, (1, D))
    g_conv_w, off = _take(total, off, (3, D))
    vgrad["conv_w"] = lax.dynamic_slice_in_dim(g_conv_w, chip * CW, CW, axis=1)[None]
    vgrad["conv_b"], off = _take(total, off, (1, D))
    g_ffn_w, off = _take(total, off, (3, F2))
    vgrad["ffn_conv_w"] = lax.dynamic_slice_in_dim(g_ffn_w, chip * FCW, FCW, axis=1)[None]
    vgrad["ffn_conv_b"], off = _take(total, off, (1, F2))
    loss = total[off]

    dmod_all = gathered.reshape(NDEV, -1)[:, dmod_off:dmod_off + DIN]
    dmod_cols = lax.dynamic_slice_in_dim(dmod_all, chip * ADA_W, ADA_W, axis=1)
    g_ada = _matmul_f32(jnp.pad(c_all.T, ((0, 0), (0, 128 - NDEV))), jnp.pad(dmod_cols, ((0, 128 - NDEV), (0, 0))),
                        "ada_wgrad")

    reduced = _reduce_scatter(local, received, place)
    mgrad = {"w_ada": g_ada}
    for sp, g in zip(SHARDED, reduced):
        mgrad[sp.name] = g

    grad, delta, new_m, new_v = {}, {}, {}, {}
    for n in MATRIX_NAMES:
        shape = weights[n].shape
        two_d = (-1, shape[-1])
        d, nm, nv = _adamw(weights[n].reshape(two_d), mgrad[n].reshape(two_d), mom1[n].reshape(two_d),
                           mom2[n].reshape(two_d), n)
        grad[n], delta[n], new_m[n], new_v[n] = (a.reshape(shape) for a in (mgrad[n], d, nm, nv))
    flat = lambda tree: jnp.concatenate([tree[n].reshape(1, -1) for n in VECTOR_NAMES], axis=1)
    d, nm, nv = _adamw(flat(weights), flat(vgrad), flat(mom1), flat(mom2), "vectors")
    off = 0
    for n in VECTOR_NAMES:
        shape = weights[n].shape
        grad[n] = vgrad[n].reshape(shape)
        delta[n], _ = _take(d[0], off, shape)
        new_m[n], _ = _take(nm[0], off, shape)
        new_v[n], off = _take(nv[0], off, shape)

    return (loss, dx[None], *[grad[n] for n in WEIGHT_NAMES], *[delta[n] for n in WEIGHT_NAMES],
            *[new_m[n] for n in WEIGHT_NAMES], *[new_v[n] for n in WEIGHT_NAMES])
```

```python
import jax
import jax.numpy as jnp
from jax import lax
from jax.experimental import pallas as pl
from jax.experimental.pallas import tpu as pltpu

F32 = jnp.float32
BF16 = jnp.bfloat16

D = 1024
DIN = 6 * D
F = 2816
F2 = 2 * F
NG = 4
GW = D // NG
POOL_CARRY = 16
CONV_CARRY = 3
EPS = 1e-6
NCHIP = 4
NDEV = 8

ADAM_LR = 0.001
ADAM_B1 = 0.9
ADAM_B2 = 0.999
ADAM_EPS = 1e-08
ADAM_WD = 0.01
ADAM_STEP = 10

VMEM_LIMIT = 60 * 1024 * 1024

(V_SH1, V_SC1, V_GT1, V_SH2, V_SC2, V_GT2, V_GPRE1, V_GPOST1, V_GPRE2, V_GPOST2,
 V_PSCALE, V_CB, V_CW0, V_CW1, V_CW2) = range(15)
VD_ROWS = 16
FV_W0, FV_W1, FV_W2, FV_B = range(4)
FV_ROWS = 8

MESH = pl.DeviceIdType.MESH


def _params(sem=None, vmem=VMEM_LIMIT):
    return pltpu.CompilerParams(dimension_semantics=sem, vmem_limit_bytes=vmem)


def _row(ref, r):
    return ref[r:r + 1, :]


def _load_once(pairs, sem):
    @pl.when(pl.program_id(0) == 0)
    def _():
        copies = [pltpu.make_async_copy(src, dst, sem.at[n]) for n, (src, dst) in enumerate(pairs)]
        for cp in copies:
            cp.start()
        for cp in copies:
            cp.wait()


def _dot(a, b):
    return jnp.dot(a, b, preferred_element_type=F32)


def _dot_nt(a, b):
    return lax.dot_general(a, b, (((1,), (1,)), ((), ())), preferred_element_type=F32)


BLK = 256
SEG = BLK // 8


def _load_rows(ref, ts):
    blocks = [jnp.swapaxes(ref[b * BLK:(b + 1) * BLK, :].reshape(8, SEG, ref.shape[-1]), 0, 1).reshape(BLK, -1)
              for b in range(ts // BLK)]
    return jnp.concatenate(blocks, axis=0)


def _store_rows(ref, val, ts):
    for b in range(ts // BLK):
        blk = val[b * BLK:(b + 1) * BLK, :].reshape(SEG, 8, val.shape[-1])
        ref[b * BLK:(b + 1) * BLK, :] = jnp.swapaxes(blk, 0, 1).reshape(BLK, -1)


def _times(t0):
    p = lax.broadcasted_iota(jnp.int32, (BLK, 1), 0)
    return t0 + (p & 7) * SEG + (p >> 3)


def _before(x, carry, s):
    x3 = x.reshape(SEG, 8, x.shape[-1])
    tail = pltpu.roll(x3[SEG - s:], 1, 1)
    row = lax.broadcasted_iota(jnp.int32, tail.shape, 1)
    out = jnp.concatenate([jnp.where(row == 0, carry, tail), x3[:SEG - s]], axis=0)
    return out.reshape(x.shape), tail


def _after(x, carry, s):
    x3 = x.reshape(SEG, 8, x.shape[-1])
    head = pltpu.roll(x3[:s], 7, 1)
    row = lax.broadcasted_iota(jnp.int32, head.shape, 1)
    out = jnp.concatenate([x3[s:], jnp.where(row == 7, carry, head)], axis=0)
    return out.reshape(x.shape), head


def _causal_conv(x, carry, cols, w0, w1, w2, b):
    x1, carry[0:1, :, cols] = _before(x, carry[0:1, :, cols], 1)
    x2, carry[1:3, :, cols] = _before(x, carry[1:3, :, cols], 2)
    return b + w2 * x + w1 * x1 + w0 * x2


def _causal_conv_bwd(dy, carry, cols, w0, w1, w2):
    d1, carry[0:1, :, cols] = _after(dy, carry[0:1, :, cols], 1)
    d2, carry[1:3, :, cols] = _after(dy, carry[1:3, :, cols], 2)
    return w2 * dy + w1 * d1 + w0 * d2, d1, d2


def _pool_counts(t0, g):
    return jnp.minimum((_times(t0) + 1).astype(F32), float(2 << g))


def _rms(x):
    return lax.rsqrt(jnp.mean(x * x, axis=-1, keepdims=True) + EPS)


def _rms_bwd(dn, n, r):
    return r * (dn - n * jnp.mean(dn * n, axis=-1, keepdims=True))


def _colsum(x):
    return jnp.sum(x, axis=0, keepdims=True)


def _gelu_and_grad(x):
    k, a = 0.7978845608028654, 0.044715
    x2 = x * x
    th1 = 1.0 + jnp.tanh(x * (x2 * (k * a) + k))
    hx = 0.5 * x
    gelu = hx * th1
    dgelu = 0.5 * th1 + (hx * (th1 * (2.0 - th1))) * (x2 * (3.0 * k * a) + k)
    return gelu, dgelu


def _fwd_proj(x, vec_d, placed_in, placed_rest, place, ts):
    s = x.shape[0]
    nt = s // ts
    cw = DIN // NCHIP
    sp_in = SHARDED[0]
    gather = _WeightGather(SHARDED[1:4])
    n = gather.n

    def body(*refs):
        p_ref, x_ref, v_ref = refs[:3]
        proj_ref, h1_ref, xs_ref, w_full = refs[4 + n:8 + n]
        rest = refs[8 + n:8 + 2 * n]
        w_vmem, h1_all, sem, in_send, in_recv, send_sems, recv_sems = refs[8 + 2 * n:]
        j, i = pl.program_id(0), pl.program_id(1)
        x_, y_, c, k_me, _, _ = _mesh_place()
        sibling = (x_, y_, 1 - c)

        def peer(t):
            return (x_ ^ (t >> 1), y_ ^ (t & 1))

        def w_in_sends():
            mine = sp_in.piece(w_full, k_me, c)
            return [_remote(mine, mine, in_send.at[t - 1], in_recv.at[t - 1], (*peer(t), c)) for t in (1, 2, 3)]

        def load_block(k):
            cp = pltpu.make_async_copy(sp_in.shard(w_full, k), w_vmem.at[k], sem.at[0])
            cp.start()
            cp.wait()

        @pl.when((j == 0) & (i == 0))
        def _():
            for cp in w_in_sends()[:2]:
                cp.start()
            load_block(k_me)

        @pl.when((j == 1) & (i == 0))
        def _():
            for cp in w_in_sends()[:2]:
                cp.wait_send()
            w_in_sends()[2].start()
            gather.start(rest, send_sems, recv_sems)

        for t in (1, 2, 3):
            @pl.when((j == t) & (i == 0))
            def _(t=t):
                k = k_me ^ t
                landed = sp_in.piece(w_full, k, c)
                _remote(landed, landed, in_send.at[t - 1], in_recv.at[t - 1], (*peer(t), c)).wait_recv()
                _remote(landed, landed, in_send.at[2 + t], in_recv.at[2 + t], sibling).start()
                other = sp_in.piece(w_full, k, 1 - c)
                _remote(other, other, in_send.at[2 + t], in_recv.at[2 + t], sibling).wait_recv()
                load_block(k)

        @pl.when(j == 0)
        def _():
            xv = _load_rows(x_ref, ts)
            xs_ref[...] = xv
            n1 = xv * _rms(xv)
            h = n1 * (_row(v_ref, V_GPRE1) * (1.0 + _row(v_ref, V_SC1))) + _row(v_ref, V_SH1)
            hb = h.astype(BF16)
            h1_ref[...] = hb
            h1_all[i] = hb

        proj_ref[...] = _dot(h1_all[i], w_vmem[k_me ^ j]).astype(BF16)

        @pl.when((j == NCHIP - 1) & (i == nt - 1))
        def _():
            w_in_sends()[2].wait_send()
            for t in (1, 2, 3):
                landed = sp_in.piece(w_full, k_me ^ t, c)
                _remote(landed, landed, in_send.at[2 + t], in_recv.at[2 + t], sibling).wait_send()
            gather.finish(rest, send_sems, recv_sems)

    once = lambda w: pl.BlockSpec((ts, w), lambda j, i, p: (jnp.where(j == 0, i, nt - 1), 0))
    return pl.pallas_call(
        body, name="fwd_proj",
        grid_spec=pltpu.PrefetchScalarGridSpec(
            num_scalar_prefetch=1, grid=(NCHIP, nt),
            in_specs=[once(D), pl.BlockSpec((VD_ROWS, D), lambda j, i, p: (0, 0)),
                      pl.BlockSpec(memory_space=pl.ANY)] + gather.specs_any,
            out_specs=[pl.BlockSpec((ts, cw), lambda j, i, p: (i, p[0] ^ j)), once(D), once(D),
                       pl.BlockSpec(memory_space=pl.ANY)] + gather.specs_any,
            scratch_shapes=[pltpu.VMEM((NCHIP, D, cw), BF16), pltpu.VMEM((nt, ts, D), BF16),
                            pltpu.SemaphoreType.DMA((1,)),
                            pltpu.SemaphoreType.DMA((6,)), pltpu.SemaphoreType.DMA((6,))] + gather.scratch),
        out_shape=[jax.ShapeDtypeStruct((s, DIN), BF16), jax.ShapeDtypeStruct((s, D), BF16),
                   jax.ShapeDtypeStruct((s, D), F32), jax.ShapeDtypeStruct(sp_in.full_shape, BF16)] + gather.out_shape,
        input_output_aliases={3 + w: 3 + w for w in range(n + 1)},
        compiler_params=_params(("arbitrary", "arbitrary")),
    )(place, x, vec_d, placed_in, *placed_rest)


def _fwd_mix(proj, x, vec_d, w_pool, w_bout, w_o, placed_ffn, ts):
    s = x.shape[0]
    gather = _WeightGather(SHARDED[4:])
    n = gather.n

    def body(*refs):
        ins, outs, rest = refs[:6], refs[6 + n:14 + n], refs[14 + n:14 + 2 * n]
        scratch, sems = refs[14 + 2 * n:-2], refs[-2:]
        i = pl.program_id(0)
        nt = s // ts
        pl.when(i == 0)(lambda: gather.start(rest, *sems))
        pl.when(i == nt - 1 - nt // 8)(lambda: gather.forward(rest, *sems))
        compute(*ins, *outs, *scratch)
        pl.when(i == nt - 1)(lambda: gather.drain(rest, *sems))

    def compute(p_ref, x_ref, v_ref, wp_hbm, wb_hbm, wo_hbm,
                x1_ref, o_ref, pg_ref, q_ref, mg_ref, ya_ref, yb_ref, cv_ref,
                wp, wb, wo, carry_p, carry_v, sem):
        i = pl.program_id(0)
        _load_once([(wp_hbm, wp), (wb_hbm, wb), (wo_hbm, wo)], sem)

        @pl.when(i == 0)
        def _():
            carry_p[...] = jnp.zeros_like(carry_p)
            carry_v[...] = jnp.zeros_like(carry_v)

        t0 = i * ts
        for g in range(NG):
            cols = slice(g * GW, (g + 1) * GW)
            u = p_ref[:, cols].astype(F32)
            e = u
            for l in range(g + 1):
                slot = slice((1 << l) - 1, (2 << l) - 1)
                shifted, carry_p[slot, :, cols] = _before(e, carry_p[slot, :, cols], 1 << l)
                e = e + shifted
            pgb = (e / _pool_counts(t0, g) - u).astype(BF16)
            pg_ref[:, cols] = pgb
            ya_ref[:, cols] = _dot(pgb, wp[g]).astype(BF16)

        u_x = p_ref[:, D:2 * D].astype(F32)
        u_c = p_ref[:, 3 * D:4 * D].astype(F32)
        v = u_c * u_x
        cv = _causal_conv(v, carry_v, slice(None), _row(v_ref, V_CW0), _row(v_ref, V_CW1),
                          _row(v_ref, V_CW2), _row(v_ref, V_CB))
        cv_ref[...] = cv.astype(BF16)
        q = (p_ref[:, 2 * D:3 * D].astype(F32) * cv).astype(BF16)
        q_ref[...] = q
        y_b = _dot(q, wb[...])
        yb_ref[...] = y_b.astype(BF16)

        y_a = ya_ref[...].astype(F32) * _row(v_ref, V_PSCALE)
        merged = (jax.nn.sigmoid(p_ref[:, 4 * D:5 * D].astype(F32)) * y_a
                  + jax.nn.sigmoid(p_ref[:, 5 * D:6 * D].astype(F32)) * y_b).astype(BF16)
        mg_ref[...] = merged
        o = _dot(merged, wo[...])
        o_ref[...] = o.astype(BF16)
        x1_ref[...] = x_ref[...] + _row(v_ref, V_GT1) * ((o * _rms(o)) * _row(v_ref, V_GPOST1))

    tile = lambda w: pl.BlockSpec((ts, w), lambda i: (i, 0))
    hbm = pl.BlockSpec(memory_space=pl.ANY)
    return pl.pallas_call(
        body, name="fwd_mix", grid=(s // ts,),
        in_specs=[tile(DIN), tile(D), pl.BlockSpec((VD_ROWS, D), lambda i: (0, 0)), hbm, hbm, hbm] + gather.specs_any,
        out_specs=[tile(D)] * 8 + gather.specs_any,
        out_shape=[jax.ShapeDtypeStruct((s, D), F32)] + [jax.ShapeDtypeStruct((s, D), BF16)] * 7 + gather.out_shape,
        input_output_aliases={6 + w: 8 + w for w in range(n)},
        scratch_shapes=[pltpu.VMEM((NG, GW, GW), BF16), pltpu.VMEM((D, D), BF16), pltpu.VMEM((D, D), BF16),
                        pltpu.VMEM((POOL_CARRY, 8, D), F32), pltpu.VMEM((CONV_CARRY, 8, D), F32),
                        pltpu.SemaphoreType.DMA((3,))] + gather.scratch,
        compiler_params=_params(("arbitrary",)),
    )(proj, x, vec_d, w_pool, w_bout, w_o, *placed_ffn)


def _fwd_ffn(x1, tgt, vec_d, vec_f, w_up, w_down, ts):
    s = x1.shape[0]

    def body(x1_ref, t_ref, v_ref, f_ref, wu_hbm, wd_hbm,
             up_ref, upc_ref, a_ref, h2_ref, dx2_ref, dff_ref, vo_ref, loss_ref,
             wu, wd, carry, sem):
        i = pl.program_id(0)
        _load_once([(wu_hbm, wu), (wd_hbm, wd)], sem)

        @pl.when(i == 0)
        def _():
            carry[...] = jnp.zeros_like(carry)
            vo_ref[...] = jnp.zeros_like(vo_ref)
            loss_ref[...] = jnp.zeros_like(loss_ref)

        x1v = x1_ref[...]
        n3 = x1v * _rms(x1v)
        h2 = (n3 * (_row(v_ref, V_GPRE2) * (1.0 + _row(v_ref, V_SC2))) + _row(v_ref, V_SH2)).astype(BF16)
        h2_ref[...] = h2

        ff = jnp.zeros((ts, D), F32)
        for lo, hi in FFN_SLABS_FWD:
            up = []
            for cols in (slice(lo, hi), slice(F + lo, F + hi)):
                u0 = _dot(h2, wu[:, cols])
                up_ref[:, cols] = u0.astype(BF16)
                y = _causal_conv(u0, carry, cols, f_ref[FV_W0:FV_W0 + 1, cols], f_ref[FV_W1:FV_W1 + 1, cols],
                                 f_ref[FV_W2:FV_W2 + 1, cols], f_ref[FV_B:FV_B + 1, cols])
                upc_ref[:, cols] = y.astype(BF16)
                up.append(y)
            gelu, _ = _gelu_and_grad(up[0])
            a = (gelu * up[1]).astype(BF16)
            a_ref[:, lo:hi] = a
            ff = ff + _dot(a, wd[lo:hi, :])

        r4 = _rms(ff)
        n4 = ff * r4
        gt2 = _row(v_ref, V_GT2)
        gpost = _row(v_ref, V_GPOST2)
        gate_gain = gt2 * gpost
        diff = (x1v + gate_gain * n4) - _load_rows(t_ref, ts)
        loss_ref[...] += jnp.full(loss_ref.shape, 0.5 / D * jnp.sum(diff * diff), F32)
        dx2_ref[...] = diff * (1.0 / D)
        s1 = _colsum(diff * n4)
        vo_ref[0:1, :] += s1 * (gpost * (1.0 / D))
        vo_ref[1:2, :] += s1 * (gt2 * (1.0 / D))
        dff_ref[...] = _rms_bwd(diff * (gate_gain * (1.0 / D)), n4, r4).astype(BF16)

    tile = lambda w: pl.BlockSpec((ts, w), lambda i: (i, 0))
    full = lambda r, w: pl.BlockSpec((r, w), lambda i: (0, 0))
    hbm = pl.BlockSpec(memory_space=pl.ANY)
    return pl.pallas_call(
        body, name="fwd_ffn", grid=(s // ts,),
        in_specs=[tile(D), tile(D), full(VD_ROWS, D), full(FV_ROWS, F2), hbm, hbm],
        out_specs=[tile(F2), tile(F2), tile(F), tile(D), tile(D), tile(D), full(8, D), full(8, 128)],
        out_shape=[jax.ShapeDtypeStruct((s, F2), BF16), jax.ShapeDtypeStruct((s, F2), BF16),
                   jax.ShapeDtypeStruct((s, F), BF16),
                   jax.ShapeDtypeStruct((s, D), BF16), jax.ShapeDtypeStruct((s, D), F32),
                   jax.ShapeDtypeStruct((s, D), BF16), jax.ShapeDtypeStruct((8, D), F32),
                   jax.ShapeDtypeStruct((8, 128), F32)],
        scratch_shapes=[pltpu.VMEM((D, F2), BF16), pltpu.VMEM((F, D), BF16), pltpu.VMEM((CONV_CARRY, 8, F2), F32),
                        pltpu.SemaphoreType.DMA((2,))],
        compiler_params=_params(("arbitrary",)),
    )(x1, tgt, vec_d, vec_f, w_up, w_down)


def _bwd_ffn(dff, dx2, x1, up0, upc, vec_d, vec_f, w_up, w_down, exchange, ex_grads, ts):
    s = x1.shape[0]
    nt = s // ts
    n = exchange.n

    def body(*refs):
        ins, grads = refs[:9], refs[9:9 + n]
        outs, recvs = refs[9 + n:13 + n], refs[13 + n:13 + 2 * n]
        scratch, sems = refs[13 + 2 * n:-2], refs[-2:]
        i = pl.program_id(0)
        pl.when(i == 0)(lambda: exchange.start(grads, recvs, *sems))
        compute(*ins, *outs, *scratch)
        pl.when(i == nt - 1)(lambda: exchange.finish(grads, recvs, *sems))

    def compute(dff_ref, dx2_ref, x1_ref, up_ref, upc_ref, v_ref, f_ref, wu_hbm, wd_hbm,
                dx1_ref, dup_ref, vo_ref, fo_ref, wu, wd, carry, sem):
        i = pl.program_id(0)
        _load_once([(wu_hbm, wu), (wd_hbm, wd)], sem)

        @pl.when(i == 0)
        def _():
            carry[...] = jnp.zeros_like(carry)
            vo_ref[...] = jnp.zeros_like(vo_ref)
            fo_ref[...] = jnp.zeros_like(fo_ref)

        dffb = dff_ref[...]

        dh2 = jnp.zeros((ts, D), F32)
        for lo, hi in FFN_SLABS_BWD:
            slabs = (slice(lo, hi), slice(F + lo, F + hi))
            gelu, dgelu = _gelu_and_grad(upc_ref[:, slabs[0]].astype(F32))
            da = _dot_nt(dffb, wd[lo:hi, :])
            dups = (da * upc_ref[:, slabs[1]].astype(F32) * dgelu, da * gelu)
            for cols, dup in zip(slabs, dups):
                du0, d1, d2 = _causal_conv_bwd(dup, carry, cols, f_ref[FV_W0:FV_W0 + 1, cols],
                                               f_ref[FV_W1:FV_W1 + 1, cols], f_ref[FV_W2:FV_W2 + 1, cols])
                u0 = up_ref[:, cols].astype(F32)
                fo_ref[FV_B:FV_B + 1, cols] += _colsum(dup)
                fo_ref[FV_W2:FV_W2 + 1, cols] += _colsum(dup * u0)
                fo_ref[FV_W1:FV_W1 + 1, cols] += _colsum(d1 * u0)
                fo_ref[FV_W0:FV_W0 + 1, cols] += _colsum(d2 * u0)
                du0 = du0.astype(BF16)
                dup_ref[:, cols] = du0
                dh2 = dh2 + _dot_nt(du0, wu[:, cols])

        x1v = x1_ref[...]
        r3 = _rms(x1v)
        n3 = x1v * r3
        gpre = _row(v_ref, V_GPRE2)
        sc = 1.0 + _row(v_ref, V_SC2)
        vo_ref[0:1, :] += _colsum(dh2)
        s2 = _colsum(dh2 * n3)
        vo_ref[1:2, :] += s2 * gpre
        vo_ref[2:3, :] += s2 * sc
        dx1_ref[...] = dx2_ref[...] + _rms_bwd(dh2 * (gpre * sc), n3, r3)

    rev = lambda w: pl.BlockSpec((ts, w), lambda i: (nt - 1 - i, 0))
    full = lambda r, w: pl.BlockSpec((r, w), lambda i: (0, 0))
    hbm = pl.BlockSpec(memory_space=pl.ANY)
    return pl.pallas_call(
        body, name="bwd_ffn", grid=(nt,),
        in_specs=[rev(D), rev(D), rev(D), rev(F2), rev(F2), full(VD_ROWS, D), full(FV_ROWS, F2), hbm, hbm]
        + exchange.specs_any,
        out_specs=[rev(D), rev(F2), full(8, D), full(FV_ROWS, F2)] + exchange.specs_any,
        out_shape=[jax.ShapeDtypeStruct((s, D), F32), jax.ShapeDtypeStruct((s, F2), BF16),
                   jax.ShapeDtypeStruct((8, D), F32), jax.ShapeDtypeStruct((FV_ROWS, F2), F32)] + exchange.out_shape,
        scratch_shapes=[pltpu.VMEM((D, F2), BF16), pltpu.VMEM((F, D), BF16), pltpu.VMEM((CONV_CARRY, 8, F2), F32),
                        pltpu.SemaphoreType.DMA((2,))] + exchange.scratch,
        compiler_params=_params(("arbitrary",)),
    )(dff, dx2, x1, up0, upc, vec_d, vec_f, w_up, w_down, *ex_grads)


def _bwd_mix(dx1, o, proj, cv, ya0, yb, vec_d, w_pool, w_bout, w_o, exchange, ex_grads, ts):
    s = dx1.shape[0]
    nt = s // ts
    n = exchange.n

    def body(*refs):
        ins, grads = refs[:10], refs[10:10 + n]
        outs, recvs = refs[10 + n:15 + n], refs[15 + n:15 + 2 * n]
        scratch, sems = refs[15 + 2 * n:-2], refs[-2:]
        i = pl.program_id(0)
        pl.when(i == 0)(lambda: exchange.start(grads, recvs, *sems))
        compute(*ins, *outs, *scratch)
        pl.when(i == nt - 1)(lambda: exchange.finish(grads, recvs, *sems))

    def compute(dx1_ref, o_ref, p_ref, cv_ref, ya_ref, yb_ref, v_ref, wp_hbm, wb_hbm, wo_hbm,
                dp_ref, do_ref, dyb_ref, dya_ref, vo_ref, wp, wb, wo, carry_d, carry_e, sem):
        i = pl.program_id(0)
        _load_once([(wp_hbm, wp), (wb_hbm, wb), (wo_hbm, wo)], sem)

        @pl.when(i == 0)
        def _():
            carry_d[...] = jnp.zeros_like(carry_d)
            carry_e[...] = jnp.zeros_like(carry_e)
            vo_ref[...] = jnp.zeros_like(vo_ref)

        t0 = (nt - 1 - i) * ts
        dx1v = dx1_ref[...]
        ov = o_ref[...].astype(F32)
        r2 = _rms(ov)
        n2 = ov * r2
        gpost = _row(v_ref, V_GPOST1)
        gt1 = _row(v_ref, V_GT1)
        s1 = _colsum(dx1v * n2)
        vo_ref[0:1, :] += s1 * gpost
        vo_ref[1:2, :] += s1 * gt1
        dob = _rms_bwd(dx1v * (gt1 * gpost), n2, r2).astype(BF16)
        do_ref[...] = dob
        dmerged = _dot_nt(dob, wo[...])

        ya0 = ya_ref[...].astype(F32)
        pscale = _row(v_ref, V_PSCALE)
        sa = jax.nn.sigmoid(p_ref[:, 4 * D:5 * D].astype(F32))
        dp_ref[:, 4 * D:5 * D] = (dmerged * (ya0 * pscale) * sa * (1.0 - sa)).astype(BF16)
        dy_a = dmerged * sa
        vo_ref[2:3, :] += _colsum(dy_a * ya0)
        dya0 = (dy_a * pscale).astype(BF16)
        dya_ref[...] = dya0

        sb = jax.nn.sigmoid(p_ref[:, 5 * D:6 * D].astype(F32))
        dp_ref[:, 5 * D:6 * D] = (dmerged * yb_ref[...].astype(F32) * sb * (1.0 - sb)).astype(BF16)
        dy_b = (dmerged * sb).astype(BF16)
        dyb_ref[...] = dy_b
        dq = _dot_nt(dy_b, wb[...])

        u_x = p_ref[:, D:2 * D].astype(F32)
        u_b = p_ref[:, 2 * D:3 * D].astype(F32)
        u_c = p_ref[:, 3 * D:4 * D].astype(F32)
        w0, w1, w2 = _row(v_ref, V_CW0), _row(v_ref, V_CW1), _row(v_ref, V_CW2)
        dp_ref[:, 2 * D:3 * D] = (dq * cv_ref[...].astype(F32)).astype(BF16)
        dcv = dq * u_b
        dv, d1, d2 = _causal_conv_bwd(dcv, carry_d, slice(None), w0, w1, w2)
        v = u_c * u_x
        vo_ref[3:4, :] += _colsum(dcv)
        vo_ref[4:5, :] += _colsum(d2 * v)
        vo_ref[5:6, :] += _colsum(d1 * v)
        vo_ref[6:7, :] += _colsum(dcv * v)
        dp_ref[:, D:2 * D] = (dv * u_c).astype(BF16)
        dp_ref[:, 3 * D:4 * D] = (dv * u_x).astype(BF16)

        for g in range(NG):
            cols = slice(g * GW, (g + 1) * GW)
            dpg = _dot_nt(dya0[:, cols], wp[g])
            e = dpg / _pool_counts(t0, g)
            for l in range(g + 1):
                slot = slice((1 << l) - 1, (2 << l) - 1)
                shifted, carry_e[slot, :, cols] = _after(e, carry_e[slot, :, cols], 1 << l)
                e = e + shifted
            dp_ref[:, cols] = (e - dpg).astype(BF16)

    rev = lambda w: pl.BlockSpec((ts, w), lambda i: (nt - 1 - i, 0))
    hbm = pl.BlockSpec(memory_space=pl.ANY)
    return pl.pallas_call(
        body, name="bwd_mix", grid=(nt,),
        in_specs=[rev(D), rev(D), rev(DIN), rev(D), rev(D), rev(D), pl.BlockSpec((VD_ROWS, D), lambda i: (0, 0)),
                  hbm, hbm, hbm] + exchange.specs_any,
        out_specs=[rev(DIN), rev(D), rev(D), rev(D), pl.BlockSpec((8, D), lambda i: (0, 0))] + exchange.specs_any,
        out_shape=[jax.ShapeDtypeStruct((s, DIN), BF16)] + [jax.ShapeDtypeStruct((s, D), BF16)] * 3
        + [jax.ShapeDtypeStruct((8, D), F32)] + exchange.out_shape,
        scratch_shapes=[pltpu.VMEM((NG, GW, GW), BF16), pltpu.VMEM((D, D), BF16), pltpu.VMEM((D, D), BF16),
                        pltpu.VMEM((CONV_CARRY, 8, D), F32), pltpu.VMEM((POOL_CARRY, 8, D), F32),
                        pltpu.SemaphoreType.DMA((3,))] + exchange.scratch,
        compiler_params=_params(("arbitrary",)),
    )(dx1, o, proj, cv, ya0, yb, vec_d, w_pool, w_bout, w_o, *ex_grads)


def _bwd_in(dproj, dx1, x, vec_d, w_in, exchange, ex_grads, ts):
    s = x.shape[0]
    nt = s // ts
    n = exchange.n

    def body(*refs):
        ins, grads = refs[:5], refs[5:5 + n]
        outs, recvs = refs[5 + n:7 + n], refs[7 + n:7 + 2 * n]
        scratch, sems = refs[7 + 2 * n:-2], refs[-2:]
        i = pl.program_id(0)
        pl.when(i == 0)(lambda: exchange.start(grads, recvs, *sems))
        compute(*ins, *outs, *scratch)
        pl.when(i == nt - 1)(lambda: exchange.finish(grads, recvs, *sems))

    def compute(dp_ref, dx1_ref, x_ref, v_ref, w_hbm, dx_ref, vo_ref, w_vmem, sem):
        _load_once([(w_hbm, w_vmem)], sem)

        @pl.when(pl.program_id(0) == 0)
        def _():
            vo_ref[...] = jnp.zeros_like(vo_ref)

        dh1 = _dot_nt(dp_ref[...], w_vmem[...])
        xv = x_ref[...]
        r1 = _rms(xv)
        n1 = xv * r1
        gpre = _row(v_ref, V_GPRE1)
        sc = 1.0 + _row(v_ref, V_SC1)
        vo_ref[0:1, :] += _colsum(dh1)
        s1 = _colsum(dh1 * n1)
        vo_ref[1:2, :] += s1 * gpre
        vo_ref[2:3, :] += s1 * sc
        _store_rows(dx_ref, dx1_ref[...] + _rms_bwd(dh1 * (gpre * sc), n1, r1), ts)

    tile = lambda w: pl.BlockSpec((ts, w), lambda i: (i, 0))
    return pl.pallas_call(
        body, name="bwd_in", grid=(s // ts,),
        in_specs=[tile(DIN), tile(D), tile(D), pl.BlockSpec((VD_ROWS, D), lambda i: (0, 0)),
                  pl.BlockSpec(memory_space=pl.ANY)] + exchange.specs_any,
        out_specs=[tile(D), pl.BlockSpec((8, D), lambda i: (0, 0))] + exchange.specs_any,
        out_shape=[jax.ShapeDtypeStruct((s, D), F32), jax.ShapeDtypeStruct((8, D), F32)] + exchange.out_shape,
        scratch_shapes=[pltpu.VMEM((D, DIN), BF16), pltpu.SemaphoreType.DMA((1,))] + exchange.scratch,
        compiler_params=_params(("arbitrary",)),
    )(dproj, dx1, x, vec_d, w_in, *ex_grads)


def _dot_tn(a, b):
    return lax.dot_general(a, b, (((0,), (0,)), ((), ())), preferred_element_type=F32)


def _wgrad(a, b, tm, tn, ts, name, dtype, exchange=None, ex_grads=()):
    s, m = a.shape
    nn = b.shape[1]
    grid = (m // tm, nn // tn, s // ts)
    n = exchange.n if exchange else 0

    def body(*refs):
        a_ref, b_ref = refs[:2]
        grads = refs[2:2 + n]
        o_ref = refs[2 + n]
        recvs = refs[3 + n:3 + 2 * n]
        acc = refs[3 + 2 * n]
        sems = refs[4 + 2 * n:]
        i, j, k = pl.program_id(0), pl.program_id(1), pl.program_id(2)
        if exchange:
            pl.when((i == 0) & (j == 0) & (k == 0))(lambda: exchange.start(grads, recvs, *sems))
        part = _dot_tn(a_ref[...], b_ref[...])

        @pl.when(k == 0)
        def _():
            acc[...] = part

        @pl.when(k > 0)
        def _():
            acc[...] += part

        @pl.when(k == grid[2] - 1)
        def _():
            o_ref[...] = acc[...].astype(dtype)

        if exchange:
            pl.when((i == grid[0] - 1) & (j == grid[1] - 1) & (k == grid[2] - 1))(
                lambda: exchange.finish(grads, recvs, *sems))

    hosted = exchange.specs_any if exchange else []
    return pl.pallas_call(
        body, name=name, grid=grid,
        in_specs=[pl.BlockSpec((ts, tm), lambda i, j, k: (k, i)), pl.BlockSpec((ts, tn), lambda i, j, k: (k, j))]
        + hosted,
        out_specs=[pl.BlockSpec((tm, tn), lambda i, j, k: (i, j))] + hosted,
        out_shape=[jax.ShapeDtypeStruct((m, nn), dtype)] + (exchange.out_shape if exchange else []),
        scratch_shapes=[pltpu.VMEM((tm, tn), F32)] + (exchange.scratch if exchange else []),
        compiler_params=_params(("arbitrary", "arbitrary", "arbitrary")),
    )(a, b, *ex_grads)


def _wgrad_pool(pg, dya0, ts):
    s = pg.shape[0]
    nk = s // ts

    def body(a_ref, b_ref, o_ref, acc):
        k = pl.program_id(1)
        part = _dot_tn(a_ref[...], b_ref[...])

        @pl.when(k == 0)
        def _():
            acc[...] = part

        @pl.when(k > 0)
        def _():
            acc[...] += part

        @pl.when(k == nk - 1)
        def _():
            o_ref[0] = acc[...].astype(BF16)

    return pl.pallas_call(
        body, name="wgrad_pool", grid=(NG, nk),
        in_specs=[pl.BlockSpec((ts, GW), lambda g, k: (k, g)), pl.BlockSpec((ts, GW), lambda g, k: (k, g))],
        out_specs=pl.BlockSpec((1, GW, GW), lambda g, k: (g, 0, 0)),
        out_shape=jax.ShapeDtypeStruct((NG, GW, GW), BF16),
        scratch_shapes=[pltpu.VMEM((GW, GW), F32)],
        compiler_params=_params(("arbitrary", "arbitrary")),
    )(pg, dya0)


FFN_SLABS_FWD = ((0, 2816),)
FFN_SLABS_BWD = ((0, 1536), (1536, 2816))
TS_PROJ = 512
TS_MIX = 256
TS_FFN = 256
TS_WGRAD = 2048


def _local_step(x, tgt, vec_d, vec_f, placed, place):
    s = x.shape[0]
    tw = min(TS_WGRAD, s)
    sp_in, sp_pool, sp_bout, sp_o, sp_up, sp_down = SHARDED
    proj, h1, xs, w_in, w_pool, w_bout, w_o = _fwd_proj(x, vec_d, placed[0], placed[1:4], place, min(TS_PROJ, s))
    x1, o, pg, q, merged, ya0, yb, cv, w_up, w_down = _fwd_mix(proj, xs, vec_d, w_pool, w_bout, w_o, placed[4:],
                                                               min(TS_MIX, s))
    up0, upc, a, h2, dx2, dff, vo_f, loss = _fwd_ffn(x1, tgt, vec_d, vec_f, w_up, w_down, min(TS_FFN, s))
    g_down, = _wgrad(a, dff, F // 2, D, tw, "wgrad_down", BF16)
    dx1, dup0, vo_b, fo, r_down = _bwd_ffn(dff, dx2, x1, up0, upc, vec_d, vec_f, w_up, w_down,
                                           _GradExchange([sp_down]), [g_down], min(TS_FFN, s))
    g_up, = _wgrad(h2, dup0, D, F2 // NCHIP, tw, "wgrad_up", BF16)
    dproj, do, dyb, dya0, vo_m, r_up = _bwd_mix(dx1, o, proj, cv, ya0, yb, vec_d, w_pool, w_bout, w_o,
                                                _GradExchange([sp_up]), [g_up], min(TS_MIX, s))
    g_o, = _wgrad(merged, do, D, D, tw, "wgrad_o", BF16)
    g_bout, = _wgrad(q, dyb, D, D, tw, "wgrad_bout", BF16)
    g_pool = _wgrad_pool(pg, dya0, tw)
    g_in, r_pool, r_bout, r_o = _wgrad(h1, dproj, D, DIN // NCHIP, tw, "wgrad_in", BF16,
                                       _GradExchange([sp_pool, sp_bout, sp_o]), [g_pool, g_bout, g_o])
    dx, vo_i, r_in = _bwd_in(dproj, dx1, xs, vec_d, w_in, _GradExchange([sp_in]), [g_in], min(TS_PROJ, s))
    vecs = dict(
        dsh1=vo_i[0], dsc1=vo_i[1], dg_pre_mix=vo_i[2],
        dgt1=vo_m[0], dg_post_mix=vo_m[1], dpool_scale=vo_m[2], dconv_b=vo_m[3],
        dconv_w=vo_m[4:7],
        dsh2=vo_b[0], dsc2=vo_b[1], dg_pre_ffn=vo_b[2],
        dgt2=vo_f[0], dg_post_ffn=vo_f[1],
        dffn_conv_w=fo[FV_W0:FV_W2 + 1], dffn_conv_b=fo[FV_B],
    )
    local = dict(w_in=g_in, w_pool=g_pool, w_bout=g_bout, w_o=g_o, w_up=g_up, w_down=g_down)
    received = dict(w_in=r_in, w_pool=r_pool, w_bout=r_bout, w_o=r_o, w_up=r_up, w_down=r_down)
    return loss, dx, vecs, local, received


def _aligned(offset, n):
    return offset if isinstance(offset, int) else pl.multiple_of(offset, n)


class _Sharded:
    def __init__(self, name, full_shape, shard_axis, half_axis):
        self.name = name
        self.full_shape = full_shape
        self.shard_axis = shard_axis
        self.half_axis = half_axis
        self.shard_shape = tuple(n // NCHIP if a == shard_axis else n for a, n in enumerate(full_shape))
        self.piece_shape = tuple(n // 2 if a == half_axis else n for a, n in enumerate(self.shard_shape))

    def piece(self, full_ref, k, h):
        idx = []
        for a, n in enumerate(self.piece_shape):
            if a == self.shard_axis and a == self.half_axis:
                idx.append(pl.ds(_aligned((2 * k + h) * n, n), n))
            elif a == self.shard_axis:
                idx.append(pl.ds(_aligned(k * n, n), n))
            elif a == self.half_axis:
                idx.append(pl.ds(_aligned(h * n, n), n))
            else:
                idx.append(slice(None))
        return full_ref.at[tuple(idx)]

    def shard(self, full_ref, k):
        n = self.shard_shape[self.shard_axis]
        idx = [pl.ds(_aligned(k * n, n), n) if a == self.shard_axis else slice(None)
               for a in range(len(self.full_shape))]
        return full_ref.at[tuple(idx)]

    def half(self, shard_ref, h):
        n = self.piece_shape[self.half_axis]
        idx = [pl.ds(_aligned(h * n, n), n) if a == self.half_axis else slice(None)
               for a in range(len(self.full_shape))]
        return shard_ref.at[tuple(idx)]

SHARDED = (
    _Sharded("w_in", (D, DIN), 1, 0),
    _Sharded("w_pool", (NG, GW, GW), 1, 0),
    _Sharded("w_bout", (D, D), 0, 0),
    _Sharded("w_o", (D, D), 0, 0),
    _Sharded("w_up", (D, F2), 1, 0),
    _Sharded("w_down", (F, D), 0, 0),
)
NW = len(SHARDED)


def _mesh_place():
    x, y, c = lax.axis_index("x"), lax.axis_index("y"), lax.axis_index("c")
    chips = [(1 - x, y), (x, 1 - y), (1 - x, 1 - y)]
    return x, y, c, 2 * x + y, chips, [2 * px + py for px, py in chips]


def _remote(src, dst, send_sem, recv_sem, device):
    return pltpu.make_async_remote_copy(src_ref=src, dst_ref=dst, send_sem=send_sem, recv_sem=recv_sem,
                                        device_id=device, device_id_type=MESH)


def _all_gather_small(block, name):
    m_per, n = block.shape

    def body(x_ref, out_ref, send_sems, recv_sems, local_sem):
        x, y, c, _, chips, _ = _mesh_place()
        me, sibling = (x, y, c), (x, y, 1 - c)

        def rows(px, py, pc):
            return out_ref.at[pl.ds((4 * px + 2 * py + pc) * m_per, m_per), :]

        def copy(k, blk, to, src=None):
            return _remote(rows(*blk) if src is None else src, rows(*blk), send_sems.at[k], recv_sems.at[k], to)

        mine = pltpu.make_async_copy(x_ref, rows(*me), local_sem)
        mine.start()
        first = [copy(0, me, sibling, src=x_ref)]
        first += [copy(1 + j, me, (*chip, c), src=x_ref) for j, chip in enumerate(chips)]
        for cp in first:
            cp.start()
        passed = [copy(4 + j, (*chip, c), sibling) for j, chip in enumerate(chips)]
        for j, chip in enumerate(chips):
            copy(1 + j, (*chip, c), me).wait_recv()
            passed[j].start()
        copy(0, sibling, me).wait_recv()
        for j, chip in enumerate(chips):
            copy(4 + j, (*chip, 1 - c), me).wait_recv()
        for cp in first + passed:
            cp.wait_send()
        mine.wait()

    return pl.pallas_call(
        body, name=name,
        out_shape=jax.ShapeDtypeStruct((NDEV * m_per, n), block.dtype),
        in_specs=[pl.BlockSpec(memory_space=pltpu.VMEM)],
        out_specs=pl.BlockSpec(memory_space=pltpu.VMEM),
        scratch_shapes=[pltpu.SemaphoreType.DMA((7,)), pltpu.SemaphoreType.DMA((7,)), pltpu.SemaphoreType.DMA],
        compiler_params=pltpu.CompilerParams(vmem_limit_bytes=VMEM_LIMIT),
    )(block)


class _WeightGather:
    def __init__(self, specs):
        self.specs = specs
        self.n = len(specs)
        self.specs_any = [pl.BlockSpec(memory_space=pl.ANY)] * self.n
        self.out_shape = [jax.ShapeDtypeStruct(sp.full_shape, BF16) for sp in specs]
        self.scratch = [pltpu.SemaphoreType.DMA((6 * self.n,)), pltpu.SemaphoreType.DMA((6 * self.n,))]

    def _sends(self, outs, send_sems, recv_sems):
        x, y, c, k_me, chips, _ = _mesh_place()
        sends = []
        for j, chip in enumerate(chips):
            for w, sp in enumerate(self.specs):
                mine = sp.piece(outs[w], k_me, c)
                sends.append(_remote(mine, mine, send_sems.at[6 * w + j], recv_sems.at[6 * w + j], (*chip, c)))
        return sends

    def start(self, outs, send_sems, recv_sems):
        for cp in self._sends(outs, send_sems, recv_sems):
            cp.start()

    def _passes(self, outs, send_sems, recv_sems):
        x, y, c, _, chips, kidx = _mesh_place()
        return [_remote(sp.piece(outs[w], kidx[j], c), sp.piece(outs[w], kidx[j], c),
                        send_sems.at[6 * w + 3 + j], recv_sems.at[6 * w + 3 + j], (x, y, 1 - c))
                for j in range(3) for w, sp in enumerate(self.specs)]

    def forward(self, outs, send_sems, recv_sems):
        x, y, c, _, chips, kidx = _mesh_place()
        for j, chip in enumerate(chips):
            for w, sp in enumerate(self.specs):
                landed = sp.piece(outs[w], kidx[j], c)
                _remote(landed, landed, send_sems.at[6 * w + j], recv_sems.at[6 * w + j], (*chip, c)).wait_recv()
        for cp in self._passes(outs, send_sems, recv_sems):
            cp.start()

    def drain(self, outs, send_sems, recv_sems):
        x, y, c, _, chips, kidx = _mesh_place()
        for j in range(3):
            for w, sp in enumerate(self.specs):
                landed = sp.piece(outs[w], kidx[j], 1 - c)
                _remote(landed, landed, send_sems.at[6 * w + 3 + j], recv_sems.at[6 * w + 3 + j],
                        (x, y, 1 - c)).wait_recv()
        for cp in self._sends(outs, send_sems, recv_sems) + self._passes(outs, send_sems, recv_sems):
            cp.wait_send()

    def finish(self, outs, send_sems, recv_sems):
        self.forward(outs, send_sems, recv_sems)
        self.drain(outs, send_sems, recv_sems)


class _GradExchange:
    def __init__(self, specs):
        self.specs = specs
        self.n = len(specs)
        self.specs_any = [pl.BlockSpec(memory_space=pl.ANY)] * self.n
        self.out_shape = [jax.ShapeDtypeStruct((NDEV,) + sp.piece_shape, BF16) for sp in specs]
        self.scratch = [pltpu.SemaphoreType.DMA((7 * self.n,)), pltpu.SemaphoreType.DMA((NDEV * self.n,))]

    def _sends(self, grads, recvs, send_sems, recv_sems):
        x, y, c, k_me, chips, kidx = _mesh_place()
        dev = 2 * k_me + c
        sends = []
        for w, sp in enumerate(self.specs):
            slot, arrival = recvs[w].at[dev], recv_sems.at[NDEV * w + dev]
            sends.append(_remote(sp.piece(grads[w], k_me, 1 - c), slot, send_sems.at[7 * w], arrival, (x, y, 1 - c)))
            for j, chip in enumerate(chips):
                for h in range(2):
                    sends.append(_remote(sp.piece(grads[w], kidx[j], h), slot, send_sems.at[7 * w + 1 + 2 * j + h],
                                         arrival, (*chip, h)))
        return sends

    def start(self, grads, recvs, send_sems, recv_sems):
        for cp in self._sends(grads, recvs, send_sems, recv_sems):
            cp.start()

    def finish(self, grads, recvs, send_sems, recv_sems):
        x, y, c, k_me, _, _ = _mesh_place()
        dev = 2 * k_me + c
        for w in range(self.n):
            for d in range(NDEV):
                landed = recvs[w].at[d]
                arrival = _remote(landed, landed, send_sems.at[7 * w], recv_sems.at[NDEV * w + d], (x, y, c))
                pl.when(d != dev)(arrival.wait_recv)
        for cp in self._sends(grads, recvs, send_sems, recv_sems):
            cp.wait_send()


def _device_sums(locals_, recvs, place):
    def body(p_ref, *refs):
        a_refs, b_refs, o_refs = refs[:NW], refs[NW:2 * NW], refs[2 * NW:]
        d = pl.program_id(0)
        own = d == p_ref[2]
        terms = [jnp.where(own, a_ref[...], b_ref[...]).astype(F32) for a_ref, b_ref in zip(a_refs, b_refs)]

        @pl.when(d == 0)
        def _():
            for o_ref, term in zip(o_refs, terms):
                o_ref[...] = term

        @pl.when(d > 0)
        def _():
            for o_ref, term in zip(o_refs, terms):
                o_ref[...] += term

    def mine(sp):
        nd = len(sp.piece_shape)
        return pl.BlockSpec(sp.piece_shape, lambda d, p_ref: tuple(
            2 * p_ref[0] + p_ref[1] if a == sp.shard_axis == sp.half_axis else
            p_ref[0] if a == sp.shard_axis else p_ref[1] if a == sp.half_axis else 0 for a in range(nd)))

    def others(sp):
        nd = len(sp.piece_shape)
        return pl.BlockSpec((None,) + sp.piece_shape,
                            lambda d, p_ref: (jnp.where(d == p_ref[2], (d + 1) % NDEV, d),) + (0,) * nd)

    def half(sp):
        nd = len(sp.piece_shape)
        return pl.BlockSpec(sp.piece_shape,
                            lambda d, p_ref: tuple(p_ref[1] if a == sp.half_axis else 0 for a in range(nd)))

    return pl.pallas_call(
        body, name="rs_device_sums",
        grid_spec=pltpu.PrefetchScalarGridSpec(
            num_scalar_prefetch=1, grid=(NDEV,),
            in_specs=[mine(sp) for sp in SHARDED] + [others(sp) for sp in SHARDED],
            out_specs=[half(sp) for sp in SHARDED]),
        out_shape=[jax.ShapeDtypeStruct(sp.shard_shape, F32) for sp in SHARDED],
        compiler_params=_params(("arbitrary",)),
    )(place, *locals_, *recvs)


def _pair_share(halves):
    def body(*refs):
        outs = refs[NW:2 * NW]
        send_sems, recv_sems = refs[2 * NW:]
        x, y, c, _, _, _ = _mesh_place()
        sibling = (x, y, 1 - c)
        sent = []
        for w, sp in enumerate(SHARDED):
            mine = sp.half(outs[w], c)
            cp = _remote(mine, mine, send_sems.at[w], recv_sems.at[w], sibling)
            cp.start()
            sent.append(cp)
        for w, sp in enumerate(SHARDED):
            landed = sp.half(outs[w], 1 - c)
            _remote(landed, landed, send_sems.at[w], recv_sems.at[w], sibling).wait_recv()
        for cp in sent:
            cp.wait_send()

    hbm = pl.BlockSpec(memory_space=pl.ANY)
    return pl.pallas_call(
        body, name="rs_pair_share",
        out_shape=[jax.ShapeDtypeStruct(sp.shard_shape, F32) for sp in SHARDED],
        in_specs=[hbm] * NW, out_specs=[hbm] * NW,
        input_output_aliases={w: w for w in range(NW)},
        scratch_shapes=[pltpu.SemaphoreType.DMA((NW,)), pltpu.SemaphoreType.DMA((NW,))],
    )(*halves)


def _reduce_scatter(local, received, place):
    return _pair_share(_device_sums([local[sp.name] for sp in SHARDED], [received[sp.name] for sp in SHARDED], place))


def _place_bf16(sp, w, place):
    nd = len(sp.full_shape)

    def body(p_ref, w_ref, o_ref):
        o_ref[...] = w_ref[...].astype(BF16)

    return pl.pallas_call(
        body, name="place_" + sp.name,
        grid_spec=pltpu.PrefetchScalarGridSpec(
            num_scalar_prefetch=1, grid=(1,),
            in_specs=[pl.BlockSpec(sp.shard_shape, lambda i, p_ref: (0,) * nd)],
            out_specs=pl.BlockSpec(sp.shard_shape,
                                   lambda i, p_ref: tuple(p_ref[0] if a == sp.shard_axis else 0 for a in range(nd)))),
        out_shape=jax.ShapeDtypeStruct(sp.full_shape, BF16),
        compiler_params=_params(("arbitrary",)),
    )(place, w)


def _matmul_f32(a, b, name):
    def body(a_ref, b_ref, o_ref):
        o_ref[...] = jnp.dot(a_ref[...], b_ref[...], preferred_element_type=F32, precision=lax.Precision.HIGHEST)

    return pl.pallas_call(body, name=name, out_shape=jax.ShapeDtypeStruct((a.shape[0], b.shape[1]), F32),
                          compiler_params=pltpu.CompilerParams(vmem_limit_bytes=VMEM_LIMIT))(a, b)


def _sum_devices(stacked):
    def body(x_ref, o_ref):
        acc = x_ref[0]
        for d in range(1, NDEV):
            acc = acc + x_ref[d]
        o_ref[...] = acc

    return pl.pallas_call(body, name="sum_devices", out_shape=jax.ShapeDtypeStruct(stacked.shape[1:], F32),
                          compiler_params=pltpu.CompilerParams(vmem_limit_bytes=VMEM_LIMIT))(stacked)


def _adamw(w, g, m, v, name):
    r, cdim = w.shape
    tr = r if r <= 256 else (256 if r % 256 == 0 else r // 2)

    def body(w_ref, g_ref, m_ref, v_ref, d_ref, nm_ref, nv_ref):
        gv = g_ref[...]
        nm = ADAM_B1 * m_ref[...] + (1.0 - ADAM_B1) * gv
        nv = ADAM_B2 * v_ref[...] + (1.0 - ADAM_B2) * (gv * gv)
        m_hat = nm / (1.0 - ADAM_B1 ** ADAM_STEP)
        v_hat = nv / (1.0 - ADAM_B2 ** ADAM_STEP)
        d_ref[...] = -ADAM_LR * (m_hat / (jnp.sqrt(v_hat) + ADAM_EPS) + ADAM_WD * w_ref[...])
        nm_ref[...] = nm
        nv_ref[...] = nv

    blk = pl.BlockSpec((tr, cdim), lambda i: (i, 0))
    return pl.pallas_call(
        body, name="adamw_" + name, grid=(r // tr,), in_specs=[blk] * 4, out_specs=[blk] * 3,
        out_shape=[jax.ShapeDtypeStruct(w.shape, F32)] * 3,
        compiler_params=_params(("parallel",)),
    )(w, g, m, v)


WEIGHT_NAMES = ("g_pre_mix", "g_post_mix", "g_pre_ffn", "g_post_ffn", "w_ada", "b_ada", "w_in", "w_pool",
                "pool_scale", "conv_w", "conv_b", "w_bout", "w_o", "w_up", "ffn_conv_w", "ffn_conv_b", "w_down")
MATRIX_NAMES = ("w_ada",) + tuple(sp.name for sp in SHARDED)
VECTOR_NAMES = tuple(n for n in WEIGHT_NAMES if n not in MATRIX_NAMES)

CW = D // NCHIP
FCW = F2 // NCHIP
ADA_W = DIN // NCHIP
COND_BLOCK = (8, 768)
GRAD_BLOCK = (8, 4864)


def _flat_pad(parts, shape):
    flat = jnp.concatenate([p.reshape(-1) for p in parts])
    return jnp.pad(flat, (0, shape[0] * shape[1] - flat.shape[0])).reshape(shape)


def _take(flat, offset, shape):
    size = 1
    for n in shape:
        size *= n
    return flat[offset:offset + size].reshape(shape), offset + size


def kernel(x, c, g_pre_mix, g_post_mix, g_pre_ffn, g_post_ffn, w_ada, b_ada, w_in, w_pool, pool_scale, conv_w, conv_b, w_bout, w_o, w_up, ffn_conv_w, ffn_conv_b, w_down, loss_target, m_g_pre_mix, m_g_post_mix, m_g_pre_ffn, m_g_post_ffn, m_w_ada, m_b_ada, m_w_in, m_w_pool, m_pool_scale, m_conv_w, m_conv_b, m_w_bout, m_w_o, m_w_up, m_ffn_conv_w, m_ffn_conv_b, m_w_down, v_g_pre_mix, v_g_post_mix, v_g_pre_ffn, v_g_post_ffn, v_w_ada, v_b_ada, v_w_in, v_w_pool, v_pool_scale, v_conv_w, v_conv_b, v_w_bout, v_w_o, v_w_up, v_ffn_conv_w, v_ffn_conv_b, v_w_down):
    weights = dict(g_pre_mix=g_pre_mix, g_post_mix=g_post_mix, g_pre_ffn=g_pre_ffn, g_post_ffn=g_post_ffn,
                   w_ada=w_ada, b_ada=b_ada, w_in=w_in, w_pool=w_pool, pool_scale=pool_scale, conv_w=conv_w,
                   conv_b=conv_b, w_bout=w_bout, w_o=w_o, w_up=w_up, ffn_conv_w=ffn_conv_w, ffn_conv_b=ffn_conv_b,
                   w_down=w_down)
    mom1 = dict(g_pre_mix=m_g_pre_mix, g_post_mix=m_g_post_mix, g_pre_ffn=m_g_pre_ffn, g_post_ffn=m_g_post_ffn,
                w_ada=m_w_ada, b_ada=m_b_ada, w_in=m_w_in, w_pool=m_w_pool, pool_scale=m_pool_scale,
                conv_w=m_conv_w, conv_b=m_conv_b, w_bout=m_w_bout, w_o=m_w_o, w_up=m_w_up,
                ffn_conv_w=m_ffn_conv_w, ffn_conv_b=m_ffn_conv_b, w_down=m_w_down)
    mom2 = dict(g_pre_mix=v_g_pre_mix, g_post_mix=v_g_post_mix, g_pre_ffn=v_g_pre_ffn, g_post_ffn=v_g_post_ffn,
                w_ada=v_w_ada, b_ada=v_b_ada, w_in=v_w_in, w_pool=v_w_pool, pool_scale=v_pool_scale,
                conv_w=v_conv_w, conv_b=v_conv_b, w_bout=v_w_bout, w_o=v_w_o, w_up=v_w_up,
                ffn_conv_w=v_ffn_conv_w, ffn_conv_b=v_ffn_conv_b, w_down=v_w_down)

    chip = 2 * lax.axis_index("x") + lax.axis_index("y")
    core = lax.axis_index("c")
    dev = 2 * chip + core
    place = jnp.stack([chip, core, dev]).astype(jnp.int32)

    cond = _all_gather_small(_flat_pad([c, conv_w, ffn_conv_w], COND_BLOCK), "gather_cond")
    cond = cond.reshape(NDEV, -1)
    c_all = cond[:, :D]
    by_chip = cond[0::2]
    conv_w_full = by_chip[:, D:D + 3 * CW].reshape(NCHIP, 3, CW).transpose(1, 0, 2).reshape(3, D)
    ffn_w_full = by_chip[:, D + 3 * CW:D + 3 * CW + 3 * FCW].reshape(NCHIP, 3, FCW).transpose(1, 0, 2).reshape(3, F2)

    mod_cols = _all_gather_small(_matmul_f32(c_all, w_ada[0], "ada_mod"), "gather_mod")
    mod_cols = mod_cols.reshape(NDEV, NDEV, ADA_W)[0::2]
    mod = lax.dynamic_index_in_dim(mod_cols, dev, axis=1, keepdims=False).reshape(6, D) + b_ada.reshape(6, D)
    vec_d = jnp.concatenate([mod, g_pre_mix, g_post_mix, g_pre_ffn, g_post_ffn, pool_scale, conv_b, conv_w_full,
                             jnp.zeros((VD_ROWS - 15, D), F32)], axis=0)
    vec_f = jnp.concatenate([ffn_w_full, ffn_conv_b, jnp.zeros((FV_ROWS - 4, F2), F32)], axis=0)

    placed = [_place_bf16(sp, weights[sp.name][0], place) for sp in SHARDED]
    loss_blk, dx, vecs, local, received = _local_step(x[0], loss_target[0], vec_d, vec_f, placed, place)

    dmod = [vecs[n] for n in ("dsh1", "dsc1", "dgt1", "dsh2", "dsc2", "dgt2")]
    small = [vecs["dg_pre_mix"], vecs["dg_post_mix"], vecs["dg_pre_ffn"], vecs["dg_post_ffn"]] + dmod + [
        vecs["dpool_scale"], vecs["dconv_w"], vecs["dconv_b"], vecs["dffn_conv_w"], vecs["dffn_conv_b"],
        loss_blk[0]]
    gathered = _all_gather_small(_flat_pad(small, GRAD_BLOCK), "gather_vector_grads")
    total = _sum_devices(gathered.reshape((NDEV,) + GRAD_BLOCK)).reshape(-1)
    vgrad = {}
    off = 0
    for n in ("g_pre_mix", "g_post_mix", "g_pre_ffn", "g_post_ffn"):
        vgrad[n], off = _take(total, off, (1, D))
    dmod_off = off
    vgrad["b_ada"], off = _take(total, off, (1, DIN))
    vgrad["pool_scale"], off = _take(total, off, (1, D))
    g_conv_w, off = _take(total, off, (3, D))
    vgrad["conv_w"] = lax.dynamic_slice_in_dim(g_conv_w, chip * CW, CW, axis=1)[None]
    vgrad["conv_b"], off = _take(total, off, (1, D))
    g_ffn_w, off = _take(total, off, (3, F2))
    vgrad["ffn_conv_w"] = lax.dynamic_slice_in_dim(g_ffn_w, chip * FCW, FCW, axis=1)[None]
    vgrad["ffn_conv_b"], off = _take(total, off, (1, F2))
    loss = total[off]

    dmod_all = gathered.reshape(NDEV, -1)[:, dmod_off:dmod_off + DIN]
    dmod_cols = lax.dynamic_slice_in_dim(dmod_all, chip * ADA_W, ADA_W, axis=1)
    g_ada = _matmul_f32(jnp.pad(c_all.T, ((0, 0), (0, 128 - NDEV))), jnp.pad(dmod_cols, ((0, 128 - NDEV), (0, 0))),
                        "ada_wgrad")

    reduced = _reduce_scatter(local, received, place)
    mgrad = {"w_ada": g_ada}
    for sp, g in zip(SHARDED, reduced):
        mgrad[sp.name] = g

    grad, delta, new_m, new_v = {}, {}, {}, {}
    for n in MATRIX_NAMES:
        shape = weights[n].shape
        two_d = (-1, shape[-1])
        d, nm, nv = _adamw(weights[n].reshape(two_d), mgrad[n].reshape(two_d), mom1[n].reshape(two_d),
                           mom2[n].reshape(two_d), n)
        grad[n], delta[n], new_m[n], new_v[n] = (a.reshape(shape) for a in (mgrad[n], d, nm, nv))
    flat = lambda tree: jnp.concatenate([tree[n].reshape(1, -1) for n in VECTOR_NAMES], axis=1)
    d, nm, nv = _adamw(flat(weights), flat(vgrad), flat(mom1), flat(mom2), "vectors")
    off = 0
    for n in VECTOR_NAMES:
        shape = weights[n].shape
        grad[n] = vgrad[n].reshape(shape)
        delta[n], _ = _take(d[0], off, shape)
        new_m[n], _ = _take(nm[0], off, shape)
        new_v[n], off = _take(nv[0], off, shape)

    return (loss, dx[None], *[grad[n] for n in WEIGHT_NAMES], *[delta[n] for n in WEIGHT_NAMES],
            *[new_m[n] for n in WEIGHT_NAMES], *[new_v[n] for n in WEIGHT_NAMES])
```

```python
import jax
import jax.numpy as jnp
from jax import lax
from jax.experimental import pallas as pl
from jax.experimental.pallas import tpu as pltpu

F32 = jnp.float32
BF16 = jnp.bfloat16

D = 1024
DIN = 6 * D
F = 2816
F2 = 2 * F
NG = 4
GW = D // NG
POOL_CARRY = 16
CONV_CARRY = 3
EPS = 1e-6
NCHIP = 4
NDEV = 8

ADAM_LR = 0.001
ADAM_B1 = 0.9
ADAM_B2 = 0.999
ADAM_EPS = 1e-08
ADAM_WD = 0.01
ADAM_STEP = 10

VMEM_LIMIT = 60 * 1024 * 1024

(V_SH1, V_SC1, V_GT1, V_SH2, V_SC2, V_GT2, V_GPRE1, V_GPOST1, V_GPRE2, V_GPOST2,
 V_PSCALE, V_CB, V_CW0, V_CW1, V_CW2) = range(15)
VD_ROWS = 16
FV_W0, FV_W1, FV_W2, FV_B = range(4)
FV_ROWS = 8

MESH = pl.DeviceIdType.MESH


def _params(sem=None, vmem=VMEM_LIMIT):
    return pltpu.CompilerParams(dimension_semantics=sem, vmem_limit_bytes=vmem)


def _row(ref, r):
    return ref[r:r + 1, :]


def _load_once(pairs, sem):
    @pl.when(pl.program_id(0) == 0)
    def _():
        copies = [pltpu.make_async_copy(src, dst, sem.at[n]) for n, (src, dst) in enumerate(pairs)]
        for cp in copies:
            cp.start()
        for cp in copies:
            cp.wait()


def _dot(a, b):
    return jnp.dot(a, b, preferred_element_type=F32)


def _dot_nt(a, b):
    return lax.dot_general(a, b, (((1,), (1,)), ((), ())), preferred_element_type=F32)


BLK = 256
SEG = BLK // 8


def _load_rows(ref, ts):
    blocks = [jnp.swapaxes(ref[b * BLK:(b + 1) * BLK, :].reshape(8, SEG, ref.shape[-1]), 0, 1).reshape(BLK, -1)
              for b in range(ts // BLK)]
    return jnp.concatenate(blocks, axis=0)


def _store_rows(ref, val, ts):
    for b in range(ts // BLK):
        blk = val[b * BLK:(b + 1) * BLK, :].reshape(SEG, 8, val.shape[-1])
        ref[b * BLK:(b + 1) * BLK, :] = jnp.swapaxes(blk, 0, 1).reshape(BLK, -1)


def _times(t0):
    p = lax.broadcasted_iota(jnp.int32, (BLK, 1), 0)
    return t0 + (p & 7) * SEG + (p >> 3)


def _before(x, carry, s):
    x3 = x.reshape(SEG, 8, x.shape[-1])
    tail = pltpu.roll(x3[SEG - s:], 1, 1)
    row = lax.broadcasted_iota(jnp.int32, tail.shape, 1)
    out = jnp.concatenate([jnp.where(row == 0, carry, tail), x3[:SEG - s]], axis=0)
    return out.reshape(x.shape), tail


def _after(x, carry, s):
    x3 = x.reshape(SEG, 8, x.shape[-1])
    head = pltpu.roll(x3[:s], 7, 1)
    row = lax.broadcasted_iota(jnp.int32, head.shape, 1)
    out = jnp.concatenate([x3[s:], jnp.where(row == 7, carry, head)], axis=0)
    return out.reshape(x.shape), head


def _causal_conv(x, carry, cols, w0, w1, w2, b):
    x1, carry[0:1, :, cols] = _before(x, carry[0:1, :, cols], 1)
    x2, carry[1:3, :, cols] = _before(x, carry[1:3, :, cols], 2)
    return b + w2 * x + w1 * x1 + w0 * x2


def _causal_conv_bwd(dy, carry, cols, w0, w1, w2):
    d1, carry[0:1, :, cols] = _after(dy, carry[0:1, :, cols], 1)
    d2, carry[1:3, :, cols] = _after(dy, carry[1:3, :, cols], 2)
    return w2 * dy + w1 * d1 + w0 * d2, d1, d2


def _pool_counts(t0, g):
    return jnp.minimum((_times(t0) + 1).astype(F32), float(2 << g))


def _rms(x):
    return lax.rsqrt(jnp.mean(x * x, axis=-1, keepdims=True) + EPS)


def _rms_bwd(dn, n, r):
    return r * (dn - n * jnp.mean(dn * n, axis=-1, keepdims=True))


def _colsum(x):
    return jnp.sum(x, axis=0, keepdims=True)


def _gelu_and_grad(x):
    k, a = 0.7978845608028654, 0.044715
    x2 = x * x
    th1 = 1.0 + jnp.tanh(x * (x2 * (k * a) + k))
    hx = 0.5 * x
    gelu = hx * th1
    dgelu = 0.5 * th1 + (hx * (th1 * (2.0 - th1))) * (x2 * (3.0 * k * a) + k)
    return gelu, dgelu


def _fwd_proj(x, vec_d, placed_in, placed_rest, place, ts):
    s = x.shape[0]
    nt = s // ts
    cw = DIN // NCHIP
    sp_in = SHARDED[0]
    gather = _WeightGather(SHARDED[1:4])
    n = gather.n

    def body(*refs):
        p_ref, x_ref, v_ref = refs[:3]
        proj_ref, h1_ref, w_full = refs[4 + n:7 + n]
        rest = refs[7 + n:7 + 2 * n]
        w_vmem, h1_all, sem, in_send, in_recv, send_sems, recv_sems = refs[7 + 2 * n:]
        j, i = pl.program_id(0), pl.program_id(1)
        x_, y_, c, k_me, _, _ = _mesh_place()
        sibling = (x_, y_, 1 - c)

        def peer(t):
            return (x_ ^ (t >> 1), y_ ^ (t & 1))

        def w_in_sends():
            mine = sp_in.piece(w_full, k_me, c)
            return [_remote(mine, mine, in_send.at[t - 1], in_recv.at[t - 1], (*peer(t), c)) for t in (1, 2, 3)]

        def load_block(k):
            cp = pltpu.make_async_copy(sp_in.shard(w_full, k), w_vmem.at[k], sem.at[0])
            cp.start()
            cp.wait()

        @pl.when((j == 0) & (i == 0))
        def _():
            for cp in w_in_sends()[:2]:
                cp.start()
            load_block(k_me)

        @pl.when((j == 1) & (i == 0))
        def _():
            for cp in w_in_sends()[:2]:
                cp.wait_send()
            w_in_sends()[2].start()
            gather.start(rest, send_sems, recv_sems)

        for t in (1, 2, 3):
            @pl.when((j == t) & (i == 0))
            def _(t=t):
                k = k_me ^ t
                landed = sp_in.piece(w_full, k, c)
                _remote(landed, landed, in_send.at[t - 1], in_recv.at[t - 1], (*peer(t), c)).wait_recv()
                _remote(landed, landed, in_send.at[2 + t], in_recv.at[2 + t], sibling).start()
                other = sp_in.piece(w_full, k, 1 - c)
                _remote(other, other, in_send.at[2 + t], in_recv.at[2 + t], sibling).wait_recv()
                load_block(k)

        @pl.when(j == 0)
        def _():
            xv = _load_rows(x_ref, ts)
            n1 = xv * _rms(xv)
            h = n1 * (_row(v_ref, V_GPRE1) * (1.0 + _row(v_ref, V_SC1))) + _row(v_ref, V_SH1)
            hb = h.astype(BF16)
            h1_ref[...] = hb
            h1_all[i] = hb

        proj_ref[...] = _dot(h1_all[i], w_vmem[k_me ^ j]).astype(BF16)

        @pl.when((j == NCHIP - 1) & (i == nt - 1))
        def _():
            w_in_sends()[2].wait_send()
            for t in (1, 2, 3):
                landed = sp_in.piece(w_full, k_me ^ t, c)
                _remote(landed, landed, in_send.at[2 + t], in_recv.at[2 + t], sibling).wait_send()
            gather.finish(rest, send_sems, recv_sems)

    once = lambda w: pl.BlockSpec((ts, w), lambda j, i, p: (jnp.where(j == 0, i, nt - 1), 0))
    return pl.pallas_call(
        body, name="fwd_proj",
        grid_spec=pltpu.PrefetchScalarGridSpec(
            num_scalar_prefetch=1, grid=(NCHIP, nt),
            in_specs=[once(D), pl.BlockSpec((VD_ROWS, D), lambda j, i, p: (0, 0)),
                      pl.BlockSpec(memory_space=pl.ANY)] + gather.specs_any,
            out_specs=[pl.BlockSpec((ts, cw), lambda j, i, p: (i, p[0] ^ j)), once(D),
                       pl.BlockSpec(memory_space=pl.ANY)] + gather.specs_any,
            scratch_shapes=[pltpu.VMEM((NCHIP, D, cw), BF16), pltpu.VMEM((nt, ts, D), BF16),
                            pltpu.SemaphoreType.DMA((1,)),
                            pltpu.SemaphoreType.DMA((6,)), pltpu.SemaphoreType.DMA((6,))] + gather.scratch),
        out_shape=[jax.ShapeDtypeStruct((s, DIN), BF16), jax.ShapeDtypeStruct((s, D), BF16),
                   jax.ShapeDtypeStruct(sp_in.full_shape, BF16)] + gather.out_shape,
        input_output_aliases={3 + w: 2 + w for w in range(n + 1)},
        compiler_params=_params(("arbitrary", "arbitrary")),
    )(place, x, vec_d, placed_in, *placed_rest)


def _fwd_mix(proj, x, vec_d, w_pool, w_bout, w_o, placed_ffn, ts):
    s = x.shape[0]
    gather = _WeightGather(SHARDED[4:])
    n = gather.n

    def body(*refs):
        ins, outs, rest = refs[:6], refs[6 + n:14 + n], refs[14 + n:14 + 2 * n]
        scratch, sems = refs[14 + 2 * n:-2], refs[-2:]
        i = pl.program_id(0)
        nt = s // ts
        pl.when(i == 0)(lambda: gather.start(rest, *sems))
        pl.when(i == nt - 1 - nt // 8)(lambda: gather.forward(rest, *sems))
        compute(*ins, *outs, *scratch)
        pl.when(i == nt - 1)(lambda: gather.drain(rest, *sems))

    def compute(p_ref, x_ref, v_ref, wp_hbm, wb_hbm, wo_hbm,
                x1_ref, o_ref, pg_ref, q_ref, mg_ref, ya_ref, yb_ref, cv_ref,
                wp, wb, wo, carry_p, carry_v, sem):
        i = pl.program_id(0)
        _load_once([(wp_hbm, wp), (wb_hbm, wb), (wo_hbm, wo)], sem)

        @pl.when(i == 0)
        def _():
            carry_p[...] = jnp.zeros_like(carry_p)
            carry_v[...] = jnp.zeros_like(carry_v)

        t0 = i * ts
        for g in range(NG):
            cols = slice(g * GW, (g + 1) * GW)
            u = p_ref[:, cols].astype(F32)
            e = u
            for l in range(g + 1):
                slot = slice((1 << l) - 1, (2 << l) - 1)
                shifted, carry_p[slot, :, cols] = _before(e, carry_p[slot, :, cols], 1 << l)
                e = e + shifted
            pgb = (e / _pool_counts(t0, g) - u).astype(BF16)
            pg_ref[:, cols] = pgb
            ya_ref[:, cols] = _dot(pgb, wp[g]).astype(BF16)

        u_x = p_ref[:, D:2 * D].astype(F32)
        u_c = p_ref[:, 3 * D:4 * D].astype(F32)
        v = u_c * u_x
        cv = _causal_conv(v, carry_v, slice(None), _row(v_ref, V_CW0), _row(v_ref, V_CW1),
                          _row(v_ref, V_CW2), _row(v_ref, V_CB))
        cv_ref[...] = cv.astype(BF16)
        q = (p_ref[:, 2 * D:3 * D].astype(F32) * cv).astype(BF16)
        q_ref[...] = q
        y_b = _dot(q, wb[...])
        yb_ref[...] = y_b.astype(BF16)

        y_a = ya_ref[...].astype(F32) * _row(v_ref, V_PSCALE)
        merged = (jax.nn.sigmoid(p_ref[:, 4 * D:5 * D].astype(F32)) * y_a
                  + jax.nn.sigmoid(p_ref[:, 5 * D:6 * D].astype(F32)) * y_b).astype(BF16)
        mg_ref[...] = merged
        o = _dot(merged, wo[...])
        o_ref[...] = o.astype(BF16)
        x1_ref[...] = _load_rows(x_ref, ts) + _row(v_ref, V_GT1) * ((o * _rms(o)) * _row(v_ref, V_GPOST1))

    tile = lambda w: pl.BlockSpec((ts, w), lambda i: (i, 0))
    hbm = pl.BlockSpec(memory_space=pl.ANY)
    return pl.pallas_call(
        body, name="fwd_mix", grid=(s // ts,),
        in_specs=[tile(DIN), tile(D), pl.BlockSpec((VD_ROWS, D), lambda i: (0, 0)), hbm, hbm, hbm] + gather.specs_any,
        out_specs=[tile(D)] * 8 + gather.specs_any,
        out_shape=[jax.ShapeDtypeStruct((s, D), F32)] + [jax.ShapeDtypeStruct((s, D), BF16)] * 7 + gather.out_shape,
        input_output_aliases={6 + w: 8 + w for w in range(n)},
        scratch_shapes=[pltpu.VMEM((NG, GW, GW), BF16), pltpu.VMEM((D, D), BF16), pltpu.VMEM((D, D), BF16),
                        pltpu.VMEM((POOL_CARRY, 8, D), F32), pltpu.VMEM((CONV_CARRY, 8, D), F32),
                        pltpu.SemaphoreType.DMA((3,))] + gather.scratch,
        compiler_params=_params(("arbitrary",)),
    )(proj, x, vec_d, w_pool, w_bout, w_o, *placed_ffn)


def _fwd_ffn(x1, tgt, vec_d, vec_f, w_up, w_down, ts):
    s = x1.shape[0]

    def body(x1_ref, t_ref, v_ref, f_ref, wu_hbm, wd_hbm,
             up_ref, upc_ref, a_ref, h2_ref, dx2_ref, dff_ref, vo_ref, loss_ref,
             wu, wd, carry, sem):
        i = pl.program_id(0)
        _load_once([(wu_hbm, wu), (wd_hbm, wd)], sem)

        @pl.when(i == 0)
        def _():
            carry[...] = jnp.zeros_like(carry)
            vo_ref[...] = jnp.zeros_like(vo_ref)
            loss_ref[...] = jnp.zeros_like(loss_ref)

        x1v = x1_ref[...]
        n3 = x1v * _rms(x1v)
        h2 = (n3 * (_row(v_ref, V_GPRE2) * (1.0 + _row(v_ref, V_SC2))) + _row(v_ref, V_SH2)).astype(BF16)
        h2_ref[...] = h2

        ff = jnp.zeros((ts, D), F32)
        for lo, hi in FFN_SLABS_FWD:
            up = []
            for cols in (slice(lo, hi), slice(F + lo, F + hi)):
                u0 = _dot(h2, wu[:, cols])
                up_ref[:, cols] = u0.astype(BF16)
                y = _causal_conv(u0, carry, cols, f_ref[FV_W0:FV_W0 + 1, cols], f_ref[FV_W1:FV_W1 + 1, cols],
                                 f_ref[FV_W2:FV_W2 + 1, cols], f_ref[FV_B:FV_B + 1, cols])
                upc_ref[:, cols] = y.astype(BF16)
                up.append(y)
            gelu, _ = _gelu_and_grad(up[0])
            a = (gelu * up[1]).astype(BF16)
            a_ref[:, lo:hi] = a
            ff = ff + _dot(a, wd[lo:hi, :])

        r4 = _rms(ff)
        n4 = ff * r4
        gt2 = _row(v_ref, V_GT2)
        gpost = _row(v_ref, V_GPOST2)
        gate_gain = gt2 * gpost
        diff = (x1v + gate_gain * n4) - _load_rows(t_ref, ts)
        loss_ref[...] += jnp.full(loss_ref.shape, 0.5 / D * jnp.sum(diff * diff), F32)
        dx2_ref[...] = diff * (1.0 / D)
        s1 = _colsum(diff * n4)
        vo_ref[0:1, :] += s1 * (gpost * (1.0 / D))
        vo_ref[1:2, :] += s1 * (gt2 * (1.0 / D))
        dff_ref[...] = _rms_bwd(diff * (gate_gain * (1.0 / D)), n4, r4).astype(BF16)

    tile = lambda w: pl.BlockSpec((ts, w), lambda i: (i, 0))
    full = lambda r, w: pl.BlockSpec((r, w), lambda i: (0, 0))
    hbm = pl.BlockSpec(memory_space=pl.ANY)
    return pl.pallas_call(
        body, name="fwd_ffn", grid=(s // ts,),
        in_specs=[tile(D), tile(D), full(VD_ROWS, D), full(FV_ROWS, F2), hbm, hbm],
        out_specs=[tile(F2), tile(F2), tile(F), tile(D), tile(D), tile(D), full(8, D), full(8, 128)],
        out_shape=[jax.ShapeDtypeStruct((s, F2), BF16), jax.ShapeDtypeStruct((s, F2), BF16),
                   jax.ShapeDtypeStruct((s, F), BF16),
                   jax.ShapeDtypeStruct((s, D), BF16), jax.ShapeDtypeStruct((s, D), F32),
                   jax.ShapeDtypeStruct((s, D), BF16), jax.ShapeDtypeStruct((8, D), F32),
                   jax.ShapeDtypeStruct((8, 128), F32)],
        scratch_shapes=[pltpu.VMEM((D, F2), BF16), pltpu.VMEM((F, D), BF16), pltpu.VMEM((CONV_CARRY, 8, F2), F32),
                        pltpu.SemaphoreType.DMA((2,))],
        compiler_params=_params(("arbitrary",)),
    )(x1, tgt, vec_d, vec_f, w_up, w_down)


def _bwd_ffn(dff, dx2, x1, up0, upc, vec_d, vec_f, w_up, w_down, exchange, ex_grads, ts):
    s = x1.shape[0]
    nt = s // ts
    n = exchange.n

    def body(*refs):
        ins, grads = refs[:9], refs[9:9 + n]
        outs, recvs = refs[9 + n:13 + n], refs[13 + n:13 + 2 * n]
        scratch, sems = refs[13 + 2 * n:-2], refs[-2:]
        i = pl.program_id(0)
        pl.when(i == 0)(lambda: exchange.start(grads, recvs, *sems))
        compute(*ins, *outs, *scratch)
        pl.when(i == nt - 1)(lambda: exchange.finish(grads, recvs, *sems))

    def compute(dff_ref, dx2_ref, x1_ref, up_ref, upc_ref, v_ref, f_ref, wu_hbm, wd_hbm,
                dx1_ref, dup_ref, vo_ref, fo_ref, wu, wd, carry, sem):
        i = pl.program_id(0)
        _load_once([(wu_hbm, wu), (wd_hbm, wd)], sem)

        @pl.when(i == 0)
        def _():
            carry[...] = jnp.zeros_like(carry)
            vo_ref[...] = jnp.zeros_like(vo_ref)
            fo_ref[...] = jnp.zeros_like(fo_ref)

        dffb = dff_ref[...]

        dh2 = jnp.zeros((ts, D), F32)
        for lo, hi in FFN_SLABS_BWD:
            slabs = (slice(lo, hi), slice(F + lo, F + hi))
            gelu, dgelu = _gelu_and_grad(upc_ref[:, slabs[0]].astype(F32))
            da = _dot_nt(dffb, wd[lo:hi, :])
            dups = (da * upc_ref[:, slabs[1]].astype(F32) * dgelu, da * gelu)
            for cols, dup in zip(slabs, dups):
                du0, d1, d2 = _causal_conv_bwd(dup, carry, cols, f_ref[FV_W0:FV_W0 + 1, cols],
                                               f_ref[FV_W1:FV_W1 + 1, cols], f_ref[FV_W2:FV_W2 + 1, cols])
                u0 = up_ref[:, cols].astype(F32)
                fo_ref[FV_B:FV_B + 1, cols] += _colsum(dup)
                fo_ref[FV_W2:FV_W2 + 1, cols] += _colsum(dup * u0)
                fo_ref[FV_W1:FV_W1 + 1, cols] += _colsum(d1 * u0)
                fo_ref[FV_W0:FV_W0 + 1, cols] += _colsum(d2 * u0)
                du0 = du0.astype(BF16)
                dup_ref[:, cols] = du0
                dh2 = dh2 + _dot_nt(du0, wu[:, cols])

        x1v = x1_ref[...]
        r3 = _rms(x1v)
        n3 = x1v * r3
        gpre = _row(v_ref, V_GPRE2)
        sc = 1.0 + _row(v_ref, V_SC2)
        vo_ref[0:1, :] += _colsum(dh2)
        s2 = _colsum(dh2 * n3)
        vo_ref[1:2, :] += s2 * gpre
        vo_ref[2:3, :] += s2 * sc
        dx1_ref[...] = dx2_ref[...] + _rms_bwd(dh2 * (gpre * sc), n3, r3)

    rev = lambda w: pl.BlockSpec((ts, w), lambda i: (nt - 1 - i, 0))
    full = lambda r, w: pl.BlockSpec((r, w), lambda i: (0, 0))
    hbm = pl.BlockSpec(memory_space=pl.ANY)
    return pl.pallas_call(
        body, name="bwd_ffn", grid=(nt,),
        in_specs=[rev(D), rev(D), rev(D), rev(F2), rev(F2), full(VD_ROWS, D), full(FV_ROWS, F2), hbm, hbm]
        + exchange.specs_any,
        out_specs=[rev(D), rev(F2), full(8, D), full(FV_ROWS, F2)] + exchange.specs_any,
        out_shape=[jax.ShapeDtypeStruct((s, D), F32), jax.ShapeDtypeStruct((s, F2), BF16),
                   jax.ShapeDtypeStruct((8, D), F32), jax.ShapeDtypeStruct((FV_ROWS, F2), F32)] + exchange.out_shape,
        scratch_shapes=[pltpu.VMEM((D, F2), BF16), pltpu.VMEM((F, D), BF16), pltpu.VMEM((CONV_CARRY, 8, F2), F32),
                        pltpu.SemaphoreType.DMA((2,))] + exchange.scratch,
        compiler_params=_params(("arbitrary",)),
    )(dff, dx2, x1, up0, upc, vec_d, vec_f, w_up, w_down, *ex_grads)


def _bwd_mix(dx1, o, proj, cv, ya0, yb, vec_d, w_pool, w_bout, w_o, exchange, ex_grads, ts):
    s = dx1.shape[0]
    nt = s // ts
    n = exchange.n

    def body(*refs):
        ins, grads = refs[:10], refs[10:10 + n]
        outs, recvs = refs[10 + n:15 + n], refs[15 + n:15 + 2 * n]
        scratch, sems = refs[15 + 2 * n:-2], refs[-2:]
        i = pl.program_id(0)
        pl.when(i == 0)(lambda: exchange.start(grads, recvs, *sems))
        compute(*ins, *outs, *scratch)
        pl.when(i == nt - 1)(lambda: exchange.finish(grads, recvs, *sems))

    def compute(dx1_ref, o_ref, p_ref, cv_ref, ya_ref, yb_ref, v_ref, wp_hbm, wb_hbm, wo_hbm,
                dp_ref, do_ref, dyb_ref, dya_ref, vo_ref, wp, wb, wo, carry_d, carry_e, sem):
        i = pl.program_id(0)
        _load_once([(wp_hbm, wp), (wb_hbm, wb), (wo_hbm, wo)], sem)

        @pl.when(i == 0)
        def _():
            carry_d[...] = jnp.zeros_like(carry_d)
            carry_e[...] = jnp.zeros_like(carry_e)
            vo_ref[...] = jnp.zeros_like(vo_ref)

        t0 = (nt - 1 - i) * ts
        dx1v = dx1_ref[...]
        ov = o_ref[...].astype(F32)
        r2 = _rms(ov)
        n2 = ov * r2
        gpost = _row(v_ref, V_GPOST1)
        gt1 = _row(v_ref, V_GT1)
        s1 = _colsum(dx1v * n2)
        vo_ref[0:1, :] += s1 * gpost
        vo_ref[1:2, :] += s1 * gt1
        dob = _rms_bwd(dx1v * (gt1 * gpost), n2, r2).astype(BF16)
        do_ref[...] = dob
        dmerged = _dot_nt(dob, wo[...])

        ya0 = ya_ref[...].astype(F32)
        pscale = _row(v_ref, V_PSCALE)
        sa = jax.nn.sigmoid(p_ref[:, 4 * D:5 * D].astype(F32))
        dp_ref[:, 4 * D:5 * D] = (dmerged * (ya0 * pscale) * sa * (1.0 - sa)).astype(BF16)
        dy_a = dmerged * sa
        vo_ref[2:3, :] += _colsum(dy_a * ya0)
        dya0 = (dy_a * pscale).astype(BF16)
        dya_ref[...] = dya0

        sb = jax.nn.sigmoid(p_ref[:, 5 * D:6 * D].astype(F32))
        dp_ref[:, 5 * D:6 * D] = (dmerged * yb_ref[...].astype(F32) * sb * (1.0 - sb)).astype(BF16)
        dy_b = (dmerged * sb).astype(BF16)
        dyb_ref[...] = dy_b
        dq = _dot_nt(dy_b, wb[...])

        u_x = p_ref[:, D:2 * D].astype(F32)
        u_b = p_ref[:, 2 * D:3 * D].astype(F32)
        u_c = p_ref[:, 3 * D:4 * D].astype(F32)
        w0, w1, w2 = _row(v_ref, V_CW0), _row(v_ref, V_CW1), _row(v_ref, V_CW2)
        dp_ref[:, 2 * D:3 * D] = (dq * cv_ref[...].astype(F32)).astype(BF16)
        dcv = dq * u_b
        dv, d1, d2 = _causal_conv_bwd(dcv, carry_d, slice(None), w0, w1, w2)
        v = u_c * u_x
        vo_ref[3:4, :] += _colsum(dcv)
        vo_ref[4:5, :] += _colsum(d2 * v)
        vo_ref[5:6, :] += _colsum(d1 * v)
        vo_ref[6:7, :] += _colsum(dcv * v)
        dp_ref[:, D:2 * D] = (dv * u_c).astype(BF16)
        dp_ref[:, 3 * D:4 * D] = (dv * u_x).astype(BF16)

        for g in range(NG):
            cols = slice(g * GW, (g + 1) * GW)
            dpg = _dot_nt(dya0[:, cols], wp[g])
            e = dpg / _pool_counts(t0, g)
            for l in range(g + 1):
                slot = slice((1 << l) - 1, (2 << l) - 1)
                shifted, carry_e[slot, :, cols] = _after(e, carry_e[slot, :, cols], 1 << l)
                e = e + shifted
            dp_ref[:, cols] = (e - dpg).astype(BF16)

    rev = lambda w: pl.BlockSpec((ts, w), lambda i: (nt - 1 - i, 0))
    hbm = pl.BlockSpec(memory_space=pl.ANY)
    return pl.pallas_call(
        body, name="bwd_mix", grid=(nt,),
        in_specs=[rev(D), rev(D), rev(DIN), rev(D), rev(D), rev(D), pl.BlockSpec((VD_ROWS, D), lambda i: (0, 0)),
                  hbm, hbm, hbm] + exchange.specs_any,
        out_specs=[rev(DIN), rev(D), rev(D), rev(D), pl.BlockSpec((8, D), lambda i: (0, 0))] + exchange.specs_any,
        out_shape=[jax.ShapeDtypeStruct((s, DIN), BF16)] + [jax.ShapeDtypeStruct((s, D), BF16)] * 3
        + [jax.ShapeDtypeStruct((8, D), F32)] + exchange.out_shape,
        scratch_shapes=[pltpu.VMEM((NG, GW, GW), BF16), pltpu.VMEM((D, D), BF16), pltpu.VMEM((D, D), BF16),
                        pltpu.VMEM((CONV_CARRY, 8, D), F32), pltpu.VMEM((POOL_CARRY, 8, D), F32),
                        pltpu.SemaphoreType.DMA((3,))] + exchange.scratch,
        compiler_params=_params(("arbitrary",)),
    )(dx1, o, proj, cv, ya0, yb, vec_d, w_pool, w_bout, w_o, *ex_grads)


def _bwd_in(dproj, dx1, x, vec_d, w_in, exchange, ex_grads, ts):
    s = x.shape[0]
    nt = s // ts
    n = exchange.n

    def body(*refs):
        ins, grads = refs[:5], refs[5:5 + n]
        outs, recvs = refs[5 + n:7 + n], refs[7 + n:7 + 2 * n]
        scratch, sems = refs[7 + 2 * n:-2], refs[-2:]
        i = pl.program_id(0)
        pl.when(i == 0)(lambda: exchange.start(grads, recvs, *sems))
        compute(*ins, *outs, *scratch)
        pl.when(i == nt - 1)(lambda: exchange.finish(grads, recvs, *sems))

    def compute(dp_ref, dx1_ref, x_ref, v_ref, w_hbm, dx_ref, vo_ref, w_vmem, sem):
        _load_once([(w_hbm, w_vmem)], sem)

        @pl.when(pl.program_id(0) == 0)
        def _():
            vo_ref[...] = jnp.zeros_like(vo_ref)

        dh1 = _dot_nt(dp_ref[...], w_vmem[...])
        xv = _load_rows(x_ref, ts)
        r1 = _rms(xv)
        n1 = xv * r1
        gpre = _row(v_ref, V_GPRE1)
        sc = 1.0 + _row(v_ref, V_SC1)
        vo_ref[0:1, :] += _colsum(dh1)
        s1 = _colsum(dh1 * n1)
        vo_ref[1:2, :] += s1 * gpre
        vo_ref[2:3, :] += s1 * sc
        _store_rows(dx_ref, dx1_ref[...] + _rms_bwd(dh1 * (gpre * sc), n1, r1), ts)

    tile = lambda w: pl.BlockSpec((ts, w), lambda i: (i, 0))
    return pl.pallas_call(
        body, name="bwd_in", grid=(s // ts,),
        in_specs=[tile(DIN), tile(D), tile(D), pl.BlockSpec((VD_ROWS, D), lambda i: (0, 0)),
                  pl.BlockSpec(memory_space=pl.ANY)] + exchange.specs_any,
        out_specs=[tile(D), pl.BlockSpec((8, D), lambda i: (0, 0))] + exchange.specs_any,
        out_shape=[jax.ShapeDtypeStruct((s, D), F32), jax.ShapeDtypeStruct((8, D), F32)] + exchange.out_shape,
        scratch_shapes=[pltpu.VMEM((D, DIN), BF16), pltpu.SemaphoreType.DMA((1,))] + exchange.scratch,
        compiler_params=_params(("arbitrary",)),
    )(dproj, dx1, x, vec_d, w_in, *ex_grads)


def _dot_tn(a, b):
    return lax.dot_general(a, b, (((0,), (0,)), ((), ())), preferred_element_type=F32)


def _wgrad(a, b, tm, tn, ts, name, dtype, exchange=None, ex_grads=()):
    s, m = a.shape
    nn = b.shape[1]
    grid = (m // tm, nn // tn, s // ts)
    n = exchange.n if exchange else 0

    def body(*refs):
        a_ref, b_ref = refs[:2]
        grads = refs[2:2 + n]
        o_ref = refs[2 + n]
        recvs = refs[3 + n:3 + 2 * n]
        acc = refs[3 + 2 * n]
        sems = refs[4 + 2 * n:]
        i, j, k = pl.program_id(0), pl.program_id(1), pl.program_id(2)
        if exchange:
            pl.when((i == 0) & (j == 0) & (k == 0))(lambda: exchange.start(grads, recvs, *sems))
        part = _dot_tn(a_ref[...], b_ref[...])

        @pl.when(k == 0)
        def _():
            acc[...] = part

        @pl.when(k > 0)
        def _():
            acc[...] += part

        @pl.when(k == grid[2] - 1)
        def _():
            o_ref[...] = acc[...].astype(dtype)

        if exchange:
            pl.when((i == grid[0] - 1) & (j == grid[1] - 1) & (k == grid[2] - 1))(
                lambda: exchange.finish(grads, recvs, *sems))

    hosted = exchange.specs_any if exchange else []
    return pl.pallas_call(
        body, name=name, grid=grid,
        in_specs=[pl.BlockSpec((ts, tm), lambda i, j, k: (k, i)), pl.BlockSpec((ts, tn), lambda i, j, k: (k, j))]
        + hosted,
        out_specs=[pl.BlockSpec((tm, tn), lambda i, j, k: (i, j))] + hosted,
        out_shape=[jax.ShapeDtypeStruct((m, nn), dtype)] + (exchange.out_shape if exchange else []),
        scratch_shapes=[pltpu.VMEM((tm, tn), F32)] + (exchange.scratch if exchange else []),
        compiler_params=_params(("arbitrary", "arbitrary", "arbitrary")),
    )(a, b, *ex_grads)


def _wgrad_pool(pg, dya0, ts):
    s = pg.shape[0]
    nk = s // ts

    def body(a_ref, b_ref, o_ref, acc):
        k = pl.program_id(1)
        part = _dot_tn(a_ref[...], b_ref[...])

        @pl.when(k == 0)
        def _():
            acc[...] = part

        @pl.when(k > 0)
        def _():
            acc[...] += part

        @pl.when(k == nk - 1)
        def _():
            o_ref[0] = acc[...].astype(BF16)

    return pl.pallas_call(
        body, name="wgrad_pool", grid=(NG, nk),
        in_specs=[pl.BlockSpec((ts, GW), lambda g, k: (k, g)), pl.BlockSpec((ts, GW), lambda g, k: (k, g))],
        out_specs=pl.BlockSpec((1, GW, GW), lambda g, k: (g, 0, 0)),
        out_shape=jax.ShapeDtypeStruct((NG, GW, GW), BF16),
        scratch_shapes=[pltpu.VMEM((GW, GW), F32)],
        compiler_params=_params(("arbitrary", "arbitrary")),
    )(pg, dya0)


FFN_SLABS_FWD = ((0, 2816),)
FFN_SLABS_BWD = ((0, 1536), (1536, 2816))
TS_PROJ = 512
TS_MIX = 256
TS_FFN = 256
TS_WGRAD = 2048


def _local_step(x, tgt, vec_d, vec_f, placed, place):
    s = x.shape[0]
    tw = min(TS_WGRAD, s)
    sp_in, sp_pool, sp_bout, sp_o, sp_up, sp_down = SHARDED
    proj, h1, w_in, w_pool, w_bout, w_o = _fwd_proj(x, vec_d, placed[0], placed[1:4], place, min(TS_PROJ, s))
    x1, o, pg, q, merged, ya0, yb, cv, w_up, w_down = _fwd_mix(proj, x, vec_d, w_pool, w_bout, w_o, placed[4:],
                                                               min(TS_MIX, s))
    up0, upc, a, h2, dx2, dff, vo_f, loss = _fwd_ffn(x1, tgt, vec_d, vec_f, w_up, w_down, min(TS_FFN, s))
    g_down, = _wgrad(a, dff, F // 2, D, tw, "wgrad_down", BF16)
    dx1, dup0, vo_b, fo, r_down = _bwd_ffn(dff, dx2, x1, up0, upc, vec_d, vec_f, w_up, w_down,
                                           _GradExchange([sp_down]), [g_down], min(TS_FFN, s))
    g_up, = _wgrad(h2, dup0, D, F2 // NCHIP, tw, "wgrad_up", BF16)
    dproj, do, dyb, dya0, vo_m, r_up = _bwd_mix(dx1, o, proj, cv, ya0, yb, vec_d, w_pool, w_bout, w_o,
                                                _GradExchange([sp_up]), [g_up], min(TS_MIX, s))
    g_o, = _wgrad(merged, do, D, D, tw, "wgrad_o", BF16)
    g_bout, = _wgrad(q, dyb, D, D, tw, "wgrad_bout", BF16)
    g_pool = _wgrad_pool(pg, dya0, tw)
    g_in, r_pool, r_bout, r_o = _wgrad(h1, dproj, D, DIN // NCHIP, tw, "wgrad_in", BF16,
                                       _GradExchange([sp_pool, sp_bout, sp_o]), [g_pool, g_bout, g_o])
    dx, vo_i, r_in = _bwd_in(dproj, dx1, x, vec_d, w_in, _GradExchange([sp_in]), [g_in], min(TS_PROJ, s))
    vecs = dict(
        dsh1=vo_i[0], dsc1=vo_i[1], dg_pre_mix=vo_i[2],
        dgt1=vo_m[0], dg_post_mix=vo_m[1], dpool_scale=vo_m[2], dconv_b=vo_m[3],
        dconv_w=vo_m[4:7],
        dsh2=vo_b[0], dsc2=vo_b[1], dg_pre_ffn=vo_b[2],
        dgt2=vo_f[0], dg_post_ffn=vo_f[1],
        dffn_conv_w=fo[FV_W0:FV_W2 + 1], dffn_conv_b=fo[FV_B],
    )
    local = dict(w_in=g_in, w_pool=g_pool, w_bout=g_bout, w_o=g_o, w_up=g_up, w_down=g_down)
    received = dict(w_in=r_in, w_pool=r_pool, w_bout=r_bout, w_o=r_o, w_up=r_up, w_down=r_down)
    return loss, dx, vecs, local, received


def _aligned(offset, n):
    return offset if isinstance(offset, int) else pl.multiple_of(offset, n)


class _Sharded:
    def __init__(self, name, full_shape, shard_axis, half_axis):
        self.name = name
        self.full_shape = full_shape
        self.shard_axis = shard_axis
        self.half_axis = half_axis
        self.shard_shape = tuple(n // NCHIP if a == shard_axis else n for a, n in enumerate(full_shape))
        self.piece_shape = tuple(n // 2 if a == half_axis else n for a, n in enumerate(self.shard_shape))

    def piece(self, full_ref, k, h):
        idx = []
        for a, n in enumerate(self.piece_shape):
            if a == self.shard_axis and a == self.half_axis:
                idx.append(pl.ds(_aligned((2 * k + h) * n, n), n))
            elif a == self.shard_axis:
                idx.append(pl.ds(_aligned(k * n, n), n))
            elif a == self.half_axis:
                idx.append(pl.ds(_aligned(h * n, n), n))
            else:
                idx.append(slice(None))
        return full_ref.at[tuple(idx)]

    def shard(self, full_ref, k):
        n = self.shard_shape[self.shard_axis]
        idx = [pl.ds(_aligned(k * n, n), n) if a == self.shard_axis else slice(None)
               for a in range(len(self.full_shape))]
        return full_ref.at[tuple(idx)]

    def half(self, shard_ref, h):
        n = self.piece_shape[self.half_axis]
        idx = [pl.ds(_aligned(h * n, n), n) if a == self.half_axis else slice(None)
               for a in range(len(self.full_shape))]
        return shard_ref.at[tuple(idx)]

SHARDED = (
    _Sharded("w_in", (D, DIN), 1, 0),
    _Sharded("w_pool", (NG, GW, GW), 1, 0),
    _Sharded("w_bout", (D, D), 0, 0),
    _Sharded("w_o", (D, D), 0, 0),
    _Sharded("w_up", (D, F2), 1, 0),
    _Sharded("w_down", (F, D), 0, 0),
)
NW = len(SHARDED)


def _mesh_place():
    x, y, c = lax.axis_index("x"), lax.axis_index("y"), lax.axis_index("c")
    chips = [(1 - x, y), (x, 1 - y), (1 - x, 1 - y)]
    return x, y, c, 2 * x + y, chips, [2 * px + py for px, py in chips]


def _remote(src, dst, send_sem, recv_sem, device):
    return pltpu.make_async_remote_copy(src_ref=src, dst_ref=dst, send_sem=send_sem, recv_sem=recv_sem,
                                        device_id=device, device_id_type=MESH)


def _all_gather_small(block, name):
    m_per, n = block.shape

    def body(x_ref, out_ref, send_sems, recv_sems, local_sem):
        x, y, c, _, chips, _ = _mesh_place()
        me, sibling = (x, y, c), (x, y, 1 - c)

        def rows(px, py, pc):
            return out_ref.at[pl.ds((4 * px + 2 * py + pc) * m_per, m_per), :]

        def copy(k, blk, to, src=None):
            return _remote(rows(*blk) if src is None else src, rows(*blk), send_sems.at[k], recv_sems.at[k], to)

        mine = pltpu.make_async_copy(x_ref, rows(*me), local_sem)
        mine.start()
        first = [copy(0, me, sibling, src=x_ref)]
        first += [copy(1 + j, me, (*chip, c), src=x_ref) for j, chip in enumerate(chips)]
        for cp in first:
            cp.start()
        passed = [copy(4 + j, (*chip, c), sibling) for j, chip in enumerate(chips)]
        for j, chip in enumerate(chips):
            copy(1 + j, (*chip, c), me).wait_recv()
            passed[j].start()
        copy(0, sibling, me).wait_recv()
        for j, chip in enumerate(chips):
            copy(4 + j, (*chip, 1 - c), me).wait_recv()
        for cp in first + passed:
            cp.wait_send()
        mine.wait()

    return pl.pallas_call(
        body, name=name,
        out_shape=jax.ShapeDtypeStruct((NDEV * m_per, n), block.dtype),
        in_specs=[pl.BlockSpec(memory_space=pltpu.VMEM)],
        out_specs=pl.BlockSpec(memory_space=pltpu.VMEM),
        scratch_shapes=[pltpu.SemaphoreType.DMA((7,)), pltpu.SemaphoreType.DMA((7,)), pltpu.SemaphoreType.DMA],
        compiler_params=pltpu.CompilerParams(vmem_limit_bytes=VMEM_LIMIT),
    )(block)


class _WeightGather:
    def __init__(self, specs):
        self.specs = specs
        self.n = len(specs)
        self.specs_any = [pl.BlockSpec(memory_space=pl.ANY)] * self.n
        self.out_shape = [jax.ShapeDtypeStruct(sp.full_shape, BF16) for sp in specs]
        self.scratch = [pltpu.SemaphoreType.DMA((6 * self.n,)), pltpu.SemaphoreType.DMA((6 * self.n,))]

    def _sends(self, outs, send_sems, recv_sems):
        x, y, c, k_me, chips, _ = _mesh_place()
        sends = []
        for j, chip in enumerate(chips):
            for w, sp in enumerate(self.specs):
                mine = sp.piece(outs[w], k_me, c)
                sends.append(_remote(mine, mine, send_sems.at[6 * w + j], recv_sems.at[6 * w + j], (*chip, c)))
        return sends

    def start(self, outs, send_sems, recv_sems):
        for cp in self._sends(outs, send_sems, recv_sems):
            cp.start()

    def _passes(self, outs, send_sems, recv_sems):
        x, y, c, _, chips, kidx = _mesh_place()
        return [_remote(sp.piece(outs[w], kidx[j], c), sp.piece(outs[w], kidx[j], c),
                        send_sems.at[6 * w + 3 + j], recv_sems.at[6 * w + 3 + j], (x, y, 1 - c))
                for j in range(3) for w, sp in enumerate(self.specs)]

    def forward(self, outs, send_sems, recv_sems):
        x, y, c, _, chips, kidx = _mesh_place()
        for j, chip in enumerate(chips):
            for w, sp in enumerate(self.specs):
                landed = sp.piece(outs[w], kidx[j], c)
                _remote(landed, landed, send_sems.at[6 * w + j], recv_sems.at[6 * w + j], (*chip, c)).wait_recv()
        for cp in self._passes(outs, send_sems, recv_sems):
            cp.start()

    def drain(self, outs, send_sems, recv_sems):
        x, y, c, _, chips, kidx = _mesh_place()
        for j in range(3):
            for w, sp in enumerate(self.specs):
                landed = sp.piece(outs[w], kidx[j], 1 - c)
                _remote(landed, landed, send_sems.at[6 * w + 3 + j], recv_sems.at[6 * w + 3 + j],
                        (x, y, 1 - c)).wait_recv()
        for cp in self._sends(outs, send_sems, recv_sems) + self._passes(outs, send_sems, recv_sems):
            cp.wait_send()

    def finish(self, outs, send_sems, recv_sems):
        self.forward(outs, send_sems, recv_sems)
        self.drain(outs, send_sems, recv_sems)


class _GradExchange:
    def __init__(self, specs):
        self.specs = specs
        self.n = len(specs)
        self.specs_any = [pl.BlockSpec(memory_space=pl.ANY)] * self.n
        self.out_shape = [jax.ShapeDtypeStruct((NDEV,) + sp.piece_shape, BF16) for sp in specs]
        self.scratch = [pltpu.SemaphoreType.DMA((7 * self.n,)), pltpu.SemaphoreType.DMA((NDEV * self.n,))]

    def _sends(self, grads, recvs, send_sems, recv_sems):
        x, y, c, k_me, chips, kidx = _mesh_place()
        dev = 2 * k_me + c
        sends = []
        for w, sp in enumerate(self.specs):
            slot, arrival = recvs[w].at[dev], recv_sems.at[NDEV * w + dev]
            sends.append(_remote(sp.piece(grads[w], k_me, 1 - c), slot, send_sems.at[7 * w], arrival, (x, y, 1 - c)))
            for j, chip in enumerate(chips):
                for h in range(2):
                    sends.append(_remote(sp.piece(grads[w], kidx[j], h), slot, send_sems.at[7 * w + 1 + 2 * j + h],
                                         arrival, (*chip, h)))
        return sends

    def start(self, grads, recvs, send_sems, recv_sems):
        for cp in self._sends(grads, recvs, send_sems, recv_sems):
            cp.start()

    def finish(self, grads, recvs, send_sems, recv_sems):
        x, y, c, k_me, _, _ = _mesh_place()
        dev = 2 * k_me + c
        for w in range(self.n):
            for d in range(NDEV):
                landed = recvs[w].at[d]
                arrival = _remote(landed, landed, send_sems.at[7 * w], recv_sems.at[NDEV * w + d], (x, y, c))
                pl.when(d != dev)(arrival.wait_recv)
        for cp in self._sends(grads, recvs, send_sems, recv_sems):
            cp.wait_send()


def _device_sums(locals_, recvs, place):
    def body(p_ref, *refs):
        a_refs, b_refs, o_refs = refs[:NW], refs[NW:2 * NW], refs[2 * NW:]
        d = pl.program_id(0)
        own = d == p_ref[2]
        terms = [jnp.where(own, a_ref[...], b_ref[...]).astype(F32) for a_ref, b_ref in zip(a_refs, b_refs)]

        @pl.when(d == 0)
        def _():
            for o_ref, term in zip(o_refs, terms):
                o_ref[...] = term

        @pl.when(d > 0)
        def _():
            for o_ref, term in zip(o_refs, terms):
                o_ref[...] += term

    def mine(sp):
        nd = len(sp.piece_shape)
        return pl.BlockSpec(sp.piece_shape, lambda d, p_ref: tuple(
            2 * p_ref[0] + p_ref[1] if a == sp.shard_axis == sp.half_axis else
            p_ref[0] if a == sp.shard_axis else p_ref[1] if a == sp.half_axis else 0 for a in range(nd)))

    def others(sp):
        nd = len(sp.piece_shape)
        return pl.BlockSpec((None,) + sp.piece_shape,
                            lambda d, p_ref: (jnp.where(d == p_ref[2], (d + 1) % NDEV, d),) + (0,) * nd)

    def half(sp):
        nd = len(sp.piece_shape)
        return pl.BlockSpec(sp.piece_shape,
                            lambda d, p_ref: tuple(p_ref[1] if a == sp.half_axis else 0 for a in range(nd)))

    return pl.pallas_call(
        body, name="rs_device_sums",
        grid_spec=pltpu.PrefetchScalarGridSpec(
            num_scalar_prefetch=1, grid=(NDEV,),
            in_specs=[mine(sp) for sp in SHARDED] + [others(sp) for sp in SHARDED],
            out_specs=[half(sp) for sp in SHARDED]),
        out_shape=[jax.ShapeDtypeStruct(sp.shard_shape, F32) for sp in SHARDED],
        compiler_params=_params(("arbitrary",)),
    )(place, *locals_, *recvs)


def _pair_share(halves):
    def body(*refs):
        outs = refs[NW:2 * NW]
        send_sems, recv_sems = refs[2 * NW:]
        x, y, c, _, _, _ = _mesh_place()
        sibling = (x, y, 1 - c)
        sent = []
        for w, sp in enumerate(SHARDED):
            mine = sp.half(outs[w], c)
            cp = _remote(mine, mine, send_sems.at[w], recv_sems.at[w], sibling)
            cp.start()
            sent.append(cp)
        for w, sp in enumerate(SHARDED):
            landed = sp.half(outs[w], 1 - c)
            _remote(landed, landed, send_sems.at[w], recv_sems.at[w], sibling).wait_recv()
        for cp in sent:
            cp.wait_send()

    hbm = pl.BlockSpec(memory_space=pl.ANY)
    return pl.pallas_call(
        body, name="rs_pair_share",
        out_shape=[jax.ShapeDtypeStruct(sp.shard_shape, F32) for sp in SHARDED],
        in_specs=[hbm] * NW, out_specs=[hbm] * NW,
        input_output_aliases={w: w for w in range(NW)},
        scratch_shapes=[pltpu.SemaphoreType.DMA((NW,)), pltpu.SemaphoreType.DMA((NW,))],
    )(*halves)


def _reduce_scatter(local, received, place):
    return _pair_share(_device_sums([local[sp.name] for sp in SHARDED], [received[sp.name] for sp in SHARDED], place))


def _place_bf16(sp, w, place):
    nd = len(sp.full_shape)

    def body(p_ref, w_ref, o_ref):
        o_ref[...] = w_ref[...].astype(BF16)

    return pl.pallas_call(
        body, name="place_" + sp.name,
        grid_spec=pltpu.PrefetchScalarGridSpec(
            num_scalar_prefetch=1, grid=(1,),
            in_specs=[pl.BlockSpec(sp.shard_shape, lambda i, p_ref: (0,) * nd)],
            out_specs=pl.BlockSpec(sp.shard_shape,
                                   lambda i, p_ref: tuple(p_ref[0] if a == sp.shard_axis else 0 for a in range(nd)))),
        out_shape=jax.ShapeDtypeStruct(sp.full_shape, BF16),
        compiler_params=_params(("arbitrary",)),
    )(place, w)


def _matmul_f32(a, b, name):
    def body(a_ref, b_ref, o_ref):
        o_ref[...] = jnp.dot(a_ref[...], b_ref[...], preferred_element_type=F32, precision=lax.Precision.HIGHEST)

    return pl.pallas_call(body, name=name, out_shape=jax.ShapeDtypeStruct((a.shape[0], b.shape[1]), F32),
                          compiler_params=pltpu.CompilerParams(vmem_limit_bytes=VMEM_LIMIT))(a, b)


def _sum_devices(stacked):
    def body(x_ref, o_ref):
        acc = x_ref[0]
        for d in range(1, NDEV):
            acc = acc + x_ref[d]
        o_ref[...] = acc

    return pl.pallas_call(body, name="sum_devices", out_shape=jax.ShapeDtypeStruct(stacked.shape[1:], F32),
                          compiler_params=pltpu.CompilerParams(vmem_limit_bytes=VMEM_LIMIT))(stacked)


ADAMW_STEPS = 8


def _adamw(ws, gs, ms, vs, name):
    n = len(ws)
    steps = ADAMW_STEPS if all(w.shape[0] % (8 * ADAMW_STEPS) == 0 for w in ws) else 1

    def body(*refs):
        ins, outs = refs[:4 * n], refs[4 * n:]
        for k in range(n):
            w_ref, g_ref, m_ref, v_ref = ins[k], ins[n + k], ins[2 * n + k], ins[3 * n + k]
            gv = g_ref[...]
            nm = ADAM_B1 * m_ref[...] + (1.0 - ADAM_B1) * gv
            nv = ADAM_B2 * v_ref[...] + (1.0 - ADAM_B2) * (gv * gv)
            m_hat = nm / (1.0 - ADAM_B1 ** ADAM_STEP)
            v_hat = nv / (1.0 - ADAM_B2 ** ADAM_STEP)
            outs[k][...] = -ADAM_LR * (m_hat / (jnp.sqrt(v_hat) + ADAM_EPS) + ADAM_WD * w_ref[...])
            outs[n + k][...] = nm
            outs[2 * n + k][...] = nv

    blks = [pl.BlockSpec((w.shape[0] // steps, w.shape[1]), lambda i: (i, 0)) for w in ws]
    shapes = [jax.ShapeDtypeStruct(w.shape, F32) for w in ws]
    out = pl.pallas_call(
        body, name="adamw_" + name, grid=(steps,), in_specs=blks * 4, out_specs=blks * 3, out_shape=shapes * 3,
        compiler_params=_params(("parallel",)),
    )(*ws, *gs, *ms, *vs)
    return out[:n], out[n:2 * n], out[2 * n:]


WEIGHT_NAMES = ("g_pre_mix", "g_post_mix", "g_pre_ffn", "g_post_ffn", "w_ada", "b_ada", "w_in", "w_pool",
                "pool_scale", "conv_w", "conv_b", "w_bout", "w_o", "w_up", "ffn_conv_w", "ffn_conv_b", "w_down")
MATRIX_NAMES = ("w_ada",) + tuple(sp.name for sp in SHARDED)
VECTOR_NAMES = tuple(n for n in WEIGHT_NAMES if n not in MATRIX_NAMES)

CW = D // NCHIP
FCW = F2 // NCHIP
ADA_W = DIN // NCHIP
COND_BLOCK = (8, 768)
GRAD_BLOCK = (8, 4864)


def _flat_pad(parts, shape):
    flat = jnp.concatenate([p.reshape(-1) for p in parts])
    return jnp.pad(flat, (0, shape[0] * shape[1] - flat.shape[0])).reshape(shape)


def _take(flat, offset, shape):
    size = 1
    for n in shape:
        size *= n
    return flat[offset:offset + size].reshape(shape), offset + size


def kernel(x, c, g_pre_mix, g_post_mix, g_pre_ffn, g_post_ffn, w_ada, b_ada, w_in, w_pool, pool_scale, conv_w, conv_b, w_bout, w_o, w_up, ffn_conv_w, ffn_conv_b, w_down, loss_target, m_g_pre_mix, m_g_post_mix, m_g_pre_ffn, m_g_post_ffn, m_w_ada, m_b_ada, m_w_in, m_w_pool, m_pool_scale, m_conv_w, m_conv_b, m_w_bout, m_w_o, m_w_up, m_ffn_conv_w, m_ffn_conv_b, m_w_down, v_g_pre_mix, v_g_post_mix, v_g_pre_ffn, v_g_post_ffn, v_w_ada, v_b_ada, v_w_in, v_w_pool, v_pool_scale, v_conv_w, v_conv_b, v_w_bout, v_w_o, v_w_up, v_ffn_conv_w, v_ffn_conv_b, v_w_down):
    weights = dict(g_pre_mix=g_pre_mix, g_post_mix=g_post_mix, g_pre_ffn=g_pre_ffn, g_post_ffn=g_post_ffn,
                   w_ada=w_ada, b_ada=b_ada, w_in=w_in, w_pool=w_pool, pool_scale=pool_scale, conv_w=conv_w,
                   conv_b=conv_b, w_bout=w_bout, w_o=w_o, w_up=w_up, ffn_conv_w=ffn_conv_w, ffn_conv_b=ffn_conv_b,
                   w_down=w_down)
    mom1 = dict(g_pre_mix=m_g_pre_mix, g_post_mix=m_g_post_mix, g_pre_ffn=m_g_pre_ffn, g_post_ffn=m_g_post_ffn,
                w_ada=m_w_ada, b_ada=m_b_ada, w_in=m_w_in, w_pool=m_w_pool, pool_scale=m_pool_scale,
                conv_w=m_conv_w, conv_b=m_conv_b, w_bout=m_w_bout, w_o=m_w_o, w_up=m_w_up,
                ffn_conv_w=m_ffn_conv_w, ffn_conv_b=m_ffn_conv_b, w_down=m_w_down)
    mom2 = dict(g_pre_mix=v_g_pre_mix, g_post_mix=v_g_post_mix, g_pre_ffn=v_g_pre_ffn, g_post_ffn=v_g_post_ffn,
                w_ada=v_w_ada, b_ada=v_b_ada, w_in=v_w_in, w_pool=v_w_pool, pool_scale=v_pool_scale,
                conv_w=v_conv_w, conv_b=v_conv_b, w_bout=v_w_bout, w_o=v_w_o, w_up=v_w_up,
                ffn_conv_w=v_ffn_conv_w, ffn_conv_b=v_ffn_conv_b, w_down=v_w_down)

    chip = 2 * lax.axis_index("x") + lax.axis_index("y")
    core = lax.axis_index("c")
    dev = 2 * chip + core
    place = jnp.stack([chip, core, dev]).astype(jnp.int32)

    cond = _all_gather_small(_flat_pad([c, conv_w, ffn_conv_w], COND_BLOCK), "gather_cond")
    cond = cond.reshape(NDEV, -1)
    c_all = cond[:, :D]
    by_chip = cond[0::2]
    conv_w_full = by_chip[:, D:D + 3 * CW].reshape(NCHIP, 3, CW).transpose(1, 0, 2).reshape(3, D)
    ffn_w_full = by_chip[:, D + 3 * CW:D + 3 * CW + 3 * FCW].reshape(NCHIP, 3, FCW).transpose(1, 0, 2).reshape(3, F2)

    mod_cols = _all_gather_small(_matmul_f32(c_all, w_ada[0], "ada_mod"), "gather_mod")
    mod_cols = mod_cols.reshape(NDEV, NDEV, ADA_W)[0::2]
    mod = lax.dynamic_index_in_dim(mod_cols, dev, axis=1, keepdims=False).reshape(6, D) + b_ada.reshape(6, D)
    vec_d = jnp.concatenate([mod, g_pre_mix, g_post_mix, g_pre_ffn, g_post_ffn, pool_scale, conv_b, conv_w_full,
                             jnp.zeros((VD_ROWS - 15, D), F32)], axis=0)
    vec_f = jnp.concatenate([ffn_w_full, ffn_conv_b, jnp.zeros((FV_ROWS - 4, F2), F32)], axis=0)

    placed = [_place_bf16(sp, weights[sp.name][0], place) for sp in SHARDED]
    loss_blk, dx, vecs, local, received = _local_step(x[0], loss_target[0], vec_d, vec_f, placed, place)

    dmod = [vecs[n] for n in ("dsh1", "dsc1", "dgt1", "dsh2", "dsc2", "dgt2")]
    small = [vecs["dg_pre_mix"], vecs["dg_post_mix"], vecs["dg_pre_ffn"], vecs["dg_post_ffn"]] + dmod + [
        vecs["dpool_scale"], vecs["dconv_w"], vecs["dconv_b"], vecs["dffn_conv_w"], vecs["dffn_conv_b"],
        loss_blk[0]]
    gathered = _all_gather_small(_flat_pad(small, GRAD_BLOCK), "gather_vector_grads")
    total = _sum_devices(gathered.reshape((NDEV,) + GRAD_BLOCK)).reshape(-1)
    vgrad = {}
    off = 0
    for n in ("g_pre_mix", "g_post_mix", "g_pre_ffn", "g_post_ffn"):
        vgrad[n], off = _take(total, off, (1, D))
    dmod_off = off
    vgrad["b_ada"], off = _take(total, off, (1, DIN))
    vgrad["pool_scale"], off = _take(total, off, (1, D))
    g_conv_w, off = _take(total, off, (3, D))
    vgrad["conv_w"] = lax.dynamic_slice_in_dim(g_conv_w, chip * CW, CW, axis=1)[None]
    vgrad["conv_b"], off = _take(total, off, (1, D))
    g_ffn_w, off = _take(total, off, (3, F2))
    vgrad["ffn_conv_w"] = lax.dynamic_slice_in_dim(g_ffn_w, chip * FCW, FCW, axis=1)[None]
    vgrad["ffn_conv_b"], off = _take(total, off, (1, F2))
    loss = total[off]

    dmod_all = gathered.reshape(NDEV, -1)[:, dmod_off:dmod_off + DIN]
    dmod_cols = lax.dynamic_slice_in_dim(dmod_all, chip * ADA_W, ADA_W, axis=1)
    g_ada = _matmul_f32(jnp.pad(c_all.T, ((0, 0), (0, 128 - NDEV))), jnp.pad(dmod_cols, ((0, 128 - NDEV), (0, 0))),
                        "ada_wgrad")

    reduced = _reduce_scatter(local, received, place)
    mgrad = {"w_ada": g_ada}
    for sp, g in zip(SHARDED, reduced):
        mgrad[sp.name] = g

    grad, delta, new_m, new_v = {}, {}, {}, {}
    two_d = lambda tree: [tree[n].reshape(-1, weights[n].shape[-1]) for n in MATRIX_NAMES]
    ds, nms, nvs = _adamw(two_d(weights), two_d(mgrad), two_d(mom1), two_d(mom2), "matrices")
    for n, d, nm, nv in zip(MATRIX_NAMES, ds, nms, nvs):
        shape = weights[n].shape
        grad[n], delta[n], new_m[n], new_v[n] = (a.reshape(shape) for a in (mgrad[n], d, nm, nv))
    flat = lambda tree: [jnp.concatenate([tree[n].reshape(1, -1) for n in VECTOR_NAMES], axis=1)]
    (d,), (nm,), (nv,) = _adamw(flat(weights), flat(vgrad), flat(mom1), flat(mom2), "vectors")
    off = 0
    for n in VECTOR_NAMES:
        shape = weights[n].shape
        grad[n] = vgrad[n].reshape(shape)
        delta[n], _ = _take(d[0], off, shape)
        new_m[n], _ = _take(nm[0], off, shape)
        new_v[n], off = _take(nv[0], off, shape)

    return (loss, dx[None], *[grad[n] for n in WEIGHT_NAMES], *[delta[n] for n in WEIGHT_NAMES],
            *[new_m[n] for n in WEIGHT_NAMES], *[new_v[n] for n in WEIGHT_NAMES])
```

```python
import jax
import jax.numpy as jnp
from jax import lax
from jax.experimental import pallas as pl
from jax.experimental.pallas import tpu as pltpu

F32 = jnp.float32
BF16 = jnp.bfloat16

D = 1024
DIN = 6 * D
F = 2816
F2 = 2 * F
NG = 4
GW = D // NG
POOL_CARRY = 16
CONV_CARRY = 3
EPS = 1e-6
NCHIP = 4
NDEV = 8

ADAM_LR = 0.001
ADAM_B1 = 0.9
ADAM_B2 = 0.999
ADAM_EPS = 1e-08
ADAM_WD = 0.01
ADAM_STEP = 10

VMEM_LIMIT = 60 * 1024 * 1024

(V_SH1, V_SC1, V_GT1, V_SH2, V_SC2, V_GT2, V_GPRE1, V_GPOST1, V_GPRE2, V_GPOST2,
 V_PSCALE, V_CB, V_CW0, V_CW1, V_CW2) = range(15)
VD_ROWS = 16
FV_W0, FV_W1, FV_W2, FV_B = range(4)
FV_ROWS = 8

MESH = pl.DeviceIdType.MESH


def _params(sem=None, vmem=VMEM_LIMIT):
    return pltpu.CompilerParams(dimension_semantics=sem, vmem_limit_bytes=vmem)


def _row(ref, r):
    return ref[r:r + 1, :]


def _load_once(pairs, sem):
    @pl.when(pl.program_id(0) == 0)
    def _():
        copies = [pltpu.make_async_copy(src, dst, sem.at[n]) for n, (src, dst) in enumerate(pairs)]
        for cp in copies:
            cp.start()
        for cp in copies:
            cp.wait()


def _dot(a, b):
    return jnp.dot(a, b, preferred_element_type=F32)


def _dot_nt(a, b):
    return lax.dot_general(a, b, (((1,), (1,)), ((), ())), preferred_element_type=F32)


BLK = 256
SEG = BLK // 8


def _load_rows(ref, ts):
    blocks = [jnp.swapaxes(ref[b * BLK:(b + 1) * BLK, :].reshape(8, SEG, ref.shape[-1]), 0, 1).reshape(BLK, -1)
              for b in range(ts // BLK)]
    return jnp.concatenate(blocks, axis=0)


def _store_rows(ref, val, ts):
    for b in range(ts // BLK):
        blk = val[b * BLK:(b + 1) * BLK, :].reshape(SEG, 8, val.shape[-1])
        ref[b * BLK:(b + 1) * BLK, :] = jnp.swapaxes(blk, 0, 1).reshape(BLK, -1)


def _times(t0):
    p = lax.broadcasted_iota(jnp.int32, (BLK, 1), 0)
    return t0 + (p & 7) * SEG + (p >> 3)


def _before(x, carry, s):
    x3 = x.reshape(SEG, 8, x.shape[-1])
    tail = pltpu.roll(x3[SEG - s:], 1, 1)
    row = lax.broadcasted_iota(jnp.int32, tail.shape, 1)
    out = jnp.concatenate([jnp.where(row == 0, carry, tail), x3[:SEG - s]], axis=0)
    return out.reshape(x.shape), tail


def _after(x, carry, s):
    x3 = x.reshape(SEG, 8, x.shape[-1])
    head = pltpu.roll(x3[:s], 7, 1)
    row = lax.broadcasted_iota(jnp.int32, head.shape, 1)
    out = jnp.concatenate([x3[s:], jnp.where(row == 7, carry, head)], axis=0)
    return out.reshape(x.shape), head


def _causal_conv(x, carry, cols, w0, w1, w2, b):
    x1, carry[0:1, :, cols] = _before(x, carry[0:1, :, cols], 1)
    x2, carry[1:3, :, cols] = _before(x, carry[1:3, :, cols], 2)
    return b + w2 * x + w1 * x1 + w0 * x2


def _causal_conv_bwd(dy, carry, cols, w0, w1, w2):
    d1, carry[0:1, :, cols] = _after(dy, carry[0:1, :, cols], 1)
    d2, carry[1:3, :, cols] = _after(dy, carry[1:3, :, cols], 2)
    return w2 * dy + w1 * d1 + w0 * d2, d1, d2


def _pool_counts(t0, g):
    return jnp.minimum((_times(t0) + 1).astype(F32), float(2 << g))


def _rms(x):
    return lax.rsqrt(jnp.mean(x * x, axis=-1, keepdims=True) + EPS)


def _rms_bwd(dn, n, r):
    return r * (dn - n * jnp.mean(dn * n, axis=-1, keepdims=True))


def _colsum(x):
    return jnp.sum(x, axis=0, keepdims=True)


def _gelu_and_grad(x):
    k, a = 0.7978845608028654, 0.044715
    x2 = x * x
    th1 = 1.0 + jnp.tanh(x * (x2 * (k * a) + k))
    hx = 0.5 * x
    gelu = hx * th1
    dgelu = 0.5 * th1 + (hx * (th1 * (2.0 - th1))) * (x2 * (3.0 * k * a) + k)
    return gelu, dgelu


def _fwd_proj(x, vec_d, placed_in, placed_rest, place, ts):
    s = x.shape[0]
    nt = s // ts
    cw = DIN // NCHIP
    sp_in = SHARDED[0]
    gather = _WeightGather(SHARDED[1:4])
    n = gather.n

    def body(*refs):
        p_ref, x_ref, v_ref = refs[:3]
        proj_ref, h1_ref, w_full = refs[4 + n:7 + n]
        rest = refs[7 + n:7 + 2 * n]
        w_vmem, h1_all, sem, in_send, in_recv, send_sems, recv_sems = refs[7 + 2 * n:]
        j, i = pl.program_id(0), pl.program_id(1)
        x_, y_, c, k_me, _, _ = _mesh_place()
        sibling = (x_, y_, 1 - c)

        def peer(t):
            return (x_ ^ (t >> 1), y_ ^ (t & 1))

        def w_in_sends():
            mine = sp_in.piece(w_full, k_me, c)
            return [_remote(mine, mine, in_send.at[t - 1], in_recv.at[t - 1], (*peer(t), c)) for t in (1, 2, 3)]

        def load_block(k):
            cp = pltpu.make_async_copy(sp_in.shard(w_full, k), w_vmem.at[k], sem.at[0])
            cp.start()
            cp.wait()

        @pl.when((j == 0) & (i == 0))
        def _():
            for cp in w_in_sends()[:2]:
                cp.start()
            load_block(k_me)

        @pl.when((j == 1) & (i == 0))
        def _():
            for cp in w_in_sends()[:2]:
                cp.wait_send()
            w_in_sends()[2].start()
            gather.start(rest, send_sems, recv_sems)

        for t in (1, 2, 3):
            @pl.when((j == t) & (i == 0))
            def _(t=t):
                k = k_me ^ t
                landed = sp_in.piece(w_full, k, c)
                _remote(landed, landed, in_send.at[t - 1], in_recv.at[t - 1], (*peer(t), c)).wait_recv()
                _remote(landed, landed, in_send.at[2 + t], in_recv.at[2 + t], sibling).start()
                other = sp_in.piece(w_full, k, 1 - c)
                _remote(other, other, in_send.at[2 + t], in_recv.at[2 + t], sibling).wait_recv()
                load_block(k)

        @pl.when(j == 0)
        def _():
            xv = _load_rows(x_ref, ts)
            n1 = xv * _rms(xv)
            h = n1 * (_row(v_ref, V_GPRE1) * (1.0 + _row(v_ref, V_SC1))) + _row(v_ref, V_SH1)
            hb = h.astype(BF16)
            h1_ref[...] = hb
            h1_all[i] = hb

        proj_ref[...] = _dot(h1_all[i], w_vmem[k_me ^ j]).astype(BF16)

        @pl.when((j == NCHIP - 1) & (i == nt - 1))
        def _():
            w_in_sends()[2].wait_send()
            for t in (1, 2, 3):
                landed = sp_in.piece(w_full, k_me ^ t, c)
                _remote(landed, landed, in_send.at[2 + t], in_recv.at[2 + t], sibling).wait_send()
            gather.finish(rest, send_sems, recv_sems)

    once = lambda w: pl.BlockSpec((ts, w), lambda j, i, p: (jnp.where(j == 0, i, nt - 1), 0))
    return pl.pallas_call(
        body, name="fwd_proj",
        grid_spec=pltpu.PrefetchScalarGridSpec(
            num_scalar_prefetch=1, grid=(NCHIP, nt),
            in_specs=[once(D), pl.BlockSpec((VD_ROWS, D), lambda j, i, p: (0, 0)),
                      pl.BlockSpec(memory_space=pl.ANY)] + gather.specs_any,
            out_specs=[pl.BlockSpec((ts, cw), lambda j, i, p: (i, p[0] ^ j)), once(D),
                       pl.BlockSpec(memory_space=pl.ANY)] + gather.specs_any,
            scratch_shapes=[pltpu.VMEM((NCHIP, D, cw), BF16), pltpu.VMEM((nt, ts, D), BF16),
                            pltpu.SemaphoreType.DMA((1,)),
                            pltpu.SemaphoreType.DMA((6,)), pltpu.SemaphoreType.DMA((6,))] + gather.scratch),
        out_shape=[jax.ShapeDtypeStruct((s, DIN), BF16), jax.ShapeDtypeStruct((s, D), BF16),
                   jax.ShapeDtypeStruct(sp_in.full_shape, BF16)] + gather.out_shape,
        input_output_aliases={3 + w: 2 + w for w in range(n + 1)},
        compiler_params=_params(("arbitrary", "arbitrary")),
    )(place, x, vec_d, placed_in, *placed_rest)


def _fwd_mix(proj, x, vec_d, w_pool, w_bout, w_o, placed_ffn, ts):
    s = x.shape[0]
    gather = _WeightGather(SHARDED[4:])
    n = gather.n

    def body(*refs):
        ins, outs, rest = refs[:6], refs[6 + n:14 + n], refs[14 + n:14 + 2 * n]
        scratch, sems = refs[14 + 2 * n:-2], refs[-2:]
        i = pl.program_id(0)
        nt = s // ts
        pl.when(i == 0)(lambda: gather.start(rest, *sems))
        pl.when(i == nt - 1 - nt // 8)(lambda: gather.forward(rest, *sems))
        compute(*ins, *outs, *scratch)
        pl.when(i == nt - 1)(lambda: gather.drain(rest, *sems))

    def compute(p_ref, x_ref, v_ref, wp_hbm, wb_hbm, wo_hbm,
                x1_ref, o_ref, pg_ref, q_ref, mg_ref, ya_ref, yb_ref, cv_ref,
                wp, wb, wo, carry_p, carry_v, sem):
        i = pl.program_id(0)
        _load_once([(wp_hbm, wp), (wb_hbm, wb), (wo_hbm, wo)], sem)

        @pl.when(i == 0)
        def _():
            carry_p[...] = jnp.zeros_like(carry_p)
            carry_v[...] = jnp.zeros_like(carry_v)

        t0 = i * ts
        for g in range(NG):
            cols = slice(g * GW, (g + 1) * GW)
            u = p_ref[:, cols].astype(F32)
            e = u
            for l in range(g + 1):
                slot = slice((1 << l) - 1, (2 << l) - 1)
                shifted, carry_p[slot, :, cols] = _before(e, carry_p[slot, :, cols], 1 << l)
                e = e + shifted
            pgb = (e / _pool_counts(t0, g) - u).astype(BF16)
            pg_ref[:, cols] = pgb
            ya_ref[:, cols] = _dot(pgb, wp[g]).astype(BF16)

        u_x = p_ref[:, D:2 * D].astype(F32)
        u_c = p_ref[:, 3 * D:4 * D].astype(F32)
        v = u_c * u_x
        cv = _causal_conv(v, carry_v, slice(None), _row(v_ref, V_CW0), _row(v_ref, V_CW1),
                          _row(v_ref, V_CW2), _row(v_ref, V_CB))
        cv_ref[...] = cv.astype(BF16)
        q = (p_ref[:, 2 * D:3 * D].astype(F32) * cv).astype(BF16)
        q_ref[...] = q
        y_b = _dot(q, wb[...])
        yb_ref[...] = y_b.astype(BF16)

        y_a = ya_ref[...].astype(F32) * _row(v_ref, V_PSCALE)
        merged = (jax.nn.sigmoid(p_ref[:, 4 * D:5 * D].astype(F32)) * y_a
                  + jax.nn.sigmoid(p_ref[:, 5 * D:6 * D].astype(F32)) * y_b).astype(BF16)
        mg_ref[...] = merged
        o = _dot(merged, wo[...])
        o_ref[...] = o.astype(BF16)
        x1_ref[...] = _load_rows(x_ref, ts) + _row(v_ref, V_GT1) * ((o * _rms(o)) * _row(v_ref, V_GPOST1))

    tile = lambda w: pl.BlockSpec((ts, w), lambda i: (i, 0))
    hbm = pl.BlockSpec(memory_space=pl.ANY)
    return pl.pallas_call(
        body, name="fwd_mix", grid=(s // ts,),
        in_specs=[tile(DIN), tile(D), pl.BlockSpec((VD_ROWS, D), lambda i: (0, 0)), hbm, hbm, hbm] + gather.specs_any,
        out_specs=[tile(D)] * 8 + gather.specs_any,
        out_shape=[jax.ShapeDtypeStruct((s, D), F32)] + [jax.ShapeDtypeStruct((s, D), BF16)] * 7 + gather.out_shape,
        input_output_aliases={6 + w: 8 + w for w in range(n)},
        scratch_shapes=[pltpu.VMEM((NG, GW, GW), BF16), pltpu.VMEM((D, D), BF16), pltpu.VMEM((D, D), BF16),
                        pltpu.VMEM((POOL_CARRY, 8, D), F32), pltpu.VMEM((CONV_CARRY, 8, D), F32),
                        pltpu.SemaphoreType.DMA((3,))] + gather.scratch,
        compiler_params=_params(("arbitrary",)),
    )(proj, x, vec_d, w_pool, w_bout, w_o, *placed_ffn)


def _fwd_ffn(x1, tgt, vec_d, vec_f, w_up, w_down, ts):
    s = x1.shape[0]

    def body(x1_ref, t_ref, v_ref, f_ref, wu_hbm, wd_hbm,
             up_ref, upc_ref, a_ref, h2_ref, dx2_ref, dff_ref, vo_ref, loss_ref,
             wu, wd, carry, sem):
        i = pl.program_id(0)
        _load_once([(wu_hbm, wu), (wd_hbm, wd)], sem)

        @pl.when(i == 0)
        def _():
            carry[...] = jnp.zeros_like(carry)
            vo_ref[...] = jnp.zeros_like(vo_ref)
            loss_ref[...] = jnp.zeros_like(loss_ref)

        x1v = x1_ref[...]
        n3 = x1v * _rms(x1v)
        h2 = (n3 * (_row(v_ref, V_GPRE2) * (1.0 + _row(v_ref, V_SC2))) + _row(v_ref, V_SH2)).astype(BF16)
        h2_ref[...] = h2

        ff = jnp.zeros((ts, D), F32)
        for lo, hi in FFN_SLABS_FWD:
            up = []
            for cols in (slice(lo, hi), slice(F + lo, F + hi)):
                u0 = _dot(h2, wu[:, cols])
                up_ref[:, cols] = u0.astype(BF16)
                y = _causal_conv(u0, carry, cols, f_ref[FV_W0:FV_W0 + 1, cols], f_ref[FV_W1:FV_W1 + 1, cols],
                                 f_ref[FV_W2:FV_W2 + 1, cols], f_ref[FV_B:FV_B + 1, cols])
                upc_ref[:, cols] = y.astype(BF16)
                up.append(y)
            gelu, _ = _gelu_and_grad(up[0])
            a = (gelu * up[1]).astype(BF16)
            a_ref[:, lo:hi] = a
            ff = ff + _dot(a, wd[lo:hi, :])

        r4 = _rms(ff)
        n4 = ff * r4
        gt2 = _row(v_ref, V_GT2)
        gpost = _row(v_ref, V_GPOST2)
        gate_gain = gt2 * gpost
        diff = (x1v + gate_gain * n4) - _load_rows(t_ref, ts)
        loss_ref[...] += jnp.full(loss_ref.shape, 0.5 / D * jnp.sum(diff * diff), F32)
        dx2_ref[...] = diff * (1.0 / D)
        s1 = _colsum(diff * n4)
        vo_ref[0:1, :] += s1 * (gpost * (1.0 / D))
        vo_ref[1:2, :] += s1 * (gt2 * (1.0 / D))
        dff_ref[...] = _rms_bwd(diff * (gate_gain * (1.0 / D)), n4, r4).astype(BF16)

    tile = lambda w: pl.BlockSpec((ts, w), lambda i: (i, 0))
    full = lambda r, w: pl.BlockSpec((r, w), lambda i: (0, 0))
    hbm = pl.BlockSpec(memory_space=pl.ANY)
    return pl.pallas_call(
        body, name="fwd_ffn", grid=(s // ts,),
        in_specs=[tile(D), tile(D), full(VD_ROWS, D), full(FV_ROWS, F2), hbm, hbm],
        out_specs=[tile(F2), tile(F2), tile(F), tile(D), tile(D), tile(D), full(8, D), full(8, 128)],
        out_shape=[jax.ShapeDtypeStruct((s, F2), BF16), jax.ShapeDtypeStruct((s, F2), BF16),
                   jax.ShapeDtypeStruct((s, F), BF16),
                   jax.ShapeDtypeStruct((s, D), BF16), jax.ShapeDtypeStruct((s, D), F32),
                   jax.ShapeDtypeStruct((s, D), BF16), jax.ShapeDtypeStruct((8, D), F32),
                   jax.ShapeDtypeStruct((8, 128), F32)],
        scratch_shapes=[pltpu.VMEM((D, F2), BF16), pltpu.VMEM((F, D), BF16), pltpu.VMEM((CONV_CARRY, 8, F2), F32),
                        pltpu.SemaphoreType.DMA((2,))],
        compiler_params=_params(("arbitrary",)),
    )(x1, tgt, vec_d, vec_f, w_up, w_down)


def _bwd_ffn(dff, dx2, x1, up0, upc, vec_d, vec_f, w_up, w_down, exchange, ex_grads, ts):
    s = x1.shape[0]
    nt = s // ts
    n = exchange.n

    def body(*refs):
        ins, grads = refs[:9], refs[9:9 + n]
        outs, recvs = refs[9 + n:13 + n], refs[13 + n:13 + 2 * n]
        scratch, sems = refs[13 + 2 * n:-2], refs[-2:]
        i = pl.program_id(0)
        pl.when(i == 0)(lambda: exchange.start(grads, recvs, *sems))
        compute(*ins, *outs, *scratch)
        pl.when(i == nt - 1)(lambda: exchange.finish(grads, recvs, *sems))

    def compute(dff_ref, dx2_ref, x1_ref, up_ref, upc_ref, v_ref, f_ref, wu_hbm, wd_hbm,
                dx1_ref, dup_ref, vo_ref, fo_ref, wu, wd, carry, sem):
        i = pl.program_id(0)
        _load_once([(wu_hbm, wu), (wd_hbm, wd)], sem)

        @pl.when(i == 0)
        def _():
            carry[...] = jnp.zeros_like(carry)
            vo_ref[...] = jnp.zeros_like(vo_ref)
            fo_ref[...] = jnp.zeros_like(fo_ref)

        dffb = dff_ref[...]

        dh2 = jnp.zeros((ts, D), F32)
        for lo, hi in FFN_SLABS_BWD:
            slabs = (slice(lo, hi), slice(F + lo, F + hi))
            gelu, dgelu = _gelu_and_grad(upc_ref[:, slabs[0]].astype(F32))
            da = _dot_nt(dffb, wd[lo:hi, :])
            dups = (da * upc_ref[:, slabs[1]].astype(F32) * dgelu, da * gelu)
            for cols, dup in zip(slabs, dups):
                du0, d1, d2 = _causal_conv_bwd(dup, carry, cols, f_ref[FV_W0:FV_W0 + 1, cols],
                                               f_ref[FV_W1:FV_W1 + 1, cols], f_ref[FV_W2:FV_W2 + 1, cols])
                u0 = up_ref[:, cols].astype(F32)
                fo_ref[FV_B:FV_B + 1, cols] += _colsum(dup)
                fo_ref[FV_W2:FV_W2 + 1, cols] += _colsum(dup * u0)
                fo_ref[FV_W1:FV_W1 + 1, cols] += _colsum(d1 * u0)
                fo_ref[FV_W0:FV_W0 + 1, cols] += _colsum(d2 * u0)
                du0 = du0.astype(BF16)
                dup_ref[:, cols] = du0
                dh2 = dh2 + _dot_nt(du0, wu[:, cols])

        x1v = x1_ref[...]
        r3 = _rms(x1v)
        n3 = x1v * r3
        gpre = _row(v_ref, V_GPRE2)
        sc = 1.0 + _row(v_ref, V_SC2)
        vo_ref[0:1, :] += _colsum(dh2)
        s2 = _colsum(dh2 * n3)
        vo_ref[1:2, :] += s2 * gpre
        vo_ref[2:3, :] += s2 * sc
        dx1_ref[...] = dx2_ref[...] + _rms_bwd(dh2 * (gpre * sc), n3, r3)

    rev = lambda w: pl.BlockSpec((ts, w), lambda i: (nt - 1 - i, 0))
    full = lambda r, w: pl.BlockSpec((r, w), lambda i: (0, 0))
    hbm = pl.BlockSpec(memory_space=pl.ANY)
    return pl.pallas_call(
        body, name="bwd_ffn", grid=(nt,),
        in_specs=[rev(D), rev(D), rev(D), rev(F2), rev(F2), full(VD_ROWS, D), full(FV_ROWS, F2), hbm, hbm]
        + exchange.specs_any,
        out_specs=[rev(D), rev(F2), full(8, D), full(FV_ROWS, F2)] + exchange.specs_any,
        out_shape=[jax.ShapeDtypeStruct((s, D), F32), jax.ShapeDtypeStruct((s, F2), BF16),
                   jax.ShapeDtypeStruct((8, D), F32), jax.ShapeDtypeStruct((FV_ROWS, F2), F32)] + exchange.out_shape,
        scratch_shapes=[pltpu.VMEM((D, F2), BF16), pltpu.VMEM((F, D), BF16), pltpu.VMEM((CONV_CARRY, 8, F2), F32),
                        pltpu.SemaphoreType.DMA((2,))] + exchange.scratch,
        compiler_params=_params(("arbitrary",)),
    )(dff, dx2, x1, up0, upc, vec_d, vec_f, w_up, w_down, *ex_grads)


def _bwd_mix(dx1, o, proj, cv, ya0, yb, merged, q, pg, vec_d, w_pool, w_bout, w_o, exchange, ex_grads, ts):
    s = dx1.shape[0]
    nt = s // ts
    n = exchange.n

    def body(*refs):
        ins, grads = refs[:13], refs[13:13 + n]
        outs, recvs = refs[13 + n:18 + n], refs[18 + n:18 + 2 * n]
        scratch, sems = refs[18 + 2 * n:-2], refs[-2:]
        i = pl.program_id(0)
        pl.when(i == 0)(lambda: exchange.start(grads, recvs, *sems))
        compute(*ins, *outs, *scratch)
        pl.when(i == nt - 1)(lambda: exchange.finish(grads, recvs, *sems))

    def compute(dx1_ref, o_ref, p_ref, cv_ref, ya_ref, yb_ref, mg_ref, q_ref, pg_ref, v_ref, wp_hbm, wb_hbm, wo_hbm,
                dp_ref, vo_ref, go_ref, gb_ref, gp_ref, wp, wb, wo, carry_d, carry_e, acc_o, acc_b, acc_p, sem):
        i = pl.program_id(0)
        _load_once([(wp_hbm, wp), (wb_hbm, wb), (wo_hbm, wo)], sem)

        @pl.when(i == 0)
        def _():
            carry_d[...] = jnp.zeros_like(carry_d)
            carry_e[...] = jnp.zeros_like(carry_e)
            vo_ref[...] = jnp.zeros_like(vo_ref)
            acc_o[...] = jnp.zeros_like(acc_o)
            acc_b[...] = jnp.zeros_like(acc_b)
            acc_p[...] = jnp.zeros_like(acc_p)

        t0 = (nt - 1 - i) * ts
        dx1v = dx1_ref[...]
        ov = o_ref[...].astype(F32)
        r2 = _rms(ov)
        n2 = ov * r2
        gpost = _row(v_ref, V_GPOST1)
        gt1 = _row(v_ref, V_GT1)
        s1 = _colsum(dx1v * n2)
        vo_ref[0:1, :] += s1 * gpost
        vo_ref[1:2, :] += s1 * gt1
        dob = _rms_bwd(dx1v * (gt1 * gpost), n2, r2).astype(BF16)
        acc_o[...] += _dot_tn(mg_ref[...], dob)
        dmerged = _dot_nt(dob, wo[...])

        ya0 = ya_ref[...].astype(F32)
        pscale = _row(v_ref, V_PSCALE)
        sa = jax.nn.sigmoid(p_ref[:, 4 * D:5 * D].astype(F32))
        dp_ref[:, 4 * D:5 * D] = (dmerged * (ya0 * pscale) * sa * (1.0 - sa)).astype(BF16)
        dy_a = dmerged * sa
        vo_ref[2:3, :] += _colsum(dy_a * ya0)
        dya0 = (dy_a * pscale).astype(BF16)

        sb = jax.nn.sigmoid(p_ref[:, 5 * D:6 * D].astype(F32))
        dp_ref[:, 5 * D:6 * D] = (dmerged * yb_ref[...].astype(F32) * sb * (1.0 - sb)).astype(BF16)
        dy_b = (dmerged * sb).astype(BF16)
        acc_b[...] += _dot_tn(q_ref[...], dy_b)
        dq = _dot_nt(dy_b, wb[...])

        u_x = p_ref[:, D:2 * D].astype(F32)
        u_b = p_ref[:, 2 * D:3 * D].astype(F32)
        u_c = p_ref[:, 3 * D:4 * D].astype(F32)
        w0, w1, w2 = _row(v_ref, V_CW0), _row(v_ref, V_CW1), _row(v_ref, V_CW2)
        dp_ref[:, 2 * D:3 * D] = (dq * cv_ref[...].astype(F32)).astype(BF16)
        dcv = dq * u_b
        dv, d1, d2 = _causal_conv_bwd(dcv, carry_d, slice(None), w0, w1, w2)
        v = u_c * u_x
        vo_ref[3:4, :] += _colsum(dcv)
        vo_ref[4:5, :] += _colsum(d2 * v)
        vo_ref[5:6, :] += _colsum(d1 * v)
        vo_ref[6:7, :] += _colsum(dcv * v)
        dp_ref[:, D:2 * D] = (dv * u_c).astype(BF16)
        dp_ref[:, 3 * D:4 * D] = (dv * u_x).astype(BF16)

        for g in range(NG):
            cols = slice(g * GW, (g + 1) * GW)
            acc_p[g] += _dot_tn(pg_ref[:, cols], dya0[:, cols])
            dpg = _dot_nt(dya0[:, cols], wp[g])
            e = dpg / _pool_counts(t0, g)
            for l in range(g + 1):
                slot = slice((1 << l) - 1, (2 << l) - 1)
                shifted, carry_e[slot, :, cols] = _after(e, carry_e[slot, :, cols], 1 << l)
                e = e + shifted
            dp_ref[:, cols] = (e - dpg).astype(BF16)

        @pl.when(i == nt - 1)
        def _():
            go_ref[...] = acc_o[...].astype(BF16)
            gb_ref[...] = acc_b[...].astype(BF16)
            gp_ref[...] = acc_p[...].astype(BF16)

    rev = lambda w: pl.BlockSpec((ts, w), lambda i: (nt - 1 - i, 0))
    hbm = pl.BlockSpec(memory_space=pl.ANY)
    whole = lambda shape: pl.BlockSpec(shape, lambda i: (0,) * len(shape))
    return pl.pallas_call(
        body, name="bwd_mix", grid=(nt,),
        in_specs=[rev(D), rev(D), rev(DIN)] + [rev(D)] * 6 + [whole((VD_ROWS, D)), hbm, hbm, hbm] + exchange.specs_any,
        out_specs=[rev(DIN), whole((8, D)), whole((D, D)), whole((D, D)), whole((NG, GW, GW))] + exchange.specs_any,
        out_shape=[jax.ShapeDtypeStruct((s, DIN), BF16), jax.ShapeDtypeStruct((8, D), F32),
                   jax.ShapeDtypeStruct((D, D), BF16), jax.ShapeDtypeStruct((D, D), BF16),
                   jax.ShapeDtypeStruct((NG, GW, GW), BF16)] + exchange.out_shape,
        scratch_shapes=[pltpu.VMEM((NG, GW, GW), BF16), pltpu.VMEM((D, D), BF16), pltpu.VMEM((D, D), BF16),
                        pltpu.VMEM((CONV_CARRY, 8, D), F32), pltpu.VMEM((POOL_CARRY, 8, D), F32),
                        pltpu.VMEM((D, D), F32), pltpu.VMEM((D, D), F32), pltpu.VMEM((NG, GW, GW), F32),
                        pltpu.SemaphoreType.DMA((3,))] + exchange.scratch,
        compiler_params=_params(("arbitrary",)),
    )(dx1, o, proj, cv, ya0, yb, merged, q, pg, vec_d, w_pool, w_bout, w_o, *ex_grads)


def _bwd_in(dproj, dx1, x, vec_d, w_in, exchange, ex_grads, ts):
    s = x.shape[0]
    nt = s // ts
    n = exchange.n

    def body(*refs):
        ins, grads = refs[:5], refs[5:5 + n]
        outs, recvs = refs[5 + n:7 + n], refs[7 + n:7 + 2 * n]
        scratch, sems = refs[7 + 2 * n:-2], refs[-2:]
        i = pl.program_id(0)
        pl.when(i == 0)(lambda: exchange.start(grads, recvs, *sems))
        compute(*ins, *outs, *scratch)
        pl.when(i == nt - 1)(lambda: exchange.finish(grads, recvs, *sems))

    def compute(dp_ref, dx1_ref, x_ref, v_ref, w_hbm, dx_ref, vo_ref, w_vmem, sem):
        _load_once([(w_hbm, w_vmem)], sem)

        @pl.when(pl.program_id(0) == 0)
        def _():
            vo_ref[...] = jnp.zeros_like(vo_ref)

        dh1 = _dot_nt(dp_ref[...], w_vmem[...])
        xv = _load_rows(x_ref, ts)
        r1 = _rms(xv)
        n1 = xv * r1
        gpre = _row(v_ref, V_GPRE1)
        sc = 1.0 + _row(v_ref, V_SC1)
        vo_ref[0:1, :] += _colsum(dh1)
        s1 = _colsum(dh1 * n1)
        vo_ref[1:2, :] += s1 * gpre
        vo_ref[2:3, :] += s1 * sc
        _store_rows(dx_ref, dx1_ref[...] + _rms_bwd(dh1 * (gpre * sc), n1, r1), ts)

    tile = lambda w: pl.BlockSpec((ts, w), lambda i: (i, 0))
    return pl.pallas_call(
        body, name="bwd_in", grid=(s // ts,),
        in_specs=[tile(DIN), tile(D), tile(D), pl.BlockSpec((VD_ROWS, D), lambda i: (0, 0)),
                  pl.BlockSpec(memory_space=pl.ANY)] + exchange.specs_any,
        out_specs=[tile(D), pl.BlockSpec((8, D), lambda i: (0, 0))] + exchange.specs_any,
        out_shape=[jax.ShapeDtypeStruct((s, D), F32), jax.ShapeDtypeStruct((8, D), F32)] + exchange.out_shape,
        scratch_shapes=[pltpu.VMEM((D, DIN), BF16), pltpu.SemaphoreType.DMA((1,))] + exchange.scratch,
        compiler_params=_params(("arbitrary",)),
    )(dproj, dx1, x, vec_d, w_in, *ex_grads)


def _dot_tn(a, b):
    return lax.dot_general(a, b, (((0,), (0,)), ((), ())), preferred_element_type=F32)


def _wgrad(a, b, tm, tn, ts, name, dtype, exchange=None, ex_grads=()):
    s, m = a.shape
    nn = b.shape[1]
    grid = (m // tm, nn // tn, s // ts)
    n = exchange.n if exchange else 0

    def body(*refs):
        a_ref, b_ref = refs[:2]
        grads = refs[2:2 + n]
        o_ref = refs[2 + n]
        recvs = refs[3 + n:3 + 2 * n]
        acc = refs[3 + 2 * n]
        sems = refs[4 + 2 * n:]
        i, j, k = pl.program_id(0), pl.program_id(1), pl.program_id(2)
        if exchange:
            pl.when((i == 0) & (j == 0) & (k == 0))(lambda: exchange.start(grads, recvs, *sems))
        part = _dot_tn(a_ref[...], b_ref[...])

        @pl.when(k == 0)
        def _():
            acc[...] = part

        @pl.when(k > 0)
        def _():
            acc[...] += part

        @pl.when(k == grid[2] - 1)
        def _():
            o_ref[...] = acc[...].astype(dtype)

        if exchange:
            pl.when((i == grid[0] - 1) & (j == grid[1] - 1) & (k == grid[2] - 1))(
                lambda: exchange.finish(grads, recvs, *sems))

    hosted = exchange.specs_any if exchange else []
    return pl.pallas_call(
        body, name=name, grid=grid,
        in_specs=[pl.BlockSpec((ts, tm), lambda i, j, k: (k, i)), pl.BlockSpec((ts, tn), lambda i, j, k: (k, j))]
        + hosted,
        out_specs=[pl.BlockSpec((tm, tn), lambda i, j, k: (i, j))] + hosted,
        out_shape=[jax.ShapeDtypeStruct((m, nn), dtype)] + (exchange.out_shape if exchange else []),
        scratch_shapes=[pltpu.VMEM((tm, tn), F32)] + (exchange.scratch if exchange else []),
        compiler_params=_params(("arbitrary", "arbitrary", "arbitrary")),
    )(a, b, *ex_grads)


FFN_SLABS_FWD = ((0, 2816),)
FFN_SLABS_BWD = ((0, 1536), (1536, 2816))
TS_PROJ = 512
TS_MIX = 256
TS_FFN = 256
TS_WGRAD = 2048


def _local_step(x, tgt, vec_d, vec_f, placed, place):
    s = x.shape[0]
    tw = min(TS_WGRAD, s)
    sp_in, sp_pool, sp_bout, sp_o, sp_up, sp_down = SHARDED
    proj, h1, w_in, w_pool, w_bout, w_o = _fwd_proj(x, vec_d, placed[0], placed[1:4], place, min(TS_PROJ, s))
    x1, o, pg, q, merged, ya0, yb, cv, w_up, w_down = _fwd_mix(proj, x, vec_d, w_pool, w_bout, w_o, placed[4:],
                                                               min(TS_MIX, s))
    up0, upc, a, h2, dx2, dff, vo_f, loss = _fwd_ffn(x1, tgt, vec_d, vec_f, w_up, w_down, min(TS_FFN, s))
    g_down, = _wgrad(a, dff, F // 2, D, tw, "wgrad_down", BF16)
    dx1, dup0, vo_b, fo, r_down = _bwd_ffn(dff, dx2, x1, up0, upc, vec_d, vec_f, w_up, w_down,
                                           _GradExchange([sp_down]), [g_down], min(TS_FFN, s))
    g_up, = _wgrad(h2, dup0, D, F2 // NCHIP, tw, "wgrad_up", BF16)
    dproj, vo_m, g_o, g_bout, g_pool, r_up = _bwd_mix(dx1, o, proj, cv, ya0, yb, merged, q, pg, vec_d,
                                                      w_pool, w_bout, w_o, _GradExchange([sp_up]), [g_up],
                                                      min(TS_MIX, s))
    g_in, r_pool, r_bout, r_o = _wgrad(h1, dproj, D, DIN // NCHIP, tw, "wgrad_in", BF16,
                                       _GradExchange([sp_pool, sp_bout, sp_o]), [g_pool, g_bout, g_o])
    dx, vo_i, r_in = _bwd_in(dproj, dx1, x, vec_d, w_in, _GradExchange([sp_in]), [g_in], min(TS_PROJ, s))
    vecs = dict(
        dsh1=vo_i[0], dsc1=vo_i[1], dg_pre_mix=vo_i[2],
        dgt1=vo_m[0], dg_post_mix=vo_m[1], dpool_scale=vo_m[2], dconv_b=vo_m[3],
        dconv_w=vo_m[4:7],
        dsh2=vo_b[0], dsc2=vo_b[1], dg_pre_ffn=vo_b[2],
        dgt2=vo_f[0], dg_post_ffn=vo_f[1],
        dffn_conv_w=fo[FV_W0:FV_W2 + 1], dffn_conv_b=fo[FV_B],
    )
    local = dict(w_in=g_in, w_pool=g_pool, w_bout=g_bout, w_o=g_o, w_up=g_up, w_down=g_down)
    received = dict(w_in=r_in, w_pool=r_pool, w_bout=r_bout, w_o=r_o, w_up=r_up, w_down=r_down)
    return loss, dx, vecs, local, received


def _aligned(offset, n):
    return offset if isinstance(offset, int) else pl.multiple_of(offset, n)


class _Sharded:
    def __init__(self, name, full_shape, shard_axis, half_axis):
        self.name = name
        self.full_shape = full_shape
        self.shard_axis = shard_axis
        self.half_axis = half_axis
        self.shard_shape = tuple(n // NCHIP if a == shard_axis else n for a, n in enumerate(full_shape))
        self.piece_shape = tuple(n // 2 if a == half_axis else n for a, n in enumerate(self.shard_shape))

    def piece(self, full_ref, k, h):
        idx = []
        for a, n in enumerate(self.piece_shape):
            if a == self.shard_axis and a == self.half_axis:
                idx.append(pl.ds(_aligned((2 * k + h) * n, n), n))
            elif a == self.shard_axis:
                idx.append(pl.ds(_aligned(k * n, n), n))
            elif a == self.half_axis:
                idx.append(pl.ds(_aligned(h * n, n), n))
            else:
                idx.append(slice(None))
        return full_ref.at[tuple(idx)]

    def shard(self, full_ref, k):
        n = self.shard_shape[self.shard_axis]
        idx = [pl.ds(_aligned(k * n, n), n) if a == self.shard_axis else slice(None)
               for a in range(len(self.full_shape))]
        return full_ref.at[tuple(idx)]

    def half(self, shard_ref, h):
        n = self.piece_shape[self.half_axis]
        idx = [pl.ds(_aligned(h * n, n), n) if a == self.half_axis else slice(None)
               for a in range(len(self.full_shape))]
        return shard_ref.at[tuple(idx)]

SHARDED = (
    _Sharded("w_in", (D, DIN), 1, 0),
    _Sharded("w_pool", (NG, GW, GW), 1, 0),
    _Sharded("w_bout", (D, D), 0, 0),
    _Sharded("w_o", (D, D), 0, 0),
    _Sharded("w_up", (D, F2), 1, 0),
    _Sharded("w_down", (F, D), 0, 0),
)
NW = len(SHARDED)


def _mesh_place():
    x, y, c = lax.axis_index("x"), lax.axis_index("y"), lax.axis_index("c")
    chips = [(1 - x, y), (x, 1 - y), (1 - x, 1 - y)]
    return x, y, c, 2 * x + y, chips, [2 * px + py for px, py in chips]


def _remote(src, dst, send_sem, recv_sem, device):
    return pltpu.make_async_remote_copy(src_ref=src, dst_ref=dst, send_sem=send_sem, recv_sem=recv_sem,
                                        device_id=device, device_id_type=MESH)


def _all_gather_small(block, name):
    m_per, n = block.shape

    def body(x_ref, out_ref, send_sems, recv_sems, local_sem):
        x, y, c, _, chips, _ = _mesh_place()
        me, sibling = (x, y, c), (x, y, 1 - c)

        def rows(px, py, pc):
            return out_ref.at[pl.ds((4 * px + 2 * py + pc) * m_per, m_per), :]

        def copy(k, blk, to, src=None):
            return _remote(rows(*blk) if src is None else src, rows(*blk), send_sems.at[k], recv_sems.at[k], to)

        mine = pltpu.make_async_copy(x_ref, rows(*me), local_sem)
        mine.start()
        first = [copy(0, me, sibling, src=x_ref)]
        first += [copy(1 + j, me, (*chip, c), src=x_ref) for j, chip in enumerate(chips)]
        for cp in first:
            cp.start()
        passed = [copy(4 + j, (*chip, c), sibling) for j, chip in enumerate(chips)]
        for j, chip in enumerate(chips):
            copy(1 + j, (*chip, c), me).wait_recv()
            passed[j].start()
        copy(0, sibling, me).wait_recv()
        for j, chip in enumerate(chips):
            copy(4 + j, (*chip, 1 - c), me).wait_recv()
        for cp in first + passed:
            cp.wait_send()
        mine.wait()

    return pl.pallas_call(
        body, name=name,
        out_shape=jax.ShapeDtypeStruct((NDEV * m_per, n), block.dtype),
        in_specs=[pl.BlockSpec(memory_space=pltpu.VMEM)],
        out_specs=pl.BlockSpec(memory_space=pltpu.VMEM),
        scratch_shapes=[pltpu.SemaphoreType.DMA((7,)), pltpu.SemaphoreType.DMA((7,)), pltpu.SemaphoreType.DMA],
        compiler_params=pltpu.CompilerParams(vmem_limit_bytes=VMEM_LIMIT),
    )(block)


class _WeightGather:
    def __init__(self, specs):
        self.specs = specs
        self.n = len(specs)
        self.specs_any = [pl.BlockSpec(memory_space=pl.ANY)] * self.n
        self.out_shape = [jax.ShapeDtypeStruct(sp.full_shape, BF16) for sp in specs]
        self.scratch = [pltpu.SemaphoreType.DMA((6 * self.n,)), pltpu.SemaphoreType.DMA((6 * self.n,))]

    def _sends(self, outs, send_sems, recv_sems):
        x, y, c, k_me, chips, _ = _mesh_place()
        sends = []
        for j, chip in enumerate(chips):
            for w, sp in enumerate(self.specs):
                mine = sp.piece(outs[w], k_me, c)
                sends.append(_remote(mine, mine, send_sems.at[6 * w + j], recv_sems.at[6 * w + j], (*chip, c)))
        return sends

    def start(self, outs, send_sems, recv_sems):
        for cp in self._sends(outs, send_sems, recv_sems):
            cp.start()

    def _passes(self, outs, send_sems, recv_sems):
        x, y, c, _, chips, kidx = _mesh_place()
        return [_remote(sp.piece(outs[w], kidx[j], c), sp.piece(outs[w], kidx[j], c),
                        send_sems.at[6 * w + 3 + j], recv_sems.at[6 * w + 3 + j], (x, y, 1 - c))
                for j in range(3) for w, sp in enumerate(self.specs)]

    def forward(self, outs, send_sems, recv_sems):
        x, y, c, _, chips, kidx = _mesh_place()
        for j, chip in enumerate(chips):
            for w, sp in enumerate(self.specs):
                landed = sp.piece(outs[w], kidx[j], c)
                _remote(landed, landed, send_sems.at[6 * w + j], recv_sems.at[6 * w + j], (*chip, c)).wait_recv()
        for cp in self._passes(outs, send_sems, recv_sems):
            cp.start()

    def drain(self, outs, send_sems, recv_sems):
        x, y, c, _, chips, kidx = _mesh_place()
        for j in range(3):
            for w, sp in enumerate(self.specs):
                landed = sp.piece(outs[w], kidx[j], 1 - c)
                _remote(landed, landed, send_sems.at[6 * w + 3 + j], recv_sems.at[6 * w + 3 + j],
                        (x, y, 1 - c)).wait_recv()
        for cp in self._sends(outs, send_sems, recv_sems) + self._passes(outs, send_sems, recv_sems):
            cp.wait_send()

    def finish(self, outs, send_sems, recv_sems):
        self.forward(outs, send_sems, recv_sems)
        self.drain(outs, send_sems, recv_sems)


class _GradExchange:
    def __init__(self, specs):
        self.specs = specs
        self.n = len(specs)
        self.specs_any = [pl.BlockSpec(memory_space=pl.ANY)] * self.n
        self.out_shape = [jax.ShapeDtypeStruct((NDEV,) + sp.piece_shape, BF16) for sp in specs]
        self.scratch = [pltpu.SemaphoreType.DMA((7 * self.n,)), pltpu.SemaphoreType.DMA((NDEV * self.n,))]

    def _sends(self, grads, recvs, send_sems, recv_sems):
        x, y, c, k_me, chips, kidx = _mesh_place()
        dev = 2 * k_me + c
        sends = []
        for w, sp in enumerate(self.specs):
            slot, arrival = recvs[w].at[dev], recv_sems.at[NDEV * w + dev]
            sends.append(_remote(sp.piece(grads[w], k_me, 1 - c), slot, send_sems.at[7 * w], arrival, (x, y, 1 - c)))
            for j, chip in enumerate(chips):
                for h in range(2):
                    sends.append(_remote(sp.piece(grads[w], kidx[j], h), slot, send_sems.at[7 * w + 1 + 2 * j + h],
                                         arrival, (*chip, h)))
        return sends

    def start(self, grads, recvs, send_sems, recv_sems):
        for cp in self._sends(grads, recvs, send_sems, recv_sems):
            cp.start()

    def finish(self, grads, recvs, send_sems, recv_sems):
        x, y, c, k_me, _, _ = _mesh_place()
        dev = 2 * k_me + c
        for w in range(self.n):
            for d in range(NDEV):
                landed = recvs[w].at[d]
                arrival = _remote(landed, landed, send_sems.at[7 * w], recv_sems.at[NDEV * w + d], (x, y, c))
                pl.when(d != dev)(arrival.wait_recv)
        for cp in self._sends(grads, recvs, send_sems, recv_sems):
            cp.wait_send()


def _device_sums(locals_, recvs, place):
    def body(p_ref, *refs):
        a_refs, b_refs, o_refs = refs[:NW], refs[NW:2 * NW], refs[2 * NW:]
        d = pl.program_id(0)
        own = d == p_ref[2]
        terms = [jnp.where(own, a_ref[...], b_ref[...]).astype(F32) for a_ref, b_ref in zip(a_refs, b_refs)]

        @pl.when(d == 0)
        def _():
            for o_ref, term in zip(o_refs, terms):
                o_ref[...] = term

        @pl.when(d > 0)
        def _():
            for o_ref, term in zip(o_refs, terms):
                o_ref[...] += term

    def mine(sp):
        nd = len(sp.piece_shape)
        return pl.BlockSpec(sp.piece_shape, lambda d, p_ref: tuple(
            2 * p_ref[0] + p_ref[1] if a == sp.shard_axis == sp.half_axis else
            p_ref[0] if a == sp.shard_axis else p_ref[1] if a == sp.half_axis else 0 for a in range(nd)))

    def others(sp):
        nd = len(sp.piece_shape)
        return pl.BlockSpec((None,) + sp.piece_shape,
                            lambda d, p_ref: (jnp.where(d == p_ref[2], (d + 1) % NDEV, d),) + (0,) * nd)

    def half(sp):
        nd = len(sp.piece_shape)
        return pl.BlockSpec(sp.piece_shape,
                            lambda d, p_ref: tuple(p_ref[1] if a == sp.half_axis else 0 for a in range(nd)))

    return pl.pallas_call(
        body, name="rs_device_sums",
        grid_spec=pltpu.PrefetchScalarGridSpec(
            num_scalar_prefetch=1, grid=(NDEV,),
            in_specs=[mine(sp) for sp in SHARDED] + [others(sp) for sp in SHARDED],
            out_specs=[half(sp) for sp in SHARDED]),
        out_shape=[jax.ShapeDtypeStruct(sp.shard_shape, F32) for sp in SHARDED],
        compiler_params=_params(("arbitrary",)),
    )(place, *locals_, *recvs)


def _pair_share(halves):
    def body(*refs):
        outs = refs[NW:2 * NW]
        send_sems, recv_sems = refs[2 * NW:]
        x, y, c, _, _, _ = _mesh_place()
        sibling = (x, y, 1 - c)
        sent = []
        for w, sp in enumerate(SHARDED):
            mine = sp.half(outs[w], c)
            cp = _remote(mine, mine, send_sems.at[w], recv_sems.at[w], sibling)
            cp.start()
            sent.append(cp)
        for w, sp in enumerate(SHARDED):
            landed = sp.half(outs[w], 1 - c)
            _remote(landed, landed, send_sems.at[w], recv_sems.at[w], sibling).wait_recv()
        for cp in sent:
            cp.wait_send()

    hbm = pl.BlockSpec(memory_space=pl.ANY)
    return pl.pallas_call(
        body, name="rs_pair_share",
        out_shape=[jax.ShapeDtypeStruct(sp.shard_shape, F32) for sp in SHARDED],
        in_specs=[hbm] * NW, out_specs=[hbm] * NW,
        input_output_aliases={w: w for w in range(NW)},
        scratch_shapes=[pltpu.SemaphoreType.DMA((NW,)), pltpu.SemaphoreType.DMA((NW,))],
    )(*halves)


def _reduce_scatter(local, received, place):
    return _pair_share(_device_sums([local[sp.name] for sp in SHARDED], [received[sp.name] for sp in SHARDED], place))


def _place_bf16(sp, w, place):
    nd = len(sp.full_shape)

    def body(p_ref, w_ref, o_ref):
        o_ref[...] = w_ref[...].astype(BF16)

    return pl.pallas_call(
        body, name="place_" + sp.name,
        grid_spec=pltpu.PrefetchScalarGridSpec(
            num_scalar_prefetch=1, grid=(1,),
            in_specs=[pl.BlockSpec(sp.shard_shape, lambda i, p_ref: (0,) * nd)],
            out_specs=pl.BlockSpec(sp.shard_shape,
                                   lambda i, p_ref: tuple(p_ref[0] if a == sp.shard_axis else 0 for a in range(nd)))),
        out_shape=jax.ShapeDtypeStruct(sp.full_shape, BF16),
        compiler_params=_params(("arbitrary",)),
    )(place, w)


def _matmul_f32(a, b, name):
    def body(a_ref, b_ref, o_ref):
        o_ref[...] = jnp.dot(a_ref[...], b_ref[...], preferred_element_type=F32, precision=lax.Precision.HIGHEST)

    return pl.pallas_call(body, name=name, out_shape=jax.ShapeDtypeStruct((a.shape[0], b.shape[1]), F32),
                          compiler_params=pltpu.CompilerParams(vmem_limit_bytes=VMEM_LIMIT))(a, b)


def _sum_devices(stacked):
    def body(x_ref, o_ref):
        acc = x_ref[0]
        for d in range(1, NDEV):
            acc = acc + x_ref[d]
        o_ref[...] = acc

    return pl.pallas_call(body, name="sum_devices", out_shape=jax.ShapeDtypeStruct(stacked.shape[1:], F32),
                          compiler_params=pltpu.CompilerParams(vmem_limit_bytes=VMEM_LIMIT))(stacked)


ADAMW_STEPS = 8


def _adamw(ws, gs, ms, vs, name):
    n = len(ws)
    steps = ADAMW_STEPS if all(w.shape[0] % (8 * ADAMW_STEPS) == 0 for w in ws) else 1

    def body(*refs):
        ins, outs = refs[:4 * n], refs[4 * n:]
        for k in range(n):
            w_ref, g_ref, m_ref, v_ref = ins[k], ins[n + k], ins[2 * n + k], ins[3 * n + k]
            gv = g_ref[...]
            nm = ADAM_B1 * m_ref[...] + (1.0 - ADAM_B1) * gv
            nv = ADAM_B2 * v_ref[...] + (1.0 - ADAM_B2) * (gv * gv)
            m_hat = nm / (1.0 - ADAM_B1 ** ADAM_STEP)
            v_hat = nv / (1.0 - ADAM_B2 ** ADAM_STEP)
            outs[k][...] = -ADAM_LR * (m_hat / (jnp.sqrt(v_hat) + ADAM_EPS) + ADAM_WD * w_ref[...])
            outs[n + k][...] = nm
            outs[2 * n + k][...] = nv

    blks = [pl.BlockSpec((w.shape[0] // steps, w.shape[1]), lambda i: (i, 0)) for w in ws]
    shapes = [jax.ShapeDtypeStruct(w.shape, F32) for w in ws]
    out = pl.pallas_call(
        body, name="adamw_" + name, grid=(steps,), in_specs=blks * 4, out_specs=blks * 3, out_shape=shapes * 3,
        compiler_params=_params(("parallel",)),
    )(*ws, *gs, *ms, *vs)
    return out[:n], out[n:2 * n], out[2 * n:]


WEIGHT_NAMES = ("g_pre_mix", "g_post_mix", "g_pre_ffn", "g_post_ffn", "w_ada", "b_ada", "w_in", "w_pool",
                "pool_scale", "conv_w", "conv_b", "w_bout", "w_o", "w_up", "ffn_conv_w", "ffn_conv_b", "w_down")
MATRIX_NAMES = ("w_ada",) + tuple(sp.name for sp in SHARDED)
VECTOR_NAMES = tuple(n for n in WEIGHT_NAMES if n not in MATRIX_NAMES)

CW = D // NCHIP
FCW = F2 // NCHIP
ADA_W = DIN // NCHIP
COND_BLOCK = (8, 768)
GRAD_BLOCK = (8, 4864)


def _flat_pad(parts, shape):
    flat = jnp.concatenate([p.reshape(-1) for p in parts])
    return jnp.pad(flat, (0, shape[0] * shape[1] - flat.shape[0])).reshape(shape)


def _take(flat, offset, shape):
    size = 1
    for n in shape:
        size *= n
    return flat[offset:offset + size].reshape(shape), offset + size


def kernel(x, c, g_pre_mix, g_post_mix, g_pre_ffn, g_post_ffn, w_ada, b_ada, w_in, w_pool, pool_scale, conv_w, conv_b, w_bout, w_o, w_up, ffn_conv_w, ffn_conv_b, w_down, loss_target, m_g_pre_mix, m_g_post_mix, m_g_pre_ffn, m_g_post_ffn, m_w_ada, m_b_ada, m_w_in, m_w_pool, m_pool_scale, m_conv_w, m_conv_b, m_w_bout, m_w_o, m_w_up, m_ffn_conv_w, m_ffn_conv_b, m_w_down, v_g_pre_mix, v_g_post_mix, v_g_pre_ffn, v_g_post_ffn, v_w_ada, v_b_ada, v_w_in, v_w_pool, v_pool_scale, v_conv_w, v_conv_b, v_w_bout, v_w_o, v_w_up, v_ffn_conv_w, v_ffn_conv_b, v_w_down):
    weights = dict(g_pre_mix=g_pre_mix, g_post_mix=g_post_mix, g_pre_ffn=g_pre_ffn, g_post_ffn=g_post_ffn,
                   w_ada=w_ada, b_ada=b_ada, w_in=w_in, w_pool=w_pool, pool_scale=pool_scale, conv_w=conv_w,
                   conv_b=conv_b, w_bout=w_bout, w_o=w_o, w_up=w_up, ffn_conv_w=ffn_conv_w, ffn_conv_b=ffn_conv_b,
                   w_down=w_down)
    mom1 = dict(g_pre_mix=m_g_pre_mix, g_post_mix=m_g_post_mix, g_pre_ffn=m_g_pre_ffn, g_post_ffn=m_g_post_ffn,
                w_ada=m_w_ada, b_ada=m_b_ada, w_in=m_w_in, w_pool=m_w_pool, pool_scale=m_pool_scale,
                conv_w=m_conv_w, conv_b=m_conv_b, w_bout=m_w_bout, w_o=m_w_o, w_up=m_w_up,
                ffn_conv_w=m_ffn_conv_w, ffn_conv_b=m_ffn_conv_b, w_down=m_w_down)
    mom2 = dict(g_pre_mix=v_g_pre_mix, g_post_mix=v_g_post_mix, g_pre_ffn=v_g_pre_ffn, g_post_ffn=v_g_post_ffn,
                w_ada=v_w_ada, b_ada=v_b_ada, w_in=v_w_in, w_pool=v_w_pool, pool_scale=v_pool_scale,
                conv_w=v_conv_w, conv_b=v_conv_b, w_bout=v_w_bout, w_o=v_w_o, w_up=v_w_up,
                ffn_conv_w=v_ffn_conv_w, ffn_conv_b=v_ffn_conv_b, w_down=v_w_down)

    chip = 2 * lax.axis_index("x") + lax.axis_index("y")
    core = lax.axis_index("c")
    dev = 2 * chip + core
    place = jnp.stack([chip, core, dev]).astype(jnp.int32)

    cond = _all_gather_small(_flat_pad([c, conv_w, ffn_conv_w], COND_BLOCK), "gather_cond")
    cond = cond.reshape(NDEV, -1)
    c_all = cond[:, :D]
    by_chip = cond[0::2]
    conv_w_full = by_chip[:, D:D + 3 * CW].reshape(NCHIP, 3, CW).transpose(1, 0, 2).reshape(3, D)
    ffn_w_full = by_chip[:, D + 3 * CW:D + 3 * CW + 3 * FCW].reshape(NCHIP, 3, FCW).transpose(1, 0, 2).reshape(3, F2)

    mod_cols = _all_gather_small(_matmul_f32(c_all, w_ada[0], "ada_mod"), "gather_mod")
    mod_cols = mod_cols.reshape(NDEV, NDEV, ADA_W)[0::2]
    mod = lax.dynamic_index_in_dim(mod_cols, dev, axis=1, keepdims=False).reshape(6, D) + b_ada.reshape(6, D)
    vec_d = jnp.concatenate([mod, g_pre_mix, g_post_mix, g_pre_ffn, g_post_ffn, pool_scale, conv_b, conv_w_full,
                             jnp.zeros((VD_ROWS - 15, D), F32)], axis=0)
    vec_f = jnp.concatenate([ffn_w_full, ffn_conv_b, jnp.zeros((FV_ROWS - 4, F2), F32)], axis=0)

    placed = [_place_bf16(sp, weights[sp.name][0], place) for sp in SHARDED]
    loss_blk, dx, vecs, local, received = _local_step(x[0], loss_target[0], vec_d, vec_f, placed, place)

    dmod = [vecs[n] for n in ("dsh1", "dsc1", "dgt1", "dsh2", "dsc2", "dgt2")]
    small = [vecs["dg_pre_mix"], vecs["dg_post_mix"], vecs["dg_pre_ffn"], vecs["dg_post_ffn"]] + dmod + [
        vecs["dpool_scale"], vecs["dconv_w"], vecs["dconv_b"], vecs["dffn_conv_w"], vecs["dffn_conv_b"],
        loss_blk[0]]
    gathered = _all_gather_small(_flat_pad(small, GRAD_BLOCK), "gather_vector_grads")
    total = _sum_devices(gathered.reshape((NDEV,) + GRAD_BLOCK)).reshape(-1)
    vgrad = {}
    off = 0
    for n in ("g_pre_mix", "g_post_mix", "g_pre_ffn", "g_post_ffn"):
        vgrad[n], off = _take(total, off, (1, D))
    dmod_off = off
    vgrad["b_ada"], off = _take(total, off, (1, DIN))
    vgrad["pool_scale"], off = _take(total, off, (1, D))
    g_conv_w, off = _take(total, off, (3, D))
    vgrad["conv_w"] = lax.dynamic_slice_in_dim(g_conv_w, chip * CW, CW, axis=1)[None]
    vgrad["conv_b"], off = _take(total, off, (1, D))
    g_ffn_w, off = _take(total, off, (3, F2))
    vgrad["ffn_conv_w"] = lax.dynamic_slice_in_dim(g_ffn_w, chip * FCW, FCW, axis=1)[None]
    vgrad["ffn_conv_b"], off = _take(total, off, (1, F2))
    loss = total[off]

    dmod_all = gathered.reshape(NDEV, -1)[:, dmod_off:dmod_off + DIN]
    dmod_cols = lax.dynamic_slice_in_dim(dmod_all, chip * ADA_W, ADA_W, axis=1)
    g_ada = _matmul_f32(jnp.pad(c_all.T, ((0, 0), (0, 128 - NDEV))), jnp.pad(dmod_cols, ((0, 128 - NDEV), (0, 0))),
                        "ada_wgrad")

    reduced = _reduce_scatter(local, received, place)
    mgrad = {"w_ada": g_ada}
    for sp, g in zip(SHARDED, reduced):
        mgrad[sp.name] = g

    grad, delta, new_m, new_v = {}, {}, {}, {}
    two_d = lambda tree: [tree[n].reshape(-1, weights[n].shape[-1]) for n in MATRIX_NAMES]
    ds, nms, nvs = _adamw(two_d(weights), two_d(mgrad), two_d(mom1), two_d(mom2), "matrices")
    for n, d, nm, nv in zip(MATRIX_NAMES, ds, nms, nvs):
        shape = weights[n].shape
        grad[n], delta[n], new_m[n], new_v[n] = (a.reshape(shape) for a in (mgrad[n], d, nm, nv))
    flat = lambda tree: [jnp.concatenate([tree[n].reshape(1, -1) for n in VECTOR_NAMES], axis=1)]
    (d,), (nm,), (nv,) = _adamw(flat(weights), flat(vgrad), flat(mom1), flat(mom2), "vectors")
    off = 0
    for n in VECTOR_NAMES:
        shape = weights[n].shape
        grad[n] = vgrad[n].reshape(shape)
        delta[n], _ = _take(d[0], off, shape)
        new_m[n], _ = _take(nm[0], off, shape)
        new_v[n], off = _take(nv[0], off, shape)

    return (loss, dx[None], *[grad[n] for n in WEIGHT_NAMES], *[delta[n] for n in WEIGHT_NAMES],
            *[new_m[n] for n in WEIGHT_NAMES], *[new_v[n] for n in WEIGHT_NAMES])
```

```python
import jax
import jax.numpy as jnp
from jax import lax
from jax.experimental import pallas as pl
from jax.experimental.pallas import tpu as pltpu

F32 = jnp.float32
BF16 = jnp.bfloat16

D = 1024
DIN = 6 * D
F = 2816
F2 = 2 * F
NG = 4
GW = D // NG
POOL_CARRY = 16
CONV_CARRY = 3
EPS = 1e-6
NCHIP = 4
NDEV = 8

ADAM_LR = 0.001
ADAM_B1 = 0.9
ADAM_B2 = 0.999
ADAM_EPS = 1e-08
ADAM_WD = 0.01
ADAM_STEP = 10

VMEM_LIMIT = 60 * 1024 * 1024

(V_SH1, V_SC1, V_GT1, V_SH2, V_SC2, V_GT2, V_GPRE1, V_GPOST1, V_GPRE2, V_GPOST2,
 V_PSCALE, V_CB, V_CW0, V_CW1, V_CW2) = range(15)
VD_ROWS = 16
FV_W0, FV_W1, FV_W2, FV_B = range(4)
FV_ROWS = 8

MESH = pl.DeviceIdType.MESH


def _params(sem=None, vmem=VMEM_LIMIT):
    return pltpu.CompilerParams(dimension_semantics=sem, vmem_limit_bytes=vmem)


def _row(ref, r):
    return ref[r:r + 1, :]


def _load_once(pairs, sem):
    @pl.when(pl.program_id(0) == 0)
    def _():
        copies = [pltpu.make_async_copy(src, dst, sem.at[n]) for n, (src, dst) in enumerate(pairs)]
        for cp in copies:
            cp.start()
        for cp in copies:
            cp.wait()


def _dot(a, b):
    return jnp.dot(a, b, preferred_element_type=F32)


def _dot_nt(a, b):
    return lax.dot_general(a, b, (((1,), (1,)), ((), ())), preferred_element_type=F32)


BLK = 256
SEG = BLK // 8


def _load_rows(ref, ts):
    blocks = [jnp.swapaxes(ref[b * BLK:(b + 1) * BLK, :].reshape(8, SEG, ref.shape[-1]), 0, 1).reshape(BLK, -1)
              for b in range(ts // BLK)]
    return jnp.concatenate(blocks, axis=0)


def _store_rows(ref, val, ts):
    for b in range(ts // BLK):
        blk = val[b * BLK:(b + 1) * BLK, :].reshape(SEG, 8, val.shape[-1])
        ref[b * BLK:(b + 1) * BLK, :] = jnp.swapaxes(blk, 0, 1).reshape(BLK, -1)


def _times(t0):
    p = lax.broadcasted_iota(jnp.int32, (BLK, 1), 0)
    return t0 + (p & 7) * SEG + (p >> 3)


def _before(x, carry, s):
    x3 = x.reshape(SEG, 8, x.shape[-1])
    tail = pltpu.roll(x3[SEG - s:], 1, 1)
    row = lax.broadcasted_iota(jnp.int32, tail.shape, 1)
    out = jnp.concatenate([jnp.where(row == 0, carry, tail), x3[:SEG - s]], axis=0)
    return out.reshape(x.shape), tail


def _after(x, carry, s):
    x3 = x.reshape(SEG, 8, x.shape[-1])
    head = pltpu.roll(x3[:s], 7, 1)
    row = lax.broadcasted_iota(jnp.int32, head.shape, 1)
    out = jnp.concatenate([x3[s:], jnp.where(row == 7, carry, head)], axis=0)
    return out.reshape(x.shape), head


def _causal_conv(x, carry, cols, w0, w1, w2, b):
    x1, carry[0:1, :, cols] = _before(x, carry[0:1, :, cols], 1)
    x2, carry[1:3, :, cols] = _before(x, carry[1:3, :, cols], 2)
    return b + w2 * x + w1 * x1 + w0 * x2


def _causal_conv_bwd(dy, carry, cols, w0, w1, w2):
    d1, carry[0:1, :, cols] = _after(dy, carry[0:1, :, cols], 1)
    d2, carry[1:3, :, cols] = _after(dy, carry[1:3, :, cols], 2)
    return w2 * dy + w1 * d1 + w0 * d2, d1, d2


def _pool_counts(t0, g):
    return jnp.minimum((_times(t0) + 1).astype(F32), float(2 << g))


def _rms(x):
    return lax.rsqrt(jnp.mean(x * x, axis=-1, keepdims=True) + EPS)


def _rms_bwd(dn, n, r):
    return r * (dn - n * jnp.mean(dn * n, axis=-1, keepdims=True))


def _colsum(x):
    return jnp.sum(x, axis=0, keepdims=True)


def _gelu_and_grad(x):
    k, a = 0.7978845608028654, 0.044715
    x2 = x * x
    th1 = 1.0 + jnp.tanh(x * (x2 * (k * a) + k))
    hx = 0.5 * x
    gelu = hx * th1
    dgelu = 0.5 * th1 + (hx * (th1 * (2.0 - th1))) * (x2 * (3.0 * k * a) + k)
    return gelu, dgelu


def _fwd_proj(x, vec_d, placed_in, placed_rest, place, ts):
    s = x.shape[0]
    nt = s // ts
    cw = DIN // NCHIP
    sp_in = SHARDED[0]
    gather = _WeightGather(SHARDED[1:4])
    n = gather.n

    def body(*refs):
        p_ref, x_ref, v_ref = refs[:3]
        proj_ref, h1_ref, w_full = refs[4 + n:7 + n]
        rest = refs[7 + n:7 + 2 * n]
        w_vmem, h1_all, sem, in_send, in_recv, send_sems, recv_sems = refs[7 + 2 * n:]
        j, i = pl.program_id(0), pl.program_id(1)
        x_, y_, c, k_me, _, _ = _mesh_place()
        sibling = (x_, y_, 1 - c)

        def peer(t):
            return (x_ ^ (t >> 1), y_ ^ (t & 1))

        def w_in_sends():
            mine = sp_in.piece(w_full, k_me, c)
            return [_remote(mine, mine, in_send.at[t - 1], in_recv.at[t - 1], (*peer(t), c)) for t in (1, 2, 3)]

        def load_block(k):
            cp = pltpu.make_async_copy(sp_in.shard(w_full, k), w_vmem.at[k], sem.at[0])
            cp.start()
            cp.wait()

        @pl.when((j == 0) & (i == 0))
        def _():
            for cp in w_in_sends()[:2]:
                cp.start()
            load_block(k_me)

        @pl.when((j == 1) & (i == 0))
        def _():
            for cp in w_in_sends()[:2]:
                cp.wait_send()
            w_in_sends()[2].start()
            gather.start(rest, send_sems, recv_sems)

        for t in (1, 2, 3):
            @pl.when((j == t) & (i == 0))
            def _(t=t):
                k = k_me ^ t
                landed = sp_in.piece(w_full, k, c)
                _remote(landed, landed, in_send.at[t - 1], in_recv.at[t - 1], (*peer(t), c)).wait_recv()
                _remote(landed, landed, in_send.at[2 + t], in_recv.at[2 + t], sibling).start()
                other = sp_in.piece(w_full, k, 1 - c)
                _remote(other, other, in_send.at[2 + t], in_recv.at[2 + t], sibling).wait_recv()
                load_block(k)

        @pl.when(j == 0)
        def _():
            xv = _load_rows(x_ref, ts)
            n1 = xv * _rms(xv)
            h = n1 * (_row(v_ref, V_GPRE1) * (1.0 + _row(v_ref, V_SC1))) + _row(v_ref, V_SH1)
            hb = h.astype(BF16)
            h1_ref[...] = hb
            h1_all[i] = hb

        proj_ref[...] = _dot(h1_all[i], w_vmem[k_me ^ j]).astype(BF16)

        @pl.when((j == NCHIP - 1) & (i == nt - 1))
        def _():
            w_in_sends()[2].wait_send()
            for t in (1, 2, 3):
                landed = sp_in.piece(w_full, k_me ^ t, c)
                _remote(landed, landed, in_send.at[2 + t], in_recv.at[2 + t], sibling).wait_send()
            gather.finish(rest, send_sems, recv_sems)

    once = lambda w: pl.BlockSpec((ts, w), lambda j, i, p: (jnp.where(j == 0, i, nt - 1), 0))
    return pl.pallas_call(
        body, name="fwd_proj",
        grid_spec=pltpu.PrefetchScalarGridSpec(
            num_scalar_prefetch=1, grid=(NCHIP, nt),
            in_specs=[once(D), pl.BlockSpec((VD_ROWS, D), lambda j, i, p: (0, 0)),
                      pl.BlockSpec(memory_space=pl.ANY)] + gather.specs_any,
            out_specs=[pl.BlockSpec((ts, cw), lambda j, i, p: (i, p[0] ^ j)), once(D),
                       pl.BlockSpec(memory_space=pl.ANY)] + gather.specs_any,
            scratch_shapes=[pltpu.VMEM((NCHIP, D, cw), BF16), pltpu.VMEM((nt, ts, D), BF16),
                            pltpu.SemaphoreType.DMA((1,)),
                            pltpu.SemaphoreType.DMA((6,)), pltpu.SemaphoreType.DMA((6,))] + gather.scratch),
        out_shape=[jax.ShapeDtypeStruct((s, DIN), BF16), jax.ShapeDtypeStruct((s, D), BF16),
                   jax.ShapeDtypeStruct(sp_in.full_shape, BF16)] + gather.out_shape,
        input_output_aliases={3 + w: 2 + w for w in range(n + 1)},
        compiler_params=_params(("arbitrary", "arbitrary")),
    )(place, x, vec_d, placed_in, *placed_rest)


def _fwd_mix(proj, x, vec_d, w_pool, w_bout, w_o, placed_ffn, ts):
    s = x.shape[0]
    gather = _WeightGather(SHARDED[4:])
    n = gather.n

    def body(*refs):
        ins, outs, rest = refs[:6], refs[6 + n:14 + n], refs[14 + n:14 + 2 * n]
        scratch, sems = refs[14 + 2 * n:-2], refs[-2:]
        i = pl.program_id(0)
        nt = s // ts
        pl.when(i == 0)(lambda: gather.start(rest, *sems))
        pl.when(i == nt - 1 - nt // 8)(lambda: gather.forward(rest, *sems))
        compute(*ins, *outs, *scratch)
        pl.when(i == nt - 1)(lambda: gather.drain(rest, *sems))

    def compute(p_ref, x_ref, v_ref, wp_hbm, wb_hbm, wo_hbm,
                x1_ref, o_ref, pg_ref, q_ref, mg_ref, ya_ref, yb_ref, cv_ref,
                wp, wb, wo, carry_p, carry_v, sem):
        i = pl.program_id(0)
        _load_once([(wp_hbm, wp), (wb_hbm, wb), (wo_hbm, wo)], sem)

        @pl.when(i == 0)
        def _():
            carry_p[...] = jnp.zeros_like(carry_p)
            carry_v[...] = jnp.zeros_like(carry_v)

        t0 = i * ts
        for g in range(NG):
            cols = slice(g * GW, (g + 1) * GW)
            u = p_ref[:, cols].astype(F32)
            e = u
            for l in range(g + 1):
                slot = slice((1 << l) - 1, (2 << l) - 1)
                shifted, carry_p[slot, :, cols] = _before(e, carry_p[slot, :, cols], 1 << l)
                e = e + shifted
            pgb = (e / _pool_counts(t0, g) - u).astype(BF16)
            pg_ref[:, cols] = pgb
            ya_ref[:, cols] = _dot(pgb, wp[g]).astype(BF16)

        u_x = p_ref[:, D:2 * D].astype(F32)
        u_c = p_ref[:, 3 * D:4 * D].astype(F32)
        v = u_c * u_x
        cv = _causal_conv(v, carry_v, slice(None), _row(v_ref, V_CW0), _row(v_ref, V_CW1),
                          _row(v_ref, V_CW2), _row(v_ref, V_CB))
        cv_ref[...] = cv.astype(BF16)
        q = (p_ref[:, 2 * D:3 * D].astype(F32) * cv).astype(BF16)
        q_ref[...] = q
        y_b = _dot(q, wb[...])
        yb_ref[...] = y_b.astype(BF16)

        y_a = ya_ref[...].astype(F32) * _row(v_ref, V_PSCALE)
        merged = (jax.nn.sigmoid(p_ref[:, 4 * D:5 * D].astype(F32)) * y_a
                  + jax.nn.sigmoid(p_ref[:, 5 * D:6 * D].astype(F32)) * y_b).astype(BF16)
        mg_ref[...] = merged
        o = _dot(merged, wo[...])
        o_ref[...] = o.astype(BF16)
        x1_ref[...] = _load_rows(x_ref, ts) + _row(v_ref, V_GT1) * ((o * _rms(o)) * _row(v_ref, V_GPOST1))

    tile = lambda w: pl.BlockSpec((ts, w), lambda i: (i, 0))
    hbm = pl.BlockSpec(memory_space=pl.ANY)
    return pl.pallas_call(
        body, name="fwd_mix", grid=(s // ts,),
        in_specs=[tile(DIN), tile(D), pl.BlockSpec((VD_ROWS, D), lambda i: (0, 0)), hbm, hbm, hbm] + gather.specs_any,
        out_specs=[tile(D)] * 8 + gather.specs_any,
        out_shape=[jax.ShapeDtypeStruct((s, D), F32)] + [jax.ShapeDtypeStruct((s, D), BF16)] * 7 + gather.out_shape,
        input_output_aliases={6 + w: 8 + w for w in range(n)},
        scratch_shapes=[pltpu.VMEM((NG, GW, GW), BF16), pltpu.VMEM((D, D), BF16), pltpu.VMEM((D, D), BF16),
                        pltpu.VMEM((POOL_CARRY, 8, D), F32), pltpu.VMEM((CONV_CARRY, 8, D), F32),
                        pltpu.SemaphoreType.DMA((3,))] + gather.scratch,
        compiler_params=_params(("arbitrary",)),
    )(proj, x, vec_d, w_pool, w_bout, w_o, *placed_ffn)


def _fwd_ffn(x1, tgt, vec_d, vec_f, w_up, w_down, ts):
    s = x1.shape[0]

    def body(x1_ref, t_ref, v_ref, f_ref, wu_hbm, wd_hbm,
             up_ref, upc_ref, a_ref, h2_ref, dx2_ref, dff_ref, vo_ref, loss_ref,
             wu, wd, carry, sem):
        i = pl.program_id(0)
        _load_once([(wu_hbm, wu), (wd_hbm, wd)], sem)

        @pl.when(i == 0)
        def _():
            carry[...] = jnp.zeros_like(carry)
            vo_ref[...] = jnp.zeros_like(vo_ref)
            loss_ref[...] = jnp.zeros_like(loss_ref)

        x1v = x1_ref[...]
        n3 = x1v * _rms(x1v)
        h2 = (n3 * (_row(v_ref, V_GPRE2) * (1.0 + _row(v_ref, V_SC2))) + _row(v_ref, V_SH2)).astype(BF16)
        h2_ref[...] = h2

        ff = jnp.zeros((ts, D), F32)
        for lo, hi in FFN_SLABS_FWD:
            up = []
            for cols in (slice(lo, hi), slice(F + lo, F + hi)):
                u0 = _dot(h2, wu[:, cols])
                up_ref[:, cols] = u0.astype(BF16)
                y = _causal_conv(u0, carry, cols, f_ref[FV_W0:FV_W0 + 1, cols], f_ref[FV_W1:FV_W1 + 1, cols],
                                 f_ref[FV_W2:FV_W2 + 1, cols], f_ref[FV_B:FV_B + 1, cols])
                upc_ref[:, cols] = y.astype(BF16)
                up.append(y)
            gelu, _ = _gelu_and_grad(up[0])
            a = (gelu * up[1]).astype(BF16)
            a_ref[:, lo:hi] = a
            ff = ff + _dot(a, wd[lo:hi, :])

        r4 = _rms(ff)
        n4 = ff * r4
        gt2 = _row(v_ref, V_GT2)
        gpost = _row(v_ref, V_GPOST2)
        gate_gain = gt2 * gpost
        diff = (x1v + gate_gain * n4) - _load_rows(t_ref, ts)
        loss_ref[...] += jnp.full(loss_ref.shape, 0.5 / D * jnp.sum(diff * diff), F32)
        dx2_ref[...] = diff * (1.0 / D)
        s1 = _colsum(diff * n4)
        vo_ref[0:1, :] += s1 * (gpost * (1.0 / D))
        vo_ref[1:2, :] += s1 * (gt2 * (1.0 / D))
        dff_ref[...] = _rms_bwd(diff * (gate_gain * (1.0 / D)), n4, r4).astype(BF16)

    tile = lambda w: pl.BlockSpec((ts, w), lambda i: (i, 0))
    full = lambda r, w: pl.BlockSpec((r, w), lambda i: (0, 0))
    hbm = pl.BlockSpec(memory_space=pl.ANY)
    return pl.pallas_call(
        body, name="fwd_ffn", grid=(s // ts,),
        in_specs=[tile(D), tile(D), full(VD_ROWS, D), full(FV_ROWS, F2), hbm, hbm],
        out_specs=[tile(F2), tile(F2), tile(F), tile(D), tile(D), tile(D), full(8, D), full(8, 128)],
        out_shape=[jax.ShapeDtypeStruct((s, F2), BF16), jax.ShapeDtypeStruct((s, F2), BF16),
                   jax.ShapeDtypeStruct((s, F), BF16),
                   jax.ShapeDtypeStruct((s, D), BF16), jax.ShapeDtypeStruct((s, D), F32),
                   jax.ShapeDtypeStruct((s, D), BF16), jax.ShapeDtypeStruct((8, D), F32),
                   jax.ShapeDtypeStruct((8, 128), F32)],
        scratch_shapes=[pltpu.VMEM((D, F2), BF16), pltpu.VMEM((F, D), BF16), pltpu.VMEM((CONV_CARRY, 8, F2), F32),
                        pltpu.SemaphoreType.DMA((2,))],
        compiler_params=_params(("arbitrary",)),
    )(x1, tgt, vec_d, vec_f, w_up, w_down)


def _bwd_ffn(dff, dx2, x1, up0, upc, vec_d, vec_f, w_up, w_down, exchange, ex_grads, ts):
    s = x1.shape[0]
    nt = s // ts
    n = exchange.n

    def body(*refs):
        ins, grads = refs[:9], refs[9:9 + n]
        outs, recvs = refs[9 + n:13 + n], refs[13 + n:13 + 2 * n]
        scratch, sems = refs[13 + 2 * n:-2], refs[-2:]
        i = pl.program_id(0)
        pl.when(i == 0)(lambda: exchange.start(grads, recvs, *sems))
        compute(*ins, *outs, *scratch)
        pl.when(i == nt - 1)(lambda: exchange.finish(grads, recvs, *sems))

    def compute(dff_ref, dx2_ref, x1_ref, up_ref, upc_ref, v_ref, f_ref, wu_hbm, wd_hbm,
                dx1_ref, dup_ref, vo_ref, fo_ref, wu, wd, carry, sem):
        i = pl.program_id(0)
        _load_once([(wu_hbm, wu), (wd_hbm, wd)], sem)

        @pl.when(i == 0)
        def _():
            carry[...] = jnp.zeros_like(carry)
            vo_ref[...] = jnp.zeros_like(vo_ref)
            fo_ref[...] = jnp.zeros_like(fo_ref)

        dffb = dff_ref[...]

        dh2 = jnp.zeros((ts, D), F32)
        for lo, hi in FFN_SLABS_BWD:
            slabs = (slice(lo, hi), slice(F + lo, F + hi))
            gelu, dgelu = _gelu_and_grad(upc_ref[:, slabs[0]].astype(F32))
            da = _dot_nt(dffb, wd[lo:hi, :])
            dups = (da * upc_ref[:, slabs[1]].astype(F32) * dgelu, da * gelu)
            for cols, dup in zip(slabs, dups):
                du0, d1, d2 = _causal_conv_bwd(dup, carry, cols, f_ref[FV_W0:FV_W0 + 1, cols],
                                               f_ref[FV_W1:FV_W1 + 1, cols], f_ref[FV_W2:FV_W2 + 1, cols])
                u0 = up_ref[:, cols].astype(F32)
                fo_ref[FV_B:FV_B + 1, cols] += _colsum(dup)
                fo_ref[FV_W2:FV_W2 + 1, cols] += _colsum(dup * u0)
                fo_ref[FV_W1:FV_W1 + 1, cols] += _colsum(d1 * u0)
                fo_ref[FV_W0:FV_W0 + 1, cols] += _colsum(d2 * u0)
                du0 = du0.astype(BF16)
                dup_ref[:, cols] = du0
                dh2 = dh2 + _dot_nt(du0, wu[:, cols])

        x1v = x1_ref[...]
        r3 = _rms(x1v)
        n3 = x1v * r3
        gpre = _row(v_ref, V_GPRE2)
        sc = 1.0 + _row(v_ref, V_SC2)
        vo_ref[0:1, :] += _colsum(dh2)
        s2 = _colsum(dh2 * n3)
        vo_ref[1:2, :] += s2 * gpre
        vo_ref[2:3, :] += s2 * sc
        dx1_ref[...] = dx2_ref[...] + _rms_bwd(dh2 * (gpre * sc), n3, r3)

    rev = lambda w: pl.BlockSpec((ts, w), lambda i: (nt - 1 - i, 0))
    full = lambda r, w: pl.BlockSpec((r, w), lambda i: (0, 0))
    hbm = pl.BlockSpec(memory_space=pl.ANY)
    return pl.pallas_call(
        body, name="bwd_ffn", grid=(nt,),
        in_specs=[rev(D), rev(D), rev(D), rev(F2), rev(F2), full(VD_ROWS, D), full(FV_ROWS, F2), hbm, hbm]
        + exchange.specs_any,
        out_specs=[rev(D), rev(F2), full(8, D), full(FV_ROWS, F2)] + exchange.specs_any,
        out_shape=[jax.ShapeDtypeStruct((s, D), F32), jax.ShapeDtypeStruct((s, F2), BF16),
                   jax.ShapeDtypeStruct((8, D), F32), jax.ShapeDtypeStruct((FV_ROWS, F2), F32)] + exchange.out_shape,
        scratch_shapes=[pltpu.VMEM((D, F2), BF16), pltpu.VMEM((F, D), BF16), pltpu.VMEM((CONV_CARRY, 8, F2), F32),
                        pltpu.SemaphoreType.DMA((2,))] + exchange.scratch,
        compiler_params=_params(("arbitrary",)),
    )(dff, dx2, x1, up0, upc, vec_d, vec_f, w_up, w_down, *ex_grads)


def _bwd_mix(dx1, o, proj, cv, ya0, yb, merged, q, pg, vec_d, w_pool, w_bout, w_o, exchange, ex_grads, ts):
    s = dx1.shape[0]
    nt = s // ts
    n = exchange.n

    def body(*refs):
        ins, grads = refs[:13], refs[13:13 + n]
        outs, recvs = refs[13 + n:18 + n], refs[18 + n:18 + 2 * n]
        scratch, sems = refs[18 + 2 * n:-2], refs[-2:]
        i = pl.program_id(0)
        pl.when(i == 0)(lambda: exchange.start(grads, recvs, *sems))
        compute(*ins, *outs, *scratch)
        pl.when(i == nt - 1)(lambda: exchange.finish(grads, recvs, *sems))

    def compute(dx1_ref, o_ref, p_ref, cv_ref, ya_ref, yb_ref, mg_ref, q_ref, pg_ref, v_ref, wp_hbm, wb_hbm, wo_hbm,
                dp_ref, vo_ref, go_ref, gb_ref, gp_ref, wp, wb, wo, carry_d, carry_e, acc_o, acc_b, acc_p, sem):
        i = pl.program_id(0)
        _load_once([(wp_hbm, wp), (wb_hbm, wb), (wo_hbm, wo)], sem)

        @pl.when(i == 0)
        def _():
            carry_d[...] = jnp.zeros_like(carry_d)
            carry_e[...] = jnp.zeros_like(carry_e)
            vo_ref[...] = jnp.zeros_like(vo_ref)
            acc_o[...] = jnp.zeros_like(acc_o)
            acc_b[...] = jnp.zeros_like(acc_b)
            acc_p[...] = jnp.zeros_like(acc_p)

        t0 = (nt - 1 - i) * ts
        dx1v = dx1_ref[...]
        ov = o_ref[...].astype(F32)
        r2 = _rms(ov)
        n2 = ov * r2
        gpost = _row(v_ref, V_GPOST1)
        gt1 = _row(v_ref, V_GT1)
        s1 = _colsum(dx1v * n2)
        vo_ref[0:1, :] += s1 * gpost
        vo_ref[1:2, :] += s1 * gt1
        dob = _rms_bwd(dx1v * (gt1 * gpost), n2, r2).astype(BF16)
        acc_o[...] += _dot_tn(mg_ref[...], dob)
        dmerged = _dot_nt(dob, wo[...])

        ya0 = ya_ref[...].astype(F32)
        pscale = _row(v_ref, V_PSCALE)
        sa = jax.nn.sigmoid(p_ref[:, 4 * D:5 * D].astype(F32))
        dp_ref[:, 4 * D:5 * D] = (dmerged * (ya0 * pscale) * sa * (1.0 - sa)).astype(BF16)
        dy_a = dmerged * sa
        vo_ref[2:3, :] += _colsum(dy_a * ya0)
        dya0 = (dy_a * pscale).astype(BF16)

        sb = jax.nn.sigmoid(p_ref[:, 5 * D:6 * D].astype(F32))
        dp_ref[:, 5 * D:6 * D] = (dmerged * yb_ref[...].astype(F32) * sb * (1.0 - sb)).astype(BF16)
        dy_b = (dmerged * sb).astype(BF16)
        acc_b[...] += _dot_tn(q_ref[...], dy_b)
        dq = _dot_nt(dy_b, wb[...])

        u_x = p_ref[:, D:2 * D].astype(F32)
        u_b = p_ref[:, 2 * D:3 * D].astype(F32)
        u_c = p_ref[:, 3 * D:4 * D].astype(F32)
        w0, w1, w2 = _row(v_ref, V_CW0), _row(v_ref, V_CW1), _row(v_ref, V_CW2)
        dp_ref[:, 2 * D:3 * D] = (dq * cv_ref[...].astype(F32)).astype(BF16)
        dcv = dq * u_b
        dv, d1, d2 = _causal_conv_bwd(dcv, carry_d, slice(None), w0, w1, w2)
        v = u_c * u_x
        vo_ref[3:4, :] += _colsum(dcv)
        vo_ref[4:5, :] += _colsum(d2 * v)
        vo_ref[5:6, :] += _colsum(d1 * v)
        vo_ref[6:7, :] += _colsum(dcv * v)
        dp_ref[:, D:2 * D] = (dv * u_c).astype(BF16)
        dp_ref[:, 3 * D:4 * D] = (dv * u_x).astype(BF16)

        for g in range(NG):
            cols = slice(g * GW, (g + 1) * GW)
            acc_p[g] += _dot_tn(pg_ref[:, cols], dya0[:, cols])
            dpg = _dot_nt(dya0[:, cols], wp[g])
            e = dpg / _pool_counts(t0, g)
            for l in range(g + 1):
                slot = slice((1 << l) - 1, (2 << l) - 1)
                shifted, carry_e[slot, :, cols] = _after(e, carry_e[slot, :, cols], 1 << l)
                e = e + shifted
            dp_ref[:, cols] = (e - dpg).astype(BF16)

        @pl.when(i == nt - 1)
        def _():
            go_ref[...] = acc_o[...].astype(BF16)
            gb_ref[...] = acc_b[...].astype(BF16)
            gp_ref[...] = acc_p[...].astype(BF16)

    rev = lambda w: pl.BlockSpec((ts, w), lambda i: (nt - 1 - i, 0))
    hbm = pl.BlockSpec(memory_space=pl.ANY)
    whole = lambda shape: pl.BlockSpec(shape, lambda i: (0,) * len(shape))
    return pl.pallas_call(
        body, name="bwd_mix", grid=(nt,),
        in_specs=[rev(D), rev(D), rev(DIN)] + [rev(D)] * 6 + [whole((VD_ROWS, D)), hbm, hbm, hbm] + exchange.specs_any,
        out_specs=[rev(DIN), whole((8, D)), whole((D, D)), whole((D, D)), whole((NG, GW, GW))] + exchange.specs_any,
        out_shape=[jax.ShapeDtypeStruct((s, DIN), BF16), jax.ShapeDtypeStruct((8, D), F32),
                   jax.ShapeDtypeStruct((D, D), BF16), jax.ShapeDtypeStruct((D, D), BF16),
                   jax.ShapeDtypeStruct((NG, GW, GW), BF16)] + exchange.out_shape,
        scratch_shapes=[pltpu.VMEM((NG, GW, GW), BF16), pltpu.VMEM((D, D), BF16), pltpu.VMEM((D, D), BF16),
                        pltpu.VMEM((CONV_CARRY, 8, D), F32), pltpu.VMEM((POOL_CARRY, 8, D), F32),
                        pltpu.VMEM((D, D), F32), pltpu.VMEM((D, D), F32), pltpu.VMEM((NG, GW, GW), F32),
                        pltpu.SemaphoreType.DMA((3,))] + exchange.scratch,
        compiler_params=_params(("arbitrary",)),
    )(dx1, o, proj, cv, ya0, yb, merged, q, pg, vec_d, w_pool, w_bout, w_o, *ex_grads)


def _bwd_in(dproj, dx1, x, vec_d, w_in, exchange, ex_grads, ts):
    s = x.shape[0]
    nt = s // ts
    n = exchange.n

    def body(*refs):
        ins, grads = refs[:5], refs[5:5 + n]
        outs, recvs = refs[5 + n:7 + n], refs[7 + n:7 + 2 * n]
        scratch, sems = refs[7 + 2 * n:-2], refs[-2:]
        i = pl.program_id(0)
        pl.when(i == 0)(lambda: exchange.start(grads, recvs, *sems))
        compute(*ins, *outs, *scratch)
        pl.when(i == nt - 1)(lambda: exchange.finish(grads, recvs, *sems))

    def compute(dp_ref, dx1_ref, x_ref, v_ref, w_hbm, dx_ref, vo_ref, w_vmem, sem):
        _load_once([(w_hbm, w_vmem)], sem)

        @pl.when(pl.program_id(0) == 0)
        def _():
            vo_ref[...] = jnp.zeros_like(vo_ref)

        dh1 = _dot_nt(dp_ref[...], w_vmem[...])
        xv = _load_rows(x_ref, ts)
        r1 = _rms(xv)
        n1 = xv * r1
        gpre = _row(v_ref, V_GPRE1)
        sc = 1.0 + _row(v_ref, V_SC1)
        vo_ref[0:1, :] += _colsum(dh1)
        s1 = _colsum(dh1 * n1)
        vo_ref[1:2, :] += s1 * gpre
        vo_ref[2:3, :] += s1 * sc
        _store_rows(dx_ref, dx1_ref[...] + _rms_bwd(dh1 * (gpre * sc), n1, r1), ts)

    tile = lambda w: pl.BlockSpec((ts, w), lambda i: (i, 0))
    return pl.pallas_call(
        body, name="bwd_in", grid=(s // ts,),
        in_specs=[tile(DIN), tile(D), tile(D), pl.BlockSpec((VD_ROWS, D), lambda i: (0, 0)),
                  pl.BlockSpec(memory_space=pl.ANY)] + exchange.specs_any,
        out_specs=[tile(D), pl.BlockSpec((8, D), lambda i: (0, 0))] + exchange.specs_any,
        out_shape=[jax.ShapeDtypeStruct((s, D), F32), jax.ShapeDtypeStruct((8, D), F32)] + exchange.out_shape,
        scratch_shapes=[pltpu.VMEM((D, DIN), BF16), pltpu.SemaphoreType.DMA((1,))] + exchange.scratch,
        compiler_params=_params(("arbitrary",)),
    )(dproj, dx1, x, vec_d, w_in, *ex_grads)


def _dot_tn(a, b):
    return lax.dot_general(a, b, (((0,), (0,)), ((), ())), preferred_element_type=F32)


def _wgrad(a, b, tm, tn, ts, name, dtype, exchange=None, ex_grads=()):
    s, m = a.shape
    nn = b.shape[1]
    grid = (m // tm, nn // tn, s // ts)
    n = exchange.n if exchange else 0

    def body(*refs):
        a_ref, b_ref = refs[:2]
        grads = refs[2:2 + n]
        o_ref = refs[2 + n]
        recvs = refs[3 + n:3 + 2 * n]
        acc = refs[3 + 2 * n]
        sems = refs[4 + 2 * n:]
        i, j, k = pl.program_id(0), pl.program_id(1), pl.program_id(2)
        if exchange:
            pl.when((i == 0) & (j == 0) & (k == 0))(lambda: exchange.start(grads, recvs, *sems))
        part = _dot_tn(a_ref[...], b_ref[...])

        @pl.when(k == 0)
        def _():
            acc[...] = part

        @pl.when(k > 0)
        def _():
            acc[...] += part

        @pl.when(k == grid[2] - 1)
        def _():
            o_ref[...] = acc[...].astype(dtype)

        if exchange:
            pl.when((i == grid[0] - 1) & (j == grid[1] - 1) & (k == grid[2] - 1))(
                lambda: exchange.finish(grads, recvs, *sems))

    hosted = exchange.specs_any if exchange else []
    return pl.pallas_call(
        body, name=name, grid=grid,
        in_specs=[pl.BlockSpec((ts, tm), lambda i, j, k: (k, i)), pl.BlockSpec((ts, tn), lambda i, j, k: (k, j))]
        + hosted,
        out_specs=[pl.BlockSpec((tm, tn), lambda i, j, k: (i, j))] + hosted,
        out_shape=[jax.ShapeDtypeStruct((m, nn), dtype)] + (exchange.out_shape if exchange else []),
        scratch_shapes=[pltpu.VMEM((tm, tn), F32)] + (exchange.scratch if exchange else []),
        compiler_params=_params(("arbitrary", "arbitrary", "arbitrary")),
    )(a, b, *ex_grads)


FFN_SLABS_FWD = ((0, 2816),)
FFN_SLABS_BWD = ((0, 1536), (1536, 2816))
TS_PROJ = 512
TS_MIX = 256
TS_FFN = 256
TS_WGRAD = 2048


def _local_step(x, tgt, vec_d, vec_f, placed, place):
    s = x.shape[0]
    tw = min(TS_WGRAD, s)
    sp_in, sp_pool, sp_bout, sp_o, sp_up, sp_down = SHARDED
    proj, h1, w_in, w_pool, w_bout, w_o = _fwd_proj(x, vec_d, placed[0], placed[1:4], place, min(TS_PROJ, s))
    x1, o, pg, q, merged, ya0, yb, cv, w_up, w_down = _fwd_mix(proj, x, vec_d, w_pool, w_bout, w_o, placed[4:],
                                                               min(TS_MIX, s))
    up0, upc, a, h2, dx2, dff, vo_f, loss = _fwd_ffn(x1, tgt, vec_d, vec_f, w_up, w_down, min(TS_FFN, s))
    g_down, = _wgrad(a, dff, F // 2, D, tw, "wgrad_down", BF16)
    dx1, dup0, vo_b, fo, r_down = _bwd_ffn(dff, dx2, x1, up0, upc, vec_d, vec_f, w_up, w_down,
                                           _GradExchange([sp_down]), [g_down], min(TS_FFN, s))
    g_up, = _wgrad(h2, dup0, D, F2 // NCHIP, tw, "wgrad_up", BF16)
    dproj, vo_m, g_o, g_bout, g_pool, r_up = _bwd_mix(dx1, o, proj, cv, ya0, yb, merged, q, pg, vec_d,
                                                      w_pool, w_bout, w_o, _GradExchange([sp_up]), [g_up],
                                                      min(TS_MIX, s))
    g_in, r_pool, r_bout, r_o = _wgrad(h1, dproj, D, DIN // NCHIP, tw, "wgrad_in", BF16,
                                       _GradExchange([sp_pool, sp_bout, sp_o]), [g_pool, g_bout, g_o])
    dx, vo_i, r_in = _bwd_in(dproj, dx1, x, vec_d, w_in, _GradExchange([sp_in]), [g_in], min(TS_PROJ, s))
    vecs = dict(
        dsh1=vo_i[0], dsc1=vo_i[1], dg_pre_mix=vo_i[2],
        dgt1=vo_m[0], dg_post_mix=vo_m[1], dpool_scale=vo_m[2], dconv_b=vo_m[3],
        dconv_w=vo_m[4:7],
        dsh2=vo_b[0], dsc2=vo_b[1], dg_pre_ffn=vo_b[2],
        dgt2=vo_f[0], dg_post_ffn=vo_f[1],
        dffn_conv_w=fo[FV_W0:FV_W2 + 1], dffn_conv_b=fo[FV_B],
    )
    local = dict(w_in=g_in, w_pool=g_pool, w_bout=g_bout, w_o=g_o, w_up=g_up, w_down=g_down)
    received = dict(w_in=r_in, w_pool=r_pool, w_bout=r_bout, w_o=r_o, w_up=r_up, w_down=r_down)
    return loss, dx, vecs, local, received


def _aligned(offset, n):
    return offset if isinstance(offset, int) else pl.multiple_of(offset, n)


class _Sharded:
    def __init__(self, name, full_shape, shard_axis, half_axis):
        self.name = name
        self.full_shape = full_shape
        self.shard_axis = shard_axis
        self.half_axis = half_axis
        self.shard_shape = tuple(n // NCHIP if a == shard_axis else n for a, n in enumerate(full_shape))
        self.piece_shape = tuple(n // 2 if a == half_axis else n for a, n in enumerate(self.shard_shape))

    def piece(self, full_ref, k, h):
        idx = []
        for a, n in enumerate(self.piece_shape):
            if a == self.shard_axis and a == self.half_axis:
                idx.append(pl.ds(_aligned((2 * k + h) * n, n), n))
            elif a == self.shard_axis:
                idx.append(pl.ds(_aligned(k * n, n), n))
            elif a == self.half_axis:
                idx.append(pl.ds(_aligned(h * n, n), n))
            else:
                idx.append(slice(None))
        return full_ref.at[tuple(idx)]

    def shard(self, full_ref, k):
        n = self.shard_shape[self.shard_axis]
        idx = [pl.ds(_aligned(k * n, n), n) if a == self.shard_axis else slice(None)
               for a in range(len(self.full_shape))]
        return full_ref.at[tuple(idx)]

    def half(self, shard_ref, h):
        n = self.piece_shape[self.half_axis]
        idx = [pl.ds(_aligned(h * n, n), n) if a == self.half_axis else slice(None)
               for a in range(len(self.full_shape))]
        return shard_ref.at[tuple(idx)]

SHARDED = (
    _Sharded("w_in", (D, DIN), 1, 0),
    _Sharded("w_pool", (NG, GW, GW), 1, 0),
    _Sharded("w_bout", (D, D), 0, 0),
    _Sharded("w_o", (D, D), 0, 0),
    _Sharded("w_up", (D, F2), 1, 0),
    _Sharded("w_down", (F, D), 0, 0),
)
NW = len(SHARDED)


def _mesh_place():
    x, y, c = lax.axis_index("x"), lax.axis_index("y"), lax.axis_index("c")
    chips = [(1 - x, y), (x, 1 - y), (1 - x, 1 - y)]
    return x, y, c, 2 * x + y, chips, [2 * px + py for px, py in chips]


def _remote(src, dst, send_sem, recv_sem, device):
    return pltpu.make_async_remote_copy(src_ref=src, dst_ref=dst, send_sem=send_sem, recv_sem=recv_sem,
                                        device_id=device, device_id_type=MESH)


SMALL_GATHER_SCRATCH = [pltpu.SemaphoreType.DMA((7,)), pltpu.SemaphoreType.DMA((7,)), pltpu.SemaphoreType.DMA]


def _small_gather(x_ref, out_ref, send_sems, recv_sems, local_sem):
    m_per = x_ref.shape[0]
    x, y, c, _, chips, _ = _mesh_place()
    me, sibling = (x, y, c), (x, y, 1 - c)

    def rows(px, py, pc):
        return out_ref.at[pl.ds((4 * px + 2 * py + pc) * m_per, m_per), :]

    def copy(k, blk, to, src=None):
        return _remote(rows(*blk) if src is None else src, rows(*blk), send_sems.at[k], recv_sems.at[k], to)

    mine = pltpu.make_async_copy(x_ref, rows(*me), local_sem)
    mine.start()
    first = [copy(0, me, sibling, src=x_ref)]
    first += [copy(1 + j, me, (*chip, c), src=x_ref) for j, chip in enumerate(chips)]
    for cp in first:
        cp.start()
    passed = [copy(4 + j, (*chip, c), sibling) for j, chip in enumerate(chips)]
    for j, chip in enumerate(chips):
        copy(1 + j, (*chip, c), me).wait_recv()
        passed[j].start()
    copy(0, sibling, me).wait_recv()
    for j, chip in enumerate(chips):
        copy(4 + j, (*chip, 1 - c), me).wait_recv()
    for cp in first + passed:
        cp.wait_send()
    mine.wait()


def _all_gather_small(block, name):
    m_per, n = block.shape
    return pl.pallas_call(
        _small_gather_body(), name=name,
        out_shape=jax.ShapeDtypeStruct((NDEV * m_per, n), block.dtype),
        in_specs=[pl.BlockSpec(memory_space=pltpu.VMEM)],
        out_specs=pl.BlockSpec(memory_space=pltpu.VMEM),
        scratch_shapes=SMALL_GATHER_SCRATCH,
        compiler_params=pltpu.CompilerParams(vmem_limit_bytes=VMEM_LIMIT),
    )(block)


def _small_gather_body():
    def body(x_ref, out_ref, send_sems, recv_sems, local_sem):
        _small_gather(x_ref, out_ref, send_sems, recv_sems, local_sem)
    return body


class _WeightGather:
    def __init__(self, specs):
        self.specs = specs
        self.n = len(specs)
        self.specs_any = [pl.BlockSpec(memory_space=pl.ANY)] * self.n
        self.out_shape = [jax.ShapeDtypeStruct(sp.full_shape, BF16) for sp in specs]
        self.scratch = [pltpu.SemaphoreType.DMA((6 * self.n,)), pltpu.SemaphoreType.DMA((6 * self.n,))]

    def _sends(self, outs, send_sems, recv_sems):
        x, y, c, k_me, chips, _ = _mesh_place()
        sends = []
        for j, chip in enumerate(chips):
            for w, sp in enumerate(self.specs):
                mine = sp.piece(outs[w], k_me, c)
                sends.append(_remote(mine, mine, send_sems.at[6 * w + j], recv_sems.at[6 * w + j], (*chip, c)))
        return sends

    def start(self, outs, send_sems, recv_sems):
        for cp in self._sends(outs, send_sems, recv_sems):
            cp.start()

    def _passes(self, outs, send_sems, recv_sems):
        x, y, c, _, chips, kidx = _mesh_place()
        return [_remote(sp.piece(outs[w], kidx[j], c), sp.piece(outs[w], kidx[j], c),
                        send_sems.at[6 * w + 3 + j], recv_sems.at[6 * w + 3 + j], (x, y, 1 - c))
                for j in range(3) for w, sp in enumerate(self.specs)]

    def forward(self, outs, send_sems, recv_sems):
        x, y, c, _, chips, kidx = _mesh_place()
        for j, chip in enumerate(chips):
            for w, sp in enumerate(self.specs):
                landed = sp.piece(outs[w], kidx[j], c)
                _remote(landed, landed, send_sems.at[6 * w + j], recv_sems.at[6 * w + j], (*chip, c)).wait_recv()
        for cp in self._passes(outs, send_sems, recv_sems):
            cp.start()

    def drain(self, outs, send_sems, recv_sems):
        x, y, c, _, chips, kidx = _mesh_place()
        for j in range(3):
            for w, sp in enumerate(self.specs):
                landed = sp.piece(outs[w], kidx[j], 1 - c)
                _remote(landed, landed, send_sems.at[6 * w + 3 + j], recv_sems.at[6 * w + 3 + j],
                        (x, y, 1 - c)).wait_recv()
        for cp in self._sends(outs, send_sems, recv_sems) + self._passes(outs, send_sems, recv_sems):
            cp.wait_send()

    def finish(self, outs, send_sems, recv_sems):
        self.forward(outs, send_sems, recv_sems)
        self.drain(outs, send_sems, recv_sems)


class _GradExchange:
    def __init__(self, specs):
        self.specs = specs
        self.n = len(specs)
        self.specs_any = [pl.BlockSpec(memory_space=pl.ANY)] * self.n
        self.out_shape = [jax.ShapeDtypeStruct((NDEV,) + sp.piece_shape, BF16) for sp in specs]
        self.scratch = [pltpu.SemaphoreType.DMA((7 * self.n,)), pltpu.SemaphoreType.DMA((NDEV * self.n,))]

    def _sends(self, grads, recvs, send_sems, recv_sems):
        x, y, c, k_me, chips, kidx = _mesh_place()
        dev = 2 * k_me + c
        sends = []
        for w, sp in enumerate(self.specs):
            slot, arrival = recvs[w].at[dev], recv_sems.at[NDEV * w + dev]
            sends.append(_remote(sp.piece(grads[w], k_me, 1 - c), slot, send_sems.at[7 * w], arrival, (x, y, 1 - c)))
            for j, chip in enumerate(chips):
                for h in range(2):
                    sends.append(_remote(sp.piece(grads[w], kidx[j], h), slot, send_sems.at[7 * w + 1 + 2 * j + h],
                                         arrival, (*chip, h)))
        return sends

    def start(self, grads, recvs, send_sems, recv_sems):
        for cp in self._sends(grads, recvs, send_sems, recv_sems):
            cp.start()

    def finish(self, grads, recvs, send_sems, recv_sems):
        x, y, c, k_me, _, _ = _mesh_place()
        dev = 2 * k_me + c
        for w in range(self.n):
            for d in range(NDEV):
                landed = recvs[w].at[d]
                arrival = _remote(landed, landed, send_sems.at[7 * w], recv_sems.at[NDEV * w + d], (x, y, c))
                pl.when(d != dev)(arrival.wait_recv)
        for cp in self._sends(grads, recvs, send_sems, recv_sems):
            cp.wait_send()


def _device_sums(locals_, recvs, place):
    def body(p_ref, *refs):
        a_refs, b_refs, o_refs = refs[:NW], refs[NW:2 * NW], refs[2 * NW:]
        d = pl.program_id(0)
        own = d == p_ref[2]
        terms = [jnp.where(own, a_ref[...], b_ref[...]).astype(F32) for a_ref, b_ref in zip(a_refs, b_refs)]

        @pl.when(d == 0)
        def _():
            for o_ref, term in zip(o_refs, terms):
                o_ref[...] = term

        @pl.when(d > 0)
        def _():
            for o_ref, term in zip(o_refs, terms):
                o_ref[...] += term

    def mine(sp):
        nd = len(sp.piece_shape)
        return pl.BlockSpec(sp.piece_shape, lambda d, p_ref: tuple(
            2 * p_ref[0] + p_ref[1] if a == sp.shard_axis == sp.half_axis else
            p_ref[0] if a == sp.shard_axis else p_ref[1] if a == sp.half_axis else 0 for a in range(nd)))

    def others(sp):
        nd = len(sp.piece_shape)
        return pl.BlockSpec((None,) + sp.piece_shape,
                            lambda d, p_ref: (jnp.where(d == p_ref[2], (d + 1) % NDEV, d),) + (0,) * nd)

    def half(sp):
        nd = len(sp.piece_shape)
        return pl.BlockSpec(sp.piece_shape,
                            lambda d, p_ref: tuple(p_ref[1] if a == sp.half_axis else 0 for a in range(nd)))

    return pl.pallas_call(
        body, name="rs_device_sums",
        grid_spec=pltpu.PrefetchScalarGridSpec(
            num_scalar_prefetch=1, grid=(NDEV,),
            in_specs=[mine(sp) for sp in SHARDED] + [others(sp) for sp in SHARDED],
            out_specs=[half(sp) for sp in SHARDED]),
        out_shape=[jax.ShapeDtypeStruct(sp.shard_shape, F32) for sp in SHARDED],
        compiler_params=_params(("arbitrary",)),
    )(place, *locals_, *recvs)


def _pair_share(halves, vector_block):
    m_per, n = vector_block.shape

    def body(*refs):
        x_ref = refs[NW]
        outs, gathered = refs[NW + 1:2 * NW + 1], refs[2 * NW + 1]
        send_sems, recv_sems = refs[2 * NW + 2:2 * NW + 4]
        x, y, c, _, _, _ = _mesh_place()
        sibling = (x, y, 1 - c)
        sent = []
        for w, sp in enumerate(SHARDED):
            mine = sp.half(outs[w], c)
            cp = _remote(mine, mine, send_sems.at[w], recv_sems.at[w], sibling)
            cp.start()
            sent.append(cp)
        _small_gather(x_ref, gathered, *refs[2 * NW + 4:])
        for w, sp in enumerate(SHARDED):
            landed = sp.half(outs[w], 1 - c)
            _remote(landed, landed, send_sems.at[w], recv_sems.at[w], sibling).wait_recv()
        for cp in sent:
            cp.wait_send()

    hbm = pl.BlockSpec(memory_space=pl.ANY)
    vmem = pl.BlockSpec(memory_space=pltpu.VMEM)
    out = pl.pallas_call(
        body, name="rs_pair_share",
        out_shape=[jax.ShapeDtypeStruct(sp.shard_shape, F32) for sp in SHARDED]
        + [jax.ShapeDtypeStruct((NDEV * m_per, n), F32)],
        in_specs=[hbm] * NW + [vmem], out_specs=[hbm] * NW + [vmem],
        input_output_aliases={w: w for w in range(NW)},
        scratch_shapes=[pltpu.SemaphoreType.DMA((NW,)), pltpu.SemaphoreType.DMA((NW,))] + SMALL_GATHER_SCRATCH,
        compiler_params=pltpu.CompilerParams(vmem_limit_bytes=VMEM_LIMIT),
    )(*halves, vector_block)
    return out[:NW], out[NW]


def _reduce_scatter(local, received, place, vector_block):
    halves = _device_sums([local[sp.name] for sp in SHARDED], [received[sp.name] for sp in SHARDED], place)
    return _pair_share(halves, vector_block)


def _place_bf16(sp, w, place):
    nd = len(sp.full_shape)

    def body(p_ref, w_ref, o_ref):
        o_ref[...] = w_ref[...].astype(BF16)

    return pl.pallas_call(
        body, name="place_" + sp.name,
        grid_spec=pltpu.PrefetchScalarGridSpec(
            num_scalar_prefetch=1, grid=(1,),
            in_specs=[pl.BlockSpec(sp.shard_shape, lambda i, p_ref: (0,) * nd)],
            out_specs=pl.BlockSpec(sp.shard_shape,
                                   lambda i, p_ref: tuple(p_ref[0] if a == sp.shard_axis else 0 for a in range(nd)))),
        out_shape=jax.ShapeDtypeStruct(sp.full_shape, BF16),
        compiler_params=_params(("arbitrary",)),
    )(place, w)


def _matmul_f32(a, b, name):
    def body(a_ref, b_ref, o_ref):
        o_ref[...] = jnp.dot(a_ref[...], b_ref[...], preferred_element_type=F32, precision=lax.Precision.HIGHEST)

    return pl.pallas_call(body, name=name, out_shape=jax.ShapeDtypeStruct((a.shape[0], b.shape[1]), F32),
                          compiler_params=pltpu.CompilerParams(vmem_limit_bytes=VMEM_LIMIT))(a, b)


def _sum_devices(stacked):
    def body(x_ref, o_ref):
        acc = x_ref[0]
        for d in range(1, NDEV):
            acc = acc + x_ref[d]
        o_ref[...] = acc

    return pl.pallas_call(body, name="sum_devices", out_shape=jax.ShapeDtypeStruct(stacked.shape[1:], F32),
                          compiler_params=pltpu.CompilerParams(vmem_limit_bytes=VMEM_LIMIT))(stacked)


ADAMW_STEPS = 8


def _adamw(ws, gs, ms, vs, name):
    n = len(ws)
    steps = ADAMW_STEPS if all(w.shape[0] % (8 * ADAMW_STEPS) == 0 for w in ws) else 1

    def body(*refs):
        ins, outs = refs[:4 * n], refs[4 * n:]
        for k in range(n):
            w_ref, g_ref, m_ref, v_ref = ins[k], ins[n + k], ins[2 * n + k], ins[3 * n + k]
            gv = g_ref[...]
            nm = ADAM_B1 * m_ref[...] + (1.0 - ADAM_B1) * gv
            nv = ADAM_B2 * v_ref[...] + (1.0 - ADAM_B2) * (gv * gv)
            m_hat = nm / (1.0 - ADAM_B1 ** ADAM_STEP)
            v_hat = nv / (1.0 - ADAM_B2 ** ADAM_STEP)
            outs[k][...] = -ADAM_LR * (m_hat / (jnp.sqrt(v_hat) + ADAM_EPS) + ADAM_WD * w_ref[...])
            outs[n + k][...] = nm
            outs[2 * n + k][...] = nv

    blks = [pl.BlockSpec((w.shape[0] // steps, w.shape[1]), lambda i: (i, 0)) for w in ws]
    shapes = [jax.ShapeDtypeStruct(w.shape, F32) for w in ws]
    out = pl.pallas_call(
        body, name="adamw_" + name, grid=(steps,), in_specs=blks * 4, out_specs=blks * 3, out_shape=shapes * 3,
        compiler_params=_params(("parallel",)),
    )(*ws, *gs, *ms, *vs)
    return out[:n], out[n:2 * n], out[2 * n:]


WEIGHT_NAMES = ("g_pre_mix", "g_post_mix", "g_pre_ffn", "g_post_ffn", "w_ada", "b_ada", "w_in", "w_pool",
                "pool_scale", "conv_w", "conv_b", "w_bout", "w_o", "w_up", "ffn_conv_w", "ffn_conv_b", "w_down")
MATRIX_NAMES = ("w_ada",) + tuple(sp.name for sp in SHARDED)
VECTOR_NAMES = tuple(n for n in WEIGHT_NAMES if n not in MATRIX_NAMES)

CW = D // NCHIP
FCW = F2 // NCHIP
ADA_W = DIN // NCHIP
COND_BLOCK = (8, 768)
GRAD_BLOCK = (8, 4864)


def _flat_pad(parts, shape):
    flat = jnp.concatenate([p.reshape(-1) for p in parts])
    return jnp.pad(flat, (0, shape[0] * shape[1] - flat.shape[0])).reshape(shape)


def _take(flat, offset, shape):
    size = 1
    for n in shape:
        size *= n
    return flat[offset:offset + size].reshape(shape), offset + size


def kernel(x, c, g_pre_mix, g_post_mix, g_pre_ffn, g_post_ffn, w_ada, b_ada, w_in, w_pool, pool_scale, conv_w, conv_b, w_bout, w_o, w_up, ffn_conv_w, ffn_conv_b, w_down, loss_target, m_g_pre_mix, m_g_post_mix, m_g_pre_ffn, m_g_post_ffn, m_w_ada, m_b_ada, m_w_in, m_w_pool, m_pool_scale, m_conv_w, m_conv_b, m_w_bout, m_w_o, m_w_up, m_ffn_conv_w, m_ffn_conv_b, m_w_down, v_g_pre_mix, v_g_post_mix, v_g_pre_ffn, v_g_post_ffn, v_w_ada, v_b_ada, v_w_in, v_w_pool, v_pool_scale, v_conv_w, v_conv_b, v_w_bout, v_w_o, v_w_up, v_ffn_conv_w, v_ffn_conv_b, v_w_down):
    weights = dict(g_pre_mix=g_pre_mix, g_post_mix=g_post_mix, g_pre_ffn=g_pre_ffn, g_post_ffn=g_post_ffn,
                   w_ada=w_ada, b_ada=b_ada, w_in=w_in, w_pool=w_pool, pool_scale=pool_scale, conv_w=conv_w,
                   conv_b=conv_b, w_bout=w_bout, w_o=w_o, w_up=w_up, ffn_conv_w=ffn_conv_w, ffn_conv_b=ffn_conv_b,
                   w_down=w_down)
    mom1 = dict(g_pre_mix=m_g_pre_mix, g_post_mix=m_g_post_mix, g_pre_ffn=m_g_pre_ffn, g_post_ffn=m_g_post_ffn,
                w_ada=m_w_ada, b_ada=m_b_ada, w_in=m_w_in, w_pool=m_w_pool, pool_scale=m_pool_scale,
                conv_w=m_conv_w, conv_b=m_conv_b, w_bout=m_w_bout, w_o=m_w_o, w_up=m_w_up,
                ffn_conv_w=m_ffn_conv_w, ffn_conv_b=m_ffn_conv_b, w_down=m_w_down)
    mom2 = dict(g_pre_mix=v_g_pre_mix, g_post_mix=v_g_post_mix, g_pre_ffn=v_g_pre_ffn, g_post_ffn=v_g_post_ffn,
                w_ada=v_w_ada, b_ada=v_b_ada, w_in=v_w_in, w_pool=v_w_pool, pool_scale=v_pool_scale,
                conv_w=v_conv_w, conv_b=v_conv_b, w_bout=v_w_bout, w_o=v_w_o, w_up=v_w_up,
                ffn_conv_w=v_ffn_conv_w, ffn_conv_b=v_ffn_conv_b, w_down=v_w_down)

    chip = 2 * lax.axis_index("x") + lax.axis_index("y")
    core = lax.axis_index("c")
    dev = 2 * chip + core
    place = jnp.stack([chip, core, dev]).astype(jnp.int32)

    cond = _all_gather_small(_flat_pad([c, conv_w, ffn_conv_w], COND_BLOCK), "gather_cond")
    cond = cond.reshape(NDEV, -1)
    c_all = cond[:, :D]
    by_chip = cond[0::2]
    conv_w_full = by_chip[:, D:D + 3 * CW].reshape(NCHIP, 3, CW).transpose(1, 0, 2).reshape(3, D)
    ffn_w_full = by_chip[:, D + 3 * CW:D + 3 * CW + 3 * FCW].reshape(NCHIP, 3, FCW).transpose(1, 0, 2).reshape(3, F2)

    mod_cols = _all_gather_small(_matmul_f32(c_all, w_ada[0], "ada_mod"), "gather_mod")
    mod_cols = mod_cols.reshape(NDEV, NDEV, ADA_W)[0::2]
    mod = lax.dynamic_index_in_dim(mod_cols, dev, axis=1, keepdims=False).reshape(6, D) + b_ada.reshape(6, D)
    vec_d = jnp.concatenate([mod, g_pre_mix, g_post_mix, g_pre_ffn, g_post_ffn, pool_scale, conv_b, conv_w_full,
                             jnp.zeros((VD_ROWS - 15, D), F32)], axis=0)
    vec_f = jnp.concatenate([ffn_w_full, ffn_conv_b, jnp.zeros((FV_ROWS - 4, F2), F32)], axis=0)

    placed = [_place_bf16(sp, weights[sp.name][0], place) for sp in SHARDED]
    loss_blk, dx, vecs, local, received = _local_step(x[0], loss_target[0], vec_d, vec_f, placed, place)

    dmod = [vecs[n] for n in ("dsh1", "dsc1", "dgt1", "dsh2", "dsc2", "dgt2")]
    small = [vecs["dg_pre_mix"], vecs["dg_post_mix"], vecs["dg_pre_ffn"], vecs["dg_post_ffn"]] + dmod + [
        vecs["dpool_scale"], vecs["dconv_w"], vecs["dconv_b"], vecs["dffn_conv_w"], vecs["dffn_conv_b"],
        loss_blk[0]]
    reduced, gathered = _reduce_scatter(local, received, place, _flat_pad(small, GRAD_BLOCK))
    total = _sum_devices(gathered.reshape((NDEV,) + GRAD_BLOCK)).reshape(-1)
    vgrad = {}
    off = 0
    for n in ("g_pre_mix", "g_post_mix", "g_pre_ffn", "g_post_ffn"):
        vgrad[n], off = _take(total, off, (1, D))
    dmod_off = off
    vgrad["b_ada"], off = _take(total, off, (1, DIN))
    vgrad["pool_scale"], off = _take(total, off, (1, D))
    g_conv_w, off = _take(total, off, (3, D))
    vgrad["conv_w"] = lax.dynamic_slice_in_dim(g_conv_w, chip * CW, CW, axis=1)[None]
    vgrad["conv_b"], off = _take(total, off, (1, D))
    g_ffn_w, off = _take(total, off, (3, F2))
    vgrad["ffn_conv_w"] = lax.dynamic_slice_in_dim(g_ffn_w, chip * FCW, FCW, axis=1)[None]
    vgrad["ffn_conv_b"], off = _take(total, off, (1, F2))
    loss = total[off]

    dmod_all = gathered.reshape(NDEV, -1)[:, dmod_off:dmod_off + DIN]
    dmod_cols = lax.dynamic_slice_in_dim(dmod_all, chip * ADA_W, ADA_W, axis=1)
    g_ada = _matmul_f32(jnp.pad(c_all.T, ((0, 0), (0, 128 - NDEV))), jnp.pad(dmod_cols, ((0, 128 - NDEV), (0, 0))),
                        "ada_wgrad")

    mgrad = {"w_ada": g_ada}
    for sp, g in zip(SHARDED, reduced):
        mgrad[sp.name] = g

    grad, delta, new_m, new_v = {}, {}, {}, {}
    two_d = lambda tree: [tree[n].reshape(-1, weights[n].shape[-1]) for n in MATRIX_NAMES]
    ds, nms, nvs = _adamw(two_d(weights), two_d(mgrad), two_d(mom1), two_d(mom2), "matrices")
    for n, d, nm, nv in zip(MATRIX_NAMES, ds, nms, nvs):
        shape = weights[n].shape
        grad[n], delta[n], new_m[n], new_v[n] = (a.reshape(shape) for a in (mgrad[n], d, nm, nv))
    flat = lambda tree: [jnp.concatenate([tree[n].reshape(1, -1) for n in VECTOR_NAMES], axis=1)]
    (d,), (nm,), (nv,) = _adamw(flat(weights), flat(vgrad), flat(mom1), flat(mom2), "vectors")
    off = 0
    for n in VECTOR_NAMES:
        shape = weights[n].shape
        grad[n] = vgrad[n].reshape(shape)
        delta[n], _ = _take(d[0], off, shape)
        new_m[n], _ = _take(nm[0], off, shape)
        new_v[n], off = _take(nv[0], off, shape)

    return (loss, dx[None], *[grad[n] for n in WEIGHT_NAMES], *[delta[n] for n in WEIGHT_NAMES],
            *[new_m[n] for n in WEIGHT_NAMES], *[new_v[n] for n in WEIGHT_NAMES])
```

```python
import jax
import jax.numpy as jnp
from jax import lax
from jax.experimental import pallas as pl
from jax.experimental.pallas import tpu as pltpu

F32 = jnp.float32
BF16 = jnp.bfloat16

D = 1024
DIN = 6 * D
F = 2816
F2 = 2 * F
NG = 4
GW = D // NG
POOL_CARRY = 16
CONV_CARRY = 3
EPS = 1e-6
NCHIP = 4
NDEV = 8

ADAM_LR = 0.001
ADAM_B1 = 0.9
ADAM_B2 = 0.999
ADAM_EPS = 1e-08
ADAM_WD = 0.01
ADAM_STEP = 10

VMEM_LIMIT = 60 * 1024 * 1024

(V_SH1, V_SC1, V_GT1, V_SH2, V_SC2, V_GT2, V_GPRE1, V_GPOST1, V_GPRE2, V_GPOST2,
 V_PSCALE, V_CB, V_CW0, V_CW1, V_CW2) = range(15)
VD_ROWS = 16
FV_W0, FV_W1, FV_W2, FV_B = range(4)
FV_ROWS = 8

MESH = pl.DeviceIdType.MESH


def _params(sem=None, vmem=VMEM_LIMIT):
    return pltpu.CompilerParams(dimension_semantics=sem, vmem_limit_bytes=vmem)


def _row(ref, r):
    return ref[r:r + 1, :]


def _load_once(pairs, sem):
    @pl.when(pl.program_id(0) == 0)
    def _():
        copies = [pltpu.make_async_copy(src, dst, sem.at[n]) for n, (src, dst) in enumerate(pairs)]
        for cp in copies:
            cp.start()
        for cp in copies:
            cp.wait()


def _dot(a, b):
    return jnp.dot(a, b, preferred_element_type=F32)


def _dot_nt(a, b):
    return lax.dot_general(a, b, (((1,), (1,)), ((), ())), preferred_element_type=F32)


BLK = 256
SEG = BLK // 8


def _load_rows(ref, ts):
    blocks = [jnp.swapaxes(ref[b * BLK:(b + 1) * BLK, :].reshape(8, SEG, ref.shape[-1]), 0, 1).reshape(BLK, -1)
              for b in range(ts // BLK)]
    return jnp.concatenate(blocks, axis=0)


def _store_rows(ref, val, ts):
    for b in range(ts // BLK):
        blk = val[b * BLK:(b + 1) * BLK, :].reshape(SEG, 8, val.shape[-1])
        ref[b * BLK:(b + 1) * BLK, :] = jnp.swapaxes(blk, 0, 1).reshape(BLK, -1)


def _times(t0):
    p = lax.broadcasted_iota(jnp.int32, (BLK, 1), 0)
    return t0 + (p & 7) * SEG + (p >> 3)


def _before(x, carry, s):
    x3 = x.reshape(SEG, 8, x.shape[-1])
    tail = pltpu.roll(x3[SEG - s:], 1, 1)
    row = lax.broadcasted_iota(jnp.int32, tail.shape, 1)
    out = jnp.concatenate([jnp.where(row == 0, carry, tail), x3[:SEG - s]], axis=0)
    return out.reshape(x.shape), tail


def _after(x, carry, s):
    x3 = x.reshape(SEG, 8, x.shape[-1])
    head = pltpu.roll(x3[:s], 7, 1)
    row = lax.broadcasted_iota(jnp.int32, head.shape, 1)
    out = jnp.concatenate([x3[s:], jnp.where(row == 7, carry, head)], axis=0)
    return out.reshape(x.shape), head


def _causal_conv(x, carry, cols, w0, w1, w2, b):
    x1, carry[0:1, :, cols] = _before(x, carry[0:1, :, cols], 1)
    x2, carry[1:3, :, cols] = _before(x, carry[1:3, :, cols], 2)
    return b + w2 * x + w1 * x1 + w0 * x2


def _causal_conv_bwd(dy, carry, cols, w0, w1, w2):
    d1, carry[0:1, :, cols] = _after(dy, carry[0:1, :, cols], 1)
    d2, carry[1:3, :, cols] = _after(dy, carry[1:3, :, cols], 2)
    return w2 * dy + w1 * d1 + w0 * d2, d1, d2


def _pool_counts(t0, g):
    return jnp.minimum((_times(t0) + 1).astype(F32), float(2 << g))


def _rms(x):
    return lax.rsqrt(jnp.mean(x * x, axis=-1, keepdims=True) + EPS)


def _rms_bwd(dn, n, r):
    return r * (dn - n * jnp.mean(dn * n, axis=-1, keepdims=True))


def _colsum(x):
    return jnp.sum(x, axis=0, keepdims=True)


def _gelu_and_grad(x):
    k, a = 0.7978845608028654, 0.044715
    x2 = x * x
    th1 = 1.0 + jnp.tanh(x * (x2 * (k * a) + k))
    hx = 0.5 * x
    gelu = hx * th1
    dgelu = 0.5 * th1 + (hx * (th1 * (2.0 - th1))) * (x2 * (3.0 * k * a) + k)
    return gelu, dgelu


def _fwd_proj(x, vec_d, placed_in, placed_rest, place, ts):
    s = x.shape[0]
    nt = s // ts
    cw = DIN // NCHIP
    sp_in = SHARDED[0]
    gather = _WeightGather(SHARDED[1:4])
    n = gather.n

    def body(*refs):
        p_ref, x_ref, v_ref = refs[:3]
        proj_ref, h1_ref, w_full = refs[4 + n:7 + n]
        rest = refs[7 + n:7 + 2 * n]
        w_vmem, h1_all, sem, in_send, in_recv, send_sems, recv_sems = refs[7 + 2 * n:]
        j, i = pl.program_id(0), pl.program_id(1)
        x_, y_, c, k_me, _, _ = _mesh_place()
        sibling = (x_, y_, 1 - c)

        def peer(t):
            return (x_ ^ (t >> 1), y_ ^ (t & 1))

        def w_in_sends():
            mine = sp_in.piece(w_full, k_me, c)
            return [_remote(mine, mine, in_send.at[t - 1], in_recv.at[t - 1], (*peer(t), c)) for t in (1, 2, 3)]

        def load_block(k):
            cp = pltpu.make_async_copy(sp_in.shard(w_full, k), w_vmem.at[k], sem.at[0])
            cp.start()
            cp.wait()

        @pl.when((j == 0) & (i == 0))
        def _():
            for cp in w_in_sends()[:2]:
                cp.start()
            load_block(k_me)

        @pl.when((j == 1) & (i == 0))
        def _():
            for cp in w_in_sends()[:2]:
                cp.wait_send()
            w_in_sends()[2].start()
            gather.start(rest, send_sems, recv_sems)

        for t in (1, 2, 3):
            @pl.when((j == t) & (i == 0))
            def _(t=t):
                k = k_me ^ t
                landed = sp_in.piece(w_full, k, c)
                _remote(landed, landed, in_send.at[t - 1], in_recv.at[t - 1], (*peer(t), c)).wait_recv()
                _remote(landed, landed, in_send.at[2 + t], in_recv.at[2 + t], sibling).start()
                other = sp_in.piece(w_full, k, 1 - c)
                _remote(other, other, in_send.at[2 + t], in_recv.at[2 + t], sibling).wait_recv()
                load_block(k)

        @pl.when(j == 0)
        def _():
            xv = _load_rows(x_ref, ts)
            n1 = xv * _rms(xv)
            h = n1 * (_row(v_ref, V_GPRE1) * (1.0 + _row(v_ref, V_SC1))) + _row(v_ref, V_SH1)
            hb = h.astype(BF16)
            h1_ref[...] = hb
            h1_all[i] = hb

        proj_ref[...] = _dot(h1_all[i], w_vmem[k_me ^ j]).astype(BF16)

        @pl.when((j == NCHIP - 1) & (i == nt - 1))
        def _():
            w_in_sends()[2].wait_send()
            for t in (1, 2, 3):
                landed = sp_in.piece(w_full, k_me ^ t, c)
                _remote(landed, landed, in_send.at[2 + t], in_recv.at[2 + t], sibling).wait_send()
            gather.finish(rest, send_sems, recv_sems)

    once = lambda w: pl.BlockSpec((ts, w), lambda j, i, p: (jnp.where(j == 0, i, nt - 1), 0))
    return pl.pallas_call(
        body, name="fwd_proj",
        grid_spec=pltpu.PrefetchScalarGridSpec(
            num_scalar_prefetch=1, grid=(NCHIP, nt),
            in_specs=[once(D), pl.BlockSpec((VD_ROWS, D), lambda j, i, p: (0, 0)),
                      pl.BlockSpec(memory_space=pl.ANY)] + gather.specs_any,
            out_specs=[pl.BlockSpec((ts, cw), lambda j, i, p: (i, p[0] ^ j)), once(D),
                       pl.BlockSpec(memory_space=pl.ANY)] + gather.specs_any,
            scratch_shapes=[pltpu.VMEM((NCHIP, D, cw), BF16), pltpu.VMEM((nt, ts, D), BF16),
                            pltpu.SemaphoreType.DMA((1,)),
                            pltpu.SemaphoreType.DMA((6,)), pltpu.SemaphoreType.DMA((6,))] + gather.scratch),
        out_shape=[jax.ShapeDtypeStruct((s, DIN), BF16), jax.ShapeDtypeStruct((s, D), BF16),
                   jax.ShapeDtypeStruct(sp_in.full_shape, BF16)] + gather.out_shape,
        input_output_aliases={3 + w: 2 + w for w in range(n + 1)},
        compiler_params=_params(("arbitrary", "arbitrary")),
    )(place, x, vec_d, placed_in, *placed_rest)


def _fwd_mix(proj, x, vec_d, w_pool, w_bout, w_o, placed_ffn, ts):
    s = x.shape[0]
    gather = _WeightGather(SHARDED[4:])
    n = gather.n

    def body(*refs):
        ins, outs, rest = refs[:6], refs[6 + n:14 + n], refs[14 + n:14 + 2 * n]
        scratch, sems = refs[14 + 2 * n:-2], refs[-2:]
        i = pl.program_id(0)
        nt = s // ts
        pl.when(i == 0)(lambda: gather.start(rest, *sems))
        pl.when(i == nt - 1 - nt // 8)(lambda: gather.forward(rest, *sems))
        compute(*ins, *outs, *scratch)
        pl.when(i == nt - 1)(lambda: gather.drain(rest, *sems))

    def compute(p_ref, x_ref, v_ref, wp_hbm, wb_hbm, wo_hbm,
                x1_ref, o_ref, pg_ref, q_ref, mg_ref, ya_ref, yb_ref, cv_ref,
                wp, wb, wo, carry_p, carry_v, sem):
        i = pl.program_id(0)
        _load_once([(wp_hbm, wp), (wb_hbm, wb), (wo_hbm, wo)], sem)

        @pl.when(i == 0)
        def _():
            carry_p[...] = jnp.zeros_like(carry_p)
            carry_v[...] = jnp.zeros_like(carry_v)

        t0 = i * ts
        for g in range(NG):
            cols = slice(g * GW, (g + 1) * GW)
            u = p_ref[:, cols].astype(F32)
            e = u
            for l in range(g + 1):
                slot = slice((1 << l) - 1, (2 << l) - 1)
                shifted, carry_p[slot, :, cols] = _before(e, carry_p[slot, :, cols], 1 << l)
                e = e + shifted
            pgb = (e / _pool_counts(t0, g) - u).astype(BF16)
            pg_ref[:, cols] = pgb
            ya_ref[:, cols] = _dot(pgb, wp[g]).astype(BF16)

        u_x = p_ref[:, D:2 * D].astype(F32)
        u_c = p_ref[:, 3 * D:4 * D].astype(F32)
        v = u_c * u_x
        cv = _causal_conv(v, carry_v, slice(None), _row(v_ref, V_CW0), _row(v_ref, V_CW1),
                          _row(v_ref, V_CW2), _row(v_ref, V_CB))
        cv_ref[...] = cv.astype(BF16)
        q = (p_ref[:, 2 * D:3 * D].astype(F32) * cv).astype(BF16)
        q_ref[...] = q
        y_b = _dot(q, wb[...])
        yb_ref[...] = y_b.astype(BF16)

        y_a = ya_ref[...].astype(F32) * _row(v_ref, V_PSCALE)
        merged = (jax.nn.sigmoid(p_ref[:, 4 * D:5 * D].astype(F32)) * y_a
                  + jax.nn.sigmoid(p_ref[:, 5 * D:6 * D].astype(F32)) * y_b).astype(BF16)
        mg_ref[...] = merged
        o = _dot(merged, wo[...])
        o_ref[...] = o.astype(BF16)
        x1_ref[...] = _load_rows(x_ref, ts) + _row(v_ref, V_GT1) * ((o * _rms(o)) * _row(v_ref, V_GPOST1))

    tile = lambda w: pl.BlockSpec((ts, w), lambda i: (i, 0))
    hbm = pl.BlockSpec(memory_space=pl.ANY)
    return pl.pallas_call(
        body, name="fwd_mix", grid=(s // ts,),
        in_specs=[tile(DIN), tile(D), pl.BlockSpec((VD_ROWS, D), lambda i: (0, 0)), hbm, hbm, hbm] + gather.specs_any,
        out_specs=[tile(D)] * 8 + gather.specs_any,
        out_shape=[jax.ShapeDtypeStruct((s, D), F32)] + [jax.ShapeDtypeStruct((s, D), BF16)] * 7 + gather.out_shape,
        input_output_aliases={6 + w: 8 + w for w in range(n)},
        scratch_shapes=[pltpu.VMEM((NG, GW, GW), BF16), pltpu.VMEM((D, D), BF16), pltpu.VMEM((D, D), BF16),
                        pltpu.VMEM((POOL_CARRY, 8, D), F32), pltpu.VMEM((CONV_CARRY, 8, D), F32),
                        pltpu.SemaphoreType.DMA((3,))] + gather.scratch,
        compiler_params=_params(("arbitrary",)),
    )(proj, x, vec_d, w_pool, w_bout, w_o, *placed_ffn)


def _fwd_ffn(x1, tgt, vec_d, vec_f, w_up, w_down, ts):
    s = x1.shape[0]

    def body(x1_ref, t_ref, v_ref, f_ref, wu_hbm, wd_hbm,
             up_ref, upc_ref, a_ref, h2_ref, dx2_ref, dff_ref, vo_ref, loss_ref,
             wu, wd, carry, sem):
        i = pl.program_id(0)
        _load_once([(wu_hbm, wu), (wd_hbm, wd)], sem)

        @pl.when(i == 0)
        def _():
            carry[...] = jnp.zeros_like(carry)
            vo_ref[...] = jnp.zeros_like(vo_ref)
            loss_ref[...] = jnp.zeros_like(loss_ref)

        x1v = x1_ref[...]
        n3 = x1v * _rms(x1v)
        h2 = (n3 * (_row(v_ref, V_GPRE2) * (1.0 + _row(v_ref, V_SC2))) + _row(v_ref, V_SH2)).astype(BF16)
        h2_ref[...] = h2

        ff = jnp.zeros((ts, D), F32)
        for lo, hi in FFN_SLABS_FWD:
            up = []
            for cols in (slice(lo, hi), slice(F + lo, F + hi)):
                u0 = _dot(h2, wu[:, cols])
                up_ref[:, cols] = u0.astype(BF16)
                y = _causal_conv(u0, carry, cols, f_ref[FV_W0:FV_W0 + 1, cols], f_ref[FV_W1:FV_W1 + 1, cols],
                                 f_ref[FV_W2:FV_W2 + 1, cols], f_ref[FV_B:FV_B + 1, cols])
                upc_ref[:, cols] = y.astype(BF16)
                up.append(y)
            gelu, _ = _gelu_and_grad(up[0])
            a = (gelu * up[1]).astype(BF16)
            a_ref[:, lo:hi] = a
            ff = ff + _dot(a, wd[lo:hi, :])

        r4 = _rms(ff)
        n4 = ff * r4
        gt2 = _row(v_ref, V_GT2)
        gpost = _row(v_ref, V_GPOST2)
        gate_gain = gt2 * gpost
        diff = (x1v + gate_gain * n4) - _load_rows(t_ref, ts)
        loss_ref[...] += jnp.full(loss_ref.shape, 0.5 / D * jnp.sum(diff * diff), F32)
        dx2_ref[...] = diff * (1.0 / D)
        s1 = _colsum(diff * n4)
        vo_ref[0:1, :] += s1 * (gpost * (1.0 / D))
        vo_ref[1:2, :] += s1 * (gt2 * (1.0 / D))
        dff_ref[...] = _rms_bwd(diff * (gate_gain * (1.0 / D)), n4, r4).astype(BF16)

    tile = lambda w: pl.BlockSpec((ts, w), lambda i: (i, 0))
    full = lambda r, w: pl.BlockSpec((r, w), lambda i: (0, 0))
    hbm = pl.BlockSpec(memory_space=pl.ANY)
    return pl.pallas_call(
        body, name="fwd_ffn", grid=(s // ts,),
        in_specs=[tile(D), tile(D), full(VD_ROWS, D), full(FV_ROWS, F2), hbm, hbm],
        out_specs=[tile(F2), tile(F2), tile(F), tile(D), tile(D), tile(D), full(8, D), full(8, 128)],
        out_shape=[jax.ShapeDtypeStruct((s, F2), BF16), jax.ShapeDtypeStruct((s, F2), BF16),
                   jax.ShapeDtypeStruct((s, F), BF16),
                   jax.ShapeDtypeStruct((s, D), BF16), jax.ShapeDtypeStruct((s, D), F32),
                   jax.ShapeDtypeStruct((s, D), BF16), jax.ShapeDtypeStruct((8, D), F32),
                   jax.ShapeDtypeStruct((8, 128), F32)],
        scratch_shapes=[pltpu.VMEM((D, F2), BF16), pltpu.VMEM((F, D), BF16), pltpu.VMEM((CONV_CARRY, 8, F2), F32),
                        pltpu.SemaphoreType.DMA((2,))],
        compiler_params=_params(("arbitrary",)),
    )(x1, tgt, vec_d, vec_f, w_up, w_down)


def _bwd_ffn(dff, dx2, x1, up0, upc, vec_d, vec_f, w_up, w_down, exchange, ex_grads, ts):
    s = x1.shape[0]
    nt = s // ts
    n = exchange.n

    def body(*refs):
        ins, grads = refs[:9], refs[9:9 + n]
        outs, recvs = refs[9 + n:13 + n], refs[13 + n:13 + 2 * n]
        scratch, sems = refs[13 + 2 * n:-2], refs[-2:]
        i = pl.program_id(0)
        pl.when(i == 0)(lambda: exchange.start(grads, recvs, *sems))
        compute(*ins, *outs, *scratch)
        pl.when(i == nt - 1)(lambda: exchange.finish(grads, recvs, *sems))

    def compute(dff_ref, dx2_ref, x1_ref, up_ref, upc_ref, v_ref, f_ref, wu_hbm, wd_hbm,
                dx1_ref, dup_ref, vo_ref, fo_ref, wu, wd, carry, sem):
        i = pl.program_id(0)
        _load_once([(wu_hbm, wu), (wd_hbm, wd)], sem)

        @pl.when(i == 0)
        def _():
            carry[...] = jnp.zeros_like(carry)
            vo_ref[...] = jnp.zeros_like(vo_ref)
            fo_ref[...] = jnp.zeros_like(fo_ref)

        dffb = dff_ref[...]

        dh2 = jnp.zeros((ts, D), F32)
        for lo, hi in FFN_SLABS_BWD:
            slabs = (slice(lo, hi), slice(F + lo, F + hi))
            gelu, dgelu = _gelu_and_grad(upc_ref[:, slabs[0]].astype(F32))
            da = _dot_nt(dffb, wd[lo:hi, :])
            dups = (da * upc_ref[:, slabs[1]].astype(F32) * dgelu, da * gelu)
            for cols, dup in zip(slabs, dups):
                du0, d1, d2 = _causal_conv_bwd(dup, carry, cols, f_ref[FV_W0:FV_W0 + 1, cols],
                                               f_ref[FV_W1:FV_W1 + 1, cols], f_ref[FV_W2:FV_W2 + 1, cols])
                u0 = up_ref[:, cols].astype(F32)
                fo_ref[FV_B:FV_B + 1, cols] += _colsum(dup)
                fo_ref[FV_W2:FV_W2 + 1, cols] += _colsum(dup * u0)
                fo_ref[FV_W1:FV_W1 + 1, cols] += _colsum(d1 * u0)
                fo_ref[FV_W0:FV_W0 + 1, cols] += _colsum(d2 * u0)
                du0 = du0.astype(BF16)
                dup_ref[:, cols] = du0
                dh2 = dh2 + _dot_nt(du0, wu[:, cols])

        x1v = x1_ref[...]
        r3 = _rms(x1v)
        n3 = x1v * r3
        gpre = _row(v_ref, V_GPRE2)
        sc = 1.0 + _row(v_ref, V_SC2)
        vo_ref[0:1, :] += _colsum(dh2)
        s2 = _colsum(dh2 * n3)
        vo_ref[1:2, :] += s2 * gpre
        vo_ref[2:3, :] += s2 * sc
        dx1_ref[...] = dx2_ref[...] + _rms_bwd(dh2 * (gpre * sc), n3, r3)

    rev = lambda w: pl.BlockSpec((ts, w), lambda i: (nt - 1 - i, 0))
    full = lambda r, w: pl.BlockSpec((r, w), lambda i: (0, 0))
    hbm = pl.BlockSpec(memory_space=pl.ANY)
    return pl.pallas_call(
        body, name="bwd_ffn", grid=(nt,),
        in_specs=[rev(D), rev(D), rev(D), rev(F2), rev(F2), full(VD_ROWS, D), full(FV_ROWS, F2), hbm, hbm]
        + exchange.specs_any,
        out_specs=[rev(D), rev(F2), full(8, D), full(FV_ROWS, F2)] + exchange.specs_any,
        out_shape=[jax.ShapeDtypeStruct((s, D), F32), jax.ShapeDtypeStruct((s, F2), BF16),
                   jax.ShapeDtypeStruct((8, D), F32), jax.ShapeDtypeStruct((FV_ROWS, F2), F32)] + exchange.out_shape,
        scratch_shapes=[pltpu.VMEM((D, F2), BF16), pltpu.VMEM((F, D), BF16), pltpu.VMEM((CONV_CARRY, 8, F2), F32),
                        pltpu.SemaphoreType.DMA((2,))] + exchange.scratch,
        compiler_params=_params(("arbitrary",)),
    )(dff, dx2, x1, up0, upc, vec_d, vec_f, w_up, w_down, *ex_grads)


def _bwd_mix(dx1, o, proj, cv, ya0, yb, merged, q, pg, vec_d, w_pool, w_bout, w_o, exchange, ex_grads, ts):
    s = dx1.shape[0]
    nt = s // ts
    n = exchange.n
    assert ts == BLK and nt % 2 == 0, (ts, nt)

    def body(*refs):
        ins, grads = refs[:13], refs[13:13 + n]
        outs, recvs = refs[13 + n:18 + n], refs[18 + n:18 + 2 * n]
        scratch, sems = refs[18 + 2 * n:-2], refs[-2:]
        i = pl.program_id(0)
        pl.when(i == 0)(lambda: exchange.start(grads, recvs, *sems))
        compute(*ins, *outs, *scratch)
        pl.when(i == nt - 1)(lambda: exchange.finish(grads, recvs, *sems))

    def compute(dx1_ref, o_ref, p_ref, cv_ref, ya_ref, yb_ref, mg_ref, q_ref, pg_ref, v_ref, wp_hbm, wb_hbm, wo_hbm,
                dp_ref, vo_ref, go_ref, gb_ref, gp_ref, wp, wb, wo, carry_d, carry_e, acc_o, acc_b, acc_p, stash, sem):
        i = pl.program_id(0)
        _load_once([(wp_hbm, wp), (wb_hbm, wb), (wo_hbm, wo)], sem)

        @pl.when(i == 0)
        def _():
            carry_d[...] = jnp.zeros_like(carry_d)
            carry_e[...] = jnp.zeros_like(carry_e)
            vo_ref[...] = jnp.zeros_like(vo_ref)
            acc_o[...] = jnp.zeros_like(acc_o)
            acc_b[...] = jnp.zeros_like(acc_b)
            acc_p[...] = jnp.zeros_like(acc_p)

        t0 = (nt - 1 - i) * ts
        dx1v = dx1_ref[...]
        ov = o_ref[...].astype(F32)
        r2 = _rms(ov)
        n2 = ov * r2
        gpost = _row(v_ref, V_GPOST1)
        gt1 = _row(v_ref, V_GT1)
        s1 = _colsum(dx1v * n2)
        vo_ref[0:1, :] += s1 * gpost
        vo_ref[1:2, :] += s1 * gt1
        dob = _rms_bwd(dx1v * (gt1 * gpost), n2, r2).astype(BF16)
        dmerged = _dot_nt(dob, wo[...])

        ya0 = ya_ref[...].astype(F32)
        pscale = _row(v_ref, V_PSCALE)
        sa = jax.nn.sigmoid(p_ref[:, 4 * D:5 * D].astype(F32))
        dp_ref[:, 4 * D:5 * D] = (dmerged * (ya0 * pscale) * sa * (1.0 - sa)).astype(BF16)
        dy_a = dmerged * sa
        vo_ref[2:3, :] += _colsum(dy_a * ya0)
        dya0 = (dy_a * pscale).astype(BF16)

        sb = jax.nn.sigmoid(p_ref[:, 5 * D:6 * D].astype(F32))
        dp_ref[:, 5 * D:6 * D] = (dmerged * yb_ref[...].astype(F32) * sb * (1.0 - sb)).astype(BF16)
        dy_b = (dmerged * sb).astype(BF16)
        dq = _dot_nt(dy_b, wb[...])

        u_x = p_ref[:, D:2 * D].astype(F32)
        u_b = p_ref[:, 2 * D:3 * D].astype(F32)
        u_c = p_ref[:, 3 * D:4 * D].astype(F32)
        w0, w1, w2 = _row(v_ref, V_CW0), _row(v_ref, V_CW1), _row(v_ref, V_CW2)
        dp_ref[:, 2 * D:3 * D] = (dq * cv_ref[...].astype(F32)).astype(BF16)
        dcv = dq * u_b
        dv, d1, d2 = _causal_conv_bwd(dcv, carry_d, slice(None), w0, w1, w2)
        v = u_c * u_x
        vo_ref[3:4, :] += _colsum(dcv)
        vo_ref[4:5, :] += _colsum(d2 * v)
        vo_ref[5:6, :] += _colsum(d1 * v)
        vo_ref[6:7, :] += _colsum(dcv * v)
        dp_ref[:, D:2 * D] = (dv * u_c).astype(BF16)
        dp_ref[:, 3 * D:4 * D] = (dv * u_x).astype(BF16)

        for g in range(NG):
            cols = slice(g * GW, (g + 1) * GW)
            dpg = _dot_nt(dya0[:, cols], wp[g])
            e = dpg / _pool_counts(t0, g)
            for l in range(g + 1):
                slot = slice((1 << l) - 1, (2 << l) - 1)
                shifted, carry_e[slot, :, cols] = _after(e, carry_e[slot, :, cols], 1 << l)
                e = e + shifted
            dp_ref[:, cols] = (e - dpg).astype(BF16)

        pairs = ((mg_ref[...], dob), (q_ref[...], dy_b), (pg_ref[...], dya0))

        @pl.when(i % 2 == 0)
        def _():
            for k, (a, b) in enumerate(pairs):
                stash[2 * k] = a
                stash[2 * k + 1] = b

        @pl.when(i % 2 == 1)
        def _():
            def both(k):
                a, b = pairs[k]
                return jnp.concatenate([stash[2 * k], a], axis=0), jnp.concatenate([stash[2 * k + 1], b], axis=0)

            acc_o[...] += _dot_tn(*both(0))
            acc_b[...] += _dot_tn(*both(1))
            a, b = both(2)
            for g in range(NG):
                cols = slice(g * GW, (g + 1) * GW)
                acc_p[g] += _dot_tn(a[:, cols], b[:, cols])

        @pl.when(i == nt - 1)
        def _():
            go_ref[...] = acc_o[...].astype(BF16)
            gb_ref[...] = acc_b[...].astype(BF16)
            gp_ref[...] = acc_p[...].astype(BF16)

    rev = lambda w: pl.BlockSpec((ts, w), lambda i: (nt - 1 - i, 0))
    hbm = pl.BlockSpec(memory_space=pl.ANY)
    whole = lambda shape: pl.BlockSpec(shape, lambda i: (0,) * len(shape))
    return pl.pallas_call(
        body, name="bwd_mix", grid=(nt,),
        in_specs=[rev(D), rev(D), rev(DIN)] + [rev(D)] * 6 + [whole((VD_ROWS, D)), hbm, hbm, hbm] + exchange.specs_any,
        out_specs=[rev(DIN), whole((8, D)), whole((D, D)), whole((D, D)), whole((NG, GW, GW))] + exchange.specs_any,
        out_shape=[jax.ShapeDtypeStruct((s, DIN), BF16), jax.ShapeDtypeStruct((8, D), F32),
                   jax.ShapeDtypeStruct((D, D), BF16), jax.ShapeDtypeStruct((D, D), BF16),
                   jax.ShapeDtypeStruct((NG, GW, GW), BF16)] + exchange.out_shape,
        scratch_shapes=[pltpu.VMEM((NG, GW, GW), BF16), pltpu.VMEM((D, D), BF16), pltpu.VMEM((D, D), BF16),
                        pltpu.VMEM((CONV_CARRY, 8, D), F32), pltpu.VMEM((POOL_CARRY, 8, D), F32),
                        pltpu.VMEM((D, D), F32), pltpu.VMEM((D, D), F32), pltpu.VMEM((NG, GW, GW), F32),
                        pltpu.VMEM((6, ts, D), BF16),
                        pltpu.SemaphoreType.DMA((3,))] + exchange.scratch,
        compiler_params=_params(("arbitrary",)),
    )(dx1, o, proj, cv, ya0, yb, merged, q, pg, vec_d, w_pool, w_bout, w_o, *ex_grads)


def _bwd_in(dproj, dx1, x, vec_d, w_in, exchange, ex_grads, ts):
    s = x.shape[0]
    nt = s // ts
    n = exchange.n

    def body(*refs):
        ins, grads = refs[:5], refs[5:5 + n]
        outs, recvs = refs[5 + n:7 + n], refs[7 + n:7 + 2 * n]
        scratch, sems = refs[7 + 2 * n:-2], refs[-2:]
        i = pl.program_id(0)
        pl.when(i == 0)(lambda: exchange.start(grads, recvs, *sems))
        compute(*ins, *outs, *scratch)
        pl.when(i == nt - 1)(lambda: exchange.finish(grads, recvs, *sems))

    def compute(dp_ref, dx1_ref, x_ref, v_ref, w_hbm, dx_ref, vo_ref, w_vmem, sem):
        _load_once([(w_hbm, w_vmem)], sem)

        @pl.when(pl.program_id(0) == 0)
        def _():
            vo_ref[...] = jnp.zeros_like(vo_ref)

        dh1 = _dot_nt(dp_ref[...], w_vmem[...])
        xv = _load_rows(x_ref, ts)
        r1 = _rms(xv)
        n1 = xv * r1
        gpre = _row(v_ref, V_GPRE1)
        sc = 1.0 + _row(v_ref, V_SC1)
        vo_ref[0:1, :] += _colsum(dh1)
        s1 = _colsum(dh1 * n1)
        vo_ref[1:2, :] += s1 * gpre
        vo_ref[2:3, :] += s1 * sc
        _store_rows(dx_ref, dx1_ref[...] + _rms_bwd(dh1 * (gpre * sc), n1, r1), ts)

    tile = lambda w: pl.BlockSpec((ts, w), lambda i: (i, 0))
    return pl.pallas_call(
        body, name="bwd_in", grid=(s // ts,),
        in_specs=[tile(DIN), tile(D), tile(D), pl.BlockSpec((VD_ROWS, D), lambda i: (0, 0)),
                  pl.BlockSpec(memory_space=pl.ANY)] + exchange.specs_any,
        out_specs=[tile(D), pl.BlockSpec((8, D), lambda i: (0, 0))] + exchange.specs_any,
        out_shape=[jax.ShapeDtypeStruct((s, D), F32), jax.ShapeDtypeStruct((8, D), F32)] + exchange.out_shape,
        scratch_shapes=[pltpu.VMEM((D, DIN), BF16), pltpu.SemaphoreType.DMA((1,))] + exchange.scratch,
        compiler_params=_params(("arbitrary",)),
    )(dproj, dx1, x, vec_d, w_in, *ex_grads)


def _dot_tn(a, b):
    return lax.dot_general(a, b, (((0,), (0,)), ((), ())), preferred_element_type=F32)


def _wgrad(a, b, tm, tn, ts, name, dtype, exchange=None, ex_grads=()):
    s, m = a.shape
    nn = b.shape[1]
    grid = (m // tm, nn // tn, s // ts)
    n = exchange.n if exchange else 0

    def body(*refs):
        a_ref, b_ref = refs[:2]
        grads = refs[2:2 + n]
        o_ref = refs[2 + n]
        recvs = refs[3 + n:3 + 2 * n]
        acc = refs[3 + 2 * n]
        sems = refs[4 + 2 * n:]
        i, j, k = pl.program_id(0), pl.program_id(1), pl.program_id(2)
        if exchange:
            pl.when((i == 0) & (j == 0) & (k == 0))(lambda: exchange.start(grads, recvs, *sems))
        part = _dot_tn(a_ref[...], b_ref[...])

        @pl.when(k == 0)
        def _():
            acc[...] = part

        @pl.when(k > 0)
        def _():
            acc[...] += part

        @pl.when(k == grid[2] - 1)
        def _():
            o_ref[...] = acc[...].astype(dtype)

        if exchange:
            pl.when((i == grid[0] - 1) & (j == grid[1] - 1) & (k == grid[2] - 1))(
                lambda: exchange.finish(grads, recvs, *sems))

    hosted = exchange.specs_any if exchange else []
    return pl.pallas_call(
        body, name=name, grid=grid,
        in_specs=[pl.BlockSpec((ts, tm), lambda i, j, k: (k, i)), pl.BlockSpec((ts, tn), lambda i, j, k: (k, j))]
        + hosted,
        out_specs=[pl.BlockSpec((tm, tn), lambda i, j, k: (i, j))] + hosted,
        out_shape=[jax.ShapeDtypeStruct((m, nn), dtype)] + (exchange.out_shape if exchange else []),
        scratch_shapes=[pltpu.VMEM((tm, tn), F32)] + (exchange.scratch if exchange else []),
        compiler_params=_params(("arbitrary", "arbitrary", "arbitrary")),
    )(a, b, *ex_grads)


FFN_SLABS_FWD = ((0, 2816),)
FFN_SLABS_BWD = ((0, 1536), (1536, 2816))
TS_PROJ = 512
TS_MIX = 256
TS_FFN = 256
TS_WGRAD = 2048


def _local_step(x, tgt, vec_d, vec_f, placed, place):
    s = x.shape[0]
    tw = min(TS_WGRAD, s)
    sp_in, sp_pool, sp_bout, sp_o, sp_up, sp_down = SHARDED
    proj, h1, w_in, w_pool, w_bout, w_o = _fwd_proj(x, vec_d, placed[0], placed[1:4], place, min(TS_PROJ, s))
    x1, o, pg, q, merged, ya0, yb, cv, w_up, w_down = _fwd_mix(proj, x, vec_d, w_pool, w_bout, w_o, placed[4:],
                                                               min(TS_MIX, s))
    up0, upc, a, h2, dx2, dff, vo_f, loss = _fwd_ffn(x1, tgt, vec_d, vec_f, w_up, w_down, min(TS_FFN, s))
    g_down, = _wgrad(a, dff, F // 2, D, tw, "wgrad_down", BF16)
    dx1, dup0, vo_b, fo, r_down = _bwd_ffn(dff, dx2, x1, up0, upc, vec_d, vec_f, w_up, w_down,
                                           _GradExchange([sp_down]), [g_down], min(TS_FFN, s))
    g_up, = _wgrad(h2, dup0, D, F2 // NCHIP, tw, "wgrad_up", BF16)
    dproj, vo_m, g_o, g_bout, g_pool, r_up = _bwd_mix(dx1, o, proj, cv, ya0, yb, merged, q, pg, vec_d,
                                                      w_pool, w_bout, w_o, _GradExchange([sp_up]), [g_up],
                                                      min(TS_MIX, s))
    g_in, r_pool, r_bout, r_o = _wgrad(h1, dproj, D, DIN // NCHIP, tw, "wgrad_in", BF16,
                                       _GradExchange([sp_pool, sp_bout, sp_o]), [g_pool, g_bout, g_o])
    dx, vo_i, r_in = _bwd_in(dproj, dx1, x, vec_d, w_in, _GradExchange([sp_in]), [g_in], min(TS_PROJ, s))
    vecs = dict(
        dsh1=vo_i[0], dsc1=vo_i[1], dg_pre_mix=vo_i[2],
        dgt1=vo_m[0], dg_post_mix=vo_m[1], dpool_scale=vo_m[2], dconv_b=vo_m[3],
        dconv_w=vo_m[4:7],
        dsh2=vo_b[0], dsc2=vo_b[1], dg_pre_ffn=vo_b[2],
        dgt2=vo_f[0], dg_post_ffn=vo_f[1],
        dffn_conv_w=fo[FV_W0:FV_W2 + 1], dffn_conv_b=fo[FV_B],
    )
    local = dict(w_in=g_in, w_pool=g_pool, w_bout=g_bout, w_o=g_o, w_up=g_up, w_down=g_down)
    received = dict(w_in=r_in, w_pool=r_pool, w_bout=r_bout, w_o=r_o, w_up=r_up, w_down=r_down)
    return loss, dx, vecs, local, received


def _aligned(offset, n):
    return offset if isinstance(offset, int) else pl.multiple_of(offset, n)


class _Sharded:
    def __init__(self, name, full_shape, shard_axis, half_axis):
        self.name = name
        self.full_shape = full_shape
        self.shard_axis = shard_axis
        self.half_axis = half_axis
        self.shard_shape = tuple(n // NCHIP if a == shard_axis else n for a, n in enumerate(full_shape))
        self.piece_shape = tuple(n // 2 if a == half_axis else n for a, n in enumerate(self.shard_shape))

    def piece(self, full_ref, k, h):
        idx = []
        for a, n in enumerate(self.piece_shape):
            if a == self.shard_axis and a == self.half_axis:
                idx.append(pl.ds(_aligned((2 * k + h) * n, n), n))
            elif a == self.shard_axis:
                idx.append(pl.ds(_aligned(k * n, n), n))
            elif a == self.half_axis:
                idx.append(pl.ds(_aligned(h * n, n), n))
            else:
                idx.append(slice(None))
        return full_ref.at[tuple(idx)]

    def shard(self, full_ref, k):
        n = self.shard_shape[self.shard_axis]
        idx = [pl.ds(_aligned(k * n, n), n) if a == self.shard_axis else slice(None)
               for a in range(len(self.full_shape))]
        return full_ref.at[tuple(idx)]

    def half(self, shard_ref, h):
        n = self.piece_shape[self.half_axis]
        idx = [pl.ds(_aligned(h * n, n), n) if a == self.half_axis else slice(None)
               for a in range(len(self.full_shape))]
        return shard_ref.at[tuple(idx)]

SHARDED = (
    _Sharded("w_in", (D, DIN), 1, 0),
    _Sharded("w_pool", (NG, GW, GW), 1, 0),
    _Sharded("w_bout", (D, D), 0, 0),
    _Sharded("w_o", (D, D), 0, 0),
    _Sharded("w_up", (D, F2), 1, 0),
    _Sharded("w_down", (F, D), 0, 0),
)
NW = len(SHARDED)


def _mesh_place():
    x, y, c = lax.axis_index("x"), lax.axis_index("y"), lax.axis_index("c")
    chips = [(1 - x, y), (x, 1 - y), (1 - x, 1 - y)]
    return x, y, c, 2 * x + y, chips, [2 * px + py for px, py in chips]


def _remote(src, dst, send_sem, recv_sem, device):
    return pltpu.make_async_remote_copy(src_ref=src, dst_ref=dst, send_sem=send_sem, recv_sem=recv_sem,
                                        device_id=device, device_id_type=MESH)


SMALL_GATHER_SCRATCH = [pltpu.SemaphoreType.DMA((7,)), pltpu.SemaphoreType.DMA((7,)), pltpu.SemaphoreType.DMA]


def _small_gather(x_ref, out_ref, send_sems, recv_sems, local_sem):
    m_per = x_ref.shape[0]
    x, y, c, _, chips, _ = _mesh_place()
    me, sibling = (x, y, c), (x, y, 1 - c)

    def rows(px, py, pc):
        return out_ref.at[pl.ds((4 * px + 2 * py + pc) * m_per, m_per), :]

    def copy(k, blk, to, src=None):
        return _remote(rows(*blk) if src is None else src, rows(*blk), send_sems.at[k], recv_sems.at[k], to)

    mine = pltpu.make_async_copy(x_ref, rows(*me), local_sem)
    mine.start()
    first = [copy(0, me, sibling, src=x_ref)]
    first += [copy(1 + j, me, (*chip, c), src=x_ref) for j, chip in enumerate(chips)]
    for cp in first:
        cp.start()
    passed = [copy(4 + j, (*chip, c), sibling) for j, chip in enumerate(chips)]
    for j, chip in enumerate(chips):
        copy(1 + j, (*chip, c), me).wait_recv()
        passed[j].start()
    copy(0, sibling, me).wait_recv()
    for j, chip in enumerate(chips):
        copy(4 + j, (*chip, 1 - c), me).wait_recv()
    for cp in first + passed:
        cp.wait_send()
    mine.wait()


def _all_gather_small(block, name):
    m_per, n = block.shape
    return pl.pallas_call(
        _small_gather_body(), name=name,
        out_shape=jax.ShapeDtypeStruct((NDEV * m_per, n), block.dtype),
        in_specs=[pl.BlockSpec(memory_space=pltpu.VMEM)],
        out_specs=pl.BlockSpec(memory_space=pltpu.VMEM),
        scratch_shapes=SMALL_GATHER_SCRATCH,
        compiler_params=pltpu.CompilerParams(vmem_limit_bytes=VMEM_LIMIT),
    )(block)


def _small_gather_body():
    def body(x_ref, out_ref, send_sems, recv_sems, local_sem):
        _small_gather(x_ref, out_ref, send_sems, recv_sems, local_sem)
    return body


class _WeightGather:
    def __init__(self, specs):
        self.specs = specs
        self.n = len(specs)
        self.specs_any = [pl.BlockSpec(memory_space=pl.ANY)] * self.n
        self.out_shape = [jax.ShapeDtypeStruct(sp.full_shape, BF16) for sp in specs]
        self.scratch = [pltpu.SemaphoreType.DMA((6 * self.n,)), pltpu.SemaphoreType.DMA((6 * self.n,))]

    def _sends(self, outs, send_sems, recv_sems):
        x, y, c, k_me, chips, _ = _mesh_place()
        sends = []
        for j, chip in enumerate(chips):
            for w, sp in enumerate(self.specs):
                mine = sp.piece(outs[w], k_me, c)
                sends.append(_remote(mine, mine, send_sems.at[6 * w + j], recv_sems.at[6 * w + j], (*chip, c)))
        return sends

    def start(self, outs, send_sems, recv_sems):
        for cp in self._sends(outs, send_sems, recv_sems):
            cp.start()

    def _passes(self, outs, send_sems, recv_sems):
        x, y, c, _, chips, kidx = _mesh_place()
        return [_remote(sp.piece(outs[w], kidx[j], c), sp.piece(outs[w], kidx[j], c),
                        send_sems.at[6 * w + 3 + j], recv_sems.at[6 * w + 3 + j], (x, y, 1 - c))
                for j in range(3) for w, sp in enumerate(self.specs)]

    def forward(self, outs, send_sems, recv_sems):
        x, y, c, _, chips, kidx = _mesh_place()
        for j, chip in enumerate(chips):
            for w, sp in enumerate(self.specs):
                landed = sp.piece(outs[w], kidx[j], c)
                _remote(landed, landed, send_sems.at[6 * w + j], recv_sems.at[6 * w + j], (*chip, c)).wait_recv()
        for cp in self._passes(outs, send_sems, recv_sems):
            cp.start()

    def drain(self, outs, send_sems, recv_sems):
        x, y, c, _, chips, kidx = _mesh_place()
        for j in range(3):
            for w, sp in enumerate(self.specs):
                landed = sp.piece(outs[w], kidx[j], 1 - c)
                _remote(landed, landed, send_sems.at[6 * w + 3 + j], recv_sems.at[6 * w + 3 + j],
                        (x, y, 1 - c)).wait_recv()
        for cp in self._sends(outs, send_sems, recv_sems) + self._passes(outs, send_sems, recv_sems):
            cp.wait_send()

    def finish(self, outs, send_sems, recv_sems):
        self.forward(outs, send_sems, recv_sems)
        self.drain(outs, send_sems, recv_sems)


class _GradExchange:
    def __init__(self, specs):
        self.specs = specs
        self.n = len(specs)
        self.specs_any = [pl.BlockSpec(memory_space=pl.ANY)] * self.n
        self.out_shape = [jax.ShapeDtypeStruct((NDEV,) + sp.piece_shape, BF16) for sp in specs]
        self.scratch = [pltpu.SemaphoreType.DMA((7 * self.n,)), pltpu.SemaphoreType.DMA((NDEV * self.n,))]

    def _sends(self, grads, recvs, send_sems, recv_sems):
        x, y, c, k_me, chips, kidx = _mesh_place()
        dev = 2 * k_me + c
        sends = []
        for w, sp in enumerate(self.specs):
            slot, arrival = recvs[w].at[dev], recv_sems.at[NDEV * w + dev]
            sends.append(_remote(sp.piece(grads[w], k_me, 1 - c), slot, send_sems.at[7 * w], arrival, (x, y, 1 - c)))
            for j, chip in enumerate(chips):
                for h in range(2):
                    sends.append(_remote(sp.piece(grads[w], kidx[j], h), slot, send_sems.at[7 * w + 1 + 2 * j + h],
                                         arrival, (*chip, h)))
        return sends

    def start(self, grads, recvs, send_sems, recv_sems):
        for cp in self._sends(grads, recvs, send_sems, recv_sems):
            cp.start()

    def finish(self, grads, recvs, send_sems, recv_sems):
        x, y, c, k_me, _, _ = _mesh_place()
        dev = 2 * k_me + c
        for w in range(self.n):
            for d in range(NDEV):
                landed = recvs[w].at[d]
                arrival = _remote(landed, landed, send_sems.at[7 * w], recv_sems.at[NDEV * w + d], (x, y, c))
                pl.when(d != dev)(arrival.wait_recv)
        for cp in self._sends(grads, recvs, send_sems, recv_sems):
            cp.wait_send()


def _device_sums(locals_, recvs, place):
    def body(p_ref, *refs):
        a_refs, b_refs, o_refs = refs[:NW], refs[NW:2 * NW], refs[2 * NW:]
        d = pl.program_id(0)
        own = d == p_ref[2]
        terms = [jnp.where(own, a_ref[...], b_ref[...]).astype(F32) for a_ref, b_ref in zip(a_refs, b_refs)]

        @pl.when(d == 0)
        def _():
            for o_ref, term in zip(o_refs, terms):
                o_ref[...] = term

        @pl.when(d > 0)
        def _():
            for o_ref, term in zip(o_refs, terms):
                o_ref[...] += term

    def mine(sp):
        nd = len(sp.piece_shape)
        return pl.BlockSpec(sp.piece_shape, lambda d, p_ref: tuple(
            2 * p_ref[0] + p_ref[1] if a == sp.shard_axis == sp.half_axis else
            p_ref[0] if a == sp.shard_axis else p_ref[1] if a == sp.half_axis else 0 for a in range(nd)))

    def others(sp):
        nd = len(sp.piece_shape)
        return pl.BlockSpec((None,) + sp.piece_shape,
                            lambda d, p_ref: (jnp.where(d == p_ref[2], (d + 1) % NDEV, d),) + (0,) * nd)

    def half(sp):
        nd = len(sp.piece_shape)
        return pl.BlockSpec(sp.piece_shape,
                            lambda d, p_ref: tuple(p_ref[1] if a == sp.half_axis else 0 for a in range(nd)))

    return pl.pallas_call(
        body, name="rs_device_sums",
        grid_spec=pltpu.PrefetchScalarGridSpec(
            num_scalar_prefetch=1, grid=(NDEV,),
            in_specs=[mine(sp) for sp in SHARDED] + [others(sp) for sp in SHARDED],
            out_specs=[half(sp) for sp in SHARDED]),
        out_shape=[jax.ShapeDtypeStruct(sp.shard_shape, F32) for sp in SHARDED],
        compiler_params=_params(("arbitrary",)),
    )(place, *locals_, *recvs)


def _pair_share(halves, vector_block):
    m_per, n = vector_block.shape

    def body(*refs):
        x_ref = refs[NW]
        outs, gathered = refs[NW + 1:2 * NW + 1], refs[2 * NW + 1]
        send_sems, recv_sems = refs[2 * NW + 2:2 * NW + 4]
        x, y, c, _, _, _ = _mesh_place()
        sibling = (x, y, 1 - c)
        sent = []
        for w, sp in enumerate(SHARDED):
            mine = sp.half(outs[w], c)
            cp = _remote(mine, mine, send_sems.at[w], recv_sems.at[w], sibling)
            cp.start()
            sent.append(cp)
        _small_gather(x_ref, gathered, *refs[2 * NW + 4:])
        for w, sp in enumerate(SHARDED):
            landed = sp.half(outs[w], 1 - c)
            _remote(landed, landed, send_sems.at[w], recv_sems.at[w], sibling).wait_recv()
        for cp in sent:
            cp.wait_send()

    hbm = pl.BlockSpec(memory_space=pl.ANY)
    vmem = pl.BlockSpec(memory_space=pltpu.VMEM)
    out = pl.pallas_call(
        body, name="rs_pair_share",
        out_shape=[jax.ShapeDtypeStruct(sp.shard_shape, F32) for sp in SHARDED]
        + [jax.ShapeDtypeStruct((NDEV * m_per, n), F32)],
        in_specs=[hbm] * NW + [vmem], out_specs=[hbm] * NW + [vmem],
        input_output_aliases={w: w for w in range(NW)},
        scratch_shapes=[pltpu.SemaphoreType.DMA((NW,)), pltpu.SemaphoreType.DMA((NW,))] + SMALL_GATHER_SCRATCH,
        compiler_params=pltpu.CompilerParams(vmem_limit_bytes=VMEM_LIMIT),
    )(*halves, vector_block)
    return out[:NW], out[NW]


def _reduce_scatter(local, received, place, vector_block):
    halves = _device_sums([local[sp.name] for sp in SHARDED], [received[sp.name] for sp in SHARDED], place)
    return _pair_share(halves, vector_block)


def _place_bf16(sp, w, place):
    nd = len(sp.full_shape)

    def body(p_ref, w_ref, o_ref):
        o_ref[...] = w_ref[...].astype(BF16)

    return pl.pallas_call(
        body, name="place_" + sp.name,
        grid_spec=pltpu.PrefetchScalarGridSpec(
            num_scalar_prefetch=1, grid=(1,),
            in_specs=[pl.BlockSpec(sp.shard_shape, lambda i, p_ref: (0,) * nd)],
            out_specs=pl.BlockSpec(sp.shard_shape,
                                   lambda i, p_ref: tuple(p_ref[0] if a == sp.shard_axis else 0 for a in range(nd)))),
        out_shape=jax.ShapeDtypeStruct(sp.full_shape, BF16),
        compiler_params=_params(("arbitrary",)),
    )(place, w)


def _matmul_f32(a, b, name):
    def body(a_ref, b_ref, o_ref):
        o_ref[...] = jnp.dot(a_ref[...], b_ref[...], preferred_element_type=F32, precision=lax.Precision.HIGHEST)

    return pl.pallas_call(body, name=name, out_shape=jax.ShapeDtypeStruct((a.shape[0], b.shape[1]), F32),
                          compiler_params=pltpu.CompilerParams(vmem_limit_bytes=VMEM_LIMIT))(a, b)


def _sum_devices(stacked):
    def body(x_ref, o_ref):
        acc = x_ref[0]
        for d in range(1, NDEV):
            acc = acc + x_ref[d]
        o_ref[...] = acc

    return pl.pallas_call(body, name="sum_devices", out_shape=jax.ShapeDtypeStruct(stacked.shape[1:], F32),
                          compiler_params=pltpu.CompilerParams(vmem_limit_bytes=VMEM_LIMIT))(stacked)


ADAMW_STEPS = 8


def _adamw(ws, gs, ms, vs, name):
    n = len(ws)
    steps = ADAMW_STEPS if all(w.shape[0] % (8 * ADAMW_STEPS) == 0 for w in ws) else 1

    def body(*refs):
        ins, outs = refs[:4 * n], refs[4 * n:]
        for k in range(n):
            w_ref, g_ref, m_ref, v_ref = ins[k], ins[n + k], ins[2 * n + k], ins[3 * n + k]
            gv = g_ref[...]
            nm = ADAM_B1 * m_ref[...] + (1.0 - ADAM_B1) * gv
            nv = ADAM_B2 * v_ref[...] + (1.0 - ADAM_B2) * (gv * gv)
            m_hat = nm / (1.0 - ADAM_B1 ** ADAM_STEP)
            v_hat = nv / (1.0 - ADAM_B2 ** ADAM_STEP)
            outs[k][...] = -ADAM_LR * (m_hat / (jnp.sqrt(v_hat) + ADAM_EPS) + ADAM_WD * w_ref[...])
            outs[n + k][...] = nm
            outs[2 * n + k][...] = nv

    blks = [pl.BlockSpec((w.shape[0] // steps, w.shape[1]), lambda i: (i, 0)) for w in ws]
    shapes = [jax.ShapeDtypeStruct(w.shape, F32) for w in ws]
    out = pl.pallas_call(
        body, name="adamw_" + name, grid=(steps,), in_specs=blks * 4, out_specs=blks * 3, out_shape=shapes * 3,
        compiler_params=_params(("parallel",)),
    )(*ws, *gs, *ms, *vs)
    return out[:n], out[n:2 * n], out[2 * n:]


WEIGHT_NAMES = ("g_pre_mix", "g_post_mix", "g_pre_ffn", "g_post_ffn", "w_ada", "b_ada", "w_in", "w_pool",
                "pool_scale", "conv_w", "conv_b", "w_bout", "w_o", "w_up", "ffn_conv_w", "ffn_conv_b", "w_down")
MATRIX_NAMES = ("w_ada",) + tuple(sp.name for sp in SHARDED)
VECTOR_NAMES = tuple(n for n in WEIGHT_NAMES if n not in MATRIX_NAMES)

CW = D // NCHIP
FCW = F2 // NCHIP
ADA_W = DIN // NCHIP
COND_BLOCK = (8, 768)
GRAD_BLOCK = (8, 4864)


def _flat_pad(parts, shape):
    flat = jnp.concatenate([p.reshape(-1) for p in parts])
    return jnp.pad(flat, (0, shape[0] * shape[1] - flat.shape[0])).reshape(shape)


def _take(flat, offset, shape):
    size = 1
    for n in shape:
        size *= n
    return flat[offset:offset + size].reshape(shape), offset + size


def kernel(x, c, g_pre_mix, g_post_mix, g_pre_ffn, g_post_ffn, w_ada, b_ada, w_in, w_pool, pool_scale, conv_w, conv_b, w_bout, w_o, w_up, ffn_conv_w, ffn_conv_b, w_down, loss_target, m_g_pre_mix, m_g_post_mix, m_g_pre_ffn, m_g_post_ffn, m_w_ada, m_b_ada, m_w_in, m_w_pool, m_pool_scale, m_conv_w, m_conv_b, m_w_bout, m_w_o, m_w_up, m_ffn_conv_w, m_ffn_conv_b, m_w_down, v_g_pre_mix, v_g_post_mix, v_g_pre_ffn, v_g_post_ffn, v_w_ada, v_b_ada, v_w_in, v_w_pool, v_pool_scale, v_conv_w, v_conv_b, v_w_bout, v_w_o, v_w_up, v_ffn_conv_w, v_ffn_conv_b, v_w_down):
    weights = dict(g_pre_mix=g_pre_mix, g_post_mix=g_post_mix, g_pre_ffn=g_pre_ffn, g_post_ffn=g_post_ffn,
                   w_ada=w_ada, b_ada=b_ada, w_in=w_in, w_pool=w_pool, pool_scale=pool_scale, conv_w=conv_w,
                   conv_b=conv_b, w_bout=w_bout, w_o=w_o, w_up=w_up, ffn_conv_w=ffn_conv_w, ffn_conv_b=ffn_conv_b,
                   w_down=w_down)
    mom1 = dict(g_pre_mix=m_g_pre_mix, g_post_mix=m_g_post_mix, g_pre_ffn=m_g_pre_ffn, g_post_ffn=m_g_post_ffn,
                w_ada=m_w_ada, b_ada=m_b_ada, w_in=m_w_in, w_pool=m_w_pool, pool_scale=m_pool_scale,
                conv_w=m_conv_w, conv_b=m_conv_b, w_bout=m_w_bout, w_o=m_w_o, w_up=m_w_up,
                ffn_conv_w=m_ffn_conv_w, ffn_conv_b=m_ffn_conv_b, w_down=m_w_down)
    mom2 = dict(g_pre_mix=v_g_pre_mix, g_post_mix=v_g_post_mix, g_pre_ffn=v_g_pre_ffn, g_post_ffn=v_g_post_ffn,
                w_ada=v_w_ada, b_ada=v_b_ada, w_in=v_w_in, w_pool=v_w_pool, pool_scale=v_pool_scale,
                conv_w=v_conv_w, conv_b=v_conv_b, w_bout=v_w_bout, w_o=v_w_o, w_up=v_w_up,
                ffn_conv_w=v_ffn_conv_w, ffn_conv_b=v_ffn_conv_b, w_down=v_w_down)

    chip = 2 * lax.axis_index("x") + lax.axis_index("y")
    core = lax.axis_index("c")
    dev = 2 * chip + core
    place = jnp.stack([chip, core, dev]).astype(jnp.int32)

    cond = _all_gather_small(_flat_pad([c, conv_w, ffn_conv_w], COND_BLOCK), "gather_cond")
    cond = cond.reshape(NDEV, -1)
    c_all = cond[:, :D]
    by_chip = cond[0::2]
    conv_w_full = by_chip[:, D:D + 3 * CW].reshape(NCHIP, 3, CW).transpose(1, 0, 2).reshape(3, D)
    ffn_w_full = by_chip[:, D + 3 * CW:D + 3 * CW + 3 * FCW].reshape(NCHIP, 3, FCW).transpose(1, 0, 2).reshape(3, F2)

    mod_cols = _all_gather_small(_matmul_f32(c_all, w_ada[0], "ada_mod"), "gather_mod")
    mod_cols = mod_cols.reshape(NDEV, NDEV, ADA_W)[0::2]
    mod = lax.dynamic_index_in_dim(mod_cols, dev, axis=1, keepdims=False).reshape(6, D) + b_ada.reshape(6, D)
    vec_d = jnp.concatenate([mod, g_pre_mix, g_post_mix, g_pre_ffn, g_post_ffn, pool_scale, conv_b, conv_w_full,
                             jnp.zeros((VD_ROWS - 15, D), F32)], axis=0)
    vec_f = jnp.concatenate([ffn_w_full, ffn_conv_b, jnp.zeros((FV_ROWS - 4, F2), F32)], axis=0)

    placed = [_place_bf16(sp, weights[sp.name][0], place) for sp in SHARDED]
    loss_blk, dx, vecs, local, received = _local_step(x[0], loss_target[0], vec_d, vec_f, placed, place)

    dmod = [vecs[n] for n in ("dsh1", "dsc1", "dgt1", "dsh2", "dsc2", "dgt2")]
    small = [vecs["dg_pre_mix"], vecs["dg_post_mix"], vecs["dg_pre_ffn"], vecs["dg_post_ffn"]] + dmod + [
        vecs["dpool_scale"], vecs["dconv_w"], vecs["dconv_b"], vecs["dffn_conv_w"], vecs["dffn_conv_b"],
        loss_blk[0]]
    reduced, gathered = _reduce_scatter(local, received, place, _flat_pad(small, GRAD_BLOCK))
    total = _sum_devices(gathered.reshape((NDEV,) + GRAD_BLOCK)).reshape(-1)
    vgrad = {}
    off = 0
    for n in ("g_pre_mix", "g_post_mix", "g_pre_ffn", "g_post_ffn"):
        vgrad[n], off = _take(total, off, (1, D))
    dmod_off = off
    vgrad["b_ada"], off = _take(total, off, (1, DIN))
    vgrad["pool_scale"], off = _take(total, off, (1, D))
    g_conv_w, off = _take(total, off, (3, D))
    vgrad["conv_w"] = lax.dynamic_slice_in_dim(g_conv_w, chip * CW, CW, axis=1)[None]
    vgrad["conv_b"], off = _take(total, off, (1, D))
    g_ffn_w, off = _take(total, off, (3, F2))
    vgrad["ffn_conv_w"] = lax.dynamic_slice_in_dim(g_ffn_w, chip * FCW, FCW, axis=1)[None]
    vgrad["ffn_conv_b"], off = _take(total, off, (1, F2))
    loss = total[off]

    dmod_all = gathered.reshape(NDEV, -1)[:, dmod_off:dmod_off + DIN]
    dmod_cols = lax.dynamic_slice_in_dim(dmod_all, chip * ADA_W, ADA_W, axis=1)
    g_ada = _matmul_f32(jnp.pad(c_all.T, ((0, 0), (0, 128 - NDEV))), jnp.pad(dmod_cols, ((0, 128 - NDEV), (0, 0))),
                        "ada_wgrad")

    mgrad = {"w_ada": g_ada}
    for sp, g in zip(SHARDED, reduced):
        mgrad[sp.name] = g

    grad, delta, new_m, new_v = {}, {}, {}, {}
    two_d = lambda tree: [tree[n].reshape(-1, weights[n].shape[-1]) for n in MATRIX_NAMES]
    ds, nms, nvs = _adamw(two_d(weights), two_d(mgrad), two_d(mom1), two_d(mom2), "matrices")
    for n, d, nm, nv in zip(MATRIX_NAMES, ds, nms, nvs):
        shape = weights[n].shape
        grad[n], delta[n], new_m[n], new_v[n] = (a.reshape(shape) for a in (mgrad[n], d, nm, nv))
    flat = lambda tree: [jnp.concatenate([tree[n].reshape(1, -1) for n in VECTOR_NAMES], axis=1)]
    (d,), (nm,), (nv,) = _adamw(flat(weights), flat(vgrad), flat(mom1), flat(mom2), "vectors")
    off = 0
    for n in VECTOR_NAMES:
        shape = weights[n].shape
        grad[n] = vgrad[n].reshape(shape)
        delta[n], _ = _take(d[0], off, shape)
        new_m[n], _ = _take(nm[0], off, shape)
        new_v[n], off = _take(nv[0], off, shape)

    return (loss, dx[None], *[grad[n] for n in WEIGHT_NAMES], *[delta[n] for n in WEIGHT_NAMES],
            *[new_m[n] for n in WEIGHT_NAMES], *[new_v[n] for n in WEIGHT_NAMES])
```

```python
import jax
import jax.numpy as jnp
from jax import lax
from jax.experimental import pallas as pl
from jax.experimental.pallas import tpu as pltpu

F32 = jnp.float32
BF16 = jnp.bfloat16

D = 1024
DIN = 6 * D
F = 2816
F2 = 2 * F
NG = 4
GW = D // NG
POOL_CARRY = 16
CONV_CARRY = 3
EPS = 1e-6
NCHIP = 4
NDEV = 8

ADAM_LR = 0.001
ADAM_B1 = 0.9
ADAM_B2 = 0.999
ADAM_EPS = 1e-08
ADAM_WD = 0.01
ADAM_STEP = 10

VMEM_LIMIT = 60 * 1024 * 1024

(V_SH1, V_SC1, V_GT1, V_SH2, V_SC2, V_GT2, V_GPRE1, V_GPOST1, V_GPRE2, V_GPOST2,
 V_PSCALE, V_CB, V_CW0, V_CW1, V_CW2) = range(15)
VD_ROWS = 16
FV_W0, FV_W1, FV_W2, FV_B = range(4)
FV_ROWS = 8

MESH = pl.DeviceIdType.MESH


def _params(sem=None, vmem=VMEM_LIMIT):
    return pltpu.CompilerParams(dimension_semantics=sem, vmem_limit_bytes=vmem)


def _row(ref, r):
    return ref[r:r + 1, :]


def _load_once(pairs, sem):
    @pl.when(pl.program_id(0) == 0)
    def _():
        copies = [pltpu.make_async_copy(src, dst, sem.at[n]) for n, (src, dst) in enumerate(pairs)]
        for cp in copies:
            cp.start()
        for cp in copies:
            cp.wait()


def _dot(a, b):
    return jnp.dot(a, b, preferred_element_type=F32)


def _dot_nt(a, b):
    return lax.dot_general(a, b, (((1,), (1,)), ((), ())), preferred_element_type=F32)


BLK = 256
SEG = BLK // 8


def _load_rows(ref, ts):
    blocks = [jnp.swapaxes(ref[b * BLK:(b + 1) * BLK, :].reshape(8, SEG, ref.shape[-1]), 0, 1).reshape(BLK, -1)
              for b in range(ts // BLK)]
    return jnp.concatenate(blocks, axis=0)


def _store_rows(ref, val, ts):
    for b in range(ts // BLK):
        blk = val[b * BLK:(b + 1) * BLK, :].reshape(SEG, 8, val.shape[-1])
        ref[b * BLK:(b + 1) * BLK, :] = jnp.swapaxes(blk, 0, 1).reshape(BLK, -1)


def _times(t0):
    p = lax.broadcasted_iota(jnp.int32, (BLK, 1), 0)
    return t0 + (p & 7) * SEG + (p >> 3)


def _before(x, carry, s):
    x3 = x.reshape(SEG, 8, x.shape[-1])
    tail = pltpu.roll(x3[SEG - s:], 1, 1)
    row = lax.broadcasted_iota(jnp.int32, tail.shape, 1)
    out = jnp.concatenate([jnp.where(row == 0, carry, tail), x3[:SEG - s]], axis=0)
    return out.reshape(x.shape), tail


def _after(x, carry, s):
    x3 = x.reshape(SEG, 8, x.shape[-1])
    head = pltpu.roll(x3[:s], 7, 1)
    row = lax.broadcasted_iota(jnp.int32, head.shape, 1)
    out = jnp.concatenate([x3[s:], jnp.where(row == 7, carry, head)], axis=0)
    return out.reshape(x.shape), head


def _causal_conv(x, carry, cols, w0, w1, w2, b):
    x1, carry[0:1, :, cols] = _before(x, carry[0:1, :, cols], 1)
    x2, carry[1:3, :, cols] = _before(x, carry[1:3, :, cols], 2)
    return b + w2 * x + w1 * x1 + w0 * x2


def _causal_conv_bwd(dy, carry, cols, w0, w1, w2):
    d1, carry[0:1, :, cols] = _after(dy, carry[0:1, :, cols], 1)
    d2, carry[1:3, :, cols] = _after(dy, carry[1:3, :, cols], 2)
    return w2 * dy + w1 * d1 + w0 * d2, d1, d2


def _pool_counts(t0, g):
    return jnp.minimum((_times(t0) + 1).astype(F32), float(2 << g))


def _rms(x):
    return lax.rsqrt(jnp.mean(x * x, axis=-1, keepdims=True) + EPS)


def _rms_bwd(dn, n, r):
    return r * (dn - n * jnp.mean(dn * n, axis=-1, keepdims=True))


def _colsum(x):
    return jnp.sum(x, axis=0, keepdims=True)


def _gelu_and_grad(x):
    k, a = 0.7978845608028654, 0.044715
    x2 = x * x
    th1 = 1.0 + jnp.tanh(x * (x2 * (k * a) + k))
    hx = 0.5 * x
    gelu = hx * th1
    dgelu = 0.5 * th1 + (hx * (th1 * (2.0 - th1))) * (x2 * (3.0 * k * a) + k)
    return gelu, dgelu


def _fwd_proj(x, vec_d, placed_in, placed_rest, place, ts):
    s = x.shape[0]
    nt = s // ts
    cw = DIN // NCHIP
    sp_in = SHARDED[0]
    gather = _WeightGather(SHARDED[1:4])
    n = gather.n

    def body(*refs):
        p_ref, x_ref, v_ref = refs[:3]
        proj_ref, h1_ref, w_full = refs[4 + n:7 + n]
        rest = refs[7 + n:7 + 2 * n]
        w_vmem, h1_all, sem, in_send, in_recv, send_sems, recv_sems = refs[7 + 2 * n:]
        j, i = pl.program_id(0), pl.program_id(1)
        x_, y_, c, k_me, _, _ = _mesh_place()
        sibling = (x_, y_, 1 - c)

        def peer(t):
            return (x_ ^ (t >> 1), y_ ^ (t & 1))

        def w_in_sends():
            mine = sp_in.piece(w_full, k_me, c)
            return [_remote(mine, mine, in_send.at[t - 1], in_recv.at[t - 1], (*peer(t), c)) for t in (1, 2, 3)]

        def load_block(k):
            cp = pltpu.make_async_copy(sp_in.shard(w_full, k), w_vmem.at[k], sem.at[0])
            cp.start()
            cp.wait()

        @pl.when((j == 0) & (i == 0))
        def _():
            for cp in w_in_sends()[:2]:
                cp.start()
            load_block(k_me)

        @pl.when((j == 1) & (i == 0))
        def _():
            for cp in w_in_sends()[:2]:
                cp.wait_send()
            w_in_sends()[2].start()
            gather.start(rest, send_sems, recv_sems)

        for t in (1, 2, 3):
            @pl.when((j == t) & (i == 0))
            def _(t=t):
                k = k_me ^ t
                landed = sp_in.piece(w_full, k, c)
                _remote(landed, landed, in_send.at[t - 1], in_recv.at[t - 1], (*peer(t), c)).wait_recv()
                _remote(landed, landed, in_send.at[2 + t], in_recv.at[2 + t], sibling).start()
                other = sp_in.piece(w_full, k, 1 - c)
                _remote(other, other, in_send.at[2 + t], in_recv.at[2 + t], sibling).wait_recv()
                load_block(k)

        @pl.when(j == 0)
        def _():
            xv = _load_rows(x_ref, ts)
            n1 = xv * _rms(xv)
            h = n1 * (_row(v_ref, V_GPRE1) * (1.0 + _row(v_ref, V_SC1))) + _row(v_ref, V_SH1)
            hb = h.astype(BF16)
            h1_ref[...] = hb
            h1_all[i] = hb

        proj_ref[...] = _dot(h1_all[i], w_vmem[k_me ^ j]).astype(BF16)

        @pl.when((j == NCHIP - 1) & (i == nt - 1))
        def _():
            w_in_sends()[2].wait_send()
            for t in (1, 2, 3):
                landed = sp_in.piece(w_full, k_me ^ t, c)
                _remote(landed, landed, in_send.at[2 + t], in_recv.at[2 + t], sibling).wait_send()
            gather.finish(rest, send_sems, recv_sems)

    once = lambda w: pl.BlockSpec((ts, w), lambda j, i, p: (jnp.where(j == 0, i, nt - 1), 0))
    return pl.pallas_call(
        body, name="fwd_proj",
        grid_spec=pltpu.PrefetchScalarGridSpec(
            num_scalar_prefetch=1, grid=(NCHIP, nt),
            in_specs=[once(D), pl.BlockSpec((VD_ROWS, D), lambda j, i, p: (0, 0)),
                      pl.BlockSpec(memory_space=pl.ANY)] + gather.specs_any,
            out_specs=[pl.BlockSpec((ts, cw), lambda j, i, p: (i, p[0] ^ j)), once(D),
                       pl.BlockSpec(memory_space=pl.ANY)] + gather.specs_any,
            scratch_shapes=[pltpu.VMEM((NCHIP, D, cw), BF16), pltpu.VMEM((nt, ts, D), BF16),
                            pltpu.SemaphoreType.DMA((1,)),
                            pltpu.SemaphoreType.DMA((6,)), pltpu.SemaphoreType.DMA((6,))] + gather.scratch),
        out_shape=[jax.ShapeDtypeStruct((s, DIN), BF16), jax.ShapeDtypeStruct((s, D), BF16),
                   jax.ShapeDtypeStruct(sp_in.full_shape, BF16)] + gather.out_shape,
        input_output_aliases={3 + w: 2 + w for w in range(n + 1)},
        compiler_params=_params(("arbitrary", "arbitrary")),
    )(place, x, vec_d, placed_in, *placed_rest)


def _fwd_mix(proj, x, vec_d, w_pool, w_bout, w_o, placed_ffn, ts):
    s = x.shape[0]
    gather = _WeightGather(SHARDED[4:])
    n = gather.n

    def body(*refs):
        ins, outs, rest = refs[:6], refs[6 + n:14 + n], refs[14 + n:14 + 2 * n]
        scratch, sems = refs[14 + 2 * n:-2], refs[-2:]
        i = pl.program_id(0)
        nt = s // ts
        pl.when(i == 0)(lambda: gather.start(rest, *sems))
        pl.when(i == nt - 1 - nt // 8)(lambda: gather.forward(rest, *sems))
        compute(*ins, *outs, *scratch)
        pl.when(i == nt - 1)(lambda: gather.drain(rest, *sems))

    def compute(p_ref, x_ref, v_ref, wp_hbm, wb_hbm, wo_hbm,
                x1_ref, o_ref, pg_ref, q_ref, mg_ref, ya_ref, yb_ref, cv_ref,
                wp, wb, wo, carry_p, carry_v, sem):
        i = pl.program_id(0)
        _load_once([(wp_hbm, wp), (wb_hbm, wb), (wo_hbm, wo)], sem)

        @pl.when(i == 0)
        def _():
            carry_p[...] = jnp.zeros_like(carry_p)
            carry_v[...] = jnp.zeros_like(carry_v)

        t0 = i * ts
        for g in range(NG):
            cols = slice(g * GW, (g + 1) * GW)
            u = p_ref[:, cols].astype(F32)
            e = u
            for l in range(g + 1):
                slot = slice((1 << l) - 1, (2 << l) - 1)
                shifted, carry_p[slot, :, cols] = _before(e, carry_p[slot, :, cols], 1 << l)
                e = e + shifted
            pgb = (e / _pool_counts(t0, g) - u).astype(BF16)
            pg_ref[:, cols] = pgb
            ya_ref[:, cols] = _dot(pgb, wp[g]).astype(BF16)

        u_x = p_ref[:, D:2 * D].astype(F32)
        u_c = p_ref[:, 3 * D:4 * D].astype(F32)
        v = u_c * u_x
        cv = _causal_conv(v, carry_v, slice(None), _row(v_ref, V_CW0), _row(v_ref, V_CW1),
                          _row(v_ref, V_CW2), _row(v_ref, V_CB))
        cv_ref[...] = cv.astype(BF16)
        q = (p_ref[:, 2 * D:3 * D].astype(F32) * cv).astype(BF16)
        q_ref[...] = q
        y_b = _dot(q, wb[...])
        yb_ref[...] = y_b.astype(BF16)

        y_a = ya_ref[...].astype(F32) * _row(v_ref, V_PSCALE)
        merged = (jax.nn.sigmoid(p_ref[:, 4 * D:5 * D].astype(F32)) * y_a
                  + jax.nn.sigmoid(p_ref[:, 5 * D:6 * D].astype(F32)) * y_b).astype(BF16)
        mg_ref[...] = merged
        o = _dot(merged, wo[...])
        o_ref[...] = o.astype(BF16)
        x1_ref[...] = _load_rows(x_ref, ts) + _row(v_ref, V_GT1) * ((o * _rms(o)) * _row(v_ref, V_GPOST1))

    tile = lambda w: pl.BlockSpec((ts, w), lambda i: (i, 0))
    hbm = pl.BlockSpec(memory_space=pl.ANY)
    return pl.pallas_call(
        body, name="fwd_mix", grid=(s // ts,),
        in_specs=[tile(DIN), tile(D), pl.BlockSpec((VD_ROWS, D), lambda i: (0, 0)), hbm, hbm, hbm] + gather.specs_any,
        out_specs=[tile(D)] * 8 + gather.specs_any,
        out_shape=[jax.ShapeDtypeStruct((s, D), F32)] + [jax.ShapeDtypeStruct((s, D), BF16)] * 7 + gather.out_shape,
        input_output_aliases={6 + w: 8 + w for w in range(n)},
        scratch_shapes=[pltpu.VMEM((NG, GW, GW), BF16), pltpu.VMEM((D, D), BF16), pltpu.VMEM((D, D), BF16),
                        pltpu.VMEM((POOL_CARRY, 8, D), F32), pltpu.VMEM((CONV_CARRY, 8, D), F32),
                        pltpu.SemaphoreType.DMA((3,))] + gather.scratch,
        compiler_params=_params(("arbitrary",)),
    )(proj, x, vec_d, w_pool, w_bout, w_o, *placed_ffn)


def _fwd_ffn(x1, tgt, vec_d, vec_f, w_up, w_down, ts):
    s = x1.shape[0]

    def body(x1_ref, t_ref, v_ref, f_ref, wu_hbm, wd_hbm,
             up_ref, upc_ref, a_ref, h2_ref, dx2_ref, dff_ref, vo_ref, loss_ref,
             wu, wd, carry, sem):
        i = pl.program_id(0)
        _load_once([(wu_hbm, wu), (wd_hbm, wd)], sem)

        @pl.when(i == 0)
        def _():
            carry[...] = jnp.zeros_like(carry)
            vo_ref[...] = jnp.zeros_like(vo_ref)
            loss_ref[...] = jnp.zeros_like(loss_ref)

        x1v = x1_ref[...]
        n3 = x1v * _rms(x1v)
        h2 = (n3 * (_row(v_ref, V_GPRE2) * (1.0 + _row(v_ref, V_SC2))) + _row(v_ref, V_SH2)).astype(BF16)
        h2_ref[...] = h2

        ff = jnp.zeros((ts, D), F32)
        for lo, hi in FFN_SLABS_FWD:
            up = []
            for cols in (slice(lo, hi), slice(F + lo, F + hi)):
                u0 = _dot(h2, wu[:, cols])
                up_ref[:, cols] = u0.astype(BF16)
                y = _causal_conv(u0, carry, cols, f_ref[FV_W0:FV_W0 + 1, cols], f_ref[FV_W1:FV_W1 + 1, cols],
                                 f_ref[FV_W2:FV_W2 + 1, cols], f_ref[FV_B:FV_B + 1, cols])
                upc_ref[:, cols] = y.astype(BF16)
                up.append(y)
            gelu, _ = _gelu_and_grad(up[0])
            a = (gelu * up[1]).astype(BF16)
            a_ref[:, lo:hi] = a
            ff = ff + _dot(a, wd[lo:hi, :])

        r4 = _rms(ff)
        n4 = ff * r4
        gt2 = _row(v_ref, V_GT2)
        gpost = _row(v_ref, V_GPOST2)
        gate_gain = gt2 * gpost
        diff = (x1v + gate_gain * n4) - _load_rows(t_ref, ts)
        loss_ref[...] += jnp.full(loss_ref.shape, 0.5 / D * jnp.sum(diff * diff), F32)
        dx2_ref[...] = diff * (1.0 / D)
        s1 = _colsum(diff * n4)
        vo_ref[0:1, :] += s1 * (gpost * (1.0 / D))
        vo_ref[1:2, :] += s1 * (gt2 * (1.0 / D))
        dff_ref[...] = _rms_bwd(diff * (gate_gain * (1.0 / D)), n4, r4).astype(BF16)

    tile = lambda w: pl.BlockSpec((ts, w), lambda i: (i, 0))
    full = lambda r, w: pl.BlockSpec((r, w), lambda i: (0, 0))
    hbm = pl.BlockSpec(memory_space=pl.ANY)
    return pl.pallas_call(
        body, name="fwd_ffn", grid=(s // ts,),
        in_specs=[tile(D), tile(D), full(VD_ROWS, D), full(FV_ROWS, F2), hbm, hbm],
        out_specs=[tile(F2), tile(F2), tile(F), tile(D), tile(D), tile(D), full(8, D), full(8, 128)],
        out_shape=[jax.ShapeDtypeStruct((s, F2), BF16), jax.ShapeDtypeStruct((s, F2), BF16),
                   jax.ShapeDtypeStruct((s, F), BF16),
                   jax.ShapeDtypeStruct((s, D), BF16), jax.ShapeDtypeStruct((s, D), F32),
                   jax.ShapeDtypeStruct((s, D), BF16), jax.ShapeDtypeStruct((8, D), F32),
                   jax.ShapeDtypeStruct((8, 128), F32)],
        scratch_shapes=[pltpu.VMEM((D, F2), BF16), pltpu.VMEM((F, D), BF16), pltpu.VMEM((CONV_CARRY, 8, F2), F32),
                        pltpu.SemaphoreType.DMA((2,))],
        compiler_params=_params(("arbitrary",)),
    )(x1, tgt, vec_d, vec_f, w_up, w_down)


def _bwd_ffn(dff, dx2, x1, up0, upc, vec_d, vec_f, w_up, w_down, exchange, ex_grads, ts):
    s = x1.shape[0]
    nt = s // ts
    n = exchange.n

    def body(*refs):
        ins, grads = refs[:9], refs[9:9 + n]
        outs, recvs = refs[9 + n:13 + n], refs[13 + n:13 + 2 * n]
        scratch, sems = refs[13 + 2 * n:-2], refs[-2:]
        i = pl.program_id(0)
        pl.when(i == 0)(lambda: exchange.start(grads, recvs, *sems))
        compute(*ins, *outs, *scratch)
        pl.when(i == nt - 1)(lambda: exchange.finish(grads, recvs, *sems))

    def compute(dff_ref, dx2_ref, x1_ref, up_ref, upc_ref, v_ref, f_ref, wu_hbm, wd_hbm,
                dx1_ref, dup_ref, vo_ref, fo_ref, wu, wd, carry, sem):
        i = pl.program_id(0)
        _load_once([(wu_hbm, wu), (wd_hbm, wd)], sem)

        @pl.when(i == 0)
        def _():
            carry[...] = jnp.zeros_like(carry)
            vo_ref[...] = jnp.zeros_like(vo_ref)
            fo_ref[...] = jnp.zeros_like(fo_ref)

        dffb = dff_ref[...]

        dh2 = jnp.zeros((ts, D), F32)
        for lo, hi in FFN_SLABS_BWD:
            slabs = (slice(lo, hi), slice(F + lo, F + hi))
            gelu, dgelu = _gelu_and_grad(upc_ref[:, slabs[0]].astype(F32))
            da = _dot_nt(dffb, wd[lo:hi, :])
            dups = (da * upc_ref[:, slabs[1]].astype(F32) * dgelu, da * gelu)
            for cols, dup in zip(slabs, dups):
                du0, d1, d2 = _causal_conv_bwd(dup, carry, cols, f_ref[FV_W0:FV_W0 + 1, cols],
                                               f_ref[FV_W1:FV_W1 + 1, cols], f_ref[FV_W2:FV_W2 + 1, cols])
                u0 = up_ref[:, cols].astype(F32)
                fo_ref[FV_B:FV_B + 1, cols] += _colsum(dup)
                fo_ref[FV_W2:FV_W2 + 1, cols] += _colsum(dup * u0)
                fo_ref[FV_W1:FV_W1 + 1, cols] += _colsum(d1 * u0)
                fo_ref[FV_W0:FV_W0 + 1, cols] += _colsum(d2 * u0)
                du0 = du0.astype(BF16)
                dup_ref[:, cols] = du0
                dh2 = dh2 + _dot_nt(du0, wu[:, cols])

        x1v = x1_ref[...]
        r3 = _rms(x1v)
        n3 = x1v * r3
        gpre = _row(v_ref, V_GPRE2)
        sc = 1.0 + _row(v_ref, V_SC2)
        vo_ref[0:1, :] += _colsum(dh2)
        s2 = _colsum(dh2 * n3)
        vo_ref[1:2, :] += s2 * gpre
        vo_ref[2:3, :] += s2 * sc
        dx1_ref[...] = dx2_ref[...] + _rms_bwd(dh2 * (gpre * sc), n3, r3)

    rev = lambda w: pl.BlockSpec((ts, w), lambda i: (nt - 1 - i, 0))
    full = lambda r, w: pl.BlockSpec((r, w), lambda i: (0, 0))
    hbm = pl.BlockSpec(memory_space=pl.ANY)
    return pl.pallas_call(
        body, name="bwd_ffn", grid=(nt,),
        in_specs=[rev(D), rev(D), rev(D), rev(F2), rev(F2), full(VD_ROWS, D), full(FV_ROWS, F2), hbm, hbm]
        + exchange.specs_any,
        out_specs=[rev(D), rev(F2), full(8, D), full(FV_ROWS, F2)] + exchange.specs_any,
        out_shape=[jax.ShapeDtypeStruct((s, D), F32), jax.ShapeDtypeStruct((s, F2), BF16),
                   jax.ShapeDtypeStruct((8, D), F32), jax.ShapeDtypeStruct((FV_ROWS, F2), F32)] + exchange.out_shape,
        scratch_shapes=[pltpu.VMEM((D, F2), BF16), pltpu.VMEM((F, D), BF16), pltpu.VMEM((CONV_CARRY, 8, F2), F32),
                        pltpu.SemaphoreType.DMA((2,))] + exchange.scratch,
        compiler_params=_params(("arbitrary",)),
    )(dff, dx2, x1, up0, upc, vec_d, vec_f, w_up, w_down, *ex_grads)


def _bwd_mix(dx1, o, proj, cv, ya0, yb, merged, q, pg, vec_d, w_pool, w_bout, w_o, exchange, ex_grads, ts):
    s = dx1.shape[0]
    nt = s // ts
    n = exchange.n

    def body(*refs):
        ins, grads = refs[:13], refs[13:13 + n]
        outs, recvs = refs[13 + n:18 + n], refs[18 + n:18 + 2 * n]
        scratch, sems = refs[18 + 2 * n:-2], refs[-2:]
        i = pl.program_id(0)
        pl.when(i == 0)(lambda: exchange.start(grads, recvs, *sems))
        compute(*ins, *outs, *scratch)
        pl.when(i == nt - 1)(lambda: exchange.finish(grads, recvs, *sems))

    def compute(dx1_ref, o_ref, p_ref, cv_ref, ya_ref, yb_ref, mg_ref, q_ref, pg_ref, v_ref, wp_hbm, wb_hbm, wo_hbm,
                dp_ref, vo_ref, go_ref, gb_ref, gp_ref, wp, wb, wo, carry_d, carry_e, acc_o, acc_b, acc_p, sem):
        i = pl.program_id(0)
        _load_once([(wp_hbm, wp), (wb_hbm, wb), (wo_hbm, wo)], sem)

        @pl.when(i == 0)
        def _():
            carry_d[...] = jnp.zeros_like(carry_d)
            carry_e[...] = jnp.zeros_like(carry_e)
            vo_ref[...] = jnp.zeros_like(vo_ref)
            acc_o[...] = jnp.zeros_like(acc_o)
            acc_b[...] = jnp.zeros_like(acc_b)
            acc_p[...] = jnp.zeros_like(acc_p)

        t0 = (nt - 1 - i) * ts
        dx1v = dx1_ref[...]
        ov = o_ref[...].astype(F32)
        r2 = _rms(ov)
        n2 = ov * r2
        gpost = _row(v_ref, V_GPOST1)
        gt1 = _row(v_ref, V_GT1)
        s1 = _colsum(dx1v * n2)
        vo_ref[0:1, :] += s1 * gpost
        vo_ref[1:2, :] += s1 * gt1
        dob = _rms_bwd(dx1v * (gt1 * gpost), n2, r2).astype(BF16)
        acc_o[...] += _dot_tn(mg_ref[...], dob)
        dmerged = _dot_nt(dob, wo[...])

        ya0 = ya_ref[...].astype(F32)
        pscale = _row(v_ref, V_PSCALE)
        sa = jax.nn.sigmoid(p_ref[:, 4 * D:5 * D].astype(F32))
        dp_ref[:, 4 * D:5 * D] = (dmerged * (ya0 * pscale) * sa * (1.0 - sa)).astype(BF16)
        dy_a = dmerged * sa
        vo_ref[2:3, :] += _colsum(dy_a * ya0)
        dya0 = (dy_a * pscale).astype(BF16)

        sb = jax.nn.sigmoid(p_ref[:, 5 * D:6 * D].astype(F32))
        dp_ref[:, 5 * D:6 * D] = (dmerged * yb_ref[...].astype(F32) * sb * (1.0 - sb)).astype(BF16)
        dy_b = (dmerged * sb).astype(BF16)
        acc_b[...] += _dot_tn(q_ref[...], dy_b)
        dq = _dot_nt(dy_b, wb[...])

        u_x = p_ref[:, D:2 * D].astype(F32)
        u_b = p_ref[:, 2 * D:3 * D].astype(F32)
        u_c = p_ref[:, 3 * D:4 * D].astype(F32)
        w0, w1, w2 = _row(v_ref, V_CW0), _row(v_ref, V_CW1), _row(v_ref, V_CW2)
        dp_ref[:, 2 * D:3 * D] = (dq * cv_ref[...].astype(F32)).astype(BF16)
        dcv = dq * u_b
        dv, d1, d2 = _causal_conv_bwd(dcv, carry_d, slice(None), w0, w1, w2)
        v = u_c * u_x
        vo_ref[3:4, :] += _colsum(dcv)
        vo_ref[4:5, :] += _colsum(d2 * v)
        vo_ref[5:6, :] += _colsum(d1 * v)
        vo_ref[6:7, :] += _colsum(dcv * v)
        dp_ref[:, D:2 * D] = (dv * u_c).astype(BF16)
        dp_ref[:, 3 * D:4 * D] = (dv * u_x).astype(BF16)

        for g in range(NG):
            cols = slice(g * GW, (g + 1) * GW)
            acc_p[g] += _dot_tn(pg_ref[:, cols], dya0[:, cols])
            dpg = _dot_nt(dya0[:, cols], wp[g])
            e = dpg / _pool_counts(t0, g)
            for l in range(g + 1):
                slot = slice((1 << l) - 1, (2 << l) - 1)
                shifted, carry_e[slot, :, cols] = _after(e, carry_e[slot, :, cols], 1 << l)
                e = e + shifted
            dp_ref[:, cols] = (e - dpg).astype(BF16)

        @pl.when(i == nt - 1)
        def _():
            go_ref[...] = acc_o[...].astype(BF16)
            gb_ref[...] = acc_b[...].astype(BF16)
            gp_ref[...] = acc_p[...].astype(BF16)

    rev = lambda w: pl.BlockSpec((ts, w), lambda i: (nt - 1 - i, 0))
    hbm = pl.BlockSpec(memory_space=pl.ANY)
    whole = lambda shape: pl.BlockSpec(shape, lambda i: (0,) * len(shape))
    return pl.pallas_call(
        body, name="bwd_mix", grid=(nt,),
        in_specs=[rev(D), rev(D), rev(DIN)] + [rev(D)] * 6 + [whole((VD_ROWS, D)), hbm, hbm, hbm] + exchange.specs_any,
        out_specs=[rev(DIN), whole((8, D)), whole((D, D)), whole((D, D)), whole((NG, GW, GW))] + exchange.specs_any,
        out_shape=[jax.ShapeDtypeStruct((s, DIN), BF16), jax.ShapeDtypeStruct((8, D), F32),
                   jax.ShapeDtypeStruct((D, D), BF16), jax.ShapeDtypeStruct((D, D), BF16),
                   jax.ShapeDtypeStruct((NG, GW, GW), BF16)] + exchange.out_shape,
        scratch_shapes=[pltpu.VMEM((NG, GW, GW), BF16), pltpu.VMEM((D, D), BF16), pltpu.VMEM((D, D), BF16),
                        pltpu.VMEM((CONV_CARRY, 8, D), F32), pltpu.VMEM((POOL_CARRY, 8, D), F32),
                        pltpu.VMEM((D, D), F32), pltpu.VMEM((D, D), F32), pltpu.VMEM((NG, GW, GW), F32),
                        pltpu.SemaphoreType.DMA((3,))] + exchange.scratch,
        compiler_params=_params(("arbitrary",)),
    )(dx1, o, proj, cv, ya0, yb, merged, q, pg, vec_d, w_pool, w_bout, w_o, *ex_grads)


def _bwd_in(dproj, dx1, x, vec_d, w_in, exchange, ex_grads, ts):
    s = x.shape[0]
    nt = s // ts
    n = exchange.n

    def body(*refs):
        ins, grads = refs[:5], refs[5:5 + n]
        outs, recvs = refs[5 + n:7 + n], refs[7 + n:7 + 2 * n]
        scratch, sems = refs[7 + 2 * n:-2], refs[-2:]
        i = pl.program_id(0)
        pl.when(i == 0)(lambda: exchange.start(grads, recvs, *sems))
        compute(*ins, *outs, *scratch)
        pl.when(i == nt - 1)(lambda: exchange.finish(grads, recvs, *sems))

    def compute(dp_ref, dx1_ref, x_ref, v_ref, w_hbm, dx_ref, vo_ref, w_vmem, sem):
        _load_once([(w_hbm, w_vmem)], sem)

        @pl.when(pl.program_id(0) == 0)
        def _():
            vo_ref[...] = jnp.zeros_like(vo_ref)

        dh1 = _dot_nt(dp_ref[...], w_vmem[...])
        xv = _load_rows(x_ref, ts)
        r1 = _rms(xv)
        n1 = xv * r1
        gpre = _row(v_ref, V_GPRE1)
        sc = 1.0 + _row(v_ref, V_SC1)
        vo_ref[0:1, :] += _colsum(dh1)
        s1 = _colsum(dh1 * n1)
        vo_ref[1:2, :] += s1 * gpre
        vo_ref[2:3, :] += s1 * sc
        _store_rows(dx_ref, dx1_ref[...] + _rms_bwd(dh1 * (gpre * sc), n1, r1), ts)

    tile = lambda w: pl.BlockSpec((ts, w), lambda i: (i, 0))
    return pl.pallas_call(
        body, name="bwd_in", grid=(s // ts,),
        in_specs=[tile(DIN), tile(D), tile(D), pl.BlockSpec((VD_ROWS, D), lambda i: (0, 0)),
                  pl.BlockSpec(memory_space=pl.ANY)] + exchange.specs_any,
        out_specs=[tile(D), pl.BlockSpec((8, D), lambda i: (0, 0))] + exchange.specs_any,
        out_shape=[jax.ShapeDtypeStruct((s, D), F32), jax.ShapeDtypeStruct((8, D), F32)] + exchange.out_shape,
        scratch_shapes=[pltpu.VMEM((D, DIN), BF16), pltpu.SemaphoreType.DMA((1,))] + exchange.scratch,
        compiler_params=_params(("arbitrary",)),
    )(dproj, dx1, x, vec_d, w_in, *ex_grads)


def _dot_tn(a, b):
    return lax.dot_general(a, b, (((0,), (0,)), ((), ())), preferred_element_type=F32)


def _wgrad(a, b, tm, tn, ts, name, dtype, exchange=None, ex_grads=()):
    s, m = a.shape
    nn = b.shape[1]
    grid = (m // tm, nn // tn, s // ts)
    n = exchange.n if exchange else 0

    def body(*refs):
        a_ref, b_ref = refs[:2]
        grads = refs[2:2 + n]
        o_ref = refs[2 + n]
        recvs = refs[3 + n:3 + 2 * n]
        acc = refs[3 + 2 * n]
        sems = refs[4 + 2 * n:]
        i, j, k = pl.program_id(0), pl.program_id(1), pl.program_id(2)
        if exchange:
            pl.when((i == 0) & (j == 0) & (k == 0))(lambda: exchange.start(grads, recvs, *sems))
        part = _dot_tn(a_ref[...], b_ref[...])

        @pl.when(k == 0)
        def _():
            acc[...] = part

        @pl.when(k > 0)
        def _():
            acc[...] += part

        @pl.when(k == grid[2] - 1)
        def _():
            o_ref[...] = acc[...].astype(dtype)

        if exchange:
            pl.when((i == grid[0] - 1) & (j == grid[1] - 1) & (k == grid[2] - 1))(
                lambda: exchange.finish(grads, recvs, *sems))

    hosted = exchange.specs_any if exchange else []
    return pl.pallas_call(
        body, name=name, grid=grid,
        in_specs=[pl.BlockSpec((ts, tm), lambda i, j, k: (k, i)), pl.BlockSpec((ts, tn), lambda i, j, k: (k, j))]
        + hosted,
        out_specs=[pl.BlockSpec((tm, tn), lambda i, j, k: (i, j))] + hosted,
        out_shape=[jax.ShapeDtypeStruct((m, nn), dtype)] + (exchange.out_shape if exchange else []),
        scratch_shapes=[pltpu.VMEM((tm, tn), F32)] + (exchange.scratch if exchange else []),
        compiler_params=_params(("arbitrary", "arbitrary", "arbitrary")),
    )(a, b, *ex_grads)


FFN_SLABS_FWD = ((0, 1024), (1024, 2816))
FFN_SLABS_BWD = ((0, 1024), (1024, 2816))
TS_PROJ = 512
TS_MIX = 256
TS_FFN = 256
TS_WGRAD = 2048


def _local_step(x, tgt, vec_d, vec_f, placed, place):
    s = x.shape[0]
    tw = min(TS_WGRAD, s)
    sp_in, sp_pool, sp_bout, sp_o, sp_up, sp_down = SHARDED
    proj, h1, w_in, w_pool, w_bout, w_o = _fwd_proj(x, vec_d, placed[0], placed[1:4], place, min(TS_PROJ, s))
    x1, o, pg, q, merged, ya0, yb, cv, w_up, w_down = _fwd_mix(proj, x, vec_d, w_pool, w_bout, w_o, placed[4:],
                                                               min(TS_MIX, s))
    up0, upc, a, h2, dx2, dff, vo_f, loss = _fwd_ffn(x1, tgt, vec_d, vec_f, w_up, w_down, min(TS_FFN, s))
    g_down, = _wgrad(a, dff, F // 2, D, tw, "wgrad_down", BF16)
    dx1, dup0, vo_b, fo, r_down = _bwd_ffn(dff, dx2, x1, up0, upc, vec_d, vec_f, w_up, w_down,
                                           _GradExchange([sp_down]), [g_down], min(TS_FFN, s))
    g_up, = _wgrad(h2, dup0, D, F2 // NCHIP, tw, "wgrad_up", BF16)
    dproj, vo_m, g_o, g_bout, g_pool, r_up = _bwd_mix(dx1, o, proj, cv, ya0, yb, merged, q, pg, vec_d,
                                                      w_pool, w_bout, w_o, _GradExchange([sp_up]), [g_up],
                                                      min(TS_MIX, s))
    g_in, r_pool, r_bout, r_o = _wgrad(h1, dproj, D, DIN // NCHIP, tw, "wgrad_in", BF16,
                                       _GradExchange([sp_pool, sp_bout, sp_o]), [g_pool, g_bout, g_o])
    dx, vo_i, r_in = _bwd_in(dproj, dx1, x, vec_d, w_in, _GradExchange([sp_in]), [g_in], min(TS_PROJ, s))
    vecs = dict(
        dsh1=vo_i[0], dsc1=vo_i[1], dg_pre_mix=vo_i[2],
        dgt1=vo_m[0], dg_post_mix=vo_m[1], dpool_scale=vo_m[2], dconv_b=vo_m[3],
        dconv_w=vo_m[4:7],
        dsh2=vo_b[0], dsc2=vo_b[1], dg_pre_ffn=vo_b[2],
        dgt2=vo_f[0], dg_post_ffn=vo_f[1],
        dffn_conv_w=fo[FV_W0:FV_W2 + 1], dffn_conv_b=fo[FV_B],
    )
    local = dict(w_in=g_in, w_pool=g_pool, w_bout=g_bout, w_o=g_o, w_up=g_up, w_down=g_down)
    received = dict(w_in=r_in, w_pool=r_pool, w_bout=r_bout, w_o=r_o, w_up=r_up, w_down=r_down)
    return loss, dx, vecs, local, received


def _aligned(offset, n):
    return offset if isinstance(offset, int) else pl.multiple_of(offset, n)


class _Sharded:
    def __init__(self, name, full_shape, shard_axis, half_axis):
        self.name = name
        self.full_shape = full_shape
        self.shard_axis = shard_axis
        self.half_axis = half_axis
        self.shard_shape = tuple(n // NCHIP if a == shard_axis else n for a, n in enumerate(full_shape))
        self.piece_shape = tuple(n // 2 if a == half_axis else n for a, n in enumerate(self.shard_shape))

    def piece(self, full_ref, k, h):
        idx = []
        for a, n in enumerate(self.piece_shape):
            if a == self.shard_axis and a == self.half_axis:
                idx.append(pl.ds(_aligned((2 * k + h) * n, n), n))
            elif a == self.shard_axis:
                idx.append(pl.ds(_aligned(k * n, n), n))
            elif a == self.half_axis:
                idx.append(pl.ds(_aligned(h * n, n), n))
            else:
                idx.append(slice(None))
        return full_ref.at[tuple(idx)]

    def shard(self, full_ref, k):
        n = self.shard_shape[self.shard_axis]
        idx = [pl.ds(_aligned(k * n, n), n) if a == self.shard_axis else slice(None)
               for a in range(len(self.full_shape))]
        return full_ref.at[tuple(idx)]

    def half(self, shard_ref, h):
        n = self.piece_shape[self.half_axis]
        idx = [pl.ds(_aligned(h * n, n), n) if a == self.half_axis else slice(None)
               for a in range(len(self.full_shape))]
        return shard_ref.at[tuple(idx)]

SHARDED = (
    _Sharded("w_in", (D, DIN), 1, 0),
    _Sharded("w_pool", (NG, GW, GW), 1, 0),
    _Sharded("w_bout", (D, D), 0, 0),
    _Sharded("w_o", (D, D), 0, 0),
    _Sharded("w_up", (D, F2), 1, 0),
    _Sharded("w_down", (F, D), 0, 0),
)
NW = len(SHARDED)


def _mesh_place():
    x, y, c = lax.axis_index("x"), lax.axis_index("y"), lax.axis_index("c")
    chips = [(1 - x, y), (x, 1 - y), (1 - x, 1 - y)]
    return x, y, c, 2 * x + y, chips, [2 * px + py for px, py in chips]


def _remote(src, dst, send_sem, recv_sem, device):
    return pltpu.make_async_remote_copy(src_ref=src, dst_ref=dst, send_sem=send_sem, recv_sem=recv_sem,
                                        device_id=device, device_id_type=MESH)


SMALL_GATHER_SCRATCH = [pltpu.SemaphoreType.DMA((7,)), pltpu.SemaphoreType.DMA((7,)), pltpu.SemaphoreType.DMA]


def _small_gather(x_ref, out_ref, send_sems, recv_sems, local_sem):
    m_per = x_ref.shape[0]
    x, y, c, _, chips, _ = _mesh_place()
    me, sibling = (x, y, c), (x, y, 1 - c)

    def rows(px, py, pc):
        return out_ref.at[pl.ds((4 * px + 2 * py + pc) * m_per, m_per), :]

    def copy(k, blk, to, src=None):
        return _remote(rows(*blk) if src is None else src, rows(*blk), send_sems.at[k], recv_sems.at[k], to)

    mine = pltpu.make_async_copy(x_ref, rows(*me), local_sem)
    mine.start()
    first = [copy(0, me, sibling, src=x_ref)]
    first += [copy(1 + j, me, (*chip, c), src=x_ref) for j, chip in enumerate(chips)]
    for cp in first:
        cp.start()
    passed = [copy(4 + j, (*chip, c), sibling) for j, chip in enumerate(chips)]
    for j, chip in enumerate(chips):
        copy(1 + j, (*chip, c), me).wait_recv()
        passed[j].start()
    copy(0, sibling, me).wait_recv()
    for j, chip in enumerate(chips):
        copy(4 + j, (*chip, 1 - c), me).wait_recv()
    for cp in first + passed:
        cp.wait_send()
    mine.wait()


def _all_gather_small(block, name):
    m_per, n = block.shape
    return pl.pallas_call(
        _small_gather_body(), name=name,
        out_shape=jax.ShapeDtypeStruct((NDEV * m_per, n), block.dtype),
        in_specs=[pl.BlockSpec(memory_space=pltpu.VMEM)],
        out_specs=pl.BlockSpec(memory_space=pltpu.VMEM),
        scratch_shapes=SMALL_GATHER_SCRATCH,
        compiler_params=pltpu.CompilerParams(vmem_limit_bytes=VMEM_LIMIT),
    )(block)


def _small_gather_body():
    def body(x_ref, out_ref, send_sems, recv_sems, local_sem):
        _small_gather(x_ref, out_ref, send_sems, recv_sems, local_sem)
    return body


class _WeightGather:
    def __init__(self, specs):
        self.specs = specs
        self.n = len(specs)
        self.specs_any = [pl.BlockSpec(memory_space=pl.ANY)] * self.n
        self.out_shape = [jax.ShapeDtypeStruct(sp.full_shape, BF16) for sp in specs]
        self.scratch = [pltpu.SemaphoreType.DMA((6 * self.n,)), pltpu.SemaphoreType.DMA((6 * self.n,))]

    def _sends(self, outs, send_sems, recv_sems):
        x, y, c, k_me, chips, _ = _mesh_place()
        sends = []
        for j, chip in enumerate(chips):
            for w, sp in enumerate(self.specs):
                mine = sp.piece(outs[w], k_me, c)
                sends.append(_remote(mine, mine, send_sems.at[6 * w + j], recv_sems.at[6 * w + j], (*chip, c)))
        return sends

    def start(self, outs, send_sems, recv_sems):
        for cp in self._sends(outs, send_sems, recv_sems):
            cp.start()

    def _passes(self, outs, send_sems, recv_sems):
        x, y, c, _, chips, kidx = _mesh_place()
        return [_remote(sp.piece(outs[w], kidx[j], c), sp.piece(outs[w], kidx[j], c),
                        send_sems.at[6 * w + 3 + j], recv_sems.at[6 * w + 3 + j], (x, y, 1 - c))
                for j in range(3) for w, sp in enumerate(self.specs)]

    def forward(self, outs, send_sems, recv_sems):
        x, y, c, _, chips, kidx = _mesh_place()
        for j, chip in enumerate(chips):
            for w, sp in enumerate(self.specs):
                landed = sp.piece(outs[w], kidx[j], c)
                _remote(landed, landed, send_sems.at[6 * w + j], recv_sems.at[6 * w + j], (*chip, c)).wait_recv()
        for cp in self._passes(outs, send_sems, recv_sems):
            cp.start()

    def drain(self, outs, send_sems, recv_sems):
        x, y, c, _, chips, kidx = _mesh_place()
        for j in range(3):
            for w, sp in enumerate(self.specs):
                landed = sp.piece(outs[w], kidx[j], 1 - c)
                _remote(landed, landed, send_sems.at[6 * w + 3 + j], recv_sems.at[6 * w + 3 + j],
                        (x, y, 1 - c)).wait_recv()
        for cp in self._sends(outs, send_sems, recv_sems) + self._passes(outs, send_sems, recv_sems):
            cp.wait_send()

    def finish(self, outs, send_sems, recv_sems):
        self.forward(outs, send_sems, recv_sems)
        self.drain(outs, send_sems, recv_sems)


class _GradExchange:
    def __init__(self, specs):
        self.specs = specs
        self.n = len(specs)
        self.specs_any = [pl.BlockSpec(memory_space=pl.ANY)] * self.n
        self.out_shape = [jax.ShapeDtypeStruct((NDEV,) + sp.piece_shape, BF16) for sp in specs]
        self.scratch = [pltpu.SemaphoreType.DMA((7 * self.n,)), pltpu.SemaphoreType.DMA((NDEV * self.n,))]

    def _sends(self, grads, recvs, send_sems, recv_sems):
        x, y, c, k_me, chips, kidx = _mesh_place()
        dev = 2 * k_me + c
        sends = []
        for w, sp in enumerate(self.specs):
            slot, arrival = recvs[w].at[dev], recv_sems.at[NDEV * w + dev]
            sends.append(_remote(sp.piece(grads[w], k_me, 1 - c), slot, send_sems.at[7 * w], arrival, (x, y, 1 - c)))
            for j, chip in enumerate(chips):
                for h in range(2):
                    sends.append(_remote(sp.piece(grads[w], kidx[j], h), slot, send_sems.at[7 * w + 1 + 2 * j + h],
                                         arrival, (*chip, h)))
        return sends

    def start(self, grads, recvs, send_sems, recv_sems):
        for cp in self._sends(grads, recvs, send_sems, recv_sems):
            cp.start()

    def finish(self, grads, recvs, send_sems, recv_sems):
        x, y, c, k_me, _, _ = _mesh_place()
        dev = 2 * k_me + c
        for w in range(self.n):
            for d in range(NDEV):
                landed = recvs[w].at[d]
                arrival = _remote(landed, landed, send_sems.at[7 * w], recv_sems.at[NDEV * w + d], (x, y, c))
                pl.when(d != dev)(arrival.wait_recv)
        for cp in self._sends(grads, recvs, send_sems, recv_sems):
            cp.wait_send()


def _device_sums(locals_, recvs, place):
    def body(p_ref, *refs):
        a_refs, b_refs, o_refs = refs[:NW], refs[NW:2 * NW], refs[2 * NW:]
        d = pl.program_id(0)
        own = d == p_ref[2]
        terms = [jnp.where(own, a_ref[...], b_ref[...]).astype(F32) for a_ref, b_ref in zip(a_refs, b_refs)]

        @pl.when(d == 0)
        def _():
            for o_ref, term in zip(o_refs, terms):
                o_ref[...] = term

        @pl.when(d > 0)
        def _():
            for o_ref, term in zip(o_refs, terms):
                o_ref[...] += term

    def mine(sp):
        nd = len(sp.piece_shape)
        return pl.BlockSpec(sp.piece_shape, lambda d, p_ref: tuple(
            2 * p_ref[0] + p_ref[1] if a == sp.shard_axis == sp.half_axis else
            p_ref[0] if a == sp.shard_axis else p_ref[1] if a == sp.half_axis else 0 for a in range(nd)))

    def others(sp):
        nd = len(sp.piece_shape)
        return pl.BlockSpec((None,) + sp.piece_shape,
                            lambda d, p_ref: (jnp.where(d == p_ref[2], (d + 1) % NDEV, d),) + (0,) * nd)

    def half(sp):
        nd = len(sp.piece_shape)
        return pl.BlockSpec(sp.piece_shape,
                            lambda d, p_ref: tuple(p_ref[1] if a == sp.half_axis else 0 for a in range(nd)))

    return pl.pallas_call(
        body, name="rs_device_sums",
        grid_spec=pltpu.PrefetchScalarGridSpec(
            num_scalar_prefetch=1, grid=(NDEV,),
            in_specs=[mine(sp) for sp in SHARDED] + [others(sp) for sp in SHARDED],
            out_specs=[half(sp) for sp in SHARDED]),
        out_shape=[jax.ShapeDtypeStruct(sp.shard_shape, F32) for sp in SHARDED],
        compiler_params=_params(("arbitrary",)),
    )(place, *locals_, *recvs)


def _pair_share(halves, vector_block):
    m_per, n = vector_block.shape

    def body(*refs):
        x_ref = refs[NW]
        outs, gathered = refs[NW + 1:2 * NW + 1], refs[2 * NW + 1]
        send_sems, recv_sems = refs[2 * NW + 2:2 * NW + 4]
        x, y, c, _, _, _ = _mesh_place()
        sibling = (x, y, 1 - c)
        sent = []
        for w, sp in enumerate(SHARDED):
            mine = sp.half(outs[w], c)
            cp = _remote(mine, mine, send_sems.at[w], recv_sems.at[w], sibling)
            cp.start()
            sent.append(cp)
        _small_gather(x_ref, gathered, *refs[2 * NW + 4:])
        for w, sp in enumerate(SHARDED):
            landed = sp.half(outs[w], 1 - c)
            _remote(landed, landed, send_sems.at[w], recv_sems.at[w], sibling).wait_recv()
        for cp in sent:
            cp.wait_send()

    hbm = pl.BlockSpec(memory_space=pl.ANY)
    vmem = pl.BlockSpec(memory_space=pltpu.VMEM)
    out = pl.pallas_call(
        body, name="rs_pair_share",
        out_shape=[jax.ShapeDtypeStruct(sp.shard_shape, F32) for sp in SHARDED]
        + [jax.ShapeDtypeStruct((NDEV * m_per, n), F32)],
        in_specs=[hbm] * NW + [vmem], out_specs=[hbm] * NW + [vmem],
        input_output_aliases={w: w for w in range(NW)},
        scratch_shapes=[pltpu.SemaphoreType.DMA((NW,)), pltpu.SemaphoreType.DMA((NW,))] + SMALL_GATHER_SCRATCH,
        compiler_params=pltpu.CompilerParams(vmem_limit_bytes=VMEM_LIMIT),
    )(*halves, vector_block)
    return out[:NW], out[NW]


def _reduce_scatter(local, received, place, vector_block):
    halves = _device_sums([local[sp.name] for sp in SHARDED], [received[sp.name] for sp in SHARDED], place)
    return _pair_share(halves, vector_block)


def _place_bf16(sp, w, place):
    nd = len(sp.full_shape)

    def body(p_ref, w_ref, o_ref):
        o_ref[...] = w_ref[...].astype(BF16)

    return pl.pallas_call(
        body, name="place_" + sp.name,
        grid_spec=pltpu.PrefetchScalarGridSpec(
            num_scalar_prefetch=1, grid=(1,),
            in_specs=[pl.BlockSpec(sp.shard_shape, lambda i, p_ref: (0,) * nd)],
            out_specs=pl.BlockSpec(sp.shard_shape,
                                   lambda i, p_ref: tuple(p_ref[0] if a == sp.shard_axis else 0 for a in range(nd)))),
        out_shape=jax.ShapeDtypeStruct(sp.full_shape, BF16),
        compiler_params=_params(("arbitrary",)),
    )(place, w)


def _matmul_f32(a, b, name):
    def body(a_ref, b_ref, o_ref):
        o_ref[...] = jnp.dot(a_ref[...], b_ref[...], preferred_element_type=F32, precision=lax.Precision.HIGHEST)

    return pl.pallas_call(body, name=name, out_shape=jax.ShapeDtypeStruct((a.shape[0], b.shape[1]), F32),
                          compiler_params=pltpu.CompilerParams(vmem_limit_bytes=VMEM_LIMIT))(a, b)


def _sum_devices(stacked):
    def body(x_ref, o_ref):
        acc = x_ref[0]
        for d in range(1, NDEV):
            acc = acc + x_ref[d]
        o_ref[...] = acc

    return pl.pallas_call(body, name="sum_devices", out_shape=jax.ShapeDtypeStruct(stacked.shape[1:], F32),
                          compiler_params=pltpu.CompilerParams(vmem_limit_bytes=VMEM_LIMIT))(stacked)


ADAMW_STEPS = 8


def _adamw(ws, gs, ms, vs, name):
    n = len(ws)
    steps = ADAMW_STEPS if all(w.shape[0] % (8 * ADAMW_STEPS) == 0 for w in ws) else 1

    def body(*refs):
        ins, outs = refs[:4 * n], refs[4 * n:]
        for k in range(n):
            w_ref, g_ref, m_ref, v_ref = ins[k], ins[n + k], ins[2 * n + k], ins[3 * n + k]
            gv = g_ref[...]
            nm = ADAM_B1 * m_ref[...] + (1.0 - ADAM_B1) * gv
            nv = ADAM_B2 * v_ref[...] + (1.0 - ADAM_B2) * (gv * gv)
            m_hat = nm / (1.0 - ADAM_B1 ** ADAM_STEP)
            v_hat = nv / (1.0 - ADAM_B2 ** ADAM_STEP)
            outs[k][...] = -ADAM_LR * (m_hat / (jnp.sqrt(v_hat) + ADAM_EPS) + ADAM_WD * w_ref[...])
            outs[n + k][...] = nm
            outs[2 * n + k][...] = nv

    blks = [pl.BlockSpec((w.shape[0] // steps, w.shape[1]), lambda i: (i, 0)) for w in ws]
    shapes = [jax.ShapeDtypeStruct(w.shape, F32) for w in ws]
    out = pl.pallas_call(
        body, name="adamw_" + name, grid=(steps,), in_specs=blks * 4, out_specs=blks * 3, out_shape=shapes * 3,
        compiler_params=_params(("parallel",)),
    )(*ws, *gs, *ms, *vs)
    return out[:n], out[n:2 * n], out[2 * n:]


WEIGHT_NAMES = ("g_pre_mix", "g_post_mix", "g_pre_ffn", "g_post_ffn", "w_ada", "b_ada", "w_in", "w_pool",
                "pool_scale", "conv_w", "conv_b", "w_bout", "w_o", "w_up", "ffn_conv_w", "ffn_conv_b", "w_down")
MATRIX_NAMES = ("w_ada",) + tuple(sp.name for sp in SHARDED)
VECTOR_NAMES = tuple(n for n in WEIGHT_NAMES if n not in MATRIX_NAMES)

CW = D // NCHIP
FCW = F2 // NCHIP
ADA_W = DIN // NCHIP
COND_BLOCK = (8, 768)
GRAD_BLOCK = (8, 4864)


def _flat_pad(parts, shape):
    flat = jnp.concatenate([p.reshape(-1) for p in parts])
    return jnp.pad(flat, (0, shape[0] * shape[1] - flat.shape[0])).reshape(shape)


def _take(flat, offset, shape):
    size = 1
    for n in shape:
        size *= n
    return flat[offset:offset + size].reshape(shape), offset + size


def kernel(x, c, g_pre_mix, g_post_mix, g_pre_ffn, g_post_ffn, w_ada, b_ada, w_in, w_pool, pool_scale, conv_w, conv_b, w_bout, w_o, w_up, ffn_conv_w, ffn_conv_b, w_down, loss_target, m_g_pre_mix, m_g_post_mix, m_g_pre_ffn, m_g_post_ffn, m_w_ada, m_b_ada, m_w_in, m_w_pool, m_pool_scale, m_conv_w, m_conv_b, m_w_bout, m_w_o, m_w_up, m_ffn_conv_w, m_ffn_conv_b, m_w_down, v_g_pre_mix, v_g_post_mix, v_g_pre_ffn, v_g_post_ffn, v_w_ada, v_b_ada, v_w_in, v_w_pool, v_pool_scale, v_conv_w, v_conv_b, v_w_bout, v_w_o, v_w_up, v_ffn_conv_w, v_ffn_conv_b, v_w_down):
    weights = dict(g_pre_mix=g_pre_mix, g_post_mix=g_post_mix, g_pre_ffn=g_pre_ffn, g_post_ffn=g_post_ffn,
                   w_ada=w_ada, b_ada=b_ada, w_in=w_in, w_pool=w_pool, pool_scale=pool_scale, conv_w=conv_w,
                   conv_b=conv_b, w_bout=w_bout, w_o=w_o, w_up=w_up, ffn_conv_w=ffn_conv_w, ffn_conv_b=ffn_conv_b,
                   w_down=w_down)
    mom1 = dict(g_pre_mix=m_g_pre_mix, g_post_mix=m_g_post_mix, g_pre_ffn=m_g_pre_ffn, g_post_ffn=m_g_post_ffn,
                w_ada=m_w_ada, b_ada=m_b_ada, w_in=m_w_in, w_pool=m_w_pool, pool_scale=m_pool_scale,
                conv_w=m_conv_w, conv_b=m_conv_b, w_bout=m_w_bout, w_o=m_w_o, w_up=m_w_up,
                ffn_conv_w=m_ffn_conv_w, ffn_conv_b=m_ffn_conv_b, w_down=m_w_down)
    mom2 = dict(g_pre_mix=v_g_pre_mix, g_post_mix=v_g_post_mix, g_pre_ffn=v_g_pre_ffn, g_post_ffn=v_g_post_ffn,
                w_ada=v_w_ada, b_ada=v_b_ada, w_in=v_w_in, w_pool=v_w_pool, pool_scale=v_pool_scale,
                conv_w=v_conv_w, conv_b=v_conv_b, w_bout=v_w_bout, w_o=v_w_o, w_up=v_w_up,
                ffn_conv_w=v_ffn_conv_w, ffn_conv_b=v_ffn_conv_b, w_down=v_w_down)

    chip = 2 * lax.axis_index("x") + lax.axis_index("y")
    core = lax.axis_index("c")
    dev = 2 * chip + core
    place = jnp.stack([chip, core, dev]).astype(jnp.int32)

    cond = _all_gather_small(_flat_pad([c, conv_w, ffn_conv_w], COND_BLOCK), "gather_cond")
    cond = cond.reshape(NDEV, -1)
    c_all = cond[:, :D]
    by_chip = cond[0::2]
    conv_w_full = by_chip[:, D:D + 3 * CW].reshape(NCHIP, 3, CW).transpose(1, 0, 2).reshape(3, D)
    ffn_w_full = by_chip[:, D + 3 * CW:D + 3 * CW + 3 * FCW].reshape(NCHIP, 3, FCW).transpose(1, 0, 2).reshape(3, F2)

    mod_cols = _all_gather_small(_matmul_f32(c_all, w_ada[0], "ada_mod"), "gather_mod")
    mod_cols = mod_cols.reshape(NDEV, NDEV, ADA_W)[0::2]
    mod = lax.dynamic_index_in_dim(mod_cols, dev, axis=1, keepdims=False).reshape(6, D) + b_ada.reshape(6, D)
    vec_d = jnp.concatenate([mod, g_pre_mix, g_post_mix, g_pre_ffn, g_post_ffn, pool_scale, conv_b, conv_w_full,
                             jnp.zeros((VD_ROWS - 15, D), F32)], axis=0)
    vec_f = jnp.concatenate([ffn_w_full, ffn_conv_b, jnp.zeros((FV_ROWS - 4, F2), F32)], axis=0)

    placed = [_place_bf16(sp, weights[sp.name][0], place) for sp in SHARDED]
    loss_blk, dx, vecs, local, received = _local_step(x[0], loss_target[0], vec_d, vec_f, placed, place)

    dmod = [vecs[n] for n in ("dsh1", "dsc1", "dgt1", "dsh2", "dsc2", "dgt2")]
    small = [vecs["dg_pre_mix"], vecs["dg_post_mix"], vecs["dg_pre_ffn"], vecs["dg_post_ffn"]] + dmod + [
        vecs["dpool_scale"], vecs["dconv_w"], vecs["dconv_b"], vecs["dffn_conv_w"], vecs["dffn_conv_b"],
        loss_blk[0]]
    reduced, gathered = _reduce_scatter(local, received, place, _flat_pad(small, GRAD_BLOCK))
    total = _sum_devices(gathered.reshape((NDEV,) + GRAD_BLOCK)).reshape(-1)
    vgrad = {}
    off = 0
    for n in ("g_pre_mix", "g_post_mix", "g_pre_ffn", "g_post_ffn"):
        vgrad[n], off = _take(total, off, (1, D))
    dmod_off = off
    vgrad["b_ada"], off = _take(total, off, (1, DIN))
    vgrad["pool_scale"], off = _take(total, off, (1, D))
    g_conv_w, off = _take(total, off, (3, D))
    vgrad["conv_w"] = lax.dynamic_slice_in_dim(g_conv_w, chip * CW, CW, axis=1)[None]
    vgrad["conv_b"], off = _take(total, off, (1, D))
    g_ffn_w, off = _take(total, off, (3, F2))
    vgrad["ffn_conv_w"] = lax.dynamic_slice_in_dim(g_ffn_w, chip * FCW, FCW, axis=1)[None]
    vgrad["ffn_conv_b"], off = _take(total, off, (1, F2))
    loss = total[off]

    dmod_all = gathered.reshape(NDEV, -1)[:, dmod_off:dmod_off + DIN]
    dmod_cols = lax.dynamic_slice_in_dim(dmod_all, chip * ADA_W, ADA_W, axis=1)
    g_ada = _matmul_f32(jnp.pad(c_all.T, ((0, 0), (0, 128 - NDEV))), jnp.pad(dmod_cols, ((0, 128 - NDEV), (0, 0))),
                        "ada_wgrad")

    mgrad = {"w_ada": g_ada}
    for sp, g in zip(SHARDED, reduced):
        mgrad[sp.name] = g

    grad, delta, new_m, new_v = {}, {}, {}, {}
    two_d = lambda tree: [tree[n].reshape(-1, weights[n].shape[-1]) for n in MATRIX_NAMES]
    ds, nms, nvs = _adamw(two_d(weights), two_d(mgrad), two_d(mom1), two_d(mom2), "matrices")
    for n, d, nm, nv in zip(MATRIX_NAMES, ds, nms, nvs):
        shape = weights[n].shape
        grad[n], delta[n], new_m[n], new_v[n] = (a.reshape(shape) for a in (mgrad[n], d, nm, nv))
    flat = lambda tree: [jnp.concatenate([tree[n].reshape(1, -1) for n in VECTOR_NAMES], axis=1)]
    (d,), (nm,), (nv,) = _adamw(flat(weights), flat(vgrad), flat(mom1), flat(mom2), "vectors")
    off = 0
    for n in VECTOR_NAMES:
        shape = weights[n].shape
        grad[n] = vgrad[n].reshape(shape)
        delta[n], _ = _take(d[0], off, shape)
        new_m[n], _ = _take(nm[0], off, shape)
        new_v[n], off = _take(nv[0], off, shape)

    return (loss, dx[None], *[grad[n] for n in WEIGHT_NAMES], *[delta[n] for n in WEIGHT_NAMES],
            *[new_m[n] for n in WEIGHT_NAMES], *[new_v[n] for n in WEIGHT_NAMES])
```

```python
import jax
import jax.numpy as jnp
from jax import lax
from jax.experimental import pallas as pl
from jax.experimental.pallas import tpu as pltpu

F32 = jnp.float32
BF16 = jnp.bfloat16

D = 1024
DIN = 6 * D
F = 2816
F2 = 2 * F
NG = 4
GW = D // NG
POOL_CARRY = 16
CONV_CARRY = 3
EPS = 1e-6
NCHIP = 4
NDEV = 8

ADAM_LR = 0.001
ADAM_B1 = 0.9
ADAM_B2 = 0.999
ADAM_EPS = 1e-08
ADAM_WD = 0.01
ADAM_STEP = 10

VMEM_LIMIT = 60 * 1024 * 1024

(V_SH1, V_SC1, V_GT1, V_SH2, V_SC2, V_GT2, V_GPRE1, V_GPOST1, V_GPRE2, V_GPOST2,
 V_PSCALE, V_CB, V_CW0, V_CW1, V_CW2) = range(15)
VD_ROWS = 16
FV_W0, FV_W1, FV_W2, FV_B = range(4)
FV_ROWS = 8

MESH = pl.DeviceIdType.MESH


def _params(sem=None, vmem=VMEM_LIMIT):
    return pltpu.CompilerParams(dimension_semantics=sem, vmem_limit_bytes=vmem)


def _row(ref, r):
    return ref[r:r + 1, :]


def _load_once(pairs, sem):
    @pl.when(pl.program_id(0) == 0)
    def _():
        copies = [pltpu.make_async_copy(src, dst, sem.at[n]) for n, (src, dst) in enumerate(pairs)]
        for cp in copies:
            cp.start()
        for cp in copies:
            cp.wait()


def _dot(a, b):
    return jnp.dot(a, b, preferred_element_type=F32)


def _dot_nt(a, b):
    return lax.dot_general(a, b, (((1,), (1,)), ((), ())), preferred_element_type=F32)


BLK = 256
SEG = BLK // 8


def _load_rows(ref, ts):
    blocks = [jnp.swapaxes(ref[b * BLK:(b + 1) * BLK, :].reshape(8, SEG, ref.shape[-1]), 0, 1).reshape(BLK, -1)
              for b in range(ts // BLK)]
    return jnp.concatenate(blocks, axis=0)


def _store_rows(ref, val, ts):
    for b in range(ts // BLK):
        blk = val[b * BLK:(b + 1) * BLK, :].reshape(SEG, 8, val.shape[-1])
        ref[b * BLK:(b + 1) * BLK, :] = jnp.swapaxes(blk, 0, 1).reshape(BLK, -1)


def _times(t0):
    p = lax.broadcasted_iota(jnp.int32, (BLK, 1), 0)
    return t0 + (p & 7) * SEG + (p >> 3)


def _before(x, carry, s):
    x3 = x.reshape(SEG, 8, x.shape[-1])
    tail = pltpu.roll(x3[SEG - s:], 1, 1)
    row = lax.broadcasted_iota(jnp.int32, tail.shape, 1)
    out = jnp.concatenate([jnp.where(row == 0, carry, tail), x3[:SEG - s]], axis=0)
    return out.reshape(x.shape), tail


def _after(x, carry, s):
    x3 = x.reshape(SEG, 8, x.shape[-1])
    head = pltpu.roll(x3[:s], 7, 1)
    row = lax.broadcasted_iota(jnp.int32, head.shape, 1)
    out = jnp.concatenate([x3[s:], jnp.where(row == 7, carry, head)], axis=0)
    return out.reshape(x.shape), head


def _causal_conv(x, carry, cols, w0, w1, w2, b):
    x1, carry[0:1, :, cols] = _before(x, carry[0:1, :, cols], 1)
    x2, carry[1:3, :, cols] = _before(x, carry[1:3, :, cols], 2)
    return b + w2 * x + w1 * x1 + w0 * x2


def _causal_conv_bwd(dy, carry, cols, w0, w1, w2):
    d1, carry[0:1, :, cols] = _after(dy, carry[0:1, :, cols], 1)
    d2, carry[1:3, :, cols] = _after(dy, carry[1:3, :, cols], 2)
    return w2 * dy + w1 * d1 + w0 * d2, d1, d2


def _pool_counts(t0, g):
    return jnp.minimum((_times(t0) + 1).astype(F32), float(2 << g))


def _rms(x):
    return lax.rsqrt(jnp.mean(x * x, axis=-1, keepdims=True) + EPS)


def _rms_bwd(dn, n, r):
    return r * (dn - n * jnp.mean(dn * n, axis=-1, keepdims=True))


def _colsum(x):
    return jnp.sum(x, axis=0, keepdims=True)


def _gelu_and_grad(x):
    k, a = 0.7978845608028654, 0.044715
    x2 = x * x
    th1 = 1.0 + jnp.tanh(x * (x2 * (k * a) + k))
    hx = 0.5 * x
    gelu = hx * th1
    dgelu = 0.5 * th1 + (hx * (th1 * (2.0 - th1))) * (x2 * (3.0 * k * a) + k)
    return gelu, dgelu


def _fwd_proj(x, vec_d, placed_in, placed_rest, place, ts):
    s = x.shape[0]
    nt = s // ts
    cw = DIN // NCHIP
    sp_in = SHARDED[0]
    gather = _WeightGather(SHARDED[1:4])
    n = gather.n

    def body(*refs):
        p_ref, x_ref, v_ref = refs[:3]
        proj_ref, h1_ref, w_full = refs[4 + n:7 + n]
        rest = refs[7 + n:7 + 2 * n]
        w_vmem, h1_all, sem, in_send, in_recv, send_sems, recv_sems = refs[7 + 2 * n:]
        j, i = pl.program_id(0), pl.program_id(1)
        x_, y_, c, k_me, _, _ = _mesh_place()
        sibling = (x_, y_, 1 - c)

        def peer(t):
            return (x_ ^ (t >> 1), y_ ^ (t & 1))

        def w_in_sends():
            mine = sp_in.piece(w_full, k_me, c)
            return [_remote(mine, mine, in_send.at[t - 1], in_recv.at[t - 1], (*peer(t), c)) for t in (1, 2, 3)]

        def load_block(k):
            cp = pltpu.make_async_copy(sp_in.shard(w_full, k), w_vmem.at[k], sem.at[0])
            cp.start()
            cp.wait()

        @pl.when((j == 0) & (i == 0))
        def _():
            for cp in w_in_sends()[:2]:
                cp.start()
            load_block(k_me)

        @pl.when((j == 1) & (i == 0))
        def _():
            for cp in w_in_sends()[:2]:
                cp.wait_send()
            w_in_sends()[2].start()
            gather.start(rest, send_sems, recv_sems)

        for t in (1, 2, 3):
            @pl.when((j == t) & (i == 0))
            def _(t=t):
                k = k_me ^ t
                landed = sp_in.piece(w_full, k, c)
                _remote(landed, landed, in_send.at[t - 1], in_recv.at[t - 1], (*peer(t), c)).wait_recv()
                _remote(landed, landed, in_send.at[2 + t], in_recv.at[2 + t], sibling).start()
                other = sp_in.piece(w_full, k, 1 - c)
                _remote(other, other, in_send.at[2 + t], in_recv.at[2 + t], sibling).wait_recv()
                load_block(k)

        @pl.when(j == 0)
        def _():
            xv = _load_rows(x_ref, ts)
            n1 = xv * _rms(xv)
            h = n1 * (_row(v_ref, V_GPRE1) * (1.0 + _row(v_ref, V_SC1))) + _row(v_ref, V_SH1)
            hb = h.astype(BF16)
            h1_ref[...] = hb
            h1_all[i] = hb

        proj_ref[...] = _dot(h1_all[i], w_vmem[k_me ^ j]).astype(BF16)

        @pl.when((j == NCHIP - 1) & (i == nt - 1))
        def _():
            w_in_sends()[2].wait_send()
            for t in (1, 2, 3):
                landed = sp_in.piece(w_full, k_me ^ t, c)
                _remote(landed, landed, in_send.at[2 + t], in_recv.at[2 + t], sibling).wait_send()
            gather.finish(rest, send_sems, recv_sems)

    once = lambda w: pl.BlockSpec((ts, w), lambda j, i, p: (jnp.where(j == 0, i, nt - 1), 0))
    return pl.pallas_call(
        body, name="fwd_proj",
        grid_spec=pltpu.PrefetchScalarGridSpec(
            num_scalar_prefetch=1, grid=(NCHIP, nt),
            in_specs=[once(D), pl.BlockSpec((VD_ROWS, D), lambda j, i, p: (0, 0)),
                      pl.BlockSpec(memory_space=pl.ANY)] + gather.specs_any,
            out_specs=[pl.BlockSpec((ts, cw), lambda j, i, p: (i, p[0] ^ j)), once(D),
                       pl.BlockSpec(memory_space=pl.ANY)] + gather.specs_any,
            scratch_shapes=[pltpu.VMEM((NCHIP, D, cw), BF16), pltpu.VMEM((nt, ts, D), BF16),
                            pltpu.SemaphoreType.DMA((1,)),
                            pltpu.SemaphoreType.DMA((6,)), pltpu.SemaphoreType.DMA((6,))] + gather.scratch),
        out_shape=[jax.ShapeDtypeStruct((s, DIN), BF16), jax.ShapeDtypeStruct((s, D), BF16),
                   jax.ShapeDtypeStruct(sp_in.full_shape, BF16)] + gather.out_shape,
        input_output_aliases={3 + w: 2 + w for w in range(n + 1)},
        compiler_params=_params(("arbitrary", "arbitrary")),
    )(place, x, vec_d, placed_in, *placed_rest)


def _fwd_mix(proj, x, vec_d, w_pool, w_bout, w_o, placed_ffn, ts):
    s = x.shape[0]
    gather = _WeightGather(SHARDED[4:])
    n = gather.n

    def body(*refs):
        ins, outs, rest = refs[:6], refs[6 + n:14 + n], refs[14 + n:14 + 2 * n]
        scratch, sems = refs[14 + 2 * n:-2], refs[-2:]
        i = pl.program_id(0)
        nt = s // ts
        pl.when(i == 0)(lambda: gather.start(rest, *sems))
        pl.when(i == nt - 1 - nt // 8)(lambda: gather.forward(rest, *sems))
        compute(*ins, *outs, *scratch)
        pl.when(i == nt - 1)(lambda: gather.drain(rest, *sems))

    def compute(p_ref, x_ref, v_ref, wp_hbm, wb_hbm, wo_hbm,
                x1_ref, o_ref, pg_ref, q_ref, mg_ref, ya_ref, yb_ref, cv_ref,
                wp, wb, wo, carry_p, carry_v, sem):
        i = pl.program_id(0)
        _load_once([(wp_hbm, wp), (wb_hbm, wb), (wo_hbm, wo)], sem)

        @pl.when(i == 0)
        def _():
            carry_p[...] = jnp.zeros_like(carry_p)
            carry_v[...] = jnp.zeros_like(carry_v)

        t0 = i * ts
        for g in range(NG):
            cols = slice(g * GW, (g + 1) * GW)
            u = p_ref[:, cols].astype(F32)
            e = u
            for l in range(g + 1):
                slot = slice((1 << l) - 1, (2 << l) - 1)
                shifted, carry_p[slot, :, cols] = _before(e, carry_p[slot, :, cols], 1 << l)
                e = e + shifted
            pgb = (e / _pool_counts(t0, g) - u).astype(BF16)
            pg_ref[:, cols] = pgb
            ya_ref[:, cols] = _dot(pgb, wp[g]).astype(BF16)

        u_x = p_ref[:, D:2 * D].astype(F32)
        u_c = p_ref[:, 3 * D:4 * D].astype(F32)
        v = u_c * u_x
        cv = _causal_conv(v, carry_v, slice(None), _row(v_ref, V_CW0), _row(v_ref, V_CW1),
                          _row(v_ref, V_CW2), _row(v_ref, V_CB))
        cv_ref[...] = cv.astype(BF16)
        q = (p_ref[:, 2 * D:3 * D].astype(F32) * cv).astype(BF16)
        q_ref[...] = q
        y_b = _dot(q, wb[...])
        yb_ref[...] = y_b.astype(BF16)

        y_a = ya_ref[...].astype(F32) * _row(v_ref, V_PSCALE)
        merged = (jax.nn.sigmoid(p_ref[:, 4 * D:5 * D].astype(F32)) * y_a
                  + jax.nn.sigmoid(p_ref[:, 5 * D:6 * D].astype(F32)) * y_b).astype(BF16)
        mg_ref[...] = merged
        o = _dot(merged, wo[...])
        o_ref[...] = o.astype(BF16)
        x1_ref[...] = _load_rows(x_ref, ts) + _row(v_ref, V_GT1) * ((o * _rms(o)) * _row(v_ref, V_GPOST1))

    tile = lambda w: pl.BlockSpec((ts, w), lambda i: (i, 0))
    hbm = pl.BlockSpec(memory_space=pl.ANY)
    return pl.pallas_call(
        body, name="fwd_mix", grid=(s // ts,),
        in_specs=[tile(DIN), tile(D), pl.BlockSpec((VD_ROWS, D), lambda i: (0, 0)), hbm, hbm, hbm] + gather.specs_any,
        out_specs=[tile(D)] * 8 + gather.specs_any,
        out_shape=[jax.ShapeDtypeStruct((s, D), F32)] + [jax.ShapeDtypeStruct((s, D), BF16)] * 7 + gather.out_shape,
        input_output_aliases={6 + w: 8 + w for w in range(n)},
        scratch_shapes=[pltpu.VMEM((NG, GW, GW), BF16), pltpu.VMEM((D, D), BF16), pltpu.VMEM((D, D), BF16),
                        pltpu.VMEM((POOL_CARRY, 8, D), F32), pltpu.VMEM((CONV_CARRY, 8, D), F32),
                        pltpu.SemaphoreType.DMA((3,))] + gather.scratch,
        compiler_params=_params(("arbitrary",)),
    )(proj, x, vec_d, w_pool, w_bout, w_o, *placed_ffn)


def _fwd_ffn(x1, tgt, vec_d, vec_f, w_up, w_down, ts):
    s = x1.shape[0]

    def body(x1_ref, t_ref, v_ref, f_ref, wu_hbm, wd_hbm,
             up_ref, upc_ref, a_ref, h2_ref, dx2_ref, dff_ref, vo_ref, loss_ref,
             wu, wd, carry, sem):
        i = pl.program_id(0)
        _load_once([(wu_hbm, wu), (wd_hbm, wd)], sem)

        @pl.when(i == 0)
        def _():
            carry[...] = jnp.zeros_like(carry)
            vo_ref[...] = jnp.zeros_like(vo_ref)
            loss_ref[...] = jnp.zeros_like(loss_ref)

        x1v = x1_ref[...]
        n3 = x1v * _rms(x1v)
        h2 = (n3 * (_row(v_ref, V_GPRE2) * (1.0 + _row(v_ref, V_SC2))) + _row(v_ref, V_SH2)).astype(BF16)
        h2_ref[...] = h2

        ff = jnp.zeros((ts, D), F32)
        for lo, hi in FFN_SLABS_FWD:
            up = []
            for cols in (slice(lo, hi), slice(F + lo, F + hi)):
                u0 = _dot(h2, wu[:, cols])
                up_ref[:, cols] = u0.astype(BF16)
                y = _causal_conv(u0, carry, cols, f_ref[FV_W0:FV_W0 + 1, cols], f_ref[FV_W1:FV_W1 + 1, cols],
                                 f_ref[FV_W2:FV_W2 + 1, cols], f_ref[FV_B:FV_B + 1, cols])
                upc_ref[:, cols] = y.astype(BF16)
                up.append(y)
            gelu, _ = _gelu_and_grad(up[0])
            a = (gelu * up[1]).astype(BF16)
            a_ref[:, lo:hi] = a
            ff = ff + _dot(a, wd[lo:hi, :])

        r4 = _rms(ff)
        n4 = ff * r4
        gt2 = _row(v_ref, V_GT2)
        gpost = _row(v_ref, V_GPOST2)
        gate_gain = gt2 * gpost
        diff = (x1v + gate_gain * n4) - _load_rows(t_ref, ts)
        loss_ref[...] += jnp.full(loss_ref.shape, 0.5 / D * jnp.sum(diff * diff), F32)
        dx2_ref[...] = diff * (1.0 / D)
        s1 = _colsum(diff * n4)
        vo_ref[0:1, :] += s1 * (gpost * (1.0 / D))
        vo_ref[1:2, :] += s1 * (gt2 * (1.0 / D))
        dff_ref[...] = _rms_bwd(diff * (gate_gain * (1.0 / D)), n4, r4).astype(BF16)

    tile = lambda w: pl.BlockSpec((ts, w), lambda i: (i, 0))
    full = lambda r, w: pl.BlockSpec((r, w), lambda i: (0, 0))
    hbm = pl.BlockSpec(memory_space=pl.ANY)
    return pl.pallas_call(
        body, name="fwd_ffn", grid=(s // ts,),
        in_specs=[tile(D), tile(D), full(VD_ROWS, D), full(FV_ROWS, F2), hbm, hbm],
        out_specs=[tile(F2), tile(F2), tile(F), tile(D), tile(D), tile(D), full(8, D), full(8, 128)],
        out_shape=[jax.ShapeDtypeStruct((s, F2), BF16), jax.ShapeDtypeStruct((s, F2), BF16),
                   jax.ShapeDtypeStruct((s, F), BF16),
                   jax.ShapeDtypeStruct((s, D), BF16), jax.ShapeDtypeStruct((s, D), F32),
                   jax.ShapeDtypeStruct((s, D), BF16), jax.ShapeDtypeStruct((8, D), F32),
                   jax.ShapeDtypeStruct((8, 128), F32)],
        scratch_shapes=[pltpu.VMEM((D, F2), BF16), pltpu.VMEM((F, D), BF16), pltpu.VMEM((CONV_CARRY, 8, F2), F32),
                        pltpu.SemaphoreType.DMA((2,))],
        compiler_params=_params(("arbitrary",)),
    )(x1, tgt, vec_d, vec_f, w_up, w_down)


def _bwd_ffn(dff, dx2, x1, up0, upc, vec_d, vec_f, w_up, w_down, exchange, ex_grads, ts):
    s = x1.shape[0]
    nt = s // ts
    n = exchange.n

    def body(*refs):
        ins, grads = refs[:9], refs[9:9 + n]
        outs, recvs = refs[9 + n:13 + n], refs[13 + n:13 + 2 * n]
        scratch, sems = refs[13 + 2 * n:-2], refs[-2:]
        i = pl.program_id(0)
        pl.when(i == 0)(lambda: exchange.start(grads, recvs, *sems))
        compute(*ins, *outs, *scratch)
        pl.when(i == nt - 1)(lambda: exchange.finish(grads, recvs, *sems))

    def compute(dff_ref, dx2_ref, x1_ref, up_ref, upc_ref, v_ref, f_ref, wu_hbm, wd_hbm,
                dx1_ref, dup_ref, vo_ref, fo_ref, wu, wd, carry, sem):
        i = pl.program_id(0)
        _load_once([(wu_hbm, wu), (wd_hbm, wd)], sem)

        @pl.when(i == 0)
        def _():
            carry[...] = jnp.zeros_like(carry)
            vo_ref[...] = jnp.zeros_like(vo_ref)
            fo_ref[...] = jnp.zeros_like(fo_ref)

        dffb = dff_ref[...]

        dh2 = jnp.zeros((ts, D), F32)
        for lo, hi in FFN_SLABS_BWD:
            slabs = (slice(lo, hi), slice(F + lo, F + hi))
            gelu, dgelu = _gelu_and_grad(upc_ref[:, slabs[0]].astype(F32))
            da = _dot_nt(dffb, wd[lo:hi, :])
            dups = (da * upc_ref[:, slabs[1]].astype(F32) * dgelu, da * gelu)
            for cols, dup in zip(slabs, dups):
                du0, d1, d2 = _causal_conv_bwd(dup, carry, cols, f_ref[FV_W0:FV_W0 + 1, cols],
                                               f_ref[FV_W1:FV_W1 + 1, cols], f_ref[FV_W2:FV_W2 + 1, cols])
                u0 = up_ref[:, cols].astype(F32)
                fo_ref[FV_B:FV_B + 1, cols] += _colsum(dup)
                fo_ref[FV_W2:FV_W2 + 1, cols] += _colsum(dup * u0)
                fo_ref[FV_W1:FV_W1 + 1, cols] += _colsum(d1 * u0)
                fo_ref[FV_W0:FV_W0 + 1, cols] += _colsum(d2 * u0)
                du0 = du0.astype(BF16)
                dup_ref[:, cols] = du0
                dh2 = dh2 + _dot_nt(du0, wu[:, cols])

        x1v = x1_ref[...]
        r3 = _rms(x1v)
        n3 = x1v * r3
        gpre = _row(v_ref, V_GPRE2)
        sc = 1.0 + _row(v_ref, V_SC2)
        vo_ref[0:1, :] += _colsum(dh2)
        s2 = _colsum(dh2 * n3)
        vo_ref[1:2, :] += s2 * gpre
        vo_ref[2:3, :] += s2 * sc
        dx1_ref[...] = dx2_ref[...] + _rms_bwd(dh2 * (gpre * sc), n3, r3)

    rev = lambda w: pl.BlockSpec((ts, w), lambda i: (nt - 1 - i, 0))
    full = lambda r, w: pl.BlockSpec((r, w), lambda i: (0, 0))
    hbm = pl.BlockSpec(memory_space=pl.ANY)
    return pl.pallas_call(
        body, name="bwd_ffn", grid=(nt,),
        in_specs=[rev(D), rev(D), rev(D), rev(F2), rev(F2), full(VD_ROWS, D), full(FV_ROWS, F2), hbm, hbm]
        + exchange.specs_any,
        out_specs=[rev(D), rev(F2), full(8, D), full(FV_ROWS, F2)] + exchange.specs_any,
        out_shape=[jax.ShapeDtypeStruct((s, D), F32), jax.ShapeDtypeStruct((s, F2), BF16),
                   jax.ShapeDtypeStruct((8, D), F32), jax.ShapeDtypeStruct((FV_ROWS, F2), F32)] + exchange.out_shape,
        scratch_shapes=[pltpu.VMEM((D, F2), BF16), pltpu.VMEM((F, D), BF16), pltpu.VMEM((CONV_CARRY, 8, F2), F32),
                        pltpu.SemaphoreType.DMA((2,))] + exchange.scratch,
        compiler_params=_params(("arbitrary",)),
    )(dff, dx2, x1, up0, upc, vec_d, vec_f, w_up, w_down, *ex_grads)


def _bwd_mix(dx1, o, proj, cv, ya0, yb, merged, q, pg, vec_d, w_pool, w_bout, w_o, exchange, ex_grads, ts):
    s = dx1.shape[0]
    nt = s // ts
    n = exchange.n

    def body(*refs):
        ins, grads = refs[:13], refs[13:13 + n]
        outs, recvs = refs[13 + n:18 + n], refs[18 + n:18 + 2 * n]
        scratch, sems = refs[18 + 2 * n:-2], refs[-2:]
        i = pl.program_id(0)
        pl.when(i == 0)(lambda: exchange.start(grads, recvs, *sems))
        compute(*ins, *outs, *scratch)
        pl.when(i == nt - 1)(lambda: exchange.finish(grads, recvs, *sems))

    def compute(dx1_ref, o_ref, p_ref, cv_ref, ya_ref, yb_ref, mg_ref, q_ref, pg_ref, v_ref, wp_hbm, wb_hbm, wo_hbm,
                dp_ref, vo_ref, go_ref, gb_ref, gp_ref, wp, wb, wo, carry_d, carry_e, acc_o, acc_b, acc_p, sem):
        i = pl.program_id(0)
        _load_once([(wp_hbm, wp), (wb_hbm, wb), (wo_hbm, wo)], sem)

        @pl.when(i == 0)
        def _():
            carry_d[...] = jnp.zeros_like(carry_d)
            carry_e[...] = jnp.zeros_like(carry_e)
            vo_ref[...] = jnp.zeros_like(vo_ref)
            acc_o[...] = jnp.zeros_like(acc_o)
            acc_b[...] = jnp.zeros_like(acc_b)
            acc_p[...] = jnp.zeros_like(acc_p)

        t0 = (nt - 1 - i) * ts
        dx1v = dx1_ref[...]
        ov = o_ref[...].astype(F32)
        r2 = _rms(ov)
        n2 = ov * r2
        gpost = _row(v_ref, V_GPOST1)
        gt1 = _row(v_ref, V_GT1)
        s1 = _colsum(dx1v * n2)
        vo_ref[0:1, :] += s1 * gpost
        vo_ref[1:2, :] += s1 * gt1
        dob = _rms_bwd(dx1v * (gt1 * gpost), n2, r2).astype(BF16)
        acc_o[...] += _dot_tn(mg_ref[...], dob)
        dmerged = _dot_nt(dob, wo[...])

        ya0 = ya_ref[...].astype(F32)
        pscale = _row(v_ref, V_PSCALE)
        sa = jax.nn.sigmoid(p_ref[:, 4 * D:5 * D].astype(F32))
        dp_ref[:, 4 * D:5 * D] = (dmerged * (ya0 * pscale) * sa * (1.0 - sa)).astype(BF16)
        dy_a = dmerged * sa
        vo_ref[2:3, :] += _colsum(dy_a * ya0)
        dya0 = (dy_a * pscale).astype(BF16)

        sb = jax.nn.sigmoid(p_ref[:, 5 * D:6 * D].astype(F32))
        dp_ref[:, 5 * D:6 * D] = (dmerged * yb_ref[...].astype(F32) * sb * (1.0 - sb)).astype(BF16)
        dy_b = (dmerged * sb).astype(BF16)
        acc_b[...] += _dot_tn(q_ref[...], dy_b)
        dq = _dot_nt(dy_b, wb[...])

        u_x = p_ref[:, D:2 * D].astype(F32)
        u_b = p_ref[:, 2 * D:3 * D].astype(F32)
        u_c = p_ref[:, 3 * D:4 * D].astype(F32)
        w0, w1, w2 = _row(v_ref, V_CW0), _row(v_ref, V_CW1), _row(v_ref, V_CW2)
        dp_ref[:, 2 * D:3 * D] = (dq * cv_ref[...].astype(F32)).astype(BF16)
        dcv = dq * u_b
        dv, d1, d2 = _causal_conv_bwd(dcv, carry_d, slice(None), w0, w1, w2)
        v = u_c * u_x
        vo_ref[3:4, :] += _colsum(dcv)
        vo_ref[4:5, :] += _colsum(d2 * v)
        vo_ref[5:6, :] += _colsum(d1 * v)
        vo_ref[6:7, :] += _colsum(dcv * v)
        dp_ref[:, D:2 * D] = (dv * u_c).astype(BF16)
        dp_ref[:, 3 * D:4 * D] = (dv * u_x).astype(BF16)

        for g in range(NG):
            cols = slice(g * GW, (g + 1) * GW)
            acc_p[g] += _dot_tn(pg_ref[:, cols], dya0[:, cols])
            dpg = _dot_nt(dya0[:, cols], wp[g])
            e = dpg / _pool_counts(t0, g)
            for l in range(g + 1):
                slot = slice((1 << l) - 1, (2 << l) - 1)
                shifted, carry_e[slot, :, cols] = _after(e, carry_e[slot, :, cols], 1 << l)
                e = e + shifted
            dp_ref[:, cols] = (e - dpg).astype(BF16)

        @pl.when(i == nt - 1)
        def _():
            go_ref[...] = acc_o[...].astype(BF16)
            gb_ref[...] = acc_b[...].astype(BF16)
            gp_ref[...] = acc_p[...].astype(BF16)

    rev = lambda w: pl.BlockSpec((ts, w), lambda i: (nt - 1 - i, 0))
    hbm = pl.BlockSpec(memory_space=pl.ANY)
    whole = lambda shape: pl.BlockSpec(shape, lambda i: (0,) * len(shape))
    return pl.pallas_call(
        body, name="bwd_mix", grid=(nt,),
        in_specs=[rev(D), rev(D), rev(DIN)] + [rev(D)] * 6 + [whole((VD_ROWS, D)), hbm, hbm, hbm] + exchange.specs_any,
        out_specs=[rev(DIN), whole((8, D)), whole((D, D)), whole((D, D)), whole((NG, GW, GW))] + exchange.specs_any,
        out_shape=[jax.ShapeDtypeStruct((s, DIN), BF16), jax.ShapeDtypeStruct((8, D), F32),
                   jax.ShapeDtypeStruct((D, D), BF16), jax.ShapeDtypeStruct((D, D), BF16),
                   jax.ShapeDtypeStruct((NG, GW, GW), BF16)] + exchange.out_shape,
        scratch_shapes=[pltpu.VMEM((NG, GW, GW), BF16), pltpu.VMEM((D, D), BF16), pltpu.VMEM((D, D), BF16),
                        pltpu.VMEM((CONV_CARRY, 8, D), F32), pltpu.VMEM((POOL_CARRY, 8, D), F32),
                        pltpu.VMEM((D, D), F32), pltpu.VMEM((D, D), F32), pltpu.VMEM((NG, GW, GW), F32),
                        pltpu.SemaphoreType.DMA((3,))] + exchange.scratch,
        compiler_params=_params(("arbitrary",)),
    )(dx1, o, proj, cv, ya0, yb, merged, q, pg, vec_d, w_pool, w_bout, w_o, *ex_grads)


def _bwd_in(dproj, dx1, x, vec_d, w_in, exchange, ex_grads, ts):
    s = x.shape[0]
    nt = s // ts
    n = exchange.n

    def body(*refs):
        ins, grads = refs[:5], refs[5:5 + n]
        outs, recvs = refs[5 + n:7 + n], refs[7 + n:7 + 2 * n]
        scratch, sems = refs[7 + 2 * n:-2], refs[-2:]
        i = pl.program_id(0)
        pl.when(i == 0)(lambda: exchange.start(grads, recvs, *sems))
        compute(*ins, *outs, *scratch)
        pl.when(i == nt - 1)(lambda: exchange.finish(grads, recvs, *sems))

    def compute(dp_ref, dx1_ref, x_ref, v_ref, w_hbm, dx_ref, vo_ref, w_vmem, sem):
        _load_once([(w_hbm, w_vmem)], sem)

        @pl.when(pl.program_id(0) == 0)
        def _():
            vo_ref[...] = jnp.zeros_like(vo_ref)

        dh1 = _dot_nt(dp_ref[...], w_vmem[...])
        xv = _load_rows(x_ref, ts)
        r1 = _rms(xv)
        n1 = xv * r1
        gpre = _row(v_ref, V_GPRE1)
        sc = 1.0 + _row(v_ref, V_SC1)
        vo_ref[0:1, :] += _colsum(dh1)
        s1 = _colsum(dh1 * n1)
        vo_ref[1:2, :] += s1 * gpre
        vo_ref[2:3, :] += s1 * sc
        _store_rows(dx_ref, dx1_ref[...] + _rms_bwd(dh1 * (gpre * sc), n1, r1), ts)

    tile = lambda w: pl.BlockSpec((ts, w), lambda i: (i, 0))
    return pl.pallas_call(
        body, name="bwd_in", grid=(s // ts,),
        in_specs=[tile(DIN), tile(D), tile(D), pl.BlockSpec((VD_ROWS, D), lambda i: (0, 0)),
                  pl.BlockSpec(memory_space=pl.ANY)] + exchange.specs_any,
        out_specs=[tile(D), pl.BlockSpec((8, D), lambda i: (0, 0))] + exchange.specs_any,
        out_shape=[jax.ShapeDtypeStruct((s, D), F32), jax.ShapeDtypeStruct((8, D), F32)] + exchange.out_shape,
        scratch_shapes=[pltpu.VMEM((D, DIN), BF16), pltpu.SemaphoreType.DMA((1,))] + exchange.scratch,
        compiler_params=_params(("arbitrary",)),
    )(dproj, dx1, x, vec_d, w_in, *ex_grads)


def _dot_tn(a, b):
    return lax.dot_general(a, b, (((0,), (0,)), ((), ())), preferred_element_type=F32)


def _wgrad(a, b, tm, tn, ts, name, dtype, exchange=None, ex_grads=()):
    s, m = a.shape
    nn = b.shape[1]
    grid = (m // tm, nn // tn, s // ts)
    n = exchange.n if exchange else 0

    def body(*refs):
        a_ref, b_ref = refs[:2]
        grads = refs[2:2 + n]
        o_ref = refs[2 + n]
        recvs = refs[3 + n:3 + 2 * n]
        acc = refs[3 + 2 * n]
        sems = refs[4 + 2 * n:]
        i, j, k = pl.program_id(0), pl.program_id(1), pl.program_id(2)
        if exchange:
            pl.when((i == 0) & (j == 0) & (k == 0))(lambda: exchange.start(grads, recvs, *sems))
        part = _dot_tn(a_ref[...], b_ref[...])

        @pl.when(k == 0)
        def _():
            acc[...] = part

        @pl.when(k > 0)
        def _():
            acc[...] += part

        @pl.when(k == grid[2] - 1)
        def _():
            o_ref[...] = acc[...].astype(dtype)

        if exchange:
            pl.when((i == grid[0] - 1) & (j == grid[1] - 1) & (k == grid[2] - 1))(
                lambda: exchange.finish(grads, recvs, *sems))

    hosted = exchange.specs_any if exchange else []
    return pl.pallas_call(
        body, name=name, grid=grid,
        in_specs=[pl.BlockSpec((ts, tm), lambda i, j, k: (k, i)), pl.BlockSpec((ts, tn), lambda i, j, k: (k, j))]
        + hosted,
        out_specs=[pl.BlockSpec((tm, tn), lambda i, j, k: (i, j))] + hosted,
        out_shape=[jax.ShapeDtypeStruct((m, nn), dtype)] + (exchange.out_shape if exchange else []),
        scratch_shapes=[pltpu.VMEM((tm, tn), F32)] + (exchange.scratch if exchange else []),
        compiler_params=_params(("arbitrary", "arbitrary", "arbitrary")),
    )(a, b, *ex_grads)


FFN_SLABS_FWD = ((0, 2816),)
FFN_SLABS_BWD = ((0, 1536), (1536, 2816))
TS_PROJ = 512
TS_MIX = 256
TS_FFN = 256
TS_WGRAD = 2048


def _local_step(x, tgt, vec_d, vec_f, placed, place):
    s = x.shape[0]
    tw = min(TS_WGRAD, s)
    sp_in, sp_pool, sp_bout, sp_o, sp_up, sp_down = SHARDED
    proj, h1, w_in, w_pool, w_bout, w_o = _fwd_proj(x, vec_d, placed[0], placed[1:4], place, min(2 * TS_PROJ, s))
    x1, o, pg, q, merged, ya0, yb, cv, w_up, w_down = _fwd_mix(proj, x, vec_d, w_pool, w_bout, w_o, placed[4:],
                                                               min(TS_MIX, s))
    up0, upc, a, h2, dx2, dff, vo_f, loss = _fwd_ffn(x1, tgt, vec_d, vec_f, w_up, w_down, min(TS_FFN, s))
    g_down, = _wgrad(a, dff, F // 2, D, tw, "wgrad_down", BF16)
    dx1, dup0, vo_b, fo, r_down = _bwd_ffn(dff, dx2, x1, up0, upc, vec_d, vec_f, w_up, w_down,
                                           _GradExchange([sp_down]), [g_down], min(TS_FFN, s))
    g_up, = _wgrad(h2, dup0, D, F2 // NCHIP, tw, "wgrad_up", BF16)
    dproj, vo_m, g_o, g_bout, g_pool, r_up = _bwd_mix(dx1, o, proj, cv, ya0, yb, merged, q, pg, vec_d,
                                                      w_pool, w_bout, w_o, _GradExchange([sp_up]), [g_up],
                                                      min(TS_MIX, s))
    g_in, r_pool, r_bout, r_o = _wgrad(h1, dproj, D, DIN // NCHIP, tw, "wgrad_in", BF16,
                                       _GradExchange([sp_pool, sp_bout, sp_o]), [g_pool, g_bout, g_o])
    dx, vo_i, r_in = _bwd_in(dproj, dx1, x, vec_d, w_in, _GradExchange([sp_in]), [g_in], min(TS_PROJ, s))
    vecs = dict(
        dsh1=vo_i[0], dsc1=vo_i[1], dg_pre_mix=vo_i[2],
        dgt1=vo_m[0], dg_post_mix=vo_m[1], dpool_scale=vo_m[2], dconv_b=vo_m[3],
        dconv_w=vo_m[4:7],
        dsh2=vo_b[0], dsc2=vo_b[1], dg_pre_ffn=vo_b[2],
        dgt2=vo_f[0], dg_post_ffn=vo_f[1],
        dffn_conv_w=fo[FV_W0:FV_W2 + 1], dffn_conv_b=fo[FV_B],
    )
    local = dict(w_in=g_in, w_pool=g_pool, w_bout=g_bout, w_o=g_o, w_up=g_up, w_down=g_down)
    received = dict(w_in=r_in, w_pool=r_pool, w_bout=r_bout, w_o=r_o, w_up=r_up, w_down=r_down)
    return loss, dx, vecs, local, received


def _aligned(offset, n):
    return offset if isinstance(offset, int) else pl.multiple_of(offset, n)


class _Sharded:
    def __init__(self, name, full_shape, shard_axis, half_axis):
        self.name = name
        self.full_shape = full_shape
        self.shard_axis = shard_axis
        self.half_axis = half_axis
        self.shard_shape = tuple(n // NCHIP if a == shard_axis else n for a, n in enumerate(full_shape))
        self.piece_shape = tuple(n // 2 if a == half_axis else n for a, n in enumerate(self.shard_shape))

    def piece(self, full_ref, k, h):
        idx = []
        for a, n in enumerate(self.piece_shape):
            if a == self.shard_axis and a == self.half_axis:
                idx.append(pl.ds(_aligned((2 * k + h) * n, n), n))
            elif a == self.shard_axis:
                idx.append(pl.ds(_aligned(k * n, n), n))
            elif a == self.half_axis:
                idx.append(pl.ds(_aligned(h * n, n), n))
            else:
                idx.append(slice(None))
        return full_ref.at[tuple(idx)]

    def shard(self, full_ref, k):
        n = self.shard_shape[self.shard_axis]
        idx = [pl.ds(_aligned(k * n, n), n) if a == self.shard_axis else slice(None)
               for a in range(len(self.full_shape))]
        return full_ref.at[tuple(idx)]

    def half(self, shard_ref, h):
        n = self.piece_shape[self.half_axis]
        idx = [pl.ds(_aligned(h * n, n), n) if a == self.half_axis else slice(None)
               for a in range(len(self.full_shape))]
        return shard_ref.at[tuple(idx)]

SHARDED = (
    _Sharded("w_in", (D, DIN), 1, 0),
    _Sharded("w_pool", (NG, GW, GW), 1, 0),
    _Sharded("w_bout", (D, D), 0, 0),
    _Sharded("w_o", (D, D), 0, 0),
    _Sharded("w_up", (D, F2), 1, 0),
    _Sharded("w_down", (F, D), 0, 0),
)
NW = len(SHARDED)


def _mesh_place():
    x, y, c = lax.axis_index("x"), lax.axis_index("y"), lax.axis_index("c")
    chips = [(1 - x, y), (x, 1 - y), (1 - x, 1 - y)]
    return x, y, c, 2 * x + y, chips, [2 * px + py for px, py in chips]


def _remote(src, dst, send_sem, recv_sem, device):
    return pltpu.make_async_remote_copy(src_ref=src, dst_ref=dst, send_sem=send_sem, recv_sem=recv_sem,
                                        device_id=device, device_id_type=MESH)


SMALL_GATHER_SCRATCH = [pltpu.SemaphoreType.DMA((7,)), pltpu.SemaphoreType.DMA((7,)), pltpu.SemaphoreType.DMA]


def _small_gather(x_ref, out_ref, send_sems, recv_sems, local_sem):
    m_per = x_ref.shape[0]
    x, y, c, _, chips, _ = _mesh_place()
    me, sibling = (x, y, c), (x, y, 1 - c)

    def rows(px, py, pc):
        return out_ref.at[pl.ds((4 * px + 2 * py + pc) * m_per, m_per), :]

    def copy(k, blk, to, src=None):
        return _remote(rows(*blk) if src is None else src, rows(*blk), send_sems.at[k], recv_sems.at[k], to)

    mine = pltpu.make_async_copy(x_ref, rows(*me), local_sem)
    mine.start()
    first = [copy(0, me, sibling, src=x_ref)]
    first += [copy(1 + j, me, (*chip, c), src=x_ref) for j, chip in enumerate(chips)]
    for cp in first:
        cp.start()
    passed = [copy(4 + j, (*chip, c), sibling) for j, chip in enumerate(chips)]
    for j, chip in enumerate(chips):
        copy(1 + j, (*chip, c), me).wait_recv()
        passed[j].start()
    copy(0, sibling, me).wait_recv()
    for j, chip in enumerate(chips):
        copy(4 + j, (*chip, 1 - c), me).wait_recv()
    for cp in first + passed:
        cp.wait_send()
    mine.wait()


def _all_gather_small(block, name):
    m_per, n = block.shape
    return pl.pallas_call(
        _small_gather_body(), name=name,
        out_shape=jax.ShapeDtypeStruct((NDEV * m_per, n), block.dtype),
        in_specs=[pl.BlockSpec(memory_space=pltpu.VMEM)],
        out_specs=pl.BlockSpec(memory_space=pltpu.VMEM),
        scratch_shapes=SMALL_GATHER_SCRATCH,
        compiler_params=pltpu.CompilerParams(vmem_limit_bytes=VMEM_LIMIT),
    )(block)


def _small_gather_body():
    def body(x_ref, out_ref, send_sems, recv_sems, local_sem):
        _small_gather(x_ref, out_ref, send_sems, recv_sems, local_sem)
    return body


class _WeightGather:
    def __init__(self, specs):
        self.specs = specs
        self.n = len(specs)
        self.specs_any = [pl.BlockSpec(memory_space=pl.ANY)] * self.n
        self.out_shape = [jax.ShapeDtypeStruct(sp.full_shape, BF16) for sp in specs]
        self.scratch = [pltpu.SemaphoreType.DMA((6 * self.n,)), pltpu.SemaphoreType.DMA((6 * self.n,))]

    def _sends(self, outs, send_sems, recv_sems):
        x, y, c, k_me, chips, _ = _mesh_place()
        sends = []
        for j, chip in enumerate(chips):
            for w, sp in enumerate(self.specs):
                mine = sp.piece(outs[w], k_me, c)
                sends.append(_remote(mine, mine, send_sems.at[6 * w + j], recv_sems.at[6 * w + j], (*chip, c)))
        return sends

    def start(self, outs, send_sems, recv_sems):
        for cp in self._sends(outs, send_sems, recv_sems):
            cp.start()

    def _passes(self, outs, send_sems, recv_sems):
        x, y, c, _, chips, kidx = _mesh_place()
        return [_remote(sp.piece(outs[w], kidx[j], c), sp.piece(outs[w], kidx[j], c),
                        send_sems.at[6 * w + 3 + j], recv_sems.at[6 * w + 3 + j], (x, y, 1 - c))
                for j in range(3) for w, sp in enumerate(self.specs)]

    def forward(self, outs, send_sems, recv_sems):
        x, y, c, _, chips, kidx = _mesh_place()
        for j, chip in enumerate(chips):
            for w, sp in enumerate(self.specs):
                landed = sp.piece(outs[w], kidx[j], c)
                _remote(landed, landed, send_sems.at[6 * w + j], recv_sems.at[6 * w + j], (*chip, c)).wait_recv()
        for cp in self._passes(outs, send_sems, recv_sems):
            cp.start()

    def drain(self, outs, send_sems, recv_sems):
        x, y, c, _, chips, kidx = _mesh_place()
        for j in range(3):
            for w, sp in enumerate(self.specs):
                landed = sp.piece(outs[w], kidx[j], 1 - c)
                _remote(landed, landed, send_sems.at[6 * w + 3 + j], recv_sems.at[6 * w + 3 + j],
                        (x, y, 1 - c)).wait_recv()
        for cp in self._sends(outs, send_sems, recv_sems) + self._passes(outs, send_sems, recv_sems):
            cp.wait_send()

    def finish(self, outs, send_sems, recv_sems):
        self.forward(outs, send_sems, recv_sems)
        self.drain(outs, send_sems, recv_sems)


class _GradExchange:
    def __init__(self, specs):
        self.specs = specs
        self.n = len(specs)
        self.specs_any = [pl.BlockSpec(memory_space=pl.ANY)] * self.n
        self.out_shape = [jax.ShapeDtypeStruct((NDEV,) + sp.piece_shape, BF16) for sp in specs]
        self.scratch = [pltpu.SemaphoreType.DMA((7 * self.n,)), pltpu.SemaphoreType.DMA((NDEV * self.n,))]

    def _sends(self, grads, recvs, send_sems, recv_sems):
        x, y, c, k_me, chips, kidx = _mesh_place()
        dev = 2 * k_me + c
        sends = []
        for w, sp in enumerate(self.specs):
            slot, arrival = recvs[w].at[dev], recv_sems.at[NDEV * w + dev]
            sends.append(_remote(sp.piece(grads[w], k_me, 1 - c), slot, send_sems.at[7 * w], arrival, (x, y, 1 - c)))
            for j, chip in enumerate(chips):
                for h in range(2):
                    sends.append(_remote(sp.piece(grads[w], kidx[j], h), slot, send_sems.at[7 * w + 1 + 2 * j + h],
                                         arrival, (*chip, h)))
        return sends

    def start(self, grads, recvs, send_sems, recv_sems):
        for cp in self._sends(grads, recvs, send_sems, recv_sems):
            cp.start()

    def finish(self, grads, recvs, send_sems, recv_sems):
        x, y, c, k_me, _, _ = _mesh_place()
        dev = 2 * k_me + c
        for w in range(self.n):
            for d in range(NDEV):
                landed = recvs[w].at[d]
                arrival = _remote(landed, landed, send_sems.at[7 * w], recv_sems.at[NDEV * w + d], (x, y, c))
                pl.when(d != dev)(arrival.wait_recv)
        for cp in self._sends(grads, recvs, send_sems, recv_sems):
            cp.wait_send()


def _device_sums(locals_, recvs, place):
    def body(p_ref, *refs):
        a_refs, b_refs, o_refs = refs[:NW], refs[NW:2 * NW], refs[2 * NW:]
        d = pl.program_id(0)
        own = d == p_ref[2]
        terms = [jnp.where(own, a_ref[...], b_ref[...]).astype(F32) for a_ref, b_ref in zip(a_refs, b_refs)]

        @pl.when(d == 0)
        def _():
            for o_ref, term in zip(o_refs, terms):
                o_ref[...] = term

        @pl.when(d > 0)
        def _():
            for o_ref, term in zip(o_refs, terms):
                o_ref[...] += term

    def mine(sp):
        nd = len(sp.piece_shape)
        return pl.BlockSpec(sp.piece_shape, lambda d, p_ref: tuple(
            2 * p_ref[0] + p_ref[1] if a == sp.shard_axis == sp.half_axis else
            p_ref[0] if a == sp.shard_axis else p_ref[1] if a == sp.half_axis else 0 for a in range(nd)))

    def others(sp):
        nd = len(sp.piece_shape)
        return pl.BlockSpec((None,) + sp.piece_shape,
                            lambda d, p_ref: (jnp.where(d == p_ref[2], (d + 1) % NDEV, d),) + (0,) * nd)

    def half(sp):
        nd = len(sp.piece_shape)
        return pl.BlockSpec(sp.piece_shape,
                            lambda d, p_ref: tuple(p_ref[1] if a == sp.half_axis else 0 for a in range(nd)))

    return pl.pallas_call(
        body, name="rs_device_sums",
        grid_spec=pltpu.PrefetchScalarGridSpec(
            num_scalar_prefetch=1, grid=(NDEV,),
            in_specs=[mine(sp) for sp in SHARDED] + [others(sp) for sp in SHARDED],
            out_specs=[half(sp) for sp in SHARDED]),
        out_shape=[jax.ShapeDtypeStruct(sp.shard_shape, F32) for sp in SHARDED],
        compiler_params=_params(("arbitrary",)),
    )(place, *locals_, *recvs)


def _pair_share(halves, vector_block):
    m_per, n = vector_block.shape

    def body(*refs):
        x_ref = refs[NW]
        outs, gathered = refs[NW + 1:2 * NW + 1], refs[2 * NW + 1]
        send_sems, recv_sems = refs[2 * NW + 2:2 * NW + 4]
        x, y, c, _, _, _ = _mesh_place()
        sibling = (x, y, 1 - c)
        sent = []
        for w, sp in enumerate(SHARDED):
            mine = sp.half(outs[w], c)
            cp = _remote(mine, mine, send_sems.at[w], recv_sems.at[w], sibling)
            cp.start()
            sent.append(cp)
        _small_gather(x_ref, gathered, *refs[2 * NW + 4:])
        for w, sp in enumerate(SHARDED):
            landed = sp.half(outs[w], 1 - c)
            _remote(landed, landed, send_sems.at[w], recv_sems.at[w], sibling).wait_recv()
        for cp in sent:
            cp.wait_send()

    hbm = pl.BlockSpec(memory_space=pl.ANY)
    vmem = pl.BlockSpec(memory_space=pltpu.VMEM)
    out = pl.pallas_call(
        body, name="rs_pair_share",
        out_shape=[jax.ShapeDtypeStruct(sp.shard_shape, F32) for sp in SHARDED]
        + [jax.ShapeDtypeStruct((NDEV * m_per, n), F32)],
        in_specs=[hbm] * NW + [vmem], out_specs=[hbm] * NW + [vmem],
        input_output_aliases={w: w for w in range(NW)},
        scratch_shapes=[pltpu.SemaphoreType.DMA((NW,)), pltpu.SemaphoreType.DMA((NW,))] + SMALL_GATHER_SCRATCH,
        compiler_params=pltpu.CompilerParams(vmem_limit_bytes=VMEM_LIMIT),
    )(*halves, vector_block)
    return out[:NW], out[NW]


def _reduce_scatter(local, received, place, vector_block):
    halves = _device_sums([local[sp.name] for sp in SHARDED], [received[sp.name] for sp in SHARDED], place)
    return _pair_share(halves, vector_block)


def _place_bf16(sp, w, place):
    nd = len(sp.full_shape)

    def body(p_ref, w_ref, o_ref):
        o_ref[...] = w_ref[...].astype(BF16)

    return pl.pallas_call(
        body, name="place_" + sp.name,
        grid_spec=pltpu.PrefetchScalarGridSpec(
            num_scalar_prefetch=1, grid=(1,),
            in_specs=[pl.BlockSpec(sp.shard_shape, lambda i, p_ref: (0,) * nd)],
            out_specs=pl.BlockSpec(sp.shard_shape,
                                   lambda i, p_ref: tuple(p_ref[0] if a == sp.shard_axis else 0 for a in range(nd)))),
        out_shape=jax.ShapeDtypeStruct(sp.full_shape, BF16),
        compiler_params=_params(("arbitrary",)),
    )(place, w)


def _matmul_f32(a, b, name):
    def body(a_ref, b_ref, o_ref):
        o_ref[...] = jnp.dot(a_ref[...], b_ref[...], preferred_element_type=F32, precision=lax.Precision.HIGHEST)

    return pl.pallas_call(body, name=name, out_shape=jax.ShapeDtypeStruct((a.shape[0], b.shape[1]), F32),
                          compiler_params=pltpu.CompilerParams(vmem_limit_bytes=VMEM_LIMIT))(a, b)


def _sum_devices(stacked):
    def body(x_ref, o_ref):
        acc = x_ref[0]
        for d in range(1, NDEV):
            acc = acc + x_ref[d]
        o_ref[...] = acc

    return pl.pallas_call(body, name="sum_devices", out_shape=jax.ShapeDtypeStruct(stacked.shape[1:], F32),
                          compiler_params=pltpu.CompilerParams(vmem_limit_bytes=VMEM_LIMIT))(stacked)


ADAMW_STEPS = 8


def _adamw(ws, gs, ms, vs, name):
    n = len(ws)
    steps = ADAMW_STEPS if all(w.shape[0] % (8 * ADAMW_STEPS) == 0 for w in ws) else 1

    def body(*refs):
        ins, outs = refs[:4 * n], refs[4 * n:]
        for k in range(n):
            w_ref, g_ref, m_ref, v_ref = ins[k], ins[n + k], ins[2 * n + k], ins[3 * n + k]
            gv = g_ref[...]
            nm = ADAM_B1 * m_ref[...] + (1.0 - ADAM_B1) * gv
            nv = ADAM_B2 * v_ref[...] + (1.0 - ADAM_B2) * (gv * gv)
            m_hat = nm / (1.0 - ADAM_B1 ** ADAM_STEP)
            v_hat = nv / (1.0 - ADAM_B2 ** ADAM_STEP)
            outs[k][...] = -ADAM_LR * (m_hat / (jnp.sqrt(v_hat) + ADAM_EPS) + ADAM_WD * w_ref[...])
            outs[n + k][...] = nm
            outs[2 * n + k][...] = nv

    blks = [pl.BlockSpec((w.shape[0] // steps, w.shape[1]), lambda i: (i, 0)) for w in ws]
    shapes = [jax.ShapeDtypeStruct(w.shape, F32) for w in ws]
    out = pl.pallas_call(
        body, name="adamw_" + name, grid=(steps,), in_specs=blks * 4, out_specs=blks * 3, out_shape=shapes * 3,
        compiler_params=_params(("parallel",)),
    )(*ws, *gs, *ms, *vs)
    return out[:n], out[n:2 * n], out[2 * n:]


WEIGHT_NAMES = ("g_pre_mix", "g_post_mix", "g_pre_ffn", "g_post_ffn", "w_ada", "b_ada", "w_in", "w_pool",
                "pool_scale", "conv_w", "conv_b", "w_bout", "w_o", "w_up", "ffn_conv_w", "ffn_conv_b", "w_down")
MATRIX_NAMES = ("w_ada",) + tuple(sp.name for sp in SHARDED)
VECTOR_NAMES = tuple(n for n in WEIGHT_NAMES if n not in MATRIX_NAMES)

CW = D // NCHIP
FCW = F2 // NCHIP
ADA_W = DIN // NCHIP
COND_BLOCK = (8, 768)
GRAD_BLOCK = (8, 4864)


def _flat_pad(parts, shape):
    flat = jnp.concatenate([p.reshape(-1) for p in parts])
    return jnp.pad(flat, (0, shape[0] * shape[1] - flat.shape[0])).reshape(shape)


def _take(flat, offset, shape):
    size = 1
    for n in shape:
        size *= n
    return flat[offset:offset + size].reshape(shape), offset + size


def kernel(x, c, g_pre_mix, g_post_mix, g_pre_ffn, g_post_ffn, w_ada, b_ada, w_in, w_pool, pool_scale, conv_w, conv_b, w_bout, w_o, w_up, ffn_conv_w, ffn_conv_b, w_down, loss_target, m_g_pre_mix, m_g_post_mix, m_g_pre_ffn, m_g_post_ffn, m_w_ada, m_b_ada, m_w_in, m_w_pool, m_pool_scale, m_conv_w, m_conv_b, m_w_bout, m_w_o, m_w_up, m_ffn_conv_w, m_ffn_conv_b, m_w_down, v_g_pre_mix, v_g_post_mix, v_g_pre_ffn, v_g_post_ffn, v_w_ada, v_b_ada, v_w_in, v_w_pool, v_pool_scale, v_conv_w, v_conv_b, v_w_bout, v_w_o, v_w_up, v_ffn_conv_w, v_ffn_conv_b, v_w_down):
    weights = dict(g_pre_mix=g_pre_mix, g_post_mix=g_post_mix, g_pre_ffn=g_pre_ffn, g_post_ffn=g_post_ffn,
                   w_ada=w_ada, b_ada=b_ada, w_in=w_in, w_pool=w_pool, pool_scale=pool_scale, conv_w=conv_w,
                   conv_b=conv_b, w_bout=w_bout, w_o=w_o, w_up=w_up, ffn_conv_w=ffn_conv_w, ffn_conv_b=ffn_conv_b,
                   w_down=w_down)
    mom1 = dict(g_pre_mix=m_g_pre_mix, g_post_mix=m_g_post_mix, g_pre_ffn=m_g_pre_ffn, g_post_ffn=m_g_post_ffn,
                w_ada=m_w_ada, b_ada=m_b_ada, w_in=m_w_in, w_pool=m_w_pool, pool_scale=m_pool_scale,
                conv_w=m_conv_w, conv_b=m_conv_b, w_bout=m_w_bout, w_o=m_w_o, w_up=m_w_up,
                ffn_conv_w=m_ffn_conv_w, ffn_conv_b=m_ffn_conv_b, w_down=m_w_down)
    mom2 = dict(g_pre_mix=v_g_pre_mix, g_post_mix=v_g_post_mix, g_pre_ffn=v_g_pre_ffn, g_post_ffn=v_g_post_ffn,
                w_ada=v_w_ada, b_ada=v_b_ada, w_in=v_w_in, w_pool=v_w_pool, pool_scale=v_pool_scale,
                conv_w=v_conv_w, conv_b=v_conv_b, w_bout=v_w_bout, w_o=v_w_o, w_up=v_w_up,
                ffn_conv_w=v_ffn_conv_w, ffn_conv_b=v_ffn_conv_b, w_down=v_w_down)

    chip = 2 * lax.axis_index("x") + lax.axis_index("y")
    core = lax.axis_index("c")
    dev = 2 * chip + core
    place = jnp.stack([chip, core, dev]).astype(jnp.int32)

    cond = _all_gather_small(_flat_pad([c, conv_w, ffn_conv_w], COND_BLOCK), "gather_cond")
    cond = cond.reshape(NDEV, -1)
    c_all = cond[:, :D]
    by_chip = cond[0::2]
    conv_w_full = by_chip[:, D:D + 3 * CW].reshape(NCHIP, 3, CW).transpose(1, 0, 2).reshape(3, D)
    ffn_w_full = by_chip[:, D + 3 * CW:D + 3 * CW + 3 * FCW].reshape(NCHIP, 3, FCW).transpose(1, 0, 2).reshape(3, F2)

    mod_cols = _all_gather_small(_matmul_f32(c_all, w_ada[0], "ada_mod"), "gather_mod")
    mod_cols = mod_cols.reshape(NDEV, NDEV, ADA_W)[0::2]
    mod = lax.dynamic_index_in_dim(mod_cols, dev, axis=1, keepdims=False).reshape(6, D) + b_ada.reshape(6, D)
    vec_d = jnp.concatenate([mod, g_pre_mix, g_post_mix, g_pre_ffn, g_post_ffn, pool_scale, conv_b, conv_w_full,
                             jnp.zeros((VD_ROWS - 15, D), F32)], axis=0)
    vec_f = jnp.concatenate([ffn_w_full, ffn_conv_b, jnp.zeros((FV_ROWS - 4, F2), F32)], axis=0)

    placed = [_place_bf16(sp, weights[sp.name][0], place) for sp in SHARDED]
    loss_blk, dx, vecs, local, received = _local_step(x[0], loss_target[0], vec_d, vec_f, placed, place)

    dmod = [vecs[n] for n in ("dsh1", "dsc1", "dgt1", "dsh2", "dsc2", "dgt2")]
    small = [vecs["dg_pre_mix"], vecs["dg_post_mix"], vecs["dg_pre_ffn"], vecs["dg_post_ffn"]] + dmod + [
        vecs["dpool_scale"], vecs["dconv_w"], vecs["dconv_b"], vecs["dffn_conv_w"], vecs["dffn_conv_b"],
        loss_blk[0]]
    reduced, gathered = _reduce_scatter(local, received, place, _flat_pad(small, GRAD_BLOCK))
    total = _sum_devices(gathered.reshape((NDEV,) + GRAD_BLOCK)).reshape(-1)
    vgrad = {}
    off = 0
    for n in ("g_pre_mix", "g_post_mix", "g_pre_ffn", "g_post_ffn"):
        vgrad[n], off = _take(total, off, (1, D))
    dmod_off = off
    vgrad["b_ada"], off = _take(total, off, (1, DIN))
    vgrad["pool_scale"], off = _take(total, off, (1, D))
    g_conv_w, off = _take(total, off, (3, D))
    vgrad["conv_w"] = lax.dynamic_slice_in_dim(g_conv_w, chip * CW, CW, axis=1)[None]
    vgrad["conv_b"], off = _take(total, off, (1, D))
    g_ffn_w, off = _take(total, off, (3, F2))
    vgrad["ffn_conv_w"] = lax.dynamic_slice_in_dim(g_ffn_w, chip * FCW, FCW, axis=1)[None]
    vgrad["ffn_conv_b"], off = _take(total, off, (1, F2))
    loss = total[off]

    dmod_all = gathered.reshape(NDEV, -1)[:, dmod_off:dmod_off + DIN]
    dmod_cols = lax.dynamic_slice_in_dim(dmod_all, chip * ADA_W, ADA_W, axis=1)
    g_ada = _matmul_f32(jnp.pad(c_all.T, ((0, 0), (0, 128 - NDEV))), jnp.pad(dmod_cols, ((0, 128 - NDEV), (0, 0))),
                        "ada_wgrad")

    mgrad = {"w_ada": g_ada}
    for sp, g in zip(SHARDED, reduced):
        mgrad[sp.name] = g

    grad, delta, new_m, new_v = {}, {}, {}, {}
    two_d = lambda tree: [tree[n].reshape(-1, weights[n].shape[-1]) for n in MATRIX_NAMES]
    ds, nms, nvs = _adamw(two_d(weights), two_d(mgrad), two_d(mom1), two_d(mom2), "matrices")
    for n, d, nm, nv in zip(MATRIX_NAMES, ds, nms, nvs):
        shape = weights[n].shape
        grad[n], delta[n], new_m[n], new_v[n] = (a.reshape(shape) for a in (mgrad[n], d, nm, nv))
    flat = lambda tree: [jnp.concatenate([tree[n].reshape(1, -1) for n in VECTOR_NAMES], axis=1)]
    (d,), (nm,), (nv,) = _adamw(flat(weights), flat(vgrad), flat(mom1), flat(mom2), "vectors")
    off = 0
    for n in VECTOR_NAMES:
        shape = weights[n].shape
        grad[n] = vgrad[n].reshape(shape)
        delta[n], _ = _take(d[0], off, shape)
        new_m[n], _ = _take(nm[0], off, shape)
        new_v[n], off = _take(nv[0], off, shape)

    return (loss, dx[None], *[grad[n] for n in WEIGHT_NAMES], *[delta[n] for n in WEIGHT_NAMES],
            *[new_m[n] for n in WEIGHT_NAMES], *[new_v[n] for n in WEIGHT_NAMES])
```

```python
import jax
import jax.numpy as jnp
from jax import lax
from jax.experimental import pallas as pl
from jax.experimental.pallas import tpu as pltpu

F32 = jnp.float32
BF16 = jnp.bfloat16

D = 1024
DIN = 6 * D
F = 2816
F2 = 2 * F
NG = 4
GW = D // NG
POOL_CARRY = 16
CONV_CARRY = 3
EPS = 1e-6
NCHIP = 4
NDEV = 8

ADAM_LR = 0.001
ADAM_B1 = 0.9
ADAM_B2 = 0.999
ADAM_EPS = 1e-08
ADAM_WD = 0.01
ADAM_STEP = 10

VMEM_LIMIT = 60 * 1024 * 1024

(V_SH1, V_SC1, V_GT1, V_SH2, V_SC2, V_GT2, V_GPRE1, V_GPOST1, V_GPRE2, V_GPOST2,
 V_PSCALE, V_CB, V_CW0, V_CW1, V_CW2) = range(15)
VD_ROWS = 16
FV_W0, FV_W1, FV_W2, FV_B = range(4)
FV_ROWS = 8

MESH = pl.DeviceIdType.MESH


def _params(sem=None, vmem=VMEM_LIMIT):
    return pltpu.CompilerParams(dimension_semantics=sem, vmem_limit_bytes=vmem)


def _row(ref, r):
    return ref[r:r + 1, :]


def _load_once(pairs, sem):
    @pl.when(pl.program_id(0) == 0)
    def _():
        copies = [pltpu.make_async_copy(src, dst, sem.at[n]) for n, (src, dst) in enumerate(pairs)]
        for cp in copies:
            cp.start()
        for cp in copies:
            cp.wait()


def _dot(a, b):
    return jnp.dot(a, b, preferred_element_type=F32)


def _dot_nt(a, b):
    return lax.dot_general(a, b, (((1,), (1,)), ((), ())), preferred_element_type=F32)


BLK = 256
SEG = BLK // 8


def _load_rows(ref, ts):
    blocks = [jnp.swapaxes(ref[b * BLK:(b + 1) * BLK, :].reshape(8, SEG, ref.shape[-1]), 0, 1).reshape(BLK, -1)
              for b in range(ts // BLK)]
    return jnp.concatenate(blocks, axis=0)


def _store_rows(ref, val, ts):
    for b in range(ts // BLK):
        blk = val[b * BLK:(b + 1) * BLK, :].reshape(SEG, 8, val.shape[-1])
        ref[b * BLK:(b + 1) * BLK, :] = jnp.swapaxes(blk, 0, 1).reshape(BLK, -1)


def _times(t0):
    p = lax.broadcasted_iota(jnp.int32, (BLK, 1), 0)
    return t0 + (p & 7) * SEG + (p >> 3)


def _before(x, carry, s):
    x3 = x.reshape(SEG, 8, x.shape[-1])
    tail = pltpu.roll(x3[SEG - s:], 1, 1)
    row = lax.broadcasted_iota(jnp.int32, tail.shape, 1)
    out = jnp.concatenate([jnp.where(row == 0, carry, tail), x3[:SEG - s]], axis=0)
    return out.reshape(x.shape), tail


def _after(x, carry, s):
    x3 = x.reshape(SEG, 8, x.shape[-1])
    head = pltpu.roll(x3[:s], 7, 1)
    row = lax.broadcasted_iota(jnp.int32, head.shape, 1)
    out = jnp.concatenate([x3[s:], jnp.where(row == 7, carry, head)], axis=0)
    return out.reshape(x.shape), head


def _causal_conv(x, carry, cols, w0, w1, w2, b):
    x1, carry[0:1, :, cols] = _before(x, carry[0:1, :, cols], 1)
    x2, carry[1:3, :, cols] = _before(x, carry[1:3, :, cols], 2)
    return b + w2 * x + w1 * x1 + w0 * x2


def _causal_conv_bwd(dy, carry, cols, w0, w1, w2):
    d1, carry[0:1, :, cols] = _after(dy, carry[0:1, :, cols], 1)
    d2, carry[1:3, :, cols] = _after(dy, carry[1:3, :, cols], 2)
    return w2 * dy + w1 * d1 + w0 * d2, d1, d2


def _pool_counts(t0, g):
    return jnp.minimum((_times(t0) + 1).astype(F32), float(2 << g))


def _rms(x):
    return lax.rsqrt(jnp.mean(x * x, axis=-1, keepdims=True) + EPS)


def _rms_bwd(dn, n, r):
    return r * (dn - n * jnp.mean(dn * n, axis=-1, keepdims=True))


def _colsum(x):
    return jnp.sum(x, axis=0, keepdims=True)


def _gelu_and_grad(x):
    k, a = 0.7978845608028654, 0.044715
    x2 = x * x
    th1 = 1.0 + jnp.tanh(x * (x2 * (k * a) + k))
    hx = 0.5 * x
    gelu = hx * th1
    dgelu = 0.5 * th1 + (hx * (th1 * (2.0 - th1))) * (x2 * (3.0 * k * a) + k)
    return gelu, dgelu


def _fwd_proj(x, vec_d, placed_in, placed_rest, place, ts):
    s = x.shape[0]
    nt = s // ts
    cw = DIN // NCHIP
    sp_in = SHARDED[0]
    gather = _WeightGather(SHARDED[1:4])
    n = gather.n

    def body(*refs):
        p_ref, x_ref, v_ref = refs[:3]
        proj_ref, h1_ref, w_full = refs[4 + n:7 + n]
        rest = refs[7 + n:7 + 2 * n]
        w_vmem, h1_all, sem, in_send, in_recv, send_sems, recv_sems = refs[7 + 2 * n:]
        j, i = pl.program_id(0), pl.program_id(1)
        x_, y_, c, k_me, _, _ = _mesh_place()
        sibling = (x_, y_, 1 - c)

        def peer(t):
            return (x_ ^ (t >> 1), y_ ^ (t & 1))

        def w_in_sends():
            mine = sp_in.piece(w_full, k_me, c)
            return [_remote(mine, mine, in_send.at[t - 1], in_recv.at[t - 1], (*peer(t), c)) for t in (1, 2, 3)]

        def load_block(k):
            cp = pltpu.make_async_copy(sp_in.shard(w_full, k), w_vmem.at[k], sem.at[0])
            cp.start()
            cp.wait()

        @pl.when((j == 0) & (i == 0))
        def _():
            for cp in w_in_sends()[:2]:
                cp.start()
            load_block(k_me)

        @pl.when((j == 1) & (i == 0))
        def _():
            for cp in w_in_sends()[:2]:
                cp.wait_send()
            w_in_sends()[2].start()
            gather.start(rest, send_sems, recv_sems)

        for t in (1, 2, 3):
            @pl.when((j == t) & (i == 0))
            def _(t=t):
                k = k_me ^ t
                landed = sp_in.piece(w_full, k, c)
                _remote(landed, landed, in_send.at[t - 1], in_recv.at[t - 1], (*peer(t), c)).wait_recv()
                _remote(landed, landed, in_send.at[2 + t], in_recv.at[2 + t], sibling).start()
                other = sp_in.piece(w_full, k, 1 - c)
                _remote(other, other, in_send.at[2 + t], in_recv.at[2 + t], sibling).wait_recv()
                load_block(k)

        @pl.when(j == 0)
        def _():
            xv = _load_rows(x_ref, ts)
            n1 = xv * _rms(xv)
            h = n1 * (_row(v_ref, V_GPRE1) * (1.0 + _row(v_ref, V_SC1))) + _row(v_ref, V_SH1)
            hb = h.astype(BF16)
            h1_ref[...] = hb
            h1_all[i] = hb

        proj_ref[...] = _dot(h1_all[i], w_vmem[k_me ^ j]).astype(BF16)

        @pl.when((j == NCHIP - 1) & (i == nt - 1))
        def _():
            w_in_sends()[2].wait_send()
            for t in (1, 2, 3):
                landed = sp_in.piece(w_full, k_me ^ t, c)
                _remote(landed, landed, in_send.at[2 + t], in_recv.at[2 + t], sibling).wait_send()
            gather.finish(rest, send_sems, recv_sems)

    once = lambda w: pl.BlockSpec((ts, w), lambda j, i, p: (jnp.where(j == 0, i, nt - 1), 0))
    return pl.pallas_call(
        body, name="fwd_proj",
        grid_spec=pltpu.PrefetchScalarGridSpec(
            num_scalar_prefetch=1, grid=(NCHIP, nt),
            in_specs=[once(D), pl.BlockSpec((VD_ROWS, D), lambda j, i, p: (0, 0)),
                      pl.BlockSpec(memory_space=pl.ANY)] + gather.specs_any,
            out_specs=[pl.BlockSpec((ts, cw), lambda j, i, p: (i, p[0] ^ j)), once(D),
                       pl.BlockSpec(memory_space=pl.ANY)] + gather.specs_any,
            scratch_shapes=[pltpu.VMEM((NCHIP, D, cw), BF16), pltpu.VMEM((nt, ts, D), BF16),
                            pltpu.SemaphoreType.DMA((1,)),
                            pltpu.SemaphoreType.DMA((6,)), pltpu.SemaphoreType.DMA((6,))] + gather.scratch),
        out_shape=[jax.ShapeDtypeStruct((s, DIN), BF16), jax.ShapeDtypeStruct((s, D), BF16),
                   jax.ShapeDtypeStruct(sp_in.full_shape, BF16)] + gather.out_shape,
        input_output_aliases={3 + w: 2 + w for w in range(n + 1)},
        compiler_params=_params(("arbitrary", "arbitrary")),
    )(place, x, vec_d, placed_in, *placed_rest)


def _fwd_mix(proj, x, vec_d, w_pool, w_bout, w_o, placed_ffn, ts):
    s = x.shape[0]
    gather = _WeightGather(SHARDED[4:])
    n = gather.n

    def body(*refs):
        ins, outs, rest = refs[:6], refs[6 + n:14 + n], refs[14 + n:14 + 2 * n]
        scratch, sems = refs[14 + 2 * n:-2], refs[-2:]
        i = pl.program_id(0)
        nt = s // ts
        pl.when(i == 0)(lambda: gather.start(rest, *sems))
        pl.when(i == nt - 1 - nt // 16)(lambda: gather.forward(rest, *sems))
        compute(*ins, *outs, *scratch)
        pl.when(i == nt - 1)(lambda: gather.drain(rest, *sems))

    def compute(p_ref, x_ref, v_ref, wp_hbm, wb_hbm, wo_hbm,
                x1_ref, o_ref, pg_ref, q_ref, mg_ref, ya_ref, yb_ref, cv_ref,
                wp, wb, wo, carry_p, carry_v, sem):
        i = pl.program_id(0)
        _load_once([(wp_hbm, wp), (wb_hbm, wb), (wo_hbm, wo)], sem)

        @pl.when(i == 0)
        def _():
            carry_p[...] = jnp.zeros_like(carry_p)
            carry_v[...] = jnp.zeros_like(carry_v)

        t0 = i * ts
        for g in range(NG):
            cols = slice(g * GW, (g + 1) * GW)
            u = p_ref[:, cols].astype(F32)
            e = u
            for l in range(g + 1):
                slot = slice((1 << l) - 1, (2 << l) - 1)
                shifted, carry_p[slot, :, cols] = _before(e, carry_p[slot, :, cols], 1 << l)
                e = e + shifted
            pgb = (e / _pool_counts(t0, g) - u).astype(BF16)
            pg_ref[:, cols] = pgb
            ya_ref[:, cols] = _dot(pgb, wp[g]).astype(BF16)

        u_x = p_ref[:, D:2 * D].astype(F32)
        u_c = p_ref[:, 3 * D:4 * D].astype(F32)
        v = u_c * u_x
        cv = _causal_conv(v, carry_v, slice(None), _row(v_ref, V_CW0), _row(v_ref, V_CW1),
                          _row(v_ref, V_CW2), _row(v_ref, V_CB))
        cv_ref[...] = cv.astype(BF16)
        q = (p_ref[:, 2 * D:3 * D].astype(F32) * cv).astype(BF16)
        q_ref[...] = q
        y_b = _dot(q, wb[...])
        yb_ref[...] = y_b.astype(BF16)

        y_a = ya_ref[...].astype(F32) * _row(v_ref, V_PSCALE)
        merged = (jax.nn.sigmoid(p_ref[:, 4 * D:5 * D].astype(F32)) * y_a
                  + jax.nn.sigmoid(p_ref[:, 5 * D:6 * D].astype(F32)) * y_b).astype(BF16)
        mg_ref[...] = merged
        o = _dot(merged, wo[...])
        o_ref[...] = o.astype(BF16)
        x1_ref[...] = _load_rows(x_ref, ts) + _row(v_ref, V_GT1) * ((o * _rms(o)) * _row(v_ref, V_GPOST1))

    tile = lambda w: pl.BlockSpec((ts, w), lambda i: (i, 0))
    hbm = pl.BlockSpec(memory_space=pl.ANY)
    return pl.pallas_call(
        body, name="fwd_mix", grid=(s // ts,),
        in_specs=[tile(DIN), tile(D), pl.BlockSpec((VD_ROWS, D), lambda i: (0, 0)), hbm, hbm, hbm] + gather.specs_any,
        out_specs=[tile(D)] * 8 + gather.specs_any,
        out_shape=[jax.ShapeDtypeStruct((s, D), F32)] + [jax.ShapeDtypeStruct((s, D), BF16)] * 7 + gather.out_shape,
        input_output_aliases={6 + w: 8 + w for w in range(n)},
        scratch_shapes=[pltpu.VMEM((NG, GW, GW), BF16), pltpu.VMEM((D, D), BF16), pltpu.VMEM((D, D), BF16),
                        pltpu.VMEM((POOL_CARRY, 8, D), F32), pltpu.VMEM((CONV_CARRY, 8, D), F32),
                        pltpu.SemaphoreType.DMA((3,))] + gather.scratch,
        compiler_params=_params(("arbitrary",)),
    )(proj, x, vec_d, w_pool, w_bout, w_o, *placed_ffn)


def _fwd_ffn(x1, tgt, vec_d, vec_f, w_up, w_down, ts):
    s = x1.shape[0]

    def body(x1_ref, t_ref, v_ref, f_ref, wu_hbm, wd_hbm,
             up_ref, upc_ref, a_ref, h2_ref, dx2_ref, dff_ref, vo_ref, loss_ref,
             wu, wd, carry, sem):
        i = pl.program_id(0)
        _load_once([(wu_hbm, wu), (wd_hbm, wd)], sem)

        @pl.when(i == 0)
        def _():
            carry[...] = jnp.zeros_like(carry)
            vo_ref[...] = jnp.zeros_like(vo_ref)
            loss_ref[...] = jnp.zeros_like(loss_ref)

        x1v = x1_ref[...]
        n3 = x1v * _rms(x1v)
        h2 = (n3 * (_row(v_ref, V_GPRE2) * (1.0 + _row(v_ref, V_SC2))) + _row(v_ref, V_SH2)).astype(BF16)
        h2_ref[...] = h2

        ff = jnp.zeros((ts, D), F32)
        for lo, hi in FFN_SLABS_FWD:
            up = []
            for cols in (slice(lo, hi), slice(F + lo, F + hi)):
                u0 = _dot(h2, wu[:, cols])
                up_ref[:, cols] = u0.astype(BF16)
                y = _causal_conv(u0, carry, cols, f_ref[FV_W0:FV_W0 + 1, cols], f_ref[FV_W1:FV_W1 + 1, cols],
                                 f_ref[FV_W2:FV_W2 + 1, cols], f_ref[FV_B:FV_B + 1, cols])
                upc_ref[:, cols] = y.astype(BF16)
                up.append(y)
            gelu, _ = _gelu_and_grad(up[0])
            a = (gelu * up[1]).astype(BF16)
            a_ref[:, lo:hi] = a
            ff = ff + _dot(a, wd[lo:hi, :])

        r4 = _rms(ff)
        n4 = ff * r4
        gt2 = _row(v_ref, V_GT2)
        gpost = _row(v_ref, V_GPOST2)
        gate_gain = gt2 * gpost
        diff = (x1v + gate_gain * n4) - _load_rows(t_ref, ts)
        loss_ref[...] += jnp.full(loss_ref.shape, 0.5 / D * jnp.sum(diff * diff), F32)
        dx2_ref[...] = diff * (1.0 / D)
        s1 = _colsum(diff * n4)
        vo_ref[0:1, :] += s1 * (gpost * (1.0 / D))
        vo_ref[1:2, :] += s1 * (gt2 * (1.0 / D))
        dff_ref[...] = _rms_bwd(diff * (gate_gain * (1.0 / D)), n4, r4).astype(BF16)

    tile = lambda w: pl.BlockSpec((ts, w), lambda i: (i, 0))
    full = lambda r, w: pl.BlockSpec((r, w), lambda i: (0, 0))
    hbm = pl.BlockSpec(memory_space=pl.ANY)
    return pl.pallas_call(
        body, name="fwd_ffn", grid=(s // ts,),
        in_specs=[tile(D), tile(D), full(VD_ROWS, D), full(FV_ROWS, F2), hbm, hbm],
        out_specs=[tile(F2), tile(F2), tile(F), tile(D), tile(D), tile(D), full(8, D), full(8, 128)],
        out_shape=[jax.ShapeDtypeStruct((s, F2), BF16), jax.ShapeDtypeStruct((s, F2), BF16),
                   jax.ShapeDtypeStruct((s, F), BF16),
                   jax.ShapeDtypeStruct((s, D), BF16), jax.ShapeDtypeStruct((s, D), F32),
                   jax.ShapeDtypeStruct((s, D), BF16), jax.ShapeDtypeStruct((8, D), F32),
                   jax.ShapeDtypeStruct((8, 128), F32)],
        scratch_shapes=[pltpu.VMEM((D, F2), BF16), pltpu.VMEM((F, D), BF16), pltpu.VMEM((CONV_CARRY, 8, F2), F32),
                        pltpu.SemaphoreType.DMA((2,))],
        compiler_params=_params(("arbitrary",)),
    )(x1, tgt, vec_d, vec_f, w_up, w_down)


def _bwd_ffn(dff, dx2, x1, up0, upc, vec_d, vec_f, w_up, w_down, exchange, ex_grads, ts):
    s = x1.shape[0]
    nt = s // ts
    n = exchange.n

    def body(*refs):
        ins, grads = refs[:9], refs[9:9 + n]
        outs, recvs = refs[9 + n:13 + n], refs[13 + n:13 + 2 * n]
        scratch, sems = refs[13 + 2 * n:-2], refs[-2:]
        i = pl.program_id(0)
        pl.when(i == 0)(lambda: exchange.start(grads, recvs, *sems))
        compute(*ins, *outs, *scratch)
        pl.when(i == nt - 1)(lambda: exchange.finish(grads, recvs, *sems))

    def compute(dff_ref, dx2_ref, x1_ref, up_ref, upc_ref, v_ref, f_ref, wu_hbm, wd_hbm,
                dx1_ref, dup_ref, vo_ref, fo_ref, wu, wd, carry, sem):
        i = pl.program_id(0)
        _load_once([(wu_hbm, wu), (wd_hbm, wd)], sem)

        @pl.when(i == 0)
        def _():
            carry[...] = jnp.zeros_like(carry)
            vo_ref[...] = jnp.zeros_like(vo_ref)
            fo_ref[...] = jnp.zeros_like(fo_ref)

        dffb = dff_ref[...]

        dh2 = jnp.zeros((ts, D), F32)
        for lo, hi in FFN_SLABS_BWD:
            slabs = (slice(lo, hi), slice(F + lo, F + hi))
            gelu, dgelu = _gelu_and_grad(upc_ref[:, slabs[0]].astype(F32))
            da = _dot_nt(dffb, wd[lo:hi, :])
            dups = (da * upc_ref[:, slabs[1]].astype(F32) * dgelu, da * gelu)
            for cols, dup in zip(slabs, dups):
                du0, d1, d2 = _causal_conv_bwd(dup, carry, cols, f_ref[FV_W0:FV_W0 + 1, cols],
                                               f_ref[FV_W1:FV_W1 + 1, cols], f_ref[FV_W2:FV_W2 + 1, cols])
                u0 = up_ref[:, cols].astype(F32)
                fo_ref[FV_B:FV_B + 1, cols] += _colsum(dup)
                fo_ref[FV_W2:FV_W2 + 1, cols] += _colsum(dup * u0)
                fo_ref[FV_W1:FV_W1 + 1, cols] += _colsum(d1 * u0)
                fo_ref[FV_W0:FV_W0 + 1, cols] += _colsum(d2 * u0)
                du0 = du0.astype(BF16)
                dup_ref[:, cols] = du0
                dh2 = dh2 + _dot_nt(du0, wu[:, cols])

        x1v = x1_ref[...]
        r3 = _rms(x1v)
        n3 = x1v * r3
        gpre = _row(v_ref, V_GPRE2)
        sc = 1.0 + _row(v_ref, V_SC2)
        vo_ref[0:1, :] += _colsum(dh2)
        s2 = _colsum(dh2 * n3)
        vo_ref[1:2, :] += s2 * gpre
        vo_ref[2:3, :] += s2 * sc
        dx1_ref[...] = dx2_ref[...] + _rms_bwd(dh2 * (gpre * sc), n3, r3)

    rev = lambda w: pl.BlockSpec((ts, w), lambda i: (nt - 1 - i, 0))
    full = lambda r, w: pl.BlockSpec((r, w), lambda i: (0, 0))
    hbm = pl.BlockSpec(memory_space=pl.ANY)
    return pl.pallas_call(
        body, name="bwd_ffn", grid=(nt,),
        in_specs=[rev(D), rev(D), rev(D), rev(F2), rev(F2), full(VD_ROWS, D), full(FV_ROWS, F2), hbm, hbm]
        + exchange.specs_any,
        out_specs=[rev(D), rev(F2), full(8, D), full(FV_ROWS, F2)] + exchange.specs_any,
        out_shape=[jax.ShapeDtypeStruct((s, D), F32), jax.ShapeDtypeStruct((s, F2), BF16),
                   jax.ShapeDtypeStruct((8, D), F32), jax.ShapeDtypeStruct((FV_ROWS, F2), F32)] + exchange.out_shape,
        scratch_shapes=[pltpu.VMEM((D, F2), BF16), pltpu.VMEM((F, D), BF16), pltpu.VMEM((CONV_CARRY, 8, F2), F32),
                        pltpu.SemaphoreType.DMA((2,))] + exchange.scratch,
        compiler_params=_params(("arbitrary",)),
    )(dff, dx2, x1, up0, upc, vec_d, vec_f, w_up, w_down, *ex_grads)


def _bwd_mix(dx1, o, proj, cv, ya0, yb, merged, q, pg, vec_d, w_pool, w_bout, w_o, exchange, ex_grads, ts):
    s = dx1.shape[0]
    nt = s // ts
    n = exchange.n

    def body(*refs):
        ins, grads = refs[:13], refs[13:13 + n]
        outs, recvs = refs[13 + n:18 + n], refs[18 + n:18 + 2 * n]
        scratch, sems = refs[18 + 2 * n:-2], refs[-2:]
        i = pl.program_id(0)
        pl.when(i == 0)(lambda: exchange.start(grads, recvs, *sems))
        compute(*ins, *outs, *scratch)
        pl.when(i == nt - 1)(lambda: exchange.finish(grads, recvs, *sems))

    def compute(dx1_ref, o_ref, p_ref, cv_ref, ya_ref, yb_ref, mg_ref, q_ref, pg_ref, v_ref, wp_hbm, wb_hbm, wo_hbm,
                dp_ref, vo_ref, go_ref, gb_ref, gp_ref, wp, wb, wo, carry_d, carry_e, acc_o, acc_b, acc_p, sem):
        i = pl.program_id(0)
        _load_once([(wp_hbm, wp), (wb_hbm, wb), (wo_hbm, wo)], sem)

        @pl.when(i == 0)
        def _():
            carry_d[...] = jnp.zeros_like(carry_d)
            carry_e[...] = jnp.zeros_like(carry_e)
            vo_ref[...] = jnp.zeros_like(vo_ref)
            acc_o[...] = jnp.zeros_like(acc_o)
            acc_b[...] = jnp.zeros_like(acc_b)
            acc_p[...] = jnp.zeros_like(acc_p)

        t0 = (nt - 1 - i) * ts
        dx1v = dx1_ref[...]
        ov = o_ref[...].astype(F32)
        r2 = _rms(ov)
        n2 = ov * r2
        gpost = _row(v_ref, V_GPOST1)
        gt1 = _row(v_ref, V_GT1)
        s1 = _colsum(dx1v * n2)
        vo_ref[0:1, :] += s1 * gpost
        vo_ref[1:2, :] += s1 * gt1
        dob = _rms_bwd(dx1v * (gt1 * gpost), n2, r2).astype(BF16)
        acc_o[...] += _dot_tn(mg_ref[...], dob)
        dmerged = _dot_nt(dob, wo[...])

        ya0 = ya_ref[...].astype(F32)
        pscale = _row(v_ref, V_PSCALE)
        sa = jax.nn.sigmoid(p_ref[:, 4 * D:5 * D].astype(F32))
        dp_ref[:, 4 * D:5 * D] = (dmerged * (ya0 * pscale) * sa * (1.0 - sa)).astype(BF16)
        dy_a = dmerged * sa
        vo_ref[2:3, :] += _colsum(dy_a * ya0)
        dya0 = (dy_a * pscale).astype(BF16)

        sb = jax.nn.sigmoid(p_ref[:, 5 * D:6 * D].astype(F32))
        dp_ref[:, 5 * D:6 * D] = (dmerged * yb_ref[...].astype(F32) * sb * (1.0 - sb)).astype(BF16)
        dy_b = (dmerged * sb).astype(BF16)
        acc_b[...] += _dot_tn(q_ref[...], dy_b)
        dq = _dot_nt(dy_b, wb[...])

        u_x = p_ref[:, D:2 * D].astype(F32)
        u_b = p_ref[:, 2 * D:3 * D].astype(F32)
        u_c = p_ref[:, 3 * D:4 * D].astype(F32)
        w0, w1, w2 = _row(v_ref, V_CW0), _row(v_ref, V_CW1), _row(v_ref, V_CW2)
        dp_ref[:, 2 * D:3 * D] = (dq * cv_ref[...].astype(F32)).astype(BF16)
        dcv = dq * u_b
        dv, d1, d2 = _causal_conv_bwd(dcv, carry_d, slice(None), w0, w1, w2)
        v = u_c * u_x
        vo_ref[3:4, :] += _colsum(dcv)
        vo_ref[4:5, :] += _colsum(d2 * v)
        vo_ref[5:6, :] += _colsum(d1 * v)
        vo_ref[6:7, :] += _colsum(dcv * v)
        dp_ref[:, D:2 * D] = (dv * u_c).astype(BF16)
        dp_ref[:, 3 * D:4 * D] = (dv * u_x).astype(BF16)

        for g in range(NG):
            cols = slice(g * GW, (g + 1) * GW)
            acc_p[g] += _dot_tn(pg_ref[:, cols], dya0[:, cols])
            dpg = _dot_nt(dya0[:, cols], wp[g])
            e = dpg / _pool_counts(t0, g)
            for l in range(g + 1):
                slot = slice((1 << l) - 1, (2 << l) - 1)
                shifted, carry_e[slot, :, cols] = _after(e, carry_e[slot, :, cols], 1 << l)
                e = e + shifted
            dp_ref[:, cols] = (e - dpg).astype(BF16)

        @pl.when(i == nt - 1)
        def _():
            go_ref[...] = acc_o[...].astype(BF16)
            gb_ref[...] = acc_b[...].astype(BF16)
            gp_ref[...] = acc_p[...].astype(BF16)

    rev = lambda w: pl.BlockSpec((ts, w), lambda i: (nt - 1 - i, 0))
    hbm = pl.BlockSpec(memory_space=pl.ANY)
    whole = lambda shape: pl.BlockSpec(shape, lambda i: (0,) * len(shape))
    return pl.pallas_call(
        body, name="bwd_mix", grid=(nt,),
        in_specs=[rev(D), rev(D), rev(DIN)] + [rev(D)] * 6 + [whole((VD_ROWS, D)), hbm, hbm, hbm] + exchange.specs_any,
        out_specs=[rev(DIN), whole((8, D)), whole((D, D)), whole((D, D)), whole((NG, GW, GW))] + exchange.specs_any,
        out_shape=[jax.ShapeDtypeStruct((s, DIN), BF16), jax.ShapeDtypeStruct((8, D), F32),
                   jax.ShapeDtypeStruct((D, D), BF16), jax.ShapeDtypeStruct((D, D), BF16),
                   jax.ShapeDtypeStruct((NG, GW, GW), BF16)] + exchange.out_shape,
        scratch_shapes=[pltpu.VMEM((NG, GW, GW), BF16), pltpu.VMEM((D, D), BF16), pltpu.VMEM((D, D), BF16),
                        pltpu.VMEM((CONV_CARRY, 8, D), F32), pltpu.VMEM((POOL_CARRY, 8, D), F32),
                        pltpu.VMEM((D, D), F32), pltpu.VMEM((D, D), F32), pltpu.VMEM((NG, GW, GW), F32),
                        pltpu.SemaphoreType.DMA((3,))] + exchange.scratch,
        compiler_params=_params(("arbitrary",)),
    )(dx1, o, proj, cv, ya0, yb, merged, q, pg, vec_d, w_pool, w_bout, w_o, *ex_grads)


def _bwd_in(dproj, dx1, x, vec_d, w_in, exchange, ex_grads, ts):
    s = x.shape[0]
    nt = s // ts
    n = exchange.n

    def body(*refs):
        ins, grads = refs[:5], refs[5:5 + n]
        outs, recvs = refs[5 + n:7 + n], refs[7 + n:7 + 2 * n]
        scratch, sems = refs[7 + 2 * n:-2], refs[-2:]
        i = pl.program_id(0)
        pl.when(i == 0)(lambda: exchange.start(grads, recvs, *sems))
        compute(*ins, *outs, *scratch)
        pl.when(i == nt - 1)(lambda: exchange.finish(grads, recvs, *sems))

    def compute(dp_ref, dx1_ref, x_ref, v_ref, w_hbm, dx_ref, vo_ref, w_vmem, sem):
        _load_once([(w_hbm, w_vmem)], sem)

        @pl.when(pl.program_id(0) == 0)
        def _():
            vo_ref[...] = jnp.zeros_like(vo_ref)

        dh1 = _dot_nt(dp_ref[...], w_vmem[...])
        xv = _load_rows(x_ref, ts)
        r1 = _rms(xv)
        n1 = xv * r1
        gpre = _row(v_ref, V_GPRE1)
        sc = 1.0 + _row(v_ref, V_SC1)
        vo_ref[0:1, :] += _colsum(dh1)
        s1 = _colsum(dh1 * n1)
        vo_ref[1:2, :] += s1 * gpre
        vo_ref[2:3, :] += s1 * sc
        _store_rows(dx_ref, dx1_ref[...] + _rms_bwd(dh1 * (gpre * sc), n1, r1), ts)

    tile = lambda w: pl.BlockSpec((ts, w), lambda i: (i, 0))
    return pl.pallas_call(
        body, name="bwd_in", grid=(s // ts,),
        in_specs=[tile(DIN), tile(D), tile(D), pl.BlockSpec((VD_ROWS, D), lambda i: (0, 0)),
                  pl.BlockSpec(memory_space=pl.ANY)] + exchange.specs_any,
        out_specs=[tile(D), pl.BlockSpec((8, D), lambda i: (0, 0))] + exchange.specs_any,
        out_shape=[jax.ShapeDtypeStruct((s, D), F32), jax.ShapeDtypeStruct((8, D), F32)] + exchange.out_shape,
        scratch_shapes=[pltpu.VMEM((D, DIN), BF16), pltpu.SemaphoreType.DMA((1,))] + exchange.scratch,
        compiler_params=_params(("arbitrary",)),
    )(dproj, dx1, x, vec_d, w_in, *ex_grads)


def _dot_tn(a, b):
    return lax.dot_general(a, b, (((0,), (0,)), ((), ())), preferred_element_type=F32)


def _wgrad(a, b, tm, tn, ts, name, dtype, exchange=None, ex_grads=()):
    s, m = a.shape
    nn = b.shape[1]
    grid = (m // tm, nn // tn, s // ts)
    n = exchange.n if exchange else 0

    def body(*refs):
        a_ref, b_ref = refs[:2]
        grads = refs[2:2 + n]
        o_ref = refs[2 + n]
        recvs = refs[3 + n:3 + 2 * n]
        acc = refs[3 + 2 * n]
        sems = refs[4 + 2 * n:]
        i, j, k = pl.program_id(0), pl.program_id(1), pl.program_id(2)
        if exchange:
            pl.when((i == 0) & (j == 0) & (k == 0))(lambda: exchange.start(grads, recvs, *sems))
        part = _dot_tn(a_ref[...], b_ref[...])

        @pl.when(k == 0)
        def _():
            acc[...] = part

        @pl.when(k > 0)
        def _():
            acc[...] += part

        @pl.when(k == grid[2] - 1)
        def _():
            o_ref[...] = acc[...].astype(dtype)

        if exchange:
            pl.when((i == grid[0] - 1) & (j == grid[1] - 1) & (k == grid[2] - 1))(
                lambda: exchange.finish(grads, recvs, *sems))

    hosted = exchange.specs_any if exchange else []
    return pl.pallas_call(
        body, name=name, grid=grid,
        in_specs=[pl.BlockSpec((ts, tm), lambda i, j, k: (k, i)), pl.BlockSpec((ts, tn), lambda i, j, k: (k, j))]
        + hosted,
        out_specs=[pl.BlockSpec((tm, tn), lambda i, j, k: (i, j))] + hosted,
        out_shape=[jax.ShapeDtypeStruct((m, nn), dtype)] + (exchange.out_shape if exchange else []),
        scratch_shapes=[pltpu.VMEM((tm, tn), F32)] + (exchange.scratch if exchange else []),
        compiler_params=_params(("arbitrary", "arbitrary", "arbitrary")),
    )(a, b, *ex_grads)


FFN_SLABS_FWD = ((0, 1024), (1024, 2816))
FFN_SLABS_BWD = ((0, 1536), (1536, 2816))
TS_PROJ = 512
TS_MIX = 256
TS_FFN = 256
TS_WGRAD = 2048


def _local_step(x, tgt, vec_d, vec_f, placed, place):
    s = x.shape[0]
    tw = min(TS_WGRAD, s)
    sp_in, sp_pool, sp_bout, sp_o, sp_up, sp_down = SHARDED
    proj, h1, w_in, w_pool, w_bout, w_o = _fwd_proj(x, vec_d, placed[0], placed[1:4], place, min(2 * TS_PROJ, s))
    x1, o, pg, q, merged, ya0, yb, cv, w_up, w_down = _fwd_mix(proj, x, vec_d, w_pool, w_bout, w_o, placed[4:],
                                                               min(TS_MIX, s))
    up0, upc, a, h2, dx2, dff, vo_f, loss = _fwd_ffn(x1, tgt, vec_d, vec_f, w_up, w_down, min(TS_FFN, s))
    g_down, = _wgrad(a, dff, F // 2, D, tw, "wgrad_down", BF16)
    dx1, dup0, vo_b, fo, r_down = _bwd_ffn(dff, dx2, x1, up0, upc, vec_d, vec_f, w_up, w_down,
                                           _GradExchange([sp_down]), [g_down], min(TS_FFN, s))
    g_up, = _wgrad(h2, dup0, D, F2 // NCHIP, tw, "wgrad_up", BF16)
    dproj, vo_m, g_o, g_bout, g_pool, r_up = _bwd_mix(dx1, o, proj, cv, ya0, yb, merged, q, pg, vec_d,
                                                      w_pool, w_bout, w_o, _GradExchange([sp_up]), [g_up],
                                                      min(TS_MIX, s))
    g_in, r_pool, r_bout, r_o = _wgrad(h1, dproj, D, DIN // NCHIP, tw, "wgrad_in", BF16,
                                       _GradExchange([sp_pool, sp_bout, sp_o]), [g_pool, g_bout, g_o])
    dx, vo_i, r_in = _bwd_in(dproj, dx1, x, vec_d, w_in, _GradExchange([sp_in]), [g_in], min(TS_PROJ, s))
    vecs = dict(
        dsh1=vo_i[0], dsc1=vo_i[1], dg_pre_mix=vo_i[2],
        dgt1=vo_m[0], dg_post_mix=vo_m[1], dpool_scale=vo_m[2], dconv_b=vo_m[3],
        dconv_w=vo_m[4:7],
        dsh2=vo_b[0], dsc2=vo_b[1], dg_pre_ffn=vo_b[2],
        dgt2=vo_f[0], dg_post_ffn=vo_f[1],
        dffn_conv_w=fo[FV_W0:FV_W2 + 1], dffn_conv_b=fo[FV_B],
    )
    local = dict(w_in=g_in, w_pool=g_pool, w_bout=g_bout, w_o=g_o, w_up=g_up, w_down=g_down)
    received = dict(w_in=r_in, w_pool=r_pool, w_bout=r_bout, w_o=r_o, w_up=r_up, w_down=r_down)
    return loss, dx, vecs, local, received


def _aligned(offset, n):
    return offset if isinstance(offset, int) else pl.multiple_of(offset, n)


class _Sharded:
    def __init__(self, name, full_shape, shard_axis, half_axis):
        self.name = name
        self.full_shape = full_shape
        self.shard_axis = shard_axis
        self.half_axis = half_axis
        self.shard_shape = tuple(n // NCHIP if a == shard_axis else n for a, n in enumerate(full_shape))
        self.piece_shape = tuple(n // 2 if a == half_axis else n for a, n in enumerate(self.shard_shape))

    def piece(self, full_ref, k, h):
        idx = []
        for a, n in enumerate(self.piece_shape):
            if a == self.shard_axis and a == self.half_axis:
                idx.append(pl.ds(_aligned((2 * k + h) * n, n), n))
            elif a == self.shard_axis:
                idx.append(pl.ds(_aligned(k * n, n), n))
            elif a == self.half_axis:
                idx.append(pl.ds(_aligned(h * n, n), n))
            else:
                idx.append(slice(None))
        return full_ref.at[tuple(idx)]

    def shard(self, full_ref, k):
        n = self.shard_shape[self.shard_axis]
        idx = [pl.ds(_aligned(k * n, n), n) if a == self.shard_axis else slice(None)
               for a in range(len(self.full_shape))]
        return full_ref.at[tuple(idx)]

    def half(self, shard_ref, h):
        n = self.piece_shape[self.half_axis]
        idx = [pl.ds(_aligned(h * n, n), n) if a == self.half_axis else slice(None)
               for a in range(len(self.full_shape))]
        return shard_ref.at[tuple(idx)]

SHARDED = (
    _Sharded("w_in", (D, DIN), 1, 0),
    _Sharded("w_pool", (NG, GW, GW), 1, 0),
    _Sharded("w_bout", (D, D), 0, 0),
    _Sharded("w_o", (D, D), 0, 0),
    _Sharded("w_up", (D, F2), 1, 0),
    _Sharded("w_down", (F, D), 0, 0),
)
NW = len(SHARDED)


def _mesh_place():
    x, y, c = lax.axis_index("x"), lax.axis_index("y"), lax.axis_index("c")
    chips = [(1 - x, y), (x, 1 - y), (1 - x, 1 - y)]
    return x, y, c, 2 * x + y, chips, [2 * px + py for px, py in chips]


def _remote(src, dst, send_sem, recv_sem, device):
    return pltpu.make_async_remote_copy(src_ref=src, dst_ref=dst, send_sem=send_sem, recv_sem=recv_sem,
                                        device_id=device, device_id_type=MESH)


SMALL_GATHER_SCRATCH = [pltpu.SemaphoreType.DMA((7,)), pltpu.SemaphoreType.DMA((7,)), pltpu.SemaphoreType.DMA]


def _small_gather(x_ref, out_ref, send_sems, recv_sems, local_sem):
    m_per = x_ref.shape[0]
    x, y, c, _, chips, _ = _mesh_place()
    me, sibling = (x, y, c), (x, y, 1 - c)

    def rows(px, py, pc):
        return out_ref.at[pl.ds((4 * px + 2 * py + pc) * m_per, m_per), :]

    def copy(k, blk, to, src=None):
        return _remote(rows(*blk) if src is None else src, rows(*blk), send_sems.at[k], recv_sems.at[k], to)

    mine = pltpu.make_async_copy(x_ref, rows(*me), local_sem)
    mine.start()
    first = [copy(0, me, sibling, src=x_ref)]
    first += [copy(1 + j, me, (*chip, c), src=x_ref) for j, chip in enumerate(chips)]
    for cp in first:
        cp.start()
    passed = [copy(4 + j, (*chip, c), sibling) for j, chip in enumerate(chips)]
    for j, chip in enumerate(chips):
        copy(1 + j, (*chip, c), me).wait_recv()
        passed[j].start()
    copy(0, sibling, me).wait_recv()
    for j, chip in enumerate(chips):
        copy(4 + j, (*chip, 1 - c), me).wait_recv()
    for cp in first + passed:
        cp.wait_send()
    mine.wait()


def _all_gather_small(block, name):
    m_per, n = block.shape
    return pl.pallas_call(
        _small_gather_body(), name=name,
        out_shape=jax.ShapeDtypeStruct((NDEV * m_per, n), block.dtype),
        in_specs=[pl.BlockSpec(memory_space=pltpu.VMEM)],
        out_specs=pl.BlockSpec(memory_space=pltpu.VMEM),
        scratch_shapes=SMALL_GATHER_SCRATCH,
        compiler_params=pltpu.CompilerParams(vmem_limit_bytes=VMEM_LIMIT),
    )(block)


def _small_gather_body():
    def body(x_ref, out_ref, send_sems, recv_sems, local_sem):
        _small_gather(x_ref, out_ref, send_sems, recv_sems, local_sem)
    return body


class _WeightGather:
    def __init__(self, specs):
        self.specs = specs
        self.n = len(specs)
        self.specs_any = [pl.BlockSpec(memory_space=pl.ANY)] * self.n
        self.out_shape = [jax.ShapeDtypeStruct(sp.full_shape, BF16) for sp in specs]
        self.scratch = [pltpu.SemaphoreType.DMA((6 * self.n,)), pltpu.SemaphoreType.DMA((6 * self.n,))]

    def _sends(self, outs, send_sems, recv_sems):
        x, y, c, k_me, chips, _ = _mesh_place()
        sends = []
        for j, chip in enumerate(chips):
            for w, sp in enumerate(self.specs):
                mine = sp.piece(outs[w], k_me, c)
                sends.append(_remote(mine, mine, send_sems.at[6 * w + j], recv_sems.at[6 * w + j], (*chip, c)))
        return sends

    def start(self, outs, send_sems, recv_sems):
        for cp in self._sends(outs, send_sems, recv_sems):
            cp.start()

    def _passes(self, outs, send_sems, recv_sems):
        x, y, c, _, chips, kidx = _mesh_place()
        return [_remote(sp.piece(outs[w], kidx[j], c), sp.piece(outs[w], kidx[j], c),
                        send_sems.at[6 * w + 3 + j], recv_sems.at[6 * w + 3 + j], (x, y, 1 - c))
                for j in range(3) for w, sp in enumerate(self.specs)]

    def forward(self, outs, send_sems, recv_sems):
        x, y, c, _, chips, kidx = _mesh_place()
        for j, chip in enumerate(chips):
            for w, sp in enumerate(self.specs):
                landed = sp.piece(outs[w], kidx[j], c)
                _remote(landed, landed, send_sems.at[6 * w + j], recv_sems.at[6 * w + j], (*chip, c)).wait_recv()
        for cp in self._passes(outs, send_sems, recv_sems):
            cp.start()

    def drain(self, outs, send_sems, recv_sems):
        x, y, c, _, chips, kidx = _mesh_place()
        for j in range(3):
            for w, sp in enumerate(self.specs):
                landed = sp.piece(outs[w], kidx[j], 1 - c)
                _remote(landed, landed, send_sems.at[6 * w + 3 + j], recv_sems.at[6 * w + 3 + j],
                        (x, y, 1 - c)).wait_recv()
        for cp in self._sends(outs, send_sems, recv_sems) + self._passes(outs, send_sems, recv_sems):
            cp.wait_send()

    def finish(self, outs, send_sems, recv_sems):
        self.forward(outs, send_sems, recv_sems)
        self.drain(outs, send_sems, recv_sems)


class _GradExchange:
    def __init__(self, specs):
        self.specs = specs
        self.n = len(specs)
        self.specs_any = [pl.BlockSpec(memory_space=pl.ANY)] * self.n
        self.out_shape = [jax.ShapeDtypeStruct((NDEV,) + sp.piece_shape, BF16) for sp in specs]
        self.scratch = [pltpu.SemaphoreType.DMA((7 * self.n,)), pltpu.SemaphoreType.DMA((NDEV * self.n,))]

    def _sends(self, grads, recvs, send_sems, recv_sems):
        x, y, c, k_me, chips, kidx = _mesh_place()
        dev = 2 * k_me + c
        sends = []
        for w, sp in enumerate(self.specs):
            slot, arrival = recvs[w].at[dev], recv_sems.at[NDEV * w + dev]
            sends.append(_remote(sp.piece(grads[w], k_me, 1 - c), slot, send_sems.at[7 * w], arrival, (x, y, 1 - c)))
            for j, chip in enumerate(chips):
                for h in range(2):
                    sends.append(_remote(sp.piece(grads[w], kidx[j], h), slot, send_sems.at[7 * w + 1 + 2 * j + h],
                                         arrival, (*chip, h)))
        return sends

    def start(self, grads, recvs, send_sems, recv_sems):
        for cp in self._sends(grads, recvs, send_sems, recv_sems):
            cp.start()

    def finish(self, grads, recvs, send_sems, recv_sems):
        x, y, c, k_me, _, _ = _mesh_place()
        dev = 2 * k_me + c
        for w in range(self.n):
            for d in range(NDEV):
                landed = recvs[w].at[d]
                arrival = _remote(landed, landed, send_sems.at[7 * w], recv_sems.at[NDEV * w + d], (x, y, c))
                pl.when(d != dev)(arrival.wait_recv)
        for cp in self._sends(grads, recvs, send_sems, recv_sems):
            cp.wait_send()


def _device_sums(locals_, recvs, place):
    def body(p_ref, *refs):
        a_refs, b_refs, o_refs = refs[:NW], refs[NW:2 * NW], refs[2 * NW:]
        d = pl.program_id(0)
        own = d == p_ref[2]
        terms = [jnp.where(own, a_ref[...], b_ref[...]).astype(F32) for a_ref, b_ref in zip(a_refs, b_refs)]

        @pl.when(d == 0)
        def _():
            for o_ref, term in zip(o_refs, terms):
                o_ref[...] = term

        @pl.when(d > 0)
        def _():
            for o_ref, term in zip(o_refs, terms):
                o_ref[...] += term

    def mine(sp):
        nd = len(sp.piece_shape)
        return pl.BlockSpec(sp.piece_shape, lambda d, p_ref: tuple(
            2 * p_ref[0] + p_ref[1] if a == sp.shard_axis == sp.half_axis else
            p_ref[0] if a == sp.shard_axis else p_ref[1] if a == sp.half_axis else 0 for a in range(nd)))

    def others(sp):
        nd = len(sp.piece_shape)
        return pl.BlockSpec((None,) + sp.piece_shape,
                            lambda d, p_ref: (jnp.where(d == p_ref[2], (d + 1) % NDEV, d),) + (0,) * nd)

    def half(sp):
        nd = len(sp.piece_shape)
        return pl.BlockSpec(sp.piece_shape,
                            lambda d, p_ref: tuple(p_ref[1] if a == sp.half_axis else 0 for a in range(nd)))

    return pl.pallas_call(
        body, name="rs_device_sums",
        grid_spec=pltpu.PrefetchScalarGridSpec(
            num_scalar_prefetch=1, grid=(NDEV,),
            in_specs=[mine(sp) for sp in SHARDED] + [others(sp) for sp in SHARDED],
            out_specs=[half(sp) for sp in SHARDED]),
        out_shape=[jax.ShapeDtypeStruct(sp.shard_shape, F32) for sp in SHARDED],
        compiler_params=_params(("arbitrary",)),
    )(place, *locals_, *recvs)


def _pair_share(halves, vector_block):
    m_per, n = vector_block.shape

    def body(*refs):
        x_ref = refs[NW]
        outs, gathered = refs[NW + 1:2 * NW + 1], refs[2 * NW + 1]
        send_sems, recv_sems = refs[2 * NW + 2:2 * NW + 4]
        x, y, c, _, _, _ = _mesh_place()
        sibling = (x, y, 1 - c)
        sent = []
        for w, sp in enumerate(SHARDED):
            mine = sp.half(outs[w], c)
            cp = _remote(mine, mine, send_sems.at[w], recv_sems.at[w], sibling)
            cp.start()
            sent.append(cp)
        _small_gather(x_ref, gathered, *refs[2 * NW + 4:])
        for w, sp in enumerate(SHARDED):
            landed = sp.half(outs[w], 1 - c)
            _remote(landed, landed, send_sems.at[w], recv_sems.at[w], sibling).wait_recv()
        for cp in sent:
            cp.wait_send()

    hbm = pl.BlockSpec(memory_space=pl.ANY)
    vmem = pl.BlockSpec(memory_space=pltpu.VMEM)
    out = pl.pallas_call(
        body, name="rs_pair_share",
        out_shape=[jax.ShapeDtypeStruct(sp.shard_shape, F32) for sp in SHARDED]
        + [jax.ShapeDtypeStruct((NDEV * m_per, n), F32)],
        in_specs=[hbm] * NW + [vmem], out_specs=[hbm] * NW + [vmem],
        input_output_aliases={w: w for w in range(NW)},
        scratch_shapes=[pltpu.SemaphoreType.DMA((NW,)), pltpu.SemaphoreType.DMA((NW,))] + SMALL_GATHER_SCRATCH,
        compiler_params=pltpu.CompilerParams(vmem_limit_bytes=VMEM_LIMIT),
    )(*halves, vector_block)
    return out[:NW], out[NW]


def _reduce_scatter(local, received, place, vector_block):
    halves = _device_sums([local[sp.name] for sp in SHARDED], [received[sp.name] for sp in SHARDED], place)
    return _pair_share(halves, vector_block)


def _place_bf16(sp, w, place):
    nd = len(sp.full_shape)

    def body(p_ref, w_ref, o_ref):
        o_ref[...] = w_ref[...].astype(BF16)

    return pl.pallas_call(
        body, name="place_" + sp.name,
        grid_spec=pltpu.PrefetchScalarGridSpec(
            num_scalar_prefetch=1, grid=(1,),
            in_specs=[pl.BlockSpec(sp.shard_shape, lambda i, p_ref: (0,) * nd)],
            out_specs=pl.BlockSpec(sp.shard_shape,
                                   lambda i, p_ref: tuple(p_ref[0] if a == sp.shard_axis else 0 for a in range(nd)))),
        out_shape=jax.ShapeDtypeStruct(sp.full_shape, BF16),
        compiler_params=_params(("arbitrary",)),
    )(place, w)


def _matmul_f32(a, b, name):
    def body(a_ref, b_ref, o_ref):
        o_ref[...] = jnp.dot(a_ref[...], b_ref[...], preferred_element_type=F32, precision=lax.Precision.HIGHEST)

    return pl.pallas_call(body, name=name, out_shape=jax.ShapeDtypeStruct((a.shape[0], b.shape[1]), F32),
                          compiler_params=pltpu.CompilerParams(vmem_limit_bytes=VMEM_LIMIT))(a, b)


def _sum_devices(stacked):
    def body(x_ref, o_ref):
        acc = x_ref[0]
        for d in range(1, NDEV):
            acc = acc + x_ref[d]
        o_ref[...] = acc

    return pl.pallas_call(body, name="sum_devices", out_shape=jax.ShapeDtypeStruct(stacked.shape[1:], F32),
                          compiler_params=pltpu.CompilerParams(vmem_limit_bytes=VMEM_LIMIT))(stacked)


ADAMW_STEPS = 8


def _adamw(ws, gs, ms, vs, name):
    n = len(ws)
    steps = ADAMW_STEPS if all(w.shape[0] % (8 * ADAMW_STEPS) == 0 for w in ws) else 1

    def body(*refs):
        ins, outs = refs[:4 * n], refs[4 * n:]
        for k in range(n):
            w_ref, g_ref, m_ref, v_ref = ins[k], ins[n + k], ins[2 * n + k], ins[3 * n + k]
            gv = g_ref[...]
            nm = ADAM_B1 * m_ref[...] + (1.0 - ADAM_B1) * gv
            nv = ADAM_B2 * v_ref[...] + (1.0 - ADAM_B2) * (gv * gv)
            m_hat = nm / (1.0 - ADAM_B1 ** ADAM_STEP)
            v_hat = nv / (1.0 - ADAM_B2 ** ADAM_STEP)
            outs[k][...] = -ADAM_LR * (m_hat / (jnp.sqrt(v_hat) + ADAM_EPS) + ADAM_WD * w_ref[...])
            outs[n + k][...] = nm
            outs[2 * n + k][...] = nv

    blks = [pl.BlockSpec((w.shape[0] // steps, w.shape[1]), lambda i: (i, 0)) for w in ws]
    shapes = [jax.ShapeDtypeStruct(w.shape, F32) for w in ws]
    out = pl.pallas_call(
        body, name="adamw_" + name, grid=(steps,), in_specs=blks * 4, out_specs=blks * 3, out_shape=shapes * 3,
        compiler_params=_params(("parallel",)),
    )(*ws, *gs, *ms, *vs)
    return out[:n], out[n:2 * n], out[2 * n:]


WEIGHT_NAMES = ("g_pre_mix", "g_post_mix", "g_pre_ffn", "g_post_ffn", "w_ada", "b_ada", "w_in", "w_pool",
                "pool_scale", "conv_w", "conv_b", "w_bout", "w_o", "w_up", "ffn_conv_w", "ffn_conv_b", "w_down")
MATRIX_NAMES = ("w_ada",) + tuple(sp.name for sp in SHARDED)
VECTOR_NAMES = tuple(n for n in WEIGHT_NAMES if n not in MATRIX_NAMES)

CW = D // NCHIP
FCW = F2 // NCHIP
ADA_W = DIN // NCHIP
COND_BLOCK = (8, 768)
GRAD_BLOCK = (8, 4864)


def _flat_pad(parts, shape):
    flat = jnp.concatenate([p.reshape(-1) for p in parts])
    return jnp.pad(flat, (0, shape[0] * shape[1] - flat.shape[0])).reshape(shape)


def _take(flat, offset, shape):
    size = 1
    for n in shape:
        size *= n
    return flat[offset:offset + size].reshape(shape), offset + size


def kernel(x, c, g_pre_mix, g_post_mix, g_pre_ffn, g_post_ffn, w_ada, b_ada, w_in, w_pool, pool_scale, conv_w, conv_b, w_bout, w_o, w_up, ffn_conv_w, ffn_conv_b, w_down, loss_target, m_g_pre_mix, m_g_post_mix, m_g_pre_ffn, m_g_post_ffn, m_w_ada, m_b_ada, m_w_in, m_w_pool, m_pool_scale, m_conv_w, m_conv_b, m_w_bout, m_w_o, m_w_up, m_ffn_conv_w, m_ffn_conv_b, m_w_down, v_g_pre_mix, v_g_post_mix, v_g_pre_ffn, v_g_post_ffn, v_w_ada, v_b_ada, v_w_in, v_w_pool, v_pool_scale, v_conv_w, v_conv_b, v_w_bout, v_w_o, v_w_up, v_ffn_conv_w, v_ffn_conv_b, v_w_down):
    weights = dict(g_pre_mix=g_pre_mix, g_post_mix=g_post_mix, g_pre_ffn=g_pre_ffn, g_post_ffn=g_post_ffn,
                   w_ada=w_ada, b_ada=b_ada, w_in=w_in, w_pool=w_pool, pool_scale=pool_scale, conv_w=conv_w,
                   conv_b=conv_b, w_bout=w_bout, w_o=w_o, w_up=w_up, ffn_conv_w=ffn_conv_w, ffn_conv_b=ffn_conv_b,
                   w_down=w_down)
    mom1 = dict(g_pre_mix=m_g_pre_mix, g_post_mix=m_g_post_mix, g_pre_ffn=m_g_pre_ffn, g_post_ffn=m_g_post_ffn,
                w_ada=m_w_ada, b_ada=m_b_ada, w_in=m_w_in, w_pool=m_w_pool, pool_scale=m_pool_scale,
                conv_w=m_conv_w, conv_b=m_conv_b, w_bout=m_w_bout, w_o=m_w_o, w_up=m_w_up,
                ffn_conv_w=m_ffn_conv_w, ffn_conv_b=m_ffn_conv_b, w_down=m_w_down)
    mom2 = dict(g_pre_mix=v_g_pre_mix, g_post_mix=v_g_post_mix, g_pre_ffn=v_g_pre_ffn, g_post_ffn=v_g_post_ffn,
                w_ada=v_w_ada, b_ada=v_b_ada, w_in=v_w_in, w_pool=v_w_pool, pool_scale=v_pool_scale,
                conv_w=v_conv_w, conv_b=v_conv_b, w_bout=v_w_bout, w_o=v_w_o, w_up=v_w_up,
                ffn_conv_w=v_ffn_conv_w, ffn_conv_b=v_ffn_conv_b, w_down=v_w_down)

    chip = 2 * lax.axis_index("x") + lax.axis_index("y")
    core = lax.axis_index("c")
    dev = 2 * chip + core
    place = jnp.stack([chip, core, dev]).astype(jnp.int32)

    cond = _all_gather_small(_flat_pad([c, conv_w, ffn_conv_w], COND_BLOCK), "gather_cond")
    cond = cond.reshape(NDEV, -1)
    c_all = cond[:, :D]
    by_chip = cond[0::2]
    conv_w_full = by_chip[:, D:D + 3 * CW].reshape(NCHIP, 3, CW).transpose(1, 0, 2).reshape(3, D)
    ffn_w_full = by_chip[:, D + 3 * CW:D + 3 * CW + 3 * FCW].reshape(NCHIP, 3, FCW).transpose(1, 0, 2).reshape(3, F2)

    mod_cols = _all_gather_small(_matmul_f32(c_all, w_ada[0], "ada_mod"), "gather_mod")
    mod_cols = mod_cols.reshape(NDEV, NDEV, ADA_W)[0::2]
    mod = lax.dynamic_index_in_dim(mod_cols, dev, axis=1, keepdims=False).reshape(6, D) + b_ada.reshape(6, D)
    vec_d = jnp.concatenate([mod, g_pre_mix, g_post_mix, g_pre_ffn, g_post_ffn, pool_scale, conv_b, conv_w_full,
                             jnp.zeros((VD_ROWS - 15, D), F32)], axis=0)
    vec_f = jnp.concatenate([ffn_w_full, ffn_conv_b, jnp.zeros((FV_ROWS - 4, F2), F32)], axis=0)

    placed = [_place_bf16(sp, weights[sp.name][0], place) for sp in SHARDED]
    loss_blk, dx, vecs, local, received = _local_step(x[0], loss_target[0], vec_d, vec_f, placed, place)

    dmod = [vecs[n] for n in ("dsh1", "dsc1", "dgt1", "dsh2", "dsc2", "dgt2")]
    small = [vecs["dg_pre_mix"], vecs["dg_post_mix"], vecs["dg_pre_ffn"], vecs["dg_post_ffn"]] + dmod + [
        vecs["dpool_scale"], vecs["dconv_w"], vecs["dconv_b"], vecs["dffn_conv_w"], vecs["dffn_conv_b"],
        loss_blk[0]]
    reduced, gathered = _reduce_scatter(local, received, place, _flat_pad(small, GRAD_BLOCK))
    total = _sum_devices(gathered.reshape((NDEV,) + GRAD_BLOCK)).reshape(-1)
    vgrad = {}
    off = 0
    for n in ("g_pre_mix", "g_post_mix", "g_pre_ffn", "g_post_ffn"):
        vgrad[n], off = _take(total, off, (1, D))
    dmod_off = off
    vgrad["b_ada"], off = _take(total, off, (1, DIN))
    vgrad["pool_scale"], off = _take(total, off, (1, D))
    g_conv_w, off = _take(total, off, (3, D))
    vgrad["conv_w"] = lax.dynamic_slice_in_dim(g_conv_w, chip * CW, CW, axis=1)[None]
    vgrad["conv_b"], off = _take(total, off, (1, D))
    g_ffn_w, off = _take(total, off, (3, F2))
    vgrad["ffn_conv_w"] = lax.dynamic_slice_in_dim(g_ffn_w, chip * FCW, FCW, axis=1)[None]
    vgrad["ffn_conv_b"], off = _take(total, off, (1, F2))
    loss = total[off]

    dmod_all = gathered.reshape(NDEV, -1)[:, dmod_off:dmod_off + DIN]
    dmod_cols = lax.dynamic_slice_in_dim(dmod_all, chip * ADA_W, ADA_W, axis=1)
    g_ada = _matmul_f32(jnp.pad(c_all.T, ((0, 0), (0, 128 - NDEV))), jnp.pad(dmod_cols, ((0, 128 - NDEV), (0, 0))),
                        "ada_wgrad")

    mgrad = {"w_ada": g_ada}
    for sp, g in zip(SHARDED, reduced):
        mgrad[sp.name] = g

    grad, delta, new_m, new_v = {}, {}, {}, {}
    two_d = lambda tree: [tree[n].reshape(-1, weights[n].shape[-1]) for n in MATRIX_NAMES]
    ds, nms, nvs = _adamw(two_d(weights), two_d(mgrad), two_d(mom1), two_d(mom2), "matrices")
    for n, d, nm, nv in zip(MATRIX_NAMES, ds, nms, nvs):
        shape = weights[n].shape
        grad[n], delta[n], new_m[n], new_v[n] = (a.reshape(shape) for a in (mgrad[n], d, nm, nv))
    flat = lambda tree: [jnp.concatenate([tree[n].reshape(1, -1) for n in VECTOR_NAMES], axis=1)]
    (d,), (nm,), (nv,) = _adamw(flat(weights), flat(vgrad), flat(mom1), flat(mom2), "vectors")
    off = 0
    for n in VECTOR_NAMES:
        shape = weights[n].shape
        grad[n] = vgrad[n].reshape(shape)
        delta[n], _ = _take(d[0], off, shape)
        new_m[n], _ = _take(nm[0], off, shape)
        new_v[n], off = _take(nv[0], off, shape)

    return (loss, dx[None], *[grad[n] for n in WEIGHT_NAMES], *[delta[n] for n in WEIGHT_NAMES],
            *[new_m[n] for n in WEIGHT_NAMES], *[new_v[n] for n in WEIGHT_NAMES])
```

```python
import jax
import jax.numpy as jnp
from jax import lax
from jax.experimental import pallas as pl
from jax.experimental.pallas import tpu as pltpu

F32 = jnp.float32
BF16 = jnp.bfloat16

D = 1024
DIN = 6 * D
F = 2816
F2 = 2 * F
NG = 4
GW = D // NG
POOL_CARRY = 16
CONV_CARRY = 3
EPS = 1e-6
NCHIP = 4
NDEV = 8

ADAM_LR = 0.001
ADAM_B1 = 0.9
ADAM_B2 = 0.999
ADAM_EPS = 1e-08
ADAM_WD = 0.01
ADAM_STEP = 10

VMEM_LIMIT = 60 * 1024 * 1024

(V_SH1, V_SC1, V_GT1, V_SH2, V_SC2, V_GT2, V_GPRE1, V_GPOST1, V_GPRE2, V_GPOST2,
 V_PSCALE, V_CB, V_CW0, V_CW1, V_CW2) = range(15)
VD_ROWS = 16
FV_W0, FV_W1, FV_W2, FV_B = range(4)
FV_ROWS = 8

MESH = pl.DeviceIdType.MESH


def _params(sem=None, vmem=VMEM_LIMIT):
    return pltpu.CompilerParams(dimension_semantics=sem, vmem_limit_bytes=vmem)


def _row(ref, r):
    return ref[r:r + 1, :]


def _load_once(pairs, sem):
    @pl.when(pl.program_id(0) == 0)
    def _():
        copies = [pltpu.make_async_copy(src, dst, sem.at[n]) for n, (src, dst) in enumerate(pairs)]
        for cp in copies:
            cp.start()
        for cp in copies:
            cp.wait()


def _dot(a, b):
    return jnp.dot(a, b, preferred_element_type=F32)


def _dot_nt(a, b):
    return lax.dot_general(a, b, (((1,), (1,)), ((), ())), preferred_element_type=F32)


BLK = 256
SEG = BLK // 8


def _load_rows(ref, ts):
    blocks = [jnp.swapaxes(ref[b * BLK:(b + 1) * BLK, :].reshape(8, SEG, ref.shape[-1]), 0, 1).reshape(BLK, -1)
              for b in range(ts // BLK)]
    return jnp.concatenate(blocks, axis=0)


def _store_rows(ref, val, ts):
    for b in range(ts // BLK):
        blk = val[b * BLK:(b + 1) * BLK, :].reshape(SEG, 8, val.shape[-1])
        ref[b * BLK:(b + 1) * BLK, :] = jnp.swapaxes(blk, 0, 1).reshape(BLK, -1)


def _times(t0):
    p = lax.broadcasted_iota(jnp.int32, (BLK, 1), 0)
    return t0 + (p & 7) * SEG + (p >> 3)


def _before(x, carry, s):
    x3 = x.reshape(SEG, 8, x.shape[-1])
    tail = pltpu.roll(x3[SEG - s:], 1, 1)
    row = lax.broadcasted_iota(jnp.int32, tail.shape, 1)
    out = jnp.concatenate([jnp.where(row == 0, carry, tail), x3[:SEG - s]], axis=0)
    return out.reshape(x.shape), tail


def _after(x, carry, s):
    x3 = x.reshape(SEG, 8, x.shape[-1])
    head = pltpu.roll(x3[:s], 7, 1)
    row = lax.broadcasted_iota(jnp.int32, head.shape, 1)
    out = jnp.concatenate([x3[s:], jnp.where(row == 7, carry, head)], axis=0)
    return out.reshape(x.shape), head


def _causal_conv(x, carry, cols, w0, w1, w2, b):
    x1, carry[0:1, :, cols] = _before(x, carry[0:1, :, cols], 1)
    x2, carry[1:3, :, cols] = _before(x, carry[1:3, :, cols], 2)
    return b + w2 * x + w1 * x1 + w0 * x2


def _causal_conv_bwd(dy, carry, cols, w0, w1, w2):
    d1, carry[0:1, :, cols] = _after(dy, carry[0:1, :, cols], 1)
    d2, carry[1:3, :, cols] = _after(dy, carry[1:3, :, cols], 2)
    return w2 * dy + w1 * d1 + w0 * d2, d1, d2


def _pool_counts(t0, g):
    return jnp.minimum((_times(t0) + 1).astype(F32), float(2 << g))


def _rms(x):
    return lax.rsqrt(jnp.mean(x * x, axis=-1, keepdims=True) + EPS)


def _rms_bwd(dn, n, r):
    return r * (dn - n * jnp.mean(dn * n, axis=-1, keepdims=True))


def _colsum(x):
    return jnp.sum(x, axis=0, keepdims=True)


def _gelu_and_grad(x):
    k, a = 0.7978845608028654, 0.044715
    x2 = x * x
    th1 = 1.0 + jnp.tanh(x * (x2 * (k * a) + k))
    hx = 0.5 * x
    gelu = hx * th1
    dgelu = 0.5 * th1 + (hx * (th1 * (2.0 - th1))) * (x2 * (3.0 * k * a) + k)
    return gelu, dgelu


def _fwd_proj(x, vec_d, placed_in, placed_rest, place, ts):
    s = x.shape[0]
    nt = s // ts
    cw = DIN // NCHIP
    sp_in = SHARDED[0]
    gather = _WeightGather(SHARDED[1:4])
    n = gather.n

    def body(*refs):
        p_ref, x_ref, v_ref = refs[:3]
        proj_ref, h1_ref, w_full = refs[4 + n:7 + n]
        rest = refs[7 + n:7 + 2 * n]
        w_vmem, h1_all, sem, in_send, in_recv, send_sems, recv_sems = refs[7 + 2 * n:]
        j, i = pl.program_id(0), pl.program_id(1)
        x_, y_, c, k_me, _, _ = _mesh_place()
        sibling = (x_, y_, 1 - c)

        def peer(t):
            return (x_ ^ (t >> 1), y_ ^ (t & 1))

        def w_in_sends():
            mine = sp_in.piece(w_full, k_me, c)
            return [_remote(mine, mine, in_send.at[t - 1], in_recv.at[t - 1], (*peer(t), c)) for t in (1, 2, 3)]

        def load_block(k):
            cp = pltpu.make_async_copy(sp_in.shard(w_full, k), w_vmem.at[k], sem.at[0])
            cp.start()
            cp.wait()

        @pl.when((j == 0) & (i == 0))
        def _():
            for cp in w_in_sends()[:2]:
                cp.start()
            load_block(k_me)

        @pl.when((j == 1) & (i == 0))
        def _():
            for cp in w_in_sends()[:2]:
                cp.wait_send()
            w_in_sends()[2].start()
            gather.start(rest, send_sems, recv_sems)

        for t in (1, 2, 3):
            @pl.when((j == t) & (i == 0))
            def _(t=t):
                k = k_me ^ t
                landed = sp_in.piece(w_full, k, c)
                _remote(landed, landed, in_send.at[t - 1], in_recv.at[t - 1], (*peer(t), c)).wait_recv()
                _remote(landed, landed, in_send.at[2 + t], in_recv.at[2 + t], sibling).start()
                other = sp_in.piece(w_full, k, 1 - c)
                _remote(other, other, in_send.at[2 + t], in_recv.at[2 + t], sibling).wait_recv()
                load_block(k)

        @pl.when(j == 0)
        def _():
            xv = _load_rows(x_ref, ts)
            n1 = xv * _rms(xv)
            h = n1 * (_row(v_ref, V_GPRE1) * (1.0 + _row(v_ref, V_SC1))) + _row(v_ref, V_SH1)
            hb = h.astype(BF16)
            h1_ref[...] = hb
            h1_all[i] = hb

        proj_ref[...] = _dot(h1_all[i], w_vmem[k_me ^ j]).astype(BF16)

        @pl.when((j == NCHIP - 1) & (i == nt - 1))
        def _():
            w_in_sends()[2].wait_send()
            for t in (1, 2, 3):
                landed = sp_in.piece(w_full, k_me ^ t, c)
                _remote(landed, landed, in_send.at[2 + t], in_recv.at[2 + t], sibling).wait_send()
            gather.finish(rest, send_sems, recv_sems)

    once = lambda w: pl.BlockSpec((ts, w), lambda j, i, p: (jnp.where(j == 0, i, nt - 1), 0))
    return pl.pallas_call(
        body, name="fwd_proj",
        grid_spec=pltpu.PrefetchScalarGridSpec(
            num_scalar_prefetch=1, grid=(NCHIP, nt),
            in_specs=[once(D), pl.BlockSpec((VD_ROWS, D), lambda j, i, p: (0, 0)),
                      pl.BlockSpec(memory_space=pl.ANY)] + gather.specs_any,
            out_specs=[pl.BlockSpec((ts, cw), lambda j, i, p: (i, p[0] ^ j)), once(D),
                       pl.BlockSpec(memory_space=pl.ANY)] + gather.specs_any,
            scratch_shapes=[pltpu.VMEM((NCHIP, D, cw), BF16), pltpu.VMEM((nt, ts, D), BF16),
                            pltpu.SemaphoreType.DMA((1,)),
                            pltpu.SemaphoreType.DMA((6,)), pltpu.SemaphoreType.DMA((6,))] + gather.scratch),
        out_shape=[jax.ShapeDtypeStruct((s, DIN), BF16), jax.ShapeDtypeStruct((s, D), BF16),
                   jax.ShapeDtypeStruct(sp_in.full_shape, BF16)] + gather.out_shape,
        input_output_aliases={3 + w: 2 + w for w in range(n + 1)},
        compiler_params=_params(("arbitrary", "arbitrary")),
    )(place, x, vec_d, placed_in, *placed_rest)


def _fwd_mix(proj, x, vec_d, w_pool, w_bout, w_o, placed_ffn, ts):
    s = x.shape[0]
    gather = _WeightGather(SHARDED[4:])
    n = gather.n

    def body(*refs):
        ins, outs, rest = refs[:6], refs[6 + n:14 + n], refs[14 + n:14 + 2 * n]
        scratch, (pbuf, psem), sems = refs[14 + 2 * n:-4], refs[-4:-2], refs[-2:]
        i = pl.program_id(0)
        nt = s // ts
        pl.when(i == 0)(lambda: gather.start(rest, *sems))
        pl.when(i == nt - 1 - nt // 16)(lambda: gather.forward(rest, *sems))

        def fetch(t):
            rows = pl.ds(pl.multiple_of(t * ts, ts), ts)
            return pltpu.make_async_copy(ins[0].at[rows, :], pbuf.at[t % 3], psem.at[t % 3])

        @pl.when(i == 0)
        def _():
            fetch(0).start()
            fetch(1).start()

        pl.when(i + 2 < nt)(lambda: fetch(i + 2).start())
        fetch(i).wait()
        compute(pbuf.at[i % 3], *ins[1:], *outs, *scratch)
        pl.when(i == nt - 1)(lambda: gather.drain(rest, *sems))

    def compute(p_ref, x_ref, v_ref, wp_hbm, wb_hbm, wo_hbm,
                x1_ref, o_ref, pg_ref, q_ref, mg_ref, ya_ref, yb_ref, cv_ref,
                wp, wb, wo, carry_p, carry_v, sem):
        i = pl.program_id(0)
        _load_once([(wp_hbm, wp), (wb_hbm, wb), (wo_hbm, wo)], sem)

        @pl.when(i == 0)
        def _():
            carry_p[...] = jnp.zeros_like(carry_p)
            carry_v[...] = jnp.zeros_like(carry_v)

        t0 = i * ts
        for g in range(NG):
            cols = slice(g * GW, (g + 1) * GW)
            u = p_ref[:, cols].astype(F32)
            e = u
            for l in range(g + 1):
                slot = slice((1 << l) - 1, (2 << l) - 1)
                shifted, carry_p[slot, :, cols] = _before(e, carry_p[slot, :, cols], 1 << l)
                e = e + shifted
            pgb = (e / _pool_counts(t0, g) - u).astype(BF16)
            pg_ref[:, cols] = pgb
            ya_ref[:, cols] = _dot(pgb, wp[g]).astype(BF16)

        u_x = p_ref[:, D:2 * D].astype(F32)
        u_c = p_ref[:, 3 * D:4 * D].astype(F32)
        v = u_c * u_x
        cv = _causal_conv(v, carry_v, slice(None), _row(v_ref, V_CW0), _row(v_ref, V_CW1),
                          _row(v_ref, V_CW2), _row(v_ref, V_CB))
        cv_ref[...] = cv.astype(BF16)
        q = (p_ref[:, 2 * D:3 * D].astype(F32) * cv).astype(BF16)
        q_ref[...] = q
        y_b = _dot(q, wb[...])
        yb_ref[...] = y_b.astype(BF16)

        y_a = ya_ref[...].astype(F32) * _row(v_ref, V_PSCALE)
        merged = (jax.nn.sigmoid(p_ref[:, 4 * D:5 * D].astype(F32)) * y_a
                  + jax.nn.sigmoid(p_ref[:, 5 * D:6 * D].astype(F32)) * y_b).astype(BF16)
        mg_ref[...] = merged
        o = _dot(merged, wo[...])
        o_ref[...] = o.astype(BF16)
        x1_ref[...] = _load_rows(x_ref, ts) + _row(v_ref, V_GT1) * ((o * _rms(o)) * _row(v_ref, V_GPOST1))

    tile = lambda w: pl.BlockSpec((ts, w), lambda i: (i, 0))
    hbm = pl.BlockSpec(memory_space=pl.ANY)
    return pl.pallas_call(
        body, name="fwd_mix", grid=(s // ts,),
        in_specs=[hbm, tile(D), pl.BlockSpec((VD_ROWS, D), lambda i: (0, 0)), hbm, hbm, hbm] + gather.specs_any,
        out_specs=[tile(D)] * 8 + gather.specs_any,
        out_shape=[jax.ShapeDtypeStruct((s, D), F32)] + [jax.ShapeDtypeStruct((s, D), BF16)] * 7 + gather.out_shape,
        input_output_aliases={6 + w: 8 + w for w in range(n)},
        scratch_shapes=[pltpu.VMEM((NG, GW, GW), BF16), pltpu.VMEM((D, D), BF16), pltpu.VMEM((D, D), BF16),
                        pltpu.VMEM((POOL_CARRY, 8, D), F32), pltpu.VMEM((CONV_CARRY, 8, D), F32),
                        pltpu.SemaphoreType.DMA((3,)), pltpu.VMEM((3, ts, DIN), BF16),
                        pltpu.SemaphoreType.DMA((3,))] + gather.scratch,
        compiler_params=_params(("arbitrary",)),
    )(proj, x, vec_d, w_pool, w_bout, w_o, *placed_ffn)


def _fwd_ffn(x1, tgt, vec_d, vec_f, w_up, w_down, ts):
    s = x1.shape[0]

    def body(x1_ref, t_ref, v_ref, f_ref, wu_hbm, wd_hbm,
             up_ref, upc_ref, a_ref, h2_ref, dx2_ref, dff_ref, vo_ref, loss_ref,
             wu, wd, carry, sem):
        i = pl.program_id(0)
        _load_once([(wu_hbm, wu), (wd_hbm, wd)], sem)

        @pl.when(i == 0)
        def _():
            carry[...] = jnp.zeros_like(carry)
            vo_ref[...] = jnp.zeros_like(vo_ref)
            loss_ref[...] = jnp.zeros_like(loss_ref)

        x1v = x1_ref[...]
        n3 = x1v * _rms(x1v)
        h2 = (n3 * (_row(v_ref, V_GPRE2) * (1.0 + _row(v_ref, V_SC2))) + _row(v_ref, V_SH2)).astype(BF16)
        h2_ref[...] = h2

        ff = jnp.zeros((ts, D), F32)
        for lo, hi in FFN_SLABS_FWD:
            up = []
            for cols in (slice(lo, hi), slice(F + lo, F + hi)):
                u0 = _dot(h2, wu[:, cols])
                up_ref[:, cols] = u0.astype(BF16)
                y = _causal_conv(u0, carry, cols, f_ref[FV_W0:FV_W0 + 1, cols], f_ref[FV_W1:FV_W1 + 1, cols],
                                 f_ref[FV_W2:FV_W2 + 1, cols], f_ref[FV_B:FV_B + 1, cols])
                upc_ref[:, cols] = y.astype(BF16)
                up.append(y)
            gelu, _ = _gelu_and_grad(up[0])
            a = (gelu * up[1]).astype(BF16)
            a_ref[:, lo:hi] = a
            ff = ff + _dot(a, wd[lo:hi, :])

        r4 = _rms(ff)
        n4 = ff * r4
        gt2 = _row(v_ref, V_GT2)
        gpost = _row(v_ref, V_GPOST2)
        gate_gain = gt2 * gpost
        diff = (x1v + gate_gain * n4) - _load_rows(t_ref, ts)
        loss_ref[...] += jnp.full(loss_ref.shape, 0.5 / D * jnp.sum(diff * diff), F32)
        dx2_ref[...] = diff * (1.0 / D)
        s1 = _colsum(diff * n4)
        vo_ref[0:1, :] += s1 * (gpost * (1.0 / D))
        vo_ref[1:2, :] += s1 * (gt2 * (1.0 / D))
        dff_ref[...] = _rms_bwd(diff * (gate_gain * (1.0 / D)), n4, r4).astype(BF16)

    tile = lambda w: pl.BlockSpec((ts, w), lambda i: (i, 0))
    full = lambda r, w: pl.BlockSpec((r, w), lambda i: (0, 0))
    hbm = pl.BlockSpec(memory_space=pl.ANY)
    return pl.pallas_call(
        body, name="fwd_ffn", grid=(s // ts,),
        in_specs=[tile(D), tile(D), full(VD_ROWS, D), full(FV_ROWS, F2), hbm, hbm],
        out_specs=[tile(F2), tile(F2), tile(F), tile(D), tile(D), tile(D), full(8, D), full(8, 128)],
        out_shape=[jax.ShapeDtypeStruct((s, F2), BF16), jax.ShapeDtypeStruct((s, F2), BF16),
                   jax.ShapeDtypeStruct((s, F), BF16),
                   jax.ShapeDtypeStruct((s, D), BF16), jax.ShapeDtypeStruct((s, D), F32),
                   jax.ShapeDtypeStruct((s, D), BF16), jax.ShapeDtypeStruct((8, D), F32),
                   jax.ShapeDtypeStruct((8, 128), F32)],
        scratch_shapes=[pltpu.VMEM((D, F2), BF16), pltpu.VMEM((F, D), BF16), pltpu.VMEM((CONV_CARRY, 8, F2), F32),
                        pltpu.SemaphoreType.DMA((2,))],
        compiler_params=_params(("arbitrary",)),
    )(x1, tgt, vec_d, vec_f, w_up, w_down)


def _bwd_ffn(dff, dx2, x1, up0, upc, vec_d, vec_f, w_up, w_down, exchange, ex_grads, ts):
    s = x1.shape[0]
    nt = s // ts
    n = exchange.n

    def body(*refs):
        ins, grads = refs[:9], refs[9:9 + n]
        outs, recvs = refs[9 + n:13 + n], refs[13 + n:13 + 2 * n]
        scratch, sems = refs[13 + 2 * n:-2], refs[-2:]
        i = pl.program_id(0)
        pl.when(i == 0)(lambda: exchange.start(grads, recvs, *sems))
        compute(*ins, *outs, *scratch)
        pl.when(i == nt - 1)(lambda: exchange.finish(grads, recvs, *sems))

    def compute(dff_ref, dx2_ref, x1_ref, up_ref, upc_ref, v_ref, f_ref, wu_hbm, wd_hbm,
                dx1_ref, dup_ref, vo_ref, fo_ref, wu, wd, carry, sem):
        i = pl.program_id(0)
        _load_once([(wu_hbm, wu), (wd_hbm, wd)], sem)

        @pl.when(i == 0)
        def _():
            carry[...] = jnp.zeros_like(carry)
            vo_ref[...] = jnp.zeros_like(vo_ref)
            fo_ref[...] = jnp.zeros_like(fo_ref)

        dffb = dff_ref[...]

        dh2 = jnp.zeros((ts, D), F32)
        for lo, hi in FFN_SLABS_BWD:
            slabs = (slice(lo, hi), slice(F + lo, F + hi))
            gelu, dgelu = _gelu_and_grad(upc_ref[:, slabs[0]].astype(F32))
            da = _dot_nt(dffb, wd[lo:hi, :])
            dups = (da * upc_ref[:, slabs[1]].astype(F32) * dgelu, da * gelu)
            for cols, dup in zip(slabs, dups):
                du0, d1, d2 = _causal_conv_bwd(dup, carry, cols, f_ref[FV_W0:FV_W0 + 1, cols],
                                               f_ref[FV_W1:FV_W1 + 1, cols], f_ref[FV_W2:FV_W2 + 1, cols])
                u0 = up_ref[:, cols].astype(F32)
                fo_ref[FV_B:FV_B + 1, cols] += _colsum(dup)
                fo_ref[FV_W2:FV_W2 + 1, cols] += _colsum(dup * u0)
                fo_ref[FV_W1:FV_W1 + 1, cols] += _colsum(d1 * u0)
                fo_ref[FV_W0:FV_W0 + 1, cols] += _colsum(d2 * u0)
                du0 = du0.astype(BF16)
                dup_ref[:, cols] = du0
                dh2 = dh2 + _dot_nt(du0, wu[:, cols])

        x1v = x1_ref[...]
        r3 = _rms(x1v)
        n3 = x1v * r3
        gpre = _row(v_ref, V_GPRE2)
        sc = 1.0 + _row(v_ref, V_SC2)
        vo_ref[0:1, :] += _colsum(dh2)
        s2 = _colsum(dh2 * n3)
        vo_ref[1:2, :] += s2 * gpre
        vo_ref[2:3, :] += s2 * sc
        dx1_ref[...] = dx2_ref[...] + _rms_bwd(dh2 * (gpre * sc), n3, r3)

    rev = lambda w: pl.BlockSpec((ts, w), lambda i: (nt - 1 - i, 0))
    full = lambda r, w: pl.BlockSpec((r, w), lambda i: (0, 0))
    hbm = pl.BlockSpec(memory_space=pl.ANY)
    return pl.pallas_call(
        body, name="bwd_ffn", grid=(nt,),
        in_specs=[rev(D), rev(D), rev(D), rev(F2), rev(F2), full(VD_ROWS, D), full(FV_ROWS, F2), hbm, hbm]
        + exchange.specs_any,
        out_specs=[rev(D), rev(F2), full(8, D), full(FV_ROWS, F2)] + exchange.specs_any,
        out_shape=[jax.ShapeDtypeStruct((s, D), F32), jax.ShapeDtypeStruct((s, F2), BF16),
                   jax.ShapeDtypeStruct((8, D), F32), jax.ShapeDtypeStruct((FV_ROWS, F2), F32)] + exchange.out_shape,
        scratch_shapes=[pltpu.VMEM((D, F2), BF16), pltpu.VMEM((F, D), BF16), pltpu.VMEM((CONV_CARRY, 8, F2), F32),
                        pltpu.SemaphoreType.DMA((2,))] + exchange.scratch,
        compiler_params=_params(("arbitrary",)),
    )(dff, dx2, x1, up0, upc, vec_d, vec_f, w_up, w_down, *ex_grads)


def _bwd_mix(dx1, o, proj, cv, ya0, yb, merged, q, pg, vec_d, w_pool, w_bout, w_o, exchange, ex_grads, ts):
    s = dx1.shape[0]
    nt = s // ts
    n = exchange.n

    def body(*refs):
        ins, grads = refs[:13], refs[13:13 + n]
        outs, recvs = refs[13 + n:18 + n], refs[18 + n:18 + 2 * n]
        scratch, sems = refs[18 + 2 * n:-2], refs[-2:]
        i = pl.program_id(0)
        pl.when(i == 0)(lambda: exchange.start(grads, recvs, *sems))
        compute(*ins, *outs, *scratch)
        pl.when(i == nt - 1)(lambda: exchange.finish(grads, recvs, *sems))

    def compute(dx1_ref, o_ref, p_ref, cv_ref, ya_ref, yb_ref, mg_ref, q_ref, pg_ref, v_ref, wp_hbm, wb_hbm, wo_hbm,
                dp_ref, vo_ref, go_ref, gb_ref, gp_ref, wp, wb, wo, carry_d, carry_e, acc_o, acc_b, acc_p, sem):
        i = pl.program_id(0)
        _load_once([(wp_hbm, wp), (wb_hbm, wb), (wo_hbm, wo)], sem)

        @pl.when(i == 0)
        def _():
            carry_d[...] = jnp.zeros_like(carry_d)
            carry_e[...] = jnp.zeros_like(carry_e)
            vo_ref[...] = jnp.zeros_like(vo_ref)
            acc_o[...] = jnp.zeros_like(acc_o)
            acc_b[...] = jnp.zeros_like(acc_b)
            acc_p[...] = jnp.zeros_like(acc_p)

        t0 = (nt - 1 - i) * ts
        dx1v = dx1_ref[...]
        ov = o_ref[...].astype(F32)
        r2 = _rms(ov)
        n2 = ov * r2
        gpost = _row(v_ref, V_GPOST1)
        gt1 = _row(v_ref, V_GT1)
        s1 = _colsum(dx1v * n2)
        vo_ref[0:1, :] += s1 * gpost
        vo_ref[1:2, :] += s1 * gt1
        dob = _rms_bwd(dx1v * (gt1 * gpost), n2, r2).astype(BF16)
        acc_o[...] += _dot_tn(mg_ref[...], dob)
        dmerged = _dot_nt(dob, wo[...])

        ya0 = ya_ref[...].astype(F32)
        pscale = _row(v_ref, V_PSCALE)
        sa = jax.nn.sigmoid(p_ref[:, 4 * D:5 * D].astype(F32))
        dp_ref[:, 4 * D:5 * D] = (dmerged * (ya0 * pscale) * sa * (1.0 - sa)).astype(BF16)
        dy_a = dmerged * sa
        vo_ref[2:3, :] += _colsum(dy_a * ya0)
        dya0 = (dy_a * pscale).astype(BF16)

        sb = jax.nn.sigmoid(p_ref[:, 5 * D:6 * D].astype(F32))
        dp_ref[:, 5 * D:6 * D] = (dmerged * yb_ref[...].astype(F32) * sb * (1.0 - sb)).astype(BF16)
        dy_b = (dmerged * sb).astype(BF16)
        acc_b[...] += _dot_tn(q_ref[...], dy_b)
        dq = _dot_nt(dy_b, wb[...])

        u_x = p_ref[:, D:2 * D].astype(F32)
        u_b = p_ref[:, 2 * D:3 * D].astype(F32)
        u_c = p_ref[:, 3 * D:4 * D].astype(F32)
        w0, w1, w2 = _row(v_ref, V_CW0), _row(v_ref, V_CW1), _row(v_ref, V_CW2)
        dp_ref[:, 2 * D:3 * D] = (dq * cv_ref[...].astype(F32)).astype(BF16)
        dcv = dq * u_b
        dv, d1, d2 = _causal_conv_bwd(dcv, carry_d, slice(None), w0, w1, w2)
        v = u_c * u_x
        vo_ref[3:4, :] += _colsum(dcv)
        vo_ref[4:5, :] += _colsum(d2 * v)
        vo_ref[5:6, :] += _colsum(d1 * v)
        vo_ref[6:7, :] += _colsum(dcv * v)
        dp_ref[:, D:2 * D] = (dv * u_c).astype(BF16)
        dp_ref[:, 3 * D:4 * D] = (dv * u_x).astype(BF16)

        for g in range(NG):
            cols = slice(g * GW, (g + 1) * GW)
            acc_p[g] += _dot_tn(pg_ref[:, cols], dya0[:, cols])
            dpg = _dot_nt(dya0[:, cols], wp[g])
            e = dpg / _pool_counts(t0, g)
            for l in range(g + 1):
                slot = slice((1 << l) - 1, (2 << l) - 1)
                shifted, carry_e[slot, :, cols] = _after(e, carry_e[slot, :, cols], 1 << l)
                e = e + shifted
            dp_ref[:, cols] = (e - dpg).astype(BF16)

        @pl.when(i == nt - 1)
        def _():
            go_ref[...] = acc_o[...].astype(BF16)
            gb_ref[...] = acc_b[...].astype(BF16)
            gp_ref[...] = acc_p[...].astype(BF16)

    rev = lambda w: pl.BlockSpec((ts, w), lambda i: (nt - 1 - i, 0))
    hbm = pl.BlockSpec(memory_space=pl.ANY)
    whole = lambda shape: pl.BlockSpec(shape, lambda i: (0,) * len(shape))
    return pl.pallas_call(
        body, name="bwd_mix", grid=(nt,),
        in_specs=[rev(D), rev(D), rev(DIN)] + [rev(D)] * 6 + [whole((VD_ROWS, D)), hbm, hbm, hbm] + exchange.specs_any,
        out_specs=[rev(DIN), whole((8, D)), whole((D, D)), whole((D, D)), whole((NG, GW, GW))] + exchange.specs_any,
        out_shape=[jax.ShapeDtypeStruct((s, DIN), BF16), jax.ShapeDtypeStruct((8, D), F32),
                   jax.ShapeDtypeStruct((D, D), BF16), jax.ShapeDtypeStruct((D, D), BF16),
                   jax.ShapeDtypeStruct((NG, GW, GW), BF16)] + exchange.out_shape,
        scratch_shapes=[pltpu.VMEM((NG, GW, GW), BF16), pltpu.VMEM((D, D), BF16), pltpu.VMEM((D, D), BF16),
                        pltpu.VMEM((CONV_CARRY, 8, D), F32), pltpu.VMEM((POOL_CARRY, 8, D), F32),
                        pltpu.VMEM((D, D), F32), pltpu.VMEM((D, D), F32), pltpu.VMEM((NG, GW, GW), F32),
                        pltpu.SemaphoreType.DMA((3,))] + exchange.scratch,
        compiler_params=_params(("arbitrary",)),
    )(dx1, o, proj, cv, ya0, yb, merged, q, pg, vec_d, w_pool, w_bout, w_o, *ex_grads)


def _bwd_in(dproj, dx1, x, vec_d, w_in, exchange, ex_grads, ts):
    s = x.shape[0]
    nt = s // ts
    n = exchange.n

    def body(*refs):
        ins, grads = refs[:5], refs[5:5 + n]
        outs, recvs = refs[5 + n:7 + n], refs[7 + n:7 + 2 * n]
        scratch, sems = refs[7 + 2 * n:-2], refs[-2:]
        i = pl.program_id(0)
        pl.when(i == 0)(lambda: exchange.start(grads, recvs, *sems))
        compute(*ins, *outs, *scratch)
        pl.when(i == nt - 1)(lambda: exchange.finish(grads, recvs, *sems))

    def compute(dp_ref, dx1_ref, x_ref, v_ref, w_hbm, dx_ref, vo_ref, w_vmem, sem):
        _load_once([(w_hbm, w_vmem)], sem)

        @pl.when(pl.program_id(0) == 0)
        def _():
            vo_ref[...] = jnp.zeros_like(vo_ref)

        dh1 = _dot_nt(dp_ref[...], w_vmem[...])
        xv = _load_rows(x_ref, ts)
        r1 = _rms(xv)
        n1 = xv * r1
        gpre = _row(v_ref, V_GPRE1)
        sc = 1.0 + _row(v_ref, V_SC1)
        vo_ref[0:1, :] += _colsum(dh1)
        s1 = _colsum(dh1 * n1)
        vo_ref[1:2, :] += s1 * gpre
        vo_ref[2:3, :] += s1 * sc
        _store_rows(dx_ref, dx1_ref[...] + _rms_bwd(dh1 * (gpre * sc), n1, r1), ts)

    tile = lambda w: pl.BlockSpec((ts, w), lambda i: (i, 0))
    return pl.pallas_call(
        body, name="bwd_in", grid=(s // ts,),
        in_specs=[tile(DIN), tile(D), tile(D), pl.BlockSpec((VD_ROWS, D), lambda i: (0, 0)),
                  pl.BlockSpec(memory_space=pl.ANY)] + exchange.specs_any,
        out_specs=[tile(D), pl.BlockSpec((8, D), lambda i: (0, 0))] + exchange.specs_any,
        out_shape=[jax.ShapeDtypeStruct((s, D), F32), jax.ShapeDtypeStruct((8, D), F32)] + exchange.out_shape,
        scratch_shapes=[pltpu.VMEM((D, DIN), BF16), pltpu.SemaphoreType.DMA((1,))] + exchange.scratch,
        compiler_params=_params(("arbitrary",)),
    )(dproj, dx1, x, vec_d, w_in, *ex_grads)


def _dot_tn(a, b):
    return lax.dot_general(a, b, (((0,), (0,)), ((), ())), preferred_element_type=F32)


def _wgrad(a, b, tm, tn, ts, name, dtype, exchange=None, ex_grads=()):
    s, m = a.shape
    nn = b.shape[1]
    grid = (m // tm, nn // tn, s // ts)
    n = exchange.n if exchange else 0

    def body(*refs):
        a_ref, b_ref = refs[:2]
        grads = refs[2:2 + n]
        o_ref = refs[2 + n]
        recvs = refs[3 + n:3 + 2 * n]
        acc = refs[3 + 2 * n]
        sems = refs[4 + 2 * n:]
        i, j, k = pl.program_id(0), pl.program_id(1), pl.program_id(2)
        if exchange:
            pl.when((i == 0) & (j == 0) & (k == 0))(lambda: exchange.start(grads, recvs, *sems))
        part = _dot_tn(a_ref[...], b_ref[...])

        @pl.when(k == 0)
        def _():
            acc[...] = part

        @pl.when(k > 0)
        def _():
            acc[...] += part

        @pl.when(k == grid[2] - 1)
        def _():
            o_ref[...] = acc[...].astype(dtype)

        if exchange:
            pl.when((i == grid[0] - 1) & (j == grid[1] - 1) & (k == grid[2] - 1))(
                lambda: exchange.finish(grads, recvs, *sems))

    hosted = exchange.specs_any if exchange else []
    return pl.pallas_call(
        body, name=name, grid=grid,
        in_specs=[pl.BlockSpec((ts, tm), lambda i, j, k: (k, i)), pl.BlockSpec((ts, tn), lambda i, j, k: (k, j))]
        + hosted,
        out_specs=[pl.BlockSpec((tm, tn), lambda i, j, k: (i, j))] + hosted,
        out_shape=[jax.ShapeDtypeStruct((m, nn), dtype)] + (exchange.out_shape if exchange else []),
        scratch_shapes=[pltpu.VMEM((tm, tn), F32)] + (exchange.scratch if exchange else []),
        compiler_params=_params(("arbitrary", "arbitrary", "arbitrary")),
    )(a, b, *ex_grads)


FFN_SLABS_FWD = ((0, 1024), (1024, 2816))
FFN_SLABS_BWD = ((0, 1536), (1536, 2816))
TS_PROJ = 512
TS_MIX = 256
TS_FFN = 256
TS_WGRAD = 2048


def _local_step(x, tgt, vec_d, vec_f, placed, place):
    s = x.shape[0]
    tw = min(TS_WGRAD, s)
    sp_in, sp_pool, sp_bout, sp_o, sp_up, sp_down = SHARDED
    proj, h1, w_in, w_pool, w_bout, w_o = _fwd_proj(x, vec_d, placed[0], placed[1:4], place, min(2 * TS_PROJ, s))
    x1, o, pg, q, merged, ya0, yb, cv, w_up, w_down = _fwd_mix(proj, x, vec_d, w_pool, w_bout, w_o, placed[4:],
                                                               min(TS_MIX, s))
    up0, upc, a, h2, dx2, dff, vo_f, loss = _fwd_ffn(x1, tgt, vec_d, vec_f, w_up, w_down, min(TS_FFN, s))
    g_down, = _wgrad(a, dff, F // 2, D, tw, "wgrad_down", BF16)
    dx1, dup0, vo_b, fo, r_down = _bwd_ffn(dff, dx2, x1, up0, upc, vec_d, vec_f, w_up, w_down,
                                           _GradExchange([sp_down]), [g_down], min(TS_FFN, s))
    g_up, = _wgrad(h2, dup0, D, F2 // NCHIP, tw, "wgrad_up", BF16)
    dproj, vo_m, g_o, g_bout, g_pool, r_up = _bwd_mix(dx1, o, proj, cv, ya0, yb, merged, q, pg, vec_d,
                                                      w_pool, w_bout, w_o, _GradExchange([sp_up]), [g_up],
                                                      min(TS_MIX, s))
    g_in, r_pool, r_bout, r_o = _wgrad(h1, dproj, D, DIN // NCHIP, tw, "wgrad_in", BF16,
                                       _GradExchange([sp_pool, sp_bout, sp_o]), [g_pool, g_bout, g_o])
    dx, vo_i, r_in = _bwd_in(dproj, dx1, x, vec_d, w_in, _GradExchange([sp_in]), [g_in], min(TS_PROJ, s))
    vecs = dict(
        dsh1=vo_i[0], dsc1=vo_i[1], dg_pre_mix=vo_i[2],
        dgt1=vo_m[0], dg_post_mix=vo_m[1], dpool_scale=vo_m[2], dconv_b=vo_m[3],
        dconv_w=vo_m[4:7],
        dsh2=vo_b[0], dsc2=vo_b[1], dg_pre_ffn=vo_b[2],
        dgt2=vo_f[0], dg_post_ffn=vo_f[1],
        dffn_conv_w=fo[FV_W0:FV_W2 + 1], dffn_conv_b=fo[FV_B],
    )
    local = dict(w_in=g_in, w_pool=g_pool, w_bout=g_bout, w_o=g_o, w_up=g_up, w_down=g_down)
    received = dict(w_in=r_in, w_pool=r_pool, w_bout=r_bout, w_o=r_o, w_up=r_up, w_down=r_down)
    return loss, dx, vecs, local, received


def _aligned(offset, n):
    return offset if isinstance(offset, int) else pl.multiple_of(offset, n)


class _Sharded:
    def __init__(self, name, full_shape, shard_axis, half_axis):
        self.name = name
        self.full_shape = full_shape
        self.shard_axis = shard_axis
        self.half_axis = half_axis
        self.shard_shape = tuple(n // NCHIP if a == shard_axis else n for a, n in enumerate(full_shape))
        self.piece_shape = tuple(n // 2 if a == half_axis else n for a, n in enumerate(self.shard_shape))

    def piece(self, full_ref, k, h):
        idx = []
        for a, n in enumerate(self.piece_shape):
            if a == self.shard_axis and a == self.half_axis:
                idx.append(pl.ds(_aligned((2 * k + h) * n, n), n))
            elif a == self.shard_axis:
                idx.append(pl.ds(_aligned(k * n, n), n))
            elif a == self.half_axis:
                idx.append(pl.ds(_aligned(h * n, n), n))
            else:
                idx.append(slice(None))
        return full_ref.at[tuple(idx)]

    def shard(self, full_ref, k):
        n = self.shard_shape[self.shard_axis]
        idx = [pl.ds(_aligned(k * n, n), n) if a == self.shard_axis else slice(None)
               for a in range(len(self.full_shape))]
        return full_ref.at[tuple(idx)]

    def half(self, shard_ref, h):
        n = self.piece_shape[self.half_axis]
        idx = [pl.ds(_aligned(h * n, n), n) if a == self.half_axis else slice(None)
               for a in range(len(self.full_shape))]
        return shard_ref.at[tuple(idx)]

SHARDED = (
    _Sharded("w_in", (D, DIN), 1, 0),
    _Sharded("w_pool", (NG, GW, GW), 1, 0),
    _Sharded("w_bout", (D, D), 0, 0),
    _Sharded("w_o", (D, D), 0, 0),
    _Sharded("w_up", (D, F2), 1, 0),
    _Sharded("w_down", (F, D), 0, 0),
)
NW = len(SHARDED)


def _mesh_place():
    x, y, c = lax.axis_index("x"), lax.axis_index("y"), lax.axis_index("c")
    chips = [(1 - x, y), (x, 1 - y), (1 - x, 1 - y)]
    return x, y, c, 2 * x + y, chips, [2 * px + py for px, py in chips]


def _remote(src, dst, send_sem, recv_sem, device):
    return pltpu.make_async_remote_copy(src_ref=src, dst_ref=dst, send_sem=send_sem, recv_sem=recv_sem,
                                        device_id=device, device_id_type=MESH)


SMALL_GATHER_SCRATCH = [pltpu.SemaphoreType.DMA((7,)), pltpu.SemaphoreType.DMA((7,)), pltpu.SemaphoreType.DMA]


def _small_gather(x_ref, out_ref, send_sems, recv_sems, local_sem):
    m_per = x_ref.shape[0]
    x, y, c, _, chips, _ = _mesh_place()
    me, sibling = (x, y, c), (x, y, 1 - c)

    def rows(px, py, pc):
        return out_ref.at[pl.ds((4 * px + 2 * py + pc) * m_per, m_per), :]

    def copy(k, blk, to, src=None):
        return _remote(rows(*blk) if src is None else src, rows(*blk), send_sems.at[k], recv_sems.at[k], to)

    mine = pltpu.make_async_copy(x_ref, rows(*me), local_sem)
    mine.start()
    first = [copy(0, me, sibling, src=x_ref)]
    first += [copy(1 + j, me, (*chip, c), src=x_ref) for j, chip in enumerate(chips)]
    for cp in first:
        cp.start()
    passed = [copy(4 + j, (*chip, c), sibling) for j, chip in enumerate(chips)]
    for j, chip in enumerate(chips):
        copy(1 + j, (*chip, c), me).wait_recv()
        passed[j].start()
    copy(0, sibling, me).wait_recv()
    for j, chip in enumerate(chips):
        copy(4 + j, (*chip, 1 - c), me).wait_recv()
    for cp in first + passed:
        cp.wait_send()
    mine.wait()


def _all_gather_small(block, name):
    m_per, n = block.shape
    return pl.pallas_call(
        _small_gather_body(), name=name,
        out_shape=jax.ShapeDtypeStruct((NDEV * m_per, n), block.dtype),
        in_specs=[pl.BlockSpec(memory_space=pltpu.VMEM)],
        out_specs=pl.BlockSpec(memory_space=pltpu.VMEM),
        scratch_shapes=SMALL_GATHER_SCRATCH,
        compiler_params=pltpu.CompilerParams(vmem_limit_bytes=VMEM_LIMIT),
    )(block)


def _small_gather_body():
    def body(x_ref, out_ref, send_sems, recv_sems, local_sem):
        _small_gather(x_ref, out_ref, send_sems, recv_sems, local_sem)
    return body


class _WeightGather:
    def __init__(self, specs):
        self.specs = specs
        self.n = len(specs)
        self.specs_any = [pl.BlockSpec(memory_space=pl.ANY)] * self.n
        self.out_shape = [jax.ShapeDtypeStruct(sp.full_shape, BF16) for sp in specs]
        self.scratch = [pltpu.SemaphoreType.DMA((6 * self.n,)), pltpu.SemaphoreType.DMA((6 * self.n,))]

    def _sends(self, outs, send_sems, recv_sems):
        x, y, c, k_me, chips, _ = _mesh_place()
        sends = []
        for j, chip in enumerate(chips):
            for w, sp in enumerate(self.specs):
                mine = sp.piece(outs[w], k_me, c)
                sends.append(_remote(mine, mine, send_sems.at[6 * w + j], recv_sems.at[6 * w + j], (*chip, c)))
        return sends

    def start(self, outs, send_sems, recv_sems):
        for cp in self._sends(outs, send_sems, recv_sems):
            cp.start()

    def _passes(self, outs, send_sems, recv_sems):
        x, y, c, _, chips, kidx = _mesh_place()
        return [_remote(sp.piece(outs[w], kidx[j], c), sp.piece(outs[w], kidx[j], c),
                        send_sems.at[6 * w + 3 + j], recv_sems.at[6 * w + 3 + j], (x, y, 1 - c))
                for j in range(3) for w, sp in enumerate(self.specs)]

    def forward(self, outs, send_sems, recv_sems):
        x, y, c, _, chips, kidx = _mesh_place()
        for j, chip in enumerate(chips):
            for w, sp in enumerate(self.specs):
                landed = sp.piece(outs[w], kidx[j], c)
                _remote(landed, landed, send_sems.at[6 * w + j], recv_sems.at[6 * w + j], (*chip, c)).wait_recv()
        for cp in self._passes(outs, send_sems, recv_sems):
            cp.start()

    def drain(self, outs, send_sems, recv_sems):
        x, y, c, _, chips, kidx = _mesh_place()
        for j in range(3):
            for w, sp in enumerate(self.specs):
                landed = sp.piece(outs[w], kidx[j], 1 - c)
                _remote(landed, landed, send_sems.at[6 * w + 3 + j], recv_sems.at[6 * w + 3 + j],
                        (x, y, 1 - c)).wait_recv()
        for cp in self._sends(outs, send_sems, recv_sems) + self._passes(outs, send_sems, recv_sems):
            cp.wait_send()

    def finish(self, outs, send_sems, recv_sems):
        self.forward(outs, send_sems, recv_sems)
        self.drain(outs, send_sems, recv_sems)


class _GradExchange:
    def __init__(self, specs):
        self.specs = specs
        self.n = len(specs)
        self.specs_any = [pl.BlockSpec(memory_space=pl.ANY)] * self.n
        self.out_shape = [jax.ShapeDtypeStruct((NDEV,) + sp.piece_shape, BF16) for sp in specs]
        self.scratch = [pltpu.SemaphoreType.DMA((7 * self.n,)), pltpu.SemaphoreType.DMA((NDEV * self.n,))]

    def _sends(self, grads, recvs, send_sems, recv_sems):
        x, y, c, k_me, chips, kidx = _mesh_place()
        dev = 2 * k_me + c
        sends = []
        for w, sp in enumerate(self.specs):
            slot, arrival = recvs[w].at[dev], recv_sems.at[NDEV * w + dev]
            sends.append(_remote(sp.piece(grads[w], k_me, 1 - c), slot, send_sems.at[7 * w], arrival, (x, y, 1 - c)))
            for j, chip in enumerate(chips):
                for h in range(2):
                    sends.append(_remote(sp.piece(grads[w], kidx[j], h), slot, send_sems.at[7 * w + 1 + 2 * j + h],
                                         arrival, (*chip, h)))
        return sends

    def start(self, grads, recvs, send_sems, recv_sems):
        for cp in self._sends(grads, recvs, send_sems, recv_sems):
            cp.start()

    def finish(self, grads, recvs, send_sems, recv_sems):
        x, y, c, k_me, _, _ = _mesh_place()
        dev = 2 * k_me + c
        for w in range(self.n):
            for d in range(NDEV):
                landed = recvs[w].at[d]
                arrival = _remote(landed, landed, send_sems.at[7 * w], recv_sems.at[NDEV * w + d], (x, y, c))
                pl.when(d != dev)(arrival.wait_recv)
        for cp in self._sends(grads, recvs, send_sems, recv_sems):
            cp.wait_send()


def _device_sums(locals_, recvs, place):
    def body(p_ref, *refs):
        a_refs, b_refs, o_refs = refs[:NW], refs[NW:2 * NW], refs[2 * NW:]
        d = pl.program_id(0)
        own = d == p_ref[2]
        terms = [jnp.where(own, a_ref[...], b_ref[...]).astype(F32) for a_ref, b_ref in zip(a_refs, b_refs)]

        @pl.when(d == 0)
        def _():
            for o_ref, term in zip(o_refs, terms):
                o_ref[...] = term

        @pl.when(d > 0)
        def _():
            for o_ref, term in zip(o_refs, terms):
                o_ref[...] += term

    def mine(sp):
        nd = len(sp.piece_shape)
        return pl.BlockSpec(sp.piece_shape, lambda d, p_ref: tuple(
            2 * p_ref[0] + p_ref[1] if a == sp.shard_axis == sp.half_axis else
            p_ref[0] if a == sp.shard_axis else p_ref[1] if a == sp.half_axis else 0 for a in range(nd)))

    def others(sp):
        nd = len(sp.piece_shape)
        return pl.BlockSpec((None,) + sp.piece_shape,
                            lambda d, p_ref: (jnp.where(d == p_ref[2], (d + 1) % NDEV, d),) + (0,) * nd)

    def half(sp):
        nd = len(sp.piece_shape)
        return pl.BlockSpec(sp.piece_shape,
                            lambda d, p_ref: tuple(p_ref[1] if a == sp.half_axis else 0 for a in range(nd)))

    return pl.pallas_call(
        body, name="rs_device_sums",
        grid_spec=pltpu.PrefetchScalarGridSpec(
            num_scalar_prefetch=1, grid=(NDEV,),
            in_specs=[mine(sp) for sp in SHARDED] + [others(sp) for sp in SHARDED],
            out_specs=[half(sp) for sp in SHARDED]),
        out_shape=[jax.ShapeDtypeStruct(sp.shard_shape, F32) for sp in SHARDED],
        compiler_params=_params(("arbitrary",)),
    )(place, *locals_, *recvs)


def _pair_share(halves, vector_block):
    m_per, n = vector_block.shape

    def body(*refs):
        x_ref = refs[NW]
        outs, gathered = refs[NW + 1:2 * NW + 1], refs[2 * NW + 1]
        send_sems, recv_sems = refs[2 * NW + 2:2 * NW + 4]
        x, y, c, _, _, _ = _mesh_place()
        sibling = (x, y, 1 - c)
        sent = []
        for w, sp in enumerate(SHARDED):
            mine = sp.half(outs[w], c)
            cp = _remote(mine, mine, send_sems.at[w], recv_sems.at[w], sibling)
            cp.start()
            sent.append(cp)
        _small_gather(x_ref, gathered, *refs[2 * NW + 4:])
        for w, sp in enumerate(SHARDED):
            landed = sp.half(outs[w], 1 - c)
            _remote(landed, landed, send_sems.at[w], recv_sems.at[w], sibling).wait_recv()
        for cp in sent:
            cp.wait_send()

    hbm = pl.BlockSpec(memory_space=pl.ANY)
    vmem = pl.BlockSpec(memory_space=pltpu.VMEM)
    out = pl.pallas_call(
        body, name="rs_pair_share",
        out_shape=[jax.ShapeDtypeStruct(sp.shard_shape, F32) for sp in SHARDED]
        + [jax.ShapeDtypeStruct((NDEV * m_per, n), F32)],
        in_specs=[hbm] * NW + [vmem], out_specs=[hbm] * NW + [vmem],
        input_output_aliases={w: w for w in range(NW)},
        scratch_shapes=[pltpu.SemaphoreType.DMA((NW,)), pltpu.SemaphoreType.DMA((NW,))] + SMALL_GATHER_SCRATCH,
        compiler_params=pltpu.CompilerParams(vmem_limit_bytes=VMEM_LIMIT),
    )(*halves, vector_block)
    return out[:NW], out[NW]


def _reduce_scatter(local, received, place, vector_block):
    halves = _device_sums([local[sp.name] for sp in SHARDED], [received[sp.name] for sp in SHARDED], place)
    return _pair_share(halves, vector_block)


def _place_bf16(sp, w, place):
    nd = len(sp.full_shape)

    def body(p_ref, w_ref, o_ref):
        o_ref[...] = w_ref[...].astype(BF16)

    return pl.pallas_call(
        body, name="place_" + sp.name,
        grid_spec=pltpu.PrefetchScalarGridSpec(
            num_scalar_prefetch=1, grid=(1,),
            in_specs=[pl.BlockSpec(sp.shard_shape, lambda i, p_ref: (0,) * nd)],
            out_specs=pl.BlockSpec(sp.shard_shape,
                                   lambda i, p_ref: tuple(p_ref[0] if a == sp.shard_axis else 0 for a in range(nd)))),
        out_shape=jax.ShapeDtypeStruct(sp.full_shape, BF16),
        compiler_params=_params(("arbitrary",)),
    )(place, w)


def _matmul_f32(a, b, name):
    def body(a_ref, b_ref, o_ref):
        o_ref[...] = jnp.dot(a_ref[...], b_ref[...], preferred_element_type=F32, precision=lax.Precision.HIGHEST)

    return pl.pallas_call(body, name=name, out_shape=jax.ShapeDtypeStruct((a.shape[0], b.shape[1]), F32),
                          compiler_params=pltpu.CompilerParams(vmem_limit_bytes=VMEM_LIMIT))(a, b)


def _sum_devices(stacked):
    def body(x_ref, o_ref):
        acc = x_ref[0]
        for d in range(1, NDEV):
            acc = acc + x_ref[d]
        o_ref[...] = acc

    return pl.pallas_call(body, name="sum_devices", out_shape=jax.ShapeDtypeStruct(stacked.shape[1:], F32),
                          compiler_params=pltpu.CompilerParams(vmem_limit_bytes=VMEM_LIMIT))(stacked)


ADAMW_STEPS = 8


def _adamw(ws, gs, ms, vs, name):
    n = len(ws)
    steps = ADAMW_STEPS if all(w.shape[0] % (8 * ADAMW_STEPS) == 0 for w in ws) else 1

    def body(*refs):
        ins, outs = refs[:4 * n], refs[4 * n:]
        for k in range(n):
            w_ref, g_ref, m_ref, v_ref = ins[k], ins[n + k], ins[2 * n + k], ins[3 * n + k]
            gv = g_ref[...]
            nm = ADAM_B1 * m_ref[...] + (1.0 - ADAM_B1) * gv
            nv = ADAM_B2 * v_ref[...] + (1.0 - ADAM_B2) * (gv * gv)
            m_hat = nm / (1.0 - ADAM_B1 ** ADAM_STEP)
            v_hat = nv / (1.0 - ADAM_B2 ** ADAM_STEP)
            outs[k][...] = -ADAM_LR * (m_hat / (jnp.sqrt(v_hat) + ADAM_EPS) + ADAM_WD * w_ref[...])
            outs[n + k][...] = nm
            outs[2 * n + k][...] = nv

    blks = [pl.BlockSpec((w.shape[0] // steps, w.shape[1]), lambda i: (i, 0)) for w in ws]
    shapes = [jax.ShapeDtypeStruct(w.shape, F32) for w in ws]
    out = pl.pallas_call(
        body, name="adamw_" + name, grid=(steps,), in_specs=blks * 4, out_specs=blks * 3, out_shape=shapes * 3,
        compiler_params=_params(("parallel",)),
    )(*ws, *gs, *ms, *vs)
    return out[:n], out[n:2 * n], out[2 * n:]


WEIGHT_NAMES = ("g_pre_mix", "g_post_mix", "g_pre_ffn", "g_post_ffn", "w_ada", "b_ada", "w_in", "w_pool",
                "pool_scale", "conv_w", "conv_b", "w_bout", "w_o", "w_up", "ffn_conv_w", "ffn_conv_b", "w_down")
MATRIX_NAMES = ("w_ada",) + tuple(sp.name for sp in SHARDED)
VECTOR_NAMES = tuple(n for n in WEIGHT_NAMES if n not in MATRIX_NAMES)

CW = D // NCHIP
FCW = F2 // NCHIP
ADA_W = DIN // NCHIP
COND_BLOCK = (8, 768)
GRAD_BLOCK = (8, 4864)


def _flat_pad(parts, shape):
    flat = jnp.concatenate([p.reshape(-1) for p in parts])
    return jnp.pad(flat, (0, shape[0] * shape[1] - flat.shape[0])).reshape(shape)


def _take(flat, offset, shape):
    size = 1
    for n in shape:
        size *= n
    return flat[offset:offset + size].reshape(shape), offset + size


def kernel(x, c, g_pre_mix, g_post_mix, g_pre_ffn, g_post_ffn, w_ada, b_ada, w_in, w_pool, pool_scale, conv_w, conv_b, w_bout, w_o, w_up, ffn_conv_w, ffn_conv_b, w_down, loss_target, m_g_pre_mix, m_g_post_mix, m_g_pre_ffn, m_g_post_ffn, m_w_ada, m_b_ada, m_w_in, m_w_pool, m_pool_scale, m_conv_w, m_conv_b, m_w_bout, m_w_o, m_w_up, m_ffn_conv_w, m_ffn_conv_b, m_w_down, v_g_pre_mix, v_g_post_mix, v_g_pre_ffn, v_g_post_ffn, v_w_ada, v_b_ada, v_w_in, v_w_pool, v_pool_scale, v_conv_w, v_conv_b, v_w_bout, v_w_o, v_w_up, v_ffn_conv_w, v_ffn_conv_b, v_w_down):
    weights = dict(g_pre_mix=g_pre_mix, g_post_mix=g_post_mix, g_pre_ffn=g_pre_ffn, g_post_ffn=g_post_ffn,
                   w_ada=w_ada, b_ada=b_ada, w_in=w_in, w_pool=w_pool, pool_scale=pool_scale, conv_w=conv_w,
                   conv_b=conv_b, w_bout=w_bout, w_o=w_o, w_up=w_up, ffn_conv_w=ffn_conv_w, ffn_conv_b=ffn_conv_b,
                   w_down=w_down)
    mom1 = dict(g_pre_mix=m_g_pre_mix, g_post_mix=m_g_post_mix, g_pre_ffn=m_g_pre_ffn, g_post_ffn=m_g_post_ffn,
                w_ada=m_w_ada, b_ada=m_b_ada, w_in=m_w_in, w_pool=m_w_pool, pool_scale=m_pool_scale,
                conv_w=m_conv_w, conv_b=m_conv_b, w_bout=m_w_bout, w_o=m_w_o, w_up=m_w_up,
                ffn_conv_w=m_ffn_conv_w, ffn_conv_b=m_ffn_conv_b, w_down=m_w_down)
    mom2 = dict(g_pre_mix=v_g_pre_mix, g_post_mix=v_g_post_mix, g_pre_ffn=v_g_pre_ffn, g_post_ffn=v_g_post_ffn,
                w_ada=v_w_ada, b_ada=v_b_ada, w_in=v_w_in, w_pool=v_w_pool, pool_scale=v_pool_scale,
                conv_w=v_conv_w, conv_b=v_conv_b, w_bout=v_w_bout, w_o=v_w_o, w_up=v_w_up,
                ffn_conv_w=v_ffn_conv_w, ffn_conv_b=v_ffn_conv_b, w_down=v_w_down)

    chip = 2 * lax.axis_index("x") + lax.axis_index("y")
    core = lax.axis_index("c")
    dev = 2 * chip + core
    place = jnp.stack([chip, core, dev]).astype(jnp.int32)

    cond = _all_gather_small(_flat_pad([c, conv_w, ffn_conv_w], COND_BLOCK), "gather_cond")
    cond = cond.reshape(NDEV, -1)
    c_all = cond[:, :D]
    by_chip = cond[0::2]
    conv_w_full = by_chip[:, D:D + 3 * CW].reshape(NCHIP, 3, CW).transpose(1, 0, 2).reshape(3, D)
    ffn_w_full = by_chip[:, D + 3 * CW:D + 3 * CW + 3 * FCW].reshape(NCHIP, 3, FCW).transpose(1, 0, 2).reshape(3, F2)

    mod_cols = _all_gather_small(_matmul_f32(c_all, w_ada[0], "ada_mod"), "gather_mod")
    mod_cols = mod_cols.reshape(NDEV, NDEV, ADA_W)[0::2]
    mod = lax.dynamic_index_in_dim(mod_cols, dev, axis=1, keepdims=False).reshape(6, D) + b_ada.reshape(6, D)
    vec_d = jnp.concatenate([mod, g_pre_mix, g_post_mix, g_pre_ffn, g_post_ffn, pool_scale, conv_b, conv_w_full,
                             jnp.zeros((VD_ROWS - 15, D), F32)], axis=0)
    vec_f = jnp.concatenate([ffn_w_full, ffn_conv_b, jnp.zeros((FV_ROWS - 4, F2), F32)], axis=0)

    placed = [_place_bf16(sp, weights[sp.name][0], place) for sp in SHARDED]
    loss_blk, dx, vecs, local, received = _local_step(x[0], loss_target[0], vec_d, vec_f, placed, place)

    dmod = [vecs[n] for n in ("dsh1", "dsc1", "dgt1", "dsh2", "dsc2", "dgt2")]
    small = [vecs["dg_pre_mix"], vecs["dg_post_mix"], vecs["dg_pre_ffn"], vecs["dg_post_ffn"]] + dmod + [
        vecs["dpool_scale"], vecs["dconv_w"], vecs["dconv_b"], vecs["dffn_conv_w"], vecs["dffn_conv_b"],
        loss_blk[0]]
    reduced, gathered = _reduce_scatter(local, received, place, _flat_pad(small, GRAD_BLOCK))
    total = _sum_devices(gathered.reshape((NDEV,) + GRAD_BLOCK)).reshape(-1)
    vgrad = {}
    off = 0
    for n in ("g_pre_mix", "g_post_mix", "g_pre_ffn", "g_post_ffn"):
        vgrad[n], off = _take(total, off, (1, D))
    dmod_off = off
    vgrad["b_ada"], off = _take(total, off, (1, DIN))
    vgrad["pool_scale"], off = _take(total, off, (1, D))
    g_conv_w, off = _take(total, off, (3, D))
    vgrad["conv_w"] = lax.dynamic_slice_in_dim(g_conv_w, chip * CW, CW, axis=1)[None]
    vgrad["conv_b"], off = _take(total, off, (1, D))
    g_ffn_w, off = _take(total, off, (3, F2))
    vgrad["ffn_conv_w"] = lax.dynamic_slice_in_dim(g_ffn_w, chip * FCW, FCW, axis=1)[None]
    vgrad["ffn_conv_b"], off = _take(total, off, (1, F2))
    loss = total[off]

    dmod_all = gathered.reshape(NDEV, -1)[:, dmod_off:dmod_off + DIN]
    dmod_cols = lax.dynamic_slice_in_dim(dmod_all, chip * ADA_W, ADA_W, axis=1)
    g_ada = _matmul_f32(jnp.pad(c_all.T, ((0, 0), (0, 128 - NDEV))), jnp.pad(dmod_cols, ((0, 128 - NDEV), (0, 0))),
                        "ada_wgrad")

    mgrad = {"w_ada": g_ada}
    for sp, g in zip(SHARDED, reduced):
        mgrad[sp.name] = g

    grad, delta, new_m, new_v = {}, {}, {}, {}
    two_d = lambda tree: [tree[n].reshape(-1, weights[n].shape[-1]) for n in MATRIX_NAMES]
    ds, nms, nvs = _adamw(two_d(weights), two_d(mgrad), two_d(mom1), two_d(mom2), "matrices")
    for n, d, nm, nv in zip(MATRIX_NAMES, ds, nms, nvs):
        shape = weights[n].shape
        grad[n], delta[n], new_m[n], new_v[n] = (a.reshape(shape) for a in (mgrad[n], d, nm, nv))
    flat = lambda tree: [jnp.concatenate([tree[n].reshape(1, -1) for n in VECTOR_NAMES], axis=1)]
    (d,), (nm,), (nv,) = _adamw(flat(weights), flat(vgrad), flat(mom1), flat(mom2), "vectors")
    off = 0
    for n in VECTOR_NAMES:
        shape = weights[n].shape
        grad[n] = vgrad[n].reshape(shape)
        delta[n], _ = _take(d[0], off, shape)
        new_m[n], _ = _take(nm[0], off, shape)
        new_v[n], off = _take(nv[0], off, shape)

    return (loss, dx[None], *[grad[n] for n in WEIGHT_NAMES], *[delta[n] for n in WEIGHT_NAMES],
            *[new_m[n] for n in WEIGHT_NAMES], *[new_v[n] for n in WEIGHT_NAMES])
```

```python
import jax
import jax.numpy as jnp
from jax import lax
from jax.experimental import pallas as pl
from jax.experimental.pallas import tpu as pltpu

F32 = jnp.float32
BF16 = jnp.bfloat16

D = 1024
DIN = 6 * D
F = 2816
F2 = 2 * F
NG = 4
GW = D // NG
POOL_CARRY = 16
CONV_CARRY = 3
EPS = 1e-6
NCHIP = 4
NDEV = 8

ADAM_LR = 0.001
ADAM_B1 = 0.9
ADAM_B2 = 0.999
ADAM_EPS = 1e-08
ADAM_WD = 0.01
ADAM_STEP = 10

VMEM_LIMIT = 60 * 1024 * 1024

(V_SH1, V_SC1, V_GT1, V_SH2, V_SC2, V_GT2, V_GPRE1, V_GPOST1, V_GPRE2, V_GPOST2,
 V_PSCALE, V_CB, V_CW0, V_CW1, V_CW2) = range(15)
VD_ROWS = 16
FV_W0, FV_W1, FV_W2, FV_B = range(4)
FV_ROWS = 8

MESH = pl.DeviceIdType.MESH


def _params(sem=None, vmem=VMEM_LIMIT):
    return pltpu.CompilerParams(dimension_semantics=sem, vmem_limit_bytes=vmem)


def _row(ref, r):
    return ref[r:r + 1, :]


def _load_once(pairs, sem):
    @pl.when(pl.program_id(0) == 0)
    def _():
        copies = [pltpu.make_async_copy(src, dst, sem.at[n]) for n, (src, dst) in enumerate(pairs)]
        for cp in copies:
            cp.start()
        for cp in copies:
            cp.wait()


def _dot(a, b):
    return jnp.dot(a, b, preferred_element_type=F32)


def _dot_nt(a, b):
    return lax.dot_general(a, b, (((1,), (1,)), ((), ())), preferred_element_type=F32)


BLK = 256
SEG = BLK // 8


def _load_rows(ref, ts):
    blocks = [jnp.swapaxes(ref[b * BLK:(b + 1) * BLK, :].reshape(8, SEG, ref.shape[-1]), 0, 1).reshape(BLK, -1)
              for b in range(ts // BLK)]
    return jnp.concatenate(blocks, axis=0)


def _store_rows(ref, val, ts):
    for b in range(ts // BLK):
        blk = val[b * BLK:(b + 1) * BLK, :].reshape(SEG, 8, val.shape[-1])
        ref[b * BLK:(b + 1) * BLK, :] = jnp.swapaxes(blk, 0, 1).reshape(BLK, -1)


def _times(t0):
    p = lax.broadcasted_iota(jnp.int32, (BLK, 1), 0)
    return t0 + (p & 7) * SEG + (p >> 3)


def _before(x, carry, s):
    x3 = x.reshape(SEG, 8, x.shape[-1])
    tail = pltpu.roll(x3[SEG - s:], 1, 1)
    row = lax.broadcasted_iota(jnp.int32, tail.shape, 1)
    out = jnp.concatenate([jnp.where(row == 0, carry, tail), x3[:SEG - s]], axis=0)
    return out.reshape(x.shape), tail


def _after(x, carry, s):
    x3 = x.reshape(SEG, 8, x.shape[-1])
    head = pltpu.roll(x3[:s], 7, 1)
    row = lax.broadcasted_iota(jnp.int32, head.shape, 1)
    out = jnp.concatenate([x3[s:], jnp.where(row == 7, carry, head)], axis=0)
    return out.reshape(x.shape), head


def _causal_conv(x, carry, cols, w0, w1, w2, b):
    x1, carry[0:1, :, cols] = _before(x, carry[0:1, :, cols], 1)
    x2, carry[1:3, :, cols] = _before(x, carry[1:3, :, cols], 2)
    return b + w2 * x + w1 * x1 + w0 * x2


def _causal_conv_bwd(dy, carry, cols, w0, w1, w2):
    d1, carry[0:1, :, cols] = _after(dy, carry[0:1, :, cols], 1)
    d2, carry[1:3, :, cols] = _after(dy, carry[1:3, :, cols], 2)
    return w2 * dy + w1 * d1 + w0 * d2, d1, d2


def _pool_counts(t0, g):
    return jnp.minimum((_times(t0) + 1).astype(F32), float(2 << g))


def _rms(x):
    return lax.rsqrt(jnp.mean(x * x, axis=-1, keepdims=True) + EPS)


def _rms_bwd(dn, n, r):
    return r * (dn - n * jnp.mean(dn * n, axis=-1, keepdims=True))


def _colsum(x):
    return jnp.sum(x, axis=0, keepdims=True)


def _gelu_and_grad(x):
    k, a = 0.7978845608028654, 0.044715
    x2 = x * x
    th1 = 1.0 + jnp.tanh(x * (x2 * (k * a) + k))
    hx = 0.5 * x
    gelu = hx * th1
    dgelu = 0.5 * th1 + (hx * (th1 * (2.0 - th1))) * (x2 * (3.0 * k * a) + k)
    return gelu, dgelu


def _fwd_proj(x, vec_d, placed_in, placed_rest, place, ts):
    s = x.shape[0]
    nt = s // ts
    cw = DIN // NCHIP
    sp_in = SHARDED[0]
    gather = _WeightGather(SHARDED[1:4])
    n = gather.n

    def body(*refs):
        p_ref, x_ref, v_ref = refs[:3]
        proj_ref, h1_ref, w_full = refs[4 + n:7 + n]
        rest = refs[7 + n:7 + 2 * n]
        w_vmem, h1_all, sem, in_send, in_recv, send_sems, recv_sems = refs[7 + 2 * n:]
        j, i = pl.program_id(0), pl.program_id(1)
        x_, y_, c, k_me, _, _ = _mesh_place()
        sibling = (x_, y_, 1 - c)

        def peer(t):
            return (x_ ^ (t >> 1), y_ ^ (t & 1))

        def w_in_sends():
            mine = sp_in.piece(w_full, k_me, c)
            return [_remote(mine, mine, in_send.at[t - 1], in_recv.at[t - 1], (*peer(t), c)) for t in (1, 2, 3)]

        def load_block(k):
            cp = pltpu.make_async_copy(sp_in.shard(w_full, k), w_vmem.at[k], sem.at[0])
            cp.start()
            cp.wait()

        @pl.when((j == 0) & (i == 0))
        def _():
            for cp in w_in_sends()[:2]:
                cp.start()
            load_block(k_me)

        @pl.when((j == 1) & (i == 0))
        def _():
            for cp in w_in_sends()[:2]:
                cp.wait_send()
            w_in_sends()[2].start()
            gather.start(rest, send_sems, recv_sems)

        for t in (1, 2, 3):
            @pl.when((j == t) & (i == 0))
            def _(t=t):
                k = k_me ^ t
                landed = sp_in.piece(w_full, k, c)
                _remote(landed, landed, in_send.at[t - 1], in_recv.at[t - 1], (*peer(t), c)).wait_recv()
                _remote(landed, landed, in_send.at[2 + t], in_recv.at[2 + t], sibling).start()
                other = sp_in.piece(w_full, k, 1 - c)
                _remote(other, other, in_send.at[2 + t], in_recv.at[2 + t], sibling).wait_recv()
                load_block(k)

        @pl.when(j == 0)
        def _():
            xv = _load_rows(x_ref, ts)
            n1 = xv * _rms(xv)
            h = n1 * (_row(v_ref, V_GPRE1) * (1.0 + _row(v_ref, V_SC1))) + _row(v_ref, V_SH1)
            hb = h.astype(BF16)
            h1_ref[...] = hb
            h1_all[i] = hb

        proj_ref[...] = _dot(h1_all[i], w_vmem[k_me ^ j]).astype(BF16)

        @pl.when((j == NCHIP - 1) & (i == nt - 1))
        def _():
            w_in_sends()[2].wait_send()
            for t in (1, 2, 3):
                landed = sp_in.piece(w_full, k_me ^ t, c)
                _remote(landed, landed, in_send.at[2 + t], in_recv.at[2 + t], sibling).wait_send()
            gather.finish(rest, send_sems, recv_sems)

    once = lambda w: pl.BlockSpec((ts, w), lambda j, i, p: (jnp.where(j == 0, i, nt - 1), 0))
    return pl.pallas_call(
        body, name="fwd_proj",
        grid_spec=pltpu.PrefetchScalarGridSpec(
            num_scalar_prefetch=1, grid=(NCHIP, nt),
            in_specs=[once(D), pl.BlockSpec((VD_ROWS, D), lambda j, i, p: (0, 0)),
                      pl.BlockSpec(memory_space=pl.ANY)] + gather.specs_any,
            out_specs=[pl.BlockSpec((ts, cw), lambda j, i, p: (i, p[0] ^ j)), once(D),
                       pl.BlockSpec(memory_space=pl.ANY)] + gather.specs_any,
            scratch_shapes=[pltpu.VMEM((NCHIP, D, cw), BF16), pltpu.VMEM((nt, ts, D), BF16),
                            pltpu.SemaphoreType.DMA((1,)),
                            pltpu.SemaphoreType.DMA((6,)), pltpu.SemaphoreType.DMA((6,))] + gather.scratch),
        out_shape=[jax.ShapeDtypeStruct((s, DIN), BF16), jax.ShapeDtypeStruct((s, D), BF16),
                   jax.ShapeDtypeStruct(sp_in.full_shape, BF16)] + gather.out_shape,
        input_output_aliases={3 + w: 2 + w for w in range(n + 1)},
        compiler_params=_params(("arbitrary", "arbitrary")),
    )(place, x, vec_d, placed_in, *placed_rest)


def _fwd_mix(proj, x, vec_d, w_pool, w_bout, w_o, placed_ffn, ts):
    s = x.shape[0]
    gather = _WeightGather(SHARDED[4:])
    n = gather.n

    def body(*refs):
        ins, outs, rest = refs[:6], refs[6 + n:14 + n], refs[14 + n:14 + 2 * n]
        scratch, (pbuf, psem), sems = refs[14 + 2 * n:-4], refs[-4:-2], refs[-2:]
        i = pl.program_id(0)
        nt = s // ts
        pl.when(i == 0)(lambda: gather.start(rest, *sems))
        pl.when(i == nt - 1 - nt // 16)(lambda: gather.forward(rest, *sems))

        def fetch(t):
            rows = pl.ds(pl.multiple_of(t * ts, ts), ts)
            return pltpu.make_async_copy(ins[0].at[rows, :], pbuf.at[t % 3], psem.at[t % 3])

        @pl.when(i == 0)
        def _():
            fetch(0).start()
            fetch(1).start()

        pl.when(i + 2 < nt)(lambda: fetch(i + 2).start())
        fetch(i).wait()
        compute(pbuf.at[i % 3], *ins[1:], *outs, *scratch)
        pl.when(i == nt - 1)(lambda: gather.drain(rest, *sems))

    def compute(p_ref, x_ref, v_ref, wp_hbm, wb_hbm, wo_hbm,
                x1_ref, o_ref, pg_ref, q_ref, mg_ref, ya_ref, yb_ref, cv_ref,
                wp, wb, wo, carry_p, carry_v, sem):
        i = pl.program_id(0)
        _load_once([(wp_hbm, wp), (wb_hbm, wb), (wo_hbm, wo)], sem)

        @pl.when(i == 0)
        def _():
            carry_p[...] = jnp.zeros_like(carry_p)
            carry_v[...] = jnp.zeros_like(carry_v)

        t0 = i * ts
        for g in range(NG):
            cols = slice(g * GW, (g + 1) * GW)
            u = p_ref[:, cols].astype(F32)
            e = u
            for l in range(g + 1):
                slot = slice((1 << l) - 1, (2 << l) - 1)
                shifted, carry_p[slot, :, cols] = _before(e, carry_p[slot, :, cols], 1 << l)
                e = e + shifted
            pgb = (e / _pool_counts(t0, g) - u).astype(BF16)
            pg_ref[:, cols] = pgb
            ya_ref[:, cols] = _dot(pgb, wp[g]).astype(BF16)

        u_x = p_ref[:, D:2 * D].astype(F32)
        u_c = p_ref[:, 3 * D:4 * D].astype(F32)
        v = u_c * u_x
        cv = _causal_conv(v, carry_v, slice(None), _row(v_ref, V_CW0), _row(v_ref, V_CW1),
                          _row(v_ref, V_CW2), _row(v_ref, V_CB))
        cv_ref[...] = cv.astype(BF16)
        q = (p_ref[:, 2 * D:3 * D].astype(F32) * cv).astype(BF16)
        q_ref[...] = q
        y_b = _dot(q, wb[...])
        yb_ref[...] = y_b.astype(BF16)

        y_a = ya_ref[...].astype(F32) * _row(v_ref, V_PSCALE)
        merged = (jax.nn.sigmoid(p_ref[:, 4 * D:5 * D].astype(F32)) * y_a
                  + jax.nn.sigmoid(p_ref[:, 5 * D:6 * D].astype(F32)) * y_b).astype(BF16)
        mg_ref[...] = merged
        o = _dot(merged, wo[...])
        o_ref[...] = o.astype(BF16)
        x1_ref[...] = _load_rows(x_ref, ts) + _row(v_ref, V_GT1) * ((o * _rms(o)) * _row(v_ref, V_GPOST1))

    tile = lambda w: pl.BlockSpec((ts, w), lambda i: (i, 0))
    hbm = pl.BlockSpec(memory_space=pl.ANY)
    return pl.pallas_call(
        body, name="fwd_mix", grid=(s // ts,),
        in_specs=[hbm, tile(D), pl.BlockSpec((VD_ROWS, D), lambda i: (0, 0)), hbm, hbm, hbm] + gather.specs_any,
        out_specs=[tile(D)] * 8 + gather.specs_any,
        out_shape=[jax.ShapeDtypeStruct((s, D), F32)] + [jax.ShapeDtypeStruct((s, D), BF16)] * 7 + gather.out_shape,
        input_output_aliases={6 + w: 8 + w for w in range(n)},
        scratch_shapes=[pltpu.VMEM((NG, GW, GW), BF16), pltpu.VMEM((D, D), BF16), pltpu.VMEM((D, D), BF16),
                        pltpu.VMEM((POOL_CARRY, 8, D), F32), pltpu.VMEM((CONV_CARRY, 8, D), F32),
                        pltpu.SemaphoreType.DMA((3,)), pltpu.VMEM((3, ts, DIN), BF16),
                        pltpu.SemaphoreType.DMA((3,))] + gather.scratch,
        compiler_params=_params(("arbitrary",)),
    )(proj, x, vec_d, w_pool, w_bout, w_o, *placed_ffn)


def _fwd_ffn(x1, tgt, vec_d, vec_f, w_up, w_down, ts):
    s = x1.shape[0]

    def body(x1_ref, t_ref, v_ref, f_ref, wu_hbm, wd_hbm,
             up_ref, upc_ref, a_ref, h2_ref, dx2_ref, dff_ref, vo_ref, loss_ref,
             wu, wd, carry, sem):
        i = pl.program_id(0)
        _load_once([(wu_hbm, wu), (wd_hbm, wd)], sem)

        @pl.when(i == 0)
        def _():
            carry[...] = jnp.zeros_like(carry)
            vo_ref[...] = jnp.zeros_like(vo_ref)
            loss_ref[...] = jnp.zeros_like(loss_ref)

        x1v = x1_ref[...]
        n3 = x1v * _rms(x1v)
        h2 = (n3 * (_row(v_ref, V_GPRE2) * (1.0 + _row(v_ref, V_SC2))) + _row(v_ref, V_SH2)).astype(BF16)
        h2_ref[...] = h2

        ff = jnp.zeros((ts, D), F32)
        for lo, hi in FFN_SLABS_FWD:
            up = []
            for cols in (slice(lo, hi), slice(F + lo, F + hi)):
                u0 = _dot(h2, wu[:, cols])
                up_ref[:, cols] = u0.astype(BF16)
                y = _causal_conv(u0, carry, cols, f_ref[FV_W0:FV_W0 + 1, cols], f_ref[FV_W1:FV_W1 + 1, cols],
                                 f_ref[FV_W2:FV_W2 + 1, cols], f_ref[FV_B:FV_B + 1, cols])
                upc_ref[:, cols] = y.astype(BF16)
                up.append(y)
            gelu, _ = _gelu_and_grad(up[0])
            a = (gelu * up[1]).astype(BF16)
            a_ref[:, lo:hi] = a
            ff = ff + _dot(a, wd[lo:hi, :])

        r4 = _rms(ff)
        n4 = ff * r4
        gt2 = _row(v_ref, V_GT2)
        gpost = _row(v_ref, V_GPOST2)
        gate_gain = gt2 * gpost
        diff = (x1v + gate_gain * n4) - _load_rows(t_ref, ts)
        loss_ref[...] += jnp.full(loss_ref.shape, 0.5 / D * jnp.sum(diff * diff), F32)
        dx2_ref[...] = diff * (1.0 / D)
        s1 = _colsum(diff * n4)
        vo_ref[0:1, :] += s1 * (gpost * (1.0 / D))
        vo_ref[1:2, :] += s1 * (gt2 * (1.0 / D))
        dff_ref[...] = _rms_bwd(diff * (gate_gain * (1.0 / D)), n4, r4).astype(BF16)

    tile = lambda w: pl.BlockSpec((ts, w), lambda i: (i, 0))
    full = lambda r, w: pl.BlockSpec((r, w), lambda i: (0, 0))
    hbm = pl.BlockSpec(memory_space=pl.ANY)
    return pl.pallas_call(
        body, name="fwd_ffn", grid=(s // ts,),
        in_specs=[tile(D), tile(D), full(VD_ROWS, D), full(FV_ROWS, F2), hbm, hbm],
        out_specs=[tile(F2), tile(F2), tile(F), tile(D), tile(D), tile(D), full(8, D), full(8, 128)],
        out_shape=[jax.ShapeDtypeStruct((s, F2), BF16), jax.ShapeDtypeStruct((s, F2), BF16),
                   jax.ShapeDtypeStruct((s, F), BF16),
                   jax.ShapeDtypeStruct((s, D), BF16), jax.ShapeDtypeStruct((s, D), F32),
                   jax.ShapeDtypeStruct((s, D), BF16), jax.ShapeDtypeStruct((8, D), F32),
                   jax.ShapeDtypeStruct((8, 128), F32)],
        scratch_shapes=[pltpu.VMEM((D, F2), BF16), pltpu.VMEM((F, D), BF16), pltpu.VMEM((CONV_CARRY, 8, F2), F32),
                        pltpu.SemaphoreType.DMA((2,))],
        compiler_params=_params(("arbitrary",)),
    )(x1, tgt, vec_d, vec_f, w_up, w_down)


def _bwd_ffn(dff, dx2, x1, up0, upc, vec_d, vec_f, w_up, w_down, exchange, ex_grads, ts):
    s = x1.shape[0]
    nt = s // ts
    n = exchange.n

    def body(*refs):
        ins, grads = refs[:9], refs[9:9 + n]
        outs, recvs = refs[9 + n:13 + n], refs[13 + n:13 + 2 * n]
        scratch, sems = refs[13 + 2 * n:-2], refs[-2:]
        i = pl.program_id(0)
        pl.when(i == 0)(lambda: exchange.start(grads, recvs, *sems))
        compute(*ins, *outs, *scratch)
        pl.when(i == nt - 1)(lambda: exchange.finish(grads, recvs, *sems))

    def compute(dff_ref, dx2_ref, x1_ref, up_ref, upc_ref, v_ref, f_ref, wu_hbm, wd_hbm,
                dx1_ref, dup_ref, vo_ref, fo_ref, wu, wd, carry, sem):
        i = pl.program_id(0)
        _load_once([(wu_hbm, wu), (wd_hbm, wd)], sem)

        @pl.when(i == 0)
        def _():
            carry[...] = jnp.zeros_like(carry)
            vo_ref[...] = jnp.zeros_like(vo_ref)
            fo_ref[...] = jnp.zeros_like(fo_ref)

        dffb = dff_ref[...]

        dh2 = jnp.zeros((ts, D), F32)
        for lo, hi in FFN_SLABS_BWD:
            slabs = (slice(lo, hi), slice(F + lo, F + hi))
            gelu, dgelu = _gelu_and_grad(upc_ref[:, slabs[0]].astype(F32))
            da = _dot_nt(dffb, wd[lo:hi, :])
            dups = (da * upc_ref[:, slabs[1]].astype(F32) * dgelu, da * gelu)
            for cols, dup in zip(slabs, dups):
                du0, d1, d2 = _causal_conv_bwd(dup, carry, cols, f_ref[FV_W0:FV_W0 + 1, cols],
                                               f_ref[FV_W1:FV_W1 + 1, cols], f_ref[FV_W2:FV_W2 + 1, cols])
                u0 = up_ref[:, cols].astype(F32)
                fo_ref[FV_B:FV_B + 1, cols] += _colsum(dup)
                fo_ref[FV_W2:FV_W2 + 1, cols] += _colsum(dup * u0)
                fo_ref[FV_W1:FV_W1 + 1, cols] += _colsum(d1 * u0)
                fo_ref[FV_W0:FV_W0 + 1, cols] += _colsum(d2 * u0)
                du0 = du0.astype(BF16)
                dup_ref[:, cols] = du0
                dh2 = dh2 + _dot_nt(du0, wu[:, cols])

        x1v = x1_ref[...]
        r3 = _rms(x1v)
        n3 = x1v * r3
        gpre = _row(v_ref, V_GPRE2)
        sc = 1.0 + _row(v_ref, V_SC2)
        vo_ref[0:1, :] += _colsum(dh2)
        s2 = _colsum(dh2 * n3)
        vo_ref[1:2, :] += s2 * gpre
        vo_ref[2:3, :] += s2 * sc
        dx1_ref[...] = dx2_ref[...] + _rms_bwd(dh2 * (gpre * sc), n3, r3)

    rev = lambda w: pl.BlockSpec((ts, w), lambda i: (nt - 1 - i, 0))
    full = lambda r, w: pl.BlockSpec((r, w), lambda i: (0, 0))
    hbm = pl.BlockSpec(memory_space=pl.ANY)
    return pl.pallas_call(
        body, name="bwd_ffn", grid=(nt,),
        in_specs=[rev(D), rev(D), rev(D), rev(F2), rev(F2), full(VD_ROWS, D), full(FV_ROWS, F2), hbm, hbm]
        + exchange.specs_any,
        out_specs=[rev(D), rev(F2), full(8, D), full(FV_ROWS, F2)] + exchange.specs_any,
        out_shape=[jax.ShapeDtypeStruct((s, D), F32), jax.ShapeDtypeStruct((s, F2), BF16),
                   jax.ShapeDtypeStruct((8, D), F32), jax.ShapeDtypeStruct((FV_ROWS, F2), F32)] + exchange.out_shape,
        scratch_shapes=[pltpu.VMEM((D, F2), BF16), pltpu.VMEM((F, D), BF16), pltpu.VMEM((CONV_CARRY, 8, F2), F32),
                        pltpu.SemaphoreType.DMA((2,))] + exchange.scratch,
        compiler_params=_params(("arbitrary",)),
    )(dff, dx2, x1, up0, upc, vec_d, vec_f, w_up, w_down, *ex_grads)


def _bwd_mix(dx1, o, proj, cv, ya0, yb, merged, q, pg, vec_d, w_pool, w_bout, w_o, exchange, ex_grads, ts):
    s = dx1.shape[0]
    nt = s // ts
    n = exchange.n

    def body(*refs):
        ins, grads = refs[:13], refs[13:13 + n]
        outs, recvs = refs[13 + n:18 + n], refs[18 + n:18 + 2 * n]
        scratch, (pbuf, psem), sems = refs[18 + 2 * n:-4], refs[-4:-2], refs[-2:]
        i = pl.program_id(0)
        pl.when(i == 0)(lambda: exchange.start(grads, recvs, *sems))

        def fetch(t):
            rows = pl.ds(pl.multiple_of((nt - 1 - t) * ts, ts), ts)
            return pltpu.make_async_copy(ins[2].at[rows, :], pbuf.at[t % 3], psem.at[t % 3])

        @pl.when(i == 0)
        def _():
            fetch(0).start()
            fetch(1).start()

        pl.when(i + 2 < nt)(lambda: fetch(i + 2).start())
        fetch(i).wait()
        compute(*ins[:2], pbuf.at[i % 3], *ins[3:], *outs, *scratch)
        pl.when(i == nt - 1)(lambda: exchange.finish(grads, recvs, *sems))

    def compute(dx1_ref, o_ref, p_ref, cv_ref, ya_ref, yb_ref, mg_ref, q_ref, pg_ref, v_ref, wp_hbm, wb_hbm, wo_hbm,
                dp_ref, vo_ref, go_ref, gb_ref, gp_ref, wp, wb, wo, carry_d, carry_e, acc_o, acc_b, acc_p, sem):
        i = pl.program_id(0)
        _load_once([(wp_hbm, wp), (wb_hbm, wb), (wo_hbm, wo)], sem)

        @pl.when(i == 0)
        def _():
            carry_d[...] = jnp.zeros_like(carry_d)
            carry_e[...] = jnp.zeros_like(carry_e)
            vo_ref[...] = jnp.zeros_like(vo_ref)
            acc_o[...] = jnp.zeros_like(acc_o)
            acc_b[...] = jnp.zeros_like(acc_b)
            acc_p[...] = jnp.zeros_like(acc_p)

        t0 = (nt - 1 - i) * ts
        dx1v = dx1_ref[...]
        ov = o_ref[...].astype(F32)
        r2 = _rms(ov)
        n2 = ov * r2
        gpost = _row(v_ref, V_GPOST1)
        gt1 = _row(v_ref, V_GT1)
        s1 = _colsum(dx1v * n2)
        vo_ref[0:1, :] += s1 * gpost
        vo_ref[1:2, :] += s1 * gt1
        dob = _rms_bwd(dx1v * (gt1 * gpost), n2, r2).astype(BF16)
        acc_o[...] += _dot_tn(mg_ref[...], dob)
        dmerged = _dot_nt(dob, wo[...])

        ya0 = ya_ref[...].astype(F32)
        pscale = _row(v_ref, V_PSCALE)
        sa = jax.nn.sigmoid(p_ref[:, 4 * D:5 * D].astype(F32))
        dp_ref[:, 4 * D:5 * D] = (dmerged * (ya0 * pscale) * sa * (1.0 - sa)).astype(BF16)
        dy_a = dmerged * sa
        vo_ref[2:3, :] += _colsum(dy_a * ya0)
        dya0 = (dy_a * pscale).astype(BF16)

        sb = jax.nn.sigmoid(p_ref[:, 5 * D:6 * D].astype(F32))
        dp_ref[:, 5 * D:6 * D] = (dmerged * yb_ref[...].astype(F32) * sb * (1.0 - sb)).astype(BF16)
        dy_b = (dmerged * sb).astype(BF16)
        acc_b[...] += _dot_tn(q_ref[...], dy_b)
        dq = _dot_nt(dy_b, wb[...])

        u_x = p_ref[:, D:2 * D].astype(F32)
        u_b = p_ref[:, 2 * D:3 * D].astype(F32)
        u_c = p_ref[:, 3 * D:4 * D].astype(F32)
        w0, w1, w2 = _row(v_ref, V_CW0), _row(v_ref, V_CW1), _row(v_ref, V_CW2)
        dp_ref[:, 2 * D:3 * D] = (dq * cv_ref[...].astype(F32)).astype(BF16)
        dcv = dq * u_b
        dv, d1, d2 = _causal_conv_bwd(dcv, carry_d, slice(None), w0, w1, w2)
        v = u_c * u_x
        vo_ref[3:4, :] += _colsum(dcv)
        vo_ref[4:5, :] += _colsum(d2 * v)
        vo_ref[5:6, :] += _colsum(d1 * v)
        vo_ref[6:7, :] += _colsum(dcv * v)
        dp_ref[:, D:2 * D] = (dv * u_c).astype(BF16)
        dp_ref[:, 3 * D:4 * D] = (dv * u_x).astype(BF16)

        for g in range(NG):
            cols = slice(g * GW, (g + 1) * GW)
            acc_p[g] += _dot_tn(pg_ref[:, cols], dya0[:, cols])
            dpg = _dot_nt(dya0[:, cols], wp[g])
            e = dpg / _pool_counts(t0, g)
            for l in range(g + 1):
                slot = slice((1 << l) - 1, (2 << l) - 1)
                shifted, carry_e[slot, :, cols] = _after(e, carry_e[slot, :, cols], 1 << l)
                e = e + shifted
            dp_ref[:, cols] = (e - dpg).astype(BF16)

        @pl.when(i == nt - 1)
        def _():
            go_ref[...] = acc_o[...].astype(BF16)
            gb_ref[...] = acc_b[...].astype(BF16)
            gp_ref[...] = acc_p[...].astype(BF16)

    rev = lambda w: pl.BlockSpec((ts, w), lambda i: (nt - 1 - i, 0))
    hbm = pl.BlockSpec(memory_space=pl.ANY)
    whole = lambda shape: pl.BlockSpec(shape, lambda i: (0,) * len(shape))
    return pl.pallas_call(
        body, name="bwd_mix", grid=(nt,),
        in_specs=[rev(D), rev(D), hbm] + [rev(D)] * 6 + [whole((VD_ROWS, D)), hbm, hbm, hbm] + exchange.specs_any,
        out_specs=[rev(DIN), whole((8, D)), whole((D, D)), whole((D, D)), whole((NG, GW, GW))] + exchange.specs_any,
        out_shape=[jax.ShapeDtypeStruct((s, DIN), BF16), jax.ShapeDtypeStruct((8, D), F32),
                   jax.ShapeDtypeStruct((D, D), BF16), jax.ShapeDtypeStruct((D, D), BF16),
                   jax.ShapeDtypeStruct((NG, GW, GW), BF16)] + exchange.out_shape,
        scratch_shapes=[pltpu.VMEM((NG, GW, GW), BF16), pltpu.VMEM((D, D), BF16), pltpu.VMEM((D, D), BF16),
                        pltpu.VMEM((CONV_CARRY, 8, D), F32), pltpu.VMEM((POOL_CARRY, 8, D), F32),
                        pltpu.VMEM((D, D), F32), pltpu.VMEM((D, D), F32), pltpu.VMEM((NG, GW, GW), F32),
                        pltpu.SemaphoreType.DMA((3,)), pltpu.VMEM((3, ts, DIN), BF16),
                        pltpu.SemaphoreType.DMA((3,))] + exchange.scratch,
        compiler_params=_params(("arbitrary",)),
    )(dx1, o, proj, cv, ya0, yb, merged, q, pg, vec_d, w_pool, w_bout, w_o, *ex_grads)


def _bwd_in(dproj, dx1, x, vec_d, w_in, exchange, ex_grads, ts):
    s = x.shape[0]
    nt = s // ts
    n = exchange.n

    def body(*refs):
        ins, grads = refs[:5], refs[5:5 + n]
        outs, recvs = refs[5 + n:7 + n], refs[7 + n:7 + 2 * n]
        scratch, sems = refs[7 + 2 * n:-2], refs[-2:]
        i = pl.program_id(0)
        pl.when(i == 0)(lambda: exchange.start(grads, recvs, *sems))
        compute(*ins, *outs, *scratch)
        pl.when(i == nt - 1)(lambda: exchange.finish(grads, recvs, *sems))

    def compute(dp_ref, dx1_ref, x_ref, v_ref, w_hbm, dx_ref, vo_ref, w_vmem, sem):
        _load_once([(w_hbm, w_vmem)], sem)

        @pl.when(pl.program_id(0) == 0)
        def _():
            vo_ref[...] = jnp.zeros_like(vo_ref)

        dh1 = _dot_nt(dp_ref[...], w_vmem[...])
        xv = _load_rows(x_ref, ts)
        r1 = _rms(xv)
        n1 = xv * r1
        gpre = _row(v_ref, V_GPRE1)
        sc = 1.0 + _row(v_ref, V_SC1)
        vo_ref[0:1, :] += _colsum(dh1)
        s1 = _colsum(dh1 * n1)
        vo_ref[1:2, :] += s1 * gpre
        vo_ref[2:3, :] += s1 * sc
        _store_rows(dx_ref, dx1_ref[...] + _rms_bwd(dh1 * (gpre * sc), n1, r1), ts)

    tile = lambda w: pl.BlockSpec((ts, w), lambda i: (i, 0))
    return pl.pallas_call(
        body, name="bwd_in", grid=(s // ts,),
        in_specs=[tile(DIN), tile(D), tile(D), pl.BlockSpec((VD_ROWS, D), lambda i: (0, 0)),
                  pl.BlockSpec(memory_space=pl.ANY)] + exchange.specs_any,
        out_specs=[tile(D), pl.BlockSpec((8, D), lambda i: (0, 0))] + exchange.specs_any,
        out_shape=[jax.ShapeDtypeStruct((s, D), F32), jax.ShapeDtypeStruct((8, D), F32)] + exchange.out_shape,
        scratch_shapes=[pltpu.VMEM((D, DIN), BF16), pltpu.SemaphoreType.DMA((1,))] + exchange.scratch,
        compiler_params=_params(("arbitrary",)),
    )(dproj, dx1, x, vec_d, w_in, *ex_grads)


def _dot_tn(a, b):
    return lax.dot_general(a, b, (((0,), (0,)), ((), ())), preferred_element_type=F32)


def _wgrad(a, b, tm, tn, ts, name, dtype, exchange=None, ex_grads=()):
    s, m = a.shape
    nn = b.shape[1]
    grid = (m // tm, nn // tn, s // ts)
    n = exchange.n if exchange else 0

    def body(*refs):
        a_ref, b_ref = refs[:2]
        grads = refs[2:2 + n]
        o_ref = refs[2 + n]
        recvs = refs[3 + n:3 + 2 * n]
        acc = refs[3 + 2 * n]
        sems = refs[4 + 2 * n:]
        i, j, k = pl.program_id(0), pl.program_id(1), pl.program_id(2)
        if exchange:
            pl.when((i == 0) & (j == 0) & (k == 0))(lambda: exchange.start(grads, recvs, *sems))
        part = _dot_tn(a_ref[...], b_ref[...])

        @pl.when(k == 0)
        def _():
            acc[...] = part

        @pl.when(k > 0)
        def _():
            acc[...] += part

        @pl.when(k == grid[2] - 1)
        def _():
            o_ref[...] = acc[...].astype(dtype)

        if exchange:
            pl.when((i == grid[0] - 1) & (j == grid[1] - 1) & (k == grid[2] - 1))(
                lambda: exchange.finish(grads, recvs, *sems))

    hosted = exchange.specs_any if exchange else []
    return pl.pallas_call(
        body, name=name, grid=grid,
        in_specs=[pl.BlockSpec((ts, tm), lambda i, j, k: (k, i)), pl.BlockSpec((ts, tn), lambda i, j, k: (k, j))]
        + hosted,
        out_specs=[pl.BlockSpec((tm, tn), lambda i, j, k: (i, j))] + hosted,
        out_shape=[jax.ShapeDtypeStruct((m, nn), dtype)] + (exchange.out_shape if exchange else []),
        scratch_shapes=[pltpu.VMEM((tm, tn), F32)] + (exchange.scratch if exchange else []),
        compiler_params=_params(("arbitrary", "arbitrary", "arbitrary")),
    )(a, b, *ex_grads)


FFN_SLABS_FWD = ((0, 1024), (1024, 2816))
FFN_SLABS_BWD = ((0, 1536), (1536, 2816))
TS_PROJ = 512
TS_MIX = 256
TS_FFN = 256
TS_WGRAD = 2048


def _local_step(x, tgt, vec_d, vec_f, placed, place):
    s = x.shape[0]
    tw = min(TS_WGRAD, s)
    sp_in, sp_pool, sp_bout, sp_o, sp_up, sp_down = SHARDED
    proj, h1, w_in, w_pool, w_bout, w_o = _fwd_proj(x, vec_d, placed[0], placed[1:4], place, min(2 * TS_PROJ, s))
    x1, o, pg, q, merged, ya0, yb, cv, w_up, w_down = _fwd_mix(proj, x, vec_d, w_pool, w_bout, w_o, placed[4:],
                                                               min(TS_MIX, s))
    up0, upc, a, h2, dx2, dff, vo_f, loss = _fwd_ffn(x1, tgt, vec_d, vec_f, w_up, w_down, min(TS_FFN, s))
    g_down, = _wgrad(a, dff, F // 2, D, tw, "wgrad_down", BF16)
    dx1, dup0, vo_b, fo, r_down = _bwd_ffn(dff, dx2, x1, up0, upc, vec_d, vec_f, w_up, w_down,
                                           _GradExchange([sp_down]), [g_down], min(TS_FFN, s))
    g_up, = _wgrad(h2, dup0, D, F2 // NCHIP, tw, "wgrad_up", BF16)
    dproj, vo_m, g_o, g_bout, g_pool, r_up = _bwd_mix(dx1, o, proj, cv, ya0, yb, merged, q, pg, vec_d,
                                                      w_pool, w_bout, w_o, _GradExchange([sp_up]), [g_up],
                                                      min(TS_MIX, s))
    g_in, r_pool, r_bout, r_o = _wgrad(h1, dproj, D, DIN // NCHIP, tw, "wgrad_in", BF16,
                                       _GradExchange([sp_pool, sp_bout, sp_o]), [g_pool, g_bout, g_o])
    dx, vo_i, r_in = _bwd_in(dproj, dx1, x, vec_d, w_in, _GradExchange([sp_in]), [g_in], min(TS_PROJ, s))
    vecs = dict(
        dsh1=vo_i[0], dsc1=vo_i[1], dg_pre_mix=vo_i[2],
        dgt1=vo_m[0], dg_post_mix=vo_m[1], dpool_scale=vo_m[2], dconv_b=vo_m[3],
        dconv_w=vo_m[4:7],
        dsh2=vo_b[0], dsc2=vo_b[1], dg_pre_ffn=vo_b[2],
        dgt2=vo_f[0], dg_post_ffn=vo_f[1],
        dffn_conv_w=fo[FV_W0:FV_W2 + 1], dffn_conv_b=fo[FV_B],
    )
    local = dict(w_in=g_in, w_pool=g_pool, w_bout=g_bout, w_o=g_o, w_up=g_up, w_down=g_down)
    received = dict(w_in=r_in, w_pool=r_pool, w_bout=r_bout, w_o=r_o, w_up=r_up, w_down=r_down)
    return loss, dx, vecs, local, received


def _aligned(offset, n):
    return offset if isinstance(offset, int) else pl.multiple_of(offset, n)


class _Sharded:
    def __init__(self, name, full_shape, shard_axis, half_axis):
        self.name = name
        self.full_shape = full_shape
        self.shard_axis = shard_axis
        self.half_axis = half_axis
        self.shard_shape = tuple(n // NCHIP if a == shard_axis else n for a, n in enumerate(full_shape))
        self.piece_shape = tuple(n // 2 if a == half_axis else n for a, n in enumerate(self.shard_shape))

    def piece(self, full_ref, k, h):
        idx = []
        for a, n in enumerate(self.piece_shape):
            if a == self.shard_axis and a == self.half_axis:
                idx.append(pl.ds(_aligned((2 * k + h) * n, n), n))
            elif a == self.shard_axis:
                idx.append(pl.ds(_aligned(k * n, n), n))
            elif a == self.half_axis:
                idx.append(pl.ds(_aligned(h * n, n), n))
            else:
                idx.append(slice(None))
        return full_ref.at[tuple(idx)]

    def shard(self, full_ref, k):
        n = self.shard_shape[self.shard_axis]
        idx = [pl.ds(_aligned(k * n, n), n) if a == self.shard_axis else slice(None)
               for a in range(len(self.full_shape))]
        return full_ref.at[tuple(idx)]

    def half(self, shard_ref, h):
        n = self.piece_shape[self.half_axis]
        idx = [pl.ds(_aligned(h * n, n), n) if a == self.half_axis else slice(None)
               for a in range(len(self.full_shape))]
        return shard_ref.at[tuple(idx)]

SHARDED = (
    _Sharded("w_in", (D, DIN), 1, 0),
    _Sharded("w_pool", (NG, GW, GW), 1, 0),
    _Sharded("w_bout", (D, D), 0, 0),
    _Sharded("w_o", (D, D), 0, 0),
    _Sharded("w_up", (D, F2), 1, 0),
    _Sharded("w_down", (F, D), 0, 0),
)
NW = len(SHARDED)


def _mesh_place():
    x, y, c = lax.axis_index("x"), lax.axis_index("y"), lax.axis_index("c")
    chips = [(1 - x, y), (x, 1 - y), (1 - x, 1 - y)]
    return x, y, c, 2 * x + y, chips, [2 * px + py for px, py in chips]


def _remote(src, dst, send_sem, recv_sem, device):
    return pltpu.make_async_remote_copy(src_ref=src, dst_ref=dst, send_sem=send_sem, recv_sem=recv_sem,
                                        device_id=device, device_id_type=MESH)


SMALL_GATHER_SCRATCH = [pltpu.SemaphoreType.DMA((7,)), pltpu.SemaphoreType.DMA((7,)), pltpu.SemaphoreType.DMA]


def _small_gather(x_ref, out_ref, send_sems, recv_sems, local_sem):
    m_per = x_ref.shape[0]
    x, y, c, _, chips, _ = _mesh_place()
    me, sibling = (x, y, c), (x, y, 1 - c)

    def rows(px, py, pc):
        return out_ref.at[pl.ds((4 * px + 2 * py + pc) * m_per, m_per), :]

    def copy(k, blk, to, src=None):
        return _remote(rows(*blk) if src is None else src, rows(*blk), send_sems.at[k], recv_sems.at[k], to)

    mine = pltpu.make_async_copy(x_ref, rows(*me), local_sem)
    mine.start()
    first = [copy(0, me, sibling, src=x_ref)]
    first += [copy(1 + j, me, (*chip, c), src=x_ref) for j, chip in enumerate(chips)]
    for cp in first:
        cp.start()
    passed = [copy(4 + j, (*chip, c), sibling) for j, chip in enumerate(chips)]
    for j, chip in enumerate(chips):
        copy(1 + j, (*chip, c), me).wait_recv()
        passed[j].start()
    copy(0, sibling, me).wait_recv()
    for j, chip in enumerate(chips):
        copy(4 + j, (*chip, 1 - c), me).wait_recv()
    for cp in first + passed:
        cp.wait_send()
    mine.wait()


def _all_gather_small(block, name):
    m_per, n = block.shape
    return pl.pallas_call(
        _small_gather_body(), name=name,
        out_shape=jax.ShapeDtypeStruct((NDEV * m_per, n), block.dtype),
        in_specs=[pl.BlockSpec(memory_space=pltpu.VMEM)],
        out_specs=pl.BlockSpec(memory_space=pltpu.VMEM),
        scratch_shapes=SMALL_GATHER_SCRATCH,
        compiler_params=pltpu.CompilerParams(vmem_limit_bytes=VMEM_LIMIT),
    )(block)


def _small_gather_body():
    def body(x_ref, out_ref, send_sems, recv_sems, local_sem):
        _small_gather(x_ref, out_ref, send_sems, recv_sems, local_sem)
    return body


class _WeightGather:
    def __init__(self, specs):
        self.specs = specs
        self.n = len(specs)
        self.specs_any = [pl.BlockSpec(memory_space=pl.ANY)] * self.n
        self.out_shape = [jax.ShapeDtypeStruct(sp.full_shape, BF16) for sp in specs]
        self.scratch = [pltpu.SemaphoreType.DMA((6 * self.n,)), pltpu.SemaphoreType.DMA((6 * self.n,))]

    def _sends(self, outs, send_sems, recv_sems):
        x, y, c, k_me, chips, _ = _mesh_place()
        sends = []
        for j, chip in enumerate(chips):
            for w, sp in enumerate(self.specs):
                mine = sp.piece(outs[w], k_me, c)
                sends.append(_remote(mine, mine, send_sems.at[6 * w + j], recv_sems.at[6 * w + j], (*chip, c)))
        return sends

    def start(self, outs, send_sems, recv_sems):
        for cp in self._sends(outs, send_sems, recv_sems):
            cp.start()

    def _passes(self, outs, send_sems, recv_sems):
        x, y, c, _, chips, kidx = _mesh_place()
        return [_remote(sp.piece(outs[w], kidx[j], c), sp.piece(outs[w], kidx[j], c),
                        send_sems.at[6 * w + 3 + j], recv_sems.at[6 * w + 3 + j], (x, y, 1 - c))
                for j in range(3) for w, sp in enumerate(self.specs)]

    def forward(self, outs, send_sems, recv_sems):
        x, y, c, _, chips, kidx = _mesh_place()
        for j, chip in enumerate(chips):
            for w, sp in enumerate(self.specs):
                landed = sp.piece(outs[w], kidx[j], c)
                _remote(landed, landed, send_sems.at[6 * w + j], recv_sems.at[6 * w + j], (*chip, c)).wait_recv()
        for cp in self._passes(outs, send_sems, recv_sems):
            cp.start()

    def drain(self, outs, send_sems, recv_sems):
        x, y, c, _, chips, kidx = _mesh_place()
        for j in range(3):
            for w, sp in enumerate(self.specs):
                landed = sp.piece(outs[w], kidx[j], 1 - c)
                _remote(landed, landed, send_sems.at[6 * w + 3 + j], recv_sems.at[6 * w + 3 + j],
                        (x, y, 1 - c)).wait_recv()
        for cp in self._sends(outs, send_sems, recv_sems) + self._passes(outs, send_sems, recv_sems):
            cp.wait_send()

    def finish(self, outs, send_sems, recv_sems):
        self.forward(outs, send_sems, recv_sems)
        self.drain(outs, send_sems, recv_sems)


class _GradExchange:
    def __init__(self, specs):
        self.specs = specs
        self.n = len(specs)
        self.specs_any = [pl.BlockSpec(memory_space=pl.ANY)] * self.n
        self.out_shape = [jax.ShapeDtypeStruct((NDEV,) + sp.piece_shape, BF16) for sp in specs]
        self.scratch = [pltpu.SemaphoreType.DMA((7 * self.n,)), pltpu.SemaphoreType.DMA((NDEV * self.n,))]

    def _sends(self, grads, recvs, send_sems, recv_sems):
        x, y, c, k_me, chips, kidx = _mesh_place()
        dev = 2 * k_me + c
        sends = []
        for w, sp in enumerate(self.specs):
            slot, arrival = recvs[w].at[dev], recv_sems.at[NDEV * w + dev]
            sends.append(_remote(sp.piece(grads[w], k_me, 1 - c), slot, send_sems.at[7 * w], arrival, (x, y, 1 - c)))
            for j, chip in enumerate(chips):
                for h in range(2):
                    sends.append(_remote(sp.piece(grads[w], kidx[j], h), slot, send_sems.at[7 * w + 1 + 2 * j + h],
                                         arrival, (*chip, h)))
        return sends

    def start(self, grads, recvs, send_sems, recv_sems):
        for cp in self._sends(grads, recvs, send_sems, recv_sems):
            cp.start()

    def finish(self, grads, recvs, send_sems, recv_sems):
        x, y, c, k_me, _, _ = _mesh_place()
        dev = 2 * k_me + c
        for w in range(self.n):
            for d in range(NDEV):
                landed = recvs[w].at[d]
                arrival = _remote(landed, landed, send_sems.at[7 * w], recv_sems.at[NDEV * w + d], (x, y, c))
                pl.when(d != dev)(arrival.wait_recv)
        for cp in self._sends(grads, recvs, send_sems, recv_sems):
            cp.wait_send()


def _device_sums(locals_, recvs, place):
    def body(p_ref, *refs):
        a_refs, b_refs, o_refs = refs[:NW], refs[NW:2 * NW], refs[2 * NW:]
        d = pl.program_id(0)
        own = d == p_ref[2]
        terms = [jnp.where(own, a_ref[...], b_ref[...]).astype(F32) for a_ref, b_ref in zip(a_refs, b_refs)]

        @pl.when(d == 0)
        def _():
            for o_ref, term in zip(o_refs, terms):
                o_ref[...] = term

        @pl.when(d > 0)
        def _():
            for o_ref, term in zip(o_refs, terms):
                o_ref[...] += term

    def mine(sp):
        nd = len(sp.piece_shape)
        return pl.BlockSpec(sp.piece_shape, lambda d, p_ref: tuple(
            2 * p_ref[0] + p_ref[1] if a == sp.shard_axis == sp.half_axis else
            p_ref[0] if a == sp.shard_axis else p_ref[1] if a == sp.half_axis else 0 for a in range(nd)))

    def others(sp):
        nd = len(sp.piece_shape)
        return pl.BlockSpec((None,) + sp.piece_shape,
                            lambda d, p_ref: (jnp.where(d == p_ref[2], (d + 1) % NDEV, d),) + (0,) * nd)

    def half(sp):
        nd = len(sp.piece_shape)
        return pl.BlockSpec(sp.piece_shape,
                            lambda d, p_ref: tuple(p_ref[1] if a == sp.half_axis else 0 for a in range(nd)))

    return pl.pallas_call(
        body, name="rs_device_sums",
        grid_spec=pltpu.PrefetchScalarGridSpec(
            num_scalar_prefetch=1, grid=(NDEV,),
            in_specs=[mine(sp) for sp in SHARDED] + [others(sp) for sp in SHARDED],
            out_specs=[half(sp) for sp in SHARDED]),
        out_shape=[jax.ShapeDtypeStruct(sp.shard_shape, F32) for sp in SHARDED],
        compiler_params=_params(("arbitrary",)),
    )(place, *locals_, *recvs)


def _pair_share(halves, vector_block):
    m_per, n = vector_block.shape

    def body(*refs):
        x_ref = refs[NW]
        outs, gathered = refs[NW + 1:2 * NW + 1], refs[2 * NW + 1]
        send_sems, recv_sems = refs[2 * NW + 2:2 * NW + 4]
        x, y, c, _, _, _ = _mesh_place()
        sibling = (x, y, 1 - c)
        sent = []
        for w, sp in enumerate(SHARDED):
            mine = sp.half(outs[w], c)
            cp = _remote(mine, mine, send_sems.at[w], recv_sems.at[w], sibling)
            cp.start()
            sent.append(cp)
        _small_gather(x_ref, gathered, *refs[2 * NW + 4:])
        for w, sp in enumerate(SHARDED):
            landed = sp.half(outs[w], 1 - c)
            _remote(landed, landed, send_sems.at[w], recv_sems.at[w], sibling).wait_recv()
        for cp in sent:
            cp.wait_send()

    hbm = pl.BlockSpec(memory_space=pl.ANY)
    vmem = pl.BlockSpec(memory_space=pltpu.VMEM)
    out = pl.pallas_call(
        body, name="rs_pair_share",
        out_shape=[jax.ShapeDtypeStruct(sp.shard_shape, F32) for sp in SHARDED]
        + [jax.ShapeDtypeStruct((NDEV * m_per, n), F32)],
        in_specs=[hbm] * NW + [vmem], out_specs=[hbm] * NW + [vmem],
        input_output_aliases={w: w for w in range(NW)},
        scratch_shapes=[pltpu.SemaphoreType.DMA((NW,)), pltpu.SemaphoreType.DMA((NW,))] + SMALL_GATHER_SCRATCH,
        compiler_params=pltpu.CompilerParams(vmem_limit_bytes=VMEM_LIMIT),
    )(*halves, vector_block)
    return out[:NW], out[NW]


def _reduce_scatter(local, received, place, vector_block):
    halves = _device_sums([local[sp.name] for sp in SHARDED], [received[sp.name] for sp in SHARDED], place)
    return _pair_share(halves, vector_block)


def _place_bf16(sp, w, place):
    nd = len(sp.full_shape)

    def body(p_ref, w_ref, o_ref):
        o_ref[...] = w_ref[...].astype(BF16)

    return pl.pallas_call(
        body, name="place_" + sp.name,
        grid_spec=pltpu.PrefetchScalarGridSpec(
            num_scalar_prefetch=1, grid=(1,),
            in_specs=[pl.BlockSpec(sp.shard_shape, lambda i, p_ref: (0,) * nd)],
            out_specs=pl.BlockSpec(sp.shard_shape,
                                   lambda i, p_ref: tuple(p_ref[0] if a == sp.shard_axis else 0 for a in range(nd)))),
        out_shape=jax.ShapeDtypeStruct(sp.full_shape, BF16),
        compiler_params=_params(("arbitrary",)),
    )(place, w)


def _matmul_f32(a, b, name):
    def body(a_ref, b_ref, o_ref):
        o_ref[...] = jnp.dot(a_ref[...], b_ref[...], preferred_element_type=F32, precision=lax.Precision.HIGHEST)

    return pl.pallas_call(body, name=name, out_shape=jax.ShapeDtypeStruct((a.shape[0], b.shape[1]), F32),
                          compiler_params=pltpu.CompilerParams(vmem_limit_bytes=VMEM_LIMIT))(a, b)


def _sum_devices(stacked):
    def body(x_ref, o_ref):
        acc = x_ref[0]
        for d in range(1, NDEV):
            acc = acc + x_ref[d]
        o_ref[...] = acc

    return pl.pallas_call(body, name="sum_devices", out_shape=jax.ShapeDtypeStruct(stacked.shape[1:], F32),
                          compiler_params=pltpu.CompilerParams(vmem_limit_bytes=VMEM_LIMIT))(stacked)


ADAMW_STEPS = 8


def _adamw(ws, gs, ms, vs, name):
    n = len(ws)
    steps = ADAMW_STEPS if all(w.shape[0] % (8 * ADAMW_STEPS) == 0 for w in ws) else 1

    def body(*refs):
        ins, outs = refs[:4 * n], refs[4 * n:]
        for k in range(n):
            w_ref, g_ref, m_ref, v_ref = ins[k], ins[n + k], ins[2 * n + k], ins[3 * n + k]
            gv = g_ref[...]
            nm = ADAM_B1 * m_ref[...] + (1.0 - ADAM_B1) * gv
            nv = ADAM_B2 * v_ref[...] + (1.0 - ADAM_B2) * (gv * gv)
            m_hat = nm / (1.0 - ADAM_B1 ** ADAM_STEP)
            v_hat = nv / (1.0 - ADAM_B2 ** ADAM_STEP)
            outs[k][...] = -ADAM_LR * (m_hat / (jnp.sqrt(v_hat) + ADAM_EPS) + ADAM_WD * w_ref[...])
            outs[n + k][...] = nm
            outs[2 * n + k][...] = nv

    blks = [pl.BlockSpec((w.shape[0] // steps, w.shape[1]), lambda i: (i, 0)) for w in ws]
    shapes = [jax.ShapeDtypeStruct(w.shape, F32) for w in ws]
    out = pl.pallas_call(
        body, name="adamw_" + name, grid=(steps,), in_specs=blks * 4, out_specs=blks * 3, out_shape=shapes * 3,
        compiler_params=_params(("parallel",)),
    )(*ws, *gs, *ms, *vs)
    return out[:n], out[n:2 * n], out[2 * n:]


WEIGHT_NAMES = ("g_pre_mix", "g_post_mix", "g_pre_ffn", "g_post_ffn", "w_ada", "b_ada", "w_in", "w_pool",
                "pool_scale", "conv_w", "conv_b", "w_bout", "w_o", "w_up", "ffn_conv_w", "ffn_conv_b", "w_down")
MATRIX_NAMES = ("w_ada",) + tuple(sp.name for sp in SHARDED)
VECTOR_NAMES = tuple(n for n in WEIGHT_NAMES if n not in MATRIX_NAMES)

CW = D // NCHIP
FCW = F2 // NCHIP
ADA_W = DIN // NCHIP
COND_BLOCK = (8, 768)
GRAD_BLOCK = (8, 4864)


def _flat_pad(parts, shape):
    flat = jnp.concatenate([p.reshape(-1) for p in parts])
    return jnp.pad(flat, (0, shape[0] * shape[1] - flat.shape[0])).reshape(shape)


def _take(flat, offset, shape):
    size = 1
    for n in shape:
        size *= n
    return flat[offset:offset + size].reshape(shape), offset + size


def kernel(x, c, g_pre_mix, g_post_mix, g_pre_ffn, g_post_ffn, w_ada, b_ada, w_in, w_pool, pool_scale, conv_w, conv_b, w_bout, w_o, w_up, ffn_conv_w, ffn_conv_b, w_down, loss_target, m_g_pre_mix, m_g_post_mix, m_g_pre_ffn, m_g_post_ffn, m_w_ada, m_b_ada, m_w_in, m_w_pool, m_pool_scale, m_conv_w, m_conv_b, m_w_bout, m_w_o, m_w_up, m_ffn_conv_w, m_ffn_conv_b, m_w_down, v_g_pre_mix, v_g_post_mix, v_g_pre_ffn, v_g_post_ffn, v_w_ada, v_b_ada, v_w_in, v_w_pool, v_pool_scale, v_conv_w, v_conv_b, v_w_bout, v_w_o, v_w_up, v_ffn_conv_w, v_ffn_conv_b, v_w_down):
    weights = dict(g_pre_mix=g_pre_mix, g_post_mix=g_post_mix, g_pre_ffn=g_pre_ffn, g_post_ffn=g_post_ffn,
                   w_ada=w_ada, b_ada=b_ada, w_in=w_in, w_pool=w_pool, pool_scale=pool_scale, conv_w=conv_w,
                   conv_b=conv_b, w_bout=w_bout, w_o=w_o, w_up=w_up, ffn_conv_w=ffn_conv_w, ffn_conv_b=ffn_conv_b,
                   w_down=w_down)
    mom1 = dict(g_pre_mix=m_g_pre_mix, g_post_mix=m_g_post_mix, g_pre_ffn=m_g_pre_ffn, g_post_ffn=m_g_post_ffn,
                w_ada=m_w_ada, b_ada=m_b_ada, w_in=m_w_in, w_pool=m_w_pool, pool_scale=m_pool_scale,
                conv_w=m_conv_w, conv_b=m_conv_b, w_bout=m_w_bout, w_o=m_w_o, w_up=m_w_up,
                ffn_conv_w=m_ffn_conv_w, ffn_conv_b=m_ffn_conv_b, w_down=m_w_down)
    mom2 = dict(g_pre_mix=v_g_pre_mix, g_post_mix=v_g_post_mix, g_pre_ffn=v_g_pre_ffn, g_post_ffn=v_g_post_ffn,
                w_ada=v_w_ada, b_ada=v_b_ada, w_in=v_w_in, w_pool=v_w_pool, pool_scale=v_pool_scale,
                conv_w=v_conv_w, conv_b=v_conv_b, w_bout=v_w_bout, w_o=v_w_o, w_up=v_w_up,
                ffn_conv_w=v_ffn_conv_w, ffn_conv_b=v_ffn_conv_b, w_down=v_w_down)

    chip = 2 * lax.axis_index("x") + lax.axis_index("y")
    core = lax.axis_index("c")
    dev = 2 * chip + core
    place = jnp.stack([chip, core, dev]).astype(jnp.int32)

    cond = _all_gather_small(_flat_pad([c, conv_w, ffn_conv_w], COND_BLOCK), "gather_cond")
    cond = cond.reshape(NDEV, -1)
    c_all = cond[:, :D]
    by_chip = cond[0::2]
    conv_w_full = by_chip[:, D:D + 3 * CW].reshape(NCHIP, 3, CW).transpose(1, 0, 2).reshape(3, D)
    ffn_w_full = by_chip[:, D + 3 * CW:D + 3 * CW + 3 * FCW].reshape(NCHIP, 3, FCW).transpose(1, 0, 2).reshape(3, F2)

    mod_cols = _all_gather_small(_matmul_f32(c_all, w_ada[0], "ada_mod"), "gather_mod")
    mod_cols = mod_cols.reshape(NDEV, NDEV, ADA_W)[0::2]
    mod = lax.dynamic_index_in_dim(mod_cols, dev, axis=1, keepdims=False).reshape(6, D) + b_ada.reshape(6, D)
    vec_d = jnp.concatenate([mod, g_pre_mix, g_post_mix, g_pre_ffn, g_post_ffn, pool_scale, conv_b, conv_w_full,
                             jnp.zeros((VD_ROWS - 15, D), F32)], axis=0)
    vec_f = jnp.concatenate([ffn_w_full, ffn_conv_b, jnp.zeros((FV_ROWS - 4, F2), F32)], axis=0)

    placed = [_place_bf16(sp, weights[sp.name][0], place) for sp in SHARDED]
    loss_blk, dx, vecs, local, received = _local_step(x[0], loss_target[0], vec_d, vec_f, placed, place)

    dmod = [vecs[n] for n in ("dsh1", "dsc1", "dgt1", "dsh2", "dsc2", "dgt2")]
    small = [vecs["dg_pre_mix"], vecs["dg_post_mix"], vecs["dg_pre_ffn"], vecs["dg_post_ffn"]] + dmod + [
        vecs["dpool_scale"], vecs["dconv_w"], vecs["dconv_b"], vecs["dffn_conv_w"], vecs["dffn_conv_b"],
        loss_blk[0]]
    reduced, gathered = _reduce_scatter(local, received, place, _flat_pad(small, GRAD_BLOCK))
    total = _sum_devices(gathered.reshape((NDEV,) + GRAD_BLOCK)).reshape(-1)
    vgrad = {}
    off = 0
    for n in ("g_pre_mix", "g_post_mix", "g_pre_ffn", "g_post_ffn"):
        vgrad[n], off = _take(total, off, (1, D))
    dmod_off = off
    vgrad["b_ada"], off = _take(total, off, (1, DIN))
    vgrad["pool_scale"], off = _take(total, off, (1, D))
    g_conv_w, off = _take(total, off, (3, D))
    vgrad["conv_w"] = lax.dynamic_slice_in_dim(g_conv_w, chip * CW, CW, axis=1)[None]
    vgrad["conv_b"], off = _take(total, off, (1, D))
    g_ffn_w, off = _take(total, off, (3, F2))
    vgrad["ffn_conv_w"] = lax.dynamic_slice_in_dim(g_ffn_w, chip * FCW, FCW, axis=1)[None]
    vgrad["ffn_conv_b"], off = _take(total, off, (1, F2))
    loss = total[off]

    dmod_all = gathered.reshape(NDEV, -1)[:, dmod_off:dmod_off + DIN]
    dmod_cols = lax.dynamic_slice_in_dim(dmod_all, chip * ADA_W, ADA_W, axis=1)
    g_ada = _matmul_f32(jnp.pad(c_all.T, ((0, 0), (0, 128 - NDEV))), jnp.pad(dmod_cols, ((0, 128 - NDEV), (0, 0))),
                        "ada_wgrad")

    mgrad = {"w_ada": g_ada}
    for sp, g in zip(SHARDED, reduced):
        mgrad[sp.name] = g

    grad, delta, new_m, new_v = {}, {}, {}, {}
    two_d = lambda tree: [tree[n].reshape(-1, weights[n].shape[-1]) for n in MATRIX_NAMES]
    ds, nms, nvs = _adamw(two_d(weights), two_d(mgrad), two_d(mom1), two_d(mom2), "matrices")
    for n, d, nm, nv in zip(MATRIX_NAMES, ds, nms, nvs):
        shape = weights[n].shape
        grad[n], delta[n], new_m[n], new_v[n] = (a.reshape(shape) for a in (mgrad[n], d, nm, nv))
    flat = lambda tree: [jnp.concatenate([tree[n].reshape(1, -1) for n in VECTOR_NAMES], axis=1)]
    (d,), (nm,), (nv,) = _adamw(flat(weights), flat(vgrad), flat(mom1), flat(mom2), "vectors")
    off = 0
    for n in VECTOR_NAMES:
        shape = weights[n].shape
        grad[n] = vgrad[n].reshape(shape)
        delta[n], _ = _take(d[0], off, shape)
        new_m[n], _ = _take(nm[0], off, shape)
        new_v[n], off = _take(nv[0], off, shape)

    return (loss, dx[None], *[grad[n] for n in WEIGHT_NAMES], *[delta[n] for n in WEIGHT_NAMES],
            *[new_m[n] for n in WEIGHT_NAMES], *[new_v[n] for n in WEIGHT_NAMES])
```
